```python
import jax, jax.numpy as jnp
from jax import lax
import numpy as np

D_MODEL = 1024
BATCH = 8
SEQ = 4096
DEPTH = 2

CHUNK = 64
GLA_HEADS = 4
KEY_WIDTH = D_MODEL // 2
VAL_WIDTH = D_MODEL
HEAD_K = KEY_WIDTH // GLA_HEADS
HEAD_V = VAL_WIDTH // GLA_HEADS
GATE_RANK = 16
GATE_TAU = 16.0
CONV_CH = D_MODEL
CONV_WIDTH = 3
FFN_HIDDEN = -(-8 * D_MODEL // (3 * 256)) * 256
IN_WIDTH = 2 * KEY_WIDTH + 2 * VAL_WIDTH + GATE_RANK + 3 * CONV_CH + 2 * D_MODEL
NORM_EPS = 1e-6

kernel_name = "gla_shortconv_gated_hybrid"


def _split_points():
    sizes = (KEY_WIDTH, KEY_WIDTH, VAL_WIDTH, VAL_WIDTH, GATE_RANK,
             CONV_CH, CONV_CH, CONV_CH, D_MODEL, D_MODEL)
    return tuple(int(v) for v in np.cumsum(sizes)[:-1])


def rmsnorm(x, g):
    xf = x.astype(jnp.float32)
    y = xf * lax.rsqrt(jnp.mean(xf * xf, axis=-1, keepdims=True) + NORM_EPS)
    return (y * g.astype(jnp.float32)).astype(x.dtype)


def gla_chunk_causal(q, k, v, log_a):
    b_sz, s_len = q.shape[0], q.shape[1]
    n_chunks = s_len // CHUNK

    def to_chunks(t):
        return t.astype(jnp.float32).reshape(b_sz, n_chunks, CHUNK, GLA_HEADS, t.shape[-1]).transpose(1, 0, 3, 2, 4)

    qc, kc, vc, lc = to_chunks(q), to_chunks(k), to_chunks(v), to_chunks(log_a)
    cum = jnp.cumsum(lc, axis=3)
    cum_end = cum[:, :, :, -1:, :]
    kd = kc * jnp.exp(cum_end - cum)
    gamma = jnp.exp(cum_end[:, :, :, 0, :])

    def step(state, inp):
        q_i, kd_i, v_i, g_i = inp
        state = g_i[..., None] * state + jnp.einsum('bhlk,bhlv->bhkv', kd_i, v_i)
        o_i = jnp.einsum('bhlk,bhkv->bhlv', q_i, state)
        return state, o_i

    s0 = jnp.zeros((b_sz, GLA_HEADS, HEAD_K, HEAD_V), jnp.float32)
    _, o = lax.scan(step, s0, (qc, kd, vc, gamma))
    return o.transpose(1, 0, 3, 2, 4).reshape(b_sz, s_len, GLA_HEADS, HEAD_V)


def causal_depthwise_conv(u, w, bias):
    rhs = w.astype(u.dtype)[:, None, :]
    y = lax.conv_general_dilated(u, rhs, window_strides=(1,), padding=[(CONV_WIDTH - 1, 0)],
                                 dimension_numbers=('NWC', 'WIO', 'NWC'),
                                 feature_group_count=u.shape[-1])
    return y + bias.astype(u.dtype)


def _fwd_setup_inputs(seed: int = 0) -> dict:
    key = jax.random.key(seed)
    ks = jax.random.split(key, 20)

    def nrm(k, shape, scale):
        return jax.random.normal(k, shape, jnp.float32) * scale

    def gain(k, shape):
        return 1.0 + 0.02 * jax.random.normal(k, shape, jnp.float32)

    return {
        "x": nrm(ks[0], (BATCH, SEQ, D_MODEL), 1.0),
        "norm1_g": gain(ks[1], (DEPTH, D_MODEL)),
        "w_in": nrm(ks[2], (DEPTH, D_MODEL, IN_WIDTH), D_MODEL ** -0.5),
        "w_fg2": nrm(ks[3], (DEPTH, GATE_RANK, KEY_WIDTH), GATE_RANK ** -0.5),
        "b_fg": nrm(ks[4], (DEPTH, KEY_WIDTH), 0.01),
        "gla_norm_g": gain(ks[5], (DEPTH, HEAD_V)),
        "w_oa": nrm(ks[6], (DEPTH, VAL_WIDTH, D_MODEL), VAL_WIDTH ** -0.5),
        "conv_w": nrm(ks[7], (DEPTH, CONV_WIDTH, CONV_CH), CONV_WIDTH ** -0.5),
        "conv_b": nrm(ks[8], (DEPTH, CONV_CH), 0.01),
        "w_ob": nrm(ks[9], (DEPTH, CONV_CH, D_MODEL), CONV_CH ** -0.5),
        "w_o": nrm(ks[10], (DEPTH, D_MODEL, D_MODEL), D_MODEL ** -0.5),
        "norm2_g": gain(ks[11], (DEPTH, D_MODEL)),
        "w_ffn_gate": nrm(ks[12], (DEPTH, D_MODEL, FFN_HIDDEN), D_MODEL ** -0.5),
        "w_ffn_up": nrm(ks[13], (DEPTH, D_MODEL, FFN_HIDDEN), D_MODEL ** -0.5),
        "w_ffn_down": nrm(ks[14], (DEPTH, FFN_HIDDEN, D_MODEL), FFN_HIDDEN ** -0.5),
        "final_g": gain(ks[15], (D_MODEL,)),
    }


def _fwd_reference(x, norm1_g, w_in, w_fg2, b_fg, gla_norm_g, w_oa, conv_w, conv_b, w_ob, w_o,
              norm2_g, w_ffn_gate, w_ffn_up, w_ffn_down, final_g):
    b_sz, s_len, _ = x.shape
    split_pts = _split_points()
    for l in range(DEPTH):
        h = rmsnorm(x, norm1_g[l])
        proj = jnp.einsum('bsd,de->bse', h, w_in[l])
        q, k, v, r, fz, gb_in, gc_in, cx, ga, gb = jnp.split(proj, split_pts, axis=-1)

        fg = jnp.einsum('bsr,rk->bsk', fz, w_fg2[l]) + b_fg[l]
        log_a = jax.nn.log_sigmoid(fg.astype(jnp.float32)) / GATE_TAU
        qh = q.reshape(b_sz, s_len, GLA_HEADS, HEAD_K) * (HEAD_K ** -0.5)
        kh = k.reshape(b_sz, s_len, GLA_HEADS, HEAD_K)
        vh = v.reshape(b_sz, s_len, GLA_HEADS, HEAD_V)
        ah = log_a.reshape(b_sz, s_len, GLA_HEADS, HEAD_K)
        o = gla_chunk_causal(qh, kh, vh, ah)
        o = o * lax.rsqrt(jnp.mean(o * o, axis=-1, keepdims=True) + NORM_EPS) * gla_norm_g[l].astype(jnp.float32)
        o = o.reshape(b_sz, s_len, VAL_WIDTH).astype(x.dtype) * jax.nn.silu(r)
        y_a = jnp.einsum('bsv,vd->bsd', o, w_oa[l])

        conv = causal_depthwise_conv(gc_in * cx, conv_w[l], conv_b[l])
        y_b = jnp.einsum('bsc,cd->bsd', gb_in * conv, w_ob[l])

        mix = jax.nn.sigmoid(ga) * y_a + jax.nn.sigmoid(gb) * y_b
        x = x + jnp.einsum('bsd,de->bse', mix, w_o[l])

        h2 = rmsnorm(x, norm2_g[l])
        hid = jax.nn.silu(jnp.einsum('bsd,df->bsf', h2, w_ffn_gate[l])) * jnp.einsum('bsd,df->bsf', h2, w_ffn_up[l])
        x = x + jnp.einsum('bsf,fd->bsd', hid, w_ffn_down[l])
    return rmsnorm(x, final_g)


import jax as _jax
import jax.numpy as _jnp

TWIN_FORMAT = 'train_step'
FWD_PARAMS = ['x', 'norm1_g', 'w_in', 'w_fg2', 'b_fg', 'gla_norm_g', 'w_oa', 'conv_w', 'conv_b', 'w_ob', 'w_o', 'norm2_g', 'w_ffn_gate', 'w_ffn_up', 'w_ffn_down', 'final_g']
TWIN_WEIGHTS = ['norm1_g', 'w_in', 'w_fg2', 'b_fg', 'gla_norm_g', 'w_oa', 'conv_w', 'conv_b', 'w_ob', 'w_o', 'norm2_g', 'w_ffn_gate', 'w_ffn_up', 'w_ffn_down', 'final_g']
TWIN_DIFF_INPUT = 'x'
TWIN_INPUTS = ['x', 'norm1_g', 'w_in', 'w_fg2', 'b_fg', 'gla_norm_g', 'w_oa', 'conv_w', 'conv_b', 'w_ob', 'w_o', 'norm2_g', 'w_ffn_gate', 'w_ffn_up', 'w_ffn_down', 'final_g', 'loss_target', 'm_norm1_g', 'm_w_in', 'm_w_fg2', 'm_b_fg', 'm_gla_norm_g', 'm_w_oa', 'm_conv_w', 'm_conv_b', 'm_w_ob', 'm_w_o', 'm_norm2_g', 'm_w_ffn_gate', 'm_w_ffn_up', 'm_w_ffn_down', 'm_final_g', 'v_norm1_g', 'v_w_in', 'v_w_fg2', 'v_b_fg', 'v_gla_norm_g', 'v_w_oa', 'v_conv_w', 'v_conv_b', 'v_w_ob', 'v_w_o', 'v_norm2_g', 'v_w_ffn_gate', 'v_w_ffn_up', 'v_w_ffn_down', 'v_final_g']
TWIN_OUTPUTS = ['loss', 'grad_x', 'grad_norm1_g', 'grad_w_in', 'grad_w_fg2', 'grad_b_fg', 'grad_gla_norm_g', 'grad_w_oa', 'grad_conv_w', 'grad_conv_b', 'grad_w_ob', 'grad_w_o', 'grad_norm2_g', 'grad_w_ffn_gate', 'grad_w_ffn_up', 'grad_w_ffn_down', 'grad_final_g', 'delta_norm1_g', 'delta_w_in', 'delta_w_fg2', 'delta_b_fg', 'delta_gla_norm_g', 'delta_w_oa', 'delta_conv_w', 'delta_conv_b', 'delta_w_ob', 'delta_w_o', 'delta_norm2_g', 'delta_w_ffn_gate', 'delta_w_ffn_up', 'delta_w_ffn_down', 'delta_final_g', 'new_m_norm1_g', 'new_m_w_in', 'new_m_w_fg2', 'new_m_b_fg', 'new_m_gla_norm_g', 'new_m_w_oa', 'new_m_conv_w', 'new_m_conv_b', 'new_m_w_ob', 'new_m_w_o', 'new_m_norm2_g', 'new_m_w_ffn_gate', 'new_m_w_ffn_up', 'new_m_w_ffn_down', 'new_m_final_g', 'new_v_norm1_g', 'new_v_w_in', 'new_v_w_fg2', 'new_v_b_fg', 'new_v_gla_norm_g', 'new_v_w_oa', 'new_v_conv_w', 'new_v_conv_b', 'new_v_w_ob', 'new_v_w_o', 'new_v_norm2_g', 'new_v_w_ffn_gate', 'new_v_w_ffn_up', 'new_v_w_ffn_down', 'new_v_final_g']
TWIN_LEAF_KINDS = {'loss': 'loss', 'grad_x': 'grad_x', 'grad_norm1_g': 'grad_w', 'grad_w_in': 'grad_w', 'grad_w_fg2': 'grad_w', 'grad_b_fg': 'grad_w', 'grad_gla_norm_g': 'grad_w', 'grad_w_oa': 'grad_w', 'grad_conv_w': 'grad_w', 'grad_conv_b': 'grad_w', 'grad_w_ob': 'grad_w', 'grad_w_o': 'grad_w', 'grad_norm2_g': 'grad_w', 'grad_w_ffn_gate': 'grad_w', 'grad_w_ffn_up': 'grad_w', 'grad_w_ffn_down': 'grad_w', 'grad_final_g': 'grad_w', 'delta_norm1_g': 'delta_w', 'delta_w_in': 'delta_w', 'delta_w_fg2': 'delta_w', 'delta_b_fg': 'delta_w', 'delta_gla_norm_g': 'delta_w', 'delta_w_oa': 'delta_w', 'delta_conv_w': 'delta_w', 'delta_conv_b': 'delta_w', 'delta_w_ob': 'delta_w', 'delta_w_o': 'delta_w', 'delta_norm2_g': 'delta_w', 'delta_w_ffn_gate': 'delta_w', 'delta_w_ffn_up': 'delta_w', 'delta_w_ffn_down': 'delta_w', 'delta_final_g': 'delta_w', 'new_m_norm1_g': 'new_m', 'new_m_w_in': 'new_m', 'new_m_w_fg2': 'new_m', 'new_m_b_fg': 'new_m', 'new_m_gla_norm_g': 'new_m', 'new_m_w_oa': 'new_m', 'new_m_conv_w': 'new_m', 'new_m_conv_b': 'new_m', 'new_m_w_ob': 'new_m', 'new_m_w_o': 'new_m', 'new_m_norm2_g': 'new_m', 'new_m_w_ffn_gate': 'new_m', 'new_m_w_ffn_up': 'new_m', 'new_m_w_ffn_down': 'new_m', 'new_m_final_g': 'new_m', 'new_v_norm1_g': 'new_v', 'new_v_w_in': 'new_v', 'new_v_w_fg2': 'new_v', 'new_v_b_fg': 'new_v', 'new_v_gla_norm_g': 'new_v', 'new_v_w_oa': 'new_v', 'new_v_conv_w': 'new_v', 'new_v_conv_b': 'new_v', 'new_v_w_ob': 'new_v', 'new_v_w_o': 'new_v', 'new_v_norm2_g': 'new_v', 'new_v_w_ffn_gate': 'new_v', 'new_v_w_ffn_up': 'new_v', 'new_v_w_ffn_down': 'new_v', 'new_v_final_g': 'new_v'}


def _forward(args):
    return _fwd_reference(*[args[k] for k in FWD_PARAMS])


def _output_shape():
    out = _jax.eval_shape(lambda: _forward(_fwd_setup_inputs(0)))
    return out.shape, out.dtype

N_MICROBATCH = 1
ADAM_LR = 0.001
ADAM_B1 = 0.9
ADAM_B2 = 0.999
ADAM_EPS = 1e-08
ADAM_WD = 0.01
ADAM_STEP = 10
PER_EXAMPLE_BATCH_AXIS = {'x': 0, 'loss_target': 0}
SHARED_INPUTS = []
_WEIGHT_DTYPES = {'norm1_g': _jnp.float32, 'w_in': _jnp.float32, 'w_fg2': _jnp.float32, 'b_fg': _jnp.float32, 'gla_norm_g': _jnp.float32, 'w_oa': _jnp.float32, 'conv_w': _jnp.float32, 'conv_b': _jnp.float32, 'w_ob': _jnp.float32, 'w_o': _jnp.float32, 'norm2_g': _jnp.float32, 'w_ffn_gate': _jnp.float32, 'w_ffn_up': _jnp.float32, 'w_ffn_down': _jnp.float32, 'final_g': _jnp.float32}
MOMENT_SCALE = {'norm1_g': 2.082461e-01, 'w_in': 7.297991e-02, 'w_fg2': 9.592166e-03, 'b_fg': 3.699460e-02, 'gla_norm_g': 1.150699e-01, 'w_oa': 5.714254e-02, 'conv_w': 9.688971e-02, 'conv_b': 9.774005e-02, 'w_ob': 9.508336e-02, 'w_o': 1.110001e-01, 'norm2_g': 1.257308e-01, 'w_ffn_gate': 5.414401e-02, 'w_ffn_up': 5.234788e-02, 'w_ffn_down': 8.686332e-02, 'final_g': 3.199177e+01}


def _to_microbatches(a, axis):
    t = _jnp.moveaxis(a, axis, 0)
    t = t.reshape((N_MICROBATCH, t.shape[0] // N_MICROBATCH) + t.shape[1:])
    return _jnp.moveaxis(t, 1, axis + 1)


def setup_inputs(seed: int = 0) -> dict:
    inp = _fwd_setup_inputs(seed)
    key = _jax.random.fold_in(_jax.random.key(seed), 7919)
    shape, _ = _output_shape()
    out = dict(inp)
    out["loss_target"] = _jax.random.normal(_jax.random.fold_in(key, 0), shape, _jnp.float32)
    for i, name in enumerate(TWIN_WEIGHTS):
        w = inp[name].astype(_jnp.float32)
        if MOMENT_SCALE is None:
            s = _jnp.sqrt(_jnp.mean(_jnp.square(w)) + 1e-30)
        else:
            s = MOMENT_SCALE[name]
        km, kv = _jax.random.split(_jax.random.fold_in(key, i + 1))
        out[name] = w
        out["m_" + name] = s * _jax.random.normal(km, w.shape, _jnp.float32)
        out["v_" + name] = (s * s) * _jax.random.uniform(kv, w.shape, _jnp.float32, 0.5, 1.5)
    if N_MICROBATCH > 1:
        for name, axis in PER_EXAMPLE_BATCH_AXIS.items():
            out[name] = _to_microbatches(out[name], axis)
    return {'x': out['x'], 'norm1_g': out['norm1_g'], 'w_in': out['w_in'], 'w_fg2': out['w_fg2'], 'b_fg': out['b_fg'], 'gla_norm_g': out['gla_norm_g'], 'w_oa': out['w_oa'], 'conv_w': out['conv_w'], 'conv_b': out['conv_b'], 'w_ob': out['w_ob'], 'w_o': out['w_o'], 'norm2_g': out['norm2_g'], 'w_ffn_gate': out['w_ffn_gate'], 'w_ffn_up': out['w_ffn_up'], 'w_ffn_down': out['w_ffn_down'], 'final_g': out['final_g'], 'loss_target': out['loss_target'], 'm_norm1_g': out['m_norm1_g'], 'm_w_in': out['m_w_in'], 'm_w_fg2': out['m_w_fg2'], 'm_b_fg': out['m_b_fg'], 'm_gla_norm_g': out['m_gla_norm_g'], 'm_w_oa': out['m_w_oa'], 'm_conv_w': out['m_conv_w'], 'm_conv_b': out['m_conv_b'], 'm_w_ob': out['m_w_ob'], 'm_w_o': out['m_w_o'], 'm_norm2_g': out['m_norm2_g'], 'm_w_ffn_gate': out['m_w_ffn_gate'], 'm_w_ffn_up': out['m_w_ffn_up'], 'm_w_ffn_down': out['m_w_ffn_down'], 'm_final_g': out['m_final_g'], 'v_norm1_g': out['v_norm1_g'], 'v_w_in': out['v_w_in'], 'v_w_fg2': out['v_w_fg2'], 'v_b_fg': out['v_b_fg'], 'v_gla_norm_g': out['v_gla_norm_g'], 'v_w_oa': out['v_w_oa'], 'v_conv_w': out['v_conv_w'], 'v_conv_b': out['v_conv_b'], 'v_w_ob': out['v_w_ob'], 'v_w_o': out['v_w_o'], 'v_norm2_g': out['v_norm2_g'], 'v_w_ffn_gate': out['v_w_ffn_gate'], 'v_w_ffn_up': out['v_w_ffn_up'], 'v_w_ffn_down': out['v_w_ffn_down'], 'v_final_g': out['v_final_g']}


def _loss(weights, diff, rest, loss_target):
    with _jax.named_scope("forward"):
        args = {**rest, TWIN_DIFF_INPUT: diff, **{k: w.astype(_WEIGHT_DTYPES[k]) for k, w in weights.items()}}
        y = _forward(args)
    with _jax.named_scope("loss_head"):
        err = _jnp.square(y.astype(_jnp.float32) - loss_target)
        return 0.5 * _jnp.sum(_jnp.mean(err, axis=-1)) if err.ndim else 0.5 * err


def _adamw(w, g, m, v):
    m = ADAM_B1 * m + (1.0 - ADAM_B1) * g
    v = ADAM_B2 * v + (1.0 - ADAM_B2) * _jnp.square(g)
    m_hat = m / (1.0 - ADAM_B1 ** ADAM_STEP)
    v_hat = v / (1.0 - ADAM_B2 ** ADAM_STEP)
    delta = -ADAM_LR * (m_hat / (_jnp.sqrt(v_hat) + ADAM_EPS) + ADAM_WD * w)
    return delta, m, v


def reference(x, norm1_g, w_in, w_fg2, b_fg, gla_norm_g, w_oa, conv_w, conv_b, w_ob, w_o, norm2_g, w_ffn_gate, w_ffn_up, w_ffn_down, final_g, loss_target, m_norm1_g, m_w_in, m_w_fg2, m_b_fg, m_gla_norm_g, m_w_oa, m_conv_w, m_conv_b, m_w_ob, m_w_o, m_norm2_g, m_w_ffn_gate, m_w_ffn_up, m_w_ffn_down, m_final_g, v_norm1_g, v_w_in, v_w_fg2, v_b_fg, v_gla_norm_g, v_w_oa, v_conv_w, v_conv_b, v_w_ob, v_w_o, v_norm2_g, v_w_ffn_gate, v_w_ffn_up, v_w_ffn_down, v_final_g):
    given = dict(x=x, norm1_g=norm1_g, w_in=w_in, w_fg2=w_fg2, b_fg=b_fg, gla_norm_g=gla_norm_g, w_oa=w_oa, conv_w=conv_w, conv_b=conv_b, w_ob=w_ob, w_o=w_o, norm2_g=norm2_g, w_ffn_gate=w_ffn_gate, w_ffn_up=w_ffn_up, w_ffn_down=w_ffn_down, final_g=final_g, loss_target=loss_target, m_norm1_g=m_norm1_g, m_w_in=m_w_in, m_w_fg2=m_w_fg2, m_b_fg=m_b_fg, m_gla_norm_g=m_gla_norm_g, m_w_oa=m_w_oa, m_conv_w=m_conv_w, m_conv_b=m_conv_b, m_w_ob=m_w_ob, m_w_o=m_w_o, m_norm2_g=m_norm2_g, m_w_ffn_gate=m_w_ffn_gate, m_w_ffn_up=m_w_ffn_up, m_w_ffn_down=m_w_ffn_down, m_final_g=m_final_g, v_norm1_g=v_norm1_g, v_w_in=v_w_in, v_w_fg2=v_w_fg2, v_b_fg=v_b_fg, v_gla_norm_g=v_gla_norm_g, v_w_oa=v_w_oa, v_conv_w=v_conv_w, v_conv_b=v_conv_b, v_w_ob=v_w_ob, v_w_o=v_w_o, v_norm2_g=v_norm2_g, v_w_ffn_gate=v_w_ffn_gate, v_w_ffn_up=v_w_ffn_up, v_w_ffn_down=v_w_ffn_down, v_final_g=v_final_g)
    weights = {n: given[n] for n in TWIN_WEIGHTS}
    shared = {n: given[n] for n in SHARED_INPUTS}
    per_example = {n: given[n] for n in ['x']}
    grad_fn = _jax.value_and_grad(_loss, argnums=(0, 1))

    def one_microbatch(ex, loss_target):
        ex = dict(ex)
        diff = ex.pop(TWIN_DIFF_INPUT)
        return grad_fn(weights, diff, {**shared, **ex}, loss_target)

    if N_MICROBATCH == 1:
        loss, (grad_w, grad_x) = one_microbatch(per_example, given["loss_target"])
    else:
        def body(carry, xs):
            loss_sum, grad_sum = carry
            l_k, (gw_k, gx_k) = one_microbatch(xs[0], xs[1])
            with _jax.named_scope("update"):
                return (loss_sum + l_k, _jax.tree.map(_jnp.add, grad_sum, gw_k)), gx_k

        init = (_jnp.zeros((), _jnp.float32), _jax.tree.map(_jnp.zeros_like, weights))
        (loss, grad_w), grad_x = _jax.lax.scan(body, init, (per_example, given["loss_target"]))
    with _jax.named_scope("update"):
        delta_w, new_m, new_v = {}, {}, {}
        for n in TWIN_WEIGHTS:
            delta_w[n], new_m[n], new_v[n] = _adamw(weights[n], grad_w[n], given["m_" + n], given["v_" + n])
    return (loss, grad_x, *[grad_w[n] for n in TWIN_WEIGHTS], *[delta_w[n] for n in TWIN_WEIGHTS],
            *[new_m[n] for n in TWIN_WEIGHTS], *[new_v[n] for n in TWIN_WEIGHTS])
```

```python
import functools

import jax
import jax.numpy as jnp
from jax import lax
from jax.experimental import pallas as pl
from jax.experimental.pallas import tpu as pltpu

F32 = jnp.float32
BF16 = jnp.bfloat16

D_MODEL = 1024
DEPTH = 2
CHUNK = 64
GLA_HEADS = 4
KEY_WIDTH = 512
VAL_WIDTH = 1024
HEAD_K = 128
HEAD_V = 256
GATE_RANK = 16
GATE_TAU = 16.0
CONV_CH = 1024
FFN_HIDDEN = 2816
IN_WIDTH = 8208
NORM_EPS = 1e-6
Q_SCALE = HEAD_K ** -0.5
ADAM_LR = 0.001
ADAM_B1 = 0.9
ADAM_B2 = 0.999
ADAM_EPS = 1e-08
ADAM_WD = 0.01
ADAM_STEP = 10

N_CHIPS = 4
N_DEV = 8

LANE = 128
FZ_PAD = LANE
PROJ_W = 8192 + FZ_PAD
COL_Q, COL_K, COL_V, COL_R, COL_GBI, COL_GCI, COL_CX, COL_GA, COL_GB, COL_FZ = (
    0, 512, 1024, 2048, 3072, 4096, 5120, 6144, 7168, 8192)
FZ_ORIG = 3072

VMEM_LIMIT = 52 * 1024 * 1024
HALO = 16


def _cparams(sem=None):
    return pltpu.CompilerParams(dimension_semantics=sem, vmem_limit_bytes=VMEM_LIMIT)


def _sigmoid(x):
    return jax.nn.sigmoid(x)


def _log_sigmoid(x):
    return jnp.minimum(x, 0.0) - jnp.log1p(jnp.exp(-jnp.abs(x)))


_DOT_DIMS = {
    "nn": (((1,), (0,)), ((), ())),
    "nt": (((1,), (1,)), ((), ())),
    "tn": (((0,), (0,)), ((), ())),
}


def _matmul(a, b, *, mode, out_dtype, tm, tn, tk, name, add=None):
    if mode == "nn":
        (m, k), n = a.shape, b.shape[1]
    elif mode == "nt":
        (m, k), n = a.shape, b.shape[0]
    else:
        (k, m), n = a.shape, b.shape[1]
    tm, tn, tk = min(tm, m), min(tn, n), min(tk, k)
    assert m % tm == 0 and n % tn == 0 and k % tk == 0, (name, m, n, k, tm, tn, tk)
    nk = k // tk
    has_add = add is not None

    def body(*refs):
        if has_add:
            a_ref, b_ref, add_ref, o_ref = refs[:4]
            scratch = refs[4:]
        else:
            a_ref, b_ref, o_ref = refs[:3]
            add_ref = None
            scratch = refs[3:]
        part = lax.dot_general(a_ref[...], b_ref[...], _DOT_DIMS[mode], preferred_element_type=F32)

        def finish(acc):
            if add_ref is not None:
                acc = acc + add_ref[...].astype(F32)
            o_ref[...] = acc.astype(o_ref.dtype)

        if nk == 1:
            finish(part)
        else:
            acc_ref = scratch[0]
            kk = pl.program_id(2)

            @pl.when(kk == 0)
            def _():
                acc_ref[...] = part

            @pl.when(kk > 0)
            def _():
                acc_ref[...] += part

            @pl.when(kk == nk - 1)
            def _():
                finish(acc_ref[...])

    if mode == "nn":
        a_spec = pl.BlockSpec((tm, tk), lambda i, j, kk: (i, kk))
        b_spec = pl.BlockSpec((tk, tn), lambda i, j, kk: (kk, j))
    elif mode == "nt":
        a_spec = pl.BlockSpec((tm, tk), lambda i, j, kk: (i, kk))
        b_spec = pl.BlockSpec((tn, tk), lambda i, j, kk: (j, kk))
    else:
        a_spec = pl.BlockSpec((tk, tm), lambda i, j, kk: (kk, i))
        b_spec = pl.BlockSpec((tk, tn), lambda i, j, kk: (kk, j))
    o_spec = pl.BlockSpec((tm, tn), lambda i, j, kk: (i, j))
    in_specs = [a_spec, b_spec] + ([o_spec] if has_add else [])
    operands = (a, b) + ((add,) if has_add else ())
    return pl.pallas_call(
        body,
        name=name,
        out_shape=jax.ShapeDtypeStruct((m, n), out_dtype),
        grid=(m // tm, n // tn, nk),
        in_specs=in_specs,
        out_specs=o_spec,
        scratch_shapes=[pltpu.VMEM((tm, tn), F32)] if nk > 1 else [],
        compiler_params=_cparams(("parallel", "parallel", "arbitrary")),
    )(*operands)


def _rms_fwd(x, g, *, name):
    s, d = x.shape
    tr = min(512, s)

    def body(x_ref, g_ref, h_ref):
        xv = x_ref[...]
        rs = lax.rsqrt(jnp.mean(xv * xv, axis=-1, keepdims=True) + NORM_EPS)
        h_ref[...] = (xv * rs * g_ref[...]).astype(BF16)

    return pl.pallas_call(
        body, name=name,
        out_shape=jax.ShapeDtypeStruct((s, d), BF16),
        grid=(s // tr,),
        in_specs=[pl.BlockSpec((tr, d), lambda i: (i, 0)), pl.BlockSpec((1, d), lambda i: (0, 0))],
        out_specs=pl.BlockSpec((tr, d), lambda i: (i, 0)),
        compiler_params=_cparams(("parallel",)),
    )(x, g.reshape(1, d))


def _rms_bwd(x, g, dh, dres, *, name):
    s, d = x.shape
    tr = min(512, s)

    def body(x_ref, g_ref, dh_ref, dres_ref, dx_ref, dxb_ref, dg_ref):
        i = pl.program_id(0)
        xv = x_ref[...]
        rs = lax.rsqrt(jnp.mean(xv * xv, axis=-1, keepdims=True) + NORM_EPS)
        n = xv * rs
        dhv = dh_ref[...].astype(F32)
        dn = dhv * g_ref[...]
        dx = dres_ref[...] + rs * (dn - n * jnp.mean(dn * n, axis=-1, keepdims=True))
        dx_ref[...] = dx
        dxb_ref[...] = dx.astype(BF16)
        part = jnp.sum(dhv * n, axis=0, keepdims=True)

        @pl.when(i == 0)
        def _():
            dg_ref[...] = part

        @pl.when(i > 0)
        def _():
            dg_ref[...] += part

    row = pl.BlockSpec((tr, d), lambda i: (i, 0))
    vec = pl.BlockSpec((1, d), lambda i: (0, 0))
    return pl.pallas_call(
        body, name=name,
        out_shape=(jax.ShapeDtypeStruct((s, d), F32), jax.ShapeDtypeStruct((s, d), BF16),
                   jax.ShapeDtypeStruct((1, d), F32)),
        grid=(s // tr,),
        in_specs=[row, vec, row, row],
        out_specs=(row, row, vec),
        compiler_params=_cparams(("arbitrary",)),
    )(x, g.reshape(1, d), dh, dres)


def _loss_head(x, g, target):
    s, d = x.shape
    tr = min(512, s)

    def body(x_ref, g_ref, t_ref, loss_ref, dx_ref, dxb_ref, dg_ref):
        i = pl.program_id(0)
        xv = x_ref[...]
        rs = lax.rsqrt(jnp.mean(xv * xv, axis=-1, keepdims=True) + NORM_EPS)
        n = xv * rs
        gv = g_ref[...]
        err = n * gv - t_ref[...]
        row_loss = jnp.mean(err * err, axis=-1, keepdims=True)
        loss_part = 0.5 * jnp.sum(row_loss, axis=0, keepdims=True)
        dy = err * (1.0 / d)
        dn = dy * gv
        dx = rs * (dn - n * jnp.mean(dn * n, axis=-1, keepdims=True))
        dx_ref[...] = dx
        dxb_ref[...] = dx.astype(BF16)
        dg_part = jnp.sum(dy * n, axis=0, keepdims=True)

        @pl.when(i == 0)
        def _():
            loss_ref[...] = loss_part
            dg_ref[...] = dg_part

        @pl.when(i > 0)
        def _():
            loss_ref[...] += loss_part
            dg_ref[...] += dg_part

    row = pl.BlockSpec((tr, d), lambda i: (i, 0))
    vec = pl.BlockSpec((1, d), lambda i: (0, 0))
    one = pl.BlockSpec((1, 1), lambda i: (0, 0))
    return pl.pallas_call(
        body, name="loss_head",
        out_shape=(jax.ShapeDtypeStruct((1, 1), F32), jax.ShapeDtypeStruct((s, d), F32),
                   jax.ShapeDtypeStruct((s, d), BF16), jax.ShapeDtypeStruct((1, d), F32)),
        grid=(s // tr,),
        in_specs=[row, vec, row],
        out_specs=(one, row, row, vec),
        compiler_params=_cparams(("arbitrary",)),
    )(x, g.reshape(1, d), target)


def _conv_taps(u_prev, u, w_ref, rows):
    ext = jnp.concatenate([u_prev, u], axis=0)
    u1 = pltpu.roll(ext, 1, 0)[HALO:HALO + rows]
    u2 = pltpu.roll(ext, 2, 0)[HALO:HALO + rows]
    conv = w_ref[0:1, :] * u2 + w_ref[1:2, :] * u1 + w_ref[2:3, :] * u + w_ref[3:4, :]
    return conv, u1, u2


def _conv_fwd(proj, conv_wb, *, s):
    tr = min(512, s)
    c = CONV_CH
    hb = tr // HALO

    def body(gbi_ref, gci_ref, cx_ref, gci_h_ref, cx_h_ref, w_ref, o_ref):
        i = pl.program_id(0)
        u = gci_ref[...].astype(F32) * cx_ref[...].astype(F32)
        u_prev = gci_h_ref[...].astype(F32) * cx_h_ref[...].astype(F32)
        u_prev = jnp.where(i == 0, 0.0, u_prev)
        conv, _, _ = _conv_taps(u_prev, u, w_ref, tr)
        o_ref[...] = (gbi_ref[...].astype(F32) * conv).astype(BF16)

    def seg(col):
        return pl.BlockSpec((tr, c), lambda i: (i, col // c))

    def halo(col):
        return pl.BlockSpec((HALO, c), lambda i: (jnp.maximum(i * hb - 1, 0), col // c))

    return pl.pallas_call(
        body, name="conv_fwd",
        out_shape=jax.ShapeDtypeStruct((s, c), BF16),
        grid=(s // tr,),
        in_specs=[seg(COL_GBI), seg(COL_GCI), seg(COL_CX), halo(COL_GCI), halo(COL_CX),
                  pl.BlockSpec((8, c), lambda i: (0, 0))],
        out_specs=pl.BlockSpec((tr, c), lambda i: (i, 0)),
        compiler_params=_cparams(("parallel",)),
    )(proj, proj, proj, proj, proj, conv_wb)


def _conv_bwd(proj, dcb, conv_wb, *, s):
    tr = min(512, s)
    c = CONV_CH
    hb = tr // HALO
    nb = s // tr

    def body(gbi_ref, gci_ref, cx_ref, dcb_ref, gci_h_ref, cx_h_ref, gbi_n_ref, dcb_n_ref, w_ref,
             dp_ref, dw_ref):
        i = pl.program_id(0)
        gbi = gbi_ref[...].astype(F32)
        gci = gci_ref[...].astype(F32)
        cx = cx_ref[...].astype(F32)
        dcb_v = dcb_ref[...].astype(F32)
        u = gci * cx
        u_prev = jnp.where(i == 0, 0.0, gci_h_ref[...].astype(F32) * cx_h_ref[...].astype(F32))
        conv, u1, u2 = _conv_taps(u_prev, u, w_ref, tr)
        dconv = dcb_v * gbi
        dconv_next = jnp.where(i == nb - 1, 0.0, dcb_n_ref[...].astype(F32) * gbi_n_ref[...].astype(F32))
        ext = jnp.concatenate([dconv, dconv_next], axis=0)
        n_ext = tr + HALO
        d1 = pltpu.roll(ext, n_ext - 1, 0)[0:tr]
        d2 = pltpu.roll(ext, n_ext - 2, 0)[0:tr]
        du = w_ref[2:3, :] * dconv + w_ref[1:2, :] * d1 + w_ref[0:1, :] * d2
        dp_ref[:, 0:c] = (dcb_v * conv).astype(BF16)
        dp_ref[:, c:2 * c] = (du * cx).astype(BF16)
        dp_ref[:, 2 * c:3 * c] = (du * gci).astype(BF16)
        part = jnp.concatenate([
            jnp.sum(dconv * u2, axis=0, keepdims=True),
            jnp.sum(dconv * u1, axis=0, keepdims=True),
            jnp.sum(dconv * u, axis=0, keepdims=True),
            jnp.sum(dconv, axis=0, keepdims=True),
            jnp.zeros((4, c), F32)], axis=0)

        @pl.when(i == 0)
        def _():
            dw_ref[...] = part

        @pl.when(i > 0)
        def _():
            dw_ref[...] += part

    def seg(col):
        return pl.BlockSpec((tr, c), lambda i: (i, col // c))

    def halo_prev(col):
        return pl.BlockSpec((HALO, c), lambda i: (jnp.maximum(i * hb - 1, 0), col // c))

    def halo_next(col):
        return pl.BlockSpec((HALO, c), lambda i: (jnp.minimum((i + 1) * hb, nb * hb - 1), col // c))

    return pl.pallas_call(
        body, name="conv_bwd",
        out_shape=(jax.ShapeDtypeStruct((s, 3 * c), BF16), jax.ShapeDtypeStruct((8, c), F32)),
        grid=(nb,),
        in_specs=[seg(COL_GBI), seg(COL_GCI), seg(COL_CX), pl.BlockSpec((tr, c), lambda i: (i, 0)),
                  halo_prev(COL_GCI), halo_prev(COL_CX), halo_next(COL_GBI),
                  pl.BlockSpec((HALO, c), lambda i: (jnp.minimum((i + 1) * hb, nb * hb - 1), 0)),
                  pl.BlockSpec((8, c), lambda i: (0, 0))],
        out_specs=(pl.BlockSpec((tr, 3 * c), lambda i: (i, 0)), pl.BlockSpec((8, c), lambda i: (0, 0))),
        compiler_params=_cparams(("arbitrary",)),
    )(proj, proj, proj, dcb, proj, proj, proj, dcb, conv_wb)


def _mix_fwd(proj, ya, yb, *, s):
    tr = min(512, s)
    d = D_MODEL

    def body(ga_ref, gb_ref, ya_ref, yb_ref, o_ref):
        sa = _sigmoid(ga_ref[...].astype(F32))
        sb = _sigmoid(gb_ref[...].astype(F32))
        o_ref[...] = (sa * ya_ref[...].astype(F32) + sb * yb_ref[...].astype(F32)).astype(BF16)

    row = pl.BlockSpec((tr, d), lambda i: (i, 0))
    return pl.pallas_call(
        body, name="mix_fwd",
        out_shape=jax.ShapeDtypeStruct((s, d), BF16),
        grid=(s // tr,),
        in_specs=[pl.BlockSpec((tr, d), lambda i: (i, COL_GA // d)),
                  pl.BlockSpec((tr, d), lambda i: (i, COL_GB // d)), row, row],
        out_specs=row,
        compiler_params=_cparams(("parallel",)),
    )(proj, proj, ya, yb)


def _mix_bwd(proj, ya, yb, dmix, *, s):
    tr = min(512, s)
    d = D_MODEL

    def body(ga_ref, gb_ref, ya_ref, yb_ref, dm_ref, dya_ref, dyb_ref, dg_ref):
        sa = _sigmoid(ga_ref[...].astype(F32))
        sb = _sigmoid(gb_ref[...].astype(F32))
        dm = dm_ref[...].astype(F32)
        dya_ref[...] = (dm * sa).astype(BF16)
        dyb_ref[...] = (dm * sb).astype(BF16)
        dg_ref[:, 0:d] = (dm * ya_ref[...].astype(F32) * sa * (1.0 - sa)).astype(BF16)
        dg_ref[:, d:2 * d] = (dm * yb_ref[...].astype(F32) * sb * (1.0 - sb)).astype(BF16)

    row = pl.BlockSpec((tr, d), lambda i: (i, 0))
    return pl.pallas_call(
        body, name="mix_bwd",
        out_shape=(jax.ShapeDtypeStruct((s, d), BF16), jax.ShapeDtypeStruct((s, d), BF16),
                   jax.ShapeDtypeStruct((s, 2 * d), BF16)),
        grid=(s // tr,),
        in_specs=[pl.BlockSpec((tr, d), lambda i: (i, COL_GA // d)),
                  pl.BlockSpec((tr, d), lambda i: (i, COL_GB // d)), row, row, row],
        out_specs=(row, row, pl.BlockSpec((tr, 2 * d), lambda i: (i, 0))),
        compiler_params=_cparams(("parallel",)),
    )(proj, proj, ya, yb, dmix)


def _swiglu_fwd(gu, *, s):
    tr = min(256, s)
    f = FFN_HIDDEN

    def body(gt_ref, up_ref, o_ref):
        gt = gt_ref[...].astype(F32)
        o_ref[...] = (gt * _sigmoid(gt) * up_ref[...].astype(F32)).astype(BF16)

    return pl.pallas_call(
        body, name="swiglu_fwd",
        out_shape=jax.ShapeDtypeStruct((s, f), BF16),
        grid=(s // tr,),
        in_specs=[pl.BlockSpec((tr, f), lambda i: (i, 0)), pl.BlockSpec((tr, f), lambda i: (i, 1))],
        out_specs=pl.BlockSpec((tr, f), lambda i: (i, 0)),
        compiler_params=_cparams(("parallel",)),
    )(gu, gu)


def _swiglu_bwd(gu, dhid, *, s):
    tr = min(256, s)
    f = FFN_HIDDEN

    def body(gt_ref, up_ref, dh_ref, o_ref):
        gt = gt_ref[...].astype(F32)
        up = up_ref[...].astype(F32)
        dh = dh_ref[...].astype(F32)
        sg = _sigmoid(gt)
        o_ref[:, 0:f] = (dh * up * sg * (1.0 + gt * (1.0 - sg))).astype(BF16)
        o_ref[:, f:2 * f] = (dh * gt * sg).astype(BF16)

    return pl.pallas_call(
        body, name="swiglu_bwd",
        out_shape=jax.ShapeDtypeStruct((s, 2 * f), BF16),
        grid=(s // tr,),
        in_specs=[pl.BlockSpec((tr, f), lambda i: (i, 0)), pl.BlockSpec((tr, f), lambda i: (i, 1)),
                  pl.BlockSpec((tr, f), lambda i: (i, 0))],
        out_specs=pl.BlockSpec((tr, 2 * f), lambda i: (i, 0)),
        compiler_params=_cparams(("parallel",)),
    )(gu, gu, dhid)


def _tri(strict):
    r = lax.broadcasted_iota(jnp.int32, (CHUNK, CHUNK), 0)
    c = lax.broadcasted_iota(jnp.int32, (CHUNK, CHUNK), 1)
    return jnp.where((c < r) if strict else (c <= r), 1.0, 0.0).astype(F32)


def _gla_gate_terms(la_c, tri):
    cum = jnp.dot(tri, la_c, precision=lax.Precision.HIGHEST, preferred_element_type=F32)
    cend = cum[CHUNK - 1:CHUNK, :]
    return jnp.exp(cend - cum), jnp.exp(cend)


def _hk(h):
    return slice(h * HEAD_K, (h + 1) * HEAD_K)


def _hv(h):
    return slice(h * HEAD_V, (h + 1) * HEAD_V)


def _gla_fwd(proj, wfg2p, bfg, gn, *, s):
    tr = min(512, s)
    nb = s // tr
    nc = tr // CHUNK

    def body(q_ref, k_ref, v_ref, r_ref, fz_ref, w_ref, b_ref, gn_ref, oa_ref, st_ref,
             state, la_scr, o_scr):
        i = pl.program_id(0)

        @pl.when(i == 0)
        def _():
            state[...] = jnp.zeros_like(state)

        fg = jnp.dot(fz_ref[...], w_ref[...], preferred_element_type=F32) + b_ref[...]
        la_scr[...] = _log_sigmoid(fg) * (1.0 / GATE_TAU)
        tri = _tri(False)

        def chunk(ci, carry):
            r0 = pl.multiple_of(ci * CHUNK, CHUNK)
            rows = pl.ds(r0, CHUNK)
            e, gam = _gla_gate_terms(la_scr[rows, :], tri)
            kd = (k_ref[rows, :].astype(F32) * e).astype(BF16)
            qs = (q_ref[rows, :].astype(F32) * Q_SCALE).astype(BF16)
            v_c = v_ref[rows, :]
            for h in range(GLA_HEADS):
                upd = lax.dot_general(v_c[:, _hv(h)], kd[:, _hk(h)], _DOT_DIMS["tn"],
                                      preferred_element_type=F32)
                st_h = state[:, _hk(h)] * gam[:, _hk(h)] + upd
                state[:, _hk(h)] = st_h
                o_scr[rows, _hv(h)] = lax.dot_general(qs[:, _hk(h)], st_h.astype(BF16), _DOT_DIMS["nt"],
                                                       preferred_element_type=F32)
            st_ref[ci] = state[...]
            return carry

        lax.fori_loop(0, nc, chunk, 0)
        for h in range(GLA_HEADS):
            o = o_scr[:, _hv(h)]
            rs = lax.rsqrt(jnp.mean(o * o, axis=-1, keepdims=True) + NORM_EPS)
            rv = r_ref[:, _hv(h)].astype(F32)
            oa_ref[:, _hv(h)] = ((o * rs * gn_ref[...]).astype(F32) * (rv * _sigmoid(rv))).astype(BF16)

    return pl.pallas_call(
        body, name="gla_fwd",
        out_shape=(jax.ShapeDtypeStruct((s, VAL_WIDTH), BF16),
                   jax.ShapeDtypeStruct((s // CHUNK, HEAD_V, KEY_WIDTH), F32)),
        grid=(nb,),
        in_specs=[pl.BlockSpec((tr, KEY_WIDTH), lambda i: (i, COL_Q // KEY_WIDTH)),
                  pl.BlockSpec((tr, KEY_WIDTH), lambda i: (i, COL_K // KEY_WIDTH)),
                  pl.BlockSpec((tr, VAL_WIDTH), lambda i: (i, COL_V // VAL_WIDTH)),
                  pl.BlockSpec((tr, VAL_WIDTH), lambda i: (i, COL_R // VAL_WIDTH)),
                  pl.BlockSpec((tr, FZ_PAD), lambda i: (i, COL_FZ // FZ_PAD)),
                  pl.BlockSpec((FZ_PAD, KEY_WIDTH), lambda i: (0, 0)),
                  pl.BlockSpec((1, KEY_WIDTH), lambda i: (0, 0)),
                  pl.BlockSpec((1, HEAD_V), lambda i: (0, 0))],
        out_specs=(pl.BlockSpec((tr, VAL_WIDTH), lambda i: (i, 0)),
                   pl.BlockSpec((nc, HEAD_V, KEY_WIDTH), lambda i: (i, 0, 0))),
        scratch_shapes=[pltpu.VMEM((HEAD_V, KEY_WIDTH), F32), pltpu.VMEM((tr, KEY_WIDTH), F32),
                        pltpu.VMEM((tr, VAL_WIDTH), F32)],
        compiler_params=_cparams(("arbitrary",)),
    )(proj, proj, proj, proj, proj, wfg2p, bfg, gn)


def _gla_bwd(proj, doa, states, wfg2p, bfg, gn, *, s):
    tr = min(512, s)
    nb = s // tr
    nc = tr // CHUNK

    def body(q_ref, k_ref, v_ref, r_ref, fz_ref, doa_ref, st_ref, stp_ref, w_ref, b_ref, gn_ref,
             dp_ref, dfz_ref, dgn_ref, dw_ref, db_ref,
             carry, fg_scr, la_scr, dla_scr, o_scr, do_scr):
        i = pl.program_id(0)

        @pl.when(i == 0)
        def _():
            carry[...] = jnp.zeros_like(carry)
            dgn_ref[...] = jnp.zeros_like(dgn_ref)
            dw_ref[...] = jnp.zeros_like(dw_ref)
            db_ref[...] = jnp.zeros_like(db_ref)

        fz = fz_ref[...]
        fg = jnp.dot(fz, w_ref[...], preferred_element_type=F32) + b_ref[...]
        fg_scr[...] = fg
        la_scr[...] = _log_sigmoid(fg) * (1.0 / GATE_TAU)

        def fwd_chunk(ci, c0):
            r0 = pl.multiple_of(ci * CHUNK, CHUNK)
            rows = pl.ds(r0, CHUNK)
            qs = (q_ref[rows, :].astype(F32) * Q_SCALE).astype(BF16)
            st = st_ref[ci].astype(BF16)
            for h in range(GLA_HEADS):
                o_scr[rows, _hv(h)] = lax.dot_general(qs[:, _hk(h)], st[:, _hk(h)], _DOT_DIMS["nt"],
                                                       preferred_element_type=F32)
            return c0

        lax.fori_loop(0, nc, fwd_chunk, 0)

        gnv = gn_ref[...]
        dgn_part = jnp.zeros((1, HEAD_V), F32)
        for h in range(GLA_HEADS):
            o = o_scr[:, _hv(h)]
            rs = lax.rsqrt(jnp.mean(o * o, axis=-1, keepdims=True) + NORM_EPS)
            nrm = o * rs
            rv = r_ref[:, _hv(h)].astype(F32)
            sg = _sigmoid(rv)
            doa_h = doa_ref[:, _hv(h)].astype(F32)
            don = doa_h * (rv * sg)
            dp_ref[:, COL_R + h * HEAD_V:COL_R + (h + 1) * HEAD_V] = (
                doa_h * (nrm * gnv) * (sg * (1.0 + rv * (1.0 - sg)))).astype(BF16)
            dgn_part = dgn_part + jnp.sum(don * nrm, axis=0, keepdims=True)
            dn = don * gnv
            do_scr[:, _hv(h)] = (rs * (dn - nrm * jnp.mean(dn * nrm, axis=-1, keepdims=True))).astype(BF16)
        dgn_ref[...] += dgn_part

        tri = _tri(False)
        tri_s = _tri(True)
        first_block = i == nb - 1

        def bwd_chunk(cc, c0):
            ci = nc - 1 - cc
            r0 = pl.multiple_of(ci * CHUNK, CHUNK)
            rows = pl.ds(r0, CHUNK)
            e, gam = _gla_gate_terms(la_scr[rows, :], tri)
            k_c = k_ref[rows, :].astype(F32)
            kd = k_c * e
            kd_b = kd.astype(BF16)
            qs = (q_ref[rows, :].astype(F32) * Q_SCALE).astype(BF16)
            v_c = v_ref[rows, :]
            do_c = do_scr[rows, :]
            st = st_ref[ci]
            st_b = st.astype(BF16)
            st_prev_in = st_ref[jnp.maximum(ci - 1, 0)]
            st_prev_edge = jnp.where(first_block, 0.0, stp_ref[0])
            st_prev = jnp.where(ci > 0, st_prev_in, st_prev_edge)
            dkd_parts = []
            dgam_parts = []
            for h in range(GLA_HEADS):
                dst = lax.dot_general(do_c[:, _hv(h)], qs[:, _hk(h)], _DOT_DIMS["tn"],
                                      preferred_element_type=F32) + carry[:, _hk(h)]
                dst_b = dst.astype(BF16)
                dqs = jnp.dot(do_c[:, _hv(h)], st_b[:, _hk(h)], preferred_element_type=F32)
                dp_ref[rows, COL_Q + h * HEAD_K:COL_Q + (h + 1) * HEAD_K] = (dqs * Q_SCALE).astype(BF16)
                dkd_parts.append(jnp.dot(v_c[:, _hv(h)], dst_b, preferred_element_type=F32))
                dp_ref[rows, COL_V + h * HEAD_V:COL_V + (h + 1) * HEAD_V] = lax.dot_general(
                    kd_b[:, _hk(h)], dst_b, _DOT_DIMS["nt"], preferred_element_type=F32).astype(BF16)
                dgam_parts.append(jnp.sum(dst * st_prev[:, _hk(h)], axis=0, keepdims=True))
                carry[:, _hk(h)] = dst * gam[:, _hk(h)]
            dkd = jnp.concatenate(dkd_parts, axis=1)
            dgam = jnp.concatenate(dgam_parts, axis=1)
            dp_ref[rows, COL_K:COL_K + KEY_WIDTH] = (dkd * e).astype(BF16)
            dz = dkd * kd
            dla_scr[rows, :] = gam * dgam + jnp.dot(tri_s, dz, precision=lax.Precision.HIGHEST,
                                                    preferred_element_type=F32)
            return c0

        lax.fori_loop(0, nc, bwd_chunk, 0)

        dfg = dla_scr[...] * (1.0 / GATE_TAU) * _sigmoid(-fg_scr[...])
        dfg_b = dfg.astype(BF16)
        dfz_ref[...] = lax.dot_general(dfg_b, w_ref[...], _DOT_DIMS["nt"],
                                       preferred_element_type=F32).astype(BF16)
        dw_ref[...] += lax.dot_general(fz, dfg_b, _DOT_DIMS["tn"], preferred_element_type=F32)
        db_ref[...] += jnp.sum(dfg, axis=0, keepdims=True)

    def rev(i):
        return nb - 1 - i

    return pl.pallas_call(
        body, name="gla_bwd",
        out_shape=(jax.ShapeDtypeStruct((s, 3 * VAL_WIDTH), BF16),
                   jax.ShapeDtypeStruct((s, FZ_PAD), BF16),
                   jax.ShapeDtypeStruct((1, HEAD_V), F32),
                   jax.ShapeDtypeStruct((FZ_PAD, KEY_WIDTH), F32),
                   jax.ShapeDtypeStruct((1, KEY_WIDTH), F32)),
        grid=(nb,),
        in_specs=[pl.BlockSpec((tr, KEY_WIDTH), lambda i: (rev(i), COL_Q // KEY_WIDTH)),
                  pl.BlockSpec((tr, KEY_WIDTH), lambda i: (rev(i), COL_K // KEY_WIDTH)),
                  pl.BlockSpec((tr, VAL_WIDTH), lambda i: (rev(i), COL_V // VAL_WIDTH)),
                  pl.BlockSpec((tr, VAL_WIDTH), lambda i: (rev(i), COL_R // VAL_WIDTH)),
                  pl.BlockSpec((tr, FZ_PAD), lambda i: (rev(i), COL_FZ // FZ_PAD)),
                  pl.BlockSpec((tr, VAL_WIDTH), lambda i: (rev(i), 0)),
                  pl.BlockSpec((nc, HEAD_V, KEY_WIDTH), lambda i: (rev(i), 0, 0)),
                  pl.BlockSpec((1, HEAD_V, KEY_WIDTH), lambda i: (jnp.maximum(rev(i) * nc - 1, 0), 0, 0)),
                  pl.BlockSpec((FZ_PAD, KEY_WIDTH), lambda i: (0, 0)),
                  pl.BlockSpec((1, KEY_WIDTH), lambda i: (0, 0)),
                  pl.BlockSpec((1, HEAD_V), lambda i: (0, 0))],
        out_specs=(pl.BlockSpec((tr, 3 * VAL_WIDTH), lambda i: (rev(i), 0)),
                   pl.BlockSpec((tr, FZ_PAD), lambda i: (rev(i), 0)),
                   pl.BlockSpec((1, HEAD_V), lambda i: (0, 0)),
                   pl.BlockSpec((FZ_PAD, KEY_WIDTH), lambda i: (0, 0)),
                   pl.BlockSpec((1, KEY_WIDTH), lambda i: (0, 0))),
        scratch_shapes=[pltpu.VMEM((HEAD_V, KEY_WIDTH), F32),
                        pltpu.VMEM((tr, KEY_WIDTH), F32),
                        pltpu.VMEM((tr, KEY_WIDTH), F32),
                        pltpu.VMEM((tr, KEY_WIDTH), F32),
                        pltpu.VMEM((tr, VAL_WIDTH), F32),
                        pltpu.VMEM((tr, VAL_WIDTH), BF16)],
        compiler_params=_cparams(("arbitrary",)),
    )(proj, proj, proj, proj, proj, doa, states, states, wfg2p, bfg, gn)


def _adamw(w, g, m, v, *, name):
    shape = w.shape
    if w.ndim == 1:
        w, g, m, v = (t.reshape(1, 1, -1) for t in (w, g, m, v))
    elif w.ndim == 2:
        w, g, m, v = (t.reshape((1,) + t.shape) for t in (w, g, m, v))
    l, a, b = w.shape
    ta = a
    for cand in (256, 128, 64, 32, 16, 8):
        if a > cand and a % cand == 0:
            ta = cand
            break
    c1 = 1.0 / (1.0 - ADAM_B1 ** ADAM_STEP)
    c2 = 1.0 / (1.0 - ADAM_B2 ** ADAM_STEP)

    def body(w_ref, g_ref, m_ref, v_ref, d_ref, nm_ref, nv_ref):
        gv = g_ref[...]
        nm = ADAM_B1 * m_ref[...] + (1.0 - ADAM_B1) * gv
        nv = ADAM_B2 * v_ref[...] + (1.0 - ADAM_B2) * (gv * gv)
        nm_ref[...] = nm
        nv_ref[...] = nv
        d_ref[...] = -ADAM_LR * ((nm * c1) / (jnp.sqrt(nv * c2) + ADAM_EPS) + ADAM_WD * w_ref[...])

    blk = pl.BlockSpec((1, ta, b), lambda li, ai: (li, ai, 0))
    outs = pl.pallas_call(
        body, name=name,
        out_shape=tuple(jax.ShapeDtypeStruct((l, a, b), F32) for _ in range(3)),
        grid=(l, a // ta),
        in_specs=[blk, blk, blk, blk],
        out_specs=(blk, blk, blk),
        compiler_params=_cparams(("parallel", "parallel")),
    )(w, g, m, v)
    return tuple(o.reshape(shape) for o in outs)


MESH_ID = pl.DeviceIdType.MESH
ANY = pl.BlockSpec(memory_space=pl.ANY)


def _position():
    x, y, c = lax.axis_index("x"), lax.axis_index("y"), lax.axis_index("c")
    chips = [(1 - x, y), (x, 1 - y), (1 - x, 1 - y)]
    return x, y, c, chips


def _chip_index(xy):
    return 2 * xy[0] + xy[1]


def _all_gather_weights(shards):
    n = len(shards)

    def body(*refs):
        ins, outs = refs[:n], refs[n:2 * n]
        send_sems, recv_sems, pass_send, pass_recv, local_sems = refs[2 * n:]
        x, y, c, chips = _position()
        me = _chip_index((x, y))
        sibling = (x, y, 1 - c)

        def ici(a, k, chip_from):
            return pltpu.make_async_remote_copy(
                src_ref=ins[a].at[c], dst_ref=outs[a].at[c, _chip_index(chip_from)],
                send_sem=send_sems.at[a, k], recv_sem=recv_sems.at[a, k],
                device_id=(chips[k][0], chips[k][1], c), device_id_type=MESH_ID)

        def handoff(a, k, layer):
            slot = outs[a].at[layer, _chip_index(chips[k])]
            return pltpu.make_async_remote_copy(
                src_ref=slot, dst_ref=slot, send_sem=pass_send.at[a, k], recv_sem=pass_recv.at[a, k],
                device_id=sibling, device_id_type=MESH_ID)

        local = [pltpu.make_async_copy(ins[a].at[l], outs[a].at[l, me], local_sems.at[a, l])
                 for a in range(n) for l in range(DEPTH)]
        for cp in local:
            cp.start()
        for a in range(n):
            for k in range(3):
                ici(a, k, (x, y)).start()
        for k in range(3):
            for a in range(n):
                ici(a, k, chips[k]).wait_recv()
                handoff(a, k, c).start()
        for k in range(3):
            for a in range(n):
                handoff(a, k, 1 - c).wait_recv()
        for a in range(n):
            for k in range(3):
                ici(a, k, (x, y)).wait_send()
                handoff(a, k, c).wait_send()
        for cp in local:
            cp.wait()

    return pl.pallas_call(
        body, name="all_gather_weights",
        out_shape=tuple(jax.ShapeDtypeStruct((DEPTH, N_CHIPS) + t.shape[1:], t.dtype) for t in shards),
        in_specs=[ANY] * n, out_specs=tuple([ANY] * n),
        scratch_shapes=[pltpu.SemaphoreType.DMA((n, 3)), pltpu.SemaphoreType.DMA((n, 3)),
                        pltpu.SemaphoreType.DMA((n, 3)), pltpu.SemaphoreType.DMA((n, 3)),
                        pltpu.SemaphoreType.DMA((n, DEPTH))],
    )(*shards)


def _sibling_exchange(fulls):
    n = len(fulls)

    def body(*refs):
        ins, mine, sibs = refs[:n], refs[n:2 * n], refs[2 * n:3 * n]
        send_sems, recv_sems, local_sems = refs[3 * n:]
        x, y, c, _ = _position()
        local = [pltpu.make_async_copy(ins[a].at[c], mine[a], local_sems.at[a]) for a in range(n)]
        remote = [pltpu.make_async_remote_copy(
            src_ref=ins[a].at[1 - c], dst_ref=sibs[a], send_sem=send_sems.at[a], recv_sem=recv_sems.at[a],
            device_id=(x, y, 1 - c), device_id_type=MESH_ID) for a in range(n)]
        for cp in remote + local:
            cp.start()
        for cp in remote + local:
            cp.wait()

    dense = tuple(jax.ShapeDtypeStruct(t.shape[1:], t.dtype) for t in fulls)
    outs = pl.pallas_call(
        body, name="grad_sibling_exchange",
        out_shape=dense + dense,
        in_specs=[ANY] * n, out_specs=tuple([ANY] * (2 * n)),
        scratch_shapes=[pltpu.SemaphoreType.DMA((n,)), pltpu.SemaphoreType.DMA((n,)),
                        pltpu.SemaphoreType.DMA((n,))],
    )(*fulls)
    return outs[:n], outs[n:]


def _chip_exchange(parts, mines, sibs):
    n = len(parts)

    def body(*refs):
        ins, min_, sib_ = refs[:n], refs[n:2 * n], refs[2 * n:3 * n]
        outs, own_m, own_s = refs[3 * n:4 * n], refs[4 * n:5 * n], refs[5 * n:6 * n]
        send_sems, recv_sems, local_sems = refs[6 * n:]
        x, y, c, chips = _position()
        me = _chip_index((x, y))
        copies = [pltpu.make_async_remote_copy(
            src_ref=ins[a].at[_chip_index(chips[k])], dst_ref=outs[a].at[k],
            send_sem=send_sems.at[a, k], recv_sem=recv_sems.at[a, k],
            device_id=(chips[k][0], chips[k][1], c), device_id_type=MESH_ID)
            for a in range(n) for k in range(3)]
        local = [pltpu.make_async_copy(min_[a].at[me], own_m[a], local_sems.at[a, 0]) for a in range(n)]
        local += [pltpu.make_async_copy(sib_[a].at[me], own_s[a], local_sems.at[a, 1]) for a in range(n)]
        for cp in copies + local:
            cp.start()
        for cp in copies + local:
            cp.wait()

    recv = tuple(jax.ShapeDtypeStruct((3,) + t.shape[1:], t.dtype) for t in parts)
    own = tuple(jax.ShapeDtypeStruct(t.shape[1:], F32) for t in parts)
    outs = pl.pallas_call(
        body, name="grad_chip_exchange",
        out_shape=recv + own + own,
        in_specs=[ANY] * (3 * n), out_specs=tuple([ANY] * (3 * n)),
        scratch_shapes=[pltpu.SemaphoreType.DMA((n, 3)), pltpu.SemaphoreType.DMA((n, 3)),
                        pltpu.SemaphoreType.DMA((n, 2))],
    )(*parts, *mines, *sibs)
    return outs[:n], outs[n:2 * n], outs[2 * n:]


def _share_with_sibling(totals):
    n = len(totals)

    def body(*refs):
        ins, outs = refs[:n], refs[n:2 * n]
        send_sems, recv_sems, local_sems = refs[2 * n:]
        x, y, c, _ = _position()
        local = [pltpu.make_async_copy(ins[a], outs[a].at[c], local_sems.at[a]) for a in range(n)]
        remote = [pltpu.make_async_remote_copy(
            src_ref=ins[a], dst_ref=outs[a].at[c], send_sem=send_sems.at[a], recv_sem=recv_sems.at[a],
            device_id=(x, y, 1 - c), device_id_type=MESH_ID) for a in range(n)]
        for cp in local + remote:
            cp.start()
        for a in range(n):
            pltpu.make_async_remote_copy(
                src_ref=ins[a], dst_ref=outs[a].at[1 - c], send_sem=send_sems.at[a], recv_sem=recv_sems.at[a],
                device_id=(x, y, 1 - c), device_id_type=MESH_ID).wait_recv()
        for cp in remote:
            cp.wait_send()
        for cp in local:
            cp.wait()

    return pl.pallas_call(
        body, name="grad_share_with_sibling",
        out_shape=tuple(jax.ShapeDtypeStruct((DEPTH,) + t.shape, t.dtype) for t in totals),
        in_specs=[ANY] * n, out_specs=tuple([ANY] * n),
        scratch_shapes=[pltpu.SemaphoreType.DMA((n,)), pltpu.SemaphoreType.DMA((n,)),
                        pltpu.SemaphoreType.DMA((n,))],
    )(*totals)


def _row_tile(a):
    for cand in (256, 128, 64, 32, 16, 8):
        if a % cand == 0:
            return cand
    return a


def _pair_sum(mine, sib, *, name):
    nchip, a, b = mine.shape
    ta = _row_tile(a)

    def body(f_ref, s_ref, o_ref):
        o_ref[...] = (f_ref[...] + s_ref[...]).astype(BF16)

    blk = pl.BlockSpec((1, ta, b), lambda j, r: (j, r, 0))
    return pl.pallas_call(
        body, name=name,
        out_shape=jax.ShapeDtypeStruct((nchip, a, b), BF16),
        grid=(nchip, a // ta),
        in_specs=[blk, blk], out_specs=blk,
        compiler_params=_cparams(("parallel", "parallel")),
    )(mine, sib)


def _total_sum(own_m, own_s, recv, *, name):
    a, b = own_m.shape
    ta = _row_tile(a)

    def body(f_ref, s_ref, r_ref, o_ref):
        acc = f_ref[...] + s_ref[...]
        for k in range(3):
            acc = acc + r_ref[k].astype(F32)
        o_ref[...] = acc

    blk = pl.BlockSpec((ta, b), lambda r: (r, 0))
    return pl.pallas_call(
        body, name=name,
        out_shape=jax.ShapeDtypeStruct((a, b), F32),
        grid=(a // ta,),
        in_specs=[blk, blk, pl.BlockSpec((3, ta, b), lambda r: (0, r, 0))], out_specs=blk,
        compiler_params=_cparams(("parallel",)),
    )(own_m, own_s, recv)


SMALL_ROWS = 32


def _all_reduce_small(packed):
    rows, width = packed.shape

    def body(x_ref, out_ref, gath, send_sems, recv_sems, local_sem):
        x, y, c, chips = _position()
        me, sibling = (x, y, c), (x, y, 1 - c)

        def slot(px, py, pc):
            return gath.at[4 * px + 2 * py + pc]

        def copy(k, block, to, src=None):
            return pltpu.make_async_remote_copy(
                src_ref=slot(*block) if src is None else src, dst_ref=slot(*block),
                send_sem=send_sems.at[k], recv_sem=recv_sems.at[k], device_id=to, device_id_type=MESH_ID)

        mine = pltpu.make_async_copy(x_ref, slot(*me), local_sem)
        mine.start()
        first = [copy(0, me, sibling, src=x_ref)]
        first += [copy(1 + j, me, (*chip, c), src=x_ref) for j, chip in enumerate(chips)]
        for cp in first:
            cp.start()
        passed = [copy(4 + j, (*chip, c), sibling) for j, chip in enumerate(chips)]
        for j, chip in enumerate(chips):
            copy(1 + j, (*chip, c), me).wait_recv()
            passed[j].start()
        copy(0, sibling, me).wait_recv()
        for j, chip in enumerate(chips):
            copy(4 + j, (*chip, 1 - c), me).wait_recv()
        for cp in first + passed:
            cp.wait_send()
        mine.wait()
        acc = gath[0]
        for d in range(1, N_DEV):
            acc = acc + gath[d]
        out_ref[...] = acc

    return pl.pallas_call(
        body, name="all_reduce_small",
        out_shape=jax.ShapeDtypeStruct((rows, width), F32),
        in_specs=[pl.BlockSpec(memory_space=pltpu.VMEM)],
        out_specs=pl.BlockSpec(memory_space=pltpu.VMEM),
        scratch_shapes=[pltpu.VMEM((N_DEV, rows, width), F32), pltpu.SemaphoreType.DMA((7,)),
                        pltpu.SemaphoreType.DMA((7,)), pltpu.SemaphoreType.DMA],
    )(packed)


def _layer_forward(x0, w, s):
    h = _rms_fwd(x0, w["norm1_g"], name="rms1_fwd")
    proj = _matmul(h, w["w_in"], mode="nn", out_dtype=BF16, tm=1024, tn=1664, tk=1024, name="proj_fwd")
    oa, states = _gla_fwd(proj, w["w_fg2"], w["b_fg"], w["gla_norm_g"], s=s)
    cb_in = _conv_fwd(proj, w["conv_wb"], s=s)
    ya = _matmul(oa, w["w_oa"], mode="nn", out_dtype=BF16, tm=1024, tn=1024, tk=1024, name="ya_fwd")
    yb = _matmul(cb_in, w["w_ob"], mode="nn", out_dtype=BF16, tm=1024, tn=1024, tk=1024, name="yb_fwd")
    mix = _mix_fwd(proj, ya, yb, s=s)
    x1 = _matmul(mix, w["w_o"], mode="nn", out_dtype=F32, tm=1024, tn=1024, tk=1024, name="wo_fwd", add=x0)
    h2 = _rms_fwd(x1, w["norm2_g"], name="rms2_fwd")
    gu = _matmul(h2, w["w_gu"], mode="nn", out_dtype=BF16, tm=1024, tn=1408, tk=1024, name="ffn_in_fwd")
    hid = _swiglu_fwd(gu, s=s)
    x2 = _matmul(hid, w["w_ffn_down"], mode="nn", out_dtype=F32, tm=1024, tn=1024, tk=FFN_HIDDEN,
                 name="ffn_out_fwd", add=x1)
    saved = dict(x0=x0, h=h, proj=proj, oa=oa, states=states, cb_in=cb_in, ya=ya, yb=yb, mix=mix, x1=x1,
                 h2=h2, gu=gu, hid=hid)
    return x2, saved


def _layer_backward(dx2, dx2b, w, sv, s):
    g = {}
    dhid = _matmul(dx2b, w["w_ffn_down"], mode="nt", out_dtype=BF16, tm=1024, tn=FFN_HIDDEN, tk=1024,
                   name="ffn_out_bwd")
    g["w_ffn_down"] = _matmul(sv["hid"], dx2b, mode="tn", out_dtype=F32, tm=1408, tn=1024, tk=512,
                              name="ffn_out_wgrad")
    dgu = _swiglu_bwd(sv["gu"], dhid, s=s)
    g["w_gu"] = _matmul(sv["h2"], dgu, mode="tn", out_dtype=F32, tm=512, tn=FFN_HIDDEN, tk=512,
                        name="ffn_in_wgrad")
    dh2 = _matmul(dgu, w["w_gu"], mode="nt", out_dtype=F32, tm=512, tn=1024, tk=FFN_HIDDEN, name="ffn_in_bwd")
    dx1, dx1b, g["norm2_g"] = _rms_bwd(sv["x1"], w["norm2_g"], dh2, dx2, name="rms2_bwd")

    dmix = _matmul(dx1b, w["w_o"], mode="nt", out_dtype=BF16, tm=1024, tn=1024, tk=1024, name="wo_bwd")
    g["w_o"] = _matmul(sv["mix"], dx1b, mode="tn", out_dtype=F32, tm=1024, tn=1024, tk=512, name="wo_wgrad")
    dya, dyb, dgates = _mix_bwd(sv["proj"], sv["ya"], sv["yb"], dmix, s=s)
    dcb = _matmul(dyb, w["w_ob"], mode="nt", out_dtype=BF16, tm=1024, tn=1024, tk=1024, name="yb_bwd")
    g["w_ob"] = _matmul(sv["cb_in"], dyb, mode="tn", out_dtype=F32, tm=1024, tn=1024, tk=512, name="yb_wgrad")
    doa = _matmul(dya, w["w_oa"], mode="nt", out_dtype=BF16, tm=1024, tn=1024, tk=1024, name="ya_bwd")
    g["w_oa"] = _matmul(sv["oa"], dya, mode="tn", out_dtype=F32, tm=1024, tn=1024, tk=512, name="ya_wgrad")
    dconv_in, g["conv_wb"] = _conv_bwd(sv["proj"], dcb, w["conv_wb"], s=s)
    dqkvr, dfz, g["gla_norm_g"], g["w_fg2"], g["b_fg"] = _gla_bwd(
        sv["proj"], doa, sv["states"], w["w_fg2"], w["b_fg"], w["gla_norm_g"], s=s)
    dproj = jnp.concatenate([dqkvr, dconv_in, dgates, dfz], axis=1)
    g["w_in"] = _matmul(sv["h"], dproj, mode="tn", out_dtype=F32, tm=512, tn=1664, tk=1024, name="proj_wgrad")
    dh = _matmul(dproj, w["w_in"], mode="nt", out_dtype=F32, tm=1024, tn=1024, tk=1664, name="proj_bwd")
    dx0, dx0b, g["norm1_g"] = _rms_bwd(sv["x0"], w["norm1_g"], dh, dx1, name="rms1_bwd")
    return dx0, dx0b, g


def _local_step(x, target, layers, final_g):
    s = x.shape[0]
    saved = []
    for w in layers:
        x, sv = _layer_forward(x, w, s)
        saved.append(sv)
    loss, dx, dxb, dgf = _loss_head(x, final_g, target)
    grads = [None] * DEPTH
    for l in reversed(range(DEPTH)):
        dx, dxb, grads[l] = _layer_backward(dx, dxb, layers[l], saved[l], s)
    return loss, dx, grads, dgf


def _cols_from_chips(t):
    return jnp.transpose(t, (1, 0, 2)).reshape(t.shape[1], -1)


def _cols_to_chips(t):
    rows = t.shape[0]
    return jnp.transpose(t.reshape(rows, N_CHIPS, -1), (1, 0, 2))


def _w_in_to_kernel(w):
    pad = jnp.zeros((w.shape[0], FZ_PAD - GATE_RANK), w.dtype)
    return jnp.concatenate([w[:, :FZ_ORIG], w[:, FZ_ORIG + GATE_RANK:], w[:, FZ_ORIG:FZ_ORIG + GATE_RANK], pad],
                           axis=1)


def _w_in_from_kernel(w):
    return jnp.concatenate([w[:, :FZ_ORIG], w[:, COL_FZ:COL_FZ + GATE_RANK], w[:, FZ_ORIG:COL_FZ]], axis=1)


def kernel(x, norm1_g, w_in, w_fg2, b_fg, gla_norm_g, w_oa, conv_w, conv_b, w_ob, w_o, norm2_g, w_ffn_gate, w_ffn_up, w_ffn_down, final_g, loss_target, m_norm1_g, m_w_in, m_w_fg2, m_b_fg, m_gla_norm_g, m_w_oa, m_conv_w, m_conv_b, m_w_ob, m_w_o, m_norm2_g, m_w_ffn_gate, m_w_ffn_up, m_w_ffn_down, m_final_g, v_norm1_g, v_w_in, v_w_fg2, v_b_fg, v_gla_norm_g, v_w_oa, v_conv_w, v_conv_b, v_w_ob, v_w_o, v_norm2_g, v_w_ffn_gate, v_w_ffn_up, v_w_ffn_down, v_final_g):
    cx_ = lax.axis_index("x")
    cy_ = lax.axis_index("y")
    me = 2 * cx_ + cy_

    conv_w_p = jnp.pad(conv_w, ((0, 0), (0, 8 - conv_w.shape[1]), (0, 0)))
    big_names = ["w_in", "w_oa", "w_ob", "w_o", "w_ffn_gate", "w_ffn_up", "w_ffn_down"]
    big = dict(w_in=w_in, w_oa=w_oa, w_ob=w_ob, w_o=w_o, w_ffn_gate=w_ffn_gate, w_ffn_up=w_ffn_up,
               w_ffn_down=w_ffn_down)
    gathered = _all_gather_weights([big[n].astype(BF16) for n in big_names] + [w_fg2, conv_w_p])
    gw = dict(zip(big_names + ["w_fg2", "conv_w"], gathered))

    layers = []
    for l in range(DEPTH):
        w_fg2_full = _cols_from_chips(gw["w_fg2"][l])
        conv_w_full = _cols_from_chips(gw["conv_w"][l])[:3]
        layers.append(dict(
            norm1_g=norm1_g[l], norm2_g=norm2_g[l],
            w_in=_w_in_to_kernel(_cols_from_chips(gw["w_in"][l])),
            w_fg2=jnp.pad(w_fg2_full, ((0, FZ_PAD - GATE_RANK), (0, 0))).astype(BF16),
            b_fg=b_fg[l].reshape(1, KEY_WIDTH), gla_norm_g=gla_norm_g[l].reshape(1, HEAD_V),
            w_oa=gw["w_oa"][l].reshape(VAL_WIDTH, D_MODEL),
            w_ob=gw["w_ob"][l].reshape(CONV_CH, D_MODEL),
            w_o=gw["w_o"][l].reshape(D_MODEL, D_MODEL),
            conv_wb=jnp.concatenate([conv_w_full, conv_b[l].reshape(1, CONV_CH), jnp.zeros((4, CONV_CH), F32)],
                                    axis=0),
            w_gu=jnp.concatenate([_cols_from_chips(gw["w_ffn_gate"][l]), _cols_from_chips(gw["w_ffn_up"][l])],
                                 axis=1),
            w_ffn_down=gw["w_ffn_down"][l].reshape(FFN_HIDDEN, D_MODEL),
        ))

    loss_local, grad_x, grads, dgf = _local_step(x[0], loss_target[0], layers, final_g)
    loss = lax.psum(loss_local[0, 0], ("x", "y", "c"))

    def per_chip(name, l):
        g = grads[l]
        if name == "w_in":
            return _cols_to_chips(_w_in_from_kernel(g["w_in"]))
        if name == "w_ffn_gate":
            return _cols_to_chips(g["w_gu"][:, :FFN_HIDDEN])
        if name == "w_ffn_up":
            return _cols_to_chips(g["w_gu"][:, FFN_HIDDEN:])
        t = g[name]
        return t.reshape(N_CHIPS, t.shape[0] // N_CHIPS, t.shape[1])

    fulls = [jnp.stack([per_chip(n, l) for l in range(DEPTH)]) for n in big_names]
    mines, sibs = _sibling_exchange(fulls)
    parts = [_pair_sum(mn, sb, name="grad_pair_sum_" + n) for n, mn, sb in zip(big_names, mines, sibs)]
    recvs, own_m, own_s = _chip_exchange(parts, mines, sibs)
    totals = [_total_sum(om, os_, rc, name="grad_total_" + n)
              for n, om, os_, rc in zip(big_names, own_m, own_s, recvs)]
    big_grads = dict(zip(big_names, _share_with_sibling(totals)))

    def small_rows(t):
        return t.reshape(-1, D_MODEL)

    g0, g1 = grads
    pieces = [
        jnp.concatenate([g0["norm1_g"], g1["norm1_g"]], axis=0),
        jnp.concatenate([g0["norm2_g"], g1["norm2_g"]], axis=0),
        dgf,
        small_rows(jnp.concatenate([g0["b_fg"], g1["b_fg"]], axis=1)),
        small_rows(jnp.concatenate([g0["gla_norm_g"], g1["gla_norm_g"],
                                    jnp.zeros((1, D_MODEL - 2 * HEAD_V), F32)], axis=1)),
        jnp.concatenate([g0["conv_wb"][3:4], g1["conv_wb"][3:4]], axis=0),
        jnp.concatenate([g0["conv_wb"][:3], g1["conv_wb"][:3]], axis=0),
        small_rows(jnp.stack([g0["w_fg2"][:GATE_RANK], g1["w_fg2"][:GATE_RANK]])),
        jnp.zeros((1, D_MODEL), F32),
    ]
    small = _all_reduce_small(jnp.concatenate(pieces, axis=0))
    sg = dict(
        norm1_g=small[0:2], norm2_g=small[2:4], final_g=small[4],
        b_fg=small[5].reshape(DEPTH, KEY_WIDTH), gla_norm_g=small[6, :DEPTH * HEAD_V].reshape(DEPTH, HEAD_V),
        conv_b=small[7:9],
        conv_w=lax.dynamic_slice_in_dim(small[9:15].reshape(DEPTH, 3, CONV_CH), me * (CONV_CH // N_CHIPS),
                                        CONV_CH // N_CHIPS, axis=2),
        w_fg2=lax.dynamic_slice_in_dim(small[15:31].reshape(DEPTH, GATE_RANK, KEY_WIDTH),
                                       me * (KEY_WIDTH // N_CHIPS), KEY_WIDTH // N_CHIPS, axis=2),
    )

    all_g = dict(big_grads)
    all_g.update(sg)
    params = dict(norm1_g=(norm1_g, m_norm1_g, v_norm1_g), w_in=(w_in, m_w_in, v_w_in),
                  w_fg2=(w_fg2, m_w_fg2, v_w_fg2), b_fg=(b_fg, m_b_fg, v_b_fg),
                  gla_norm_g=(gla_norm_g, m_gla_norm_g, v_gla_norm_g), w_oa=(w_oa, m_w_oa, v_w_oa),
                  conv_w=(conv_w, m_conv_w, v_conv_w), conv_b=(conv_b, m_conv_b, v_conv_b),
                  w_ob=(w_ob, m_w_ob, v_w_ob), w_o=(w_o, m_w_o, v_w_o), norm2_g=(norm2_g, m_norm2_g, v_norm2_g),
                  w_ffn_gate=(w_ffn_gate, m_w_ffn_gate, v_w_ffn_gate), w_ffn_up=(w_ffn_up, m_w_ffn_up, v_w_ffn_up),
                  w_ffn_down=(w_ffn_down, m_w_ffn_down, v_w_ffn_down), final_g=(final_g, m_final_g, v_final_g))
    order = ["norm1_g", "w_in", "w_fg2", "b_fg", "gla_norm_g", "w_oa", "conv_w", "conv_b", "w_ob", "w_o",
             "norm2_g", "w_ffn_gate", "w_ffn_up", "w_ffn_down", "final_g"]
    deltas, new_m, new_v = [], [], []
    for n in order:
        w_, m_, v_ = params[n]
        d_, nm_, nv_ = _adamw(w_, all_g[n], m_, v_, name="adamw_" + n)
        deltas.append(d_)
        new_m.append(nm_)
        new_v.append(nv_)
    return (loss, grad_x[None], *[all_g[n] for n in order], *deltas, *new_m, *new_v)
```

```python
import functools

import jax
import jax.numpy as jnp
from jax import lax
from jax.experimental import pallas as pl
from jax.experimental.pallas import tpu as pltpu

F32 = jnp.float32
BF16 = jnp.bfloat16

D_MODEL = 1024
DEPTH = 2
CHUNK = 64
GLA_HEADS = 4
KEY_WIDTH = 512
VAL_WIDTH = 1024
HEAD_K = 128
HEAD_V = 256
GATE_RANK = 16
GATE_TAU = 16.0
CONV_CH = 1024
FFN_HIDDEN = 2816
IN_WIDTH = 8208
NORM_EPS = 1e-6
Q_SCALE = HEAD_K ** -0.5
ADAM_LR = 0.001
ADAM_B1 = 0.9
ADAM_B2 = 0.999
ADAM_EPS = 1e-08
ADAM_WD = 0.01
ADAM_STEP = 10

N_CHIPS = 4
N_DEV = 8

LANE = 128
FZ_PAD = LANE
PROJ_W = 8192 + FZ_PAD
COL_Q, COL_K, COL_V, COL_R, COL_GBI, COL_GCI, COL_CX, COL_GA, COL_GB, COL_FZ = (
    0, 512, 1024, 2048, 3072, 4096, 5120, 6144, 7168, 8192)
FZ_ORIG = 3072

VMEM_LIMIT = 52 * 1024 * 1024
HALO = 16


def _cparams(sem=None):
    return pltpu.CompilerParams(dimension_semantics=sem, vmem_limit_bytes=VMEM_LIMIT)


def _sigmoid(x):
    return jax.nn.sigmoid(x)


def _log_sigmoid(x):
    return jnp.minimum(x, 0.0) - jnp.log1p(jnp.exp(-jnp.abs(x)))


_DOT_DIMS = {
    "nn": (((1,), (0,)), ((), ())),
    "nt": (((1,), (1,)), ((), ())),
    "tn": (((0,), (0,)), ((), ())),
}


def _matmul(a, b, *, mode, out_dtype, tm, tn, tk, name, add=None):
    if mode == "nn":
        (m, k), n = a.shape, b.shape[1]
    elif mode == "nt":
        (m, k), n = a.shape, b.shape[0]
    else:
        (k, m), n = a.shape, b.shape[1]
    tm, tn, tk = min(tm, m), min(tn, n), min(tk, k)
    assert m % tm == 0 and n % tn == 0 and k % tk == 0, (name, m, n, k, tm, tn, tk)
    nk = k // tk
    has_add = add is not None

    def body(*refs):
        if has_add:
            a_ref, b_ref, add_ref, o_ref = refs[:4]
            scratch = refs[4:]
        else:
            a_ref, b_ref, o_ref = refs[:3]
            add_ref = None
            scratch = refs[3:]
        part = lax.dot_general(a_ref[...], b_ref[...], _DOT_DIMS[mode], preferred_element_type=F32)

        def finish(acc):
            if add_ref is not None:
                acc = acc + add_ref[...].astype(F32)
            o_ref[...] = acc.astype(o_ref.dtype)

        if nk == 1:
            finish(part)
        else:
            acc_ref = scratch[0]
            kk = pl.program_id(2)

            @pl.when(kk == 0)
            def _():
                acc_ref[...] = part

            @pl.when(kk > 0)
            def _():
                acc_ref[...] += part

            @pl.when(kk == nk - 1)
            def _():
                finish(acc_ref[...])

    if mode == "nn":
        a_spec = pl.BlockSpec((tm, tk), lambda i, j, kk: (i, kk))
        b_spec = pl.BlockSpec((tk, tn), lambda i, j, kk: (kk, j))
    elif mode == "nt":
        a_spec = pl.BlockSpec((tm, tk), lambda i, j, kk: (i, kk))
        b_spec = pl.BlockSpec((tn, tk), lambda i, j, kk: (j, kk))
    else:
        a_spec = pl.BlockSpec((tk, tm), lambda i, j, kk: (kk, i))
        b_spec = pl.BlockSpec((tk, tn), lambda i, j, kk: (kk, j))
    o_spec = pl.BlockSpec((tm, tn), lambda i, j, kk: (i, j))
    in_specs = [a_spec, b_spec] + ([o_spec] if has_add else [])
    operands = (a, b) + ((add,) if has_add else ())
    return pl.pallas_call(
        body,
        name=name,
        out_shape=jax.ShapeDtypeStruct((m, n), out_dtype),
        grid=(m // tm, n // tn, nk),
        in_specs=in_specs,
        out_specs=o_spec,
        scratch_shapes=[pltpu.VMEM((tm, tn), F32)] if nk > 1 else [],
        compiler_params=_cparams(("parallel", "parallel", "arbitrary")),
    )(*operands)


def _rms_fwd(x, g, *, name):
    s, d = x.shape
    tr = min(512, s)

    def body(x_ref, g_ref, h_ref):
        xv = x_ref[...]
        rs = lax.rsqrt(jnp.mean(xv * xv, axis=-1, keepdims=True) + NORM_EPS)
        h_ref[...] = (xv * rs * g_ref[...]).astype(BF16)

    return pl.pallas_call(
        body, name=name,
        out_shape=jax.ShapeDtypeStruct((s, d), BF16),
        grid=(s // tr,),
        in_specs=[pl.BlockSpec((tr, d), lambda i: (i, 0)), pl.BlockSpec((1, d), lambda i: (0, 0))],
        out_specs=pl.BlockSpec((tr, d), lambda i: (i, 0)),
        compiler_params=_cparams(("parallel",)),
    )(x, g.reshape(1, d))


def _rms_bwd(x, g, dh, dres, *, name):
    s, d = x.shape
    tr = min(512, s)

    def body(x_ref, g_ref, dh_ref, dres_ref, dx_ref, dxb_ref, dg_ref):
        i = pl.program_id(0)
        xv = x_ref[...]
        rs = lax.rsqrt(jnp.mean(xv * xv, axis=-1, keepdims=True) + NORM_EPS)
        n = xv * rs
        dhv = dh_ref[...].astype(F32)
        dn = dhv * g_ref[...]
        dx = dres_ref[...] + rs * (dn - n * jnp.mean(dn * n, axis=-1, keepdims=True))
        dx_ref[...] = dx
        dxb_ref[...] = dx.astype(BF16)
        part = jnp.sum(dhv * n, axis=0, keepdims=True)

        @pl.when(i == 0)
        def _():
            dg_ref[...] = part

        @pl.when(i > 0)
        def _():
            dg_ref[...] += part

    row = pl.BlockSpec((tr, d), lambda i: (i, 0))
    vec = pl.BlockSpec((1, d), lambda i: (0, 0))
    return pl.pallas_call(
        body, name=name,
        out_shape=(jax.ShapeDtypeStruct((s, d), F32), jax.ShapeDtypeStruct((s, d), BF16),
                   jax.ShapeDtypeStruct((1, d), F32)),
        grid=(s // tr,),
        in_specs=[row, vec, row, row],
        out_specs=(row, row, vec),
        compiler_params=_cparams(("arbitrary",)),
    )(x, g.reshape(1, d), dh, dres)


def _loss_head(x, g, target):
    s, d = x.shape
    tr = min(512, s)

    def body(x_ref, g_ref, t_ref, loss_ref, dx_ref, dxb_ref, dg_ref):
        i = pl.program_id(0)
        xv = x_ref[...]
        rs = lax.rsqrt(jnp.mean(xv * xv, axis=-1, keepdims=True) + NORM_EPS)
        n = xv * rs
        gv = g_ref[...]
        err = n * gv - t_ref[...]
        row_loss = jnp.mean(err * err, axis=-1, keepdims=True)
        loss_part = 0.5 * jnp.sum(row_loss, axis=0, keepdims=True)
        dy = err * (1.0 / d)
        dn = dy * gv
        dx = rs * (dn - n * jnp.mean(dn * n, axis=-1, keepdims=True))
        dx_ref[...] = dx
        dxb_ref[...] = dx.astype(BF16)
        dg_part = jnp.sum(dy * n, axis=0, keepdims=True)

        @pl.when(i == 0)
        def _():
            loss_ref[...] = loss_part
            dg_ref[...] = dg_part

        @pl.when(i > 0)
        def _():
            loss_ref[...] += loss_part
            dg_ref[...] += dg_part

    row = pl.BlockSpec((tr, d), lambda i: (i, 0))
    vec = pl.BlockSpec((1, d), lambda i: (0, 0))
    one = pl.BlockSpec((1, 1), lambda i: (0, 0))
    return pl.pallas_call(
        body, name="loss_head",
        out_shape=(jax.ShapeDtypeStruct((1, 1), F32), jax.ShapeDtypeStruct((s, d), F32),
                   jax.ShapeDtypeStruct((s, d), BF16), jax.ShapeDtypeStruct((1, d), F32)),
        grid=(s // tr,),
        in_specs=[row, vec, row],
        out_specs=(one, row, row, vec),
        compiler_params=_cparams(("arbitrary",)),
    )(x, g.reshape(1, d), target)


def _conv_taps(u_prev, u, w_ref, rows):
    ext = jnp.concatenate([u_prev, u], axis=0)
    u1 = pltpu.roll(ext, 1, 0)[HALO:HALO + rows]
    u2 = pltpu.roll(ext, 2, 0)[HALO:HALO + rows]
    conv = w_ref[0:1, :] * u2 + w_ref[1:2, :] * u1 + w_ref[2:3, :] * u + w_ref[3:4, :]
    return conv, u1, u2


def _conv_fwd(proj, conv_wb, *, s):
    tr = min(512, s)
    c = CONV_CH
    hb = tr // HALO

    def body(gbi_ref, gci_ref, cx_ref, gci_h_ref, cx_h_ref, w_ref, o_ref):
        i = pl.program_id(0)
        u = gci_ref[...].astype(F32) * cx_ref[...].astype(F32)
        u_prev = gci_h_ref[...].astype(F32) * cx_h_ref[...].astype(F32)
        u_prev = jnp.where(i == 0, 0.0, u_prev)
        conv, _, _ = _conv_taps(u_prev, u, w_ref, tr)
        o_ref[...] = (gbi_ref[...].astype(F32) * conv).astype(BF16)

    def seg(col):
        return pl.BlockSpec((tr, c), lambda i: (i, col // c))

    def halo(col):
        return pl.BlockSpec((HALO, c), lambda i: (jnp.maximum(i * hb - 1, 0), col // c))

    return pl.pallas_call(
        body, name="conv_fwd",
        out_shape=jax.ShapeDtypeStruct((s, c), BF16),
        grid=(s // tr,),
        in_specs=[seg(COL_GBI), seg(COL_GCI), seg(COL_CX), halo(COL_GCI), halo(COL_CX),
                  pl.BlockSpec((8, c), lambda i: (0, 0))],
        out_specs=pl.BlockSpec((tr, c), lambda i: (i, 0)),
        compiler_params=_cparams(("parallel",)),
    )(proj, proj, proj, proj, proj, conv_wb)


def _conv_bwd(proj, dcb, conv_wb, dproj, *, s):
    tr = min(512, s)
    c = CONV_CH
    hb = tr // HALO
    nb = s // tr

    def body(gbi_ref, gci_ref, cx_ref, dcb_ref, gci_h_ref, cx_h_ref, gbi_n_ref, dcb_n_ref, w_ref,
             dproj_in_ref, dp_ref, dw_ref):
        i = pl.program_id(0)
        gbi = gbi_ref[...].astype(F32)
        gci = gci_ref[...].astype(F32)
        cx = cx_ref[...].astype(F32)
        dcb_v = dcb_ref[...].astype(F32)
        u = gci * cx
        u_prev = jnp.where(i == 0, 0.0, gci_h_ref[...].astype(F32) * cx_h_ref[...].astype(F32))
        conv, u1, u2 = _conv_taps(u_prev, u, w_ref, tr)
        dconv = dcb_v * gbi
        dconv_next = jnp.where(i == nb - 1, 0.0, dcb_n_ref[...].astype(F32) * gbi_n_ref[...].astype(F32))
        ext = jnp.concatenate([dconv, dconv_next], axis=0)
        n_ext = tr + HALO
        d1 = pltpu.roll(ext, n_ext - 1, 0)[0:tr]
        d2 = pltpu.roll(ext, n_ext - 2, 0)[0:tr]
        du = w_ref[2:3, :] * dconv + w_ref[1:2, :] * d1 + w_ref[0:1, :] * d2
        dp_ref[:, 0:c] = (dcb_v * conv).astype(BF16)
        dp_ref[:, c:2 * c] = (du * cx).astype(BF16)
        dp_ref[:, 2 * c:3 * c] = (du * gci).astype(BF16)
        part = jnp.concatenate([
            jnp.sum(dconv * u2, axis=0, keepdims=True),
            jnp.sum(dconv * u1, axis=0, keepdims=True),
            jnp.sum(dconv * u, axis=0, keepdims=True),
            jnp.sum(dconv, axis=0, keepdims=True),
            jnp.zeros((4, c), F32)], axis=0)

        @pl.when(i == 0)
        def _():
            dw_ref[...] = part

        @pl.when(i > 0)
        def _():
            dw_ref[...] += part

    def seg(col):
        return pl.BlockSpec((tr, c), lambda i: (i, col // c))

    def halo_prev(col):
        return pl.BlockSpec((HALO, c), lambda i: (jnp.maximum(i * hb - 1, 0), col // c))

    def halo_next(col):
        return pl.BlockSpec((HALO, c), lambda i: (jnp.minimum((i + 1) * hb, nb * hb - 1), col // c))

    return pl.pallas_call(
        body, name="conv_bwd",
        out_shape=(jax.ShapeDtypeStruct((s, PROJ_W), BF16), jax.ShapeDtypeStruct((8, c), F32)),
        grid=(nb,),
        in_specs=[seg(COL_GBI), seg(COL_GCI), seg(COL_CX), pl.BlockSpec((tr, c), lambda i: (i, 0)),
                  halo_prev(COL_GCI), halo_prev(COL_CX), halo_next(COL_GBI),
                  pl.BlockSpec((HALO, c), lambda i: (jnp.minimum((i + 1) * hb, nb * hb - 1), 0)),
                  pl.BlockSpec((8, c), lambda i: (0, 0)), ANY],
        out_specs=(pl.BlockSpec((tr, 3 * c), lambda i: (i, COL_GBI // (3 * c))),
                   pl.BlockSpec((8, c), lambda i: (0, 0))),
        input_output_aliases={9: 0},
        compiler_params=_cparams(("arbitrary",)),
    )(proj, proj, proj, dcb, proj, proj, proj, dcb, conv_wb, dproj)


def _mix_fwd(proj, ya, yb, *, s):
    tr = min(512, s)
    d = D_MODEL

    def body(ga_ref, gb_ref, ya_ref, yb_ref, o_ref):
        sa = _sigmoid(ga_ref[...].astype(F32))
        sb = _sigmoid(gb_ref[...].astype(F32))
        o_ref[...] = (sa * ya_ref[...].astype(F32) + sb * yb_ref[...].astype(F32)).astype(BF16)

    row = pl.BlockSpec((tr, d), lambda i: (i, 0))
    return pl.pallas_call(
        body, name="mix_fwd",
        out_shape=jax.ShapeDtypeStruct((s, d), BF16),
        grid=(s // tr,),
        in_specs=[pl.BlockSpec((tr, d), lambda i: (i, COL_GA // d)),
                  pl.BlockSpec((tr, d), lambda i: (i, COL_GB // d)), row, row],
        out_specs=row,
        compiler_params=_cparams(("parallel",)),
    )(proj, proj, ya, yb)


def _mix_bwd(proj, ya, yb, dmix, *, s):
    tr = min(512, s)
    d = D_MODEL

    def body(ga_ref, gb_ref, ya_ref, yb_ref, dm_ref, dya_ref, dyb_ref, dg_ref):
        sa = _sigmoid(ga_ref[...].astype(F32))
        sb = _sigmoid(gb_ref[...].astype(F32))
        dm = dm_ref[...].astype(F32)
        dya_ref[...] = (dm * sa).astype(BF16)
        dyb_ref[...] = (dm * sb).astype(BF16)
        dg_ref[:, 0:d] = (dm * ya_ref[...].astype(F32) * sa * (1.0 - sa)).astype(BF16)
        dg_ref[:, d:2 * d] = (dm * yb_ref[...].astype(F32) * sb * (1.0 - sb)).astype(BF16)

    row = pl.BlockSpec((tr, d), lambda i: (i, 0))
    return pl.pallas_call(
        body, name="mix_bwd",
        out_shape=(jax.ShapeDtypeStruct((s, d), BF16), jax.ShapeDtypeStruct((s, d), BF16),
                   jax.ShapeDtypeStruct((s, PROJ_W), BF16)),
        grid=(s // tr,),
        in_specs=[pl.BlockSpec((tr, d), lambda i: (i, COL_GA // d)),
                  pl.BlockSpec((tr, d), lambda i: (i, COL_GB // d)), row, row, row],
        out_specs=(row, row, pl.BlockSpec((tr, 2 * d), lambda i: (i, COL_GA // (2 * d)))),
        compiler_params=_cparams(("parallel",)),
    )(proj, proj, ya, yb, dmix)


def _swiglu_fwd(gu, *, s):
    tr = min(256, s)
    f = FFN_HIDDEN

    def body(gt_ref, up_ref, o_ref):
        gt = gt_ref[...].astype(F32)
        o_ref[...] = (gt * _sigmoid(gt) * up_ref[...].astype(F32)).astype(BF16)

    return pl.pallas_call(
        body, name="swiglu_fwd",
        out_shape=jax.ShapeDtypeStruct((s, f), BF16),
        grid=(s // tr,),
        in_specs=[pl.BlockSpec((tr, f), lambda i: (i, 0)), pl.BlockSpec((tr, f), lambda i: (i, 1))],
        out_specs=pl.BlockSpec((tr, f), lambda i: (i, 0)),
        compiler_params=_cparams(("parallel",)),
    )(gu, gu)


def _swiglu_bwd(gu, dhid, *, s):
    tr = min(256, s)
    f = FFN_HIDDEN

    def body(gt_ref, up_ref, dh_ref, o_ref):
        gt = gt_ref[...].astype(F32)
        up = up_ref[...].astype(F32)
        dh = dh_ref[...].astype(F32)
        sg = _sigmoid(gt)
        o_ref[:, 0:f] = (dh * up * sg * (1.0 + gt * (1.0 - sg))).astype(BF16)
        o_ref[:, f:2 * f] = (dh * gt * sg).astype(BF16)

    return pl.pallas_call(
        body, name="swiglu_bwd",
        out_shape=jax.ShapeDtypeStruct((s, 2 * f), BF16),
        grid=(s // tr,),
        in_specs=[pl.BlockSpec((tr, f), lambda i: (i, 0)), pl.BlockSpec((tr, f), lambda i: (i, 1)),
                  pl.BlockSpec((tr, f), lambda i: (i, 0))],
        out_specs=pl.BlockSpec((tr, 2 * f), lambda i: (i, 0)),
        compiler_params=_cparams(("parallel",)),
    )(gu, gu, dhid)


def _tri(strict):
    r = lax.broadcasted_iota(jnp.int32, (CHUNK, CHUNK), 0)
    c = lax.broadcasted_iota(jnp.int32, (CHUNK, CHUNK), 1)
    return jnp.where((c < r) if strict else (c <= r), 1.0, 0.0).astype(F32)


def _gla_gate_terms(la_c, tri):
    cum = jnp.dot(tri, la_c, precision=lax.Precision.HIGHEST, preferred_element_type=F32)
    cend = cum[CHUNK - 1:CHUNK, :]
    return jnp.exp(cend - cum), jnp.exp(cend)


def _hk(h):
    return slice(h * HEAD_K, (h + 1) * HEAD_K)


def _hv(h):
    return slice(h * HEAD_V, (h + 1) * HEAD_V)


def _gla_fwd(proj, wfg2p, bfg, gn, *, s):
    tr = min(512, s)
    nb = s // tr
    nc = tr // CHUNK

    def body(q_ref, k_ref, v_ref, r_ref, fz_ref, w_ref, b_ref, gn_ref, oa_ref, st_ref,
             state, la_scr, o_scr):
        i = pl.program_id(0)

        @pl.when(i == 0)
        def _():
            state[...] = jnp.zeros_like(state)

        fg = jnp.dot(fz_ref[...], w_ref[...], preferred_element_type=F32) + b_ref[...]
        la_scr[...] = _log_sigmoid(fg) * (1.0 / GATE_TAU)
        tri = _tri(False)

        def chunk(ci, carry):
            r0 = pl.multiple_of(ci * CHUNK, CHUNK)
            rows = pl.ds(r0, CHUNK)
            e, gam = _gla_gate_terms(la_scr[rows, :], tri)
            kd = (k_ref[rows, :].astype(F32) * e).astype(BF16)
            qs = (q_ref[rows, :].astype(F32) * Q_SCALE).astype(BF16)
            v_c = v_ref[rows, :]
            for h in range(GLA_HEADS):
                upd = lax.dot_general(v_c[:, _hv(h)], kd[:, _hk(h)], _DOT_DIMS["tn"],
                                      preferred_element_type=F32)
                st_h = state[:, _hk(h)] * gam[:, _hk(h)] + upd
                state[:, _hk(h)] = st_h
                o_scr[rows, _hv(h)] = lax.dot_general(qs[:, _hk(h)], st_h.astype(BF16), _DOT_DIMS["nt"],
                                                       preferred_element_type=F32)
            st_ref[ci] = state[...]
            return carry

        lax.fori_loop(0, nc, chunk, 0)
        for h in range(GLA_HEADS):
            o = o_scr[:, _hv(h)]
            rs = lax.rsqrt(jnp.mean(o * o, axis=-1, keepdims=True) + NORM_EPS)
            rv = r_ref[:, _hv(h)].astype(F32)
            oa_ref[:, _hv(h)] = ((o * rs * gn_ref[...]).astype(F32) * (rv * _sigmoid(rv))).astype(BF16)

    return pl.pallas_call(
        body, name="gla_fwd",
        out_shape=(jax.ShapeDtypeStruct((s, VAL_WIDTH), BF16),
                   jax.ShapeDtypeStruct((s // CHUNK, HEAD_V, KEY_WIDTH), F32)),
        grid=(nb,),
        in_specs=[pl.BlockSpec((tr, KEY_WIDTH), lambda i: (i, COL_Q // KEY_WIDTH)),
                  pl.BlockSpec((tr, KEY_WIDTH), lambda i: (i, COL_K // KEY_WIDTH)),
                  pl.BlockSpec((tr, VAL_WIDTH), lambda i: (i, COL_V // VAL_WIDTH)),
                  pl.BlockSpec((tr, VAL_WIDTH), lambda i: (i, COL_R // VAL_WIDTH)),
                  pl.BlockSpec((tr, FZ_PAD), lambda i: (i, COL_FZ // FZ_PAD)),
                  pl.BlockSpec((FZ_PAD, KEY_WIDTH), lambda i: (0, 0)),
                  pl.BlockSpec((1, KEY_WIDTH), lambda i: (0, 0)),
                  pl.BlockSpec((1, HEAD_V), lambda i: (0, 0))],
        out_specs=(pl.BlockSpec((tr, VAL_WIDTH), lambda i: (i, 0)),
                   pl.BlockSpec((nc, HEAD_V, KEY_WIDTH), lambda i: (i, 0, 0))),
        scratch_shapes=[pltpu.VMEM((HEAD_V, KEY_WIDTH), F32), pltpu.VMEM((tr, KEY_WIDTH), F32),
                        pltpu.VMEM((tr, VAL_WIDTH), F32)],
        compiler_params=_cparams(("arbitrary",)),
    )(proj, proj, proj, proj, proj, wfg2p, bfg, gn)


def _gla_bwd(proj, doa, states, wfg2p, bfg, gn, dproj, *, s):
    tr = min(512, s)
    nb = s // tr
    nc = tr // CHUNK

    def body(q_ref, k_ref, v_ref, r_ref, fz_ref, doa_ref, st_ref, stp_ref, w_ref, b_ref, gn_ref, dproj_in_ref,
             dp_ref, dfz_ref, dgn_ref, dw_ref, db_ref,
             carry, fg_scr, la_scr, dla_scr, o_scr, do_scr):
        i = pl.program_id(0)

        @pl.when(i == 0)
        def _():
            carry[...] = jnp.zeros_like(carry)
            dgn_ref[...] = jnp.zeros_like(dgn_ref)
            dw_ref[...] = jnp.zeros_like(dw_ref)
            db_ref[...] = jnp.zeros_like(db_ref)

        fz = fz_ref[...]
        fg = jnp.dot(fz, w_ref[...], preferred_element_type=F32) + b_ref[...]
        fg_scr[...] = fg
        la_scr[...] = _log_sigmoid(fg) * (1.0 / GATE_TAU)

        def fwd_chunk(ci, c0):
            r0 = pl.multiple_of(ci * CHUNK, CHUNK)
            rows = pl.ds(r0, CHUNK)
            qs = (q_ref[rows, :].astype(F32) * Q_SCALE).astype(BF16)
            st = st_ref[ci].astype(BF16)
            for h in range(GLA_HEADS):
                o_scr[rows, _hv(h)] = lax.dot_general(qs[:, _hk(h)], st[:, _hk(h)], _DOT_DIMS["nt"],
                                                       preferred_element_type=F32)
            return c0

        lax.fori_loop(0, nc, fwd_chunk, 0)

        gnv = gn_ref[...]
        dgn_part = jnp.zeros((1, HEAD_V), F32)
        for h in range(GLA_HEADS):
            o = o_scr[:, _hv(h)]
            rs = lax.rsqrt(jnp.mean(o * o, axis=-1, keepdims=True) + NORM_EPS)
            nrm = o * rs
            rv = r_ref[:, _hv(h)].astype(F32)
            sg = _sigmoid(rv)
            doa_h = doa_ref[:, _hv(h)].astype(F32)
            don = doa_h * (rv * sg)
            dp_ref[:, COL_R + h * HEAD_V:COL_R + (h + 1) * HEAD_V] = (
                doa_h * (nrm * gnv) * (sg * (1.0 + rv * (1.0 - sg)))).astype(BF16)
            dgn_part = dgn_part + jnp.sum(don * nrm, axis=0, keepdims=True)
            dn = don * gnv
            do_scr[:, _hv(h)] = (rs * (dn - nrm * jnp.mean(dn * nrm, axis=-1, keepdims=True))).astype(BF16)
        dgn_ref[...] += dgn_part

        tri = _tri(False)
        tri_s = _tri(True)
        first_block = i == nb - 1

        def bwd_chunk(cc, c0):
            ci = nc - 1 - cc
            r0 = pl.multiple_of(ci * CHUNK, CHUNK)
            rows = pl.ds(r0, CHUNK)
            e, gam = _gla_gate_terms(la_scr[rows, :], tri)
            k_c = k_ref[rows, :].astype(F32)
            kd = k_c * e
            kd_b = kd.astype(BF16)
            qs = (q_ref[rows, :].astype(F32) * Q_SCALE).astype(BF16)
            v_c = v_ref[rows, :]
            do_c = do_scr[rows, :]
            st = st_ref[ci]
            st_b = st.astype(BF16)
            st_prev_in = st_ref[jnp.maximum(ci - 1, 0)]
            st_prev_edge = jnp.where(first_block, 0.0, stp_ref[0])
            st_prev = jnp.where(ci > 0, st_prev_in, st_prev_edge)
            dkd_parts = []
            dgam_parts = []
            for h in range(GLA_HEADS):
                dst = lax.dot_general(do_c[:, _hv(h)], qs[:, _hk(h)], _DOT_DIMS["tn"],
                                      preferred_element_type=F32) + carry[:, _hk(h)]
                dst_b = dst.astype(BF16)
                dqs = jnp.dot(do_c[:, _hv(h)], st_b[:, _hk(h)], preferred_element_type=F32)
                dp_ref[rows, COL_Q + h * HEAD_K:COL_Q + (h + 1) * HEAD_K] = (dqs * Q_SCALE).astype(BF16)
                dkd_parts.append(jnp.dot(v_c[:, _hv(h)], dst_b, preferred_element_type=F32))
                dp_ref[rows, COL_V + h * HEAD_V:COL_V + (h + 1) * HEAD_V] = lax.dot_general(
                    kd_b[:, _hk(h)], dst_b, _DOT_DIMS["nt"], preferred_element_type=F32).astype(BF16)
                dgam_parts.append(jnp.sum(dst * st_prev[:, _hk(h)], axis=0, keepdims=True))
                carry[:, _hk(h)] = dst * gam[:, _hk(h)]
            dkd = jnp.concatenate(dkd_parts, axis=1)
            dgam = jnp.concatenate(dgam_parts, axis=1)
            dp_ref[rows, COL_K:COL_K + KEY_WIDTH] = (dkd * e).astype(BF16)
            dz = dkd * kd
            dla_scr[rows, :] = gam * dgam + jnp.dot(tri_s, dz, precision=lax.Precision.HIGHEST,
                                                    preferred_element_type=F32)
            return c0

        lax.fori_loop(0, nc, bwd_chunk, 0)

        dfg = dla_scr[...] * (1.0 / GATE_TAU) * _sigmoid(-fg_scr[...])
        dfg_b = dfg.astype(BF16)
        dfz_ref[...] = lax.dot_general(dfg_b, w_ref[...], _DOT_DIMS["nt"],
                                       preferred_element_type=F32).astype(BF16)
        dw_ref[...] += lax.dot_general(fz, dfg_b, _DOT_DIMS["tn"], preferred_element_type=F32)
        db_ref[...] += jnp.sum(dfg, axis=0, keepdims=True)

    def rev(i):
        return nb - 1 - i

    return pl.pallas_call(
        body, name="gla_bwd",
        out_shape=(jax.ShapeDtypeStruct((s, PROJ_W), BF16),
                   jax.ShapeDtypeStruct((s, FZ_PAD), BF16),
                   jax.ShapeDtypeStruct((1, HEAD_V), F32),
                   jax.ShapeDtypeStruct((FZ_PAD, KEY_WIDTH), F32),
                   jax.ShapeDtypeStruct((1, KEY_WIDTH), F32)),
        grid=(nb,),
        in_specs=[pl.BlockSpec((tr, KEY_WIDTH), lambda i: (rev(i), COL_Q // KEY_WIDTH)),
                  pl.BlockSpec((tr, KEY_WIDTH), lambda i: (rev(i), COL_K // KEY_WIDTH)),
                  pl.BlockSpec((tr, VAL_WIDTH), lambda i: (rev(i), COL_V // VAL_WIDTH)),
                  pl.BlockSpec((tr, VAL_WIDTH), lambda i: (rev(i), COL_R // VAL_WIDTH)),
                  pl.BlockSpec((tr, FZ_PAD), lambda i: (rev(i), COL_FZ // FZ_PAD)),
                  pl.BlockSpec((tr, VAL_WIDTH), lambda i: (rev(i), 0)),
                  pl.BlockSpec((nc, HEAD_V, KEY_WIDTH), lambda i: (rev(i), 0, 0)),
                  pl.BlockSpec((1, HEAD_V, KEY_WIDTH), lambda i: (jnp.maximum(rev(i) * nc - 1, 0), 0, 0)),
                  pl.BlockSpec((FZ_PAD, KEY_WIDTH), lambda i: (0, 0)),
                  pl.BlockSpec((1, KEY_WIDTH), lambda i: (0, 0)),
                  pl.BlockSpec((1, HEAD_V), lambda i: (0, 0)), ANY],
        out_specs=(pl.BlockSpec((tr, 3 * VAL_WIDTH), lambda i: (rev(i), 0)),
                   pl.BlockSpec((tr, FZ_PAD), lambda i: (rev(i), 0)),
                   pl.BlockSpec((1, HEAD_V), lambda i: (0, 0)),
                   pl.BlockSpec((FZ_PAD, KEY_WIDTH), lambda i: (0, 0)),
                   pl.BlockSpec((1, KEY_WIDTH), lambda i: (0, 0))),
        scratch_shapes=[pltpu.VMEM((HEAD_V, KEY_WIDTH), F32),
                        pltpu.VMEM((tr, KEY_WIDTH), F32),
                        pltpu.VMEM((tr, KEY_WIDTH), F32),
                        pltpu.VMEM((tr, KEY_WIDTH), F32),
                        pltpu.VMEM((tr, VAL_WIDTH), F32),
                        pltpu.VMEM((tr, VAL_WIDTH), BF16)],
        input_output_aliases={11: 0},
        compiler_params=_cparams(("arbitrary",)),
    )(proj, proj, proj, proj, proj, doa, states, states, wfg2p, bfg, gn, dproj)


def _put_fz(dproj, dfz, *, s):
    tr = min(512, s)

    def body(dfz_ref, dproj_in_ref, o_ref):
        o_ref[...] = dfz_ref[...]

    return pl.pallas_call(
        body, name="put_dfz",
        out_shape=jax.ShapeDtypeStruct((s, PROJ_W), BF16),
        grid=(s // tr,),
        in_specs=[pl.BlockSpec((tr, FZ_PAD), lambda i: (i, 0)), ANY],
        out_specs=pl.BlockSpec((tr, FZ_PAD), lambda i: (i, COL_FZ // FZ_PAD)),
        input_output_aliases={1: 0},
        compiler_params=_cparams(("parallel",)),
    )(dfz, dproj)


def _adamw(w, g, m, v, *, name):
    shape = w.shape
    if w.ndim == 1:
        w, g, m, v = (t.reshape(1, 1, -1) for t in (w, g, m, v))
    elif w.ndim == 2:
        w, g, m, v = (t.reshape((1,) + t.shape) for t in (w, g, m, v))
    l, a, b = w.shape
    ta = a
    for cand in (256, 128, 64, 32, 16, 8):
        if a > cand and a % cand == 0:
            ta = cand
            break
    c1 = 1.0 / (1.0 - ADAM_B1 ** ADAM_STEP)
    c2 = 1.0 / (1.0 - ADAM_B2 ** ADAM_STEP)

    def body(w_ref, g_ref, m_ref, v_ref, d_ref, nm_ref, nv_ref):
        gv = g_ref[...]
        nm = ADAM_B1 * m_ref[...] + (1.0 - ADAM_B1) * gv
        nv = ADAM_B2 * v_ref[...] + (1.0 - ADAM_B2) * (gv * gv)
        nm_ref[...] = nm
        nv_ref[...] = nv
        d_ref[...] = -ADAM_LR * ((nm * c1) / (jnp.sqrt(nv * c2) + ADAM_EPS) + ADAM_WD * w_ref[...])

    blk = pl.BlockSpec((1, ta, b), lambda li, ai: (li, ai, 0))
    outs = pl.pallas_call(
        body, name=name,
        out_shape=tuple(jax.ShapeDtypeStruct((l, a, b), F32) for _ in range(3)),
        grid=(l, a // ta),
        in_specs=[blk, blk, blk, blk],
        out_specs=(blk, blk, blk),
        compiler_params=_cparams(("parallel", "parallel")),
    )(w, g, m, v)
    return tuple(o.reshape(shape) for o in outs)


MESH_ID = pl.DeviceIdType.MESH
ANY = pl.BlockSpec(memory_space=pl.ANY)


def _position():
    x, y, c = lax.axis_index("x"), lax.axis_index("y"), lax.axis_index("c")
    chips = [(1 - x, y), (x, 1 - y), (1 - x, 1 - y)]
    return x, y, c, chips


def _chip_index(xy):
    return 2 * xy[0] + xy[1]


def _all_gather_weights(shards):
    n = len(shards)

    def body(*refs):
        ins, outs = refs[:n], refs[n:2 * n]
        send_sems, recv_sems, pass_send, pass_recv = refs[2 * n:]
        x, y, c, chips = _position()
        sibling = (x, y, 1 - c)

        def ici(a, k, chip_from):
            return pltpu.make_async_remote_copy(
                src_ref=ins[a].at[c], dst_ref=outs[a].at[c, _chip_index(chip_from)],
                send_sem=send_sems.at[a, k], recv_sem=recv_sems.at[a, k],
                device_id=(chips[k][0], chips[k][1], c), device_id_type=MESH_ID)

        def handoff(a, k, layer):
            slot = outs[a].at[layer, _chip_index(chips[k])]
            return pltpu.make_async_remote_copy(
                src_ref=slot, dst_ref=slot, send_sem=pass_send.at[a, k], recv_sem=pass_recv.at[a, k],
                device_id=sibling, device_id_type=MESH_ID)

        for a in range(n):
            for k in range(3):
                ici(a, k, (x, y)).start()
        for k in range(3):
            for a in range(n):
                ici(a, k, chips[k]).wait_recv()
                handoff(a, k, c).start()
        for k in range(3):
            for a in range(n):
                handoff(a, k, 1 - c).wait_recv()
        for a in range(n):
            for k in range(3):
                ici(a, k, (x, y)).wait_send()
                handoff(a, k, c).wait_send()

    gathered = pl.pallas_call(
        body, name="all_gather_weights",
        out_shape=tuple(jax.ShapeDtypeStruct((DEPTH, N_CHIPS) + t.shape[1:], t.dtype) for t in shards),
        in_specs=[ANY] * n, out_specs=tuple([ANY] * n),
        scratch_shapes=[pltpu.SemaphoreType.DMA((n, 3)), pltpu.SemaphoreType.DMA((n, 3)),
                        pltpu.SemaphoreType.DMA((n, 3)), pltpu.SemaphoreType.DMA((n, 3))],
    )(*shards)
    me = _chip_index((lax.axis_index("x"), lax.axis_index("y")))
    return [lax.dynamic_update_index_in_dim(g, t[:, None], me, axis=1) for g, t in zip(gathered, shards)]


def _sibling_exchange(fulls):
    n = len(fulls)

    def body(*refs):
        ins, sibs = refs[:n], refs[n:2 * n]
        send_sems, recv_sems = refs[2 * n:]
        x, y, c, _ = _position()
        remote = [pltpu.make_async_remote_copy(
            src_ref=ins[a].at[1 - c], dst_ref=sibs[a], send_sem=send_sems.at[a], recv_sem=recv_sems.at[a],
            device_id=(x, y, 1 - c), device_id_type=MESH_ID) for a in range(n)]
        for cp in remote:
            cp.start()
        for cp in remote:
            cp.wait()

    return pl.pallas_call(
        body, name="grad_sibling_exchange",
        out_shape=tuple(jax.ShapeDtypeStruct(t.shape[1:], t.dtype) for t in fulls),
        in_specs=[ANY] * n, out_specs=tuple([ANY] * n),
        scratch_shapes=[pltpu.SemaphoreType.DMA((n,)), pltpu.SemaphoreType.DMA((n,))],
    )(*fulls)


def _chip_exchange(parts):
    n = len(parts)

    def body(*refs):
        ins, outs = refs[:n], refs[n:2 * n]
        send_sems, recv_sems = refs[2 * n:]
        x, y, c, chips = _position()
        copies = [pltpu.make_async_remote_copy(
            src_ref=ins[a].at[_chip_index(chips[k])], dst_ref=outs[a].at[k],
            send_sem=send_sems.at[a, k], recv_sem=recv_sems.at[a, k],
            device_id=(chips[k][0], chips[k][1], c), device_id_type=MESH_ID)
            for a in range(n) for k in range(3)]
        for cp in copies:
            cp.start()
        for cp in copies:
            cp.wait()

    return pl.pallas_call(
        body, name="grad_chip_exchange",
        out_shape=tuple(jax.ShapeDtypeStruct((3,) + t.shape[1:], t.dtype) for t in parts),
        in_specs=[ANY] * n, out_specs=tuple([ANY] * n),
        scratch_shapes=[pltpu.SemaphoreType.DMA((n, 3)), pltpu.SemaphoreType.DMA((n, 3))],
    )(*parts)


def _share_with_sibling(totals):
    n = len(totals)

    def body(*refs):
        ins, outs = refs[:n], refs[n:2 * n]
        send_sems, recv_sems = refs[2 * n:]
        x, y, c, _ = _position()
        remote = [pltpu.make_async_remote_copy(
            src_ref=ins[a], dst_ref=outs[a], send_sem=send_sems.at[a], recv_sem=recv_sems.at[a],
            device_id=(x, y, 1 - c), device_id_type=MESH_ID) for a in range(n)]
        for cp in remote:
            cp.start()
        for cp in remote:
            cp.wait()

    return pl.pallas_call(
        body, name="grad_share_with_sibling",
        out_shape=tuple(jax.ShapeDtypeStruct(t.shape, t.dtype) for t in totals),
        in_specs=[ANY] * n, out_specs=tuple([ANY] * n),
        scratch_shapes=[pltpu.SemaphoreType.DMA((n,)), pltpu.SemaphoreType.DMA((n,))],
    )(*totals)


def _row_tile(a):
    for cand in (256, 128, 64, 32, 16, 8):
        if a % cand == 0:
            return cand
    return a


def _pair_sum(mine, sib, *, name):
    nchip, a, b = mine.shape
    ta = _row_tile(a)

    def body(f_ref, s_ref, o_ref):
        o_ref[...] = (f_ref[...] + s_ref[...]).astype(BF16)

    blk = pl.BlockSpec((1, ta, b), lambda j, r: (j, r, 0))
    return pl.pallas_call(
        body, name=name,
        out_shape=jax.ShapeDtypeStruct((nchip, a, b), BF16),
        grid=(nchip, a // ta),
        in_specs=[blk, blk], out_specs=blk,
        compiler_params=_cparams(("parallel", "parallel")),
    )(mine, sib)


def _total_sum(own_m, own_s, recv, *, name):
    a, b = own_m.shape
    ta = _row_tile(a)

    def body(f_ref, s_ref, r_ref, o_ref):
        acc = f_ref[...] + s_ref[...]
        for k in range(3):
            acc = acc + r_ref[k].astype(F32)
        o_ref[...] = acc

    blk = pl.BlockSpec((ta, b), lambda r: (r, 0))
    return pl.pallas_call(
        body, name=name,
        out_shape=jax.ShapeDtypeStruct((a, b), F32),
        grid=(a // ta,),
        in_specs=[blk, blk, pl.BlockSpec((3, ta, b), lambda r: (0, r, 0))], out_specs=blk,
        compiler_params=_cparams(("parallel",)),
    )(own_m, own_s, recv)


SMALL_ROWS = 32


def _all_reduce_small(packed):
    rows, width = packed.shape

    def body(x_ref, out_ref, gath, send_sems, recv_sems, local_sem):
        x, y, c, chips = _position()
        me, sibling = (x, y, c), (x, y, 1 - c)

        def slot(px, py, pc):
            return gath.at[4 * px + 2 * py + pc]

        def copy(k, block, to, src=None):
            return pltpu.make_async_remote_copy(
                src_ref=slot(*block) if src is None else src, dst_ref=slot(*block),
                send_sem=send_sems.at[k], recv_sem=recv_sems.at[k], device_id=to, device_id_type=MESH_ID)

        mine = pltpu.make_async_copy(x_ref, slot(*me), local_sem)
        mine.start()
        first = [copy(0, me, sibling, src=x_ref)]
        first += [copy(1 + j, me, (*chip, c), src=x_ref) for j, chip in enumerate(chips)]
        for cp in first:
            cp.start()
        passed = [copy(4 + j, (*chip, c), sibling) for j, chip in enumerate(chips)]
        for j, chip in enumerate(chips):
            copy(1 + j, (*chip, c), me).wait_recv()
            passed[j].start()
        copy(0, sibling, me).wait_recv()
        for j, chip in enumerate(chips):
            copy(4 + j, (*chip, 1 - c), me).wait_recv()
        for cp in first + passed:
            cp.wait_send()
        mine.wait()
        acc = gath[0]
        for d in range(1, N_DEV):
            acc = acc + gath[d]
        out_ref[...] = acc

    return pl.pallas_call(
        body, name="all_reduce_small",
        out_shape=jax.ShapeDtypeStruct((rows, width), F32),
        in_specs=[pl.BlockSpec(memory_space=pltpu.VMEM)],
        out_specs=pl.BlockSpec(memory_space=pltpu.VMEM),
        scratch_shapes=[pltpu.VMEM((N_DEV, rows, width), F32), pltpu.SemaphoreType.DMA((7,)),
                        pltpu.SemaphoreType.DMA((7,)), pltpu.SemaphoreType.DMA],
    )(packed)


def _layer_forward(x0, w, s):
    h = _rms_fwd(x0, w["norm1_g"], name="rms1_fwd")
    proj = _matmul(h, w["w_in"], mode="nn", out_dtype=BF16, tm=1024, tn=1664, tk=1024, name="proj_fwd")
    oa, states = _gla_fwd(proj, w["w_fg2"], w["b_fg"], w["gla_norm_g"], s=s)
    cb_in = _conv_fwd(proj, w["conv_wb"], s=s)
    ya = _matmul(oa, w["w_oa"], mode="nn", out_dtype=BF16, tm=1024, tn=1024, tk=1024, name="ya_fwd")
    yb = _matmul(cb_in, w["w_ob"], mode="nn", out_dtype=BF16, tm=1024, tn=1024, tk=1024, name="yb_fwd")
    mix = _mix_fwd(proj, ya, yb, s=s)
    x1 = _matmul(mix, w["w_o"], mode="nn", out_dtype=F32, tm=1024, tn=1024, tk=1024, name="wo_fwd", add=x0)
    h2 = _rms_fwd(x1, w["norm2_g"], name="rms2_fwd")
    gu = _matmul(h2, w["w_gu"], mode="nn", out_dtype=BF16, tm=1024, tn=1408, tk=1024, name="ffn_in_fwd")
    hid = _swiglu_fwd(gu, s=s)
    x2 = _matmul(hid, w["w_ffn_down"], mode="nn", out_dtype=F32, tm=1024, tn=1024, tk=FFN_HIDDEN,
                 name="ffn_out_fwd", add=x1)
    saved = dict(x0=x0, h=h, proj=proj, oa=oa, states=states, cb_in=cb_in, ya=ya, yb=yb, mix=mix, x1=x1,
                 h2=h2, gu=gu, hid=hid)
    return x2, saved


def _layer_backward(dx2, dx2b, w, sv, s):
    g = {}
    dhid = _matmul(dx2b, w["w_ffn_down"], mode="nt", out_dtype=BF16, tm=1024, tn=FFN_HIDDEN, tk=1024,
                   name="ffn_out_bwd")
    g["w_ffn_down"] = _matmul(sv["hid"], dx2b, mode="tn", out_dtype=F32, tm=1408, tn=1024, tk=512,
                              name="ffn_out_wgrad")
    dgu = _swiglu_bwd(sv["gu"], dhid, s=s)
    g["w_gu"] = _matmul(sv["h2"], dgu, mode="tn", out_dtype=F32, tm=512, tn=FFN_HIDDEN, tk=512,
                        name="ffn_in_wgrad")
    dh2 = _matmul(dgu, w["w_gu"], mode="nt", out_dtype=F32, tm=512, tn=1024, tk=FFN_HIDDEN, name="ffn_in_bwd")
    dx1, dx1b, g["norm2_g"] = _rms_bwd(sv["x1"], w["norm2_g"], dh2, dx2, name="rms2_bwd")

    dmix = _matmul(dx1b, w["w_o"], mode="nt", out_dtype=BF16, tm=1024, tn=1024, tk=1024, name="wo_bwd")
    g["w_o"] = _matmul(sv["mix"], dx1b, mode="tn", out_dtype=F32, tm=1024, tn=1024, tk=512, name="wo_wgrad")
    dya, dyb, dproj = _mix_bwd(sv["proj"], sv["ya"], sv["yb"], dmix, s=s)
    dcb = _matmul(dyb, w["w_ob"], mode="nt", out_dtype=BF16, tm=1024, tn=1024, tk=1024, name="yb_bwd")
    g["w_ob"] = _matmul(sv["cb_in"], dyb, mode="tn", out_dtype=F32, tm=1024, tn=1024, tk=512, name="yb_wgrad")
    doa = _matmul(dya, w["w_oa"], mode="nt", out_dtype=BF16, tm=1024, tn=1024, tk=1024, name="ya_bwd")
    g["w_oa"] = _matmul(sv["oa"], dya, mode="tn", out_dtype=F32, tm=1024, tn=1024, tk=512, name="ya_wgrad")
    dproj, g["conv_wb"] = _conv_bwd(sv["proj"], dcb, w["conv_wb"], dproj, s=s)
    dproj, dfz, g["gla_norm_g"], g["w_fg2"], g["b_fg"] = _gla_bwd(
        sv["proj"], doa, sv["states"], w["w_fg2"], w["b_fg"], w["gla_norm_g"], dproj, s=s)
    dproj = _put_fz(dproj, dfz, s=s)
    g["w_in"] = _matmul(sv["h"], dproj, mode="tn", out_dtype=F32, tm=512, tn=1664, tk=1024, name="proj_wgrad")
    dh = _matmul(dproj, w["w_in"], mode="nt", out_dtype=F32, tm=1024, tn=1024, tk=1664, name="proj_bwd")
    dx0, dx0b, g["norm1_g"] = _rms_bwd(sv["x0"], w["norm1_g"], dh, dx1, name="rms1_bwd")
    return dx0, dx0b, g


def _local_step(x, target, layers, final_g):
    s = x.shape[0]
    saved = []
    for w in layers:
        x, sv = _layer_forward(x, w, s)
        saved.append(sv)
    loss, dx, dxb, dgf = _loss_head(x, final_g, target)
    grads = [None] * DEPTH
    for l in reversed(range(DEPTH)):
        dx, dxb, grads[l] = _layer_backward(dx, dxb, layers[l], saved[l], s)
    return loss, dx, grads, dgf


def _cols_from_chips(t):
    return jnp.transpose(t, (1, 0, 2)).reshape(t.shape[1], -1)


def _cols_to_chips(t):
    rows = t.shape[0]
    return jnp.transpose(t.reshape(rows, N_CHIPS, -1), (1, 0, 2))


def _w_in_to_kernel(w):
    pad = jnp.zeros((w.shape[0], FZ_PAD - GATE_RANK), w.dtype)
    return jnp.concatenate([w[:, :FZ_ORIG], w[:, FZ_ORIG + GATE_RANK:], w[:, FZ_ORIG:FZ_ORIG + GATE_RANK], pad],
                           axis=1)


def _w_in_from_kernel(w):
    return jnp.concatenate([w[:, :FZ_ORIG], w[:, COL_FZ:COL_FZ + GATE_RANK], w[:, FZ_ORIG:COL_FZ]], axis=1)


def kernel(x, norm1_g, w_in, w_fg2, b_fg, gla_norm_g, w_oa, conv_w, conv_b, w_ob, w_o, norm2_g, w_ffn_gate, w_ffn_up, w_ffn_down, final_g, loss_target, m_norm1_g, m_w_in, m_w_fg2, m_b_fg, m_gla_norm_g, m_w_oa, m_conv_w, m_conv_b, m_w_ob, m_w_o, m_norm2_g, m_w_ffn_gate, m_w_ffn_up, m_w_ffn_down, m_final_g, v_norm1_g, v_w_in, v_w_fg2, v_b_fg, v_gla_norm_g, v_w_oa, v_conv_w, v_conv_b, v_w_ob, v_w_o, v_norm2_g, v_w_ffn_gate, v_w_ffn_up, v_w_ffn_down, v_final_g):
    cx_ = lax.axis_index("x")
    cy_ = lax.axis_index("y")
    me = 2 * cx_ + cy_

    conv_w_p = jnp.pad(conv_w, ((0, 0), (0, 8 - conv_w.shape[1]), (0, 0)))
    big_names = ["w_in", "w_oa", "w_ob", "w_o", "w_ffn_gate", "w_ffn_up", "w_ffn_down"]
    big = dict(w_in=w_in, w_oa=w_oa, w_ob=w_ob, w_o=w_o, w_ffn_gate=w_ffn_gate, w_ffn_up=w_ffn_up,
               w_ffn_down=w_ffn_down)
    gathered = _all_gather_weights([big[n].astype(BF16) for n in big_names] + [w_fg2, conv_w_p])
    gw = dict(zip(big_names + ["w_fg2", "conv_w"], gathered))

    layers = []
    for l in range(DEPTH):
        w_fg2_full = _cols_from_chips(gw["w_fg2"][l])
        conv_w_full = _cols_from_chips(gw["conv_w"][l])[:3]
        layers.append(dict(
            norm1_g=norm1_g[l], norm2_g=norm2_g[l],
            w_in=_w_in_to_kernel(_cols_from_chips(gw["w_in"][l])),
            w_fg2=jnp.pad(w_fg2_full, ((0, FZ_PAD - GATE_RANK), (0, 0))).astype(BF16),
            b_fg=b_fg[l].reshape(1, KEY_WIDTH), gla_norm_g=gla_norm_g[l].reshape(1, HEAD_V),
            w_oa=gw["w_oa"][l].reshape(VAL_WIDTH, D_MODEL),
            w_ob=gw["w_ob"][l].reshape(CONV_CH, D_MODEL),
            w_o=gw["w_o"][l].reshape(D_MODEL, D_MODEL),
            conv_wb=jnp.concatenate([conv_w_full, conv_b[l].reshape(1, CONV_CH), jnp.zeros((4, CONV_CH), F32)],
                                    axis=0),
            w_gu=jnp.concatenate([_cols_from_chips(gw["w_ffn_gate"][l]), _cols_from_chips(gw["w_ffn_up"][l])],
                                 axis=1),
            w_ffn_down=gw["w_ffn_down"][l].reshape(FFN_HIDDEN, D_MODEL),
        ))

    loss_local, grad_x, grads, dgf = _local_step(x[0], loss_target[0], layers, final_g)
    loss = lax.psum(loss_local[0, 0], ("x", "y", "c"))

    def per_chip(name, l):
        g = grads[l]
        if name == "w_in":
            return _cols_to_chips(_w_in_from_kernel(g["w_in"]))
        if name == "w_ffn_gate":
            return _cols_to_chips(g["w_gu"][:, :FFN_HIDDEN])
        if name == "w_ffn_up":
            return _cols_to_chips(g["w_gu"][:, FFN_HIDDEN:])
        t = g[name]
        return t.reshape(N_CHIPS, t.shape[0] // N_CHIPS, t.shape[1])

    fulls = [jnp.stack([per_chip(n, l) for l in range(DEPTH)]) for n in big_names]
    cc_ = lax.axis_index("c")
    sibs = _sibling_exchange(fulls)
    mines = [lax.dynamic_index_in_dim(f, cc_, 0, keepdims=False) for f in fulls]
    parts = [_pair_sum(mn, sb, name="grad_pair_sum_" + n) for n, mn, sb in zip(big_names, mines, sibs)]
    recvs = _chip_exchange(parts)
    own_m = [lax.dynamic_index_in_dim(t, me, 0, keepdims=False) for t in mines]
    own_s = [lax.dynamic_index_in_dim(t, me, 0, keepdims=False) for t in sibs]
    totals = [_total_sum(om, os_, rc, name="grad_total_" + n)
              for n, om, os_, rc in zip(big_names, own_m, own_s, recvs)]
    others = _share_with_sibling(totals)
    big_grads = {n: jnp.where(cc_ == 0, jnp.stack([t, o]), jnp.stack([o, t]))
                 for n, t, o in zip(big_names, totals, others)}

    def small_rows(t):
        return t.reshape(-1, D_MODEL)

    g0, g1 = grads
    pieces = [
        jnp.concatenate([g0["norm1_g"], g1["norm1_g"]], axis=0),
        jnp.concatenate([g0["norm2_g"], g1["norm2_g"]], axis=0),
        dgf,
        small_rows(jnp.concatenate([g0["b_fg"], g1["b_fg"]], axis=1)),
        small_rows(jnp.concatenate([g0["gla_norm_g"], g1["gla_norm_g"],
                                    jnp.zeros((1, D_MODEL - 2 * HEAD_V), F32)], axis=1)),
        jnp.concatenate([g0["conv_wb"][3:4], g1["conv_wb"][3:4]], axis=0),
        jnp.concatenate([g0["conv_wb"][:3], g1["conv_wb"][:3]], axis=0),
        small_rows(jnp.stack([g0["w_fg2"][:GATE_RANK], g1["w_fg2"][:GATE_RANK]])),
        jnp.zeros((1, D_MODEL), F32),
    ]
    small = _all_reduce_small(jnp.concatenate(pieces, axis=0))
    sg = dict(
        norm1_g=small[0:2], norm2_g=small[2:4], final_g=small[4],
        b_fg=small[5].reshape(DEPTH, KEY_WIDTH), gla_norm_g=small[6, :DEPTH * HEAD_V].reshape(DEPTH, HEAD_V),
        conv_b=small[7:9],
        conv_w=lax.dynamic_slice_in_dim(small[9:15].reshape(DEPTH, 3, CONV_CH), me * (CONV_CH // N_CHIPS),
                                        CONV_CH // N_CHIPS, axis=2),
        w_fg2=lax.dynamic_slice_in_dim(small[15:31].reshape(DEPTH, GATE_RANK, KEY_WIDTH),
                                       me * (KEY_WIDTH // N_CHIPS), KEY_WIDTH // N_CHIPS, axis=2),
    )

    all_g = dict(big_grads)
    all_g.update(sg)
    params = dict(norm1_g=(norm1_g, m_norm1_g, v_norm1_g), w_in=(w_in, m_w_in, v_w_in),
                  w_fg2=(w_fg2, m_w_fg2, v_w_fg2), b_fg=(b_fg, m_b_fg, v_b_fg),
                  gla_norm_g=(gla_norm_g, m_gla_norm_g, v_gla_norm_g), w_oa=(w_oa, m_w_oa, v_w_oa),
                  conv_w=(conv_w, m_conv_w, v_conv_w), conv_b=(conv_b, m_conv_b, v_conv_b),
                  w_ob=(w_ob, m_w_ob, v_w_ob), w_o=(w_o, m_w_o, v_w_o), norm2_g=(norm2_g, m_norm2_g, v_norm2_g),
                  w_ffn_gate=(w_ffn_gate, m_w_ffn_gate, v_w_ffn_gate), w_ffn_up=(w_ffn_up, m_w_ffn_up, v_w_ffn_up),
                  w_ffn_down=(w_ffn_down, m_w_ffn_down, v_w_ffn_down), final_g=(final_g, m_final_g, v_final_g))
    order = ["norm1_g", "w_in", "w_fg2", "b_fg", "gla_norm_g", "w_oa", "conv_w", "conv_b", "w_ob", "w_o",
             "norm2_g", "w_ffn_gate", "w_ffn_up", "w_ffn_down", "final_g"]
    deltas, new_m, new_v = [], [], []
    for n in order:
        w_, m_, v_ = params[n]
        d_, nm_, nv_ = _adamw(w_, all_g[n], m_, v_, name="adamw_" + n)
        deltas.append(d_)
        new_m.append(nm_)
        new_v.append(nv_)
    return (loss, grad_x[None], *[all_g[n] for n in order], *deltas, *new_m, *new_v)
```

```python
import functools

import jax
import jax.numpy as jnp
from jax import lax
from jax.experimental import pallas as pl
from jax.experimental.pallas import tpu as pltpu

F32 = jnp.float32
BF16 = jnp.bfloat16

D_MODEL = 1024
DEPTH = 2
CHUNK = 64
GLA_HEADS = 4
KEY_WIDTH = 512
VAL_WIDTH = 1024
HEAD_K = 128
HEAD_V = 256
GATE_RANK = 16
GATE_TAU = 16.0
CONV_CH = 1024
FFN_HIDDEN = 2816
IN_WIDTH = 8208
NORM_EPS = 1e-6
Q_SCALE = HEAD_K ** -0.5
ADAM_LR = 0.001
ADAM_B1 = 0.9
ADAM_B2 = 0.999
ADAM_EPS = 1e-08
ADAM_WD = 0.01
ADAM_STEP = 10

N_CHIPS = 4
N_DEV = 8

LANE = 128
FZ_PAD = LANE
PROJ_W = 8192 + FZ_PAD
COL_Q, COL_K, COL_V, COL_R, COL_GBI, COL_GCI, COL_CX, COL_GA, COL_GB, COL_FZ = (
    0, 512, 1024, 2048, 3072, 4096, 5120, 6144, 7168, 8192)
FZ_ORIG = 3072

VMEM_LIMIT = 52 * 1024 * 1024
HALO = 16


def _cparams(sem=None):
    return pltpu.CompilerParams(dimension_semantics=sem, vmem_limit_bytes=VMEM_LIMIT)


def _sigmoid(x):
    return jax.nn.sigmoid(x)


def _log_sigmoid(x):
    return jnp.minimum(x, 0.0) - jnp.log1p(jnp.exp(-jnp.abs(x)))


_DOT_DIMS = {
    "nn": (((1,), (0,)), ((), ())),
    "nt": (((1,), (1,)), ((), ())),
    "tn": (((0,), (0,)), ((), ())),
}


def _matmul(a, b, *, mode, out_dtype, tm, tn, tk, name, add=None):
    if mode == "nn":
        (m, k), n = a.shape, b.shape[1]
    elif mode == "nt":
        (m, k), n = a.shape, b.shape[0]
    else:
        (k, m), n = a.shape, b.shape[1]
    tm, tn, tk = min(tm, m), min(tn, n), min(tk, k)
    assert m % tm == 0 and n % tn == 0 and k % tk == 0, (name, m, n, k, tm, tn, tk)
    nk = k // tk
    has_add = add is not None

    def body(*refs):
        if has_add:
            a_ref, b_ref, add_ref, o_ref = refs[:4]
            scratch = refs[4:]
        else:
            a_ref, b_ref, o_ref = refs[:3]
            add_ref = None
            scratch = refs[3:]
        part = lax.dot_general(a_ref[...], b_ref[...], _DOT_DIMS[mode], preferred_element_type=F32)

        def finish(acc):
            if add_ref is not None:
                acc = acc + add_ref[...].astype(F32)
            o_ref[...] = acc.astype(o_ref.dtype)

        if nk == 1:
            finish(part)
        else:
            acc_ref = scratch[0]
            kk = pl.program_id(2)

            @pl.when(kk == 0)
            def _():
                acc_ref[...] = part

            @pl.when(kk > 0)
            def _():
                acc_ref[...] += part

            @pl.when(kk == nk - 1)
            def _():
                finish(acc_ref[...])

    if mode == "nn":
        a_spec = pl.BlockSpec((tm, tk), lambda i, j, kk: (i, kk))
        b_spec = pl.BlockSpec((tk, tn), lambda i, j, kk: (kk, j))
    elif mode == "nt":
        a_spec = pl.BlockSpec((tm, tk), lambda i, j, kk: (i, kk))
        b_spec = pl.BlockSpec((tn, tk), lambda i, j, kk: (j, kk))
    else:
        a_spec = pl.BlockSpec((tk, tm), lambda i, j, kk: (kk, i))
        b_spec = pl.BlockSpec((tk, tn), lambda i, j, kk: (kk, j))
    o_spec = pl.BlockSpec((tm, tn), lambda i, j, kk: (i, j))
    in_specs = [a_spec, b_spec] + ([o_spec] if has_add else [])
    operands = (a, b) + ((add,) if has_add else ())
    return pl.pallas_call(
        body,
        name=name,
        out_shape=jax.ShapeDtypeStruct((m, n), out_dtype),
        grid=(m // tm, n // tn, nk),
        in_specs=in_specs,
        out_specs=o_spec,
        scratch_shapes=[pltpu.VMEM((tm, tn), F32)] if nk > 1 else [],
        compiler_params=_cparams(("parallel", "parallel", "arbitrary")),
    )(*operands)


def _rms_fwd(x, g, *, name):
    s, d = x.shape
    tr = min(512, s)

    def body(x_ref, g_ref, h_ref):
        xv = x_ref[...]
        rs = lax.rsqrt(jnp.mean(xv * xv, axis=-1, keepdims=True) + NORM_EPS)
        h_ref[...] = (xv * rs * g_ref[...]).astype(BF16)

    return pl.pallas_call(
        body, name=name,
        out_shape=jax.ShapeDtypeStruct((s, d), BF16),
        grid=(s // tr,),
        in_specs=[pl.BlockSpec((tr, d), lambda i: (i, 0)), pl.BlockSpec((1, d), lambda i: (0, 0))],
        out_specs=pl.BlockSpec((tr, d), lambda i: (i, 0)),
        compiler_params=_cparams(("parallel",)),
    )(x, g.reshape(1, d))


def _rms_bwd(x, g, dh, dres, *, name):
    s, d = x.shape
    tr = min(512, s)

    def body(x_ref, g_ref, dh_ref, dres_ref, dx_ref, dxb_ref, dg_ref):
        i = pl.program_id(0)
        xv = x_ref[...]
        rs = lax.rsqrt(jnp.mean(xv * xv, axis=-1, keepdims=True) + NORM_EPS)
        n = xv * rs
        dhv = dh_ref[...].astype(F32)
        dn = dhv * g_ref[...]
        dx = dres_ref[...] + rs * (dn - n * jnp.mean(dn * n, axis=-1, keepdims=True))
        dx_ref[...] = dx
        dxb_ref[...] = dx.astype(BF16)
        part = jnp.sum(dhv * n, axis=0, keepdims=True)

        @pl.when(i == 0)
        def _():
            dg_ref[...] = part

        @pl.when(i > 0)
        def _():
            dg_ref[...] += part

    row = pl.BlockSpec((tr, d), lambda i: (i, 0))
    vec = pl.BlockSpec((1, d), lambda i: (0, 0))
    return pl.pallas_call(
        body, name=name,
        out_shape=(jax.ShapeDtypeStruct((s, d), F32), jax.ShapeDtypeStruct((s, d), BF16),
                   jax.ShapeDtypeStruct((1, d), F32)),
        grid=(s // tr,),
        in_specs=[row, vec, row, row],
        out_specs=(row, row, vec),
        compiler_params=_cparams(("arbitrary",)),
    )(x, g.reshape(1, d), dh, dres)


def _loss_head(x, g, target):
    s, d = x.shape
    tr = min(512, s)

    def body(x_ref, g_ref, t_ref, loss_ref, dx_ref, dxb_ref, dg_ref):
        i = pl.program_id(0)
        xv = x_ref[...]
        rs = lax.rsqrt(jnp.mean(xv * xv, axis=-1, keepdims=True) + NORM_EPS)
        n = xv * rs
        gv = g_ref[...]
        err = n * gv - t_ref[...]
        row_loss = jnp.mean(err * err, axis=-1, keepdims=True)
        loss_part = 0.5 * jnp.sum(row_loss, axis=0, keepdims=True)
        dy = err * (1.0 / d)
        dn = dy * gv
        dx = rs * (dn - n * jnp.mean(dn * n, axis=-1, keepdims=True))
        dx_ref[...] = dx
        dxb_ref[...] = dx.astype(BF16)
        dg_part = jnp.sum(dy * n, axis=0, keepdims=True)

        @pl.when(i == 0)
        def _():
            loss_ref[...] = loss_part
            dg_ref[...] = dg_part

        @pl.when(i > 0)
        def _():
            loss_ref[...] += loss_part
            dg_ref[...] += dg_part

    row = pl.BlockSpec((tr, d), lambda i: (i, 0))
    vec = pl.BlockSpec((1, d), lambda i: (0, 0))
    one = pl.BlockSpec((1, 1), lambda i: (0, 0))
    return pl.pallas_call(
        body, name="loss_head",
        out_shape=(jax.ShapeDtypeStruct((1, 1), F32), jax.ShapeDtypeStruct((s, d), F32),
                   jax.ShapeDtypeStruct((s, d), BF16), jax.ShapeDtypeStruct((1, d), F32)),
        grid=(s // tr,),
        in_specs=[row, vec, row],
        out_specs=(one, row, row, vec),
        compiler_params=_cparams(("arbitrary",)),
    )(x, g.reshape(1, d), target)


def _conv_taps(u_prev, u, w_ref, rows):
    ext = jnp.concatenate([u_prev, u], axis=0)
    u1 = pltpu.roll(ext, 1, 0)[HALO:HALO + rows]
    u2 = pltpu.roll(ext, 2, 0)[HALO:HALO + rows]
    conv = w_ref[0:1, :] * u2 + w_ref[1:2, :] * u1 + w_ref[2:3, :] * u + w_ref[3:4, :]
    return conv, u1, u2


def _conv_fwd(proj, conv_wb, *, s):
    tr = min(512, s)
    c = CONV_CH
    hb = tr // HALO

    def body(gbi_ref, gci_ref, cx_ref, gci_h_ref, cx_h_ref, w_ref, o_ref):
        i = pl.program_id(0)
        u = gci_ref[...].astype(F32) * cx_ref[...].astype(F32)
        u_prev = gci_h_ref[...].astype(F32) * cx_h_ref[...].astype(F32)
        u_prev = jnp.where(i == 0, 0.0, u_prev)
        conv, _, _ = _conv_taps(u_prev, u, w_ref, tr)
        o_ref[...] = (gbi_ref[...].astype(F32) * conv).astype(BF16)

    def seg(col):
        return pl.BlockSpec((tr, c), lambda i: (i, col // c))

    def halo(col):
        return pl.BlockSpec((HALO, c), lambda i: (jnp.maximum(i * hb - 1, 0), col // c))

    return pl.pallas_call(
        body, name="conv_fwd",
        out_shape=jax.ShapeDtypeStruct((s, c), BF16),
        grid=(s // tr,),
        in_specs=[seg(COL_GBI), seg(COL_GCI), seg(COL_CX), halo(COL_GCI), halo(COL_CX),
                  pl.BlockSpec((8, c), lambda i: (0, 0))],
        out_specs=pl.BlockSpec((tr, c), lambda i: (i, 0)),
        compiler_params=_cparams(("parallel",)),
    )(proj, proj, proj, proj, proj, conv_wb)


def _conv_bwd(proj, dcb, conv_wb, dproj, *, s):
    tr = min(512, s)
    c = CONV_CH
    hb = tr // HALO
    nb = s // tr

    def body(gbi_ref, gci_ref, cx_ref, dcb_ref, gci_h_ref, cx_h_ref, gbi_n_ref, dcb_n_ref, w_ref,
             dproj_in_ref, dp_ref, dw_ref):
        i = pl.program_id(0)
        gbi = gbi_ref[...].astype(F32)
        gci = gci_ref[...].astype(F32)
        cx = cx_ref[...].astype(F32)
        dcb_v = dcb_ref[...].astype(F32)
        u = gci * cx
        u_prev = jnp.where(i == 0, 0.0, gci_h_ref[...].astype(F32) * cx_h_ref[...].astype(F32))
        conv, u1, u2 = _conv_taps(u_prev, u, w_ref, tr)
        dconv = dcb_v * gbi
        dconv_next = jnp.where(i == nb - 1, 0.0, dcb_n_ref[...].astype(F32) * gbi_n_ref[...].astype(F32))
        ext = jnp.concatenate([dconv, dconv_next], axis=0)
        n_ext = tr + HALO
        d1 = pltpu.roll(ext, n_ext - 1, 0)[0:tr]
        d2 = pltpu.roll(ext, n_ext - 2, 0)[0:tr]
        du = w_ref[2:3, :] * dconv + w_ref[1:2, :] * d1 + w_ref[0:1, :] * d2
        dp_ref[:, 0:c] = (dcb_v * conv).astype(BF16)
        dp_ref[:, c:2 * c] = (du * cx).astype(BF16)
        dp_ref[:, 2 * c:3 * c] = (du * gci).astype(BF16)
        part = jnp.concatenate([
            jnp.sum(dconv * u2, axis=0, keepdims=True),
            jnp.sum(dconv * u1, axis=0, keepdims=True),
            jnp.sum(dconv * u, axis=0, keepdims=True),
            jnp.sum(dconv, axis=0, keepdims=True),
            jnp.zeros((4, c), F32)], axis=0)

        @pl.when(i == 0)
        def _():
            dw_ref[...] = part

        @pl.when(i > 0)
        def _():
            dw_ref[...] += part

    def seg(col):
        return pl.BlockSpec((tr, c), lambda i: (i, col // c))

    def halo_prev(col):
        return pl.BlockSpec((HALO, c), lambda i: (jnp.maximum(i * hb - 1, 0), col // c))

    def halo_next(col):
        return pl.BlockSpec((HALO, c), lambda i: (jnp.minimum((i + 1) * hb, nb * hb - 1), col // c))

    return pl.pallas_call(
        body, name="conv_bwd",
        out_shape=(jax.ShapeDtypeStruct((s, PROJ_W), BF16), jax.ShapeDtypeStruct((8, c), F32)),
        grid=(nb,),
        in_specs=[seg(COL_GBI), seg(COL_GCI), seg(COL_CX), pl.BlockSpec((tr, c), lambda i: (i, 0)),
                  halo_prev(COL_GCI), halo_prev(COL_CX), halo_next(COL_GBI),
                  pl.BlockSpec((HALO, c), lambda i: (jnp.minimum((i + 1) * hb, nb * hb - 1), 0)),
                  pl.BlockSpec((8, c), lambda i: (0, 0)), ANY],
        out_specs=(pl.BlockSpec((tr, 3 * c), lambda i: (i, COL_GBI // (3 * c))),
                   pl.BlockSpec((8, c), lambda i: (0, 0))),
        input_output_aliases={9: 0},
        compiler_params=_cparams(("arbitrary",)),
    )(proj, proj, proj, dcb, proj, proj, proj, dcb, conv_wb, dproj)


def _mix_fwd(proj, ya, yb, *, s):
    tr = min(512, s)
    d = D_MODEL

    def body(ga_ref, gb_ref, ya_ref, yb_ref, o_ref):
        sa = _sigmoid(ga_ref[...].astype(F32))
        sb = _sigmoid(gb_ref[...].astype(F32))
        o_ref[...] = (sa * ya_ref[...].astype(F32) + sb * yb_ref[...].astype(F32)).astype(BF16)

    row = pl.BlockSpec((tr, d), lambda i: (i, 0))
    return pl.pallas_call(
        body, name="mix_fwd",
        out_shape=jax.ShapeDtypeStruct((s, d), BF16),
        grid=(s // tr,),
        in_specs=[pl.BlockSpec((tr, d), lambda i: (i, COL_GA // d)),
                  pl.BlockSpec((tr, d), lambda i: (i, COL_GB // d)), row, row],
        out_specs=row,
        compiler_params=_cparams(("parallel",)),
    )(proj, proj, ya, yb)


def _mix_bwd(proj, ya, yb, dmix, *, s):
    tr = min(512, s)
    d = D_MODEL

    def body(ga_ref, gb_ref, ya_ref, yb_ref, dm_ref, dya_ref, dyb_ref, dg_ref):
        sa = _sigmoid(ga_ref[...].astype(F32))
        sb = _sigmoid(gb_ref[...].astype(F32))
        dm = dm_ref[...].astype(F32)
        dya_ref[...] = (dm * sa).astype(BF16)
        dyb_ref[...] = (dm * sb).astype(BF16)
        dg_ref[:, 0:d] = (dm * ya_ref[...].astype(F32) * sa * (1.0 - sa)).astype(BF16)
        dg_ref[:, d:2 * d] = (dm * yb_ref[...].astype(F32) * sb * (1.0 - sb)).astype(BF16)

    row = pl.BlockSpec((tr, d), lambda i: (i, 0))
    return pl.pallas_call(
        body, name="mix_bwd",
        out_shape=(jax.ShapeDtypeStruct((s, d), BF16), jax.ShapeDtypeStruct((s, d), BF16),
                   jax.ShapeDtypeStruct((s, PROJ_W), BF16)),
        grid=(s // tr,),
        in_specs=[pl.BlockSpec((tr, d), lambda i: (i, COL_GA // d)),
                  pl.BlockSpec((tr, d), lambda i: (i, COL_GB // d)), row, row, row],
        out_specs=(row, row, pl.BlockSpec((tr, 2 * d), lambda i: (i, COL_GA // (2 * d)))),
        compiler_params=_cparams(("parallel",)),
    )(proj, proj, ya, yb, dmix)


def _swiglu_fwd(gt, up, *, s):
    tr = min(256, s)
    f = FFN_HIDDEN

    def body(gt_ref, up_ref, o_ref):
        gv = gt_ref[...].astype(F32)
        o_ref[...] = (gv * _sigmoid(gv) * up_ref[...].astype(F32)).astype(BF16)

    row = pl.BlockSpec((tr, f), lambda i: (i, 0))
    return pl.pallas_call(
        body, name="swiglu_fwd",
        out_shape=jax.ShapeDtypeStruct((s, f), BF16),
        grid=(s // tr,),
        in_specs=[row, row], out_specs=row,
        compiler_params=_cparams(("parallel",)),
    )(gt, up)


def _swiglu_bwd(gt, up, dhid, *, s):
    tr = min(256, s)
    f = FFN_HIDDEN

    def body(gt_ref, up_ref, dh_ref, dgt_ref, dup_ref):
        gv = gt_ref[...].astype(F32)
        uv = up_ref[...].astype(F32)
        dh = dh_ref[...].astype(F32)
        sg = _sigmoid(gv)
        dgt_ref[...] = (dh * uv * sg * (1.0 + gv * (1.0 - sg))).astype(BF16)
        dup_ref[...] = (dh * gv * sg).astype(BF16)

    row = pl.BlockSpec((tr, f), lambda i: (i, 0))
    return pl.pallas_call(
        body, name="swiglu_bwd",
        out_shape=(jax.ShapeDtypeStruct((s, f), BF16), jax.ShapeDtypeStruct((s, f), BF16)),
        grid=(s // tr,),
        in_specs=[row, row, row], out_specs=(row, row),
        compiler_params=_cparams(("parallel",)),
    )(gt, up, dhid)


def _tri(strict):
    r = lax.broadcasted_iota(jnp.int32, (CHUNK, CHUNK), 0)
    c = lax.broadcasted_iota(jnp.int32, (CHUNK, CHUNK), 1)
    return jnp.where((c < r) if strict else (c <= r), 1.0, 0.0).astype(F32)


def _gla_gate_terms(la_c, tri):
    cum = jnp.dot(tri, la_c, precision=lax.Precision.HIGHEST, preferred_element_type=F32)
    cend = cum[CHUNK - 1:CHUNK, :]
    return jnp.exp(cend - cum), jnp.exp(cend)


def _hk(h):
    return slice(h * HEAD_K, (h + 1) * HEAD_K)


def _hv(h):
    return slice(h * HEAD_V, (h + 1) * HEAD_V)


def _gla_fwd(proj, wfg2p, bfg, gn, *, s):
    tr = min(512, s)
    nb = s // tr
    nc = tr // CHUNK

    def body(q_ref, k_ref, v_ref, r_ref, fz_ref, w_ref, b_ref, gn_ref, oa_ref, st_ref,
             state, la_scr, o_scr):
        i = pl.program_id(0)

        @pl.when(i == 0)
        def _():
            state[...] = jnp.zeros_like(state)

        fg = jnp.dot(fz_ref[...], w_ref[...], preferred_element_type=F32) + b_ref[...]
        la_scr[...] = _log_sigmoid(fg) * (1.0 / GATE_TAU)
        tri = _tri(False)

        def chunk(ci, carry):
            r0 = pl.multiple_of(ci * CHUNK, CHUNK)
            rows = pl.ds(r0, CHUNK)
            e, gam = _gla_gate_terms(la_scr[rows, :], tri)
            kd = (k_ref[rows, :].astype(F32) * e).astype(BF16)
            qs = (q_ref[rows, :].astype(F32) * Q_SCALE).astype(BF16)
            v_c = v_ref[rows, :]
            for h in range(GLA_HEADS):
                upd = lax.dot_general(v_c[:, _hv(h)], kd[:, _hk(h)], _DOT_DIMS["tn"],
                                      preferred_element_type=F32)
                st_h = state[:, _hk(h)] * gam[:, _hk(h)] + upd
                state[:, _hk(h)] = st_h
                o_scr[rows, _hv(h)] = lax.dot_general(qs[:, _hk(h)], st_h.astype(BF16), _DOT_DIMS["nt"],
                                                       preferred_element_type=F32)
            st_ref[ci] = state[...]
            return carry

        lax.fori_loop(0, nc, chunk, 0)
        for h in range(GLA_HEADS):
            o = o_scr[:, _hv(h)]
            rs = lax.rsqrt(jnp.mean(o * o, axis=-1, keepdims=True) + NORM_EPS)
            rv = r_ref[:, _hv(h)].astype(F32)
            oa_ref[:, _hv(h)] = ((o * rs * gn_ref[...]).astype(F32) * (rv * _sigmoid(rv))).astype(BF16)

    return pl.pallas_call(
        body, name="gla_fwd",
        out_shape=(jax.ShapeDtypeStruct((s, VAL_WIDTH), BF16),
                   jax.ShapeDtypeStruct((s // CHUNK, HEAD_V, KEY_WIDTH), F32)),
        grid=(nb,),
        in_specs=[pl.BlockSpec((tr, KEY_WIDTH), lambda i: (i, COL_Q // KEY_WIDTH)),
                  pl.BlockSpec((tr, KEY_WIDTH), lambda i: (i, COL_K // KEY_WIDTH)),
                  pl.BlockSpec((tr, VAL_WIDTH), lambda i: (i, COL_V // VAL_WIDTH)),
                  pl.BlockSpec((tr, VAL_WIDTH), lambda i: (i, COL_R // VAL_WIDTH)),
                  pl.BlockSpec((tr, FZ_PAD), lambda i: (i, COL_FZ // FZ_PAD)),
                  pl.BlockSpec((FZ_PAD, KEY_WIDTH), lambda i: (0, 0)),
                  pl.BlockSpec((1, KEY_WIDTH), lambda i: (0, 0)),
                  pl.BlockSpec((1, HEAD_V), lambda i: (0, 0))],
        out_specs=(pl.BlockSpec((tr, VAL_WIDTH), lambda i: (i, 0)),
                   pl.BlockSpec((nc, HEAD_V, KEY_WIDTH), lambda i: (i, 0, 0))),
        scratch_shapes=[pltpu.VMEM((HEAD_V, KEY_WIDTH), F32), pltpu.VMEM((tr, KEY_WIDTH), F32),
                        pltpu.VMEM((tr, VAL_WIDTH), F32)],
        compiler_params=_cparams(("arbitrary",)),
    )(proj, proj, proj, proj, proj, wfg2p, bfg, gn)


def _gla_bwd(proj, doa, states, wfg2p, bfg, gn, dproj, *, s):
    tr = min(512, s)
    nb = s // tr
    nc = tr // CHUNK

    def body(q_ref, k_ref, v_ref, r_ref, fz_ref, doa_ref, st_ref, stp_ref, w_ref, b_ref, gn_ref, dproj_in_ref,
             dp_ref, dfz_ref, dgn_ref, dw_ref, db_ref,
             carry, fg_scr, la_scr, dla_scr, o_scr, do_scr):
        i = pl.program_id(0)

        @pl.when(i == 0)
        def _():
            carry[...] = jnp.zeros_like(carry)
            dgn_ref[...] = jnp.zeros_like(dgn_ref)
            dw_ref[...] = jnp.zeros_like(dw_ref)
            db_ref[...] = jnp.zeros_like(db_ref)

        fz = fz_ref[...]
        fg = jnp.dot(fz, w_ref[...], preferred_element_type=F32) + b_ref[...]
        fg_scr[...] = fg
        la_scr[...] = _log_sigmoid(fg) * (1.0 / GATE_TAU)

        def fwd_chunk(ci, c0):
            r0 = pl.multiple_of(ci * CHUNK, CHUNK)
            rows = pl.ds(r0, CHUNK)
            qs = (q_ref[rows, :].astype(F32) * Q_SCALE).astype(BF16)
            st = st_ref[ci].astype(BF16)
            for h in range(GLA_HEADS):
                o_scr[rows, _hv(h)] = lax.dot_general(qs[:, _hk(h)], st[:, _hk(h)], _DOT_DIMS["nt"],
                                                       preferred_element_type=F32)
            return c0

        lax.fori_loop(0, nc, fwd_chunk, 0)

        gnv = gn_ref[...]
        dgn_part = jnp.zeros((1, HEAD_V), F32)
        for h in range(GLA_HEADS):
            o = o_scr[:, _hv(h)]
            rs = lax.rsqrt(jnp.mean(o * o, axis=-1, keepdims=True) + NORM_EPS)
            nrm = o * rs
            rv = r_ref[:, _hv(h)].astype(F32)
            sg = _sigmoid(rv)
            doa_h = doa_ref[:, _hv(h)].astype(F32)
            don = doa_h * (rv * sg)
            dp_ref[:, COL_R + h * HEAD_V:COL_R + (h + 1) * HEAD_V] = (
                doa_h * (nrm * gnv) * (sg * (1.0 + rv * (1.0 - sg)))).astype(BF16)
            dgn_part = dgn_part + jnp.sum(don * nrm, axis=0, keepdims=True)
            dn = don * gnv
            do_scr[:, _hv(h)] = (rs * (dn - nrm * jnp.mean(dn * nrm, axis=-1, keepdims=True))).astype(BF16)
        dgn_ref[...] += dgn_part

        tri = _tri(False)
        tri_s = _tri(True)
        first_block = i == nb - 1

        def bwd_chunk(cc, c0):
            ci = nc - 1 - cc
            r0 = pl.multiple_of(ci * CHUNK, CHUNK)
            rows = pl.ds(r0, CHUNK)
            e, gam = _gla_gate_terms(la_scr[rows, :], tri)
            k_c = k_ref[rows, :].astype(F32)
            kd = k_c * e
            kd_b = kd.astype(BF16)
            qs = (q_ref[rows, :].astype(F32) * Q_SCALE).astype(BF16)
            v_c = v_ref[rows, :]
            do_c = do_scr[rows, :]
            st = st_ref[ci]
            st_b = st.astype(BF16)
            st_prev_in = st_ref[jnp.maximum(ci - 1, 0)]
            st_prev_edge = jnp.where(first_block, 0.0, stp_ref[0])
            st_prev = jnp.where(ci > 0, st_prev_in, st_prev_edge)
            dkd_parts = []
            dgam_parts = []
            for h in range(GLA_HEADS):
                dst = lax.dot_general(do_c[:, _hv(h)], qs[:, _hk(h)], _DOT_DIMS["tn"],
                                      preferred_element_type=F32) + carry[:, _hk(h)]
                dst_b = dst.astype(BF16)
                dqs = jnp.dot(do_c[:, _hv(h)], st_b[:, _hk(h)], preferred_element_type=F32)
                dp_ref[rows, COL_Q + h * HEAD_K:COL_Q + (h + 1) * HEAD_K] = (dqs * Q_SCALE).astype(BF16)
                dkd_parts.append(jnp.dot(v_c[:, _hv(h)], dst_b, preferred_element_type=F32))
                dp_ref[rows, COL_V + h * HEAD_V:COL_V + (h + 1) * HEAD_V] = lax.dot_general(
                    kd_b[:, _hk(h)], dst_b, _DOT_DIMS["nt"], preferred_element_type=F32).astype(BF16)
                dgam_parts.append(jnp.sum(dst * st_prev[:, _hk(h)], axis=0, keepdims=True))
                carry[:, _hk(h)] = dst * gam[:, _hk(h)]
            dkd = jnp.concatenate(dkd_parts, axis=1)
            dgam = jnp.concatenate(dgam_parts, axis=1)
            dp_ref[rows, COL_K:COL_K + KEY_WIDTH] = (dkd * e).astype(BF16)
            dz = dkd * kd
            dla_scr[rows, :] = gam * dgam + jnp.dot(tri_s, dz, precision=lax.Precision.HIGHEST,
                                                    preferred_element_type=F32)
            return c0

        lax.fori_loop(0, nc, bwd_chunk, 0)

        dfg = dla_scr[...] * (1.0 / GATE_TAU) * _sigmoid(-fg_scr[...])
        dfg_b = dfg.astype(BF16)
        dfz_ref[...] = lax.dot_general(dfg_b, w_ref[...], _DOT_DIMS["nt"],
                                       preferred_element_type=F32).astype(BF16)
        dw_ref[...] += lax.dot_general(fz, dfg_b, _DOT_DIMS["tn"], preferred_element_type=F32)
        db_ref[...] += jnp.sum(dfg, axis=0, keepdims=True)

    def rev(i):
        return nb - 1 - i

    return pl.pallas_call(
        body, name="gla_bwd",
        out_shape=(jax.ShapeDtypeStruct((s, PROJ_W), BF16),
                   jax.ShapeDtypeStruct((s, FZ_PAD), BF16),
                   jax.ShapeDtypeStruct((1, HEAD_V), F32),
                   jax.ShapeDtypeStruct((FZ_PAD, KEY_WIDTH), F32),
                   jax.ShapeDtypeStruct((1, KEY_WIDTH), F32)),
        grid=(nb,),
        in_specs=[pl.BlockSpec((tr, KEY_WIDTH), lambda i: (rev(i), COL_Q // KEY_WIDTH)),
                  pl.BlockSpec((tr, KEY_WIDTH), lambda i: (rev(i), COL_K // KEY_WIDTH)),
                  pl.BlockSpec((tr, VAL_WIDTH), lambda i: (rev(i), COL_V // VAL_WIDTH)),
                  pl.BlockSpec((tr, VAL_WIDTH), lambda i: (rev(i), COL_R // VAL_WIDTH)),
                  pl.BlockSpec((tr, FZ_PAD), lambda i: (rev(i), COL_FZ // FZ_PAD)),
                  pl.BlockSpec((tr, VAL_WIDTH), lambda i: (rev(i), 0)),
                  pl.BlockSpec((nc, HEAD_V, KEY_WIDTH), lambda i: (rev(i), 0, 0)),
                  pl.BlockSpec((1, HEAD_V, KEY_WIDTH), lambda i: (jnp.maximum(rev(i) * nc - 1, 0), 0, 0)),
                  pl.BlockSpec((FZ_PAD, KEY_WIDTH), lambda i: (0, 0)),
                  pl.BlockSpec((1, KEY_WIDTH), lambda i: (0, 0)),
                  pl.BlockSpec((1, HEAD_V), lambda i: (0, 0)), ANY],
        out_specs=(pl.BlockSpec((tr, 3 * VAL_WIDTH), lambda i: (rev(i), 0)),
                   pl.BlockSpec((tr, FZ_PAD), lambda i: (rev(i), 0)),
                   pl.BlockSpec((1, HEAD_V), lambda i: (0, 0)),
                   pl.BlockSpec((FZ_PAD, KEY_WIDTH), lambda i: (0, 0)),
                   pl.BlockSpec((1, KEY_WIDTH), lambda i: (0, 0))),
        scratch_shapes=[pltpu.VMEM((HEAD_V, KEY_WIDTH), F32),
                        pltpu.VMEM((tr, KEY_WIDTH), F32),
                        pltpu.VMEM((tr, KEY_WIDTH), F32),
                        pltpu.VMEM((tr, KEY_WIDTH), F32),
                        pltpu.VMEM((tr, VAL_WIDTH), F32),
                        pltpu.VMEM((tr, VAL_WIDTH), BF16)],
        input_output_aliases={11: 0},
        compiler_params=_cparams(("arbitrary",)),
    )(proj, proj, proj, proj, proj, doa, states, states, wfg2p, bfg, gn, dproj)


def _put_fz(dproj, dfz, *, s):
    tr = min(512, s)

    def body(dfz_ref, dproj_in_ref, o_ref):
        o_ref[...] = dfz_ref[...]

    return pl.pallas_call(
        body, name="put_dfz",
        out_shape=jax.ShapeDtypeStruct((s, PROJ_W), BF16),
        grid=(s // tr,),
        in_specs=[pl.BlockSpec((tr, FZ_PAD), lambda i: (i, 0)), ANY],
        out_specs=pl.BlockSpec((tr, FZ_PAD), lambda i: (i, COL_FZ // FZ_PAD)),
        input_output_aliases={1: 0},
        compiler_params=_cparams(("parallel",)),
    )(dfz, dproj)


_ADAM_C1 = 1.0 / (1.0 - ADAM_B1 ** ADAM_STEP)
_ADAM_C2 = 1.0 / (1.0 - ADAM_B2 ** ADAM_STEP)


def _adamw_math(wv, gv, mv, vv):
    nm = ADAM_B1 * mv + (1.0 - ADAM_B1) * gv
    nv = ADAM_B2 * vv + (1.0 - ADAM_B2) * (gv * gv)
    delta = -ADAM_LR * ((nm * _ADAM_C1) / (jnp.sqrt(nv * _ADAM_C2) + ADAM_EPS) + ADAM_WD * wv)
    return delta, nm, nv


def _adamw(w, g, m, v, *, name):
    shape = w.shape
    if w.ndim == 1:
        w, g, m, v = (t.reshape(1, 1, -1) for t in (w, g, m, v))
    elif w.ndim == 2:
        w, g, m, v = (t.reshape((1,) + t.shape) for t in (w, g, m, v))
    l, a, b = w.shape

    def body(w_ref, g_ref, m_ref, v_ref, d_ref, nm_ref, nv_ref):
        d_ref[...], nm_ref[...], nv_ref[...] = _adamw_math(w_ref[...], g_ref[...], m_ref[...], v_ref[...])

    blk = pl.BlockSpec((1, a, b), lambda li: (li, 0, 0))
    outs = pl.pallas_call(
        body, name=name,
        out_shape=tuple(jax.ShapeDtypeStruct((l, a, b), F32) for _ in range(3)),
        grid=(l,),
        in_specs=[blk, blk, blk, blk],
        out_specs=(blk, blk, blk),
        compiler_params=_cparams(("parallel",)),
    )(w, g, m, v)
    return tuple(o.reshape(shape) for o in outs)


def _adamw_layers(w, g_mine, g_other, mine_is_layer0, m, v, *, name):
    l, a, b = w.shape
    ta = _row_tile(a, F32_SUBLANES, 384)

    def body(flag_ref, w_ref, gm_ref, go_ref, m_ref, v_ref, g_ref, d_ref, nm_ref, nv_ref):
        layer0 = jnp.where(pl.program_id(0) == 0, 1.0, 0.0)
        f = flag_ref[...]
        take_mine = f * layer0 + (1.0 - f) * (1.0 - layer0)
        gv = jnp.where(take_mine > 0.5, gm_ref[...], go_ref[...])
        g_ref[0] = gv
        d_ref[0], nm_ref[0], nv_ref[0] = _adamw_math(w_ref[0], gv, m_ref[0], v_ref[0])

    blk = pl.BlockSpec((1, ta, b), lambda li, ai: (li, ai, 0))
    gblk = pl.BlockSpec((ta, b), lambda li, ai: (ai, 0))
    return pl.pallas_call(
        body, name=name,
        out_shape=tuple(jax.ShapeDtypeStruct((l, a, b), F32) for _ in range(4)),
        grid=(l, a // ta),
        in_specs=[pl.BlockSpec((1, 1), lambda li, ai: (0, 0)), blk, gblk, gblk, blk, blk],
        out_specs=(blk, blk, blk, blk),
        compiler_params=_cparams(("parallel", "parallel")),
    )(mine_is_layer0, w, g_mine, g_other, m, v)


MESH_ID = pl.DeviceIdType.MESH
ANY = pl.BlockSpec(memory_space=pl.ANY)


def _position():
    x, y, c = lax.axis_index("x"), lax.axis_index("y"), lax.axis_index("c")
    chips = [(1 - x, y), (x, 1 - y), (1 - x, 1 - y)]
    return x, y, c, chips


def _chip_index(xy):
    return 2 * xy[0] + xy[1]


def _all_gather_weights(shards):
    n = len(shards)

    def body(*refs):
        ins, outs = refs[:n], refs[n:2 * n]
        send_sems, recv_sems, pass_send, pass_recv = refs[2 * n:]
        x, y, c, chips = _position()
        sibling = (x, y, 1 - c)

        def ici(a, k, chip_from):
            return pltpu.make_async_remote_copy(
                src_ref=ins[a].at[c], dst_ref=outs[a].at[c, _chip_index(chip_from)],
                send_sem=send_sems.at[a, k], recv_sem=recv_sems.at[a, k],
                device_id=(chips[k][0], chips[k][1], c), device_id_type=MESH_ID)

        def handoff(a, k, layer):
            slot = outs[a].at[layer, _chip_index(chips[k])]
            return pltpu.make_async_remote_copy(
                src_ref=slot, dst_ref=slot, send_sem=pass_send.at[a, k], recv_sem=pass_recv.at[a, k],
                device_id=sibling, device_id_type=MESH_ID)

        for a in range(n):
            for k in range(3):
                ici(a, k, (x, y)).start()
        for k in range(3):
            for a in range(n):
                ici(a, k, chips[k]).wait_recv()
                handoff(a, k, c).start()
        for k in range(3):
            for a in range(n):
                handoff(a, k, 1 - c).wait_recv()
        for a in range(n):
            for k in range(3):
                ici(a, k, (x, y)).wait_send()
                handoff(a, k, c).wait_send()

    gathered = pl.pallas_call(
        body, name="all_gather_weights",
        out_shape=tuple(jax.ShapeDtypeStruct((DEPTH, N_CHIPS) + t.shape[1:], t.dtype) for t in shards),
        in_specs=[ANY] * n, out_specs=tuple([ANY] * n),
        scratch_shapes=[pltpu.SemaphoreType.DMA((n, 3)), pltpu.SemaphoreType.DMA((n, 3)),
                        pltpu.SemaphoreType.DMA((n, 3)), pltpu.SemaphoreType.DMA((n, 3))],
    )(*shards)
    me = _chip_index((lax.axis_index("x"), lax.axis_index("y")))
    return [lax.dynamic_update_index_in_dim(g, t[:, None], me, axis=1) for g, t in zip(gathered, shards)]


def _sibling_exchange(layer0, layer1):
    n = len(layer0)

    def body(*refs):
        l0, l1, sibs = refs[:n], refs[n:2 * n], refs[2 * n:3 * n]
        send_sems, recv_sems = refs[3 * n:]
        x, y, c, _ = _position()

        def copy(src, a):
            return pltpu.make_async_remote_copy(
                src_ref=src[a], dst_ref=sibs[a], send_sem=send_sems.at[a], recv_sem=recv_sems.at[a],
                device_id=(x, y, 1 - c), device_id_type=MESH_ID)

        @pl.when(c == 0)
        def _():
            for a in range(n):
                copy(l1, a).start()

        @pl.when(c == 1)
        def _():
            for a in range(n):
                copy(l0, a).start()

        for a in range(n):
            copy(l0, a).wait()

    return pl.pallas_call(
        body, name="grad_sibling_exchange",
        out_shape=tuple(jax.ShapeDtypeStruct(t.shape, t.dtype) for t in layer0),
        in_specs=[ANY] * (2 * n), out_specs=tuple([ANY] * n),
        scratch_shapes=[pltpu.SemaphoreType.DMA((n,)), pltpu.SemaphoreType.DMA((n,))],
    )(*layer0, *layer1)


def _chip_exchange(parts):
    n = len(parts)

    def body(*refs):
        ins, outs = refs[:n], refs[n:2 * n]
        send_sems, recv_sems = refs[2 * n:]
        x, y, c, chips = _position()
        copies = [pltpu.make_async_remote_copy(
            src_ref=ins[a].at[_chip_index(chips[k])], dst_ref=outs[a].at[k],
            send_sem=send_sems.at[a, k], recv_sem=recv_sems.at[a, k],
            device_id=(chips[k][0], chips[k][1], c), device_id_type=MESH_ID)
            for a in range(n) for k in range(3)]
        for cp in copies:
            cp.start()
        for cp in copies:
            cp.wait()

    return pl.pallas_call(
        body, name="grad_chip_exchange",
        out_shape=tuple(jax.ShapeDtypeStruct((3,) + t.shape[1:], t.dtype) for t in parts),
        in_specs=[ANY] * n, out_specs=tuple([ANY] * n),
        scratch_shapes=[pltpu.SemaphoreType.DMA((n, 3)), pltpu.SemaphoreType.DMA((n, 3))],
    )(*parts)


def _share_with_sibling(totals):
    n = len(totals)

    def body(*refs):
        ins, outs = refs[:n], refs[n:2 * n]
        send_sems, recv_sems = refs[2 * n:]
        x, y, c, _ = _position()
        remote = [pltpu.make_async_remote_copy(
            src_ref=ins[a], dst_ref=outs[a], send_sem=send_sems.at[a], recv_sem=recv_sems.at[a],
            device_id=(x, y, 1 - c), device_id_type=MESH_ID) for a in range(n)]
        for cp in remote:
            cp.start()
        for cp in remote:
            cp.wait()

    return pl.pallas_call(
        body, name="grad_share_with_sibling",
        out_shape=tuple(jax.ShapeDtypeStruct(t.shape, t.dtype) for t in totals),
        in_specs=[ANY] * n, out_specs=tuple([ANY] * n),
        scratch_shapes=[pltpu.SemaphoreType.DMA((n,)), pltpu.SemaphoreType.DMA((n,))],
    )(*totals)


F32_SUBLANES = 8
BF16_SUBLANES = 16


def _row_tile(a, sublanes=BF16_SUBLANES, max_rows=704):
    best = None
    for cand in range(sublanes, min(a, max_rows) + 1, sublanes):
        if a % cand == 0:
            best = cand
    assert best is not None, a
    return best


def _pair_sum(layer0, layer1, sib, mine_is_layer0, *, name):
    nchip, a, b = sib.shape
    ta = _row_tile(a)

    def body(flag_ref, l0_ref, l1_ref, s_ref, o_ref):
        mine = jnp.where(flag_ref[...] > 0.5, l0_ref[...], l1_ref[...])
        o_ref[...] = (mine + s_ref[...]).astype(BF16)

    blk = pl.BlockSpec((1, ta, b), lambda j, r: (j, r, 0))
    return pl.pallas_call(
        body, name=name,
        out_shape=jax.ShapeDtypeStruct((nchip, a, b), BF16),
        grid=(nchip, a // ta),
        in_specs=[pl.BlockSpec((1, 1, 1), lambda j, r: (0, 0, 0)), blk, blk, blk], out_specs=blk,
        compiler_params=_cparams(("parallel", "parallel")),
    )(mine_is_layer0.reshape(1, 1, 1), layer0, layer1, sib)


def _total_sum(own, recv, *, name):
    a, b = own.shape
    ta = _row_tile(a)

    def body(o_ref, r_ref, t_ref):
        acc = o_ref[...].astype(F32)
        for k in range(3):
            acc = acc + r_ref[k].astype(F32)
        t_ref[...] = acc

    blk = pl.BlockSpec((ta, b), lambda r: (r, 0))
    return pl.pallas_call(
        body, name=name,
        out_shape=jax.ShapeDtypeStruct((a, b), F32),
        grid=(a // ta,),
        in_specs=[blk, pl.BlockSpec((3, ta, b), lambda r: (0, r, 0))], out_specs=blk,
        compiler_params=_cparams(("parallel",)),
    )(own, recv)


SMALL_ROWS = 32


def _all_reduce_small(packed):
    rows, width = packed.shape

    def body(x_ref, out_ref, gath, send_sems, recv_sems, local_sem):
        x, y, c, chips = _position()
        me, sibling = (x, y, c), (x, y, 1 - c)

        def slot(px, py, pc):
            return gath.at[4 * px + 2 * py + pc]

        def copy(k, block, to, src=None):
            return pltpu.make_async_remote_copy(
                src_ref=slot(*block) if src is None else src, dst_ref=slot(*block),
                send_sem=send_sems.at[k], recv_sem=recv_sems.at[k], device_id=to, device_id_type=MESH_ID)

        mine = pltpu.make_async_copy(x_ref, slot(*me), local_sem)
        mine.start()
        first = [copy(0, me, sibling, src=x_ref)]
        first += [copy(1 + j, me, (*chip, c), src=x_ref) for j, chip in enumerate(chips)]
        for cp in first:
            cp.start()
        passed = [copy(4 + j, (*chip, c), sibling) for j, chip in enumerate(chips)]
        for j, chip in enumerate(chips):
            copy(1 + j, (*chip, c), me).wait_recv()
            passed[j].start()
        copy(0, sibling, me).wait_recv()
        for j, chip in enumerate(chips):
            copy(4 + j, (*chip, 1 - c), me).wait_recv()
        for cp in first + passed:
            cp.wait_send()
        mine.wait()
        acc = gath[0]
        for d in range(1, N_DEV):
            acc = acc + gath[d]
        out_ref[...] = acc

    return pl.pallas_call(
        body, name="all_reduce_small",
        out_shape=jax.ShapeDtypeStruct((rows, width), F32),
        in_specs=[pl.BlockSpec(memory_space=pltpu.VMEM)],
        out_specs=pl.BlockSpec(memory_space=pltpu.VMEM),
        scratch_shapes=[pltpu.VMEM((N_DEV, rows, width), F32), pltpu.SemaphoreType.DMA((7,)),
                        pltpu.SemaphoreType.DMA((7,)), pltpu.SemaphoreType.DMA],
    )(packed)


def _layer_forward(x0, w, s):
    h = _rms_fwd(x0, w["norm1_g"], name="rms1_fwd")
    proj = _matmul(h, w["w_in_t"], mode="nt", out_dtype=BF16, tm=1024, tn=1664, tk=1024, name="proj_fwd")
    oa, states = _gla_fwd(proj, w["w_fg2"], w["b_fg"], w["gla_norm_g"], s=s)
    cb_in = _conv_fwd(proj, w["conv_wb"], s=s)
    ya = _matmul(oa, w["w_oa"], mode="nn", out_dtype=BF16, tm=1024, tn=1024, tk=1024, name="ya_fwd")
    yb = _matmul(cb_in, w["w_ob"], mode="nn", out_dtype=BF16, tm=1024, tn=1024, tk=1024, name="yb_fwd")
    mix = _mix_fwd(proj, ya, yb, s=s)
    x1 = _matmul(mix, w["w_o"], mode="nn", out_dtype=F32, tm=1024, tn=1024, tk=1024, name="wo_fwd", add=x0)
    h2 = _rms_fwd(x1, w["norm2_g"], name="rms2_fwd")
    gt = _matmul(h2, w["w_gate_t"], mode="nt", out_dtype=BF16, tm=1024, tn=1408, tk=1024, name="ffn_gate_fwd")
    up = _matmul(h2, w["w_up_t"], mode="nt", out_dtype=BF16, tm=1024, tn=1408, tk=1024, name="ffn_up_fwd")
    hid = _swiglu_fwd(gt, up, s=s)
    x2 = _matmul(hid, w["w_ffn_down"], mode="nn", out_dtype=F32, tm=1024, tn=1024, tk=FFN_HIDDEN,
                 name="ffn_out_fwd", add=x1)
    saved = dict(x0=x0, h=h, proj=proj, oa=oa, states=states, cb_in=cb_in, ya=ya, yb=yb, mix=mix, x1=x1,
                 h2=h2, gt=gt, up=up, hid=hid)
    return x2, saved


def _layer_backward(dx2, dx2b, w, sv, s):
    g = {}
    dhid = _matmul(dx2b, w["w_ffn_down"], mode="nt", out_dtype=BF16, tm=1024, tn=FFN_HIDDEN, tk=1024,
                   name="ffn_out_bwd")
    g["w_ffn_down"] = _matmul(sv["hid"], dx2b, mode="tn", out_dtype=F32, tm=1408, tn=1024, tk=512,
                              name="ffn_out_wgrad")
    dgt, dup = _swiglu_bwd(sv["gt"], sv["up"], dhid, s=s)
    g["w_gate_t"] = _matmul(dgt, sv["h2"], mode="tn", out_dtype=F32, tm=1408, tn=1024, tk=512,
                            name="ffn_gate_wgrad")
    g["w_up_t"] = _matmul(dup, sv["h2"], mode="tn", out_dtype=F32, tm=1408, tn=1024, tk=512,
                          name="ffn_up_wgrad")
    dh2 = _matmul(dgt, w["w_gate_t"], mode="nn", out_dtype=F32, tm=1024, tn=1024, tk=FFN_HIDDEN,
                  name="ffn_gate_bwd")
    dh2 = _matmul(dup, w["w_up_t"], mode="nn", out_dtype=F32, tm=1024, tn=1024, tk=FFN_HIDDEN,
                  name="ffn_up_bwd", add=dh2)
    dx1, dx1b, g["norm2_g"] = _rms_bwd(sv["x1"], w["norm2_g"], dh2, dx2, name="rms2_bwd")

    dmix = _matmul(dx1b, w["w_o"], mode="nt", out_dtype=BF16, tm=1024, tn=1024, tk=1024, name="wo_bwd")
    g["w_o"] = _matmul(sv["mix"], dx1b, mode="tn", out_dtype=F32, tm=1024, tn=1024, tk=512, name="wo_wgrad")
    dya, dyb, dproj = _mix_bwd(sv["proj"], sv["ya"], sv["yb"], dmix, s=s)
    dcb = _matmul(dyb, w["w_ob"], mode="nt", out_dtype=BF16, tm=1024, tn=1024, tk=1024, name="yb_bwd")
    g["w_ob"] = _matmul(sv["cb_in"], dyb, mode="tn", out_dtype=F32, tm=1024, tn=1024, tk=512, name="yb_wgrad")
    doa = _matmul(dya, w["w_oa"], mode="nt", out_dtype=BF16, tm=1024, tn=1024, tk=1024, name="ya_bwd")
    g["w_oa"] = _matmul(sv["oa"], dya, mode="tn", out_dtype=F32, tm=1024, tn=1024, tk=512, name="ya_wgrad")
    dproj, g["conv_wb"] = _conv_bwd(sv["proj"], dcb, w["conv_wb"], dproj, s=s)
    dproj, dfz, g["gla_norm_g"], g["w_fg2"], g["b_fg"] = _gla_bwd(
        sv["proj"], doa, sv["states"], w["w_fg2"], w["b_fg"], w["gla_norm_g"], dproj, s=s)
    dproj = _put_fz(dproj, dfz, s=s)
    g["w_in_t"] = _matmul(dproj, sv["h"], mode="tn", out_dtype=F32, tm=1664, tn=1024, tk=512,
                          name="proj_wgrad")
    dh = _matmul(dproj, w["w_in_t"], mode="nn", out_dtype=F32, tm=1024, tn=1024, tk=1664, name="proj_bwd")
    dx0, dx0b, g["norm1_g"] = _rms_bwd(sv["x0"], w["norm1_g"], dh, dx1, name="rms1_bwd")
    return dx0, dx0b, g


def _local_step(x, target, layers, final_g):
    s = x.shape[0]
    saved = []
    for w in layers:
        x, sv = _layer_forward(x, w, s)
        saved.append(sv)
    loss, dx, dxb, dgf = _loss_head(x, final_g, target)
    grads = [None] * DEPTH
    for l in reversed(range(DEPTH)):
        dx, dxb, grads[l] = _layer_backward(dx, dxb, layers[l], saved[l], s)
    return loss, dx, grads, dgf


def _cols_from_chips(t):
    return jnp.transpose(t, (1, 0, 2)).reshape(t.shape[1], -1)


W_IN_ROWS = IN_WIDTH // N_CHIPS
W_IN_ROWS_PAD = -(-W_IN_ROWS // BF16_SUBLANES) * BF16_SUBLANES


def _w_in_t_shard(t):
    return jnp.pad(jnp.transpose(t, (0, 2, 1)), ((0, 0), (0, W_IN_ROWS_PAD - W_IN_ROWS), (0, 0)))


def _w_in_t_unshard(t):
    return jnp.transpose(t[:, :W_IN_ROWS], (0, 2, 1))


def _w_in_t_to_kernel(t):
    full = t[:, :W_IN_ROWS].reshape(IN_WIDTH, D_MODEL)
    pad = jnp.zeros((FZ_PAD - GATE_RANK, D_MODEL), t.dtype)
    return jnp.concatenate([full[:FZ_ORIG], full[FZ_ORIG + GATE_RANK:], full[FZ_ORIG:FZ_ORIG + GATE_RANK], pad],
                           axis=0)


def _w_in_t_from_kernel(g):
    full = jnp.concatenate([g[:FZ_ORIG], g[COL_FZ:COL_FZ + GATE_RANK], g[FZ_ORIG:COL_FZ]], axis=0)
    return jnp.pad(full.reshape(N_CHIPS, W_IN_ROWS, D_MODEL), ((0, 0), (0, W_IN_ROWS_PAD - W_IN_ROWS), (0, 0)))


def kernel(x, norm1_g, w_in, w_fg2, b_fg, gla_norm_g, w_oa, conv_w, conv_b, w_ob, w_o, norm2_g, w_ffn_gate, w_ffn_up, w_ffn_down, final_g, loss_target, m_norm1_g, m_w_in, m_w_fg2, m_b_fg, m_gla_norm_g, m_w_oa, m_conv_w, m_conv_b, m_w_ob, m_w_o, m_norm2_g, m_w_ffn_gate, m_w_ffn_up, m_w_ffn_down, m_final_g, v_norm1_g, v_w_in, v_w_fg2, v_b_fg, v_gla_norm_g, v_w_oa, v_conv_w, v_conv_b, v_w_ob, v_w_o, v_norm2_g, v_w_ffn_gate, v_w_ffn_up, v_w_ffn_down, v_final_g):
    cx_ = lax.axis_index("x")
    cy_ = lax.axis_index("y")
    cc_ = lax.axis_index("c")
    me = 2 * cx_ + cy_
    mine_is_layer0 = jnp.where(cc_ == 0, 1.0, 0.0).astype(F32).reshape(1, 1)

    def swap(t):
        return jnp.swapaxes(t, 1, 2)

    big_names = ["w_in", "w_oa", "w_ob", "w_o", "w_ffn_gate", "w_ffn_up", "w_ffn_down"]
    views = dict(
        w_in=tuple(_w_in_t_shard(t) for t in (w_in, m_w_in, v_w_in)),
        w_oa=(w_oa, m_w_oa, v_w_oa), w_ob=(w_ob, m_w_ob, v_w_ob), w_o=(w_o, m_w_o, v_w_o),
        w_ffn_gate=tuple(swap(t) for t in (w_ffn_gate, m_w_ffn_gate, v_w_ffn_gate)),
        w_ffn_up=tuple(swap(t) for t in (w_ffn_up, m_w_ffn_up, v_w_ffn_up)),
        w_ffn_down=(w_ffn_down, m_w_ffn_down, v_w_ffn_down))
    from_view = dict(w_in=_w_in_t_unshard, w_ffn_gate=swap, w_ffn_up=swap)

    conv_w_p = jnp.pad(conv_w, ((0, 0), (0, 8 - conv_w.shape[1]), (0, 0)))
    gathered = _all_gather_weights([views[n][0].astype(BF16) for n in big_names] + [w_fg2, conv_w_p])
    gw = dict(zip(big_names + ["w_fg2", "conv_w"], gathered))

    layers = []
    for l in range(DEPTH):
        w_fg2_full = _cols_from_chips(gw["w_fg2"][l])
        conv_w_full = _cols_from_chips(gw["conv_w"][l])[:3]
        layers.append(dict(
            norm1_g=norm1_g[l], norm2_g=norm2_g[l],
            w_in_t=_w_in_t_to_kernel(gw["w_in"][l]),
            w_fg2=jnp.pad(w_fg2_full, ((0, FZ_PAD - GATE_RANK), (0, 0))).astype(BF16),
            b_fg=b_fg[l].reshape(1, KEY_WIDTH), gla_norm_g=gla_norm_g[l].reshape(1, HEAD_V),
            w_oa=gw["w_oa"][l].reshape(VAL_WIDTH, D_MODEL),
            w_ob=gw["w_ob"][l].reshape(CONV_CH, D_MODEL),
            w_o=gw["w_o"][l].reshape(D_MODEL, D_MODEL),
            conv_wb=jnp.concatenate([conv_w_full, conv_b[l].reshape(1, CONV_CH), jnp.zeros((4, CONV_CH), F32)],
                                    axis=0),
            w_gate_t=gw["w_ffn_gate"][l].reshape(FFN_HIDDEN, D_MODEL),
            w_up_t=gw["w_ffn_up"][l].reshape(FFN_HIDDEN, D_MODEL),
            w_ffn_down=gw["w_ffn_down"][l].reshape(FFN_HIDDEN, D_MODEL),
        ))

    loss_local, grad_x, grads, dgf = _local_step(x[0], loss_target[0], layers, final_g)
    loss = lax.psum(loss_local[0, 0], ("x", "y", "c"))

    grad_key = dict(w_in="w_in_t", w_ffn_gate="w_gate_t", w_ffn_up="w_up_t")

    def per_chip(name, l):
        t = grads[l][grad_key.get(name, name)]
        if name == "w_in":
            return _w_in_t_from_kernel(t)
        return t.reshape(N_CHIPS, t.shape[0] // N_CHIPS, t.shape[1])

    layer0 = [per_chip(n, 0) for n in big_names]
    layer1 = [per_chip(n, 1) for n in big_names]
    sibs = _sibling_exchange(layer0, layer1)
    parts = [_pair_sum(a0, a1, sb, mine_is_layer0, name="grad_pair_sum_" + n)
             for n, a0, a1, sb in zip(big_names, layer0, layer1, sibs)]
    recvs = _chip_exchange(parts)
    owns = [lax.dynamic_index_in_dim(p, me, 0, keepdims=False) for p in parts]
    totals = [_total_sum(o, rc, name="grad_total_" + n) for n, o, rc in zip(big_names, owns, recvs)]
    others = _share_with_sibling(totals)

    def small_rows(t):
        return t.reshape(-1, D_MODEL)

    g0, g1 = grads
    pieces = [
        jnp.concatenate([g0["norm1_g"], g1["norm1_g"]], axis=0),
        jnp.concatenate([g0["norm2_g"], g1["norm2_g"]], axis=0),
        dgf,
        small_rows(jnp.concatenate([g0["b_fg"], g1["b_fg"]], axis=1)),
        small_rows(jnp.concatenate([g0["gla_norm_g"], g1["gla_norm_g"],
                                    jnp.zeros((1, D_MODEL - 2 * HEAD_V), F32)], axis=1)),
        jnp.concatenate([g0["conv_wb"][3:4], g1["conv_wb"][3:4]], axis=0),
        jnp.concatenate([g0["conv_wb"][:3], g1["conv_wb"][:3]], axis=0),
        small_rows(jnp.stack([g0["w_fg2"][:GATE_RANK], g1["w_fg2"][:GATE_RANK]])),
        jnp.zeros((1, D_MODEL), F32),
    ]
    small = _all_reduce_small(jnp.concatenate(pieces, axis=0))
    sg = dict(
        norm1_g=small[0:2], norm2_g=small[2:4], final_g=small[4],
        b_fg=small[5].reshape(DEPTH, KEY_WIDTH), gla_norm_g=small[6, :DEPTH * HEAD_V].reshape(DEPTH, HEAD_V),
        conv_b=small[7:9],
        conv_w=lax.dynamic_slice_in_dim(small[9:15].reshape(DEPTH, 3, CONV_CH), me * (CONV_CH // N_CHIPS),
                                        CONV_CH // N_CHIPS, axis=2),
        w_fg2=lax.dynamic_slice_in_dim(small[15:31].reshape(DEPTH, GATE_RANK, KEY_WIDTH),
                                       me * (KEY_WIDTH // N_CHIPS), KEY_WIDTH // N_CHIPS, axis=2),
    )

    small_params = dict(norm1_g=(norm1_g, m_norm1_g, v_norm1_g), w_fg2=(w_fg2, m_w_fg2, v_w_fg2),
                        b_fg=(b_fg, m_b_fg, v_b_fg), gla_norm_g=(gla_norm_g, m_gla_norm_g, v_gla_norm_g),
                        conv_w=(conv_w, m_conv_w, v_conv_w), conv_b=(conv_b, m_conv_b, v_conv_b),
                        norm2_g=(norm2_g, m_norm2_g, v_norm2_g), final_g=(final_g, m_final_g, v_final_g))
    order = ["norm1_g", "w_in", "w_fg2", "b_fg", "gla_norm_g", "w_oa", "conv_w", "conv_b", "w_ob", "w_o",
             "norm2_g", "w_ffn_gate", "w_ffn_up", "w_ffn_down", "final_g"]
    results = {}
    for n, t, o in zip(big_names, totals, others):
        w_, m_, v_ = views[n]
        outs = _adamw_layers(w_, t, o, mine_is_layer0, m_, v_, name="adamw_" + n)
        back = from_view.get(n)
        results[n] = tuple(back(r) for r in outs) if back else outs
    for n, (w_, m_, v_) in small_params.items():
        results[n] = (sg[n],) + _adamw(w_, sg[n], m_, v_, name="adamw_" + n)
    return (loss, grad_x[None], *[results[n][0] for n in order], *[results[n][1] for n in order],
            *[results[n][2] for n in order], *[results[n][3] for n in order])
```

```python
import functools

import jax
import jax.numpy as jnp
from jax import lax
from jax.experimental import pallas as pl
from jax.experimental.pallas import tpu as pltpu

F32 = jnp.float32
BF16 = jnp.bfloat16

D_MODEL = 1024
DEPTH = 2
CHUNK = 64
GLA_HEADS = 4
KEY_WIDTH = 512
VAL_WIDTH = 1024
HEAD_K = 128
HEAD_V = 256
GATE_RANK = 16
GATE_TAU = 16.0
CONV_CH = 1024
FFN_HIDDEN = 2816
IN_WIDTH = 8208
NORM_EPS = 1e-6
Q_SCALE = HEAD_K ** -0.5
ADAM_LR = 0.001
ADAM_B1 = 0.9
ADAM_B2 = 0.999
ADAM_EPS = 1e-08
ADAM_WD = 0.01
ADAM_STEP = 10

N_CHIPS = 4
N_DEV = 8

LANE = 128
FZ_PAD = LANE
PROJ_W = 8192 + FZ_PAD
COL_Q, COL_K, COL_V, COL_R, COL_GBI, COL_GCI, COL_CX, COL_GA, COL_GB, COL_FZ = (
    0, 512, 1024, 2048, 3072, 4096, 5120, 6144, 7168, 8192)
FZ_ORIG = 3072

VMEM_LIMIT = 52 * 1024 * 1024
HALO = 16


def _cparams(sem=None):
    return pltpu.CompilerParams(dimension_semantics=sem, vmem_limit_bytes=VMEM_LIMIT)


def _sigmoid(x):
    return jax.nn.sigmoid(x)


def _log_sigmoid(x):
    return jnp.minimum(x, 0.0) - jnp.log1p(jnp.exp(-jnp.abs(x)))


_DOT_DIMS = {
    "nn": (((1,), (0,)), ((), ())),
    "nt": (((1,), (1,)), ((), ())),
    "tn": (((0,), (0,)), ((), ())),
}


def _matmul(a, b, *, mode, out_dtype, tm, tn, tk, name, add=None, dep=None):
    if mode == "nn":
        (m, k), n = a.shape, b.shape[1]
    elif mode == "nt":
        (m, k), n = a.shape, b.shape[0]
    else:
        (k, m), n = a.shape, b.shape[1]
    tm, tn, tk = min(tm, m), min(tn, n), min(tk, k)
    assert m % tm == 0 and n % tn == 0 and k % tk == 0, (name, m, n, k, tm, tn, tk)
    nk = k // tk
    has_add = add is not None
    has_dep = dep is not None

    def body(*refs):
        if has_dep:
            refs = refs[1:]
        if has_add:
            a_ref, b_ref, add_ref, o_ref = refs[:4]
            scratch = refs[4:]
        else:
            a_ref, b_ref, o_ref = refs[:3]
            add_ref = None
            scratch = refs[3:]
        part = lax.dot_general(a_ref[...], b_ref[...], _DOT_DIMS[mode], preferred_element_type=F32)

        def finish(acc):
            if add_ref is not None:
                acc = acc + add_ref[...].astype(F32)
            o_ref[...] = acc.astype(o_ref.dtype)

        if nk == 1:
            finish(part)
        else:
            acc_ref = scratch[0]
            kk = pl.program_id(2)

            @pl.when(kk == 0)
            def _():
                acc_ref[...] = part

            @pl.when(kk > 0)
            def _():
                acc_ref[...] += part

            @pl.when(kk == nk - 1)
            def _():
                finish(acc_ref[...])

    if mode == "nn":
        a_spec = pl.BlockSpec((tm, tk), lambda i, j, kk: (i, kk))
        b_spec = pl.BlockSpec((tk, tn), lambda i, j, kk: (kk, j))
    elif mode == "nt":
        a_spec = pl.BlockSpec((tm, tk), lambda i, j, kk: (i, kk))
        b_spec = pl.BlockSpec((tn, tk), lambda i, j, kk: (j, kk))
    else:
        a_spec = pl.BlockSpec((tk, tm), lambda i, j, kk: (kk, i))
        b_spec = pl.BlockSpec((tk, tn), lambda i, j, kk: (kk, j))
    o_spec = pl.BlockSpec((tm, tn), lambda i, j, kk: (i, j))
    in_specs = [a_spec, b_spec] + ([o_spec] if has_add else [])
    operands = (a, b) + ((add,) if has_add else ())
    if has_dep:
        in_specs = [pl.BlockSpec(dep.shape, lambda i, j, kk: (0, 0))] + in_specs
        operands = (dep,) + operands
    return pl.pallas_call(
        body,
        name=name,
        out_shape=jax.ShapeDtypeStruct((m, n), out_dtype),
        grid=(m // tm, n // tn, nk),
        in_specs=in_specs,
        out_specs=o_spec,
        scratch_shapes=[pltpu.VMEM((tm, tn), F32)] if nk > 1 else [],
        compiler_params=_cparams(("parallel", "parallel", "arbitrary")),
    )(*operands)


def _rms_fwd(x, g, *, name):
    s, d = x.shape
    tr = min(512, s)

    def body(x_ref, g_ref, h_ref):
        xv = x_ref[...]
        rs = lax.rsqrt(jnp.mean(xv * xv, axis=-1, keepdims=True) + NORM_EPS)
        h_ref[...] = (xv * rs * g_ref[...]).astype(BF16)

    return pl.pallas_call(
        body, name=name,
        out_shape=jax.ShapeDtypeStruct((s, d), BF16),
        grid=(s // tr,),
        in_specs=[pl.BlockSpec((tr, d), lambda i: (i, 0)), pl.BlockSpec((1, d), lambda i: (0, 0))],
        out_specs=pl.BlockSpec((tr, d), lambda i: (i, 0)),
        compiler_params=_cparams(("parallel",)),
    )(x, g.reshape(1, d))


def _rms_bwd(x, g, dh, dres, *, name):
    s, d = x.shape
    tr = min(512, s)

    def body(x_ref, g_ref, dh_ref, dres_ref, dx_ref, dxb_ref, dg_ref):
        i = pl.program_id(0)
        xv = x_ref[...]
        rs = lax.rsqrt(jnp.mean(xv * xv, axis=-1, keepdims=True) + NORM_EPS)
        n = xv * rs
        dhv = dh_ref[...].astype(F32)
        dn = dhv * g_ref[...]
        dx = dres_ref[...] + rs * (dn - n * jnp.mean(dn * n, axis=-1, keepdims=True))
        dx_ref[...] = dx
        dxb_ref[...] = dx.astype(BF16)
        part = jnp.sum(dhv * n, axis=0, keepdims=True)

        @pl.when(i == 0)
        def _():
            dg_ref[...] = part

        @pl.when(i > 0)
        def _():
            dg_ref[...] += part

    row = pl.BlockSpec((tr, d), lambda i: (i, 0))
    vec = pl.BlockSpec((1, d), lambda i: (0, 0))
    return pl.pallas_call(
        body, name=name,
        out_shape=(jax.ShapeDtypeStruct((s, d), F32), jax.ShapeDtypeStruct((s, d), BF16),
                   jax.ShapeDtypeStruct((1, d), F32)),
        grid=(s // tr,),
        in_specs=[row, vec, row, row],
        out_specs=(row, row, vec),
        compiler_params=_cparams(("arbitrary",)),
    )(x, g.reshape(1, d), dh, dres)


def _loss_head(x, g, target):
    s, d = x.shape
    tr = min(512, s)

    def body(x_ref, g_ref, t_ref, loss_ref, dx_ref, dxb_ref, dg_ref):
        i = pl.program_id(0)
        xv = x_ref[...]
        rs = lax.rsqrt(jnp.mean(xv * xv, axis=-1, keepdims=True) + NORM_EPS)
        n = xv * rs
        gv = g_ref[...]
        err = n * gv - t_ref[...]
        row_loss = jnp.mean(err * err, axis=-1, keepdims=True)
        loss_part = 0.5 * jnp.sum(row_loss, axis=0, keepdims=True)
        dy = err * (1.0 / d)
        dn = dy * gv
        dx = rs * (dn - n * jnp.mean(dn * n, axis=-1, keepdims=True))
        dx_ref[...] = dx
        dxb_ref[...] = dx.astype(BF16)
        dg_part = jnp.sum(dy * n, axis=0, keepdims=True)

        @pl.when(i == 0)
        def _():
            loss_ref[...] = loss_part
            dg_ref[...] = dg_part

        @pl.when(i > 0)
        def _():
            loss_ref[...] += loss_part
            dg_ref[...] += dg_part

    row = pl.BlockSpec((tr, d), lambda i: (i, 0))
    vec = pl.BlockSpec((1, d), lambda i: (0, 0))
    one = pl.BlockSpec((1, 1), lambda i: (0, 0))
    return pl.pallas_call(
        body, name="loss_head",
        out_shape=(jax.ShapeDtypeStruct((1, 1), F32), jax.ShapeDtypeStruct((s, d), F32),
                   jax.ShapeDtypeStruct((s, d), BF16), jax.ShapeDtypeStruct((1, d), F32)),
        grid=(s // tr,),
        in_specs=[row, vec, row],
        out_specs=(one, row, row, vec),
        compiler_params=_cparams(("arbitrary",)),
    )(x, g.reshape(1, d), target)


def _conv_taps(u_prev, u, w_ref, rows):
    ext = jnp.concatenate([u_prev, u], axis=0)
    u1 = pltpu.roll(ext, 1, 0)[HALO:HALO + rows]
    u2 = pltpu.roll(ext, 2, 0)[HALO:HALO + rows]
    conv = w_ref[0:1, :] * u2 + w_ref[1:2, :] * u1 + w_ref[2:3, :] * u + w_ref[3:4, :]
    return conv, u1, u2


def _conv_fwd(proj, conv_wb, *, s):
    tr = min(512, s)
    c = CONV_CH
    hb = tr // HALO

    def body(gbi_ref, gci_ref, cx_ref, gci_h_ref, cx_h_ref, w_ref, o_ref):
        i = pl.program_id(0)
        u = gci_ref[...].astype(F32) * cx_ref[...].astype(F32)
        u_prev = gci_h_ref[...].astype(F32) * cx_h_ref[...].astype(F32)
        u_prev = jnp.where(i == 0, 0.0, u_prev)
        conv, _, _ = _conv_taps(u_prev, u, w_ref, tr)
        o_ref[...] = (gbi_ref[...].astype(F32) * conv).astype(BF16)

    def seg(col):
        return pl.BlockSpec((tr, c), lambda i: (i, col // c))

    def halo(col):
        return pl.BlockSpec((HALO, c), lambda i: (jnp.maximum(i * hb - 1, 0), col // c))

    return pl.pallas_call(
        body, name="conv_fwd",
        out_shape=jax.ShapeDtypeStruct((s, c), BF16),
        grid=(s // tr,),
        in_specs=[seg(COL_GBI), seg(COL_GCI), seg(COL_CX), halo(COL_GCI), halo(COL_CX),
                  pl.BlockSpec((8, c), lambda i: (0, 0))],
        out_specs=pl.BlockSpec((tr, c), lambda i: (i, 0)),
        compiler_params=_cparams(("parallel",)),
    )(proj, proj, proj, proj, proj, conv_wb)


def _conv_bwd(proj, dcb, conv_wb, dproj, *, s):
    tr = min(512, s)
    c = CONV_CH
    hb = tr // HALO
    nb = s // tr

    def body(gbi_ref, gci_ref, cx_ref, dcb_ref, gci_h_ref, cx_h_ref, gbi_n_ref, dcb_n_ref, w_ref,
             dproj_in_ref, dp_ref, dw_ref):
        i = pl.program_id(0)
        gbi = gbi_ref[...].astype(F32)
        gci = gci_ref[...].astype(F32)
        cx = cx_ref[...].astype(F32)
        dcb_v = dcb_ref[...].astype(F32)
        u = gci * cx
        u_prev = jnp.where(i == 0, 0.0, gci_h_ref[...].astype(F32) * cx_h_ref[...].astype(F32))
        conv, u1, u2 = _conv_taps(u_prev, u, w_ref, tr)
        dconv = dcb_v * gbi
        dconv_next = jnp.where(i == nb - 1, 0.0, dcb_n_ref[...].astype(F32) * gbi_n_ref[...].astype(F32))
        ext = jnp.concatenate([dconv, dconv_next], axis=0)
        n_ext = tr + HALO
        d1 = pltpu.roll(ext, n_ext - 1, 0)[0:tr]
        d2 = pltpu.roll(ext, n_ext - 2, 0)[0:tr]
        du = w_ref[2:3, :] * dconv + w_ref[1:2, :] * d1 + w_ref[0:1, :] * d2
        dp_ref[:, 0:c] = (dcb_v * conv).astype(BF16)
        dp_ref[:, c:2 * c] = (du * cx).astype(BF16)
        dp_ref[:, 2 * c:3 * c] = (du * gci).astype(BF16)
        part = jnp.concatenate([
            jnp.sum(dconv * u2, axis=0, keepdims=True),
            jnp.sum(dconv * u1, axis=0, keepdims=True),
            jnp.sum(dconv * u, axis=0, keepdims=True),
            jnp.sum(dconv, axis=0, keepdims=True),
            jnp.zeros((4, c), F32)], axis=0)

        @pl.when(i == 0)
        def _():
            dw_ref[...] = part

        @pl.when(i > 0)
        def _():
            dw_ref[...] += part

    def seg(col):
        return pl.BlockSpec((tr, c), lambda i: (i, col // c))

    def halo_prev(col):
        return pl.BlockSpec((HALO, c), lambda i: (jnp.maximum(i * hb - 1, 0), col // c))

    def halo_next(col):
        return pl.BlockSpec((HALO, c), lambda i: (jnp.minimum((i + 1) * hb, nb * hb - 1), col // c))

    return pl.pallas_call(
        body, name="conv_bwd",
        out_shape=(jax.ShapeDtypeStruct((s, PROJ_W), BF16), jax.ShapeDtypeStruct((8, c), F32)),
        grid=(nb,),
        in_specs=[seg(COL_GBI), seg(COL_GCI), seg(COL_CX), pl.BlockSpec((tr, c), lambda i: (i, 0)),
                  halo_prev(COL_GCI), halo_prev(COL_CX), halo_next(COL_GBI),
                  pl.BlockSpec((HALO, c), lambda i: (jnp.minimum((i + 1) * hb, nb * hb - 1), 0)),
                  pl.BlockSpec((8, c), lambda i: (0, 0)), ANY],
        out_specs=(pl.BlockSpec((tr, 3 * c), lambda i: (i, COL_GBI // (3 * c))),
                   pl.BlockSpec((8, c), lambda i: (0, 0))),
        input_output_aliases={9: 0},
        compiler_params=_cparams(("arbitrary",)),
    )(proj, proj, proj, dcb, proj, proj, proj, dcb, conv_wb, dproj)


def _mix_fwd(proj, ya, yb, *, s):
    tr = min(512, s)
    d = D_MODEL

    def body(ga_ref, gb_ref, ya_ref, yb_ref, o_ref):
        sa = _sigmoid(ga_ref[...].astype(F32))
        sb = _sigmoid(gb_ref[...].astype(F32))
        o_ref[...] = (sa * ya_ref[...].astype(F32) + sb * yb_ref[...].astype(F32)).astype(BF16)

    row = pl.BlockSpec((tr, d), lambda i: (i, 0))
    return pl.pallas_call(
        body, name="mix_fwd",
        out_shape=jax.ShapeDtypeStruct((s, d), BF16),
        grid=(s // tr,),
        in_specs=[pl.BlockSpec((tr, d), lambda i: (i, COL_GA // d)),
                  pl.BlockSpec((tr, d), lambda i: (i, COL_GB // d)), row, row],
        out_specs=row,
        compiler_params=_cparams(("parallel",)),
    )(proj, proj, ya, yb)


def _mix_bwd(proj, ya, yb, dmix, *, s):
    tr = min(512, s)
    d = D_MODEL

    def body(ga_ref, gb_ref, ya_ref, yb_ref, dm_ref, dya_ref, dyb_ref, dg_ref):
        sa = _sigmoid(ga_ref[...].astype(F32))
        sb = _sigmoid(gb_ref[...].astype(F32))
        dm = dm_ref[...].astype(F32)
        dya_ref[...] = (dm * sa).astype(BF16)
        dyb_ref[...] = (dm * sb).astype(BF16)
        dg_ref[:, 0:d] = (dm * ya_ref[...].astype(F32) * sa * (1.0 - sa)).astype(BF16)
        dg_ref[:, d:2 * d] = (dm * yb_ref[...].astype(F32) * sb * (1.0 - sb)).astype(BF16)

    row = pl.BlockSpec((tr, d), lambda i: (i, 0))
    return pl.pallas_call(
        body, name="mix_bwd",
        out_shape=(jax.ShapeDtypeStruct((s, d), BF16), jax.ShapeDtypeStruct((s, d), BF16),
                   jax.ShapeDtypeStruct((s, PROJ_W), BF16)),
        grid=(s // tr,),
        in_specs=[pl.BlockSpec((tr, d), lambda i: (i, COL_GA // d)),
                  pl.BlockSpec((tr, d), lambda i: (i, COL_GB // d)), row, row, row],
        out_specs=(row, row, pl.BlockSpec((tr, 2 * d), lambda i: (i, COL_GA // (2 * d)))),
        compiler_params=_cparams(("parallel",)),
    )(proj, proj, ya, yb, dmix)


def _swiglu_fwd(gt, up, *, s):
    tr = min(256, s)
    f = FFN_HIDDEN

    def body(gt_ref, up_ref, o_ref):
        gv = gt_ref[...].astype(F32)
        o_ref[...] = (gv * _sigmoid(gv) * up_ref[...].astype(F32)).astype(BF16)

    row = pl.BlockSpec((tr, f), lambda i: (i, 0))
    return pl.pallas_call(
        body, name="swiglu_fwd",
        out_shape=jax.ShapeDtypeStruct((s, f), BF16),
        grid=(s // tr,),
        in_specs=[row, row], out_specs=row,
        compiler_params=_cparams(("parallel",)),
    )(gt, up)


def _swiglu_bwd(gt, up, dhid, *, s):
    tr = min(256, s)
    f = FFN_HIDDEN

    def body(gt_ref, up_ref, dh_ref, dgt_ref, dup_ref):
        gv = gt_ref[...].astype(F32)
        uv = up_ref[...].astype(F32)
        dh = dh_ref[...].astype(F32)
        sg = _sigmoid(gv)
        dgt_ref[...] = (dh * uv * sg * (1.0 + gv * (1.0 - sg))).astype(BF16)
        dup_ref[...] = (dh * gv * sg).astype(BF16)

    row = pl.BlockSpec((tr, f), lambda i: (i, 0))
    return pl.pallas_call(
        body, name="swiglu_bwd",
        out_shape=(jax.ShapeDtypeStruct((s, f), BF16), jax.ShapeDtypeStruct((s, f), BF16)),
        grid=(s // tr,),
        in_specs=[row, row, row], out_specs=(row, row),
        compiler_params=_cparams(("parallel",)),
    )(gt, up, dhid)


def _tri(strict):
    r = lax.broadcasted_iota(jnp.int32, (CHUNK, CHUNK), 0)
    c = lax.broadcasted_iota(jnp.int32, (CHUNK, CHUNK), 1)
    return jnp.where((c < r) if strict else (c <= r), 1.0, 0.0).astype(F32)


def _gla_gate_terms(la_c, tri):
    cum = jnp.dot(tri, la_c, precision=lax.Precision.HIGHEST, preferred_element_type=F32)
    cend = cum[CHUNK - 1:CHUNK, :]
    return jnp.exp(cend - cum), jnp.exp(cend)


def _hk(h):
    return slice(h * HEAD_K, (h + 1) * HEAD_K)


def _hv(h):
    return slice(h * HEAD_V, (h + 1) * HEAD_V)


def _gla_fwd(proj, wfg2p, bfg, gn, *, s):
    tr = min(512, s)
    nb = s // tr
    nc = tr // CHUNK

    def body(q_ref, k_ref, v_ref, r_ref, fz_ref, w_ref, b_ref, gn_ref, oa_ref, st_ref,
             state, la_scr, o_scr):
        i = pl.program_id(0)

        @pl.when(i == 0)
        def _():
            state[...] = jnp.zeros_like(state)

        fg = jnp.dot(fz_ref[...], w_ref[...], preferred_element_type=F32) + b_ref[...]
        la_scr[...] = _log_sigmoid(fg) * (1.0 / GATE_TAU)
        tri = _tri(False)

        def chunk(ci, carry):
            r0 = pl.multiple_of(ci * CHUNK, CHUNK)
            rows = pl.ds(r0, CHUNK)
            e, gam = _gla_gate_terms(la_scr[rows, :], tri)
            kd = (k_ref[rows, :].astype(F32) * e).astype(BF16)
            qs = (q_ref[rows, :].astype(F32) * Q_SCALE).astype(BF16)
            v_c = v_ref[rows, :]
            for h in range(GLA_HEADS):
                upd = lax.dot_general(v_c[:, _hv(h)], kd[:, _hk(h)], _DOT_DIMS["tn"],
                                      preferred_element_type=F32)
                st_h = state[:, _hk(h)] * gam[:, _hk(h)] + upd
                state[:, _hk(h)] = st_h
                o_scr[rows, _hv(h)] = lax.dot_general(qs[:, _hk(h)], st_h.astype(BF16), _DOT_DIMS["nt"],
                                                       preferred_element_type=F32)
            st_ref[ci] = state[...]
            return carry

        lax.fori_loop(0, nc, chunk, 0)
        for h in range(GLA_HEADS):
            o = o_scr[:, _hv(h)]
            rs = lax.rsqrt(jnp.mean(o * o, axis=-1, keepdims=True) + NORM_EPS)
            rv = r_ref[:, _hv(h)].astype(F32)
            oa_ref[:, _hv(h)] = ((o * rs * gn_ref[...]).astype(F32) * (rv * _sigmoid(rv))).astype(BF16)

    return pl.pallas_call(
        body, name="gla_fwd",
        out_shape=(jax.ShapeDtypeStruct((s, VAL_WIDTH), BF16),
                   jax.ShapeDtypeStruct((s // CHUNK, HEAD_V, KEY_WIDTH), F32)),
        grid=(nb,),
        in_specs=[pl.BlockSpec((tr, KEY_WIDTH), lambda i: (i, COL_Q // KEY_WIDTH)),
                  pl.BlockSpec((tr, KEY_WIDTH), lambda i: (i, COL_K // KEY_WIDTH)),
                  pl.BlockSpec((tr, VAL_WIDTH), lambda i: (i, COL_V // VAL_WIDTH)),
                  pl.BlockSpec((tr, VAL_WIDTH), lambda i: (i, COL_R // VAL_WIDTH)),
                  pl.BlockSpec((tr, FZ_PAD), lambda i: (i, COL_FZ // FZ_PAD)),
                  pl.BlockSpec((FZ_PAD, KEY_WIDTH), lambda i: (0, 0)),
                  pl.BlockSpec((1, KEY_WIDTH), lambda i: (0, 0)),
                  pl.BlockSpec((1, HEAD_V), lambda i: (0, 0))],
        out_specs=(pl.BlockSpec((tr, VAL_WIDTH), lambda i: (i, 0)),
                   pl.BlockSpec((nc, HEAD_V, KEY_WIDTH), lambda i: (i, 0, 0))),
        scratch_shapes=[pltpu.VMEM((HEAD_V, KEY_WIDTH), F32), pltpu.VMEM((tr, KEY_WIDTH), F32),
                        pltpu.VMEM((tr, VAL_WIDTH), F32)],
        compiler_params=_cparams(("arbitrary",)),
    )(proj, proj, proj, proj, proj, wfg2p, bfg, gn)


def _gla_bwd(proj, doa, states, wfg2p, bfg, gn, dproj, *, s):
    tr = min(512, s)
    nb = s // tr
    nc = tr // CHUNK

    def body(q_ref, k_ref, v_ref, r_ref, fz_ref, doa_ref, st_ref, stp_ref, w_ref, b_ref, gn_ref, dproj_in_ref,
             dp_ref, dfz_ref, dgn_ref, dw_ref, db_ref,
             carry, fg_scr, la_scr, dla_scr, o_scr, do_scr):
        i = pl.program_id(0)

        @pl.when(i == 0)
        def _():
            carry[...] = jnp.zeros_like(carry)
            dgn_ref[...] = jnp.zeros_like(dgn_ref)
            dw_ref[...] = jnp.zeros_like(dw_ref)
            db_ref[...] = jnp.zeros_like(db_ref)

        fz = fz_ref[...]
        fg = jnp.dot(fz, w_ref[...], preferred_element_type=F32) + b_ref[...]
        fg_scr[...] = fg
        la_scr[...] = _log_sigmoid(fg) * (1.0 / GATE_TAU)

        def fwd_chunk(ci, c0):
            r0 = pl.multiple_of(ci * CHUNK, CHUNK)
            rows = pl.ds(r0, CHUNK)
            qs = (q_ref[rows, :].astype(F32) * Q_SCALE).astype(BF16)
            st = st_ref[ci].astype(BF16)
            for h in range(GLA_HEADS):
                o_scr[rows, _hv(h)] = lax.dot_general(qs[:, _hk(h)], st[:, _hk(h)], _DOT_DIMS["nt"],
                                                       preferred_element_type=F32)
            return c0

        lax.fori_loop(0, nc, fwd_chunk, 0)

        gnv = gn_ref[...]
        dgn_part = jnp.zeros((1, HEAD_V), F32)
        for h in range(GLA_HEADS):
            o = o_scr[:, _hv(h)]
            rs = lax.rsqrt(jnp.mean(o * o, axis=-1, keepdims=True) + NORM_EPS)
            nrm = o * rs
            rv = r_ref[:, _hv(h)].astype(F32)
            sg = _sigmoid(rv)
            doa_h = doa_ref[:, _hv(h)].astype(F32)
            don = doa_h * (rv * sg)
            dp_ref[:, COL_R + h * HEAD_V:COL_R + (h + 1) * HEAD_V] = (
                doa_h * (nrm * gnv) * (sg * (1.0 + rv * (1.0 - sg)))).astype(BF16)
            dgn_part = dgn_part + jnp.sum(don * nrm, axis=0, keepdims=True)
            dn = don * gnv
            do_scr[:, _hv(h)] = (rs * (dn - nrm * jnp.mean(dn * nrm, axis=-1, keepdims=True))).astype(BF16)
        dgn_ref[...] += dgn_part

        tri = _tri(False)
        tri_s = _tri(True)
        first_block = i == nb - 1

        def bwd_chunk(cc, c0):
            ci = nc - 1 - cc
            r0 = pl.multiple_of(ci * CHUNK, CHUNK)
            rows = pl.ds(r0, CHUNK)
            e, gam = _gla_gate_terms(la_scr[rows, :], tri)
            k_c = k_ref[rows, :].astype(F32)
            kd = k_c * e
            kd_b = kd.astype(BF16)
            qs = (q_ref[rows, :].astype(F32) * Q_SCALE).astype(BF16)
            v_c = v_ref[rows, :]
            do_c = do_scr[rows, :]
            st = st_ref[ci]
            st_b = st.astype(BF16)
            st_prev_in = st_ref[jnp.maximum(ci - 1, 0)]
            st_prev_edge = jnp.where(first_block, 0.0, stp_ref[0])
            st_prev = jnp.where(ci > 0, st_prev_in, st_prev_edge)
            dkd_parts = []
            dgam_parts = []
            for h in range(GLA_HEADS):
                dst = lax.dot_general(do_c[:, _hv(h)], qs[:, _hk(h)], _DOT_DIMS["tn"],
                                      preferred_element_type=F32) + carry[:, _hk(h)]
                dst_b = dst.astype(BF16)
                dqs = jnp.dot(do_c[:, _hv(h)], st_b[:, _hk(h)], preferred_element_type=F32)
                dp_ref[rows, COL_Q + h * HEAD_K:COL_Q + (h + 1) * HEAD_K] = (dqs * Q_SCALE).astype(BF16)
                dkd_parts.append(jnp.dot(v_c[:, _hv(h)], dst_b, preferred_element_type=F32))
                dp_ref[rows, COL_V + h * HEAD_V:COL_V + (h + 1) * HEAD_V] = lax.dot_general(
                    kd_b[:, _hk(h)], dst_b, _DOT_DIMS["nt"], preferred_element_type=F32).astype(BF16)
                dgam_parts.append(jnp.sum(dst * st_prev[:, _hk(h)], axis=0, keepdims=True))
                carry[:, _hk(h)] = dst * gam[:, _hk(h)]
            dkd = jnp.concatenate(dkd_parts, axis=1)
            dgam = jnp.concatenate(dgam_parts, axis=1)
            dp_ref[rows, COL_K:COL_K + KEY_WIDTH] = (dkd * e).astype(BF16)
            dz = dkd * kd
            dla_scr[rows, :] = gam * dgam + jnp.dot(tri_s, dz, precision=lax.Precision.HIGHEST,
                                                    preferred_element_type=F32)
            return c0

        lax.fori_loop(0, nc, bwd_chunk, 0)

        dfg = dla_scr[...] * (1.0 / GATE_TAU) * _sigmoid(-fg_scr[...])
        dfg_b = dfg.astype(BF16)
        dfz_ref[...] = lax.dot_general(dfg_b, w_ref[...], _DOT_DIMS["nt"],
                                       preferred_element_type=F32).astype(BF16)
        dw_ref[...] += lax.dot_general(fz, dfg_b, _DOT_DIMS["tn"], preferred_element_type=F32)
        db_ref[...] += jnp.sum(dfg, axis=0, keepdims=True)

    def rev(i):
        return nb - 1 - i

    return pl.pallas_call(
        body, name="gla_bwd",
        out_shape=(jax.ShapeDtypeStruct((s, PROJ_W), BF16),
                   jax.ShapeDtypeStruct((s, FZ_PAD), BF16),
                   jax.ShapeDtypeStruct((1, HEAD_V), F32),
                   jax.ShapeDtypeStruct((FZ_PAD, KEY_WIDTH), F32),
                   jax.ShapeDtypeStruct((1, KEY_WIDTH), F32)),
        grid=(nb,),
        in_specs=[pl.BlockSpec((tr, KEY_WIDTH), lambda i: (rev(i), COL_Q // KEY_WIDTH)),
                  pl.BlockSpec((tr, KEY_WIDTH), lambda i: (rev(i), COL_K // KEY_WIDTH)),
                  pl.BlockSpec((tr, VAL_WIDTH), lambda i: (rev(i), COL_V // VAL_WIDTH)),
                  pl.BlockSpec((tr, VAL_WIDTH), lambda i: (rev(i), COL_R // VAL_WIDTH)),
                  pl.BlockSpec((tr, FZ_PAD), lambda i: (rev(i), COL_FZ // FZ_PAD)),
                  pl.BlockSpec((tr, VAL_WIDTH), lambda i: (rev(i), 0)),
                  pl.BlockSpec((nc, HEAD_V, KEY_WIDTH), lambda i: (rev(i), 0, 0)),
                  pl.BlockSpec((1, HEAD_V, KEY_WIDTH), lambda i: (jnp.maximum(rev(i) * nc - 1, 0), 0, 0)),
                  pl.BlockSpec((FZ_PAD, KEY_WIDTH), lambda i: (0, 0)),
                  pl.BlockSpec((1, KEY_WIDTH), lambda i: (0, 0)),
                  pl.BlockSpec((1, HEAD_V), lambda i: (0, 0)), ANY],
        out_specs=(pl.BlockSpec((tr, 3 * VAL_WIDTH), lambda i: (rev(i), 0)),
                   pl.BlockSpec((tr, FZ_PAD), lambda i: (rev(i), 0)),
                   pl.BlockSpec((1, HEAD_V), lambda i: (0, 0)),
                   pl.BlockSpec((FZ_PAD, KEY_WIDTH), lambda i: (0, 0)),
                   pl.BlockSpec((1, KEY_WIDTH), lambda i: (0, 0))),
        scratch_shapes=[pltpu.VMEM((HEAD_V, KEY_WIDTH), F32),
                        pltpu.VMEM((tr, KEY_WIDTH), F32),
                        pltpu.VMEM((tr, KEY_WIDTH), F32),
                        pltpu.VMEM((tr, KEY_WIDTH), F32),
                        pltpu.VMEM((tr, VAL_WIDTH), F32),
                        pltpu.VMEM((tr, VAL_WIDTH), BF16)],
        input_output_aliases={11: 0},
        compiler_params=_cparams(("arbitrary",)),
    )(proj, proj, proj, proj, proj, doa, states, states, wfg2p, bfg, gn, dproj)


def _put_fz(dproj, dfz, *, s):
    tr = min(512, s)

    def body(dfz_ref, dproj_in_ref, o_ref):
        o_ref[...] = dfz_ref[...]

    return pl.pallas_call(
        body, name="put_dfz",
        out_shape=jax.ShapeDtypeStruct((s, PROJ_W), BF16),
        grid=(s // tr,),
        in_specs=[pl.BlockSpec((tr, FZ_PAD), lambda i: (i, 0)), ANY],
        out_specs=pl.BlockSpec((tr, FZ_PAD), lambda i: (i, COL_FZ // FZ_PAD)),
        input_output_aliases={1: 0},
        compiler_params=_cparams(("parallel",)),
    )(dfz, dproj)


_ADAM_C1 = 1.0 / (1.0 - ADAM_B1 ** ADAM_STEP)
_ADAM_C2 = 1.0 / (1.0 - ADAM_B2 ** ADAM_STEP)


def _adamw_math(wv, gv, mv, vv):
    nm = ADAM_B1 * mv + (1.0 - ADAM_B1) * gv
    nv = ADAM_B2 * vv + (1.0 - ADAM_B2) * (gv * gv)
    delta = -ADAM_LR * ((nm * _ADAM_C1) / (jnp.sqrt(nv * _ADAM_C2) + ADAM_EPS) + ADAM_WD * wv)
    return delta, nm, nv


def _adamw(w, g, m, v, *, name):
    shape = w.shape
    if w.ndim == 1:
        w, g, m, v = (t.reshape(1, 1, -1) for t in (w, g, m, v))
    elif w.ndim == 2:
        w, g, m, v = (t.reshape((1,) + t.shape) for t in (w, g, m, v))
    l, a, b = w.shape

    def body(w_ref, g_ref, m_ref, v_ref, d_ref, nm_ref, nv_ref):
        d_ref[...], nm_ref[...], nv_ref[...] = _adamw_math(w_ref[...], g_ref[...], m_ref[...], v_ref[...])

    blk = pl.BlockSpec((1, a, b), lambda li: (li, 0, 0))
    outs = pl.pallas_call(
        body, name=name,
        out_shape=tuple(jax.ShapeDtypeStruct((l, a, b), F32) for _ in range(3)),
        grid=(l,),
        in_specs=[blk, blk, blk, blk],
        out_specs=(blk, blk, blk),
        compiler_params=_cparams(("parallel",)),
    )(w, g, m, v)
    return tuple(o.reshape(shape) for o in outs)


def _adamw_layers(w, reduced, received, on_core0, m, v, *, name):
    l, a, b = w.shape
    assert l == DEPTH == 2
    ta = _row_tile(a, F32_SUBLANES, 384)

    def body(flag_ref, w_ref, r0_ref, o0_ref, r1_ref, o1_ref, m_ref, v_ref, g_ref, d_ref, nm_ref, nv_ref):
        core0 = flag_ref[...] > 0.5
        g0 = jnp.where(core0, r0_ref[...], o0_ref[...])
        g1 = jnp.where(core0, o1_ref[...], r1_ref[...])
        gv = jnp.where(pl.program_id(0) == 0, g0, g1)
        g_ref[0] = gv
        d_ref[0], nm_ref[0], nv_ref[0] = _adamw_math(w_ref[0], gv, m_ref[0], v_ref[0])

    blk = pl.BlockSpec((1, ta, b), lambda li, ai: (li, ai, 0))
    gblk = pl.BlockSpec((ta, b), lambda li, ai: (ai, 0))
    return pl.pallas_call(
        body, name=name,
        out_shape=tuple(jax.ShapeDtypeStruct((l, a, b), F32) for _ in range(4)),
        grid=(l, a // ta),
        in_specs=[pl.BlockSpec((1, 1), lambda li, ai: (0, 0)), blk, gblk, gblk, gblk, gblk, blk, blk],
        out_specs=(blk, blk, blk, blk),
        compiler_params=_cparams(("parallel", "parallel")),
    )(on_core0, w, reduced[0], received[0], reduced[1], received[1], m, v)


MESH_ID = pl.DeviceIdType.MESH
ANY = pl.BlockSpec(memory_space=pl.ANY)


def _position():
    x, y, c = lax.axis_index("x"), lax.axis_index("y"), lax.axis_index("c")
    chips = [(1 - x, y), (x, 1 - y), (1 - x, 1 - y)]
    return x, y, c, chips


def _chip_index(xy):
    return 2 * xy[0] + xy[1]


def _all_gather_weights(shards):
    n = len(shards)

    def body(*refs):
        ins, outs = refs[:n], refs[n:2 * n]
        send_sems, recv_sems, pass_send, pass_recv = refs[2 * n:]
        x, y, c, chips = _position()
        sibling = (x, y, 1 - c)

        def ici(a, k, chip_from):
            return pltpu.make_async_remote_copy(
                src_ref=ins[a].at[c], dst_ref=outs[a].at[c, _chip_index(chip_from)],
                send_sem=send_sems.at[a, k], recv_sem=recv_sems.at[a, k],
                device_id=(chips[k][0], chips[k][1], c), device_id_type=MESH_ID)

        def handoff(a, k, layer):
            slot = outs[a].at[layer, _chip_index(chips[k])]
            return pltpu.make_async_remote_copy(
                src_ref=slot, dst_ref=slot, send_sem=pass_send.at[a, k], recv_sem=pass_recv.at[a, k],
                device_id=sibling, device_id_type=MESH_ID)

        for a in range(n):
            for k in range(3):
                ici(a, k, (x, y)).start()
        for k in range(3):
            for a in range(n):
                ici(a, k, chips[k]).wait_recv()
                handoff(a, k, c).start()
        for k in range(3):
            for a in range(n):
                handoff(a, k, 1 - c).wait_recv()
        for a in range(n):
            for k in range(3):
                ici(a, k, (x, y)).wait_send()
                handoff(a, k, c).wait_send()

    gathered = pl.pallas_call(
        body, name="all_gather_weights",
        out_shape=tuple(jax.ShapeDtypeStruct((DEPTH, N_CHIPS) + t.shape[1:], t.dtype) for t in shards),
        in_specs=[ANY] * n, out_specs=tuple([ANY] * n),
        scratch_shapes=[pltpu.SemaphoreType.DMA((n, 3)), pltpu.SemaphoreType.DMA((n, 3)),
                        pltpu.SemaphoreType.DMA((n, 3)), pltpu.SemaphoreType.DMA((n, 3))],
    )(*shards)
    me = _chip_index((lax.axis_index("x"), lax.axis_index("y")))
    return [lax.dynamic_update_index_in_dim(g, t[:, None], me, axis=1) for g, t in zip(gathered, shards)]


HBM = pl.BlockSpec(memory_space=pltpu.HBM)
SEM = pl.BlockSpec(memory_space=pltpu.SEMAPHORE)
DATAFLOW_EFFECT = pltpu.SideEffectType.DATAFLOW_SIDE_EFFECTING
TOKEN_SHAPE = (8, LANE)


def _landing(shape, dtype):
    return pltpu.with_memory_space_constraint(lax.empty(shape, dtype), pltpu.HBM)


def _split_start(bufs, sem_shape, issue, *, name):
    n = len(bufs)

    def body(*refs):
        issue(refs[:n], refs[n], refs[n + 1])
        token = refs[-1]
        token[...] = jnp.zeros_like(token)

    outs = pl.pallas_call(
        body, name=name,
        out_shape=(pltpu.SemaphoreType.DMA(sem_shape), pltpu.SemaphoreType.DMA(sem_shape),
                   *[pltpu.HBM(t.shape, t.dtype) for t in bufs], jax.ShapeDtypeStruct(TOKEN_SHAPE, F32)),
        in_specs=[HBM] * n,
        out_specs=(SEM, SEM, *[HBM] * n, pl.BlockSpec(memory_space=pltpu.VMEM)),
        input_output_aliases={i: 2 + i for i in range(n)},
        compiler_params=pltpu.CompilerParams(has_side_effects=DATAFLOW_EFFECT),
    )(*[pltpu.with_memory_space_constraint(t, pltpu.HBM) for t in bufs])
    return outs[0], outs[1], list(outs[2:2 + n]), outs[-1]


def _split_wait(started, after, settle, *, name):
    send_sems, recv_sems, bufs, _ = started
    n = len(bufs)

    def body(*refs):
        settle(refs[:n], refs[n], refs[n + 1])

    outs = pl.pallas_call(
        body, name=name,
        out_shape=tuple(pltpu.HBM(t.shape, t.dtype) for t in bufs),
        in_specs=[HBM] * n + [SEM, SEM, ANY],
        out_specs=tuple([HBM] * n),
        input_output_aliases={i: i for i in range(n)},
        compiler_params=pltpu.CompilerParams(has_side_effects=DATAFLOW_EFFECT),
    )(*bufs, send_sems, recv_sems, after)
    return list(outs)


def _to_sibling(bufs, n, send_sems, recv_sems):
    x, y, c, _ = _position()
    return [pltpu.make_async_remote_copy(
        src_ref=bufs[a], dst_ref=bufs[n + a], send_sem=send_sems.at[a], recv_sem=recv_sems.at[a],
        device_id=(x, y, 1 - c), device_id_type=MESH_ID) for a in range(n)]


def _sibling_push_start(layer, sender_is_reducer, arrays, *, name):
    n = len(arrays)
    sender = layer if sender_is_reducer else 1 - layer

    def issue(bufs, send_sems, recv_sems):
        @pl.when(lax.axis_index("c") == sender)
        def _():
            for cp in _to_sibling(bufs, n, send_sems, recv_sems):
                cp.start()

    lands = [_landing(t.shape, t.dtype) for t in arrays]
    return _split_start(list(arrays) + lands, (n,), issue, name=name)


def _sibling_push_wait(layer, sender_is_reducer, started, after, *, name):
    n = len(started[2]) // 2
    sender = layer if sender_is_reducer else 1 - layer

    def settle(bufs, send_sems, recv_sems):
        c = lax.axis_index("c")

        @pl.when(c == sender)
        def _():
            for cp in _to_sibling(bufs, n, send_sems, recv_sems):
                cp.wait_send()

        @pl.when(c != sender)
        def _():
            for cp in _to_sibling(bufs, n, send_sems, recv_sems):
                cp.wait_recv()

    outs = _split_wait(started, after, settle, name=name)
    return outs[:n], outs[n:]


def _chip_copies(bufs, n, send_sems, recv_sems):
    x, y, c, chips = _position()
    return [pltpu.make_async_remote_copy(
        src_ref=bufs[a].at[_chip_index(chips[k])], dst_ref=bufs[n + a].at[k],
        send_sem=send_sems.at[3 * a + k], recv_sem=recv_sems.at[3 * a + k],
        device_id=(chips[k][0], chips[k][1], c), device_id_type=MESH_ID)
        for a in range(n) for k in range(3)]


def _chip_exchange_start(layer, parts, *, name):
    n = len(parts)

    def issue(bufs, send_sems, recv_sems):
        @pl.when(lax.axis_index("c") == layer)
        def _():
            for cp in _chip_copies(bufs, n, send_sems, recv_sems):
                cp.start()

    lands = [_landing((3,) + t.shape[1:], t.dtype) for t in parts]
    return _split_start(list(parts) + lands, (3 * n,), issue, name=name)


def _chip_exchange_wait(layer, started, after, *, name):
    n = len(started[2]) // 2

    def settle(bufs, send_sems, recv_sems):
        @pl.when(lax.axis_index("c") == layer)
        def _():
            for cp in _chip_copies(bufs, n, send_sems, recv_sems):
                cp.wait()

    outs = _split_wait(started, after, settle, name=name)
    return outs[:n], outs[n:]


def _gather_copies(layer, bufs, n, send_sems, recv_sems, arriving):
    x, y, c, chips = _position()
    me = _chip_index((x, y))
    return [pltpu.make_async_remote_copy(
        src_ref=bufs[a], dst_ref=bufs[n + a].at[_chip_index(chips[k]) if arriving else me],
        send_sem=send_sems.at[3 * a + k], recv_sem=recv_sems.at[3 * a + k],
        device_id=(chips[k][0], chips[k][1], c), device_id_type=MESH_ID)
        for a in range(n) for k in range(3)]


def _gather_start(layer, shards, *, name):
    n = len(shards)

    def issue(bufs, send_sems, recv_sems):
        @pl.when(lax.axis_index("c") == layer)
        def _():
            for cp in _gather_copies(layer, bufs, n, send_sems, recv_sems, False):
                cp.start()

    lands = [_landing((N_CHIPS,) + t.shape, t.dtype) for t in shards]
    return _split_start(list(shards) + lands, (3 * n,), issue, name=name)


def _gather_wait(layer, started, after, *, name):
    n = len(started[2]) // 2

    def settle(bufs, send_sems, recv_sems):
        @pl.when(lax.axis_index("c") == layer)
        def _():
            for cp in _gather_copies(layer, bufs, n, send_sems, recv_sems, False):
                cp.wait_send()
            for cp in _gather_copies(layer, bufs, n, send_sems, recv_sems, True):
                cp.wait_recv()

    return _split_wait(started, after, settle, name=name)[n:]


def _handoff_copies(bufs, n, send_sems, recv_sems):
    x, y, c, chips = _position()
    out = []
    for a in range(n):
        for k in range(3):
            slot = bufs[a].at[_chip_index(chips[k])]
            out.append(pltpu.make_async_remote_copy(
                src_ref=slot, dst_ref=slot, send_sem=send_sems.at[3 * a + k], recv_sem=recv_sems.at[3 * a + k],
                device_id=(x, y, 1 - c), device_id_type=MESH_ID))
    return out


def _handoff_start(layer, gathered, *, name):
    n = len(gathered)

    def issue(bufs, send_sems, recv_sems):
        @pl.when(lax.axis_index("c") == layer)
        def _():
            for cp in _handoff_copies(bufs, n, send_sems, recv_sems):
                cp.start()

    return _split_start(list(gathered), (3 * n,), issue, name=name)


def _handoff_wait(layer, started, after, *, name):
    n = len(started[2])

    def settle(bufs, send_sems, recv_sems):
        c = lax.axis_index("c")

        @pl.when(c == layer)
        def _():
            for cp in _handoff_copies(bufs, n, send_sems, recv_sems):
                cp.wait_send()

        @pl.when(c != layer)
        def _():
            for cp in _handoff_copies(bufs, n, send_sems, recv_sems):
                cp.wait_recv()

    return _split_wait(started, after, settle, name=name)


F32_SUBLANES = 8
BF16_SUBLANES = 16


def _row_tile(a, sublanes=BF16_SUBLANES, max_rows=704):
    best = None
    for cand in range(sublanes, min(a, max_rows) + 1, sublanes):
        if a % cand == 0:
            best = cand
    assert best is not None, a
    return best


def _pair_sum(mine, sib, *, name):
    nchip, a, b = sib.shape
    ta = _row_tile(a)

    def body(m_ref, s_ref, o_ref):
        o_ref[...] = (m_ref[...] + s_ref[...]).astype(BF16)

    blk = pl.BlockSpec((1, ta, b), lambda j, r: (j, r, 0))
    return pl.pallas_call(
        body, name=name,
        out_shape=jax.ShapeDtypeStruct((nchip, a, b), BF16),
        grid=(nchip, a // ta),
        in_specs=[blk, blk], out_specs=blk,
        compiler_params=_cparams(("parallel", "parallel")),
    )(mine, sib)


def _total_sum(own, recv, *, name):
    a, b = own.shape
    ta = _row_tile(a)

    def body(o_ref, r_ref, t_ref):
        acc = o_ref[...].astype(F32)
        for k in range(3):
            acc = acc + r_ref[k].astype(F32)
        t_ref[...] = acc

    blk = pl.BlockSpec((ta, b), lambda r: (r, 0))
    return pl.pallas_call(
        body, name=name,
        out_shape=jax.ShapeDtypeStruct((a, b), F32),
        grid=(a // ta,),
        in_specs=[blk, pl.BlockSpec((3, ta, b), lambda r: (0, r, 0))], out_specs=blk,
        compiler_params=_cparams(("parallel",)),
    )(own, recv)


def _all_reduce_small(packed):
    rows, width = packed.shape

    def body(x_ref, out_ref, gath, send_sems, recv_sems, local_sem):
        x, y, c, chips = _position()
        me, sibling = (x, y, c), (x, y, 1 - c)

        def slot(px, py, pc):
            return gath.at[4 * px + 2 * py + pc]

        def copy(k, block, to, src=None):
            return pltpu.make_async_remote_copy(
                src_ref=slot(*block) if src is None else src, dst_ref=slot(*block),
                send_sem=send_sems.at[k], recv_sem=recv_sems.at[k], device_id=to, device_id_type=MESH_ID)

        mine = pltpu.make_async_copy(x_ref, slot(*me), local_sem)
        mine.start()
        first = [copy(0, me, sibling, src=x_ref)]
        first += [copy(1 + j, me, (*chip, c), src=x_ref) for j, chip in enumerate(chips)]
        for cp in first:
            cp.start()
        passed = [copy(4 + j, (*chip, c), sibling) for j, chip in enumerate(chips)]
        for j, chip in enumerate(chips):
            copy(1 + j, (*chip, c), me).wait_recv()
            passed[j].start()
        copy(0, sibling, me).wait_recv()
        for j, chip in enumerate(chips):
            copy(4 + j, (*chip, 1 - c), me).wait_recv()
        for cp in first + passed:
            cp.wait_send()
        mine.wait()
        acc = gath[0]
        for d in range(1, N_DEV):
            acc = acc + gath[d]
        out_ref[...] = acc

    return pl.pallas_call(
        body, name="all_reduce_small",
        out_shape=jax.ShapeDtypeStruct((rows, width), F32),
        in_specs=[pl.BlockSpec(memory_space=pltpu.VMEM)],
        out_specs=pl.BlockSpec(memory_space=pltpu.VMEM),
        scratch_shapes=[pltpu.VMEM((N_DEV, rows, width), F32), pltpu.SemaphoreType.DMA((7,)),
                        pltpu.SemaphoreType.DMA((7,)), pltpu.SemaphoreType.DMA],
    )(packed)


def _mixer_forward(x0, w, s, dep=None):
    h = _rms_fwd(x0, w["norm1_g"], name="rms1_fwd")
    proj = _matmul(h, w["w_in_t"], mode="nt", out_dtype=BF16, tm=1024, tn=1664, tk=1024, name="proj_fwd",
                   dep=dep)
    oa, states = _gla_fwd(proj, w["w_fg2"], w["b_fg"], w["gla_norm_g"], s=s)
    cb_in = _conv_fwd(proj, w["conv_wb"], s=s)
    ya = _matmul(oa, w["w_oa"], mode="nn", out_dtype=BF16, tm=1024, tn=1024, tk=1024, name="ya_fwd")
    yb = _matmul(cb_in, w["w_ob"], mode="nn", out_dtype=BF16, tm=1024, tn=1024, tk=1024, name="yb_fwd")
    mix = _mix_fwd(proj, ya, yb, s=s)
    x1 = _matmul(mix, w["w_o"], mode="nn", out_dtype=F32, tm=1024, tn=1024, tk=1024, name="wo_fwd", add=x0)
    return x1, dict(x0=x0, h=h, proj=proj, oa=oa, states=states, cb_in=cb_in, ya=ya, yb=yb, mix=mix)


def _ffn_forward(x1, w, s, dep=None):
    h2 = _rms_fwd(x1, w["norm2_g"], name="rms2_fwd")
    gt = _matmul(h2, w["w_gate_t"], mode="nt", out_dtype=BF16, tm=1024, tn=1408, tk=1024, name="ffn_gate_fwd",
                 dep=dep)
    up = _matmul(h2, w["w_up_t"], mode="nt", out_dtype=BF16, tm=1024, tn=1408, tk=1024, name="ffn_up_fwd")
    hid = _swiglu_fwd(gt, up, s=s)
    x2 = _matmul(hid, w["w_ffn_down"], mode="nn", out_dtype=F32, tm=1024, tn=1024, tk=FFN_HIDDEN,
                 name="ffn_out_fwd", add=x1)
    return x2, dict(x1=x1, h2=h2, gt=gt, up=up, hid=hid)


def _ffn_backward_grads(dx2b, w, sv, s):
    g = {}
    dhid = _matmul(dx2b, w["w_ffn_down"], mode="nt", out_dtype=BF16, tm=1024, tn=FFN_HIDDEN, tk=1024,
                   name="ffn_out_bwd")
    g["w_ffn_down"] = _matmul(sv["hid"], dx2b, mode="tn", out_dtype=F32, tm=1408, tn=1024, tk=512,
                              name="ffn_out_wgrad")
    dgt, dup = _swiglu_bwd(sv["gt"], sv["up"], dhid, s=s)
    g["w_gate_t"] = _matmul(dgt, sv["h2"], mode="tn", out_dtype=F32, tm=1408, tn=1024, tk=512,
                            name="ffn_gate_wgrad")
    g["w_up_t"] = _matmul(dup, sv["h2"], mode="tn", out_dtype=F32, tm=1408, tn=1024, tk=512,
                          name="ffn_up_wgrad")
    return g, dgt, dup, dhid


def _ffn_backward_input(dgt, dup, dx2, w, sv, dep=None):
    dh2 = _matmul(dgt, w["w_gate_t"], mode="nn", out_dtype=F32, tm=1024, tn=1024, tk=FFN_HIDDEN,
                  name="ffn_gate_bwd", dep=dep)
    dh2 = _matmul(dup, w["w_up_t"], mode="nn", out_dtype=F32, tm=1024, tn=1024, tk=FFN_HIDDEN,
                  name="ffn_up_bwd", add=dh2)
    return _rms_bwd(sv["x1"], w["norm2_g"], dh2, dx2, name="rms2_bwd")


def _mixer_backward_branches(dx1b, w, sv, s):
    g = {}
    dmix = _matmul(dx1b, w["w_o"], mode="nt", out_dtype=BF16, tm=1024, tn=1024, tk=1024, name="wo_bwd")
    g["w_o"] = _matmul(sv["mix"], dx1b, mode="tn", out_dtype=F32, tm=1024, tn=1024, tk=512, name="wo_wgrad")
    dya, dyb, dproj = _mix_bwd(sv["proj"], sv["ya"], sv["yb"], dmix, s=s)
    dcb = _matmul(dyb, w["w_ob"], mode="nt", out_dtype=BF16, tm=1024, tn=1024, tk=1024, name="yb_bwd")
    g["w_ob"] = _matmul(sv["cb_in"], dyb, mode="tn", out_dtype=F32, tm=1024, tn=1024, tk=512, name="yb_wgrad")
    doa = _matmul(dya, w["w_oa"], mode="nt", out_dtype=BF16, tm=1024, tn=1024, tk=1024, name="ya_bwd")
    g["w_oa"] = _matmul(sv["oa"], dya, mode="tn", out_dtype=F32, tm=1024, tn=1024, tk=512, name="ya_wgrad")
    dproj, g["conv_wb"] = _conv_bwd(sv["proj"], dcb, w["conv_wb"], dproj, s=s)
    dproj, dfz, g["gla_norm_g"], g["w_fg2"], g["b_fg"] = _gla_bwd(
        sv["proj"], doa, sv["states"], w["w_fg2"], w["b_fg"], w["gla_norm_g"], dproj, s=s)
    return g, _put_fz(dproj, dfz, s=s)


def _mixer_backward_proj(dproj, w, sv, dep=None):
    g_w_in_t = _matmul(dproj, sv["h"], mode="tn", out_dtype=F32, tm=1664, tn=1024, tk=512, name="proj_wgrad",
                       dep=dep)
    dh = _matmul(dproj, w["w_in_t"], mode="nn", out_dtype=F32, tm=1024, tn=1024, tk=1664, name="proj_bwd")
    return g_w_in_t, dh


def _cols_from_chips(t):
    return jnp.transpose(t, (1, 0, 2)).reshape(t.shape[1], -1)


W_IN_ROWS = IN_WIDTH // N_CHIPS
W_IN_ROWS_PAD = -(-W_IN_ROWS // BF16_SUBLANES) * BF16_SUBLANES


def _w_in_t_shard(t):
    return jnp.pad(jnp.transpose(t, (0, 2, 1)), ((0, 0), (0, W_IN_ROWS_PAD - W_IN_ROWS), (0, 0)))


def _w_in_t_unshard(t):
    return jnp.transpose(t[:, :W_IN_ROWS], (0, 2, 1))


def _w_in_t_to_kernel(t):
    full = t[:, :W_IN_ROWS].reshape(IN_WIDTH, D_MODEL)
    pad = jnp.zeros((FZ_PAD - GATE_RANK, D_MODEL), t.dtype)
    return jnp.concatenate([full[:FZ_ORIG], full[FZ_ORIG + GATE_RANK:], full[FZ_ORIG:FZ_ORIG + GATE_RANK], pad],
                           axis=0)


def _w_in_t_from_kernel(g):
    full = jnp.concatenate([g[:FZ_ORIG], g[COL_FZ:COL_FZ + GATE_RANK], g[FZ_ORIG:COL_FZ]], axis=0)
    return jnp.pad(full.reshape(N_CHIPS, W_IN_ROWS, D_MODEL), ((0, 0), (0, W_IN_ROWS_PAD - W_IN_ROWS), (0, 0)))


def kernel(x, norm1_g, w_in, w_fg2, b_fg, gla_norm_g, w_oa, conv_w, conv_b, w_ob, w_o, norm2_g, w_ffn_gate, w_ffn_up, w_ffn_down, final_g, loss_target, m_norm1_g, m_w_in, m_w_fg2, m_b_fg, m_gla_norm_g, m_w_oa, m_conv_w, m_conv_b, m_w_ob, m_w_o, m_norm2_g, m_w_ffn_gate, m_w_ffn_up, m_w_ffn_down, m_final_g, v_norm1_g, v_w_in, v_w_fg2, v_b_fg, v_gla_norm_g, v_w_oa, v_conv_w, v_conv_b, v_w_ob, v_w_o, v_norm2_g, v_w_ffn_gate, v_w_ffn_up, v_w_ffn_down, v_final_g):
    cx_ = lax.axis_index("x")
    cy_ = lax.axis_index("y")
    cc_ = lax.axis_index("c")
    me = 2 * cx_ + cy_
    on_core0 = jnp.where(cc_ == 0, 1.0, 0.0).astype(F32).reshape(1, 1)

    def swap(t):
        return jnp.swapaxes(t, 1, 2)

    big_names = ["w_in", "w_oa", "w_ob", "w_o", "w_ffn_gate", "w_ffn_up", "w_ffn_down"]
    views = dict(
        w_in=tuple(_w_in_t_shard(t) for t in (w_in, m_w_in, v_w_in)),
        w_oa=(w_oa, m_w_oa, v_w_oa), w_ob=(w_ob, m_w_ob, v_w_ob), w_o=(w_o, m_w_o, v_w_o),
        w_ffn_gate=tuple(swap(t) for t in (w_ffn_gate, m_w_ffn_gate, v_w_ffn_gate)),
        w_ffn_up=tuple(swap(t) for t in (w_ffn_up, m_w_ffn_up, v_w_ffn_up)),
        w_ffn_down=(w_ffn_down, m_w_ffn_down, v_w_ffn_down))
    from_view = dict(w_in=_w_in_t_unshard, w_ffn_gate=swap, w_ffn_up=swap)

    s = x.shape[1]

    conv_w_p = jnp.pad(conv_w, ((0, 0), (0, 8 - conv_w.shape[1]), (0, 0)))
    w_fg2_all, conv_w_all = _all_gather_weights([w_fg2, conv_w_p])

    def shards_of(l):
        return [views[n][0][l].astype(BF16) for n in big_names]

    def layer_weights(l, gathered, shards):
        gw = {n: lax.dynamic_update_index_in_dim(g, t[None], me, axis=0)
              for n, g, t in zip(big_names, gathered, shards)}
        w_fg2_full = _cols_from_chips(w_fg2_all[l])
        conv_w_full = _cols_from_chips(conv_w_all[l])[:3]
        return dict(
            norm1_g=norm1_g[l], norm2_g=norm2_g[l],
            w_in_t=_w_in_t_to_kernel(gw["w_in"]),
            w_fg2=jnp.pad(w_fg2_full, ((0, FZ_PAD - GATE_RANK), (0, 0))).astype(BF16),
            b_fg=b_fg[l].reshape(1, KEY_WIDTH), gla_norm_g=gla_norm_g[l].reshape(1, HEAD_V),
            w_oa=gw["w_oa"].reshape(VAL_WIDTH, D_MODEL),
            w_ob=gw["w_ob"].reshape(CONV_CH, D_MODEL),
            w_o=gw["w_o"].reshape(D_MODEL, D_MODEL),
            conv_wb=(jnp.pad(conv_w_full, ((0, 5), (0, 0)))
                     + jnp.pad(conv_b[l].reshape(1, CONV_CH), ((3, 4), (0, 0)))),
            w_gate_t=gw["w_ffn_gate"].reshape(FFN_HIDDEN, D_MODEL),
            w_up_t=gw["w_ffn_up"].reshape(FFN_HIDDEN, D_MODEL),
            w_ffn_down=gw["w_ffn_down"].reshape(FFN_HIDDEN, D_MODEL))

    grad_key = dict(w_in="w_in_t", w_ffn_gate="w_gate_t", w_ffn_up="w_up_t")

    def per_chip(g):
        out = []
        for n in big_names:
            t = g[grad_key.get(n, n)]
            out.append(_w_in_t_from_kernel(t) if n == "w_in"
                       else t.reshape(N_CHIPS, t.shape[0] // N_CHIPS, t.shape[1]))
        return out

    sh0 = shards_of(0)
    started = _gather_start(0, sh0, name="gather0_start")
    gathered = _gather_wait(0, started, started[3], name="gather0_wait")
    started = _handoff_start(0, gathered, name="handoff0_start")
    gathered = _handoff_wait(0, started, started[3], name="handoff0_wait")
    w0 = layer_weights(0, gathered, sh0)

    sh1 = shards_of(1)
    started = _gather_start(1, sh1, name="gather1_start")
    x1, sv0m = _mixer_forward(x[0], w0, s, dep=started[3])
    gathered = _gather_wait(1, started, x1, name="gather1_wait")
    started = _handoff_start(1, gathered, name="handoff1_start")
    x2, sv0f = _ffn_forward(x1, w0, s, dep=started[3])
    gathered = _handoff_wait(1, started, x2, name="handoff1_wait")
    w1 = layer_weights(1, gathered, sh1)

    x3, sv1m = _mixer_forward(x2, w1, s)
    x4, sv1f = _ffn_forward(x3, w1, s)
    loss_local, dx, dxb, dgf = _loss_head(x4, final_g, loss_target[0])
    loss = lax.psum(loss_local[0, 0], ("x", "y", "c"))

    g1, dgt, dup, _ = _ffn_backward_grads(dxb, w1, sv1f, s)
    dx_mid, dxb_mid, g1["norm2_g"] = _ffn_backward_input(dgt, dup, dx, w1, sv1f)
    gm, dproj = _mixer_backward_branches(dxb_mid, w1, sv1m, s)
    g1.update(gm)
    g1["w_in_t"], dh = _mixer_backward_proj(dproj, w1, sv1m)
    fed = _sibling_push_start(1, False, per_chip(g1), name="feed1_start")
    dx, dxb, g1["norm1_g"] = _rms_bwd(sv1m["x0"], norm1_g[1] + fed[3][0, 0], dh, dx_mid, name="rms1_bwd")

    g0, dgt, dup, dhid = _ffn_backward_grads(dxb, w0, sv0f, s)
    mine, sib = _sibling_push_wait(1, False, fed, dhid, name="feed1_wait")
    parts = [_pair_sum(a, b, name="pair_sum1_" + n) for n, a, b in zip(big_names, mine, sib)]
    swapped = _chip_exchange_start(1, parts, name="exchange1_start")
    dx_mid, dxb_mid, g0["norm2_g"] = _ffn_backward_input(dgt, dup, dx, w0, sv0f, dep=swapped[3])
    gm, dproj = _mixer_backward_branches(dxb_mid, w0, sv0m, s)
    g0.update(gm)
    parts, recvs = _chip_exchange_wait(1, swapped, dproj, name="exchange1_wait")
    owns = [lax.dynamic_index_in_dim(p, me, 0, keepdims=False) for p in parts]
    totals1 = [_total_sum(o, rc, name="total1_" + n) for n, o, rc in zip(big_names, owns, recvs)]
    shared = _sibling_push_start(1, True, totals1, name="share1_start")
    g0["w_in_t"], dh = _mixer_backward_proj(dproj, w0, sv0m, dep=shared[3])
    totals1, others1 = _sibling_push_wait(1, True, shared, dh, name="share1_wait")
    grad_x, _, g0["norm1_g"] = _rms_bwd(sv0m["x0"], norm1_g[0], dh, dx_mid, name="rms1_bwd")

    fed = _sibling_push_start(0, False, per_chip(g0), name="feed0_start")
    mine, sib = _sibling_push_wait(0, False, fed, fed[3], name="feed0_wait")
    parts = [_pair_sum(a, b, name="pair_sum0_" + n) for n, a, b in zip(big_names, mine, sib)]
    swapped = _chip_exchange_start(0, parts, name="exchange0_start")
    parts, recvs = _chip_exchange_wait(0, swapped, swapped[3], name="exchange0_wait")
    owns = [lax.dynamic_index_in_dim(p, me, 0, keepdims=False) for p in parts]
    totals0 = [_total_sum(o, rc, name="total0_" + n) for n, o, rc in zip(big_names, owns, recvs)]
    shared = _sibling_push_start(0, True, totals0, name="share0_start")
    totals0, others0 = _sibling_push_wait(0, True, shared, shared[3], name="share0_wait")
    grads = [g0, g1]

    def small_rows(t):
        return t.reshape(-1, D_MODEL)

    def tile_rows(t):
        return jnp.pad(t, ((0, -t.shape[0] % F32_SUBLANES), (0, 0)))

    g0, g1 = grads
    pieces = [
        jnp.concatenate([g0["norm1_g"], g1["norm1_g"]], axis=0),
        jnp.concatenate([g0["norm2_g"], g1["norm2_g"]], axis=0),
        dgf,
        small_rows(jnp.concatenate([g0["b_fg"], g1["b_fg"]], axis=1)),
        small_rows(jnp.concatenate([g0["gla_norm_g"], g1["gla_norm_g"],
                                    jnp.zeros((1, D_MODEL - 2 * HEAD_V), F32)], axis=1)),
        jnp.concatenate([g0["conv_wb"][3:4], g1["conv_wb"][3:4]], axis=0),
        jnp.concatenate([g0["conv_wb"][:3], g1["conv_wb"][:3]], axis=0),
        small_rows(jnp.stack([g0["w_fg2"][:GATE_RANK], g1["w_fg2"][:GATE_RANK]])),
    ]
    small = _all_reduce_small(jnp.concatenate([tile_rows(p) for p in pieces], axis=0))
    sg = dict(
        norm1_g=small[0:2], norm2_g=small[8:10], final_g=small[16],
        b_fg=small[24].reshape(DEPTH, KEY_WIDTH), gla_norm_g=small[32, :DEPTH * HEAD_V].reshape(DEPTH, HEAD_V),
        conv_b=small[40:42],
        conv_w=lax.dynamic_slice_in_dim(small[48:54].reshape(DEPTH, 3, CONV_CH), me * (CONV_CH // N_CHIPS),
                                        CONV_CH // N_CHIPS, axis=2),
        w_fg2=lax.dynamic_slice_in_dim(small[56:72].reshape(DEPTH, GATE_RANK, KEY_WIDTH),
                                       me * (KEY_WIDTH // N_CHIPS), KEY_WIDTH // N_CHIPS, axis=2),
    )

    small_params = dict(norm1_g=(norm1_g, m_norm1_g, v_norm1_g), w_fg2=(w_fg2, m_w_fg2, v_w_fg2),
                        b_fg=(b_fg, m_b_fg, v_b_fg), gla_norm_g=(gla_norm_g, m_gla_norm_g, v_gla_norm_g),
                        conv_w=(conv_w, m_conv_w, v_conv_w), conv_b=(conv_b, m_conv_b, v_conv_b),
                        norm2_g=(norm2_g, m_norm2_g, v_norm2_g), final_g=(final_g, m_final_g, v_final_g))
    order = ["norm1_g", "w_in", "w_fg2", "b_fg", "gla_norm_g", "w_oa", "conv_w", "conv_b", "w_ob", "w_o",
             "norm2_g", "w_ffn_gate", "w_ffn_up", "w_ffn_down", "final_g"]
    results = {}
    for i, n in enumerate(big_names):
        w_, m_, v_ = views[n]
        outs = _adamw_layers(w_, (totals0[i], totals1[i]), (others0[i], others1[i]), on_core0, m_, v_,
                             name="adamw_" + n)
        back = from_view.get(n)
        results[n] = tuple(back(r) for r in outs) if back else outs
    for n, (w_, m_, v_) in small_params.items():
        results[n] = (sg[n],) + _adamw(w_, sg[n], m_, v_, name="adamw_" + n)
    return (loss, grad_x[None], *[results[n][0] for n in order], *[results[n][1] for n in order],
            *[results[n][2] for n in order], *[results[n][3] for n in order])
```

```python
import functools

import jax
import jax.numpy as jnp
from jax import lax
from jax.experimental import pallas as pl
from jax.experimental.pallas import tpu as pltpu

F32 = jnp.float32
BF16 = jnp.bfloat16

D_MODEL = 1024
DEPTH = 2
CHUNK = 64
GLA_HEADS = 4
KEY_WIDTH = 512
VAL_WIDTH = 1024
HEAD_K = 128
HEAD_V = 256
GATE_RANK = 16
GATE_TAU = 16.0
CONV_CH = 1024
FFN_HIDDEN = 2816
IN_WIDTH = 8208
NORM_EPS = 1e-6
Q_SCALE = HEAD_K ** -0.5
ADAM_LR = 0.001
ADAM_B1 = 0.9
ADAM_B2 = 0.999
ADAM_EPS = 1e-08
ADAM_WD = 0.01
ADAM_STEP = 10

N_CHIPS = 4
N_DEV = 8

LANE = 128
FZ_PAD = LANE
PROJ_W = 8192 + FZ_PAD
COL_Q, COL_K, COL_V, COL_R, COL_GBI, COL_GCI, COL_CX, COL_GA, COL_GB, COL_FZ = (
    0, 512, 1024, 2048, 3072, 4096, 5120, 6144, 7168, 8192)
FZ_ORIG = 3072

VMEM_LIMIT = 52 * 1024 * 1024
HALO = 16


def _cparams(sem=None):
    return pltpu.CompilerParams(dimension_semantics=sem, vmem_limit_bytes=VMEM_LIMIT)


def _sigmoid(x):
    return jax.nn.sigmoid(x)


def _log_sigmoid(x):
    return jnp.minimum(x, 0.0) - jnp.log1p(jnp.exp(-jnp.abs(x)))


_DOT_DIMS = {
    "nn": (((1,), (0,)), ((), ())),
    "nt": (((1,), (1,)), ((), ())),
    "tn": (((0,), (0,)), ((), ())),
}


def _matmul(a, b, *, mode, out_dtype, tm, tn, tk, name, add=None, dep=None):
    if mode == "nn":
        (m, k), n = a.shape, b.shape[1]
    elif mode == "nt":
        (m, k), n = a.shape, b.shape[0]
    else:
        (k, m), n = a.shape, b.shape[1]
    tm, tn, tk = min(tm, m), min(tn, n), min(tk, k)
    assert m % tm == 0 and n % tn == 0 and k % tk == 0, (name, m, n, k, tm, tn, tk)
    nk = k // tk
    has_add = add is not None
    has_dep = dep is not None

    def body(*refs):
        if has_dep:
            refs = refs[1:]
        if has_add:
            a_ref, b_ref, add_ref, o_ref = refs[:4]
            scratch = refs[4:]
        else:
            a_ref, b_ref, o_ref = refs[:3]
            add_ref = None
            scratch = refs[3:]
        part = lax.dot_general(a_ref[...], b_ref[...], _DOT_DIMS[mode], preferred_element_type=F32)

        def finish(acc):
            if add_ref is not None:
                acc = acc + add_ref[...].astype(F32)
            o_ref[...] = acc.astype(o_ref.dtype)

        if nk == 1:
            finish(part)
        else:
            acc_ref = scratch[0]
            kk = pl.program_id(2)

            @pl.when(kk == 0)
            def _():
                acc_ref[...] = part

            @pl.when(kk > 0)
            def _():
                acc_ref[...] += part

            @pl.when(kk == nk - 1)
            def _():
                finish(acc_ref[...])

    if mode == "nn":
        a_spec = pl.BlockSpec((tm, tk), lambda i, j, kk: (i, kk))
        b_spec = pl.BlockSpec((tk, tn), lambda i, j, kk: (kk, j))
    elif mode == "nt":
        a_spec = pl.BlockSpec((tm, tk), lambda i, j, kk: (i, kk))
        b_spec = pl.BlockSpec((tn, tk), lambda i, j, kk: (j, kk))
    else:
        a_spec = pl.BlockSpec((tk, tm), lambda i, j, kk: (kk, i))
        b_spec = pl.BlockSpec((tk, tn), lambda i, j, kk: (kk, j))
    o_spec = pl.BlockSpec((tm, tn), lambda i, j, kk: (i, j))
    in_specs = [a_spec, b_spec] + ([o_spec] if has_add else [])
    operands = (a, b) + ((add,) if has_add else ())
    if has_dep:
        in_specs = [pl.BlockSpec(dep.shape, lambda i, j, kk: (0, 0))] + in_specs
        operands = (dep,) + operands
    return pl.pallas_call(
        body,
        name=name,
        out_shape=jax.ShapeDtypeStruct((m, n), out_dtype),
        grid=(m // tm, n // tn, nk),
        in_specs=in_specs,
        out_specs=o_spec,
        scratch_shapes=[pltpu.VMEM((tm, tn), F32)] if nk > 1 else [],
        compiler_params=_cparams(("parallel", "parallel", "arbitrary")),
    )(*operands)


def _rms_fwd(x, g, *, name):
    s, d = x.shape
    tr = min(512, s)

    def body(x_ref, g_ref, h_ref):
        xv = x_ref[...]
        rs = lax.rsqrt(jnp.mean(xv * xv, axis=-1, keepdims=True) + NORM_EPS)
        h_ref[...] = (xv * rs * g_ref[...]).astype(BF16)

    return pl.pallas_call(
        body, name=name,
        out_shape=jax.ShapeDtypeStruct((s, d), BF16),
        grid=(s // tr,),
        in_specs=[pl.BlockSpec((tr, d), lambda i: (i, 0)), pl.BlockSpec((1, d), lambda i: (0, 0))],
        out_specs=pl.BlockSpec((tr, d), lambda i: (i, 0)),
        compiler_params=_cparams(("parallel",)),
    )(x, g.reshape(1, d))


def _rms_bwd(x, g, dh, dres, *, name):
    s, d = x.shape
    tr = min(512, s)

    def body(x_ref, g_ref, dh_ref, dres_ref, dx_ref, dxb_ref, dg_ref):
        i = pl.program_id(0)
        xv = x_ref[...]
        rs = lax.rsqrt(jnp.mean(xv * xv, axis=-1, keepdims=True) + NORM_EPS)
        n = xv * rs
        dhv = dh_ref[...].astype(F32)
        dn = dhv * g_ref[...]
        dx = dres_ref[...] + rs * (dn - n * jnp.mean(dn * n, axis=-1, keepdims=True))
        dx_ref[...] = dx
        dxb_ref[...] = dx.astype(BF16)
        part = jnp.sum(dhv * n, axis=0, keepdims=True)

        @pl.when(i == 0)
        def _():
            dg_ref[...] = part

        @pl.when(i > 0)
        def _():
            dg_ref[...] += part

    row = pl.BlockSpec((tr, d), lambda i: (i, 0))
    vec = pl.BlockSpec((1, d), lambda i: (0, 0))
    return pl.pallas_call(
        body, name=name,
        out_shape=(jax.ShapeDtypeStruct((s, d), F32), jax.ShapeDtypeStruct((s, d), BF16),
                   jax.ShapeDtypeStruct((1, d), F32)),
        grid=(s // tr,),
        in_specs=[row, vec, row, row],
        out_specs=(row, row, vec),
        compiler_params=_cparams(("arbitrary",)),
    )(x, g.reshape(1, d), dh, dres)


def _loss_head(x, g, target):
    s, d = x.shape
    tr = min(512, s)

    def body(x_ref, g_ref, t_ref, loss_ref, dx_ref, dxb_ref, dg_ref):
        i = pl.program_id(0)
        xv = x_ref[...]
        rs = lax.rsqrt(jnp.mean(xv * xv, axis=-1, keepdims=True) + NORM_EPS)
        n = xv * rs
        gv = g_ref[...]
        err = n * gv - t_ref[...]
        row_loss = jnp.mean(err * err, axis=-1, keepdims=True)
        loss_part = 0.5 * jnp.sum(row_loss, axis=0, keepdims=True)
        dy = err * (1.0 / d)
        dn = dy * gv
        dx = rs * (dn - n * jnp.mean(dn * n, axis=-1, keepdims=True))
        dx_ref[...] = dx
        dxb_ref[...] = dx.astype(BF16)
        dg_part = jnp.sum(dy * n, axis=0, keepdims=True)

        @pl.when(i == 0)
        def _():
            loss_ref[...] = loss_part
            dg_ref[...] = dg_part

        @pl.when(i > 0)
        def _():
            loss_ref[...] += loss_part
            dg_ref[...] += dg_part

    row = pl.BlockSpec((tr, d), lambda i: (i, 0))
    vec = pl.BlockSpec((1, d), lambda i: (0, 0))
    one = pl.BlockSpec((1, 1), lambda i: (0, 0))
    return pl.pallas_call(
        body, name="loss_head",
        out_shape=(jax.ShapeDtypeStruct((1, 1), F32), jax.ShapeDtypeStruct((s, d), F32),
                   jax.ShapeDtypeStruct((s, d), BF16), jax.ShapeDtypeStruct((1, d), F32)),
        grid=(s // tr,),
        in_specs=[row, vec, row],
        out_specs=(one, row, row, vec),
        compiler_params=_cparams(("arbitrary",)),
    )(x, g.reshape(1, d), target)


def _conv_taps(u_prev, u, w_ref, rows):
    ext = jnp.concatenate([u_prev, u], axis=0)
    u1 = pltpu.roll(ext, 1, 0)[HALO:HALO + rows]
    u2 = pltpu.roll(ext, 2, 0)[HALO:HALO + rows]
    conv = w_ref[0:1, :] * u2 + w_ref[1:2, :] * u1 + w_ref[2:3, :] * u + w_ref[3:4, :]
    return conv, u1, u2


def _conv_fwd(proj, conv_wb, *, s):
    tr = min(512, s)
    c = CONV_CH
    hb = tr // HALO

    def body(gbi_ref, gci_ref, cx_ref, gci_h_ref, cx_h_ref, w_ref, o_ref):
        i = pl.program_id(0)
        u = gci_ref[...].astype(F32) * cx_ref[...].astype(F32)
        u_prev = gci_h_ref[...].astype(F32) * cx_h_ref[...].astype(F32)
        u_prev = jnp.where(i == 0, 0.0, u_prev)
        conv, _, _ = _conv_taps(u_prev, u, w_ref, tr)
        o_ref[...] = (gbi_ref[...].astype(F32) * conv).astype(BF16)

    def seg(col):
        return pl.BlockSpec((tr, c), lambda i: (i, col // c))

    def halo(col):
        return pl.BlockSpec((HALO, c), lambda i: (jnp.maximum(i * hb - 1, 0), col // c))

    return pl.pallas_call(
        body, name="conv_fwd",
        out_shape=jax.ShapeDtypeStruct((s, c), BF16),
        grid=(s // tr,),
        in_specs=[seg(COL_GBI), seg(COL_GCI), seg(COL_CX), halo(COL_GCI), halo(COL_CX),
                  pl.BlockSpec((8, c), lambda i: (0, 0))],
        out_specs=pl.BlockSpec((tr, c), lambda i: (i, 0)),
        compiler_params=_cparams(("parallel",)),
    )(proj, proj, proj, proj, proj, conv_wb)


def _conv_bwd(proj, dcb, conv_wb, dproj, *, s):
    tr = min(512, s)
    c = CONV_CH
    hb = tr // HALO
    nb = s // tr

    def body(gbi_ref, gci_ref, cx_ref, dcb_ref, gci_h_ref, cx_h_ref, gbi_n_ref, dcb_n_ref, w_ref,
             dproj_in_ref, dp_ref, dw_ref):
        i = pl.program_id(0)
        gbi = gbi_ref[...].astype(F32)
        gci = gci_ref[...].astype(F32)
        cx = cx_ref[...].astype(F32)
        dcb_v = dcb_ref[...].astype(F32)
        u = gci * cx
        u_prev = jnp.where(i == 0, 0.0, gci_h_ref[...].astype(F32) * cx_h_ref[...].astype(F32))
        conv, u1, u2 = _conv_taps(u_prev, u, w_ref, tr)
        dconv = dcb_v * gbi
        dconv_next = jnp.where(i == nb - 1, 0.0, dcb_n_ref[...].astype(F32) * gbi_n_ref[...].astype(F32))
        ext = jnp.concatenate([dconv, dconv_next], axis=0)
        n_ext = tr + HALO
        d1 = pltpu.roll(ext, n_ext - 1, 0)[0:tr]
        d2 = pltpu.roll(ext, n_ext - 2, 0)[0:tr]
        du = w_ref[2:3, :] * dconv + w_ref[1:2, :] * d1 + w_ref[0:1, :] * d2
        dp_ref[:, 0:c] = (dcb_v * conv).astype(BF16)
        dp_ref[:, c:2 * c] = (du * cx).astype(BF16)
        dp_ref[:, 2 * c:3 * c] = (du * gci).astype(BF16)
        part = jnp.concatenate([
            jnp.sum(dconv * u2, axis=0, keepdims=True),
            jnp.sum(dconv * u1, axis=0, keepdims=True),
            jnp.sum(dconv * u, axis=0, keepdims=True),
            jnp.sum(dconv, axis=0, keepdims=True),
            jnp.zeros((4, c), F32)], axis=0)

        @pl.when(i == 0)
        def _():
            dw_ref[...] = part

        @pl.when(i > 0)
        def _():
            dw_ref[...] += part

    def seg(col):
        return pl.BlockSpec((tr, c), lambda i: (i, col // c))

    def halo_prev(col):
        return pl.BlockSpec((HALO, c), lambda i: (jnp.maximum(i * hb - 1, 0), col // c))

    def halo_next(col):
        return pl.BlockSpec((HALO, c), lambda i: (jnp.minimum((i + 1) * hb, nb * hb - 1), col // c))

    return pl.pallas_call(
        body, name="conv_bwd",
        out_shape=(jax.ShapeDtypeStruct((s, PROJ_W), BF16), jax.ShapeDtypeStruct((8, c), F32)),
        grid=(nb,),
        in_specs=[seg(COL_GBI), seg(COL_GCI), seg(COL_CX), pl.BlockSpec((tr, c), lambda i: (i, 0)),
                  halo_prev(COL_GCI), halo_prev(COL_CX), halo_next(COL_GBI),
                  pl.BlockSpec((HALO, c), lambda i: (jnp.minimum((i + 1) * hb, nb * hb - 1), 0)),
                  pl.BlockSpec((8, c), lambda i: (0, 0)), ANY],
        out_specs=(pl.BlockSpec((tr, 3 * c), lambda i: (i, COL_GBI // (3 * c))),
                   pl.BlockSpec((8, c), lambda i: (0, 0))),
        input_output_aliases={9: 0},
        compiler_params=_cparams(("arbitrary",)),
    )(proj, proj, proj, dcb, proj, proj, proj, dcb, conv_wb, dproj)


def _mix_fwd(proj, ya, yb, *, s):
    tr = min(512, s)
    d = D_MODEL

    def body(ga_ref, gb_ref, ya_ref, yb_ref, o_ref):
        sa = _sigmoid(ga_ref[...].astype(F32))
        sb = _sigmoid(gb_ref[...].astype(F32))
        o_ref[...] = (sa * ya_ref[...].astype(F32) + sb * yb_ref[...].astype(F32)).astype(BF16)

    row = pl.BlockSpec((tr, d), lambda i: (i, 0))
    return pl.pallas_call(
        body, name="mix_fwd",
        out_shape=jax.ShapeDtypeStruct((s, d), BF16),
        grid=(s // tr,),
        in_specs=[pl.BlockSpec((tr, d), lambda i: (i, COL_GA // d)),
                  pl.BlockSpec((tr, d), lambda i: (i, COL_GB // d)), row, row],
        out_specs=row,
        compiler_params=_cparams(("parallel",)),
    )(proj, proj, ya, yb)


def _mix_bwd(proj, ya, yb, dmix, *, s):
    tr = min(512, s)
    d = D_MODEL

    def body(ga_ref, gb_ref, ya_ref, yb_ref, dm_ref, dya_ref, dyb_ref, dg_ref):
        sa = _sigmoid(ga_ref[...].astype(F32))
        sb = _sigmoid(gb_ref[...].astype(F32))
        dm = dm_ref[...].astype(F32)
        dya_ref[...] = (dm * sa).astype(BF16)
        dyb_ref[...] = (dm * sb).astype(BF16)
        dg_ref[:, 0:d] = (dm * ya_ref[...].astype(F32) * sa * (1.0 - sa)).astype(BF16)
        dg_ref[:, d:2 * d] = (dm * yb_ref[...].astype(F32) * sb * (1.0 - sb)).astype(BF16)

    row = pl.BlockSpec((tr, d), lambda i: (i, 0))
    return pl.pallas_call(
        body, name="mix_bwd",
        out_shape=(jax.ShapeDtypeStruct((s, d), BF16), jax.ShapeDtypeStruct((s, d), BF16),
                   jax.ShapeDtypeStruct((s, PROJ_W), BF16)),
        grid=(s // tr,),
        in_specs=[pl.BlockSpec((tr, d), lambda i: (i, COL_GA // d)),
                  pl.BlockSpec((tr, d), lambda i: (i, COL_GB // d)), row, row, row],
        out_specs=(row, row, pl.BlockSpec((tr, 2 * d), lambda i: (i, COL_GA // (2 * d)))),
        compiler_params=_cparams(("parallel",)),
    )(proj, proj, ya, yb, dmix)


def _swiglu_fwd(gt, up, *, s):
    tr = min(256, s)
    f = FFN_HIDDEN

    def body(gt_ref, up_ref, o_ref):
        gv = gt_ref[...].astype(F32)
        o_ref[...] = (gv * _sigmoid(gv) * up_ref[...].astype(F32)).astype(BF16)

    row = pl.BlockSpec((tr, f), lambda i: (i, 0))
    return pl.pallas_call(
        body, name="swiglu_fwd",
        out_shape=jax.ShapeDtypeStruct((s, f), BF16),
        grid=(s // tr,),
        in_specs=[row, row], out_specs=row,
        compiler_params=_cparams(("parallel",)),
    )(gt, up)


def _swiglu_bwd(gt, up, dhid, *, s):
    tr = min(256, s)
    f = FFN_HIDDEN

    def body(gt_ref, up_ref, dh_ref, dgt_ref, dup_ref):
        gv = gt_ref[...].astype(F32)
        uv = up_ref[...].astype(F32)
        dh = dh_ref[...].astype(F32)
        sg = _sigmoid(gv)
        dgt_ref[...] = (dh * uv * sg * (1.0 + gv * (1.0 - sg))).astype(BF16)
        dup_ref[...] = (dh * gv * sg).astype(BF16)

    row = pl.BlockSpec((tr, f), lambda i: (i, 0))
    return pl.pallas_call(
        body, name="swiglu_bwd",
        out_shape=(jax.ShapeDtypeStruct((s, f), BF16), jax.ShapeDtypeStruct((s, f), BF16)),
        grid=(s // tr,),
        in_specs=[row, row, row], out_specs=(row, row),
        compiler_params=_cparams(("parallel",)),
    )(gt, up, dhid)


def _tri(strict):
    r = lax.broadcasted_iota(jnp.int32, (CHUNK, CHUNK), 0)
    c = lax.broadcasted_iota(jnp.int32, (CHUNK, CHUNK), 1)
    return jnp.where((c < r) if strict else (c <= r), 1.0, 0.0).astype(F32)


def _gla_gate_terms(la_c, tri):
    cum = jnp.dot(tri, la_c, precision=lax.Precision.HIGHEST, preferred_element_type=F32)
    cend = cum[CHUNK - 1:CHUNK, :]
    return jnp.exp(cend - cum), jnp.exp(cend)


def _hk(h):
    return slice(h * HEAD_K, (h + 1) * HEAD_K)


def _hv(h):
    return slice(h * HEAD_V, (h + 1) * HEAD_V)


def _gla_fwd(proj, wfg2p, bfg, gn, *, s):
    tr = min(512, s)
    nb = s // tr
    nc = tr // CHUNK

    def body(q_ref, k_ref, v_ref, r_ref, fz_ref, w_ref, b_ref, gn_ref, oa_ref, st_ref,
             state, la_scr, o_scr):
        i = pl.program_id(0)

        @pl.when(i == 0)
        def _():
            state[...] = jnp.zeros_like(state)

        fg = jnp.dot(fz_ref[...], w_ref[...], preferred_element_type=F32) + b_ref[...]
        la_scr[...] = _log_sigmoid(fg) * (1.0 / GATE_TAU)
        tri = _tri(False)

        def chunk(ci, carry):
            r0 = pl.multiple_of(ci * CHUNK, CHUNK)
            rows = pl.ds(r0, CHUNK)
            e, gam = _gla_gate_terms(la_scr[rows, :], tri)
            kd = (k_ref[rows, :].astype(F32) * e).astype(BF16)
            qs = (q_ref[rows, :].astype(F32) * Q_SCALE).astype(BF16)
            v_c = v_ref[rows, :]
            for h in range(GLA_HEADS):
                upd = lax.dot_general(v_c[:, _hv(h)], kd[:, _hk(h)], _DOT_DIMS["tn"],
                                      preferred_element_type=F32)
                st_h = state[:, _hk(h)] * gam[:, _hk(h)] + upd
                state[:, _hk(h)] = st_h
                o_scr[rows, _hv(h)] = lax.dot_general(qs[:, _hk(h)], st_h.astype(BF16), _DOT_DIMS["nt"],
                                                       preferred_element_type=F32)
            st_ref[ci] = state[...]
            return carry

        lax.fori_loop(0, nc, chunk, 0)
        for h in range(GLA_HEADS):
            o = o_scr[:, _hv(h)]
            rs = lax.rsqrt(jnp.mean(o * o, axis=-1, keepdims=True) + NORM_EPS)
            rv = r_ref[:, _hv(h)].astype(F32)
            oa_ref[:, _hv(h)] = ((o * rs * gn_ref[...]).astype(F32) * (rv * _sigmoid(rv))).astype(BF16)

    return pl.pallas_call(
        body, name="gla_fwd",
        out_shape=(jax.ShapeDtypeStruct((s, VAL_WIDTH), BF16),
                   jax.ShapeDtypeStruct((s // CHUNK, HEAD_V, KEY_WIDTH), F32)),
        grid=(nb,),
        in_specs=[pl.BlockSpec((tr, KEY_WIDTH), lambda i: (i, COL_Q // KEY_WIDTH)),
                  pl.BlockSpec((tr, KEY_WIDTH), lambda i: (i, COL_K // KEY_WIDTH)),
                  pl.BlockSpec((tr, VAL_WIDTH), lambda i: (i, COL_V // VAL_WIDTH)),
                  pl.BlockSpec((tr, VAL_WIDTH), lambda i: (i, COL_R // VAL_WIDTH)),
                  pl.BlockSpec((tr, FZ_PAD), lambda i: (i, COL_FZ // FZ_PAD)),
                  pl.BlockSpec((FZ_PAD, KEY_WIDTH), lambda i: (0, 0)),
                  pl.BlockSpec((1, KEY_WIDTH), lambda i: (0, 0)),
                  pl.BlockSpec((1, HEAD_V), lambda i: (0, 0))],
        out_specs=(pl.BlockSpec((tr, VAL_WIDTH), lambda i: (i, 0)),
                   pl.BlockSpec((nc, HEAD_V, KEY_WIDTH), lambda i: (i, 0, 0))),
        scratch_shapes=[pltpu.VMEM((HEAD_V, KEY_WIDTH), F32), pltpu.VMEM((tr, KEY_WIDTH), F32),
                        pltpu.VMEM((tr, VAL_WIDTH), F32)],
        compiler_params=_cparams(("arbitrary",)),
    )(proj, proj, proj, proj, proj, wfg2p, bfg, gn)


def _gla_bwd(proj, doa, states, wfg2p, bfg, gn, dproj, *, s):
    tr = min(512, s)
    nb = s // tr
    nc = tr // CHUNK

    def body(q_ref, k_ref, v_ref, r_ref, fz_ref, doa_ref, st_ref, stp_ref, w_ref, b_ref, gn_ref, dproj_in_ref,
             dp_ref, dfz_ref, dgn_ref, dw_ref, db_ref,
             carry, fg_scr, la_scr, dla_scr, o_scr, do_scr):
        i = pl.program_id(0)

        @pl.when(i == 0)
        def _():
            carry[...] = jnp.zeros_like(carry)
            dgn_ref[...] = jnp.zeros_like(dgn_ref)
            dw_ref[...] = jnp.zeros_like(dw_ref)
            db_ref[...] = jnp.zeros_like(db_ref)

        fz = fz_ref[...]
        fg = jnp.dot(fz, w_ref[...], preferred_element_type=F32) + b_ref[...]
        fg_scr[...] = fg
        la_scr[...] = _log_sigmoid(fg) * (1.0 / GATE_TAU)

        def fwd_chunk(ci, c0):
            r0 = pl.multiple_of(ci * CHUNK, CHUNK)
            rows = pl.ds(r0, CHUNK)
            qs = (q_ref[rows, :].astype(F32) * Q_SCALE).astype(BF16)
            st = st_ref[ci].astype(BF16)
            for h in range(GLA_HEADS):
                o_scr[rows, _hv(h)] = lax.dot_general(qs[:, _hk(h)], st[:, _hk(h)], _DOT_DIMS["nt"],
                                                       preferred_element_type=F32)
            return c0

        lax.fori_loop(0, nc, fwd_chunk, 0)

        gnv = gn_ref[...]
        dgn_part = jnp.zeros((1, HEAD_V), F32)
        for h in range(GLA_HEADS):
            o = o_scr[:, _hv(h)]
            rs = lax.rsqrt(jnp.mean(o * o, axis=-1, keepdims=True) + NORM_EPS)
            nrm = o * rs
            rv = r_ref[:, _hv(h)].astype(F32)
            sg = _sigmoid(rv)
            doa_h = doa_ref[:, _hv(h)].astype(F32)
            don = doa_h * (rv * sg)
            dp_ref[:, COL_R + h * HEAD_V:COL_R + (h + 1) * HEAD_V] = (
                doa_h * (nrm * gnv) * (sg * (1.0 + rv * (1.0 - sg)))).astype(BF16)
            dgn_part = dgn_part + jnp.sum(don * nrm, axis=0, keepdims=True)
            dn = don * gnv
            do_scr[:, _hv(h)] = (rs * (dn - nrm * jnp.mean(dn * nrm, axis=-1, keepdims=True))).astype(BF16)
        dgn_ref[...] += dgn_part

        tri = _tri(False)
        tri_s = _tri(True)
        first_block = i == nb - 1

        def bwd_chunk(cc, c0):
            ci = nc - 1 - cc
            r0 = pl.multiple_of(ci * CHUNK, CHUNK)
            rows = pl.ds(r0, CHUNK)
            e, gam = _gla_gate_terms(la_scr[rows, :], tri)
            k_c = k_ref[rows, :].astype(F32)
            kd = k_c * e
            kd_b = kd.astype(BF16)
            qs = (q_ref[rows, :].astype(F32) * Q_SCALE).astype(BF16)
            v_c = v_ref[rows, :]
            do_c = do_scr[rows, :]
            st = st_ref[ci]
            st_b = st.astype(BF16)
            st_prev_in = st_ref[jnp.maximum(ci - 1, 0)]
            st_prev_edge = jnp.where(first_block, 0.0, stp_ref[0])
            st_prev = jnp.where(ci > 0, st_prev_in, st_prev_edge)
            dkd_parts = []
            dgam_parts = []
            for h in range(GLA_HEADS):
                dst = lax.dot_general(do_c[:, _hv(h)], qs[:, _hk(h)], _DOT_DIMS["tn"],
                                      preferred_element_type=F32) + carry[:, _hk(h)]
                dst_b = dst.astype(BF16)
                dqs = jnp.dot(do_c[:, _hv(h)], st_b[:, _hk(h)], preferred_element_type=F32)
                dp_ref[rows, COL_Q + h * HEAD_K:COL_Q + (h + 1) * HEAD_K] = (dqs * Q_SCALE).astype(BF16)
                dkd_parts.append(jnp.dot(v_c[:, _hv(h)], dst_b, preferred_element_type=F32))
                dp_ref[rows, COL_V + h * HEAD_V:COL_V + (h + 1) * HEAD_V] = lax.dot_general(
                    kd_b[:, _hk(h)], dst_b, _DOT_DIMS["nt"], preferred_element_type=F32).astype(BF16)
                dgam_parts.append(jnp.sum(dst * st_prev[:, _hk(h)], axis=0, keepdims=True))
                carry[:, _hk(h)] = dst * gam[:, _hk(h)]
            dkd = jnp.concatenate(dkd_parts, axis=1)
            dgam = jnp.concatenate(dgam_parts, axis=1)
            dp_ref[rows, COL_K:COL_K + KEY_WIDTH] = (dkd * e).astype(BF16)
            dz = dkd * kd
            dla_scr[rows, :] = gam * dgam + jnp.dot(tri_s, dz, precision=lax.Precision.HIGHEST,
                                                    preferred_element_type=F32)
            return c0

        lax.fori_loop(0, nc, bwd_chunk, 0)

        dfg = dla_scr[...] * (1.0 / GATE_TAU) * _sigmoid(-fg_scr[...])
        dfg_b = dfg.astype(BF16)
        dfz_ref[...] = lax.dot_general(dfg_b, w_ref[...], _DOT_DIMS["nt"],
                                       preferred_element_type=F32).astype(BF16)
        dw_ref[...] += lax.dot_general(fz, dfg_b, _DOT_DIMS["tn"], preferred_element_type=F32)
        db_ref[...] += jnp.sum(dfg, axis=0, keepdims=True)

    def rev(i):
        return nb - 1 - i

    return pl.pallas_call(
        body, name="gla_bwd",
        out_shape=(jax.ShapeDtypeStruct((s, PROJ_W), BF16),
                   jax.ShapeDtypeStruct((s, FZ_PAD), BF16),
                   jax.ShapeDtypeStruct((1, HEAD_V), F32),
                   jax.ShapeDtypeStruct((FZ_PAD, KEY_WIDTH), F32),
                   jax.ShapeDtypeStruct((1, KEY_WIDTH), F32)),
        grid=(nb,),
        in_specs=[pl.BlockSpec((tr, KEY_WIDTH), lambda i: (rev(i), COL_Q // KEY_WIDTH)),
                  pl.BlockSpec((tr, KEY_WIDTH), lambda i: (rev(i), COL_K // KEY_WIDTH)),
                  pl.BlockSpec((tr, VAL_WIDTH), lambda i: (rev(i), COL_V // VAL_WIDTH)),
                  pl.BlockSpec((tr, VAL_WIDTH), lambda i: (rev(i), COL_R // VAL_WIDTH)),
                  pl.BlockSpec((tr, FZ_PAD), lambda i: (rev(i), COL_FZ // FZ_PAD)),
                  pl.BlockSpec((tr, VAL_WIDTH), lambda i: (rev(i), 0)),
                  pl.BlockSpec((nc, HEAD_V, KEY_WIDTH), lambda i: (rev(i), 0, 0)),
                  pl.BlockSpec((1, HEAD_V, KEY_WIDTH), lambda i: (jnp.maximum(rev(i) * nc - 1, 0), 0, 0)),
                  pl.BlockSpec((FZ_PAD, KEY_WIDTH), lambda i: (0, 0)),
                  pl.BlockSpec((1, KEY_WIDTH), lambda i: (0, 0)),
                  pl.BlockSpec((1, HEAD_V), lambda i: (0, 0)), ANY],
        out_specs=(pl.BlockSpec((tr, 3 * VAL_WIDTH), lambda i: (rev(i), 0)),
                   pl.BlockSpec((tr, FZ_PAD), lambda i: (rev(i), 0)),
                   pl.BlockSpec((1, HEAD_V), lambda i: (0, 0)),
                   pl.BlockSpec((FZ_PAD, KEY_WIDTH), lambda i: (0, 0)),
                   pl.BlockSpec((1, KEY_WIDTH), lambda i: (0, 0))),
        scratch_shapes=[pltpu.VMEM((HEAD_V, KEY_WIDTH), F32),
                        pltpu.VMEM((tr, KEY_WIDTH), F32),
                        pltpu.VMEM((tr, KEY_WIDTH), F32),
                        pltpu.VMEM((tr, KEY_WIDTH), F32),
                        pltpu.VMEM((tr, VAL_WIDTH), F32),
                        pltpu.VMEM((tr, VAL_WIDTH), BF16)],
        input_output_aliases={11: 0},
        compiler_params=_cparams(("arbitrary",)),
    )(proj, proj, proj, proj, proj, doa, states, states, wfg2p, bfg, gn, dproj)


def _put_fz(dproj, dfz, *, s):
    tr = min(512, s)

    def body(dfz_ref, dproj_in_ref, o_ref):
        o_ref[...] = dfz_ref[...]

    return pl.pallas_call(
        body, name="put_dfz",
        out_shape=jax.ShapeDtypeStruct((s, PROJ_W), BF16),
        grid=(s // tr,),
        in_specs=[pl.BlockSpec((tr, FZ_PAD), lambda i: (i, 0)), ANY],
        out_specs=pl.BlockSpec((tr, FZ_PAD), lambda i: (i, COL_FZ // FZ_PAD)),
        input_output_aliases={1: 0},
        compiler_params=_cparams(("parallel",)),
    )(dfz, dproj)


_ADAM_C1 = 1.0 / (1.0 - ADAM_B1 ** ADAM_STEP)
_ADAM_C2 = 1.0 / (1.0 - ADAM_B2 ** ADAM_STEP)


def _adamw_math(wv, gv, mv, vv):
    nm = ADAM_B1 * mv + (1.0 - ADAM_B1) * gv
    nv = ADAM_B2 * vv + (1.0 - ADAM_B2) * (gv * gv)
    delta = -ADAM_LR * ((nm * _ADAM_C1) / (jnp.sqrt(nv * _ADAM_C2) + ADAM_EPS) + ADAM_WD * wv)
    return delta, nm, nv


def _adamw(w, g, m, v, *, name):
    shape = w.shape
    if w.ndim == 1:
        w, g, m, v = (t.reshape(1, 1, -1) for t in (w, g, m, v))
    elif w.ndim == 2:
        w, g, m, v = (t.reshape((1,) + t.shape) for t in (w, g, m, v))
    l, a, b = w.shape

    def body(w_ref, g_ref, m_ref, v_ref, d_ref, nm_ref, nv_ref):
        d_ref[...], nm_ref[...], nv_ref[...] = _adamw_math(w_ref[...], g_ref[...], m_ref[...], v_ref[...])

    blk = pl.BlockSpec((1, a, b), lambda li: (li, 0, 0))
    outs = pl.pallas_call(
        body, name=name,
        out_shape=tuple(jax.ShapeDtypeStruct((l, a, b), F32) for _ in range(3)),
        grid=(l,),
        in_specs=[blk, blk, blk, blk],
        out_specs=(blk, blk, blk),
        compiler_params=_cparams(("parallel",)),
    )(w, g, m, v)
    return tuple(o.reshape(shape) for o in outs)


def _adamw_layers(w, reduced, received, on_core0, m, v, *, name):
    l, a, b = w.shape
    assert l == DEPTH == 2
    ta = _row_tile(a, F32_SUBLANES, 384)

    def body(flag_ref, w_ref, r0_ref, o0_ref, r1_ref, o1_ref, m_ref, v_ref, g_ref, d_ref, nm_ref, nv_ref):
        core0 = flag_ref[...] > 0.5
        g0 = jnp.where(core0, r0_ref[...], o0_ref[...])
        g1 = jnp.where(core0, o1_ref[...], r1_ref[...])
        gv = jnp.where(pl.program_id(0) == 0, g0, g1)
        g_ref[0] = gv
        d_ref[0], nm_ref[0], nv_ref[0] = _adamw_math(w_ref[0], gv, m_ref[0], v_ref[0])

    blk = pl.BlockSpec((1, ta, b), lambda li, ai: (li, ai, 0))
    gblk = pl.BlockSpec((ta, b), lambda li, ai: (ai, 0))
    return pl.pallas_call(
        body, name=name,
        out_shape=tuple(jax.ShapeDtypeStruct((l, a, b), F32) for _ in range(4)),
        grid=(l, a // ta),
        in_specs=[pl.BlockSpec((1, 1), lambda li, ai: (0, 0)), blk, gblk, gblk, gblk, gblk, blk, blk],
        out_specs=(blk, blk, blk, blk),
        compiler_params=_cparams(("parallel", "parallel")),
    )(on_core0, w, reduced[0], received[0], reduced[1], received[1], m, v)


MESH_ID = pl.DeviceIdType.MESH
ANY = pl.BlockSpec(memory_space=pl.ANY)


def _position():
    x, y, c = lax.axis_index("x"), lax.axis_index("y"), lax.axis_index("c")
    chips = [(1 - x, y), (x, 1 - y), (1 - x, 1 - y)]
    return x, y, c, chips


def _chip_index(xy):
    return 2 * xy[0] + xy[1]


def _all_gather_weights(shards):
    n = len(shards)

    def body(*refs):
        ins, outs = refs[:n], refs[n:2 * n]
        send_sems, recv_sems, pass_send, pass_recv = refs[2 * n:]
        x, y, c, chips = _position()
        sibling = (x, y, 1 - c)

        def ici(a, k, chip_from):
            return pltpu.make_async_remote_copy(
                src_ref=ins[a].at[c], dst_ref=outs[a].at[c, _chip_index(chip_from)],
                send_sem=send_sems.at[a, k], recv_sem=recv_sems.at[a, k],
                device_id=(chips[k][0], chips[k][1], c), device_id_type=MESH_ID)

        def handoff(a, k, layer):
            slot = outs[a].at[layer, _chip_index(chips[k])]
            return pltpu.make_async_remote_copy(
                src_ref=slot, dst_ref=slot, send_sem=pass_send.at[a, k], recv_sem=pass_recv.at[a, k],
                device_id=sibling, device_id_type=MESH_ID)

        for a in range(n):
            for k in range(3):
                ici(a, k, (x, y)).start()
        for k in range(3):
            for a in range(n):
                ici(a, k, chips[k]).wait_recv()
                handoff(a, k, c).start()
        for k in range(3):
            for a in range(n):
                handoff(a, k, 1 - c).wait_recv()
        for a in range(n):
            for k in range(3):
                ici(a, k, (x, y)).wait_send()
                handoff(a, k, c).wait_send()

    gathered = pl.pallas_call(
        body, name="all_gather_weights",
        out_shape=tuple(jax.ShapeDtypeStruct((DEPTH, N_CHIPS) + t.shape[1:], t.dtype) for t in shards),
        in_specs=[ANY] * n, out_specs=tuple([ANY] * n),
        scratch_shapes=[pltpu.SemaphoreType.DMA((n, 3)), pltpu.SemaphoreType.DMA((n, 3)),
                        pltpu.SemaphoreType.DMA((n, 3)), pltpu.SemaphoreType.DMA((n, 3))],
    )(*shards)
    me = _chip_index((lax.axis_index("x"), lax.axis_index("y")))
    return [lax.dynamic_update_index_in_dim(g, t[:, None], me, axis=1) for g, t in zip(gathered, shards)]


HBM = pl.BlockSpec(memory_space=pltpu.HBM)
SEM = pl.BlockSpec(memory_space=pltpu.SEMAPHORE)
DATAFLOW_EFFECT = pltpu.SideEffectType.DATAFLOW_SIDE_EFFECTING
TOKEN_SHAPE = (8, LANE)


def _landing(shape, dtype):
    return pltpu.with_memory_space_constraint(lax.empty(shape, dtype), pltpu.HBM)


def _split_start(bufs, sem_shape, issue, *, name, after=None):
    n = len(bufs)
    n_in = n + (after is not None)

    def body(*refs):
        issue(refs[:n], refs[n_in], refs[n_in + 1])
        token = refs[-1]
        token[...] = jnp.zeros_like(token)

    operands = [pltpu.with_memory_space_constraint(t, pltpu.HBM) for t in bufs]
    outs = pl.pallas_call(
        body, name=name,
        out_shape=(pltpu.SemaphoreType.DMA(sem_shape), pltpu.SemaphoreType.DMA(sem_shape),
                   *[pltpu.HBM(t.shape, t.dtype) for t in bufs], jax.ShapeDtypeStruct(TOKEN_SHAPE, F32)),
        in_specs=[HBM] * n + [ANY] * (after is not None),
        out_specs=(SEM, SEM, *[HBM] * n, pl.BlockSpec(memory_space=pltpu.VMEM)),
        input_output_aliases={i: 2 + i for i in range(n)},
        compiler_params=pltpu.CompilerParams(has_side_effects=DATAFLOW_EFFECT),
    )(*operands, *([after] if after is not None else []))
    return outs[0], outs[1], list(outs[2:2 + n]), outs[-1]


def _split_wait(started, after, settle, *, name):
    send_sems, recv_sems, bufs, _ = started
    n = len(bufs)

    def body(*refs):
        settle(refs[:n], refs[n], refs[n + 1])

    outs = pl.pallas_call(
        body, name=name,
        out_shape=tuple(pltpu.HBM(t.shape, t.dtype) for t in bufs),
        in_specs=[HBM] * n + [SEM, SEM, ANY],
        out_specs=tuple([HBM] * n),
        input_output_aliases={i: i for i in range(n)},
        compiler_params=pltpu.CompilerParams(has_side_effects=DATAFLOW_EFFECT),
    )(*bufs, send_sems, recv_sems, after)
    return list(outs)


def _to_sibling(bufs, n, send_sems, recv_sems):
    x, y, c, _ = _position()
    return [pltpu.make_async_remote_copy(
        src_ref=bufs[a], dst_ref=bufs[n + a], send_sem=send_sems.at[a], recv_sem=recv_sems.at[a],
        device_id=(x, y, 1 - c), device_id_type=MESH_ID) for a in range(n)]


def _sibling_push_start(layer, sender_is_reducer, arrays, *, name):
    n = len(arrays)
    sender = layer if sender_is_reducer else 1 - layer

    def issue(bufs, send_sems, recv_sems):
        @pl.when(lax.axis_index("c") == sender)
        def _():
            for cp in _to_sibling(bufs, n, send_sems, recv_sems):
                cp.start()

    lands = [_landing(t.shape, t.dtype) for t in arrays]
    return _split_start(list(arrays) + lands, (n,), issue, name=name)


def _sibling_push_wait(layer, sender_is_reducer, started, after, *, name):
    n = len(started[2]) // 2
    sender = layer if sender_is_reducer else 1 - layer

    def settle(bufs, send_sems, recv_sems):
        c = lax.axis_index("c")

        @pl.when(c == sender)
        def _():
            for cp in _to_sibling(bufs, n, send_sems, recv_sems):
                cp.wait_send()

        @pl.when(c != sender)
        def _():
            for cp in _to_sibling(bufs, n, send_sems, recv_sems):
                cp.wait_recv()

    outs = _split_wait(started, after, settle, name=name)
    return outs[:n], outs[n:]


def _chip_copies(bufs, n, send_sems, recv_sems):
    x, y, c, chips = _position()
    return [pltpu.make_async_remote_copy(
        src_ref=bufs[a].at[_chip_index(chips[k])], dst_ref=bufs[n + a].at[k],
        send_sem=send_sems.at[3 * a + k], recv_sem=recv_sems.at[3 * a + k],
        device_id=(chips[k][0], chips[k][1], c), device_id_type=MESH_ID)
        for a in range(n) for k in range(3)]


def _chip_exchange_start(layer, parts, *, name):
    n = len(parts)

    def issue(bufs, send_sems, recv_sems):
        @pl.when(lax.axis_index("c") == layer)
        def _():
            for cp in _chip_copies(bufs, n, send_sems, recv_sems):
                cp.start()

    lands = [_landing((3,) + t.shape[1:], t.dtype) for t in parts]
    return _split_start(list(parts) + lands, (3 * n,), issue, name=name)


def _chip_exchange_wait(layer, started, after, *, name):
    n = len(started[2]) // 2

    def settle(bufs, send_sems, recv_sems):
        @pl.when(lax.axis_index("c") == layer)
        def _():
            for cp in _chip_copies(bufs, n, send_sems, recv_sems):
                cp.wait()

    outs = _split_wait(started, after, settle, name=name)
    return outs[:n], outs[n:]


def _gather_copies(layer, bufs, n, send_sems, recv_sems, arriving):
    x, y, c, chips = _position()
    me = _chip_index((x, y))
    return [pltpu.make_async_remote_copy(
        src_ref=bufs[a], dst_ref=bufs[n + a].at[_chip_index(chips[k]) if arriving else me],
        send_sem=send_sems.at[3 * a + k], recv_sem=recv_sems.at[3 * a + k],
        device_id=(chips[k][0], chips[k][1], c), device_id_type=MESH_ID)
        for a in range(n) for k in range(3)]


def _gather_start(layer, shards, *, name, after=None):
    n = len(shards)

    def issue(bufs, send_sems, recv_sems):
        @pl.when(lax.axis_index("c") == layer)
        def _():
            for cp in _gather_copies(layer, bufs, n, send_sems, recv_sems, False):
                cp.start()

    lands = [_landing((N_CHIPS,) + t.shape, t.dtype) for t in shards]
    return _split_start(list(shards) + lands, (3 * n,), issue, name=name, after=after)


def _gather_wait(layer, started, after, *, name):
    n = len(started[2]) // 2

    def settle(bufs, send_sems, recv_sems):
        @pl.when(lax.axis_index("c") == layer)
        def _():
            for cp in _gather_copies(layer, bufs, n, send_sems, recv_sems, False):
                cp.wait_send()
            for cp in _gather_copies(layer, bufs, n, send_sems, recv_sems, True):
                cp.wait_recv()

    return _split_wait(started, after, settle, name=name)[n:]


def _handoff_copies(bufs, n, send_sems, recv_sems):
    x, y, c, chips = _position()
    out = []
    for a in range(n):
        for k in range(3):
            slot = bufs[a].at[_chip_index(chips[k])]
            out.append(pltpu.make_async_remote_copy(
                src_ref=slot, dst_ref=slot, send_sem=send_sems.at[3 * a + k], recv_sem=recv_sems.at[3 * a + k],
                device_id=(x, y, 1 - c), device_id_type=MESH_ID))
    return out


def _handoff_start(layer, gathered, *, name):
    n = len(gathered)

    def issue(bufs, send_sems, recv_sems):
        @pl.when(lax.axis_index("c") == layer)
        def _():
            for cp in _handoff_copies(bufs, n, send_sems, recv_sems):
                cp.start()

    return _split_start(list(gathered), (3 * n,), issue, name=name)


def _handoff_wait(layer, started, after, *, name):
    n = len(started[2])

    def settle(bufs, send_sems, recv_sems):
        c = lax.axis_index("c")

        @pl.when(c == layer)
        def _():
            for cp in _handoff_copies(bufs, n, send_sems, recv_sems):
                cp.wait_send()

        @pl.when(c != layer)
        def _():
            for cp in _handoff_copies(bufs, n, send_sems, recv_sems):
                cp.wait_recv()

    return _split_wait(started, after, settle, name=name)


F32_SUBLANES = 8
BF16_SUBLANES = 16


def _row_tile(a, sublanes=BF16_SUBLANES, max_rows=704):
    best = None
    for cand in range(sublanes, min(a, max_rows) + 1, sublanes):
        if a % cand == 0:
            best = cand
    assert best is not None, a
    return best


def _pair_sum(mine, sib, *, name):
    nchip, a, b = sib.shape
    ta = _row_tile(a)

    def body(m_ref, s_ref, o_ref):
        o_ref[...] = (m_ref[...] + s_ref[...]).astype(BF16)

    blk = pl.BlockSpec((1, ta, b), lambda j, r: (j, r, 0))
    return pl.pallas_call(
        body, name=name,
        out_shape=jax.ShapeDtypeStruct((nchip, a, b), BF16),
        grid=(nchip, a // ta),
        in_specs=[blk, blk], out_specs=blk,
        compiler_params=_cparams(("parallel", "parallel")),
    )(mine, sib)


def _total_sum(own, recv, *, name):
    a, b = own.shape
    ta = _row_tile(a)

    def body(o_ref, r_ref, t_ref):
        acc = o_ref[...].astype(F32)
        for k in range(3):
            acc = acc + r_ref[k].astype(F32)
        t_ref[...] = acc

    blk = pl.BlockSpec((ta, b), lambda r: (r, 0))
    return pl.pallas_call(
        body, name=name,
        out_shape=jax.ShapeDtypeStruct((a, b), F32),
        grid=(a // ta,),
        in_specs=[blk, pl.BlockSpec((3, ta, b), lambda r: (0, r, 0))], out_specs=blk,
        compiler_params=_cparams(("parallel",)),
    )(own, recv)


def _all_reduce_small(packed):
    rows, width = packed.shape

    def body(x_ref, out_ref, gath, send_sems, recv_sems, local_sem):
        x, y, c, chips = _position()
        me, sibling = (x, y, c), (x, y, 1 - c)

        def slot(px, py, pc):
            return gath.at[4 * px + 2 * py + pc]

        def copy(k, block, to, src=None):
            return pltpu.make_async_remote_copy(
                src_ref=slot(*block) if src is None else src, dst_ref=slot(*block),
                send_sem=send_sems.at[k], recv_sem=recv_sems.at[k], device_id=to, device_id_type=MESH_ID)

        mine = pltpu.make_async_copy(x_ref, slot(*me), local_sem)
        mine.start()
        first = [copy(0, me, sibling, src=x_ref)]
        first += [copy(1 + j, me, (*chip, c), src=x_ref) for j, chip in enumerate(chips)]
        for cp in first:
            cp.start()
        passed = [copy(4 + j, (*chip, c), sibling) for j, chip in enumerate(chips)]
        for j, chip in enumerate(chips):
            copy(1 + j, (*chip, c), me).wait_recv()
            passed[j].start()
        copy(0, sibling, me).wait_recv()
        for j, chip in enumerate(chips):
            copy(4 + j, (*chip, 1 - c), me).wait_recv()
        for cp in first + passed:
            cp.wait_send()
        mine.wait()
        acc = gath[0]
        for d in range(1, N_DEV):
            acc = acc + gath[d]
        out_ref[...] = acc

    return pl.pallas_call(
        body, name="all_reduce_small",
        out_shape=jax.ShapeDtypeStruct((rows, width), F32),
        in_specs=[pl.BlockSpec(memory_space=pltpu.VMEM)],
        out_specs=pl.BlockSpec(memory_space=pltpu.VMEM),
        scratch_shapes=[pltpu.VMEM((N_DEV, rows, width), F32), pltpu.SemaphoreType.DMA((7,)),
                        pltpu.SemaphoreType.DMA((7,)), pltpu.SemaphoreType.DMA],
    )(packed)


def _mixer_forward_branches(x0, w, s, dep=None):
    h = _rms_fwd(x0, w["norm1_g"], name="rms1_fwd")
    proj = _matmul(h, w["w_in_t"], mode="nt", out_dtype=BF16, tm=1024, tn=1664, tk=1024, name="proj_fwd",
                   dep=dep)
    oa, states = _gla_fwd(proj, w["w_fg2"], w["b_fg"], w["gla_norm_g"], s=s)
    cb_in = _conv_fwd(proj, w["conv_wb"], s=s)
    return dict(x0=x0, h=h, proj=proj, oa=oa, states=states, cb_in=cb_in)


def _mixer_forward_out(sv, w, s, dep=None):
    ya = _matmul(sv["oa"], w["w_oa"], mode="nn", out_dtype=BF16, tm=1024, tn=1024, tk=1024, name="ya_fwd",
                 dep=dep)
    yb = _matmul(sv["cb_in"], w["w_ob"], mode="nn", out_dtype=BF16, tm=1024, tn=1024, tk=1024, name="yb_fwd")
    mix = _mix_fwd(sv["proj"], ya, yb, s=s)
    x1 = _matmul(mix, w["w_o"], mode="nn", out_dtype=F32, tm=1024, tn=1024, tk=1024, name="wo_fwd",
                 add=sv["x0"])
    return x1, dict(sv, ya=ya, yb=yb, mix=mix)


def _ffn_forward_hidden(x1, w, s, dep=None):
    h2 = _rms_fwd(x1, w["norm2_g"], name="rms2_fwd")
    gt = _matmul(h2, w["w_gate_t"], mode="nt", out_dtype=BF16, tm=1024, tn=1408, tk=1024, name="ffn_gate_fwd",
                 dep=dep)
    up = _matmul(h2, w["w_up_t"], mode="nt", out_dtype=BF16, tm=1024, tn=1408, tk=1024, name="ffn_up_fwd")
    return dict(x1=x1, h2=h2, gt=gt, up=up, hid=_swiglu_fwd(gt, up, s=s))


def _ffn_forward_out(sv, w, dep=None):
    return _matmul(sv["hid"], w["w_ffn_down"], mode="nn", out_dtype=F32, tm=1024, tn=1024, tk=FFN_HIDDEN,
                   name="ffn_out_fwd", add=sv["x1"], dep=dep)


def _ffn_backward_grads(dx2b, w, sv, s):
    g = {}
    dhid = _matmul(dx2b, w["w_ffn_down"], mode="nt", out_dtype=BF16, tm=1024, tn=FFN_HIDDEN, tk=1024,
                   name="ffn_out_bwd")
    g["w_ffn_down"] = _matmul(sv["hid"], dx2b, mode="tn", out_dtype=F32, tm=1408, tn=1024, tk=512,
                              name="ffn_out_wgrad")
    dgt, dup = _swiglu_bwd(sv["gt"], sv["up"], dhid, s=s)
    g["w_gate_t"] = _matmul(dgt, sv["h2"], mode="tn", out_dtype=F32, tm=1408, tn=1024, tk=512,
                            name="ffn_gate_wgrad")
    g["w_up_t"] = _matmul(dup, sv["h2"], mode="tn", out_dtype=F32, tm=1408, tn=1024, tk=512,
                          name="ffn_up_wgrad")
    return g, dgt, dup, dhid


def _ffn_backward_input(dgt, dup, dx2, w, sv, dep=None):
    dh2 = _matmul(dgt, w["w_gate_t"], mode="nn", out_dtype=F32, tm=1024, tn=1024, tk=FFN_HIDDEN,
                  name="ffn_gate_bwd", dep=dep)
    dh2 = _matmul(dup, w["w_up_t"], mode="nn", out_dtype=F32, tm=1024, tn=1024, tk=FFN_HIDDEN,
                  name="ffn_up_bwd", add=dh2)
    return _rms_bwd(sv["x1"], w["norm2_g"], dh2, dx2, name="rms2_bwd")


def _mixer_backward_branches(dx1b, w, sv, s, dep=None):
    g = {}
    dmix = _matmul(dx1b, w["w_o"], mode="nt", out_dtype=BF16, tm=1024, tn=1024, tk=1024, name="wo_bwd",
                   dep=dep)
    g["w_o"] = _matmul(sv["mix"], dx1b, mode="tn", out_dtype=F32, tm=1024, tn=1024, tk=512, name="wo_wgrad")
    dya, dyb, dproj = _mix_bwd(sv["proj"], sv["ya"], sv["yb"], dmix, s=s)
    dcb = _matmul(dyb, w["w_ob"], mode="nt", out_dtype=BF16, tm=1024, tn=1024, tk=1024, name="yb_bwd")
    g["w_ob"] = _matmul(sv["cb_in"], dyb, mode="tn", out_dtype=F32, tm=1024, tn=1024, tk=512, name="yb_wgrad")
    doa = _matmul(dya, w["w_oa"], mode="nt", out_dtype=BF16, tm=1024, tn=1024, tk=1024, name="ya_bwd")
    g["w_oa"] = _matmul(sv["oa"], dya, mode="tn", out_dtype=F32, tm=1024, tn=1024, tk=512, name="ya_wgrad")
    dproj, g["conv_wb"] = _conv_bwd(sv["proj"], dcb, w["conv_wb"], dproj, s=s)
    dproj, dfz, g["gla_norm_g"], g["w_fg2"], g["b_fg"] = _gla_bwd(
        sv["proj"], doa, sv["states"], w["w_fg2"], w["b_fg"], w["gla_norm_g"], dproj, s=s)
    return g, _put_fz(dproj, dfz, s=s)


def _proj_wgrad(dproj, sv, dep=None):
    return _matmul(dproj, sv["h"], mode="tn", out_dtype=F32, tm=1664, tn=1024, tk=512, name="proj_wgrad", dep=dep)


def _proj_bwd(dproj, w, dep=None):
    return _matmul(dproj, w["w_in_t"], mode="nn", out_dtype=F32, tm=1024, tn=1024, tk=1664, name="proj_bwd",
                   dep=dep)


def _cols_from_chips(t):
    return jnp.transpose(t, (1, 0, 2)).reshape(t.shape[1], -1)


W_IN_ROWS = IN_WIDTH // N_CHIPS
W_IN_ROWS_PAD = -(-W_IN_ROWS // BF16_SUBLANES) * BF16_SUBLANES


def _w_in_t_shard(t):
    return jnp.pad(jnp.transpose(t, (0, 2, 1)), ((0, 0), (0, W_IN_ROWS_PAD - W_IN_ROWS), (0, 0)))


def _w_in_t_unshard(t):
    return jnp.transpose(t[:, :W_IN_ROWS], (0, 2, 1))


def _w_in_t_to_kernel(t):
    full = t[:, :W_IN_ROWS].reshape(IN_WIDTH, D_MODEL)
    pad = jnp.zeros((FZ_PAD - GATE_RANK, D_MODEL), t.dtype)
    return jnp.concatenate([full[:FZ_ORIG], full[FZ_ORIG + GATE_RANK:], full[FZ_ORIG:FZ_ORIG + GATE_RANK], pad],
                           axis=0)


def _w_in_t_from_kernel(g):
    full = jnp.concatenate([g[:FZ_ORIG], g[COL_FZ:COL_FZ + GATE_RANK], g[FZ_ORIG:COL_FZ]], axis=0)
    return jnp.pad(full.reshape(N_CHIPS, W_IN_ROWS, D_MODEL), ((0, 0), (0, W_IN_ROWS_PAD - W_IN_ROWS), (0, 0)))


def kernel(x, norm1_g, w_in, w_fg2, b_fg, gla_norm_g, w_oa, conv_w, conv_b, w_ob, w_o, norm2_g, w_ffn_gate, w_ffn_up, w_ffn_down, final_g, loss_target, m_norm1_g, m_w_in, m_w_fg2, m_b_fg, m_gla_norm_g, m_w_oa, m_conv_w, m_conv_b, m_w_ob, m_w_o, m_norm2_g, m_w_ffn_gate, m_w_ffn_up, m_w_ffn_down, m_final_g, v_norm1_g, v_w_in, v_w_fg2, v_b_fg, v_gla_norm_g, v_w_oa, v_conv_w, v_conv_b, v_w_ob, v_w_o, v_norm2_g, v_w_ffn_gate, v_w_ffn_up, v_w_ffn_down, v_final_g):
    cx_ = lax.axis_index("x")
    cy_ = lax.axis_index("y")
    cc_ = lax.axis_index("c")
    me = 2 * cx_ + cy_
    on_core0 = jnp.where(cc_ == 0, 1.0, 0.0).astype(F32).reshape(1, 1)

    def swap(t):
        return jnp.swapaxes(t, 1, 2)

    big_names = ["w_in", "w_oa", "w_ob", "w_o", "w_ffn_gate", "w_ffn_up", "w_ffn_down"]
    views = dict(
        w_in=tuple(_w_in_t_shard(t) for t in (w_in, m_w_in, v_w_in)),
        w_oa=(w_oa, m_w_oa, v_w_oa), w_ob=(w_ob, m_w_ob, v_w_ob), w_o=(w_o, m_w_o, v_w_o),
        w_ffn_gate=tuple(swap(t) for t in (w_ffn_gate, m_w_ffn_gate, v_w_ffn_gate)),
        w_ffn_up=tuple(swap(t) for t in (w_ffn_up, m_w_ffn_up, v_w_ffn_up)),
        w_ffn_down=(w_ffn_down, m_w_ffn_down, v_w_ffn_down))
    from_view = dict(w_in=_w_in_t_unshard, w_ffn_gate=swap, w_ffn_up=swap)

    s = x.shape[1]

    conv_w_p = jnp.pad(conv_w, ((0, 0), (0, 8 - conv_w.shape[1]), (0, 0)))
    w_fg2_all, conv_w_all = _all_gather_weights([w_fg2, conv_w_p])

    proj_names = ["w_in"]
    rest_names = ["w_oa", "w_ob", "w_o", "w_ffn_gate", "w_ffn_up", "w_ffn_down"]
    mixer_names = ["w_oa", "w_ob", "w_o"]
    ffn_names = ["w_ffn_gate", "w_ffn_up", "w_ffn_down"]
    weight_key = dict(w_in="w_in_t", w_ffn_gate="w_gate_t", w_ffn_up="w_up_t")
    full_shape = dict(w_oa=(VAL_WIDTH, D_MODEL), w_ob=(CONV_CH, D_MODEL), w_o=(D_MODEL, D_MODEL),
                      w_ffn_gate=(FFN_HIDDEN, D_MODEL), w_ffn_up=(FFN_HIDDEN, D_MODEL),
                      w_ffn_down=(FFN_HIDDEN, D_MODEL))

    def shards_of(l, names):
        return [views[n][0][l].astype(BF16) for n in names]

    def small_weights(l):
        w_fg2_full = _cols_from_chips(w_fg2_all[l])
        conv_w_full = _cols_from_chips(conv_w_all[l])[:3]
        return dict(
            norm1_g=norm1_g[l], norm2_g=norm2_g[l],
            w_fg2=jnp.pad(w_fg2_full, ((0, FZ_PAD - GATE_RANK), (0, 0))).astype(BF16),
            b_fg=b_fg[l].reshape(1, KEY_WIDTH), gla_norm_g=gla_norm_g[l].reshape(1, HEAD_V),
            conv_wb=(jnp.pad(conv_w_full, ((0, 5), (0, 0)))
                     + jnp.pad(conv_b[l].reshape(1, CONV_CH), ((3, 4), (0, 0)))))

    def full_weights(names, gathered, shards):
        out = {}
        for n, g, t in zip(names, gathered, shards):
            g = lax.dynamic_update_index_in_dim(g, t[None], me, axis=0)
            out[weight_key.get(n, n)] = _w_in_t_to_kernel(g) if n == "w_in" else g.reshape(full_shape[n])
        return out

    def per_chip(g, names):
        out = []
        for n in names:
            t = g[weight_key.get(n, n)]
            out.append(_w_in_t_from_kernel(t) if n == "w_in"
                       else t.reshape(N_CHIPS, t.shape[0] // N_CHIPS, t.shape[1]))
        return out

    def gather(l, names, tag, after=None):
        shards = shards_of(l, names)
        return shards, _gather_start(l, shards, name=f"gather_{tag}_start", after=after)

    def gathered_to_sibling(l, started, after, tag):
        return _handoff_start(l, _gather_wait(l, started, after, name=f"gather_{tag}_wait"),
                              name=f"handoff_{tag}_start")

    def feed(l, g, names, tag):
        return _sibling_push_start(l, False, per_chip(g, names), name=f"feed_{tag}_start")

    def pair_and_exchange(l, fed, after, names, tag):
        mine, sib = _sibling_push_wait(l, False, fed, after, name=f"feed_{tag}_wait")
        parts = [_pair_sum(a, b, name=f"pair_sum_{tag}_{n}") for n, a, b in zip(names, mine, sib)]
        return _chip_exchange_start(l, parts, name=f"exchange_{tag}_start")

    def total_and_share(l, swapped, after, names, tag):
        parts, recvs = _chip_exchange_wait(l, swapped, after, name=f"exchange_{tag}_wait")
        owns = [lax.dynamic_index_in_dim(p, me, 0, keepdims=False) for p in parts]
        totals = [_total_sum(o, rc, name=f"total_{tag}_{n}") for n, o, rc in zip(names, owns, recvs)]
        return _sibling_push_start(l, True, totals, name=f"share_{tag}_start")

    def shared(l, sharing, after, names, tag):
        totals, others = _sibling_push_wait(l, True, sharing, after, name=f"share_{tag}_wait")
        return {n: (t, o) for n, t, o in zip(names, totals, others)}

    shards_p, started = gather(0, proj_names, "p0")
    started = gathered_to_sibling(0, started, started[3], "p0")
    w0 = small_weights(0)
    w0.update(full_weights(proj_names, _handoff_wait(0, started, started[3], name="handoff_p0_wait"), shards_p))
    shards_r, started = gather(0, rest_names, "r0", after=w0["w_in_t"])
    sv0 = _mixer_forward_branches(x[0], w0, s, dep=started[3])
    started = gathered_to_sibling(0, started, sv0["oa"], "r0")
    w0.update(full_weights(rest_names, _handoff_wait(0, started, started[3], name="handoff_r0_wait"), shards_r))

    shards_1, started = gather(1, big_names, "l1", after=w0["w_o"])
    x1, sv0m = _mixer_forward_out(sv0, w0, s, dep=started[3])
    sv0f = _ffn_forward_hidden(x1, w0, s)
    started = gathered_to_sibling(1, started, sv0f["hid"], "l1")
    x2 = _ffn_forward_out(sv0f, w0, dep=started[3])
    w1 = small_weights(1)
    w1.update(full_weights(big_names, _handoff_wait(1, started, x2, name="handoff_l1_wait"), shards_1))

    x3, sv1m = _mixer_forward_out(_mixer_forward_branches(x2, w1, s), w1, s)
    sv1f = _ffn_forward_hidden(x3, w1, s)
    x4 = _ffn_forward_out(sv1f, w1)
    loss_local, dx, dxb, dgf = _loss_head(x4, final_g, loss_target[0])
    loss = lax.psum(loss_local[0, 0], ("x", "y", "c"))

    g1, dgt, dup, _ = _ffn_backward_grads(dxb, w1, sv1f, s)
    dx_mid, dxb_mid, g1["norm2_g"] = _ffn_backward_input(dgt, dup, dx, w1, sv1f)
    gm, dproj = _mixer_backward_branches(dxb_mid, w1, sv1m, s)
    g1.update(gm)
    g1["w_in_t"] = _proj_wgrad(dproj, sv1m)
    fed_1 = feed(1, g1, big_names, "l1")
    dh = _proj_bwd(dproj, w1, dep=fed_1[3])
    dx, dxb, g1["norm1_g"] = _rms_bwd(sv1m["x0"], norm1_g[1], dh, dx_mid, name="rms1_bwd")

    g0, dgt, dup, dhid = _ffn_backward_grads(dxb, w0, sv0f, s)
    swap_1 = pair_and_exchange(1, fed_1, dhid, big_names, "l1")
    fed_f = feed(0, g0, ffn_names, "f0")
    dx_mid, dxb_mid, g0["norm2_g"] = _ffn_backward_input(dgt, dup, dx, w0, sv0f, dep=swap_1[3] + fed_f[3])
    swap_f = pair_and_exchange(0, fed_f, dx_mid, ffn_names, "f0")
    gm, dproj = _mixer_backward_branches(dxb_mid, w0, sv0m, s, dep=swap_f[3])
    g0.update(gm)
    share_1 = total_and_share(1, swap_1, dproj, big_names, "l1")
    share_f = total_and_share(0, swap_f, dproj, ffn_names, "f0")
    fed_m = feed(0, g0, mixer_names, "m0")
    g0["w_in_t"] = _proj_wgrad(dproj, sv0m, dep=share_1[3] + share_f[3] + fed_m[3])
    fed_p = feed(0, g0, proj_names, "p0")
    swap_m = pair_and_exchange(0, fed_m, g0["w_in_t"], mixer_names, "m0")
    dh = _proj_bwd(dproj, w0, dep=fed_p[3] + swap_m[3])
    reduced1 = shared(1, share_1, dh, big_names, "l1")
    reduced0 = shared(0, share_f, dh, ffn_names, "f0")
    grad_x, _, g0["norm1_g"] = _rms_bwd(sv0m["x0"], norm1_g[0], dh, dx_mid, name="rms1_bwd")

    swap_p = pair_and_exchange(0, fed_p, grad_x, proj_names, "p0")
    share_m = total_and_share(0, swap_m, swap_p[3], mixer_names, "m0")
    share_p = total_and_share(0, swap_p, share_m[3], proj_names, "p0")
    reduced0.update(shared(0, share_m, share_p[3], mixer_names, "m0"))
    reduced0.update(shared(0, share_p, share_p[3], proj_names, "p0"))
    grads = [g0, g1]

    def small_rows(t):
        return t.reshape(-1, D_MODEL)

    def tile_rows(t):
        return jnp.pad(t, ((0, -t.shape[0] % F32_SUBLANES), (0, 0)))

    g0, g1 = grads
    pieces = [
        jnp.concatenate([g0["norm1_g"], g1["norm1_g"]], axis=0),
        jnp.concatenate([g0["norm2_g"], g1["norm2_g"]], axis=0),
        dgf,
        small_rows(jnp.concatenate([g0["b_fg"], g1["b_fg"]], axis=1)),
        small_rows(jnp.concatenate([g0["gla_norm_g"], g1["gla_norm_g"],
                                    jnp.zeros((1, D_MODEL - 2 * HEAD_V), F32)], axis=1)),
        jnp.concatenate([g0["conv_wb"][3:4], g1["conv_wb"][3:4]], axis=0),
        jnp.concatenate([g0["conv_wb"][:3], g1["conv_wb"][:3]], axis=0),
        small_rows(jnp.stack([g0["w_fg2"][:GATE_RANK], g1["w_fg2"][:GATE_RANK]])),
    ]
    small = _all_reduce_small(jnp.concatenate([tile_rows(p) for p in pieces], axis=0))
    sg = dict(
        norm1_g=small[0:2], norm2_g=small[8:10], final_g=small[16],
        b_fg=small[24].reshape(DEPTH, KEY_WIDTH), gla_norm_g=small[32, :DEPTH * HEAD_V].reshape(DEPTH, HEAD_V),
        conv_b=small[40:42],
        conv_w=lax.dynamic_slice_in_dim(small[48:54].reshape(DEPTH, 3, CONV_CH), me * (CONV_CH // N_CHIPS),
                                        CONV_CH // N_CHIPS, axis=2),
        w_fg2=lax.dynamic_slice_in_dim(small[56:72].reshape(DEPTH, GATE_RANK, KEY_WIDTH),
                                       me * (KEY_WIDTH // N_CHIPS), KEY_WIDTH // N_CHIPS, axis=2),
    )

    small_params = dict(norm1_g=(norm1_g, m_norm1_g, v_norm1_g), w_fg2=(w_fg2, m_w_fg2, v_w_fg2),
                        b_fg=(b_fg, m_b_fg, v_b_fg), gla_norm_g=(gla_norm_g, m_gla_norm_g, v_gla_norm_g),
                        conv_w=(conv_w, m_conv_w, v_conv_w), conv_b=(conv_b, m_conv_b, v_conv_b),
                        norm2_g=(norm2_g, m_norm2_g, v_norm2_g), final_g=(final_g, m_final_g, v_final_g))
    order = ["norm1_g", "w_in", "w_fg2", "b_fg", "gla_norm_g", "w_oa", "conv_w", "conv_b", "w_ob", "w_o",
             "norm2_g", "w_ffn_gate", "w_ffn_up", "w_ffn_down", "final_g"]
    results = {}
    for i, n in enumerate(big_names):
        w_, m_, v_ = views[n]
        outs = _adamw_layers(w_, (reduced0[n][0], reduced1[n][0]), (reduced0[n][1], reduced1[n][1]), on_core0,
                             m_, v_, name="adamw_" + n)
        back = from_view.get(n)
        results[n] = tuple(back(r) for r in outs) if back else outs
    for n, (w_, m_, v_) in small_params.items():
        results[n] = (sg[n],) + _adamw(w_, sg[n], m_, v_, name="adamw_" + n)
    return (loss, grad_x[None], *[results[n][0] for n in order], *[results[n][1] for n in order],
            *[results[n][2] for n in order], *[results[n][3] for n in order])
```

```python
import functools

import jax
import jax.numpy as jnp
from jax import lax
from jax.experimental import pallas as pl
from jax.experimental.pallas import tpu as pltpu

F32 = jnp.float32
BF16 = jnp.bfloat16

D_MODEL = 1024
DEPTH = 2
CHUNK = 64
GLA_HEADS = 4
KEY_WIDTH = 512
VAL_WIDTH = 1024
HEAD_K = 128
HEAD_V = 256
GATE_RANK = 16
GATE_TAU = 16.0
CONV_CH = 1024
FFN_HIDDEN = 2816
IN_WIDTH = 8208
NORM_EPS = 1e-6
Q_SCALE = HEAD_K ** -0.5
ADAM_LR = 0.001
ADAM_B1 = 0.9
ADAM_B2 = 0.999
ADAM_EPS = 1e-08
ADAM_WD = 0.01
ADAM_STEP = 10

N_CHIPS = 4
N_DEV = 8

LANE = 128
FZ_PAD = LANE
PROJ_W = 8192 + FZ_PAD
COL_Q, COL_K, COL_V, COL_R, COL_GBI, COL_GCI, COL_CX, COL_GA, COL_GB, COL_FZ = (
    0, 512, 1024, 2048, 3072, 4096, 5120, 6144, 7168, 8192)
FZ_ORIG = 3072

VMEM_LIMIT = 52 * 1024 * 1024
HALO = 16


def _cparams(sem=None):
    return pltpu.CompilerParams(dimension_semantics=sem, vmem_limit_bytes=VMEM_LIMIT)


def _sigmoid(x):
    return jax.nn.sigmoid(x)


def _log_sigmoid(x):
    return jnp.minimum(x, 0.0) - jnp.log1p(jnp.exp(-jnp.abs(x)))


_DOT_DIMS = {
    "nn": (((1,), (0,)), ((), ())),
    "nt": (((1,), (1,)), ((), ())),
    "tn": (((0,), (0,)), ((), ())),
}


def _matmul(a, b, *, mode, out_dtype, tm, tn, tk, name, add=None, dep=None):
    if mode == "nn":
        (m, k), n = a.shape, b.shape[1]
    elif mode == "nt":
        (m, k), n = a.shape, b.shape[0]
    else:
        (k, m), n = a.shape, b.shape[1]
    tm, tn, tk = min(tm, m), min(tn, n), min(tk, k)
    assert m % tm == 0 and n % tn == 0 and k % tk == 0, (name, m, n, k, tm, tn, tk)
    nk = k // tk
    has_add = add is not None
    has_dep = dep is not None

    def body(*refs):
        if has_dep:
            refs = refs[1:]
        if has_add:
            a_ref, b_ref, add_ref, o_ref = refs[:4]
            scratch = refs[4:]
        else:
            a_ref, b_ref, o_ref = refs[:3]
            add_ref = None
            scratch = refs[3:]
        part = lax.dot_general(a_ref[...], b_ref[...], _DOT_DIMS[mode], preferred_element_type=F32)

        def finish(acc):
            if add_ref is not None:
                acc = acc + add_ref[...].astype(F32)
            o_ref[...] = acc.astype(o_ref.dtype)

        if nk == 1:
            finish(part)
        else:
            acc_ref = scratch[0]
            kk = pl.program_id(2)

            @pl.when(kk == 0)
            def _():
                acc_ref[...] = part

            @pl.when(kk > 0)
            def _():
                acc_ref[...] += part

            @pl.when(kk == nk - 1)
            def _():
                finish(acc_ref[...])

    if mode == "nn":
        a_spec = pl.BlockSpec((tm, tk), lambda i, j, kk: (i, kk))
        b_spec = pl.BlockSpec((tk, tn), lambda i, j, kk: (kk, j))
    elif mode == "nt":
        a_spec = pl.BlockSpec((tm, tk), lambda i, j, kk: (i, kk))
        b_spec = pl.BlockSpec((tn, tk), lambda i, j, kk: (j, kk))
    else:
        a_spec = pl.BlockSpec((tk, tm), lambda i, j, kk: (kk, i))
        b_spec = pl.BlockSpec((tk, tn), lambda i, j, kk: (kk, j))
    o_spec = pl.BlockSpec((tm, tn), lambda i, j, kk: (i, j))
    in_specs = [a_spec, b_spec] + ([o_spec] if has_add else [])
    operands = (a, b) + ((add,) if has_add else ())
    if has_dep:
        in_specs = [pl.BlockSpec(dep.shape, lambda i, j, kk: (0, 0))] + in_specs
        operands = (dep,) + operands
    return pl.pallas_call(
        body,
        name=name,
        out_shape=jax.ShapeDtypeStruct((m, n), out_dtype),
        grid=(m // tm, n // tn, nk),
        in_specs=in_specs,
        out_specs=o_spec,
        scratch_shapes=[pltpu.VMEM((tm, tn), F32)] if nk > 1 else [],
        compiler_params=_cparams(("parallel", "parallel", "arbitrary")),
    )(*operands)


def _rms_fwd(x, g, *, name):
    s, d = x.shape
    tr = min(512, s)

    def body(x_ref, g_ref, h_ref):
        xv = x_ref[...]
        rs = lax.rsqrt(jnp.mean(xv * xv, axis=-1, keepdims=True) + NORM_EPS)
        h_ref[...] = (xv * rs * g_ref[...]).astype(BF16)

    return pl.pallas_call(
        body, name=name,
        out_shape=jax.ShapeDtypeStruct((s, d), BF16),
        grid=(s // tr,),
        in_specs=[pl.BlockSpec((tr, d), lambda i: (i, 0)), pl.BlockSpec((1, d), lambda i: (0, 0))],
        out_specs=pl.BlockSpec((tr, d), lambda i: (i, 0)),
        compiler_params=_cparams(("parallel",)),
    )(x, g.reshape(1, d))


def _rms_bwd(x, g, dh, dres, *, name):
    s, d = x.shape
    tr = min(512, s)

    def body(x_ref, g_ref, dh_ref, dres_ref, dx_ref, dxb_ref, dg_ref):
        i = pl.program_id(0)
        xv = x_ref[...]
        rs = lax.rsqrt(jnp.mean(xv * xv, axis=-1, keepdims=True) + NORM_EPS)
        n = xv * rs
        dhv = dh_ref[...].astype(F32)
        dn = dhv * g_ref[...]
        dx = dres_ref[...] + rs * (dn - n * jnp.mean(dn * n, axis=-1, keepdims=True))
        dx_ref[...] = dx
        dxb_ref[...] = dx.astype(BF16)
        part = jnp.sum(dhv * n, axis=0, keepdims=True)

        @pl.when(i == 0)
        def _():
            dg_ref[...] = part

        @pl.when(i > 0)
        def _():
            dg_ref[...] += part

    row = pl.BlockSpec((tr, d), lambda i: (i, 0))
    vec = pl.BlockSpec((1, d), lambda i: (0, 0))
    return pl.pallas_call(
        body, name=name,
        out_shape=(jax.ShapeDtypeStruct((s, d), F32), jax.ShapeDtypeStruct((s, d), BF16),
                   jax.ShapeDtypeStruct((1, d), F32)),
        grid=(s // tr,),
        in_specs=[row, vec, row, row],
        out_specs=(row, row, vec),
        compiler_params=_cparams(("arbitrary",)),
    )(x, g.reshape(1, d), dh, dres)


def _loss_head(x, g, target):
    s, d = x.shape
    tr = min(512, s)

    def body(x_ref, g_ref, t_ref, loss_ref, dx_ref, dxb_ref, dg_ref):
        i = pl.program_id(0)
        xv = x_ref[...]
        rs = lax.rsqrt(jnp.mean(xv * xv, axis=-1, keepdims=True) + NORM_EPS)
        n = xv * rs
        gv = g_ref[...]
        err = n * gv - t_ref[...]
        row_loss = jnp.mean(err * err, axis=-1, keepdims=True)
        loss_part = 0.5 * jnp.sum(row_loss, axis=0, keepdims=True)
        dy = err * (1.0 / d)
        dn = dy * gv
        dx = rs * (dn - n * jnp.mean(dn * n, axis=-1, keepdims=True))
        dx_ref[...] = dx
        dxb_ref[...] = dx.astype(BF16)
        dg_part = jnp.sum(dy * n, axis=0, keepdims=True)

        @pl.when(i == 0)
        def _():
            loss_ref[...] = loss_part
            dg_ref[...] = dg_part

        @pl.when(i > 0)
        def _():
            loss_ref[...] += loss_part
            dg_ref[...] += dg_part

    row = pl.BlockSpec((tr, d), lambda i: (i, 0))
    vec = pl.BlockSpec((1, d), lambda i: (0, 0))
    one = pl.BlockSpec((1, 1), lambda i: (0, 0))
    return pl.pallas_call(
        body, name="loss_head",
        out_shape=(jax.ShapeDtypeStruct((1, 1), F32), jax.ShapeDtypeStruct((s, d), F32),
                   jax.ShapeDtypeStruct((s, d), BF16), jax.ShapeDtypeStruct((1, d), F32)),
        grid=(s // tr,),
        in_specs=[row, vec, row],
        out_specs=(one, row, row, vec),
        compiler_params=_cparams(("arbitrary",)),
    )(x, g.reshape(1, d), target)


def _conv_taps(u_prev, u, w_ref, rows):
    ext = jnp.concatenate([u_prev, u], axis=0)
    u1 = pltpu.roll(ext, 1, 0)[HALO:HALO + rows]
    u2 = pltpu.roll(ext, 2, 0)[HALO:HALO + rows]
    conv = w_ref[0:1, :] * u2 + w_ref[1:2, :] * u1 + w_ref[2:3, :] * u + w_ref[3:4, :]
    return conv, u1, u2


def _conv_fwd(proj, conv_wb, *, s):
    tr = min(512, s)
    c = CONV_CH
    hb = tr // HALO

    def body(gbi_ref, gci_ref, cx_ref, gci_h_ref, cx_h_ref, w_ref, o_ref):
        i = pl.program_id(0)
        u = gci_ref[...].astype(F32) * cx_ref[...].astype(F32)
        u_prev = gci_h_ref[...].astype(F32) * cx_h_ref[...].astype(F32)
        u_prev = jnp.where(i == 0, 0.0, u_prev)
        conv, _, _ = _conv_taps(u_prev, u, w_ref, tr)
        o_ref[...] = (gbi_ref[...].astype(F32) * conv).astype(BF16)

    def seg(col):
        return pl.BlockSpec((tr, c), lambda i: (i, col // c))

    def halo(col):
        return pl.BlockSpec((HALO, c), lambda i: (jnp.maximum(i * hb - 1, 0), col // c))

    return pl.pallas_call(
        body, name="conv_fwd",
        out_shape=jax.ShapeDtypeStruct((s, c), BF16),
        grid=(s // tr,),
        in_specs=[seg(COL_GBI), seg(COL_GCI), seg(COL_CX), halo(COL_GCI), halo(COL_CX),
                  pl.BlockSpec((8, c), lambda i: (0, 0))],
        out_specs=pl.BlockSpec((tr, c), lambda i: (i, 0)),
        compiler_params=_cparams(("parallel",)),
    )(proj, proj, proj, proj, proj, conv_wb)


def _conv_bwd(proj, dcb, conv_wb, dproj, *, s):
    tr = min(512, s)
    c = CONV_CH
    hb = tr // HALO
    nb = s // tr

    def body(gbi_ref, gci_ref, cx_ref, dcb_ref, gci_h_ref, cx_h_ref, gbi_n_ref, dcb_n_ref, w_ref,
             dproj_in_ref, dp_ref, dw_ref):
        i = pl.program_id(0)
        gbi = gbi_ref[...].astype(F32)
        gci = gci_ref[...].astype(F32)
        cx = cx_ref[...].astype(F32)
        dcb_v = dcb_ref[...].astype(F32)
        u = gci * cx
        u_prev = jnp.where(i == 0, 0.0, gci_h_ref[...].astype(F32) * cx_h_ref[...].astype(F32))
        conv, u1, u2 = _conv_taps(u_prev, u, w_ref, tr)
        dconv = dcb_v * gbi
        dconv_next = jnp.where(i == nb - 1, 0.0, dcb_n_ref[...].astype(F32) * gbi_n_ref[...].astype(F32))
        ext = jnp.concatenate([dconv, dconv_next], axis=0)
        n_ext = tr + HALO
        d1 = pltpu.roll(ext, n_ext - 1, 0)[0:tr]
        d2 = pltpu.roll(ext, n_ext - 2, 0)[0:tr]
        du = w_ref[2:3, :] * dconv + w_ref[1:2, :] * d1 + w_ref[0:1, :] * d2
        dp_ref[:, 0:c] = (dcb_v * conv).astype(BF16)
        dp_ref[:, c:2 * c] = (du * cx).astype(BF16)
        dp_ref[:, 2 * c:3 * c] = (du * gci).astype(BF16)
        part = jnp.concatenate([
            jnp.sum(dconv * u2, axis=0, keepdims=True),
            jnp.sum(dconv * u1, axis=0, keepdims=True),
            jnp.sum(dconv * u, axis=0, keepdims=True),
            jnp.sum(dconv, axis=0, keepdims=True),
            jnp.zeros((4, c), F32)], axis=0)

        @pl.when(i == 0)
        def _():
            dw_ref[...] = part

        @pl.when(i > 0)
        def _():
            dw_ref[...] += part

    def seg(col):
        return pl.BlockSpec((tr, c), lambda i: (i, col // c))

    def halo_prev(col):
        return pl.BlockSpec((HALO, c), lambda i: (jnp.maximum(i * hb - 1, 0), col // c))

    def halo_next(col):
        return pl.BlockSpec((HALO, c), lambda i: (jnp.minimum((i + 1) * hb, nb * hb - 1), col // c))

    return pl.pallas_call(
        body, name="conv_bwd",
        out_shape=(jax.ShapeDtypeStruct((s, PROJ_W), BF16), jax.ShapeDtypeStruct((8, c), F32)),
        grid=(nb,),
        in_specs=[seg(COL_GBI), seg(COL_GCI), seg(COL_CX), pl.BlockSpec((tr, c), lambda i: (i, 0)),
                  halo_prev(COL_GCI), halo_prev(COL_CX), halo_next(COL_GBI),
                  pl.BlockSpec((HALO, c), lambda i: (jnp.minimum((i + 1) * hb, nb * hb - 1), 0)),
                  pl.BlockSpec((8, c), lambda i: (0, 0)), ANY],
        out_specs=(pl.BlockSpec((tr, 3 * c), lambda i: (i, COL_GBI // (3 * c))),
                   pl.BlockSpec((8, c), lambda i: (0, 0))),
        input_output_aliases={9: 0},
        compiler_params=_cparams(("arbitrary",)),
    )(proj, proj, proj, dcb, proj, proj, proj, dcb, conv_wb, dproj)


def _mix_fwd(proj, ya, yb, *, s):
    tr = min(512, s)
    d = D_MODEL

    def body(ga_ref, gb_ref, ya_ref, yb_ref, o_ref):
        sa = _sigmoid(ga_ref[...].astype(F32))
        sb = _sigmoid(gb_ref[...].astype(F32))
        o_ref[...] = (sa * ya_ref[...].astype(F32) + sb * yb_ref[...].astype(F32)).astype(BF16)

    row = pl.BlockSpec((tr, d), lambda i: (i, 0))
    return pl.pallas_call(
        body, name="mix_fwd",
        out_shape=jax.ShapeDtypeStruct((s, d), BF16),
        grid=(s // tr,),
        in_specs=[pl.BlockSpec((tr, d), lambda i: (i, COL_GA // d)),
                  pl.BlockSpec((tr, d), lambda i: (i, COL_GB // d)), row, row],
        out_specs=row,
        compiler_params=_cparams(("parallel",)),
    )(proj, proj, ya, yb)


def _mix_bwd(proj, ya, yb, dmix, *, s):
    tr = min(512, s)
    d = D_MODEL

    def body(ga_ref, gb_ref, ya_ref, yb_ref, dm_ref, dya_ref, dyb_ref, dg_ref):
        sa = _sigmoid(ga_ref[...].astype(F32))
        sb = _sigmoid(gb_ref[...].astype(F32))
        dm = dm_ref[...].astype(F32)
        dya_ref[...] = (dm * sa).astype(BF16)
        dyb_ref[...] = (dm * sb).astype(BF16)
        dg_ref[:, 0:d] = (dm * ya_ref[...].astype(F32) * sa * (1.0 - sa)).astype(BF16)
        dg_ref[:, d:2 * d] = (dm * yb_ref[...].astype(F32) * sb * (1.0 - sb)).astype(BF16)

    row = pl.BlockSpec((tr, d), lambda i: (i, 0))
    return pl.pallas_call(
        body, name="mix_bwd",
        out_shape=(jax.ShapeDtypeStruct((s, d), BF16), jax.ShapeDtypeStruct((s, d), BF16),
                   jax.ShapeDtypeStruct((s, PROJ_W), BF16)),
        grid=(s // tr,),
        in_specs=[pl.BlockSpec((tr, d), lambda i: (i, COL_GA // d)),
                  pl.BlockSpec((tr, d), lambda i: (i, COL_GB // d)), row, row, row],
        out_specs=(row, row, pl.BlockSpec((tr, 2 * d), lambda i: (i, COL_GA // (2 * d)))),
        compiler_params=_cparams(("parallel",)),
    )(proj, proj, ya, yb, dmix)


def _swiglu_fwd(gt, up, *, s):
    tr = min(256, s)
    f = FFN_HIDDEN

    def body(gt_ref, up_ref, o_ref):
        gv = gt_ref[...].astype(F32)
        o_ref[...] = (gv * _sigmoid(gv) * up_ref[...].astype(F32)).astype(BF16)

    row = pl.BlockSpec((tr, f), lambda i: (i, 0))
    return pl.pallas_call(
        body, name="swiglu_fwd",
        out_shape=jax.ShapeDtypeStruct((s, f), BF16),
        grid=(s // tr,),
        in_specs=[row, row], out_specs=row,
        compiler_params=_cparams(("parallel",)),
    )(gt, up)


def _swiglu_bwd(gt, up, dhid, *, s):
    tr = min(256, s)
    f = FFN_HIDDEN

    def body(gt_ref, up_ref, dh_ref, dgt_ref, dup_ref):
        gv = gt_ref[...].astype(F32)
        uv = up_ref[...].astype(F32)
        dh = dh_ref[...].astype(F32)
        sg = _sigmoid(gv)
        dgt_ref[...] = (dh * uv * sg * (1.0 + gv * (1.0 - sg))).astype(BF16)
        dup_ref[...] = (dh * gv * sg).astype(BF16)

    row = pl.BlockSpec((tr, f), lambda i: (i, 0))
    return pl.pallas_call(
        body, name="swiglu_bwd",
        out_shape=(jax.ShapeDtypeStruct((s, f), BF16), jax.ShapeDtypeStruct((s, f), BF16)),
        grid=(s // tr,),
        in_specs=[row, row, row], out_specs=(row, row),
        compiler_params=_cparams(("parallel",)),
    )(gt, up, dhid)


def _tri(strict):
    r = lax.broadcasted_iota(jnp.int32, (CHUNK, CHUNK), 0)
    c = lax.broadcasted_iota(jnp.int32, (CHUNK, CHUNK), 1)
    return jnp.where((c < r) if strict else (c <= r), 1.0, 0.0).astype(F32)


def _gla_gate_terms(la_c, tri):
    cum = jnp.dot(tri, la_c, precision=lax.Precision.HIGHEST, preferred_element_type=F32)
    cend = cum[CHUNK - 1:CHUNK, :]
    return jnp.exp(cend - cum), jnp.exp(cend)


def _hk(h):
    return slice(h * HEAD_K, (h + 1) * HEAD_K)


def _hv(h):
    return slice(h * HEAD_V, (h + 1) * HEAD_V)


def _gla_fwd(proj, wfg2p, bfg, gn, *, s):
    tr = min(512, s)
    nb = s // tr
    nc = tr // CHUNK

    def body(q_ref, k_ref, v_ref, r_ref, fz_ref, w_ref, b_ref, gn_ref, oa_ref, st_ref,
             state, la_scr, o_scr):
        i = pl.program_id(0)

        @pl.when(i == 0)
        def _():
            state[...] = jnp.zeros_like(state)

        fg = jnp.dot(fz_ref[...], w_ref[...], preferred_element_type=F32) + b_ref[...]
        la_scr[...] = _log_sigmoid(fg) * (1.0 / GATE_TAU)
        tri = _tri(False)

        def chunk(ci, carry):
            r0 = pl.multiple_of(ci * CHUNK, CHUNK)
            rows = pl.ds(r0, CHUNK)
            e, gam = _gla_gate_terms(la_scr[rows, :], tri)
            kd = (k_ref[rows, :].astype(F32) * e).astype(BF16)
            qs = (q_ref[rows, :].astype(F32) * Q_SCALE).astype(BF16)
            v_c = v_ref[rows, :]
            for h in range(GLA_HEADS):
                upd = lax.dot_general(v_c[:, _hv(h)], kd[:, _hk(h)], _DOT_DIMS["tn"],
                                      preferred_element_type=F32)
                st_h = state[:, _hk(h)] * gam[:, _hk(h)] + upd
                state[:, _hk(h)] = st_h
                o_scr[rows, _hv(h)] = lax.dot_general(qs[:, _hk(h)], st_h.astype(BF16), _DOT_DIMS["nt"],
                                                       preferred_element_type=F32)
            st_ref[ci] = state[...]
            return carry

        lax.fori_loop(0, nc, chunk, 0)
        for h in range(GLA_HEADS):
            o = o_scr[:, _hv(h)]
            rs = lax.rsqrt(jnp.mean(o * o, axis=-1, keepdims=True) + NORM_EPS)
            rv = r_ref[:, _hv(h)].astype(F32)
            oa_ref[:, _hv(h)] = ((o * rs * gn_ref[...]).astype(F32) * (rv * _sigmoid(rv))).astype(BF16)

    return pl.pallas_call(
        body, name="gla_fwd",
        out_shape=(jax.ShapeDtypeStruct((s, VAL_WIDTH), BF16),
                   jax.ShapeDtypeStruct((s // CHUNK, HEAD_V, KEY_WIDTH), F32)),
        grid=(nb,),
        in_specs=[pl.BlockSpec((tr, KEY_WIDTH), lambda i: (i, COL_Q // KEY_WIDTH)),
                  pl.BlockSpec((tr, KEY_WIDTH), lambda i: (i, COL_K // KEY_WIDTH)),
                  pl.BlockSpec((tr, VAL_WIDTH), lambda i: (i, COL_V // VAL_WIDTH)),
                  pl.BlockSpec((tr, VAL_WIDTH), lambda i: (i, COL_R // VAL_WIDTH)),
                  pl.BlockSpec((tr, FZ_PAD), lambda i: (i, COL_FZ // FZ_PAD)),
                  pl.BlockSpec((FZ_PAD, KEY_WIDTH), lambda i: (0, 0)),
                  pl.BlockSpec((1, KEY_WIDTH), lambda i: (0, 0)),
                  pl.BlockSpec((1, HEAD_V), lambda i: (0, 0))],
        out_specs=(pl.BlockSpec((tr, VAL_WIDTH), lambda i: (i, 0)),
                   pl.BlockSpec((nc, HEAD_V, KEY_WIDTH), lambda i: (i, 0, 0))),
        scratch_shapes=[pltpu.VMEM((HEAD_V, KEY_WIDTH), F32), pltpu.VMEM((tr, KEY_WIDTH), F32),
                        pltpu.VMEM((tr, VAL_WIDTH), F32)],
        compiler_params=_cparams(("arbitrary",)),
    )(proj, proj, proj, proj, proj, wfg2p, bfg, gn)


def _gla_bwd(proj, doa, states, wfg2p, bfg, gn, dproj, *, s):
    tr = min(512, s)
    nb = s // tr
    nc = tr // CHUNK

    def body(q_ref, k_ref, v_ref, r_ref, fz_ref, doa_ref, st_ref, stp_ref, w_ref, b_ref, gn_ref, dproj_in_ref,
             dp_ref, dfz_ref, dgn_ref, dw_ref, db_ref,
             carry, fg_scr, la_scr, dla_scr, o_scr, do_scr):
        i = pl.program_id(0)

        @pl.when(i == 0)
        def _():
            carry[...] = jnp.zeros_like(carry)
            dgn_ref[...] = jnp.zeros_like(dgn_ref)
            dw_ref[...] = jnp.zeros_like(dw_ref)
            db_ref[...] = jnp.zeros_like(db_ref)

        fz = fz_ref[...]
        fg = jnp.dot(fz, w_ref[...], preferred_element_type=F32) + b_ref[...]
        fg_scr[...] = fg
        la_scr[...] = _log_sigmoid(fg) * (1.0 / GATE_TAU)

        def fwd_chunk(ci, c0):
            r0 = pl.multiple_of(ci * CHUNK, CHUNK)
            rows = pl.ds(r0, CHUNK)
            qs = (q_ref[rows, :].astype(F32) * Q_SCALE).astype(BF16)
            st = st_ref[ci].astype(BF16)
            for h in range(GLA_HEADS):
                o_scr[rows, _hv(h)] = lax.dot_general(qs[:, _hk(h)], st[:, _hk(h)], _DOT_DIMS["nt"],
                                                       preferred_element_type=F32)
            return c0

        lax.fori_loop(0, nc, fwd_chunk, 0)

        gnv = gn_ref[...]
        dgn_part = jnp.zeros((1, HEAD_V), F32)
        for h in range(GLA_HEADS):
            o = o_scr[:, _hv(h)]
            rs = lax.rsqrt(jnp.mean(o * o, axis=-1, keepdims=True) + NORM_EPS)
            nrm = o * rs
            rv = r_ref[:, _hv(h)].astype(F32)
            sg = _sigmoid(rv)
            doa_h = doa_ref[:, _hv(h)].astype(F32)
            don = doa_h * (rv * sg)
            dp_ref[:, COL_R + h * HEAD_V:COL_R + (h + 1) * HEAD_V] = (
                doa_h * (nrm * gnv) * (sg * (1.0 + rv * (1.0 - sg)))).astype(BF16)
            dgn_part = dgn_part + jnp.sum(don * nrm, axis=0, keepdims=True)
            dn = don * gnv
            do_scr[:, _hv(h)] = (rs * (dn - nrm * jnp.mean(dn * nrm, axis=-1, keepdims=True))).astype(BF16)
        dgn_ref[...] += dgn_part

        tri = _tri(False)
        tri_s = _tri(True)
        first_block = i == nb - 1

        def bwd_chunk(cc, c0):
            ci = nc - 1 - cc
            r0 = pl.multiple_of(ci * CHUNK, CHUNK)
            rows = pl.ds(r0, CHUNK)
            e, gam = _gla_gate_terms(la_scr[rows, :], tri)
            k_c = k_ref[rows, :].astype(F32)
            kd = k_c * e
            kd_b = kd.astype(BF16)
            qs = (q_ref[rows, :].astype(F32) * Q_SCALE).astype(BF16)
            v_c = v_ref[rows, :]
            do_c = do_scr[rows, :]
            st = st_ref[ci]
            st_b = st.astype(BF16)
            st_prev_in = st_ref[jnp.maximum(ci - 1, 0)]
            st_prev_edge = jnp.where(first_block, 0.0, stp_ref[0])
            st_prev = jnp.where(ci > 0, st_prev_in, st_prev_edge)
            dkd_parts = []
            dgam_parts = []
            for h in range(GLA_HEADS):
                dst = lax.dot_general(do_c[:, _hv(h)], qs[:, _hk(h)], _DOT_DIMS["tn"],
                                      preferred_element_type=F32) + carry[:, _hk(h)]
                dst_b = dst.astype(BF16)
                dqs = jnp.dot(do_c[:, _hv(h)], st_b[:, _hk(h)], preferred_element_type=F32)
                dp_ref[rows, COL_Q + h * HEAD_K:COL_Q + (h + 1) * HEAD_K] = (dqs * Q_SCALE).astype(BF16)
                dkd_parts.append(jnp.dot(v_c[:, _hv(h)], dst_b, preferred_element_type=F32))
                dp_ref[rows, COL_V + h * HEAD_V:COL_V + (h + 1) * HEAD_V] = lax.dot_general(
                    kd_b[:, _hk(h)], dst_b, _DOT_DIMS["nt"], preferred_element_type=F32).astype(BF16)
                dgam_parts.append(jnp.sum(dst * st_prev[:, _hk(h)], axis=0, keepdims=True))
                carry[:, _hk(h)] = dst * gam[:, _hk(h)]
            dkd = jnp.concatenate(dkd_parts, axis=1)
            dgam = jnp.concatenate(dgam_parts, axis=1)
            dp_ref[rows, COL_K:COL_K + KEY_WIDTH] = (dkd * e).astype(BF16)
            dz = dkd * kd
            dla_scr[rows, :] = gam * dgam + jnp.dot(tri_s, dz, precision=lax.Precision.HIGHEST,
                                                    preferred_element_type=F32)
            return c0

        lax.fori_loop(0, nc, bwd_chunk, 0)

        dfg = dla_scr[...] * (1.0 / GATE_TAU) * _sigmoid(-fg_scr[...])
        dfg_b = dfg.astype(BF16)
        dfz_ref[...] = lax.dot_general(dfg_b, w_ref[...], _DOT_DIMS["nt"],
                                       preferred_element_type=F32).astype(BF16)
        dw_ref[...] += lax.dot_general(fz, dfg_b, _DOT_DIMS["tn"], preferred_element_type=F32)
        db_ref[...] += jnp.sum(dfg, axis=0, keepdims=True)

    def rev(i):
        return nb - 1 - i

    return pl.pallas_call(
        body, name="gla_bwd",
        out_shape=(jax.ShapeDtypeStruct((s, PROJ_W), BF16),
                   jax.ShapeDtypeStruct((s, FZ_PAD), BF16),
                   jax.ShapeDtypeStruct((1, HEAD_V), F32),
                   jax.ShapeDtypeStruct((FZ_PAD, KEY_WIDTH), F32),
                   jax.ShapeDtypeStruct((1, KEY_WIDTH), F32)),
        grid=(nb,),
        in_specs=[pl.BlockSpec((tr, KEY_WIDTH), lambda i: (rev(i), COL_Q // KEY_WIDTH)),
                  pl.BlockSpec((tr, KEY_WIDTH), lambda i: (rev(i), COL_K // KEY_WIDTH)),
                  pl.BlockSpec((tr, VAL_WIDTH), lambda i: (rev(i), COL_V // VAL_WIDTH)),
                  pl.BlockSpec((tr, VAL_WIDTH), lambda i: (rev(i), COL_R // VAL_WIDTH)),
                  pl.BlockSpec((tr, FZ_PAD), lambda i: (rev(i), COL_FZ // FZ_PAD)),
                  pl.BlockSpec((tr, VAL_WIDTH), lambda i: (rev(i), 0)),
                  pl.BlockSpec((nc, HEAD_V, KEY_WIDTH), lambda i: (rev(i), 0, 0)),
                  pl.BlockSpec((1, HEAD_V, KEY_WIDTH), lambda i: (jnp.maximum(rev(i) * nc - 1, 0), 0, 0)),
                  pl.BlockSpec((FZ_PAD, KEY_WIDTH), lambda i: (0, 0)),
                  pl.BlockSpec((1, KEY_WIDTH), lambda i: (0, 0)),
                  pl.BlockSpec((1, HEAD_V), lambda i: (0, 0)), ANY],
        out_specs=(pl.BlockSpec((tr, 3 * VAL_WIDTH), lambda i: (rev(i), 0)),
                   pl.BlockSpec((tr, FZ_PAD), lambda i: (rev(i), 0)),
                   pl.BlockSpec((1, HEAD_V), lambda i: (0, 0)),
                   pl.BlockSpec((FZ_PAD, KEY_WIDTH), lambda i: (0, 0)),
                   pl.BlockSpec((1, KEY_WIDTH), lambda i: (0, 0))),
        scratch_shapes=[pltpu.VMEM((HEAD_V, KEY_WIDTH), F32),
                        pltpu.VMEM((tr, KEY_WIDTH), F32),
                        pltpu.VMEM((tr, KEY_WIDTH), F32),
                        pltpu.VMEM((tr, KEY_WIDTH), F32),
                        pltpu.VMEM((tr, VAL_WIDTH), F32),
                        pltpu.VMEM((tr, VAL_WIDTH), BF16)],
        input_output_aliases={11: 0},
        compiler_params=_cparams(("arbitrary",)),
    )(proj, proj, proj, proj, proj, doa, states, states, wfg2p, bfg, gn, dproj)


def _put_fz(dproj, dfz, *, s):
    tr = min(512, s)

    def body(dfz_ref, dproj_in_ref, o_ref):
        o_ref[...] = dfz_ref[...]

    return pl.pallas_call(
        body, name="put_dfz",
        out_shape=jax.ShapeDtypeStruct((s, PROJ_W), BF16),
        grid=(s // tr,),
        in_specs=[pl.BlockSpec((tr, FZ_PAD), lambda i: (i, 0)), ANY],
        out_specs=pl.BlockSpec((tr, FZ_PAD), lambda i: (i, COL_FZ // FZ_PAD)),
        input_output_aliases={1: 0},
        compiler_params=_cparams(("parallel",)),
    )(dfz, dproj)


_ADAM_C1 = 1.0 / (1.0 - ADAM_B1 ** ADAM_STEP)
_ADAM_C2 = 1.0 / (1.0 - ADAM_B2 ** ADAM_STEP)


def _adamw_math(wv, gv, mv, vv):
    nm = ADAM_B1 * mv + (1.0 - ADAM_B1) * gv
    nv = ADAM_B2 * vv + (1.0 - ADAM_B2) * (gv * gv)
    delta = -ADAM_LR * ((nm * _ADAM_C1) / (jnp.sqrt(nv * _ADAM_C2) + ADAM_EPS) + ADAM_WD * wv)
    return delta, nm, nv


def _adamw(w, g, m, v, *, name):
    shape = w.shape
    if w.ndim == 1:
        w, g, m, v = (t.reshape(1, 1, -1) for t in (w, g, m, v))
    elif w.ndim == 2:
        w, g, m, v = (t.reshape((1,) + t.shape) for t in (w, g, m, v))
    l, a, b = w.shape

    def body(w_ref, g_ref, m_ref, v_ref, d_ref, nm_ref, nv_ref):
        d_ref[...], nm_ref[...], nv_ref[...] = _adamw_math(w_ref[...], g_ref[...], m_ref[...], v_ref[...])

    blk = pl.BlockSpec((1, a, b), lambda li: (li, 0, 0))
    outs = pl.pallas_call(
        body, name=name,
        out_shape=tuple(jax.ShapeDtypeStruct((l, a, b), F32) for _ in range(3)),
        grid=(l,),
        in_specs=[blk, blk, blk, blk],
        out_specs=(blk, blk, blk),
        compiler_params=_cparams(("parallel",)),
    )(w, g, m, v)
    return tuple(o.reshape(shape) for o in outs)


def _adamw_layers(w, reduced, received, on_core0, m, v, *, name):
    l, a, b = w.shape
    assert l == DEPTH == 2
    ta = _row_tile(a, F32_SUBLANES, 384)

    def body(flag_ref, w_ref, r0_ref, o0_ref, r1_ref, o1_ref, m_ref, v_ref, g_ref, d_ref, nm_ref, nv_ref):
        core0 = flag_ref[...] > 0.5
        g0 = jnp.where(core0, r0_ref[...], o0_ref[...])
        g1 = jnp.where(core0, o1_ref[...], r1_ref[...])
        gv = jnp.where(pl.program_id(0) == 0, g0, g1)
        g_ref[0] = gv
        d_ref[0], nm_ref[0], nv_ref[0] = _adamw_math(w_ref[0], gv, m_ref[0], v_ref[0])

    blk = pl.BlockSpec((1, ta, b), lambda li, ai: (li, ai, 0))
    gblk = pl.BlockSpec((ta, b), lambda li, ai: (ai, 0))
    return pl.pallas_call(
        body, name=name,
        out_shape=tuple(jax.ShapeDtypeStruct((l, a, b), F32) for _ in range(4)),
        grid=(l, a // ta),
        in_specs=[pl.BlockSpec((1, 1), lambda li, ai: (0, 0)), blk, gblk, gblk, gblk, gblk, blk, blk],
        out_specs=(blk, blk, blk, blk),
        compiler_params=_cparams(("parallel", "parallel")),
    )(on_core0, w, reduced[0], received[0], reduced[1], received[1], m, v)


MESH_ID = pl.DeviceIdType.MESH
ANY = pl.BlockSpec(memory_space=pl.ANY)


def _position():
    x, y, c = lax.axis_index("x"), lax.axis_index("y"), lax.axis_index("c")
    chips = [(1 - x, y), (x, 1 - y), (1 - x, 1 - y)]
    return x, y, c, chips


def _chip_index(xy):
    return 2 * xy[0] + xy[1]


def _all_gather_weights(shards):
    n = len(shards)

    def body(*refs):
        ins, outs = refs[:n], refs[n:2 * n]
        send_sems, recv_sems, pass_send, pass_recv = refs[2 * n:]
        x, y, c, chips = _position()
        sibling = (x, y, 1 - c)

        def ici(a, k, chip_from):
            return pltpu.make_async_remote_copy(
                src_ref=ins[a].at[c], dst_ref=outs[a].at[c, _chip_index(chip_from)],
                send_sem=send_sems.at[a, k], recv_sem=recv_sems.at[a, k],
                device_id=(chips[k][0], chips[k][1], c), device_id_type=MESH_ID)

        def handoff(a, k, layer):
            slot = outs[a].at[layer, _chip_index(chips[k])]
            return pltpu.make_async_remote_copy(
                src_ref=slot, dst_ref=slot, send_sem=pass_send.at[a, k], recv_sem=pass_recv.at[a, k],
                device_id=sibling, device_id_type=MESH_ID)

        for a in range(n):
            for k in range(3):
                ici(a, k, (x, y)).start()
        for k in range(3):
            for a in range(n):
                ici(a, k, chips[k]).wait_recv()
                handoff(a, k, c).start()
        for k in range(3):
            for a in range(n):
                handoff(a, k, 1 - c).wait_recv()
        for a in range(n):
            for k in range(3):
                ici(a, k, (x, y)).wait_send()
                handoff(a, k, c).wait_send()

    gathered = pl.pallas_call(
        body, name="all_gather_weights",
        out_shape=tuple(jax.ShapeDtypeStruct((DEPTH, N_CHIPS) + t.shape[1:], t.dtype) for t in shards),
        in_specs=[ANY] * n, out_specs=tuple([ANY] * n),
        scratch_shapes=[pltpu.SemaphoreType.DMA((n, 3)), pltpu.SemaphoreType.DMA((n, 3)),
                        pltpu.SemaphoreType.DMA((n, 3)), pltpu.SemaphoreType.DMA((n, 3))],
    )(*shards)
    me = _chip_index((lax.axis_index("x"), lax.axis_index("y")))
    return [lax.dynamic_update_index_in_dim(g, t[:, None], me, axis=1) for g, t in zip(gathered, shards)]


HBM = pl.BlockSpec(memory_space=pltpu.HBM)
SEM = pl.BlockSpec(memory_space=pltpu.SEMAPHORE)
DATAFLOW_EFFECT = pltpu.SideEffectType.DATAFLOW_SIDE_EFFECTING
TOKEN_SHAPE = (8, LANE)


def _landing(shape, dtype):
    return pltpu.with_memory_space_constraint(lax.empty(shape, dtype), pltpu.HBM)


def _split_start(bufs, sem_shape, issue, *, name, after=None):
    n = len(bufs)
    n_in = n + (after is not None)

    def body(*refs):
        issue(refs[:n], refs[n_in], refs[n_in + 1])
        token = refs[-1]
        token[...] = jnp.zeros_like(token)

    operands = [pltpu.with_memory_space_constraint(t, pltpu.HBM) for t in bufs]
    outs = pl.pallas_call(
        body, name=name,
        out_shape=(pltpu.SemaphoreType.DMA(sem_shape), pltpu.SemaphoreType.DMA(sem_shape),
                   *[pltpu.HBM(t.shape, t.dtype) for t in bufs], jax.ShapeDtypeStruct(TOKEN_SHAPE, F32)),
        in_specs=[HBM] * n + [ANY] * (after is not None),
        out_specs=(SEM, SEM, *[HBM] * n, pl.BlockSpec(memory_space=pltpu.VMEM)),
        input_output_aliases={i: 2 + i for i in range(n)},
        compiler_params=pltpu.CompilerParams(has_side_effects=DATAFLOW_EFFECT),
    )(*operands, *([after] if after is not None else []))
    return outs[0], outs[1], list(outs[2:2 + n]), outs[-1]


def _split_wait(started, after, settle, *, name):
    send_sems, recv_sems, bufs, _ = started
    n = len(bufs)

    def body(*refs):
        settle(refs[:n], refs[n], refs[n + 1])

    outs = pl.pallas_call(
        body, name=name,
        out_shape=tuple(pltpu.HBM(t.shape, t.dtype) for t in bufs),
        in_specs=[HBM] * n + [SEM, SEM, ANY],
        out_specs=tuple([HBM] * n),
        input_output_aliases={i: i for i in range(n)},
        compiler_params=pltpu.CompilerParams(has_side_effects=DATAFLOW_EFFECT),
    )(*bufs, send_sems, recv_sems, after)
    return list(outs)


def _to_sibling(bufs, n, send_sems, recv_sems):
    x, y, c, _ = _position()
    return [pltpu.make_async_remote_copy(
        src_ref=bufs[a], dst_ref=bufs[n + a], send_sem=send_sems.at[a], recv_sem=recv_sems.at[a],
        device_id=(x, y, 1 - c), device_id_type=MESH_ID) for a in range(n)]


def _sibling_push_start(layer, sender_is_reducer, arrays, *, name):
    n = len(arrays)
    sender = layer if sender_is_reducer else 1 - layer

    def issue(bufs, send_sems, recv_sems):
        @pl.when(lax.axis_index("c") == sender)
        def _():
            for cp in _to_sibling(bufs, n, send_sems, recv_sems):
                cp.start()

    lands = [_landing(t.shape, t.dtype) for t in arrays]
    return _split_start(list(arrays) + lands, (n,), issue, name=name)


def _sibling_push_wait(layer, sender_is_reducer, started, after, *, name):
    n = len(started[2]) // 2
    sender = layer if sender_is_reducer else 1 - layer

    def settle(bufs, send_sems, recv_sems):
        c = lax.axis_index("c")

        @pl.when(c == sender)
        def _():
            for cp in _to_sibling(bufs, n, send_sems, recv_sems):
                cp.wait_send()

        @pl.when(c != sender)
        def _():
            for cp in _to_sibling(bufs, n, send_sems, recv_sems):
                cp.wait_recv()

    outs = _split_wait(started, after, settle, name=name)
    return outs[:n], outs[n:]


def _chip_copies(bufs, n, send_sems, recv_sems):
    x, y, c, chips = _position()
    return [pltpu.make_async_remote_copy(
        src_ref=bufs[a].at[_chip_index(chips[k])], dst_ref=bufs[n + a].at[k],
        send_sem=send_sems.at[3 * a + k], recv_sem=recv_sems.at[3 * a + k],
        device_id=(chips[k][0], chips[k][1], c), device_id_type=MESH_ID)
        for a in range(n) for k in range(3)]


def _chip_exchange_start(layer, parts, *, name):
    n = len(parts)

    def issue(bufs, send_sems, recv_sems):
        @pl.when(lax.axis_index("c") == layer)
        def _():
            for cp in _chip_copies(bufs, n, send_sems, recv_sems):
                cp.start()

    lands = [_landing((3,) + t.shape[1:], t.dtype) for t in parts]
    return _split_start(list(parts) + lands, (3 * n,), issue, name=name)


def _chip_exchange_wait(layer, started, after, *, name):
    n = len(started[2]) // 2

    def settle(bufs, send_sems, recv_sems):
        @pl.when(lax.axis_index("c") == layer)
        def _():
            for cp in _chip_copies(bufs, n, send_sems, recv_sems):
                cp.wait()

    outs = _split_wait(started, after, settle, name=name)
    return outs[:n], outs[n:]


def _gather_copies(layer, bufs, n, send_sems, recv_sems, arriving):
    x, y, c, chips = _position()
    me = _chip_index((x, y))
    return [pltpu.make_async_remote_copy(
        src_ref=bufs[a], dst_ref=bufs[n + a].at[_chip_index(chips[k]) if arriving else me],
        send_sem=send_sems.at[3 * a + k], recv_sem=recv_sems.at[3 * a + k],
        device_id=(chips[k][0], chips[k][1], c), device_id_type=MESH_ID)
        for a in range(n) for k in range(3)]


def _gather_start(layer, shards, *, name, after=None):
    n = len(shards)

    def issue(bufs, send_sems, recv_sems):
        @pl.when(lax.axis_index("c") == layer)
        def _():
            for cp in _gather_copies(layer, bufs, n, send_sems, recv_sems, False):
                cp.start()

    lands = [_landing((N_CHIPS,) + t.shape, t.dtype) for t in shards]
    return _split_start(list(shards) + lands, (3 * n,), issue, name=name, after=after)


def _gather_wait(layer, started, after, *, name):
    n = len(started[2]) // 2

    def settle(bufs, send_sems, recv_sems):
        @pl.when(lax.axis_index("c") == layer)
        def _():
            for cp in _gather_copies(layer, bufs, n, send_sems, recv_sems, False):
                cp.wait_send()
            for cp in _gather_copies(layer, bufs, n, send_sems, recv_sems, True):
                cp.wait_recv()

    return _split_wait(started, after, settle, name=name)[n:]


def _handoff_copies(bufs, n, send_sems, recv_sems):
    x, y, c, chips = _position()
    out = []
    for a in range(n):
        for k in range(3):
            slot = bufs[a].at[_chip_index(chips[k])]
            out.append(pltpu.make_async_remote_copy(
                src_ref=slot, dst_ref=slot, send_sem=send_sems.at[3 * a + k], recv_sem=recv_sems.at[3 * a + k],
                device_id=(x, y, 1 - c), device_id_type=MESH_ID))
    return out


def _handoff_start(layer, gathered, *, name):
    n = len(gathered)

    def issue(bufs, send_sems, recv_sems):
        @pl.when(lax.axis_index("c") == layer)
        def _():
            for cp in _handoff_copies(bufs, n, send_sems, recv_sems):
                cp.start()

    return _split_start(list(gathered), (3 * n,), issue, name=name)


def _handoff_wait(layer, started, after, *, name):
    n = len(started[2])

    def settle(bufs, send_sems, recv_sems):
        c = lax.axis_index("c")

        @pl.when(c == layer)
        def _():
            for cp in _handoff_copies(bufs, n, send_sems, recv_sems):
                cp.wait_send()

        @pl.when(c != layer)
        def _():
            for cp in _handoff_copies(bufs, n, send_sems, recv_sems):
                cp.wait_recv()

    return _split_wait(started, after, settle, name=name)


F32_SUBLANES = 8
BF16_SUBLANES = 16


def _row_tile(a, sublanes=BF16_SUBLANES, max_rows=704):
    best = None
    for cand in range(sublanes, min(a, max_rows) + 1, sublanes):
        if a % cand == 0:
            best = cand
    assert best is not None, a
    return best


def _pair_sum(mine, sib, *, name):
    nchip, a, b = sib.shape
    ta = _row_tile(a)

    def body(m_ref, s_ref, o_ref):
        o_ref[...] = (m_ref[...].astype(F32) + s_ref[...].astype(F32)).astype(BF16)

    blk = pl.BlockSpec((1, ta, b), lambda j, r: (j, r, 0))
    return pl.pallas_call(
        body, name=name,
        out_shape=jax.ShapeDtypeStruct((nchip, a, b), BF16),
        grid=(nchip, a // ta),
        in_specs=[blk, blk], out_specs=blk,
        compiler_params=_cparams(("parallel", "parallel")),
    )(mine, sib)


def _total_sum(own, recv, *, name):
    a, b = own.shape
    ta = _row_tile(a)

    def body(o_ref, r_ref, t_ref):
        acc = o_ref[...].astype(F32)
        for k in range(3):
            acc = acc + r_ref[k].astype(F32)
        t_ref[...] = acc

    blk = pl.BlockSpec((ta, b), lambda r: (r, 0))
    return pl.pallas_call(
        body, name=name,
        out_shape=jax.ShapeDtypeStruct((a, b), F32),
        grid=(a // ta,),
        in_specs=[blk, pl.BlockSpec((3, ta, b), lambda r: (0, r, 0))], out_specs=blk,
        compiler_params=_cparams(("parallel",)),
    )(own, recv)


def _all_reduce_small(packed):
    rows, width = packed.shape

    def body(x_ref, out_ref, gath, send_sems, recv_sems, local_sem):
        x, y, c, chips = _position()
        me, sibling = (x, y, c), (x, y, 1 - c)

        def slot(px, py, pc):
            return gath.at[4 * px + 2 * py + pc]

        def copy(k, block, to, src=None):
            return pltpu.make_async_remote_copy(
                src_ref=slot(*block) if src is None else src, dst_ref=slot(*block),
                send_sem=send_sems.at[k], recv_sem=recv_sems.at[k], device_id=to, device_id_type=MESH_ID)

        mine = pltpu.make_async_copy(x_ref, slot(*me), local_sem)
        mine.start()
        first = [copy(0, me, sibling, src=x_ref)]
        first += [copy(1 + j, me, (*chip, c), src=x_ref) for j, chip in enumerate(chips)]
        for cp in first:
            cp.start()
        passed = [copy(4 + j, (*chip, c), sibling) for j, chip in enumerate(chips)]
        for j, chip in enumerate(chips):
            copy(1 + j, (*chip, c), me).wait_recv()
            passed[j].start()
        copy(0, sibling, me).wait_recv()
        for j, chip in enumerate(chips):
            copy(4 + j, (*chip, 1 - c), me).wait_recv()
        for cp in first + passed:
            cp.wait_send()
        mine.wait()
        acc = gath[0]
        for d in range(1, N_DEV):
            acc = acc + gath[d]
        out_ref[...] = acc

    return pl.pallas_call(
        body, name="all_reduce_small",
        out_shape=jax.ShapeDtypeStruct((rows, width), F32),
        in_specs=[pl.BlockSpec(memory_space=pltpu.VMEM)],
        out_specs=pl.BlockSpec(memory_space=pltpu.VMEM),
        scratch_shapes=[pltpu.VMEM((N_DEV, rows, width), F32), pltpu.SemaphoreType.DMA((7,)),
                        pltpu.SemaphoreType.DMA((7,)), pltpu.SemaphoreType.DMA],
    )(packed)


def _mixer_forward_branches(x0, w, s, dep=None):
    h = _rms_fwd(x0, w["norm1_g"], name="rms1_fwd")
    proj = _matmul(h, w["w_in_t"], mode="nt", out_dtype=BF16, tm=1024, tn=1664, tk=1024, name="proj_fwd",
                   dep=dep)
    oa, states = _gla_fwd(proj, w["w_fg2"], w["b_fg"], w["gla_norm_g"], s=s)
    cb_in = _conv_fwd(proj, w["conv_wb"], s=s)
    return dict(x0=x0, h=h, proj=proj, oa=oa, states=states, cb_in=cb_in)


def _mixer_forward_out(sv, w, s, dep=None):
    ya = _matmul(sv["oa"], w["w_oa"], mode="nn", out_dtype=BF16, tm=1024, tn=1024, tk=1024, name="ya_fwd",
                 dep=dep)
    yb = _matmul(sv["cb_in"], w["w_ob"], mode="nn", out_dtype=BF16, tm=1024, tn=1024, tk=1024, name="yb_fwd")
    mix = _mix_fwd(sv["proj"], ya, yb, s=s)
    x1 = _matmul(mix, w["w_o"], mode="nn", out_dtype=F32, tm=1024, tn=1024, tk=1024, name="wo_fwd",
                 add=sv["x0"])
    return x1, dict(sv, ya=ya, yb=yb, mix=mix)


def _ffn_forward_hidden(x1, w, s, dep=None):
    h2 = _rms_fwd(x1, w["norm2_g"], name="rms2_fwd")
    gt = _matmul(h2, w["w_gate_t"], mode="nt", out_dtype=BF16, tm=1024, tn=1408, tk=1024, name="ffn_gate_fwd",
                 dep=dep)
    up = _matmul(h2, w["w_up_t"], mode="nt", out_dtype=BF16, tm=1024, tn=1408, tk=1024, name="ffn_up_fwd")
    return dict(x1=x1, h2=h2, gt=gt, up=up, hid=_swiglu_fwd(gt, up, s=s))


def _ffn_forward_out(sv, w, dep=None):
    return _matmul(sv["hid"], w["w_ffn_down"], mode="nn", out_dtype=F32, tm=1024, tn=1024, tk=FFN_HIDDEN,
                   name="ffn_out_fwd", add=sv["x1"], dep=dep)


def _ffn_backward_grads(dx2b, w, sv, s):
    g = {}
    dhid = _matmul(dx2b, w["w_ffn_down"], mode="nt", out_dtype=BF16, tm=1024, tn=FFN_HIDDEN, tk=1024,
                   name="ffn_out_bwd")
    g["w_ffn_down"] = _matmul(sv["hid"], dx2b, mode="tn", out_dtype=BF16, tm=1408, tn=1024, tk=1024,
                              name="ffn_out_wgrad")
    dgt, dup = _swiglu_bwd(sv["gt"], sv["up"], dhid, s=s)
    g["w_gate_t"] = _matmul(dgt, sv["h2"], mode="tn", out_dtype=BF16, tm=1408, tn=1024, tk=1024,
                            name="ffn_gate_wgrad")
    g["w_up_t"] = _matmul(dup, sv["h2"], mode="tn", out_dtype=BF16, tm=1408, tn=1024, tk=1024,
                          name="ffn_up_wgrad")
    return g, dgt, dup, dhid


def _ffn_backward_input(dgt, dup, dx2, w, sv, dep=None):
    dh2 = _matmul(dgt, w["w_gate_t"], mode="nn", out_dtype=F32, tm=1024, tn=1024, tk=FFN_HIDDEN,
                  name="ffn_gate_bwd", dep=dep)
    dh2 = _matmul(dup, w["w_up_t"], mode="nn", out_dtype=F32, tm=1024, tn=1024, tk=FFN_HIDDEN,
                  name="ffn_up_bwd", add=dh2)
    return _rms_bwd(sv["x1"], w["norm2_g"], dh2, dx2, name="rms2_bwd")


def _mixer_backward_branches(dx1b, w, sv, s, dep=None):
    g = {}
    dmix = _matmul(dx1b, w["w_o"], mode="nt", out_dtype=BF16, tm=1024, tn=1024, tk=1024, name="wo_bwd",
                   dep=dep)
    g["w_o"] = _matmul(sv["mix"], dx1b, mode="tn", out_dtype=BF16, tm=1024, tn=1024, tk=2048, name="wo_wgrad")
    dya, dyb, dproj = _mix_bwd(sv["proj"], sv["ya"], sv["yb"], dmix, s=s)
    dcb = _matmul(dyb, w["w_ob"], mode="nt", out_dtype=BF16, tm=1024, tn=1024, tk=1024, name="yb_bwd")
    g["w_ob"] = _matmul(sv["cb_in"], dyb, mode="tn", out_dtype=BF16, tm=1024, tn=1024, tk=2048, name="yb_wgrad")
    doa = _matmul(dya, w["w_oa"], mode="nt", out_dtype=BF16, tm=1024, tn=1024, tk=1024, name="ya_bwd")
    g["w_oa"] = _matmul(sv["oa"], dya, mode="tn", out_dtype=BF16, tm=1024, tn=1024, tk=2048, name="ya_wgrad")
    dproj, g["conv_wb"] = _conv_bwd(sv["proj"], dcb, w["conv_wb"], dproj, s=s)
    dproj, dfz, g["gla_norm_g"], g["w_fg2"], g["b_fg"] = _gla_bwd(
        sv["proj"], doa, sv["states"], w["w_fg2"], w["b_fg"], w["gla_norm_g"], dproj, s=s)
    return g, _put_fz(dproj, dfz, s=s)


def _proj_wgrad(dproj, sv, dep=None):
    return _matmul(dproj, sv["h"], mode="tn", out_dtype=BF16, tm=1664, tn=1024, tk=1024, name="proj_wgrad",
                   dep=dep)


def _proj_bwd(dproj, w, dep=None):
    return _matmul(dproj, w["w_in_t"], mode="nn", out_dtype=F32, tm=1024, tn=1024, tk=1664, name="proj_bwd",
                   dep=dep)


def _cols_from_chips(t):
    return jnp.transpose(t, (1, 0, 2)).reshape(t.shape[1], -1)


W_IN_ROWS = IN_WIDTH // N_CHIPS
W_IN_ROWS_PAD = -(-W_IN_ROWS // BF16_SUBLANES) * BF16_SUBLANES


def _w_in_t_shard(t):
    return jnp.pad(jnp.transpose(t, (0, 2, 1)), ((0, 0), (0, W_IN_ROWS_PAD - W_IN_ROWS), (0, 0)))


def _w_in_t_unshard(t):
    return jnp.transpose(t[:, :W_IN_ROWS], (0, 2, 1))


def _w_in_t_to_kernel(t):
    full = t[:, :W_IN_ROWS].reshape(IN_WIDTH, D_MODEL)
    pad = jnp.zeros((FZ_PAD - GATE_RANK, D_MODEL), t.dtype)
    return jnp.concatenate([full[:FZ_ORIG], full[FZ_ORIG + GATE_RANK:], full[FZ_ORIG:FZ_ORIG + GATE_RANK], pad],
                           axis=0)


def _w_in_t_from_kernel(g):
    full = jnp.concatenate([g[:FZ_ORIG], g[COL_FZ:COL_FZ + GATE_RANK], g[FZ_ORIG:COL_FZ]], axis=0)
    return jnp.pad(full.reshape(N_CHIPS, W_IN_ROWS, D_MODEL), ((0, 0), (0, W_IN_ROWS_PAD - W_IN_ROWS), (0, 0)))


def kernel(x, norm1_g, w_in, w_fg2, b_fg, gla_norm_g, w_oa, conv_w, conv_b, w_ob, w_o, norm2_g, w_ffn_gate, w_ffn_up, w_ffn_down, final_g, loss_target, m_norm1_g, m_w_in, m_w_fg2, m_b_fg, m_gla_norm_g, m_w_oa, m_conv_w, m_conv_b, m_w_ob, m_w_o, m_norm2_g, m_w_ffn_gate, m_w_ffn_up, m_w_ffn_down, m_final_g, v_norm1_g, v_w_in, v_w_fg2, v_b_fg, v_gla_norm_g, v_w_oa, v_conv_w, v_conv_b, v_w_ob, v_w_o, v_norm2_g, v_w_ffn_gate, v_w_ffn_up, v_w_ffn_down, v_final_g):
    cx_ = lax.axis_index("x")
    cy_ = lax.axis_index("y")
    cc_ = lax.axis_index("c")
    me = 2 * cx_ + cy_
    on_core0 = jnp.where(cc_ == 0, 1.0, 0.0).astype(F32).reshape(1, 1)

    def swap(t):
        return jnp.swapaxes(t, 1, 2)

    big_names = ["w_in", "w_oa", "w_ob", "w_o", "w_ffn_gate", "w_ffn_up", "w_ffn_down"]
    views = dict(
        w_in=tuple(_w_in_t_shard(t) for t in (w_in, m_w_in, v_w_in)),
        w_oa=(w_oa, m_w_oa, v_w_oa), w_ob=(w_ob, m_w_ob, v_w_ob), w_o=(w_o, m_w_o, v_w_o),
        w_ffn_gate=tuple(swap(t) for t in (w_ffn_gate, m_w_ffn_gate, v_w_ffn_gate)),
        w_ffn_up=tuple(swap(t) for t in (w_ffn_up, m_w_ffn_up, v_w_ffn_up)),
        w_ffn_down=(w_ffn_down, m_w_ffn_down, v_w_ffn_down))
    from_view = dict(w_in=_w_in_t_unshard, w_ffn_gate=swap, w_ffn_up=swap)

    s = x.shape[1]

    conv_w_p = jnp.pad(conv_w, ((0, 0), (0, 8 - conv_w.shape[1]), (0, 0)))
    w_fg2_all, conv_w_all = _all_gather_weights([w_fg2, conv_w_p])

    proj_names = ["w_in"]
    rest_names = ["w_oa", "w_ob", "w_o", "w_ffn_gate", "w_ffn_up", "w_ffn_down"]
    mixer_names = ["w_oa", "w_ob", "w_o"]
    ffn_names = ["w_ffn_gate", "w_ffn_up", "w_ffn_down"]
    weight_key = dict(w_in="w_in_t", w_ffn_gate="w_gate_t", w_ffn_up="w_up_t")
    full_shape = dict(w_oa=(VAL_WIDTH, D_MODEL), w_ob=(CONV_CH, D_MODEL), w_o=(D_MODEL, D_MODEL),
                      w_ffn_gate=(FFN_HIDDEN, D_MODEL), w_ffn_up=(FFN_HIDDEN, D_MODEL),
                      w_ffn_down=(FFN_HIDDEN, D_MODEL))

    def shards_of(l, names):
        return [views[n][0][l].astype(BF16) for n in names]

    def small_weights(l):
        w_fg2_full = _cols_from_chips(w_fg2_all[l])
        conv_w_full = _cols_from_chips(conv_w_all[l])[:3]
        return dict(
            norm1_g=norm1_g[l], norm2_g=norm2_g[l],
            w_fg2=jnp.pad(w_fg2_full, ((0, FZ_PAD - GATE_RANK), (0, 0))).astype(BF16),
            b_fg=b_fg[l].reshape(1, KEY_WIDTH), gla_norm_g=gla_norm_g[l].reshape(1, HEAD_V),
            conv_wb=(jnp.pad(conv_w_full, ((0, 5), (0, 0)))
                     + jnp.pad(conv_b[l].reshape(1, CONV_CH), ((3, 4), (0, 0)))))

    def full_weights(names, gathered, shards):
        out = {}
        for n, g, t in zip(names, gathered, shards):
            g = lax.dynamic_update_index_in_dim(g, t[None], me, axis=0)
            out[weight_key.get(n, n)] = _w_in_t_to_kernel(g) if n == "w_in" else g.reshape(full_shape[n])
        return out

    def per_chip(g, names):
        out = []
        for n in names:
            t = g[weight_key.get(n, n)]
            out.append(_w_in_t_from_kernel(t) if n == "w_in"
                       else t.reshape(N_CHIPS, t.shape[0] // N_CHIPS, t.shape[1]))
        return out

    def gather(l, names, tag, after=None):
        shards = shards_of(l, names)
        return shards, _gather_start(l, shards, name=f"gather_{tag}_start", after=after)

    def gathered_to_sibling(l, started, after, tag):
        return _handoff_start(l, _gather_wait(l, started, after, name=f"gather_{tag}_wait"),
                              name=f"handoff_{tag}_start")

    def feed(l, g, names, tag):
        return _sibling_push_start(l, False, per_chip(g, names), name=f"feed_{tag}_start")

    def pair_and_exchange(l, fed, after, names, tag):
        mine, sib = _sibling_push_wait(l, False, fed, after, name=f"feed_{tag}_wait")
        parts = [_pair_sum(a, b, name=f"pair_sum_{tag}_{n}") for n, a, b in zip(names, mine, sib)]
        return _chip_exchange_start(l, parts, name=f"exchange_{tag}_start")

    def total_and_share(l, swapped, after, names, tag):
        parts, recvs = _chip_exchange_wait(l, swapped, after, name=f"exchange_{tag}_wait")
        owns = [lax.dynamic_index_in_dim(p, me, 0, keepdims=False) for p in parts]
        totals = [_total_sum(o, rc, name=f"total_{tag}_{n}") for n, o, rc in zip(names, owns, recvs)]
        return _sibling_push_start(l, True, totals, name=f"share_{tag}_start")

    def shared(l, sharing, after, names, tag):
        totals, others = _sibling_push_wait(l, True, sharing, after, name=f"share_{tag}_wait")
        return {n: (t, o) for n, t, o in zip(names, totals, others)}

    shards_p, started = gather(0, proj_names, "p0")
    started = gathered_to_sibling(0, started, started[3], "p0")
    w0 = small_weights(0)
    w0.update(full_weights(proj_names, _handoff_wait(0, started, started[3], name="handoff_p0_wait"), shards_p))
    shards_r, started = gather(0, rest_names, "r0", after=w0["w_in_t"])
    sv0 = _mixer_forward_branches(x[0], w0, s, dep=started[3])
    started = gathered_to_sibling(0, started, sv0["oa"], "r0")
    w0.update(full_weights(rest_names, _handoff_wait(0, started, started[3], name="handoff_r0_wait"), shards_r))

    shards_1, started = gather(1, big_names, "l1", after=w0["w_o"])
    x1, sv0m = _mixer_forward_out(sv0, w0, s, dep=started[3])
    sv0f = _ffn_forward_hidden(x1, w0, s)
    started = gathered_to_sibling(1, started, sv0f["hid"], "l1")
    x2 = _ffn_forward_out(sv0f, w0, dep=started[3])
    w1 = small_weights(1)
    w1.update(full_weights(big_names, _handoff_wait(1, started, x2, name="handoff_l1_wait"), shards_1))

    x3, sv1m = _mixer_forward_out(_mixer_forward_branches(x2, w1, s), w1, s)
    sv1f = _ffn_forward_hidden(x3, w1, s)
    x4 = _ffn_forward_out(sv1f, w1)
    loss_local, dx, dxb, dgf = _loss_head(x4, final_g, loss_target[0])
    loss = lax.psum(loss_local[0, 0], ("x", "y", "c"))

    g1, dgt, dup, _ = _ffn_backward_grads(dxb, w1, sv1f, s)
    dx_mid, dxb_mid, g1["norm2_g"] = _ffn_backward_input(dgt, dup, dx, w1, sv1f)
    gm, dproj = _mixer_backward_branches(dxb_mid, w1, sv1m, s)
    g1.update(gm)
    g1["w_in_t"] = _proj_wgrad(dproj, sv1m)
    fed_1 = feed(1, g1, big_names, "l1")
    dh = _proj_bwd(dproj, w1, dep=fed_1[3])
    dx, dxb, g1["norm1_g"] = _rms_bwd(sv1m["x0"], norm1_g[1], dh, dx_mid, name="rms1_bwd")

    g0, dgt, dup, dhid = _ffn_backward_grads(dxb, w0, sv0f, s)
    swap_1 = pair_and_exchange(1, fed_1, dhid, big_names, "l1")
    fed_f = feed(0, g0, ffn_names, "f0")
    dx_mid, dxb_mid, g0["norm2_g"] = _ffn_backward_input(dgt, dup, dx, w0, sv0f, dep=swap_1[3] + fed_f[3])
    swap_f = pair_and_exchange(0, fed_f, dx_mid, ffn_names, "f0")
    gm, dproj = _mixer_backward_branches(dxb_mid, w0, sv0m, s, dep=swap_f[3])
    g0.update(gm)
    share_1 = total_and_share(1, swap_1, dproj, big_names, "l1")
    share_f = total_and_share(0, swap_f, dproj, ffn_names, "f0")
    fed_m = feed(0, g0, mixer_names, "m0")
    g0["w_in_t"] = _proj_wgrad(dproj, sv0m, dep=share_1[3] + share_f[3] + fed_m[3])
    fed_p = feed(0, g0, proj_names, "p0")
    swap_m = pair_and_exchange(0, fed_m, g0["w_in_t"], mixer_names, "m0")
    dh = _proj_bwd(dproj, w0, dep=fed_p[3] + swap_m[3])
    reduced1 = shared(1, share_1, dh, big_names, "l1")
    reduced0 = shared(0, share_f, dh, ffn_names, "f0")
    grad_x, _, g0["norm1_g"] = _rms_bwd(sv0m["x0"], norm1_g[0], dh, dx_mid, name="rms1_bwd")

    swap_p = pair_and_exchange(0, fed_p, grad_x, proj_names, "p0")
    share_m = total_and_share(0, swap_m, swap_p[3], mixer_names, "m0")
    share_p = total_and_share(0, swap_p, share_m[3], proj_names, "p0")
    reduced0.update(shared(0, share_m, share_p[3], mixer_names, "m0"))
    reduced0.update(shared(0, share_p, share_p[3], proj_names, "p0"))
    grads = [g0, g1]

    def small_rows(t):
        return t.reshape(-1, D_MODEL)

    def tile_rows(t):
        return jnp.pad(t, ((0, -t.shape[0] % F32_SUBLANES), (0, 0)))

    g0, g1 = grads
    pieces = [
        jnp.concatenate([g0["norm1_g"], g1["norm1_g"]], axis=0),
        jnp.concatenate([g0["norm2_g"], g1["norm2_g"]], axis=0),
        dgf,
        small_rows(jnp.concatenate([g0["b_fg"], g1["b_fg"]], axis=1)),
        small_rows(jnp.concatenate([g0["gla_norm_g"], g1["gla_norm_g"],
                                    jnp.zeros((1, D_MODEL - 2 * HEAD_V), F32)], axis=1)),
        jnp.concatenate([g0["conv_wb"][3:4], g1["conv_wb"][3:4]], axis=0),
        jnp.concatenate([g0["conv_wb"][:3], g1["conv_wb"][:3]], axis=0),
        small_rows(jnp.stack([g0["w_fg2"][:GATE_RANK], g1["w_fg2"][:GATE_RANK]])),
    ]
    small = _all_reduce_small(jnp.concatenate([tile_rows(p) for p in pieces], axis=0))
    sg = dict(
        norm1_g=small[0:2], norm2_g=small[8:10], final_g=small[16],
        b_fg=small[24].reshape(DEPTH, KEY_WIDTH), gla_norm_g=small[32, :DEPTH * HEAD_V].reshape(DEPTH, HEAD_V),
        conv_b=small[40:42],
        conv_w=lax.dynamic_slice_in_dim(small[48:54].reshape(DEPTH, 3, CONV_CH), me * (CONV_CH // N_CHIPS),
                                        CONV_CH // N_CHIPS, axis=2),
        w_fg2=lax.dynamic_slice_in_dim(small[56:72].reshape(DEPTH, GATE_RANK, KEY_WIDTH),
                                       me * (KEY_WIDTH // N_CHIPS), KEY_WIDTH // N_CHIPS, axis=2),
    )

    small_params = dict(norm1_g=(norm1_g, m_norm1_g, v_norm1_g), w_fg2=(w_fg2, m_w_fg2, v_w_fg2),
                        b_fg=(b_fg, m_b_fg, v_b_fg), gla_norm_g=(gla_norm_g, m_gla_norm_g, v_gla_norm_g),
                        conv_w=(conv_w, m_conv_w, v_conv_w), conv_b=(conv_b, m_conv_b, v_conv_b),
                        norm2_g=(norm2_g, m_norm2_g, v_norm2_g), final_g=(final_g, m_final_g, v_final_g))
    order = ["norm1_g", "w_in", "w_fg2", "b_fg", "gla_norm_g", "w_oa", "conv_w", "conv_b", "w_ob", "w_o",
             "norm2_g", "w_ffn_gate", "w_ffn_up", "w_ffn_down", "final_g"]
    results = {}
    for i, n in enumerate(big_names):
        w_, m_, v_ = views[n]
        outs = _adamw_layers(w_, (reduced0[n][0], reduced1[n][0]), (reduced0[n][1], reduced1[n][1]), on_core0,
                             m_, v_, name="adamw_" + n)
        back = from_view.get(n)
        results[n] = tuple(back(r) for r in outs) if back else outs
    for n, (w_, m_, v_) in small_params.items():
        results[n] = (sg[n],) + _adamw(w_, sg[n], m_, v_, name="adamw_" + n)
    return (loss, grad_x[None], *[results[n][0] for n in order], *[results[n][1] for n in order],
            *[results[n][2] for n in order], *[results[n][3] for n in order])
```

```python
import functools

import jax
import jax.numpy as jnp
from jax import lax
from jax.experimental import pallas as pl
from jax.experimental.pallas import tpu as pltpu

F32 = jnp.float32
BF16 = jnp.bfloat16

D_MODEL = 1024
DEPTH = 2
CHUNK = 64
GLA_HEADS = 4
KEY_WIDTH = 512
VAL_WIDTH = 1024
HEAD_K = 128
HEAD_V = 256
GATE_RANK = 16
GATE_TAU = 16.0
CONV_CH = 1024
FFN_HIDDEN = 2816
IN_WIDTH = 8208
NORM_EPS = 1e-6
Q_SCALE = HEAD_K ** -0.5
ADAM_LR = 0.001
ADAM_B1 = 0.9
ADAM_B2 = 0.999
ADAM_EPS = 1e-08
ADAM_WD = 0.01
ADAM_STEP = 10

N_CHIPS = 4
N_DEV = 8

LANE = 128
FZ_PAD = LANE
PROJ_W = 8192 + FZ_PAD
COL_Q, COL_K, COL_V, COL_R, COL_GBI, COL_GCI, COL_CX, COL_GA, COL_GB, COL_FZ = (
    0, 512, 1024, 2048, 3072, 4096, 5120, 6144, 7168, 8192)
FZ_ORIG = 3072

VMEM_LIMIT = 52 * 1024 * 1024
HALO = 16


def _cparams(sem=None):
    return pltpu.CompilerParams(dimension_semantics=sem, vmem_limit_bytes=VMEM_LIMIT)


def _sigmoid(x):
    return jax.nn.sigmoid(x)


def _log_sigmoid(x):
    return jnp.minimum(x, 0.0) - jnp.log1p(jnp.exp(-jnp.abs(x)))


_DOT_DIMS = {
    "nn": (((1,), (0,)), ((), ())),
    "nt": (((1,), (1,)), ((), ())),
    "tn": (((0,), (0,)), ((), ())),
}


def _matmul(a, b, *, mode, out_dtype, tm, tn, tk, name, add=None, dep=None):
    if mode == "nn":
        (m, k), n = a.shape, b.shape[1]
    elif mode == "nt":
        (m, k), n = a.shape, b.shape[0]
    else:
        (k, m), n = a.shape, b.shape[1]
    tm, tn, tk = min(tm, m), min(tn, n), min(tk, k)
    assert m % tm == 0 and n % tn == 0 and k % tk == 0, (name, m, n, k, tm, tn, tk)
    nk = k // tk
    has_add = add is not None
    has_dep = dep is not None

    def body(*refs):
        if has_dep:
            refs = refs[1:]
        if has_add:
            a_ref, b_ref, add_ref, o_ref = refs[:4]
            scratch = refs[4:]
        else:
            a_ref, b_ref, o_ref = refs[:3]
            add_ref = None
            scratch = refs[3:]
        part = lax.dot_general(a_ref[...], b_ref[...], _DOT_DIMS[mode], preferred_element_type=F32)

        def finish(acc):
            if add_ref is not None:
                acc = acc + add_ref[...].astype(F32)
            o_ref[...] = acc.astype(o_ref.dtype)

        if nk == 1:
            finish(part)
        else:
            acc_ref = scratch[0]
            kk = pl.program_id(2)

            @pl.when(kk == 0)
            def _():
                acc_ref[...] = part

            @pl.when(kk > 0)
            def _():
                acc_ref[...] += part

            @pl.when(kk == nk - 1)
            def _():
                finish(acc_ref[...])

    if mode == "nn":
        a_spec = pl.BlockSpec((tm, tk), lambda i, j, kk: (i, kk))
        b_spec = pl.BlockSpec((tk, tn), lambda i, j, kk: (kk, j))
    elif mode == "nt":
        a_spec = pl.BlockSpec((tm, tk), lambda i, j, kk: (i, kk))
        b_spec = pl.BlockSpec((tn, tk), lambda i, j, kk: (j, kk))
    else:
        a_spec = pl.BlockSpec((tk, tm), lambda i, j, kk: (kk, i))
        b_spec = pl.BlockSpec((tk, tn), lambda i, j, kk: (kk, j))
    o_spec = pl.BlockSpec((tm, tn), lambda i, j, kk: (i, j))
    in_specs = [a_spec, b_spec] + ([o_spec] if has_add else [])
    operands = (a, b) + ((add,) if has_add else ())
    if has_dep:
        in_specs = [pl.BlockSpec(dep.shape, lambda i, j, kk: (0, 0))] + in_specs
        operands = (dep,) + operands
    return pl.pallas_call(
        body,
        name=name,
        out_shape=jax.ShapeDtypeStruct((m, n), out_dtype),
        grid=(m // tm, n // tn, nk),
        in_specs=in_specs,
        out_specs=o_spec,
        scratch_shapes=[pltpu.VMEM((tm, tn), F32)] if nk > 1 else [],
        compiler_params=_cparams(("parallel", "parallel", "arbitrary")),
    )(*operands)


def _rms_fwd(x, g, *, name):
    s, d = x.shape
    tr = min(512, s)

    def body(x_ref, g_ref, h_ref):
        xv = x_ref[...]
        rs = lax.rsqrt(jnp.mean(xv * xv, axis=-1, keepdims=True) + NORM_EPS)
        h_ref[...] = (xv * rs * g_ref[...]).astype(BF16)

    return pl.pallas_call(
        body, name=name,
        out_shape=jax.ShapeDtypeStruct((s, d), BF16),
        grid=(s // tr,),
        in_specs=[pl.BlockSpec((tr, d), lambda i: (i, 0)), pl.BlockSpec((1, d), lambda i: (0, 0))],
        out_specs=pl.BlockSpec((tr, d), lambda i: (i, 0)),
        compiler_params=_cparams(("parallel",)),
    )(x, g.reshape(1, d))


def _rms_bwd(x, g, dh, dres, *, name):
    s, d = x.shape
    tr = min(512, s)

    def body(x_ref, g_ref, dh_ref, dres_ref, dx_ref, dxb_ref, dg_ref):
        i = pl.program_id(0)
        xv = x_ref[...]
        rs = lax.rsqrt(jnp.mean(xv * xv, axis=-1, keepdims=True) + NORM_EPS)
        n = xv * rs
        dhv = dh_ref[...].astype(F32)
        dn = dhv * g_ref[...]
        dx = dres_ref[...] + rs * (dn - n * jnp.mean(dn * n, axis=-1, keepdims=True))
        dx_ref[...] = dx
        dxb_ref[...] = dx.astype(BF16)
        part = jnp.sum(dhv * n, axis=0, keepdims=True)

        @pl.when(i == 0)
        def _():
            dg_ref[...] = part

        @pl.when(i > 0)
        def _():
            dg_ref[...] += part

    row = pl.BlockSpec((tr, d), lambda i: (i, 0))
    vec = pl.BlockSpec((1, d), lambda i: (0, 0))
    return pl.pallas_call(
        body, name=name,
        out_shape=(jax.ShapeDtypeStruct((s, d), F32), jax.ShapeDtypeStruct((s, d), BF16),
                   jax.ShapeDtypeStruct((1, d), F32)),
        grid=(s // tr,),
        in_specs=[row, vec, row, row],
        out_specs=(row, row, vec),
        compiler_params=_cparams(("arbitrary",)),
    )(x, g.reshape(1, d), dh, dres)


def _loss_head(x, g, target):
    s, d = x.shape
    tr = min(512, s)

    def body(x_ref, g_ref, t_ref, loss_ref, dx_ref, dxb_ref, dg_ref):
        i = pl.program_id(0)
        xv = x_ref[...]
        rs = lax.rsqrt(jnp.mean(xv * xv, axis=-1, keepdims=True) + NORM_EPS)
        n = xv * rs
        gv = g_ref[...]
        err = n * gv - t_ref[...]
        row_loss = jnp.mean(err * err, axis=-1, keepdims=True)
        loss_part = 0.5 * jnp.sum(row_loss, axis=0, keepdims=True)
        dy = err * (1.0 / d)
        dn = dy * gv
        dx = rs * (dn - n * jnp.mean(dn * n, axis=-1, keepdims=True))
        dx_ref[...] = dx
        dxb_ref[...] = dx.astype(BF16)
        dg_part = jnp.sum(dy * n, axis=0, keepdims=True)

        @pl.when(i == 0)
        def _():
            loss_ref[...] = loss_part
            dg_ref[...] = dg_part

        @pl.when(i > 0)
        def _():
            loss_ref[...] += loss_part
            dg_ref[...] += dg_part

    row = pl.BlockSpec((tr, d), lambda i: (i, 0))
    vec = pl.BlockSpec((1, d), lambda i: (0, 0))
    one = pl.BlockSpec((1, 1), lambda i: (0, 0))
    return pl.pallas_call(
        body, name="loss_head",
        out_shape=(jax.ShapeDtypeStruct((1, 1), F32), jax.ShapeDtypeStruct((s, d), F32),
                   jax.ShapeDtypeStruct((s, d), BF16), jax.ShapeDtypeStruct((1, d), F32)),
        grid=(s // tr,),
        in_specs=[row, vec, row],
        out_specs=(one, row, row, vec),
        compiler_params=_cparams(("arbitrary",)),
    )(x, g.reshape(1, d), target)


def _conv_taps(u_prev, u, w_ref, rows):
    ext = jnp.concatenate([u_prev, u], axis=0)
    u1 = pltpu.roll(ext, 1, 0)[HALO:HALO + rows]
    u2 = pltpu.roll(ext, 2, 0)[HALO:HALO + rows]
    conv = w_ref[0:1, :] * u2 + w_ref[1:2, :] * u1 + w_ref[2:3, :] * u + w_ref[3:4, :]
    return conv, u1, u2


def _conv_fwd(proj, conv_wb, *, s):
    tr = min(512, s)
    c = CONV_CH
    hb = tr // HALO

    def body(gbi_ref, gci_ref, cx_ref, gci_h_ref, cx_h_ref, w_ref, o_ref):
        i = pl.program_id(0)
        u = gci_ref[...].astype(F32) * cx_ref[...].astype(F32)
        u_prev = gci_h_ref[...].astype(F32) * cx_h_ref[...].astype(F32)
        u_prev = jnp.where(i == 0, 0.0, u_prev)
        conv, _, _ = _conv_taps(u_prev, u, w_ref, tr)
        o_ref[...] = (gbi_ref[...].astype(F32) * conv).astype(BF16)

    def seg(col):
        return pl.BlockSpec((tr, c), lambda i: (i, col // c))

    def halo(col):
        return pl.BlockSpec((HALO, c), lambda i: (jnp.maximum(i * hb - 1, 0), col // c))

    return pl.pallas_call(
        body, name="conv_fwd",
        out_shape=jax.ShapeDtypeStruct((s, c), BF16),
        grid=(s // tr,),
        in_specs=[seg(COL_GBI), seg(COL_GCI), seg(COL_CX), halo(COL_GCI), halo(COL_CX),
                  pl.BlockSpec((8, c), lambda i: (0, 0))],
        out_specs=pl.BlockSpec((tr, c), lambda i: (i, 0)),
        compiler_params=_cparams(("parallel",)),
    )(proj, proj, proj, proj, proj, conv_wb)


def _conv_bwd(proj, dcb, conv_wb, dproj, *, s):
    tr = min(512, s)
    c = CONV_CH
    hb = tr // HALO
    nb = s // tr

    def body(gbi_ref, gci_ref, cx_ref, dcb_ref, gci_h_ref, cx_h_ref, gbi_n_ref, dcb_n_ref, w_ref,
             dproj_in_ref, dp_ref, dw_ref):
        i = pl.program_id(0)
        gbi = gbi_ref[...].astype(F32)
        gci = gci_ref[...].astype(F32)
        cx = cx_ref[...].astype(F32)
        dcb_v = dcb_ref[...].astype(F32)
        u = gci * cx
        u_prev = jnp.where(i == 0, 0.0, gci_h_ref[...].astype(F32) * cx_h_ref[...].astype(F32))
        conv, u1, u2 = _conv_taps(u_prev, u, w_ref, tr)
        dconv = dcb_v * gbi
        dconv_next = jnp.where(i == nb - 1, 0.0, dcb_n_ref[...].astype(F32) * gbi_n_ref[...].astype(F32))
        ext = jnp.concatenate([dconv, dconv_next], axis=0)
        n_ext = tr + HALO
        d1 = pltpu.roll(ext, n_ext - 1, 0)[0:tr]
        d2 = pltpu.roll(ext, n_ext - 2, 0)[0:tr]
        du = w_ref[2:3, :] * dconv + w_ref[1:2, :] * d1 + w_ref[0:1, :] * d2
        dp_ref[:, 0:c] = (dcb_v * conv).astype(BF16)
        dp_ref[:, c:2 * c] = (du * cx).astype(BF16)
        dp_ref[:, 2 * c:3 * c] = (du * gci).astype(BF16)
        part = jnp.concatenate([
            jnp.sum(dconv * u2, axis=0, keepdims=True),
            jnp.sum(dconv * u1, axis=0, keepdims=True),
            jnp.sum(dconv * u, axis=0, keepdims=True),
            jnp.sum(dconv, axis=0, keepdims=True),
            jnp.zeros((4, c), F32)], axis=0)

        @pl.when(i == 0)
        def _():
            dw_ref[...] = part

        @pl.when(i > 0)
        def _():
            dw_ref[...] += part

    def seg(col):
        return pl.BlockSpec((tr, c), lambda i: (i, col // c))

    def halo_prev(col):
        return pl.BlockSpec((HALO, c), lambda i: (jnp.maximum(i * hb - 1, 0), col // c))

    def halo_next(col):
        return pl.BlockSpec((HALO, c), lambda i: (jnp.minimum((i + 1) * hb, nb * hb - 1), col // c))

    return pl.pallas_call(
        body, name="conv_bwd",
        out_shape=(jax.ShapeDtypeStruct((s, PROJ_W), BF16), jax.ShapeDtypeStruct((8, c), F32)),
        grid=(nb,),
        in_specs=[seg(COL_GBI), seg(COL_GCI), seg(COL_CX), pl.BlockSpec((tr, c), lambda i: (i, 0)),
                  halo_prev(COL_GCI), halo_prev(COL_CX), halo_next(COL_GBI),
                  pl.BlockSpec((HALO, c), lambda i: (jnp.minimum((i + 1) * hb, nb * hb - 1), 0)),
                  pl.BlockSpec((8, c), lambda i: (0, 0)), ANY],
        out_specs=(pl.BlockSpec((tr, 3 * c), lambda i: (i, COL_GBI // (3 * c))),
                   pl.BlockSpec((8, c), lambda i: (0, 0))),
        input_output_aliases={9: 0},
        compiler_params=_cparams(("arbitrary",)),
    )(proj, proj, proj, dcb, proj, proj, proj, dcb, conv_wb, dproj)


def _mix_fwd(proj, ya, yb, *, s):
    tr = min(512, s)
    d = D_MODEL

    def body(ga_ref, gb_ref, ya_ref, yb_ref, o_ref):
        sa = _sigmoid(ga_ref[...].astype(F32))
        sb = _sigmoid(gb_ref[...].astype(F32))
        o_ref[...] = (sa * ya_ref[...].astype(F32) + sb * yb_ref[...].astype(F32)).astype(BF16)

    row = pl.BlockSpec((tr, d), lambda i: (i, 0))
    return pl.pallas_call(
        body, name="mix_fwd",
        out_shape=jax.ShapeDtypeStruct((s, d), BF16),
        grid=(s // tr,),
        in_specs=[pl.BlockSpec((tr, d), lambda i: (i, COL_GA // d)),
                  pl.BlockSpec((tr, d), lambda i: (i, COL_GB // d)), row, row],
        out_specs=row,
        compiler_params=_cparams(("parallel",)),
    )(proj, proj, ya, yb)


def _mix_bwd(proj, ya, yb, dmix, *, s):
    tr = min(512, s)
    d = D_MODEL

    def body(ga_ref, gb_ref, ya_ref, yb_ref, dm_ref, dya_ref, dyb_ref, dg_ref):
        sa = _sigmoid(ga_ref[...].astype(F32))
        sb = _sigmoid(gb_ref[...].astype(F32))
        dm = dm_ref[...].astype(F32)
        dya_ref[...] = (dm * sa).astype(BF16)
        dyb_ref[...] = (dm * sb).astype(BF16)
        dg_ref[:, 0:d] = (dm * ya_ref[...].astype(F32) * sa * (1.0 - sa)).astype(BF16)
        dg_ref[:, d:2 * d] = (dm * yb_ref[...].astype(F32) * sb * (1.0 - sb)).astype(BF16)

    row = pl.BlockSpec((tr, d), lambda i: (i, 0))
    return pl.pallas_call(
        body, name="mix_bwd",
        out_shape=(jax.ShapeDtypeStruct((s, d), BF16), jax.ShapeDtypeStruct((s, d), BF16),
                   jax.ShapeDtypeStruct((s, PROJ_W), BF16)),
        grid=(s // tr,),
        in_specs=[pl.BlockSpec((tr, d), lambda i: (i, COL_GA // d)),
                  pl.BlockSpec((tr, d), lambda i: (i, COL_GB // d)), row, row, row],
        out_specs=(row, row, pl.BlockSpec((tr, 2 * d), lambda i: (i, COL_GA // (2 * d)))),
        compiler_params=_cparams(("parallel",)),
    )(proj, proj, ya, yb, dmix)


def _swiglu_fwd(gt, up, *, s):
    tr = min(256, s)
    f = FFN_HIDDEN

    def body(gt_ref, up_ref, o_ref):
        gv = gt_ref[...].astype(F32)
        o_ref[...] = (gv * _sigmoid(gv) * up_ref[...].astype(F32)).astype(BF16)

    row = pl.BlockSpec((tr, f), lambda i: (i, 0))
    return pl.pallas_call(
        body, name="swiglu_fwd",
        out_shape=jax.ShapeDtypeStruct((s, f), BF16),
        grid=(s // tr,),
        in_specs=[row, row], out_specs=row,
        compiler_params=_cparams(("parallel",)),
    )(gt, up)


def _swiglu_bwd(gt, up, dhid, *, s):
    tr = min(256, s)
    f = FFN_HIDDEN

    def body(gt_ref, up_ref, dh_ref, dgt_ref, dup_ref):
        gv = gt_ref[...].astype(F32)
        uv = up_ref[...].astype(F32)
        dh = dh_ref[...].astype(F32)
        sg = _sigmoid(gv)
        dgt_ref[...] = (dh * uv * sg * (1.0 + gv * (1.0 - sg))).astype(BF16)
        dup_ref[...] = (dh * gv * sg).astype(BF16)

    row = pl.BlockSpec((tr, f), lambda i: (i, 0))
    return pl.pallas_call(
        body, name="swiglu_bwd",
        out_shape=(jax.ShapeDtypeStruct((s, f), BF16), jax.ShapeDtypeStruct((s, f), BF16)),
        grid=(s // tr,),
        in_specs=[row, row, row], out_specs=(row, row),
        compiler_params=_cparams(("parallel",)),
    )(gt, up, dhid)


def _tri(strict):
    r = lax.broadcasted_iota(jnp.int32, (CHUNK, CHUNK), 0)
    c = lax.broadcasted_iota(jnp.int32, (CHUNK, CHUNK), 1)
    return jnp.where((c < r) if strict else (c <= r), 1.0, 0.0).astype(F32)


def _gla_gate_terms(la_c, tri):
    cum = jnp.dot(tri, la_c, precision=lax.Precision.HIGHEST, preferred_element_type=F32)
    cend = cum[CHUNK - 1:CHUNK, :]
    return jnp.exp(cend - cum), jnp.exp(cend)


def _hk(h):
    return slice(h * HEAD_K, (h + 1) * HEAD_K)


def _hv(h):
    return slice(h * HEAD_V, (h + 1) * HEAD_V)


def _gla_fwd(proj, wfg2p, bfg, gn, *, s):
    tr = min(512, s)
    nb = s // tr
    nc = tr // CHUNK

    def body(q_ref, k_ref, v_ref, r_ref, fz_ref, w_ref, b_ref, gn_ref, oa_ref, st_ref,
             state, la_scr, o_scr):
        i = pl.program_id(0)

        @pl.when(i == 0)
        def _():
            state[...] = jnp.zeros_like(state)

        fg = jnp.dot(fz_ref[...], w_ref[...], preferred_element_type=F32) + b_ref[...]
        la_scr[...] = _log_sigmoid(fg) * (1.0 / GATE_TAU)
        tri = _tri(False)

        def chunk(ci, carry):
            r0 = pl.multiple_of(ci * CHUNK, CHUNK)
            rows = pl.ds(r0, CHUNK)
            e, gam = _gla_gate_terms(la_scr[rows, :], tri)
            kd = (k_ref[rows, :].astype(F32) * e).astype(BF16)
            qs = (q_ref[rows, :].astype(F32) * Q_SCALE).astype(BF16)
            v_c = v_ref[rows, :]
            for h in range(GLA_HEADS):
                upd = lax.dot_general(v_c[:, _hv(h)], kd[:, _hk(h)], _DOT_DIMS["tn"],
                                      preferred_element_type=F32)
                st_h = state[:, _hk(h)] * gam[:, _hk(h)] + upd
                state[:, _hk(h)] = st_h
                o_scr[rows, _hv(h)] = lax.dot_general(qs[:, _hk(h)], st_h.astype(BF16), _DOT_DIMS["nt"],
                                                       preferred_element_type=F32)
            st_ref[ci] = state[...]
            return carry

        lax.fori_loop(0, nc, chunk, 0, unroll=True)
        for h in range(GLA_HEADS):
            o = o_scr[:, _hv(h)]
            rs = lax.rsqrt(jnp.mean(o * o, axis=-1, keepdims=True) + NORM_EPS)
            rv = r_ref[:, _hv(h)].astype(F32)
            oa_ref[:, _hv(h)] = ((o * rs * gn_ref[...]).astype(F32) * (rv * _sigmoid(rv))).astype(BF16)

    return pl.pallas_call(
        body, name="gla_fwd",
        out_shape=(jax.ShapeDtypeStruct((s, VAL_WIDTH), BF16),
                   jax.ShapeDtypeStruct((s // CHUNK, HEAD_V, KEY_WIDTH), F32)),
        grid=(nb,),
        in_specs=[pl.BlockSpec((tr, KEY_WIDTH), lambda i: (i, COL_Q // KEY_WIDTH)),
                  pl.BlockSpec((tr, KEY_WIDTH), lambda i: (i, COL_K // KEY_WIDTH)),
                  pl.BlockSpec((tr, VAL_WIDTH), lambda i: (i, COL_V // VAL_WIDTH)),
                  pl.BlockSpec((tr, VAL_WIDTH), lambda i: (i, COL_R // VAL_WIDTH)),
                  pl.BlockSpec((tr, FZ_PAD), lambda i: (i, COL_FZ // FZ_PAD)),
                  pl.BlockSpec((FZ_PAD, KEY_WIDTH), lambda i: (0, 0)),
                  pl.BlockSpec((1, KEY_WIDTH), lambda i: (0, 0)),
                  pl.BlockSpec((1, HEAD_V), lambda i: (0, 0))],
        out_specs=(pl.BlockSpec((tr, VAL_WIDTH), lambda i: (i, 0)),
                   pl.BlockSpec((nc, HEAD_V, KEY_WIDTH), lambda i: (i, 0, 0))),
        scratch_shapes=[pltpu.VMEM((HEAD_V, KEY_WIDTH), F32), pltpu.VMEM((tr, KEY_WIDTH), F32),
                        pltpu.VMEM((tr, VAL_WIDTH), F32)],
        compiler_params=_cparams(("arbitrary",)),
    )(proj, proj, proj, proj, proj, wfg2p, bfg, gn)


def _gla_bwd(proj, doa, states, wfg2p, bfg, gn, dproj, *, s):
    tr = min(512, s)
    nb = s // tr
    nc = tr // CHUNK

    def body(q_ref, k_ref, v_ref, r_ref, fz_ref, doa_ref, st_ref, stp_ref, w_ref, b_ref, gn_ref, dproj_in_ref,
             dp_ref, dfz_ref, dgn_ref, dw_ref, db_ref,
             carry, fg_scr, la_scr, dla_scr, o_scr, do_scr):
        i = pl.program_id(0)

        @pl.when(i == 0)
        def _():
            carry[...] = jnp.zeros_like(carry)
            dgn_ref[...] = jnp.zeros_like(dgn_ref)
            dw_ref[...] = jnp.zeros_like(dw_ref)
            db_ref[...] = jnp.zeros_like(db_ref)

        fz = fz_ref[...]
        fg = jnp.dot(fz, w_ref[...], preferred_element_type=F32) + b_ref[...]
        fg_scr[...] = fg
        la_scr[...] = _log_sigmoid(fg) * (1.0 / GATE_TAU)

        def fwd_chunk(ci, c0):
            r0 = pl.multiple_of(ci * CHUNK, CHUNK)
            rows = pl.ds(r0, CHUNK)
            qs = (q_ref[rows, :].astype(F32) * Q_SCALE).astype(BF16)
            st = st_ref[ci].astype(BF16)
            for h in range(GLA_HEADS):
                o_scr[rows, _hv(h)] = lax.dot_general(qs[:, _hk(h)], st[:, _hk(h)], _DOT_DIMS["nt"],
                                                       preferred_element_type=F32)
            return c0

        lax.fori_loop(0, nc, fwd_chunk, 0, unroll=True)

        gnv = gn_ref[...]
        dgn_part = jnp.zeros((1, HEAD_V), F32)
        for h in range(GLA_HEADS):
            o = o_scr[:, _hv(h)]
            rs = lax.rsqrt(jnp.mean(o * o, axis=-1, keepdims=True) + NORM_EPS)
            nrm = o * rs
            rv = r_ref[:, _hv(h)].astype(F32)
            sg = _sigmoid(rv)
            doa_h = doa_ref[:, _hv(h)].astype(F32)
            don = doa_h * (rv * sg)
            dp_ref[:, COL_R + h * HEAD_V:COL_R + (h + 1) * HEAD_V] = (
                doa_h * (nrm * gnv) * (sg * (1.0 + rv * (1.0 - sg)))).astype(BF16)
            dgn_part = dgn_part + jnp.sum(don * nrm, axis=0, keepdims=True)
            dn = don * gnv
            do_scr[:, _hv(h)] = (rs * (dn - nrm * jnp.mean(dn * nrm, axis=-1, keepdims=True))).astype(BF16)
        dgn_ref[...] += dgn_part

        tri = _tri(False)
        tri_s = _tri(True)
        first_block = i == nb - 1

        def bwd_chunk(cc, c0):
            ci = nc - 1 - cc
            r0 = pl.multiple_of(ci * CHUNK, CHUNK)
            rows = pl.ds(r0, CHUNK)
            e, gam = _gla_gate_terms(la_scr[rows, :], tri)
            k_c = k_ref[rows, :].astype(F32)
            kd = k_c * e
            kd_b = kd.astype(BF16)
            qs = (q_ref[rows, :].astype(F32) * Q_SCALE).astype(BF16)
            v_c = v_ref[rows, :]
            do_c = do_scr[rows, :]
            st = st_ref[ci]
            st_b = st.astype(BF16)
            st_prev_in = st_ref[jnp.maximum(ci - 1, 0)]
            st_prev_edge = jnp.where(first_block, 0.0, stp_ref[0])
            st_prev = jnp.where(ci > 0, st_prev_in, st_prev_edge)
            dkd_parts = []
            dgam_parts = []
            for h in range(GLA_HEADS):
                dst = lax.dot_general(do_c[:, _hv(h)], qs[:, _hk(h)], _DOT_DIMS["tn"],
                                      preferred_element_type=F32) + carry[:, _hk(h)]
                dst_b = dst.astype(BF16)
                dqs = jnp.dot(do_c[:, _hv(h)], st_b[:, _hk(h)], preferred_element_type=F32)
                dp_ref[rows, COL_Q + h * HEAD_K:COL_Q + (h + 1) * HEAD_K] = (dqs * Q_SCALE).astype(BF16)
                dkd_parts.append(jnp.dot(v_c[:, _hv(h)], dst_b, preferred_element_type=F32))
                dp_ref[rows, COL_V + h * HEAD_V:COL_V + (h + 1) * HEAD_V] = lax.dot_general(
                    kd_b[:, _hk(h)], dst_b, _DOT_DIMS["nt"], preferred_element_type=F32).astype(BF16)
                dgam_parts.append(jnp.sum(dst * st_prev[:, _hk(h)], axis=0, keepdims=True))
                carry[:, _hk(h)] = dst * gam[:, _hk(h)]
            dkd = jnp.concatenate(dkd_parts, axis=1)
            dgam = jnp.concatenate(dgam_parts, axis=1)
            dp_ref[rows, COL_K:COL_K + KEY_WIDTH] = (dkd * e).astype(BF16)
            dz = dkd * kd
            dla_scr[rows, :] = gam * dgam + jnp.dot(tri_s, dz, precision=lax.Precision.HIGHEST,
                                                    preferred_element_type=F32)
            return c0

        lax.fori_loop(0, nc, bwd_chunk, 0, unroll=True)

        dfg = dla_scr[...] * (1.0 / GATE_TAU) * _sigmoid(-fg_scr[...])
        dfg_b = dfg.astype(BF16)
        dfz_ref[...] = lax.dot_general(dfg_b, w_ref[...], _DOT_DIMS["nt"],
                                       preferred_element_type=F32).astype(BF16)
        dw_ref[...] += lax.dot_general(fz, dfg_b, _DOT_DIMS["tn"], preferred_element_type=F32)
        db_ref[...] += jnp.sum(dfg, axis=0, keepdims=True)

    def rev(i):
        return nb - 1 - i

    return pl.pallas_call(
        body, name="gla_bwd",
        out_shape=(jax.ShapeDtypeStruct((s, PROJ_W), BF16),
                   jax.ShapeDtypeStruct((s, FZ_PAD), BF16),
                   jax.ShapeDtypeStruct((1, HEAD_V), F32),
                   jax.ShapeDtypeStruct((FZ_PAD, KEY_WIDTH), F32),
                   jax.ShapeDtypeStruct((1, KEY_WIDTH), F32)),
        grid=(nb,),
        in_specs=[pl.BlockSpec((tr, KEY_WIDTH), lambda i: (rev(i), COL_Q // KEY_WIDTH)),
                  pl.BlockSpec((tr, KEY_WIDTH), lambda i: (rev(i), COL_K // KEY_WIDTH)),
                  pl.BlockSpec((tr, VAL_WIDTH), lambda i: (rev(i), COL_V // VAL_WIDTH)),
                  pl.BlockSpec((tr, VAL_WIDTH), lambda i: (rev(i), COL_R // VAL_WIDTH)),
                  pl.BlockSpec((tr, FZ_PAD), lambda i: (rev(i), COL_FZ // FZ_PAD)),
                  pl.BlockSpec((tr, VAL_WIDTH), lambda i: (rev(i), 0)),
                  pl.BlockSpec((nc, HEAD_V, KEY_WIDTH), lambda i: (rev(i), 0, 0)),
                  pl.BlockSpec((1, HEAD_V, KEY_WIDTH), lambda i: (jnp.maximum(rev(i) * nc - 1, 0), 0, 0)),
                  pl.BlockSpec((FZ_PAD, KEY_WIDTH), lambda i: (0, 0)),
                  pl.BlockSpec((1, KEY_WIDTH), lambda i: (0, 0)),
                  pl.BlockSpec((1, HEAD_V), lambda i: (0, 0)), ANY],
        out_specs=(pl.BlockSpec((tr, 3 * VAL_WIDTH), lambda i: (rev(i), 0)),
                   pl.BlockSpec((tr, FZ_PAD), lambda i: (rev(i), 0)),
                   pl.BlockSpec((1, HEAD_V), lambda i: (0, 0)),
                   pl.BlockSpec((FZ_PAD, KEY_WIDTH), lambda i: (0, 0)),
                   pl.BlockSpec((1, KEY_WIDTH), lambda i: (0, 0))),
        scratch_shapes=[pltpu.VMEM((HEAD_V, KEY_WIDTH), F32),
                        pltpu.VMEM((tr, KEY_WIDTH), F32),
                        pltpu.VMEM((tr, KEY_WIDTH), F32),
                        pltpu.VMEM((tr, KEY_WIDTH), F32),
                        pltpu.VMEM((tr, VAL_WIDTH), F32),
                        pltpu.VMEM((tr, VAL_WIDTH), BF16)],
        input_output_aliases={11: 0},
        compiler_params=_cparams(("arbitrary",)),
    )(proj, proj, proj, proj, proj, doa, states, states, wfg2p, bfg, gn, dproj)


def _put_fz(dproj, dfz, *, s):
    tr = min(512, s)

    def body(dfz_ref, dproj_in_ref, o_ref):
        o_ref[...] = dfz_ref[...]

    return pl.pallas_call(
        body, name="put_dfz",
        out_shape=jax.ShapeDtypeStruct((s, PROJ_W), BF16),
        grid=(s // tr,),
        in_specs=[pl.BlockSpec((tr, FZ_PAD), lambda i: (i, 0)), ANY],
        out_specs=pl.BlockSpec((tr, FZ_PAD), lambda i: (i, COL_FZ // FZ_PAD)),
        input_output_aliases={1: 0},
        compiler_params=_cparams(("parallel",)),
    )(dfz, dproj)


_ADAM_C1 = 1.0 / (1.0 - ADAM_B1 ** ADAM_STEP)
_ADAM_C2 = 1.0 / (1.0 - ADAM_B2 ** ADAM_STEP)


def _adamw_math(wv, gv, mv, vv):
    nm = ADAM_B1 * mv + (1.0 - ADAM_B1) * gv
    nv = ADAM_B2 * vv + (1.0 - ADAM_B2) * (gv * gv)
    delta = -ADAM_LR * ((nm * _ADAM_C1) / (jnp.sqrt(nv * _ADAM_C2) + ADAM_EPS) + ADAM_WD * wv)
    return delta, nm, nv


def _adamw(w, g, m, v, *, name):
    shape = w.shape
    if w.ndim == 1:
        w, g, m, v = (t.reshape(1, 1, -1) for t in (w, g, m, v))
    elif w.ndim == 2:
        w, g, m, v = (t.reshape((1,) + t.shape) for t in (w, g, m, v))
    l, a, b = w.shape

    def body(w_ref, g_ref, m_ref, v_ref, d_ref, nm_ref, nv_ref):
        d_ref[...], nm_ref[...], nv_ref[...] = _adamw_math(w_ref[...], g_ref[...], m_ref[...], v_ref[...])

    blk = pl.BlockSpec((1, a, b), lambda li: (li, 0, 0))
    outs = pl.pallas_call(
        body, name=name,
        out_shape=tuple(jax.ShapeDtypeStruct((l, a, b), F32) for _ in range(3)),
        grid=(l,),
        in_specs=[blk, blk, blk, blk],
        out_specs=(blk, blk, blk),
        compiler_params=_cparams(("parallel",)),
    )(w, g, m, v)
    return tuple(o.reshape(shape) for o in outs)


def _adamw_layers(w, reduced, received, on_core0, m, v, *, name):
    l, a, b = w.shape
    assert l == DEPTH == 2
    ta = _row_tile(a, F32_SUBLANES, 384)

    def body(flag_ref, w_ref, r0_ref, o0_ref, r1_ref, o1_ref, m_ref, v_ref, g_ref, d_ref, nm_ref, nv_ref):
        core0 = flag_ref[...] > 0.5
        g0 = jnp.where(core0, r0_ref[...], o0_ref[...])
        g1 = jnp.where(core0, o1_ref[...], r1_ref[...])
        gv = jnp.where(pl.program_id(0) == 0, g0, g1)
        g_ref[0] = gv
        d_ref[0], nm_ref[0], nv_ref[0] = _adamw_math(w_ref[0], gv, m_ref[0], v_ref[0])

    blk = pl.BlockSpec((1, ta, b), lambda li, ai: (li, ai, 0))
    gblk = pl.BlockSpec((ta, b), lambda li, ai: (ai, 0))
    return pl.pallas_call(
        body, name=name,
        out_shape=tuple(jax.ShapeDtypeStruct((l, a, b), F32) for _ in range(4)),
        grid=(l, a // ta),
        in_specs=[pl.BlockSpec((1, 1), lambda li, ai: (0, 0)), blk, gblk, gblk, gblk, gblk, blk, blk],
        out_specs=(blk, blk, blk, blk),
        compiler_params=_cparams(("parallel", "parallel")),
    )(on_core0, w, reduced[0], received[0], reduced[1], received[1], m, v)


MESH_ID = pl.DeviceIdType.MESH
ANY = pl.BlockSpec(memory_space=pl.ANY)


def _position():
    x, y, c = lax.axis_index("x"), lax.axis_index("y"), lax.axis_index("c")
    chips = [(1 - x, y), (x, 1 - y), (1 - x, 1 - y)]
    return x, y, c, chips


def _chip_index(xy):
    return 2 * xy[0] + xy[1]


def _all_gather_weights(shards):
    n = len(shards)

    def body(*refs):
        ins, outs = refs[:n], refs[n:2 * n]
        send_sems, recv_sems, pass_send, pass_recv = refs[2 * n:]
        x, y, c, chips = _position()
        sibling = (x, y, 1 - c)

        def ici(a, k, chip_from):
            return pltpu.make_async_remote_copy(
                src_ref=ins[a].at[c], dst_ref=outs[a].at[c, _chip_index(chip_from)],
                send_sem=send_sems.at[a, k], recv_sem=recv_sems.at[a, k],
                device_id=(chips[k][0], chips[k][1], c), device_id_type=MESH_ID)

        def handoff(a, k, layer):
            slot = outs[a].at[layer, _chip_index(chips[k])]
            return pltpu.make_async_remote_copy(
                src_ref=slot, dst_ref=slot, send_sem=pass_send.at[a, k], recv_sem=pass_recv.at[a, k],
                device_id=sibling, device_id_type=MESH_ID)

        for a in range(n):
            for k in range(3):
                ici(a, k, (x, y)).start()
        for k in range(3):
            for a in range(n):
                ici(a, k, chips[k]).wait_recv()
                handoff(a, k, c).start()
        for k in range(3):
            for a in range(n):
                handoff(a, k, 1 - c).wait_recv()
        for a in range(n):
            for k in range(3):
                ici(a, k, (x, y)).wait_send()
                handoff(a, k, c).wait_send()

    gathered = pl.pallas_call(
        body, name="all_gather_weights",
        out_shape=tuple(jax.ShapeDtypeStruct((DEPTH, N_CHIPS) + t.shape[1:], t.dtype) for t in shards),
        in_specs=[ANY] * n, out_specs=tuple([ANY] * n),
        scratch_shapes=[pltpu.SemaphoreType.DMA((n, 3)), pltpu.SemaphoreType.DMA((n, 3)),
                        pltpu.SemaphoreType.DMA((n, 3)), pltpu.SemaphoreType.DMA((n, 3))],
    )(*shards)
    me = _chip_index((lax.axis_index("x"), lax.axis_index("y")))
    return [lax.dynamic_update_index_in_dim(g, t[:, None], me, axis=1) for g, t in zip(gathered, shards)]


HBM = pl.BlockSpec(memory_space=pltpu.HBM)
SEM = pl.BlockSpec(memory_space=pltpu.SEMAPHORE)
DATAFLOW_EFFECT = pltpu.SideEffectType.DATAFLOW_SIDE_EFFECTING
TOKEN_SHAPE = (8, LANE)


def _landing(shape, dtype):
    return pltpu.with_memory_space_constraint(lax.empty(shape, dtype), pltpu.HBM)


def _split_start(bufs, sem_shape, issue, *, name, after=None):
    n = len(bufs)
    n_in = n + (after is not None)

    def body(*refs):
        issue(refs[:n], refs[n_in], refs[n_in + 1])
        token = refs[-1]
        token[...] = jnp.zeros_like(token)

    operands = [pltpu.with_memory_space_constraint(t, pltpu.HBM) for t in bufs]
    outs = pl.pallas_call(
        body, name=name,
        out_shape=(pltpu.SemaphoreType.DMA(sem_shape), pltpu.SemaphoreType.DMA(sem_shape),
                   *[pltpu.HBM(t.shape, t.dtype) for t in bufs], jax.ShapeDtypeStruct(TOKEN_SHAPE, F32)),
        in_specs=[HBM] * n + [ANY] * (after is not None),
        out_specs=(SEM, SEM, *[HBM] * n, pl.BlockSpec(memory_space=pltpu.VMEM)),
        input_output_aliases={i: 2 + i for i in range(n)},
        compiler_params=pltpu.CompilerParams(has_side_effects=DATAFLOW_EFFECT),
    )(*operands, *([after] if after is not None else []))
    return outs[0], outs[1], list(outs[2:2 + n]), outs[-1]


def _split_wait(started, after, settle, *, name):
    send_sems, recv_sems, bufs, _ = started
    n = len(bufs)

    def body(*refs):
        settle(refs[:n], refs[n], refs[n + 1])

    outs = pl.pallas_call(
        body, name=name,
        out_shape=tuple(pltpu.HBM(t.shape, t.dtype) for t in bufs),
        in_specs=[HBM] * n + [SEM, SEM, ANY],
        out_specs=tuple([HBM] * n),
        input_output_aliases={i: i for i in range(n)},
        compiler_params=pltpu.CompilerParams(has_side_effects=DATAFLOW_EFFECT),
    )(*bufs, send_sems, recv_sems, after)
    return list(outs)


def _to_sibling(bufs, n, send_sems, recv_sems):
    x, y, c, _ = _position()
    return [pltpu.make_async_remote_copy(
        src_ref=bufs[a], dst_ref=bufs[n + a], send_sem=send_sems.at[a], recv_sem=recv_sems.at[a],
        device_id=(x, y, 1 - c), device_id_type=MESH_ID) for a in range(n)]


def _sibling_push_start(layer, sender_is_reducer, arrays, *, name):
    n = len(arrays)
    sender = layer if sender_is_reducer else 1 - layer

    def issue(bufs, send_sems, recv_sems):
        @pl.when(lax.axis_index("c") == sender)
        def _():
            for cp in _to_sibling(bufs, n, send_sems, recv_sems):
                cp.start()

    lands = [_landing(t.shape, t.dtype) for t in arrays]
    return _split_start(list(arrays) + lands, (n,), issue, name=name)


def _sibling_push_wait(layer, sender_is_reducer, started, after, *, name):
    n = len(started[2]) // 2
    sender = layer if sender_is_reducer else 1 - layer

    def settle(bufs, send_sems, recv_sems):
        c = lax.axis_index("c")

        @pl.when(c == sender)
        def _():
            for cp in _to_sibling(bufs, n, send_sems, recv_sems):
                cp.wait_send()

        @pl.when(c != sender)
        def _():
            for cp in _to_sibling(bufs, n, send_sems, recv_sems):
                cp.wait_recv()

    outs = _split_wait(started, after, settle, name=name)
    return outs[:n], outs[n:]


def _chip_copies(bufs, n, send_sems, recv_sems):
    x, y, c, chips = _position()
    return [pltpu.make_async_remote_copy(
        src_ref=bufs[a].at[_chip_index(chips[k])], dst_ref=bufs[n + a].at[k],
        send_sem=send_sems.at[3 * a + k], recv_sem=recv_sems.at[3 * a + k],
        device_id=(chips[k][0], chips[k][1], c), device_id_type=MESH_ID)
        for a in range(n) for k in range(3)]


def _chip_exchange_start(layer, parts, *, name):
    n = len(parts)

    def issue(bufs, send_sems, recv_sems):
        @pl.when(lax.axis_index("c") == layer)
        def _():
            for cp in _chip_copies(bufs, n, send_sems, recv_sems):
                cp.start()

    lands = [_landing((3,) + t.shape[1:], t.dtype) for t in parts]
    return _split_start(list(parts) + lands, (3 * n,), issue, name=name)


def _chip_exchange_wait(layer, started, after, *, name):
    n = len(started[2]) // 2

    def settle(bufs, send_sems, recv_sems):
        @pl.when(lax.axis_index("c") == layer)
        def _():
            for cp in _chip_copies(bufs, n, send_sems, recv_sems):
                cp.wait()

    outs = _split_wait(started, after, settle, name=name)
    return outs[:n], outs[n:]


def _gather_copies(layer, bufs, n, send_sems, recv_sems, arriving):
    x, y, c, chips = _position()
    me = _chip_index((x, y))
    return [pltpu.make_async_remote_copy(
        src_ref=bufs[a], dst_ref=bufs[n + a].at[_chip_index(chips[k]) if arriving else me],
        send_sem=send_sems.at[3 * a + k], recv_sem=recv_sems.at[3 * a + k],
        device_id=(chips[k][0], chips[k][1], c), device_id_type=MESH_ID)
        for a in range(n) for k in range(3)]


def _gather_start(layer, shards, *, name, after=None):
    n = len(shards)

    def issue(bufs, send_sems, recv_sems):
        @pl.when(lax.axis_index("c") == layer)
        def _():
            for cp in _gather_copies(layer, bufs, n, send_sems, recv_sems, False):
                cp.start()

    lands = [_landing((N_CHIPS,) + t.shape, t.dtype) for t in shards]
    return _split_start(list(shards) + lands, (3 * n,), issue, name=name, after=after)


def _gather_wait(layer, started, after, *, name):
    n = len(started[2]) // 2

    def settle(bufs, send_sems, recv_sems):
        @pl.when(lax.axis_index("c") == layer)
        def _():
            for cp in _gather_copies(layer, bufs, n, send_sems, recv_sems, False):
                cp.wait_send()
            for cp in _gather_copies(layer, bufs, n, send_sems, recv_sems, True):
                cp.wait_recv()

    return _split_wait(started, after, settle, name=name)[n:]


def _handoff_copies(bufs, n, send_sems, recv_sems):
    x, y, c, chips = _position()
    out = []
    for a in range(n):
        for k in range(3):
            slot = bufs[a].at[_chip_index(chips[k])]
            out.append(pltpu.make_async_remote_copy(
                src_ref=slot, dst_ref=slot, send_sem=send_sems.at[3 * a + k], recv_sem=recv_sems.at[3 * a + k],
                device_id=(x, y, 1 - c), device_id_type=MESH_ID))
    return out


def _handoff_start(layer, gathered, *, name):
    n = len(gathered)

    def issue(bufs, send_sems, recv_sems):
        @pl.when(lax.axis_index("c") == layer)
        def _():
            for cp in _handoff_copies(bufs, n, send_sems, recv_sems):
                cp.start()

    return _split_start(list(gathered), (3 * n,), issue, name=name)


def _handoff_wait(layer, started, after, *, name):
    n = len(started[2])

    def settle(bufs, send_sems, recv_sems):
        c = lax.axis_index("c")

        @pl.when(c == layer)
        def _():
            for cp in _handoff_copies(bufs, n, send_sems, recv_sems):
                cp.wait_send()

        @pl.when(c != layer)
        def _():
            for cp in _handoff_copies(bufs, n, send_sems, recv_sems):
                cp.wait_recv()

    return _split_wait(started, after, settle, name=name)


F32_SUBLANES = 8
BF16_SUBLANES = 16


def _row_tile(a, sublanes=BF16_SUBLANES, max_rows=704):
    best = None
    for cand in range(sublanes, min(a, max_rows) + 1, sublanes):
        if a % cand == 0:
            best = cand
    assert best is not None, a
    return best


def _pair_sum(mine, sib, *, name):
    nchip, a, b = sib.shape
    ta = _row_tile(a)

    def body(m_ref, s_ref, o_ref):
        o_ref[...] = (m_ref[...].astype(F32) + s_ref[...].astype(F32)).astype(BF16)

    blk = pl.BlockSpec((1, ta, b), lambda j, r: (j, r, 0))
    return pl.pallas_call(
        body, name=name,
        out_shape=jax.ShapeDtypeStruct((nchip, a, b), BF16),
        grid=(nchip, a // ta),
        in_specs=[blk, blk], out_specs=blk,
        compiler_params=_cparams(("parallel", "parallel")),
    )(mine, sib)


def _total_sum(own, recv, *, name):
    a, b = own.shape
    ta = _row_tile(a)

    def body(o_ref, r_ref, t_ref):
        acc = o_ref[...].astype(F32)
        for k in range(3):
            acc = acc + r_ref[k].astype(F32)
        t_ref[...] = acc

    blk = pl.BlockSpec((ta, b), lambda r: (r, 0))
    return pl.pallas_call(
        body, name=name,
        out_shape=jax.ShapeDtypeStruct((a, b), F32),
        grid=(a // ta,),
        in_specs=[blk, pl.BlockSpec((3, ta, b), lambda r: (0, r, 0))], out_specs=blk,
        compiler_params=_cparams(("parallel",)),
    )(own, recv)


def _all_reduce_small(packed):
    rows, width = packed.shape

    def body(x_ref, out_ref, gath, send_sems, recv_sems, local_sem):
        x, y, c, chips = _position()
        me, sibling = (x, y, c), (x, y, 1 - c)

        def slot(px, py, pc):
            return gath.at[4 * px + 2 * py + pc]

        def copy(k, block, to, src=None):
            return pltpu.make_async_remote_copy(
                src_ref=slot(*block) if src is None else src, dst_ref=slot(*block),
                send_sem=send_sems.at[k], recv_sem=recv_sems.at[k], device_id=to, device_id_type=MESH_ID)

        mine = pltpu.make_async_copy(x_ref, slot(*me), local_sem)
        mine.start()
        first = [copy(0, me, sibling, src=x_ref)]
        first += [copy(1 + j, me, (*chip, c), src=x_ref) for j, chip in enumerate(chips)]
        for cp in first:
            cp.start()
        passed = [copy(4 + j, (*chip, c), sibling) for j, chip in enumerate(chips)]
        for j, chip in enumerate(chips):
            copy(1 + j, (*chip, c), me).wait_recv()
            passed[j].start()
        copy(0, sibling, me).wait_recv()
        for j, chip in enumerate(chips):
            copy(4 + j, (*chip, 1 - c), me).wait_recv()
        for cp in first + passed:
            cp.wait_send()
        mine.wait()
        acc = gath[0]
        for d in range(1, N_DEV):
            acc = acc + gath[d]
        out_ref[...] = acc

    return pl.pallas_call(
        body, name="all_reduce_small",
        out_shape=jax.ShapeDtypeStruct((rows, width), F32),
        in_specs=[pl.BlockSpec(memory_space=pltpu.VMEM)],
        out_specs=pl.BlockSpec(memory_space=pltpu.VMEM),
        scratch_shapes=[pltpu.VMEM((N_DEV, rows, width), F32), pltpu.SemaphoreType.DMA((7,)),
                        pltpu.SemaphoreType.DMA((7,)), pltpu.SemaphoreType.DMA],
    )(packed)


def _mixer_forward_branches(x0, w, s, dep=None):
    h = _rms_fwd(x0, w["norm1_g"], name="rms1_fwd")
    proj = _matmul(h, w["w_in_t"], mode="nt", out_dtype=BF16, tm=1024, tn=1664, tk=1024, name="proj_fwd",
                   dep=dep)
    oa, states = _gla_fwd(proj, w["w_fg2"], w["b_fg"], w["gla_norm_g"], s=s)
    cb_in = _conv_fwd(proj, w["conv_wb"], s=s)
    return dict(x0=x0, h=h, proj=proj, oa=oa, states=states, cb_in=cb_in)


def _mixer_forward_out(sv, w, s, dep=None):
    ya = _matmul(sv["oa"], w["w_oa"], mode="nn", out_dtype=BF16, tm=1024, tn=1024, tk=1024, name="ya_fwd",
                 dep=dep)
    yb = _matmul(sv["cb_in"], w["w_ob"], mode="nn", out_dtype=BF16, tm=1024, tn=1024, tk=1024, name="yb_fwd")
    mix = _mix_fwd(sv["proj"], ya, yb, s=s)
    x1 = _matmul(mix, w["w_o"], mode="nn", out_dtype=F32, tm=1024, tn=1024, tk=1024, name="wo_fwd",
                 add=sv["x0"])
    return x1, dict(sv, ya=ya, yb=yb, mix=mix)


def _ffn_forward_hidden(x1, w, s, dep=None):
    h2 = _rms_fwd(x1, w["norm2_g"], name="rms2_fwd")
    gt = _matmul(h2, w["w_gate_t"], mode="nt", out_dtype=BF16, tm=1024, tn=1408, tk=1024, name="ffn_gate_fwd",
                 dep=dep)
    up = _matmul(h2, w["w_up_t"], mode="nt", out_dtype=BF16, tm=1024, tn=1408, tk=1024, name="ffn_up_fwd")
    return dict(x1=x1, h2=h2, gt=gt, up=up, hid=_swiglu_fwd(gt, up, s=s))


def _ffn_forward_out(sv, w, dep=None):
    return _matmul(sv["hid"], w["w_ffn_down"], mode="nn", out_dtype=F32, tm=1024, tn=1024, tk=FFN_HIDDEN,
                   name="ffn_out_fwd", add=sv["x1"], dep=dep)


def _ffn_backward_grads(dx2b, w, sv, s):
    g = {}
    dhid = _matmul(dx2b, w["w_ffn_down"], mode="nt", out_dtype=BF16, tm=1024, tn=FFN_HIDDEN, tk=1024,
                   name="ffn_out_bwd")
    g["w_ffn_down"] = _matmul(sv["hid"], dx2b, mode="tn", out_dtype=BF16, tm=1408, tn=1024, tk=1024,
                              name="ffn_out_wgrad")
    dgt, dup = _swiglu_bwd(sv["gt"], sv["up"], dhid, s=s)
    g["w_gate_t"] = _matmul(dgt, sv["h2"], mode="tn", out_dtype=BF16, tm=1408, tn=1024, tk=1024,
                            name="ffn_gate_wgrad")
    g["w_up_t"] = _matmul(dup, sv["h2"], mode="tn", out_dtype=BF16, tm=1408, tn=1024, tk=1024,
                          name="ffn_up_wgrad")
    return g, dgt, dup, dhid


def _ffn_backward_input(dgt, dup, dx2, w, sv, dep=None):
    dh2 = _matmul(dgt, w["w_gate_t"], mode="nn", out_dtype=F32, tm=1024, tn=1024, tk=FFN_HIDDEN,
                  name="ffn_gate_bwd", dep=dep)
    dh2 = _matmul(dup, w["w_up_t"], mode="nn", out_dtype=F32, tm=1024, tn=1024, tk=FFN_HIDDEN,
                  name="ffn_up_bwd", add=dh2)
    return _rms_bwd(sv["x1"], w["norm2_g"], dh2, dx2, name="rms2_bwd")


def _mixer_backward_branches(dx1b, w, sv, s, dep=None):
    g = {}
    dmix = _matmul(dx1b, w["w_o"], mode="nt", out_dtype=BF16, tm=1024, tn=1024, tk=1024, name="wo_bwd",
                   dep=dep)
    g["w_o"] = _matmul(sv["mix"], dx1b, mode="tn", out_dtype=BF16, tm=1024, tn=1024, tk=2048, name="wo_wgrad")
    dya, dyb, dproj = _mix_bwd(sv["proj"], sv["ya"], sv["yb"], dmix, s=s)
    dcb = _matmul(dyb, w["w_ob"], mode="nt", out_dtype=BF16, tm=1024, tn=1024, tk=1024, name="yb_bwd")
    g["w_ob"] = _matmul(sv["cb_in"], dyb, mode="tn", out_dtype=BF16, tm=1024, tn=1024, tk=2048, name="yb_wgrad")
    doa = _matmul(dya, w["w_oa"], mode="nt", out_dtype=BF16, tm=1024, tn=1024, tk=1024, name="ya_bwd")
    g["w_oa"] = _matmul(sv["oa"], dya, mode="tn", out_dtype=BF16, tm=1024, tn=1024, tk=2048, name="ya_wgrad")
    dproj, g["conv_wb"] = _conv_bwd(sv["proj"], dcb, w["conv_wb"], dproj, s=s)
    dproj, dfz, g["gla_norm_g"], g["w_fg2"], g["b_fg"] = _gla_bwd(
        sv["proj"], doa, sv["states"], w["w_fg2"], w["b_fg"], w["gla_norm_g"], dproj, s=s)
    return g, _put_fz(dproj, dfz, s=s)


def _proj_wgrad(dproj, sv, dep=None):
    return _matmul(dproj, sv["h"], mode="tn", out_dtype=BF16, tm=1664, tn=1024, tk=1024, name="proj_wgrad",
                   dep=dep)


def _proj_bwd(dproj, w, dep=None):
    return _matmul(dproj, w["w_in_t"], mode="nn", out_dtype=F32, tm=1024, tn=1024, tk=1664, name="proj_bwd",
                   dep=dep)


def _cols_from_chips(t):
    return jnp.transpose(t, (1, 0, 2)).reshape(t.shape[1], -1)


W_IN_ROWS = IN_WIDTH // N_CHIPS
W_IN_ROWS_PAD = -(-W_IN_ROWS // BF16_SUBLANES) * BF16_SUBLANES


def _w_in_t_shard(t):
    return jnp.pad(jnp.transpose(t, (0, 2, 1)), ((0, 0), (0, W_IN_ROWS_PAD - W_IN_ROWS), (0, 0)))


def _w_in_t_unshard(t):
    return jnp.transpose(t[:, :W_IN_ROWS], (0, 2, 1))


def _w_in_t_to_kernel(t):
    full = t[:, :W_IN_ROWS].reshape(IN_WIDTH, D_MODEL)
    pad = jnp.zeros((FZ_PAD - GATE_RANK, D_MODEL), t.dtype)
    return jnp.concatenate([full[:FZ_ORIG], full[FZ_ORIG + GATE_RANK:], full[FZ_ORIG:FZ_ORIG + GATE_RANK], pad],
                           axis=0)


def _w_in_t_from_kernel(g):
    full = jnp.concatenate([g[:FZ_ORIG], g[COL_FZ:COL_FZ + GATE_RANK], g[FZ_ORIG:COL_FZ]], axis=0)
    return jnp.pad(full.reshape(N_CHIPS, W_IN_ROWS, D_MODEL), ((0, 0), (0, W_IN_ROWS_PAD - W_IN_ROWS), (0, 0)))


def kernel(x, norm1_g, w_in, w_fg2, b_fg, gla_norm_g, w_oa, conv_w, conv_b, w_ob, w_o, norm2_g, w_ffn_gate, w_ffn_up, w_ffn_down, final_g, loss_target, m_norm1_g, m_w_in, m_w_fg2, m_b_fg, m_gla_norm_g, m_w_oa, m_conv_w, m_conv_b, m_w_ob, m_w_o, m_norm2_g, m_w_ffn_gate, m_w_ffn_up, m_w_ffn_down, m_final_g, v_norm1_g, v_w_in, v_w_fg2, v_b_fg, v_gla_norm_g, v_w_oa, v_conv_w, v_conv_b, v_w_ob, v_w_o, v_norm2_g, v_w_ffn_gate, v_w_ffn_up, v_w_ffn_down, v_final_g):
    cx_ = lax.axis_index("x")
    cy_ = lax.axis_index("y")
    cc_ = lax.axis_index("c")
    me = 2 * cx_ + cy_
    on_core0 = jnp.where(cc_ == 0, 1.0, 0.0).astype(F32).reshape(1, 1)

    def swap(t):
        return jnp.swapaxes(t, 1, 2)

    big_names = ["w_in", "w_oa", "w_ob", "w_o", "w_ffn_gate", "w_ffn_up", "w_ffn_down"]
    views = dict(
        w_in=tuple(_w_in_t_shard(t) for t in (w_in, m_w_in, v_w_in)),
        w_oa=(w_oa, m_w_oa, v_w_oa), w_ob=(w_ob, m_w_ob, v_w_ob), w_o=(w_o, m_w_o, v_w_o),
        w_ffn_gate=tuple(swap(t) for t in (w_ffn_gate, m_w_ffn_gate, v_w_ffn_gate)),
        w_ffn_up=tuple(swap(t) for t in (w_ffn_up, m_w_ffn_up, v_w_ffn_up)),
        w_ffn_down=(w_ffn_down, m_w_ffn_down, v_w_ffn_down))
    from_view = dict(w_in=_w_in_t_unshard, w_ffn_gate=swap, w_ffn_up=swap)

    s = x.shape[1]

    conv_w_p = jnp.pad(conv_w, ((0, 0), (0, 8 - conv_w.shape[1]), (0, 0)))
    w_fg2_all, conv_w_all = _all_gather_weights([w_fg2, conv_w_p])

    proj_names = ["w_in"]
    rest_names = ["w_oa", "w_ob", "w_o", "w_ffn_gate", "w_ffn_up", "w_ffn_down"]
    mixer_names = ["w_oa", "w_ob", "w_o"]
    ffn_names = ["w_ffn_gate", "w_ffn_up", "w_ffn_down"]
    weight_key = dict(w_in="w_in_t", w_ffn_gate="w_gate_t", w_ffn_up="w_up_t")
    full_shape = dict(w_oa=(VAL_WIDTH, D_MODEL), w_ob=(CONV_CH, D_MODEL), w_o=(D_MODEL, D_MODEL),
                      w_ffn_gate=(FFN_HIDDEN, D_MODEL), w_ffn_up=(FFN_HIDDEN, D_MODEL),
                      w_ffn_down=(FFN_HIDDEN, D_MODEL))

    def shards_of(l, names):
        return [views[n][0][l].astype(BF16) for n in names]

    def small_weights(l):
        w_fg2_full = _cols_from_chips(w_fg2_all[l])
        conv_w_full = _cols_from_chips(conv_w_all[l])[:3]
        return dict(
            norm1_g=norm1_g[l], norm2_g=norm2_g[l],
            w_fg2=jnp.pad(w_fg2_full, ((0, FZ_PAD - GATE_RANK), (0, 0))).astype(BF16),
            b_fg=b_fg[l].reshape(1, KEY_WIDTH), gla_norm_g=gla_norm_g[l].reshape(1, HEAD_V),
            conv_wb=(jnp.pad(conv_w_full, ((0, 5), (0, 0)))
                     + jnp.pad(conv_b[l].reshape(1, CONV_CH), ((3, 4), (0, 0)))))

    def full_weights(names, gathered, shards):
        out = {}
        for n, g, t in zip(names, gathered, shards):
            g = lax.dynamic_update_index_in_dim(g, t[None], me, axis=0)
            out[weight_key.get(n, n)] = _w_in_t_to_kernel(g) if n == "w_in" else g.reshape(full_shape[n])
        return out

    def per_chip(g, names):
        out = []
        for n in names:
            t = g[weight_key.get(n, n)]
            out.append(_w_in_t_from_kernel(t) if n == "w_in"
                       else t.reshape(N_CHIPS, t.shape[0] // N_CHIPS, t.shape[1]))
        return out

    def gather(l, names, tag, after=None):
        shards = shards_of(l, names)
        return shards, _gather_start(l, shards, name=f"gather_{tag}_start", after=after)

    def gathered_to_sibling(l, started, after, tag):
        return _handoff_start(l, _gather_wait(l, started, after, name=f"gather_{tag}_wait"),
                              name=f"handoff_{tag}_start")

    def feed(l, g, names, tag):
        return _sibling_push_start(l, False, per_chip(g, names), name=f"feed_{tag}_start")

    def pair_and_exchange(l, fed, after, names, tag):
        mine, sib = _sibling_push_wait(l, False, fed, after, name=f"feed_{tag}_wait")
        parts = [_pair_sum(a, b, name=f"pair_sum_{tag}_{n}") for n, a, b in zip(names, mine, sib)]
        return _chip_exchange_start(l, parts, name=f"exchange_{tag}_start")

    def total_and_share(l, swapped, after, names, tag):
        parts, recvs = _chip_exchange_wait(l, swapped, after, name=f"exchange_{tag}_wait")
        owns = [lax.dynamic_index_in_dim(p, me, 0, keepdims=False) for p in parts]
        totals = [_total_sum(o, rc, name=f"total_{tag}_{n}") for n, o, rc in zip(names, owns, recvs)]
        return _sibling_push_start(l, True, totals, name=f"share_{tag}_start")

    def shared(l, sharing, after, names, tag):
        totals, others = _sibling_push_wait(l, True, sharing, after, name=f"share_{tag}_wait")
        return {n: (t, o) for n, t, o in zip(names, totals, others)}

    shards_p, started = gather(0, proj_names, "p0")
    started = gathered_to_sibling(0, started, started[3], "p0")
    w0 = small_weights(0)
    w0.update(full_weights(proj_names, _handoff_wait(0, started, started[3], name="handoff_p0_wait"), shards_p))
    shards_r, started = gather(0, rest_names, "r0", after=w0["w_in_t"])
    sv0 = _mixer_forward_branches(x[0], w0, s, dep=started[3])
    started = gathered_to_sibling(0, started, sv0["oa"], "r0")
    w0.update(full_weights(rest_names, _handoff_wait(0, started, started[3], name="handoff_r0_wait"), shards_r))

    shards_1, started = gather(1, big_names, "l1", after=w0["w_o"])
    x1, sv0m = _mixer_forward_out(sv0, w0, s, dep=started[3])
    sv0f = _ffn_forward_hidden(x1, w0, s)
    started = gathered_to_sibling(1, started, sv0f["hid"], "l1")
    x2 = _ffn_forward_out(sv0f, w0, dep=started[3])
    w1 = small_weights(1)
    w1.update(full_weights(big_names, _handoff_wait(1, started, x2, name="handoff_l1_wait"), shards_1))

    x3, sv1m = _mixer_forward_out(_mixer_forward_branches(x2, w1, s), w1, s)
    sv1f = _ffn_forward_hidden(x3, w1, s)
    x4 = _ffn_forward_out(sv1f, w1)
    loss_local, dx, dxb, dgf = _loss_head(x4, final_g, loss_target[0])
    loss = lax.psum(loss_local[0, 0], ("x", "y", "c"))

    g1, dgt, dup, _ = _ffn_backward_grads(dxb, w1, sv1f, s)
    dx_mid, dxb_mid, g1["norm2_g"] = _ffn_backward_input(dgt, dup, dx, w1, sv1f)
    gm, dproj = _mixer_backward_branches(dxb_mid, w1, sv1m, s)
    g1.update(gm)
    g1["w_in_t"] = _proj_wgrad(dproj, sv1m)
    fed_1 = feed(1, g1, big_names, "l1")
    dh = _proj_bwd(dproj, w1, dep=fed_1[3])
    dx, dxb, g1["norm1_g"] = _rms_bwd(sv1m["x0"], norm1_g[1], dh, dx_mid, name="rms1_bwd")

    g0, dgt, dup, dhid = _ffn_backward_grads(dxb, w0, sv0f, s)
    swap_1 = pair_and_exchange(1, fed_1, dhid, big_names, "l1")
    fed_f = feed(0, g0, ffn_names, "f0")
    dx_mid, dxb_mid, g0["norm2_g"] = _ffn_backward_input(dgt, dup, dx, w0, sv0f, dep=swap_1[3] + fed_f[3])
    swap_f = pair_and_exchange(0, fed_f, dx_mid, ffn_names, "f0")
    gm, dproj = _mixer_backward_branches(dxb_mid, w0, sv0m, s, dep=swap_f[3])
    g0.update(gm)
    share_1 = total_and_share(1, swap_1, dproj, big_names, "l1")
    share_f = total_and_share(0, swap_f, dproj, ffn_names, "f0")
    fed_m = feed(0, g0, mixer_names, "m0")
    g0["w_in_t"] = _proj_wgrad(dproj, sv0m, dep=share_1[3] + share_f[3] + fed_m[3])
    fed_p = feed(0, g0, proj_names, "p0")
    swap_m = pair_and_exchange(0, fed_m, g0["w_in_t"], mixer_names, "m0")
    dh = _proj_bwd(dproj, w0, dep=fed_p[3] + swap_m[3])
    reduced1 = shared(1, share_1, dh, big_names, "l1")
    reduced0 = shared(0, share_f, dh, ffn_names, "f0")
    grad_x, _, g0["norm1_g"] = _rms_bwd(sv0m["x0"], norm1_g[0], dh, dx_mid, name="rms1_bwd")

    swap_p = pair_and_exchange(0, fed_p, grad_x, proj_names, "p0")
    share_m = total_and_share(0, swap_m, swap_p[3], mixer_names, "m0")
    share_p = total_and_share(0, swap_p, share_m[3], proj_names, "p0")
    reduced0.update(shared(0, share_m, share_p[3], mixer_names, "m0"))
    reduced0.update(shared(0, share_p, share_p[3], proj_names, "p0"))
    grads = [g0, g1]

    def small_rows(t):
        return t.reshape(-1, D_MODEL)

    def tile_rows(t):
        return jnp.pad(t, ((0, -t.shape[0] % F32_SUBLANES), (0, 0)))

    g0, g1 = grads
    pieces = [
        jnp.concatenate([g0["norm1_g"], g1["norm1_g"]], axis=0),
        jnp.concatenate([g0["norm2_g"], g1["norm2_g"]], axis=0),
        dgf,
        small_rows(jnp.concatenate([g0["b_fg"], g1["b_fg"]], axis=1)),
        small_rows(jnp.concatenate([g0["gla_norm_g"], g1["gla_norm_g"],
                                    jnp.zeros((1, D_MODEL - 2 * HEAD_V), F32)], axis=1)),
        jnp.concatenate([g0["conv_wb"][3:4], g1["conv_wb"][3:4]], axis=0),
        jnp.concatenate([g0["conv_wb"][:3], g1["conv_wb"][:3]], axis=0),
        small_rows(jnp.stack([g0["w_fg2"][:GATE_RANK], g1["w_fg2"][:GATE_RANK]])),
    ]
    small = _all_reduce_small(jnp.concatenate([tile_rows(p) for p in pieces], axis=0))
    sg = dict(
        norm1_g=small[0:2], norm2_g=small[8:10], final_g=small[16],
        b_fg=small[24].reshape(DEPTH, KEY_WIDTH), gla_norm_g=small[32, :DEPTH * HEAD_V].reshape(DEPTH, HEAD_V),
        conv_b=small[40:42],
        conv_w=lax.dynamic_slice_in_dim(small[48:54].reshape(DEPTH, 3, CONV_CH), me * (CONV_CH // N_CHIPS),
                                        CONV_CH // N_CHIPS, axis=2),
        w_fg2=lax.dynamic_slice_in_dim(small[56:72].reshape(DEPTH, GATE_RANK, KEY_WIDTH),
                                       me * (KEY_WIDTH // N_CHIPS), KEY_WIDTH // N_CHIPS, axis=2),
    )

    small_params = dict(norm1_g=(norm1_g, m_norm1_g, v_norm1_g), w_fg2=(w_fg2, m_w_fg2, v_w_fg2),
                        b_fg=(b_fg, m_b_fg, v_b_fg), gla_norm_g=(gla_norm_g, m_gla_norm_g, v_gla_norm_g),
                        conv_w=(conv_w, m_conv_w, v_conv_w), conv_b=(conv_b, m_conv_b, v_conv_b),
                        norm2_g=(norm2_g, m_norm2_g, v_norm2_g), final_g=(final_g, m_final_g, v_final_g))
    order = ["norm1_g", "w_in", "w_fg2", "b_fg", "gla_norm_g", "w_oa", "conv_w", "conv_b", "w_ob", "w_o",
             "norm2_g", "w_ffn_gate", "w_ffn_up", "w_ffn_down", "final_g"]
    results = {}
    for i, n in enumerate(big_names):
        w_, m_, v_ = views[n]
        outs = _adamw_layers(w_, (reduced0[n][0], reduced1[n][0]), (reduced0[n][1], reduced1[n][1]), on_core0,
                             m_, v_, name="adamw_" + n)
        back = from_view.get(n)
        results[n] = tuple(back(r) for r in outs) if back else outs
    for n, (w_, m_, v_) in small_params.items():
        results[n] = (sg[n],) + _adamw(w_, sg[n], m_, v_, name="adamw_" + n)
    return (loss, grad_x[None], *[results[n][0] for n in order], *[results[n][1] for n in order],
            *[results[n][2] for n in order], *[results[n][3] for n in order])
```

```python
import functools

import jax
import jax.numpy as jnp
from jax import lax
from jax.experimental import pallas as pl
from jax.experimental.pallas import tpu as pltpu

F32 = jnp.float32
BF16 = jnp.bfloat16

D_MODEL = 1024
DEPTH = 2
CHUNK = 64
GLA_HEADS = 4
KEY_WIDTH = 512
VAL_WIDTH = 1024
HEAD_K = 128
HEAD_V = 256
GATE_RANK = 16
GATE_TAU = 16.0
CONV_CH = 1024
FFN_HIDDEN = 2816
IN_WIDTH = 8208
NORM_EPS = 1e-6
Q_SCALE = HEAD_K ** -0.5
ADAM_LR = 0.001
ADAM_B1 = 0.9
ADAM_B2 = 0.999
ADAM_EPS = 1e-08
ADAM_WD = 0.01
ADAM_STEP = 10

N_CHIPS = 4
N_DEV = 8

LANE = 128
FZ_PAD = LANE
PROJ_W = 8192 + FZ_PAD
COL_Q, COL_K, COL_V, COL_R, COL_GBI, COL_GCI, COL_CX, COL_GA, COL_GB, COL_FZ = (
    0, 512, 1024, 2048, 3072, 4096, 5120, 6144, 7168, 8192)
FZ_ORIG = 3072

VMEM_LIMIT = 52 * 1024 * 1024
HALO = 16


def _cparams(sem=None):
    return pltpu.CompilerParams(dimension_semantics=sem, vmem_limit_bytes=VMEM_LIMIT)


def _sigmoid(x):
    return jax.nn.sigmoid(x)


def _log_sigmoid(x):
    return jnp.minimum(x, 0.0) - jnp.log1p(jnp.exp(-jnp.abs(x)))


_DOT_DIMS = {
    "nn": (((1,), (0,)), ((), ())),
    "nt": (((1,), (1,)), ((), ())),
    "tn": (((0,), (0,)), ((), ())),
}


def _matmul(a, b, *, mode, out_dtype, tm, tn, tk, name, add=None, dep=None):
    if mode == "nn":
        (m, k), n = a.shape, b.shape[1]
    elif mode == "nt":
        (m, k), n = a.shape, b.shape[0]
    else:
        (k, m), n = a.shape, b.shape[1]
    tm, tn, tk = min(tm, m), min(tn, n), min(tk, k)
    assert m % tm == 0 and n % tn == 0 and k % tk == 0, (name, m, n, k, tm, tn, tk)
    nk = k // tk
    has_add = add is not None
    has_dep = dep is not None

    def body(*refs):
        if has_dep:
            refs = refs[1:]
        if has_add:
            a_ref, b_ref, add_ref, o_ref = refs[:4]
            scratch = refs[4:]
        else:
            a_ref, b_ref, o_ref = refs[:3]
            add_ref = None
            scratch = refs[3:]
        part = lax.dot_general(a_ref[...], b_ref[...], _DOT_DIMS[mode], preferred_element_type=F32)

        def finish(acc):
            if add_ref is not None:
                acc = acc + add_ref[...].astype(F32)
            o_ref[...] = acc.astype(o_ref.dtype)

        if nk == 1:
            finish(part)
        else:
            acc_ref = scratch[0]
            kk = pl.program_id(2)

            @pl.when(kk == 0)
            def _():
                acc_ref[...] = part

            @pl.when(kk > 0)
            def _():
                acc_ref[...] += part

            @pl.when(kk == nk - 1)
            def _():
                finish(acc_ref[...])

    if mode == "nn":
        a_spec = pl.BlockSpec((tm, tk), lambda i, j, kk: (i, kk))
        b_spec = pl.BlockSpec((tk, tn), lambda i, j, kk: (kk, j))
    elif mode == "nt":
        a_spec = pl.BlockSpec((tm, tk), lambda i, j, kk: (i, kk))
        b_spec = pl.BlockSpec((tn, tk), lambda i, j, kk: (j, kk))
    else:
        a_spec = pl.BlockSpec((tk, tm), lambda i, j, kk: (kk, i))
        b_spec = pl.BlockSpec((tk, tn), lambda i, j, kk: (kk, j))
    o_spec = pl.BlockSpec((tm, tn), lambda i, j, kk: (i, j))
    in_specs = [a_spec, b_spec] + ([o_spec] if has_add else [])
    operands = (a, b) + ((add,) if has_add else ())
    if has_dep:
        in_specs = [pl.BlockSpec(dep.shape, lambda i, j, kk: (0, 0))] + in_specs
        operands = (dep,) + operands
    return pl.pallas_call(
        body,
        name=name,
        out_shape=jax.ShapeDtypeStruct((m, n), out_dtype),
        grid=(m // tm, n // tn, nk),
        in_specs=in_specs,
        out_specs=o_spec,
        scratch_shapes=[pltpu.VMEM((tm, tn), F32)] if nk > 1 else [],
        compiler_params=_cparams(("parallel", "parallel", "arbitrary")),
    )(*operands)


def _rms_fwd(x, g, *, name):
    s, d = x.shape
    tr = min(512, s)

    def body(x_ref, g_ref, h_ref):
        xv = x_ref[...]
        rs = lax.rsqrt(jnp.mean(xv * xv, axis=-1, keepdims=True) + NORM_EPS)
        h_ref[...] = (xv * rs * g_ref[...]).astype(BF16)

    return pl.pallas_call(
        body, name=name,
        out_shape=jax.ShapeDtypeStruct((s, d), BF16),
        grid=(s // tr,),
        in_specs=[pl.BlockSpec((tr, d), lambda i: (i, 0)), pl.BlockSpec((1, d), lambda i: (0, 0))],
        out_specs=pl.BlockSpec((tr, d), lambda i: (i, 0)),
        compiler_params=_cparams(("parallel",)),
    )(x, g.reshape(1, d))


def _rms_bwd(x, g, dh, dres, *, name):
    s, d = x.shape
    tr = min(512, s)

    def body(x_ref, g_ref, dh_ref, dres_ref, dx_ref, dxb_ref, dg_ref):
        i = pl.program_id(0)
        xv = x_ref[...]
        rs = lax.rsqrt(jnp.mean(xv * xv, axis=-1, keepdims=True) + NORM_EPS)
        n = xv * rs
        dhv = dh_ref[...].astype(F32)
        dn = dhv * g_ref[...]
        dx = dres_ref[...] + rs * (dn - n * jnp.mean(dn * n, axis=-1, keepdims=True))
        dx_ref[...] = dx
        dxb_ref[...] = dx.astype(BF16)
        part = jnp.sum(dhv * n, axis=0, keepdims=True)

        @pl.when(i == 0)
        def _():
            dg_ref[...] = part

        @pl.when(i > 0)
        def _():
            dg_ref[...] += part

    row = pl.BlockSpec((tr, d), lambda i: (i, 0))
    vec = pl.BlockSpec((1, d), lambda i: (0, 0))
    return pl.pallas_call(
        body, name=name,
        out_shape=(jax.ShapeDtypeStruct((s, d), F32), jax.ShapeDtypeStruct((s, d), BF16),
                   jax.ShapeDtypeStruct((1, d), F32)),
        grid=(s // tr,),
        in_specs=[row, vec, row, row],
        out_specs=(row, row, vec),
        compiler_params=_cparams(("arbitrary",)),
    )(x, g.reshape(1, d), dh, dres)


def _loss_head(x, g, target):
    s, d = x.shape
    tr = min(512, s)

    def body(x_ref, g_ref, t_ref, loss_ref, dx_ref, dxb_ref, dg_ref):
        i = pl.program_id(0)
        xv = x_ref[...]
        rs = lax.rsqrt(jnp.mean(xv * xv, axis=-1, keepdims=True) + NORM_EPS)
        n = xv * rs
        gv = g_ref[...]
        err = n * gv - t_ref[...]
        row_loss = jnp.mean(err * err, axis=-1, keepdims=True)
        loss_part = 0.5 * jnp.sum(row_loss, axis=0, keepdims=True)
        dy = err * (1.0 / d)
        dn = dy * gv
        dx = rs * (dn - n * jnp.mean(dn * n, axis=-1, keepdims=True))
        dx_ref[...] = dx
        dxb_ref[...] = dx.astype(BF16)
        dg_part = jnp.sum(dy * n, axis=0, keepdims=True)

        @pl.when(i == 0)
        def _():
            loss_ref[...] = loss_part
            dg_ref[...] = dg_part

        @pl.when(i > 0)
        def _():
            loss_ref[...] += loss_part
            dg_ref[...] += dg_part

    row = pl.BlockSpec((tr, d), lambda i: (i, 0))
    vec = pl.BlockSpec((1, d), lambda i: (0, 0))
    one = pl.BlockSpec((1, 1), lambda i: (0, 0))
    return pl.pallas_call(
        body, name="loss_head",
        out_shape=(jax.ShapeDtypeStruct((1, 1), F32), jax.ShapeDtypeStruct((s, d), F32),
                   jax.ShapeDtypeStruct((s, d), BF16), jax.ShapeDtypeStruct((1, d), F32)),
        grid=(s // tr,),
        in_specs=[row, vec, row],
        out_specs=(one, row, row, vec),
        compiler_params=_cparams(("arbitrary",)),
    )(x, g.reshape(1, d), target)


def _conv_taps(u_prev, u, w_ref, rows):
    ext = jnp.concatenate([u_prev, u], axis=0)
    u1 = pltpu.roll(ext, 1, 0)[HALO:HALO + rows]
    u2 = pltpu.roll(ext, 2, 0)[HALO:HALO + rows]
    conv = w_ref[0:1, :] * u2 + w_ref[1:2, :] * u1 + w_ref[2:3, :] * u + w_ref[3:4, :]
    return conv, u1, u2


def _conv_fwd(proj, conv_wb, *, s):
    tr = min(512, s)
    c = CONV_CH
    hb = tr // HALO

    def body(gbi_ref, gci_ref, cx_ref, gci_h_ref, cx_h_ref, w_ref, o_ref):
        i = pl.program_id(0)
        u = gci_ref[...].astype(F32) * cx_ref[...].astype(F32)
        u_prev = gci_h_ref[...].astype(F32) * cx_h_ref[...].astype(F32)
        u_prev = jnp.where(i == 0, 0.0, u_prev)
        conv, _, _ = _conv_taps(u_prev, u, w_ref, tr)
        o_ref[...] = (gbi_ref[...].astype(F32) * conv).astype(BF16)

    def seg(col):
        return pl.BlockSpec((tr, c), lambda i: (i, col // c))

    def halo(col):
        return pl.BlockSpec((HALO, c), lambda i: (jnp.maximum(i * hb - 1, 0), col // c))

    return pl.pallas_call(
        body, name="conv_fwd",
        out_shape=jax.ShapeDtypeStruct((s, c), BF16),
        grid=(s // tr,),
        in_specs=[seg(COL_GBI), seg(COL_GCI), seg(COL_CX), halo(COL_GCI), halo(COL_CX),
                  pl.BlockSpec((8, c), lambda i: (0, 0))],
        out_specs=pl.BlockSpec((tr, c), lambda i: (i, 0)),
        compiler_params=_cparams(("parallel",)),
    )(proj, proj, proj, proj, proj, conv_wb)


def _conv_bwd(proj, dcb, conv_wb, dproj, *, s):
    tr = min(512, s)
    c = CONV_CH
    hb = tr // HALO
    nb = s // tr

    def body(gbi_ref, gci_ref, cx_ref, dcb_ref, gci_h_ref, cx_h_ref, gbi_n_ref, dcb_n_ref, w_ref,
             dproj_in_ref, dp_ref, dw_ref):
        i = pl.program_id(0)
        gbi = gbi_ref[...].astype(F32)
        gci = gci_ref[...].astype(F32)
        cx = cx_ref[...].astype(F32)
        dcb_v = dcb_ref[...].astype(F32)
        u = gci * cx
        u_prev = jnp.where(i == 0, 0.0, gci_h_ref[...].astype(F32) * cx_h_ref[...].astype(F32))
        conv, u1, u2 = _conv_taps(u_prev, u, w_ref, tr)
        dconv = dcb_v * gbi
        dconv_next = jnp.where(i == nb - 1, 0.0, dcb_n_ref[...].astype(F32) * gbi_n_ref[...].astype(F32))
        ext = jnp.concatenate([dconv, dconv_next], axis=0)
        n_ext = tr + HALO
        d1 = pltpu.roll(ext, n_ext - 1, 0)[0:tr]
        d2 = pltpu.roll(ext, n_ext - 2, 0)[0:tr]
        du = w_ref[2:3, :] * dconv + w_ref[1:2, :] * d1 + w_ref[0:1, :] * d2
        dp_ref[:, 0:c] = (dcb_v * conv).astype(BF16)
        dp_ref[:, c:2 * c] = (du * cx).astype(BF16)
        dp_ref[:, 2 * c:3 * c] = (du * gci).astype(BF16)
        part = jnp.concatenate([
            jnp.sum(dconv * u2, axis=0, keepdims=True),
            jnp.sum(dconv * u1, axis=0, keepdims=True),
            jnp.sum(dconv * u, axis=0, keepdims=True),
            jnp.sum(dconv, axis=0, keepdims=True),
            jnp.zeros((4, c), F32)], axis=0)

        @pl.when(i == 0)
        def _():
            dw_ref[...] = part

        @pl.when(i > 0)
        def _():
            dw_ref[...] += part

    def seg(col):
        return pl.BlockSpec((tr, c), lambda i: (i, col // c))

    def halo_prev(col):
        return pl.BlockSpec((HALO, c), lambda i: (jnp.maximum(i * hb - 1, 0), col // c))

    def halo_next(col):
        return pl.BlockSpec((HALO, c), lambda i: (jnp.minimum((i + 1) * hb, nb * hb - 1), col // c))

    return pl.pallas_call(
        body, name="conv_bwd",
        out_shape=(jax.ShapeDtypeStruct((s, PROJ_W), BF16), jax.ShapeDtypeStruct((8, c), F32)),
        grid=(nb,),
        in_specs=[seg(COL_GBI), seg(COL_GCI), seg(COL_CX), pl.BlockSpec((tr, c), lambda i: (i, 0)),
                  halo_prev(COL_GCI), halo_prev(COL_CX), halo_next(COL_GBI),
                  pl.BlockSpec((HALO, c), lambda i: (jnp.minimum((i + 1) * hb, nb * hb - 1), 0)),
                  pl.BlockSpec((8, c), lambda i: (0, 0)), ANY],
        out_specs=(pl.BlockSpec((tr, 3 * c), lambda i: (i, COL_GBI // (3 * c))),
                   pl.BlockSpec((8, c), lambda i: (0, 0))),
        input_output_aliases={9: 0},
        compiler_params=_cparams(("arbitrary",)),
    )(proj, proj, proj, dcb, proj, proj, proj, dcb, conv_wb, dproj)


def _mix_fwd(proj, ya, yb, *, s):
    tr = min(512, s)
    d = D_MODEL

    def body(ga_ref, gb_ref, ya_ref, yb_ref, o_ref):
        sa = _sigmoid(ga_ref[...].astype(F32))
        sb = _sigmoid(gb_ref[...].astype(F32))
        o_ref[...] = (sa * ya_ref[...].astype(F32) + sb * yb_ref[...].astype(F32)).astype(BF16)

    row = pl.BlockSpec((tr, d), lambda i: (i, 0))
    return pl.pallas_call(
        body, name="mix_fwd",
        out_shape=jax.ShapeDtypeStruct((s, d), BF16),
        grid=(s // tr,),
        in_specs=[pl.BlockSpec((tr, d), lambda i: (i, COL_GA // d)),
                  pl.BlockSpec((tr, d), lambda i: (i, COL_GB // d)), row, row],
        out_specs=row,
        compiler_params=_cparams(("parallel",)),
    )(proj, proj, ya, yb)


def _mix_bwd(proj, ya, yb, dmix, *, s):
    tr = min(512, s)
    d = D_MODEL

    def body(ga_ref, gb_ref, ya_ref, yb_ref, dm_ref, dya_ref, dyb_ref, dg_ref):
        sa = _sigmoid(ga_ref[...].astype(F32))
        sb = _sigmoid(gb_ref[...].astype(F32))
        dm = dm_ref[...].astype(F32)
        dya_ref[...] = (dm * sa).astype(BF16)
        dyb_ref[...] = (dm * sb).astype(BF16)
        dg_ref[:, 0:d] = (dm * ya_ref[...].astype(F32) * sa * (1.0 - sa)).astype(BF16)
        dg_ref[:, d:2 * d] = (dm * yb_ref[...].astype(F32) * sb * (1.0 - sb)).astype(BF16)

    row = pl.BlockSpec((tr, d), lambda i: (i, 0))
    return pl.pallas_call(
        body, name="mix_bwd",
        out_shape=(jax.ShapeDtypeStruct((s, d), BF16), jax.ShapeDtypeStruct((s, d), BF16),
                   jax.ShapeDtypeStruct((s, PROJ_W), BF16)),
        grid=(s // tr,),
        in_specs=[pl.BlockSpec((tr, d), lambda i: (i, COL_GA // d)),
                  pl.BlockSpec((tr, d), lambda i: (i, COL_GB // d)), row, row, row],
        out_specs=(row, row, pl.BlockSpec((tr, 2 * d), lambda i: (i, COL_GA // (2 * d)))),
        compiler_params=_cparams(("parallel",)),
    )(proj, proj, ya, yb, dmix)


def _swiglu_fwd(gt, up, *, s):
    tr = min(256, s)
    f = FFN_HIDDEN

    def body(gt_ref, up_ref, o_ref):
        gv = gt_ref[...].astype(F32)
        o_ref[...] = (gv * _sigmoid(gv) * up_ref[...].astype(F32)).astype(BF16)

    row = pl.BlockSpec((tr, f), lambda i: (i, 0))
    return pl.pallas_call(
        body, name="swiglu_fwd",
        out_shape=jax.ShapeDtypeStruct((s, f), BF16),
        grid=(s // tr,),
        in_specs=[row, row], out_specs=row,
        compiler_params=_cparams(("parallel",)),
    )(gt, up)


def _swiglu_bwd(gt, up, dhid, *, s):
    tr = min(256, s)
    f = FFN_HIDDEN

    def body(gt_ref, up_ref, dh_ref, dgt_ref, dup_ref):
        gv = gt_ref[...].astype(F32)
        uv = up_ref[...].astype(F32)
        dh = dh_ref[...].astype(F32)
        sg = _sigmoid(gv)
        dgt_ref[...] = (dh * uv * sg * (1.0 + gv * (1.0 - sg))).astype(BF16)
        dup_ref[...] = (dh * gv * sg).astype(BF16)

    row = pl.BlockSpec((tr, f), lambda i: (i, 0))
    return pl.pallas_call(
        body, name="swiglu_bwd",
        out_shape=(jax.ShapeDtypeStruct((s, f), BF16), jax.ShapeDtypeStruct((s, f), BF16)),
        grid=(s // tr,),
        in_specs=[row, row, row], out_specs=(row, row),
        compiler_params=_cparams(("parallel",)),
    )(gt, up, dhid)


def _tri(strict):
    r = lax.broadcasted_iota(jnp.int32, (CHUNK, CHUNK), 0)
    c = lax.broadcasted_iota(jnp.int32, (CHUNK, CHUNK), 1)
    return jnp.where((c < r) if strict else (c <= r), 1.0, 0.0).astype(F32)


def _gla_gate_terms(la_c, tri):
    cum = jnp.dot(tri, la_c, precision=lax.Precision.HIGHEST, preferred_element_type=F32)
    cend = cum[CHUNK - 1:CHUNK, :]
    return jnp.exp(cend - cum), jnp.exp(cend)


def _hk(h):
    return slice(h * HEAD_K, (h + 1) * HEAD_K)


def _hv(h):
    return slice(h * HEAD_V, (h + 1) * HEAD_V)


def _gla_fwd(proj, wfg2p, bfg, gn, *, s):
    tr = min(512, s)
    nb = s // tr
    nc = tr // CHUNK

    def body(q_ref, k_ref, v_ref, r_ref, fz_ref, w_ref, b_ref, gn_ref, oa_ref, st_ref,
             state, la_scr, o_scr):
        i = pl.program_id(0)

        @pl.when(i == 0)
        def _():
            state[...] = jnp.zeros_like(state)

        fg = jnp.dot(fz_ref[...], w_ref[...], preferred_element_type=F32) + b_ref[...]
        la_scr[...] = _log_sigmoid(fg) * (1.0 / GATE_TAU)
        tri = _tri(False)

        def chunk(ci, carry):
            r0 = pl.multiple_of(ci * CHUNK, CHUNK)
            rows = pl.ds(r0, CHUNK)
            e, gam = _gla_gate_terms(la_scr[rows, :], tri)
            kd = (k_ref[rows, :].astype(F32) * e).astype(BF16)
            qs = (q_ref[rows, :].astype(F32) * Q_SCALE).astype(BF16)
            v_c = v_ref[rows, :]
            for h in range(GLA_HEADS):
                upd = lax.dot_general(v_c[:, _hv(h)], kd[:, _hk(h)], _DOT_DIMS["tn"],
                                      preferred_element_type=F32)
                st_h = state[:, _hk(h)] * gam[:, _hk(h)] + upd
                state[:, _hk(h)] = st_h
                o_scr[rows, _hv(h)] = lax.dot_general(qs[:, _hk(h)], st_h.astype(BF16), _DOT_DIMS["nt"],
                                                       preferred_element_type=F32)
            st_ref[ci] = state[...]
            return carry

        lax.fori_loop(0, nc, chunk, 0, unroll=True)
        for h in range(GLA_HEADS):
            o = o_scr[:, _hv(h)]
            rs = lax.rsqrt(jnp.mean(o * o, axis=-1, keepdims=True) + NORM_EPS)
            rv = r_ref[:, _hv(h)].astype(F32)
            oa_ref[:, _hv(h)] = ((o * rs * gn_ref[...]).astype(F32) * (rv * _sigmoid(rv))).astype(BF16)

    return pl.pallas_call(
        body, name="gla_fwd",
        out_shape=(jax.ShapeDtypeStruct((s, VAL_WIDTH), BF16),
                   jax.ShapeDtypeStruct((s // CHUNK, HEAD_V, KEY_WIDTH), F32)),
        grid=(nb,),
        in_specs=[pl.BlockSpec((tr, KEY_WIDTH), lambda i: (i, COL_Q // KEY_WIDTH)),
                  pl.BlockSpec((tr, KEY_WIDTH), lambda i: (i, COL_K // KEY_WIDTH)),
                  pl.BlockSpec((tr, VAL_WIDTH), lambda i: (i, COL_V // VAL_WIDTH)),
                  pl.BlockSpec((tr, VAL_WIDTH), lambda i: (i, COL_R // VAL_WIDTH)),
                  pl.BlockSpec((tr, FZ_PAD), lambda i: (i, COL_FZ // FZ_PAD)),
                  pl.BlockSpec((FZ_PAD, KEY_WIDTH), lambda i: (0, 0)),
                  pl.BlockSpec((1, KEY_WIDTH), lambda i: (0, 0)),
                  pl.BlockSpec((1, HEAD_V), lambda i: (0, 0))],
        out_specs=(pl.BlockSpec((tr, VAL_WIDTH), lambda i: (i, 0)),
                   pl.BlockSpec((nc, HEAD_V, KEY_WIDTH), lambda i: (i, 0, 0))),
        scratch_shapes=[pltpu.VMEM((HEAD_V, KEY_WIDTH), F32), pltpu.VMEM((tr, KEY_WIDTH), F32),
                        pltpu.VMEM((tr, VAL_WIDTH), F32)],
        compiler_params=_cparams(("arbitrary",)),
    )(proj, proj, proj, proj, proj, wfg2p, bfg, gn)


def _gla_bwd(proj, doa, states, wfg2p, bfg, gn, dproj, *, s):
    tr = min(512, s)
    nb = s // tr
    nc = tr // CHUNK

    def body(q_ref, k_ref, v_ref, r_ref, fz_ref, doa_ref, st_ref, stp_ref, w_ref, b_ref, gn_ref, dproj_in_ref,
             dp_ref, dfz_ref, dgn_ref, dw_ref, db_ref,
             carry, fg_scr, la_scr, dla_scr, o_scr, do_scr):
        i = pl.program_id(0)

        @pl.when(i == 0)
        def _():
            carry[...] = jnp.zeros_like(carry)
            dgn_ref[...] = jnp.zeros_like(dgn_ref)
            dw_ref[...] = jnp.zeros_like(dw_ref)
            db_ref[...] = jnp.zeros_like(db_ref)

        fz = fz_ref[...]
        fg = jnp.dot(fz, w_ref[...], preferred_element_type=F32) + b_ref[...]
        fg_scr[...] = fg
        la_scr[...] = _log_sigmoid(fg) * (1.0 / GATE_TAU)

        def fwd_chunk(ci, c0):
            r0 = pl.multiple_of(ci * CHUNK, CHUNK)
            rows = pl.ds(r0, CHUNK)
            qs = (q_ref[rows, :].astype(F32) * Q_SCALE).astype(BF16)
            st = st_ref[ci].astype(BF16)
            for h in range(GLA_HEADS):
                o_scr[rows, _hv(h)] = lax.dot_general(qs[:, _hk(h)], st[:, _hk(h)], _DOT_DIMS["nt"],
                                                       preferred_element_type=F32)
            return c0

        lax.fori_loop(0, nc, fwd_chunk, 0, unroll=True)

        gnv = gn_ref[...]
        dgn_part = jnp.zeros((1, HEAD_V), F32)
        for h in range(GLA_HEADS):
            o = o_scr[:, _hv(h)]
            rs = lax.rsqrt(jnp.mean(o * o, axis=-1, keepdims=True) + NORM_EPS)
            nrm = o * rs
            rv = r_ref[:, _hv(h)].astype(F32)
            sg = _sigmoid(rv)
            doa_h = doa_ref[:, _hv(h)].astype(F32)
            don = doa_h * (rv * sg)
            dp_ref[:, COL_R + h * HEAD_V:COL_R + (h + 1) * HEAD_V] = (
                doa_h * (nrm * gnv) * (sg * (1.0 + rv * (1.0 - sg)))).astype(BF16)
            dgn_part = dgn_part + jnp.sum(don * nrm, axis=0, keepdims=True)
            dn = don * gnv
            do_scr[:, _hv(h)] = (rs * (dn - nrm * jnp.mean(dn * nrm, axis=-1, keepdims=True))).astype(BF16)
        dgn_ref[...] += dgn_part

        tri = _tri(False)
        tri_s = _tri(True)
        first_block = i == nb - 1

        def bwd_chunk(cc, c0):
            ci = nc - 1 - cc
            r0 = pl.multiple_of(ci * CHUNK, CHUNK)
            rows = pl.ds(r0, CHUNK)
            e, gam = _gla_gate_terms(la_scr[rows, :], tri)
            k_c = k_ref[rows, :].astype(F32)
            kd = k_c * e
            kd_b = kd.astype(BF16)
            qs = (q_ref[rows, :].astype(F32) * Q_SCALE).astype(BF16)
            v_c = v_ref[rows, :]
            do_c = do_scr[rows, :]
            st = st_ref[ci]
            st_b = st.astype(BF16)
            st_prev_in = st_ref[jnp.maximum(ci - 1, 0)]
            st_prev_edge = jnp.where(first_block, 0.0, stp_ref[0])
            st_prev = jnp.where(ci > 0, st_prev_in, st_prev_edge)
            dkd_parts = []
            dgam_parts = []
            for h in range(GLA_HEADS):
                dst = lax.dot_general(do_c[:, _hv(h)], qs[:, _hk(h)], _DOT_DIMS["tn"],
                                      preferred_element_type=F32) + carry[:, _hk(h)]
                dst_b = dst.astype(BF16)
                dqs = jnp.dot(do_c[:, _hv(h)], st_b[:, _hk(h)], preferred_element_type=F32)
                dp_ref[rows, COL_Q + h * HEAD_K:COL_Q + (h + 1) * HEAD_K] = (dqs * Q_SCALE).astype(BF16)
                dkd_parts.append(jnp.dot(v_c[:, _hv(h)], dst_b, preferred_element_type=F32))
                dp_ref[rows, COL_V + h * HEAD_V:COL_V + (h + 1) * HEAD_V] = lax.dot_general(
                    kd_b[:, _hk(h)], dst_b, _DOT_DIMS["nt"], preferred_element_type=F32).astype(BF16)
                dgam_parts.append(jnp.sum(dst * st_prev[:, _hk(h)], axis=0, keepdims=True))
                carry[:, _hk(h)] = dst * gam[:, _hk(h)]
            dkd = jnp.concatenate(dkd_parts, axis=1)
            dgam = jnp.concatenate(dgam_parts, axis=1)
            dp_ref[rows, COL_K:COL_K + KEY_WIDTH] = (dkd * e).astype(BF16)
            dz = dkd * kd
            dla_scr[rows, :] = gam * dgam + jnp.dot(tri_s, dz, precision=lax.Precision.HIGHEST,
                                                    preferred_element_type=F32)
            return c0

        lax.fori_loop(0, nc, bwd_chunk, 0, unroll=True)

        dfg = dla_scr[...] * (1.0 / GATE_TAU) * _sigmoid(-fg_scr[...])
        dfg_b = dfg.astype(BF16)
        dfz_ref[...] = lax.dot_general(dfg_b, w_ref[...], _DOT_DIMS["nt"],
                                       preferred_element_type=F32).astype(BF16)
        dw_ref[...] += lax.dot_general(fz, dfg_b, _DOT_DIMS["tn"], preferred_element_type=F32)
        db_ref[...] += jnp.sum(dfg, axis=0, keepdims=True)

    def rev(i):
        return nb - 1 - i

    return pl.pallas_call(
        body, name="gla_bwd",
        out_shape=(jax.ShapeDtypeStruct((s, PROJ_W), BF16),
                   jax.ShapeDtypeStruct((s, FZ_PAD), BF16),
                   jax.ShapeDtypeStruct((1, HEAD_V), F32),
                   jax.ShapeDtypeStruct((FZ_PAD, KEY_WIDTH), F32),
                   jax.ShapeDtypeStruct((1, KEY_WIDTH), F32)),
        grid=(nb,),
        in_specs=[pl.BlockSpec((tr, KEY_WIDTH), lambda i: (rev(i), COL_Q // KEY_WIDTH)),
                  pl.BlockSpec((tr, KEY_WIDTH), lambda i: (rev(i), COL_K // KEY_WIDTH)),
                  pl.BlockSpec((tr, VAL_WIDTH), lambda i: (rev(i), COL_V // VAL_WIDTH)),
                  pl.BlockSpec((tr, VAL_WIDTH), lambda i: (rev(i), COL_R // VAL_WIDTH)),
                  pl.BlockSpec((tr, FZ_PAD), lambda i: (rev(i), COL_FZ // FZ_PAD)),
                  pl.BlockSpec((tr, VAL_WIDTH), lambda i: (rev(i), 0)),
                  pl.BlockSpec((nc, HEAD_V, KEY_WIDTH), lambda i: (rev(i), 0, 0)),
                  pl.BlockSpec((1, HEAD_V, KEY_WIDTH), lambda i: (jnp.maximum(rev(i) * nc - 1, 0), 0, 0)),
                  pl.BlockSpec((FZ_PAD, KEY_WIDTH), lambda i: (0, 0)),
                  pl.BlockSpec((1, KEY_WIDTH), lambda i: (0, 0)),
                  pl.BlockSpec((1, HEAD_V), lambda i: (0, 0)), ANY],
        out_specs=(pl.BlockSpec((tr, 3 * VAL_WIDTH), lambda i: (rev(i), 0)),
                   pl.BlockSpec((tr, FZ_PAD), lambda i: (rev(i), 0)),
                   pl.BlockSpec((1, HEAD_V), lambda i: (0, 0)),
                   pl.BlockSpec((FZ_PAD, KEY_WIDTH), lambda i: (0, 0)),
                   pl.BlockSpec((1, KEY_WIDTH), lambda i: (0, 0))),
        scratch_shapes=[pltpu.VMEM((HEAD_V, KEY_WIDTH), F32),
                        pltpu.VMEM((tr, KEY_WIDTH), F32),
                        pltpu.VMEM((tr, KEY_WIDTH), F32),
                        pltpu.VMEM((tr, KEY_WIDTH), F32),
                        pltpu.VMEM((tr, VAL_WIDTH), F32),
                        pltpu.VMEM((tr, VAL_WIDTH), BF16)],
        input_output_aliases={11: 0},
        compiler_params=_cparams(("arbitrary",)),
    )(proj, proj, proj, proj, proj, doa, states, states, wfg2p, bfg, gn, dproj)


def _put_fz(dproj, dfz, *, s):
    tr = min(512, s)

    def body(dfz_ref, dproj_in_ref, o_ref):
        o_ref[...] = dfz_ref[...]

    return pl.pallas_call(
        body, name="put_dfz",
        out_shape=jax.ShapeDtypeStruct((s, PROJ_W), BF16),
        grid=(s // tr,),
        in_specs=[pl.BlockSpec((tr, FZ_PAD), lambda i: (i, 0)), ANY],
        out_specs=pl.BlockSpec((tr, FZ_PAD), lambda i: (i, COL_FZ // FZ_PAD)),
        input_output_aliases={1: 0},
        compiler_params=_cparams(("parallel",)),
    )(dfz, dproj)


_ADAM_C1 = 1.0 / (1.0 - ADAM_B1 ** ADAM_STEP)
_ADAM_C2 = 1.0 / (1.0 - ADAM_B2 ** ADAM_STEP)


def _adamw_math(wv, gv, mv, vv):
    nm = ADAM_B1 * mv + (1.0 - ADAM_B1) * gv
    nv = ADAM_B2 * vv + (1.0 - ADAM_B2) * (gv * gv)
    delta = -ADAM_LR * ((nm * _ADAM_C1) / (jnp.sqrt(nv * _ADAM_C2) + ADAM_EPS) + ADAM_WD * wv)
    return delta, nm, nv


def _adamw(w, g, m, v, *, name):
    shape = w.shape
    if w.ndim == 1:
        w, g, m, v = (t.reshape(1, 1, -1) for t in (w, g, m, v))
    elif w.ndim == 2:
        w, g, m, v = (t.reshape((1,) + t.shape) for t in (w, g, m, v))
    l, a, b = w.shape

    def body(w_ref, g_ref, m_ref, v_ref, d_ref, nm_ref, nv_ref):
        d_ref[...], nm_ref[...], nv_ref[...] = _adamw_math(w_ref[...], g_ref[...], m_ref[...], v_ref[...])

    blk = pl.BlockSpec((1, a, b), lambda li: (li, 0, 0))
    outs = pl.pallas_call(
        body, name=name,
        out_shape=tuple(jax.ShapeDtypeStruct((l, a, b), F32) for _ in range(3)),
        grid=(l,),
        in_specs=[blk, blk, blk, blk],
        out_specs=(blk, blk, blk),
        compiler_params=_cparams(("parallel",)),
    )(w, g, m, v)
    return tuple(o.reshape(shape) for o in outs)


def _adamw_layers(w, reduced, received, on_core0, m, v, *, name):
    l, a, b = w.shape
    assert l == DEPTH == 2
    ta = _row_tile(a, F32_SUBLANES, 384)

    def body(flag_ref, w_ref, r0_ref, o0_ref, r1_ref, o1_ref, m_ref, v_ref, g_ref, d_ref, nm_ref, nv_ref):
        core0 = flag_ref[...] > 0.5
        g0 = jnp.where(core0, r0_ref[...], o0_ref[...])
        g1 = jnp.where(core0, o1_ref[...], r1_ref[...])
        gv = jnp.where(pl.program_id(0) == 0, g0, g1)
        g_ref[0] = gv
        d_ref[0], nm_ref[0], nv_ref[0] = _adamw_math(w_ref[0], gv, m_ref[0], v_ref[0])

    blk = pl.BlockSpec((1, ta, b), lambda li, ai: (li, ai, 0))
    gblk = pl.BlockSpec((ta, b), lambda li, ai: (ai, 0))
    return pl.pallas_call(
        body, name=name,
        out_shape=tuple(jax.ShapeDtypeStruct((l, a, b), F32) for _ in range(4)),
        grid=(l, a // ta),
        in_specs=[pl.BlockSpec((1, 1), lambda li, ai: (0, 0)), blk, gblk, gblk, gblk, gblk, blk, blk],
        out_specs=(blk, blk, blk, blk),
        compiler_params=_cparams(("parallel", "parallel")),
    )(on_core0, w, reduced[0], received[0], reduced[1], received[1], m, v)


MESH_ID = pl.DeviceIdType.MESH
ANY = pl.BlockSpec(memory_space=pl.ANY)


def _position():
    x, y, c = lax.axis_index("x"), lax.axis_index("y"), lax.axis_index("c")
    chips = [(1 - x, y), (x, 1 - y), (1 - x, 1 - y)]
    return x, y, c, chips


def _chip_index(xy):
    return 2 * xy[0] + xy[1]


def _all_gather_weights(shards):
    n = len(shards)

    def body(*refs):
        ins, outs = refs[:n], refs[n:2 * n]
        send_sems, recv_sems, pass_send, pass_recv = refs[2 * n:]
        x, y, c, chips = _position()
        sibling = (x, y, 1 - c)

        def ici(a, k, chip_from):
            return pltpu.make_async_remote_copy(
                src_ref=ins[a].at[c], dst_ref=outs[a].at[c, _chip_index(chip_from)],
                send_sem=send_sems.at[a, k], recv_sem=recv_sems.at[a, k],
                device_id=(chips[k][0], chips[k][1], c), device_id_type=MESH_ID)

        def handoff(a, k, layer):
            slot = outs[a].at[layer, _chip_index(chips[k])]
            return pltpu.make_async_remote_copy(
                src_ref=slot, dst_ref=slot, send_sem=pass_send.at[a, k], recv_sem=pass_recv.at[a, k],
                device_id=sibling, device_id_type=MESH_ID)

        for a in range(n):
            for k in range(3):
                ici(a, k, (x, y)).start()
        for k in range(3):
            for a in range(n):
                ici(a, k, chips[k]).wait_recv()
                handoff(a, k, c).start()
        for k in range(3):
            for a in range(n):
                handoff(a, k, 1 - c).wait_recv()
        for a in range(n):
            for k in range(3):
                ici(a, k, (x, y)).wait_send()
                handoff(a, k, c).wait_send()

    gathered = pl.pallas_call(
        body, name="all_gather_weights",
        out_shape=tuple(jax.ShapeDtypeStruct((DEPTH, N_CHIPS) + t.shape[1:], t.dtype) for t in shards),
        in_specs=[ANY] * n, out_specs=tuple([ANY] * n),
        scratch_shapes=[pltpu.SemaphoreType.DMA((n, 3)), pltpu.SemaphoreType.DMA((n, 3)),
                        pltpu.SemaphoreType.DMA((n, 3)), pltpu.SemaphoreType.DMA((n, 3))],
    )(*shards)
    me = _chip_index((lax.axis_index("x"), lax.axis_index("y")))
    return [lax.dynamic_update_index_in_dim(g, t[:, None], me, axis=1) for g, t in zip(gathered, shards)]


HBM = pl.BlockSpec(memory_space=pltpu.HBM)
SEM = pl.BlockSpec(memory_space=pltpu.SEMAPHORE)
DATAFLOW_EFFECT = pltpu.SideEffectType.DATAFLOW_SIDE_EFFECTING
TOKEN_SHAPE = (8, LANE)


def _landing(shape, dtype):
    return pltpu.with_memory_space_constraint(lax.empty(shape, dtype), pltpu.HBM)


def _split_start(bufs, sem_shape, issue, *, name, after=None):
    n = len(bufs)
    n_in = n + (after is not None)

    def body(*refs):
        issue(refs[:n], refs[n_in], refs[n_in + 1])
        token = refs[-1]
        token[...] = jnp.zeros_like(token)

    operands = [pltpu.with_memory_space_constraint(t, pltpu.HBM) for t in bufs]
    outs = pl.pallas_call(
        body, name=name,
        out_shape=(pltpu.SemaphoreType.DMA(sem_shape), pltpu.SemaphoreType.DMA(sem_shape),
                   *[pltpu.HBM(t.shape, t.dtype) for t in bufs], jax.ShapeDtypeStruct(TOKEN_SHAPE, F32)),
        in_specs=[HBM] * n + [ANY] * (after is not None),
        out_specs=(SEM, SEM, *[HBM] * n, pl.BlockSpec(memory_space=pltpu.VMEM)),
        input_output_aliases={i: 2 + i for i in range(n)},
        compiler_params=pltpu.CompilerParams(has_side_effects=DATAFLOW_EFFECT),
    )(*operands, *([after] if after is not None else []))
    return outs[0], outs[1], list(outs[2:2 + n]), outs[-1]


def _split_wait(started, after, settle, *, name):
    send_sems, recv_sems, bufs, _ = started
    n = len(bufs)

    def body(*refs):
        settle(refs[:n], refs[n], refs[n + 1])

    outs = pl.pallas_call(
        body, name=name,
        out_shape=tuple(pltpu.HBM(t.shape, t.dtype) for t in bufs),
        in_specs=[HBM] * n + [SEM, SEM, ANY],
        out_specs=tuple([HBM] * n),
        input_output_aliases={i: i for i in range(n)},
        compiler_params=pltpu.CompilerParams(has_side_effects=DATAFLOW_EFFECT),
    )(*bufs, send_sems, recv_sems, after)
    return list(outs)


def _to_sibling(bufs, n, send_sems, recv_sems):
    x, y, c, _ = _position()
    return [pltpu.make_async_remote_copy(
        src_ref=bufs[a], dst_ref=bufs[n + a], send_sem=send_sems.at[a], recv_sem=recv_sems.at[a],
        device_id=(x, y, 1 - c), device_id_type=MESH_ID) for a in range(n)]


def _sibling_push_start(layer, sender_is_reducer, arrays, *, name):
    n = len(arrays)
    sender = layer if sender_is_reducer else 1 - layer

    def issue(bufs, send_sems, recv_sems):
        @pl.when(lax.axis_index("c") == sender)
        def _():
            for cp in _to_sibling(bufs, n, send_sems, recv_sems):
                cp.start()

    lands = [_landing(t.shape, t.dtype) for t in arrays]
    return _split_start(list(arrays) + lands, (n,), issue, name=name)


def _sibling_push_wait(layer, sender_is_reducer, started, after, *, name):
    n = len(started[2]) // 2
    sender = layer if sender_is_reducer else 1 - layer

    def settle(bufs, send_sems, recv_sems):
        c = lax.axis_index("c")

        @pl.when(c == sender)
        def _():
            for cp in _to_sibling(bufs, n, send_sems, recv_sems):
                cp.wait_send()

        @pl.when(c != sender)
        def _():
            for cp in _to_sibling(bufs, n, send_sems, recv_sems):
                cp.wait_recv()

    outs = _split_wait(started, after, settle, name=name)
    return outs[:n], outs[n:]


def _chip_copies(bufs, n, send_sems, recv_sems):
    x, y, c, chips = _position()
    return [pltpu.make_async_remote_copy(
        src_ref=bufs[a].at[_chip_index(chips[k])], dst_ref=bufs[n + a].at[k],
        send_sem=send_sems.at[3 * a + k], recv_sem=recv_sems.at[3 * a + k],
        device_id=(chips[k][0], chips[k][1], c), device_id_type=MESH_ID)
        for a in range(n) for k in range(3)]


def _chip_exchange_start(layer, parts, *, name):
    n = len(parts)

    def issue(bufs, send_sems, recv_sems):
        @pl.when(lax.axis_index("c") == layer)
        def _():
            for cp in _chip_copies(bufs, n, send_sems, recv_sems):
                cp.start()

    lands = [_landing((3,) + t.shape[1:], t.dtype) for t in parts]
    return _split_start(list(parts) + lands, (3 * n,), issue, name=name)


def _chip_exchange_wait(layer, started, after, *, name):
    n = len(started[2]) // 2

    def settle(bufs, send_sems, recv_sems):
        @pl.when(lax.axis_index("c") == layer)
        def _():
            for cp in _chip_copies(bufs, n, send_sems, recv_sems):
                cp.wait()

    outs = _split_wait(started, after, settle, name=name)
    return outs[:n], outs[n:]


def _gather_copies(layer, bufs, n, send_sems, recv_sems, arriving):
    x, y, c, chips = _position()
    me = _chip_index((x, y))
    return [pltpu.make_async_remote_copy(
        src_ref=bufs[a], dst_ref=bufs[n + a].at[_chip_index(chips[k]) if arriving else me],
        send_sem=send_sems.at[3 * a + k], recv_sem=recv_sems.at[3 * a + k],
        device_id=(chips[k][0], chips[k][1], c), device_id_type=MESH_ID)
        for a in range(n) for k in range(3)]


def _gather_start(layer, shards, *, name, after=None):
    n = len(shards)

    def issue(bufs, send_sems, recv_sems):
        @pl.when(lax.axis_index("c") == layer)
        def _():
            for cp in _gather_copies(layer, bufs, n, send_sems, recv_sems, False):
                cp.start()

    lands = [_landing((N_CHIPS,) + t.shape, t.dtype) for t in shards]
    return _split_start(list(shards) + lands, (3 * n,), issue, name=name, after=after)


def _gather_wait(layer, started, after, *, name):
    n = len(started[2]) // 2

    def settle(bufs, send_sems, recv_sems):
        @pl.when(lax.axis_index("c") == layer)
        def _():
            for cp in _gather_copies(layer, bufs, n, send_sems, recv_sems, False):
                cp.wait_send()
            for cp in _gather_copies(layer, bufs, n, send_sems, recv_sems, True):
                cp.wait_recv()

    return _split_wait(started, after, settle, name=name)[n:]


def _handoff_copies(bufs, n, send_sems, recv_sems):
    x, y, c, chips = _position()
    out = []
    for a in range(n):
        for k in range(3):
            slot = bufs[a].at[_chip_index(chips[k])]
            out.append(pltpu.make_async_remote_copy(
                src_ref=slot, dst_ref=slot, send_sem=send_sems.at[3 * a + k], recv_sem=recv_sems.at[3 * a + k],
                device_id=(x, y, 1 - c), device_id_type=MESH_ID))
    return out


def _handoff_start(layer, gathered, *, name):
    n = len(gathered)

    def issue(bufs, send_sems, recv_sems):
        @pl.when(lax.axis_index("c") == layer)
        def _():
            for cp in _handoff_copies(bufs, n, send_sems, recv_sems):
                cp.start()

    return _split_start(list(gathered), (3 * n,), issue, name=name)


def _handoff_wait(layer, started, after, *, name):
    n = len(started[2])

    def settle(bufs, send_sems, recv_sems):
        c = lax.axis_index("c")

        @pl.when(c == layer)
        def _():
            for cp in _handoff_copies(bufs, n, send_sems, recv_sems):
                cp.wait_send()

        @pl.when(c != layer)
        def _():
            for cp in _handoff_copies(bufs, n, send_sems, recv_sems):
                cp.wait_recv()

    return _split_wait(started, after, settle, name=name)


F32_SUBLANES = 8
BF16_SUBLANES = 16


def _row_tile(a, sublanes=BF16_SUBLANES, max_rows=704):
    best = None
    for cand in range(sublanes, min(a, max_rows) + 1, sublanes):
        if a % cand == 0:
            best = cand
    assert best is not None, a
    return best


def _pair_sum(mine, sib, *, name):
    nchip, a, b = sib.shape
    ta = _row_tile(a)

    def body(m_ref, s_ref, o_ref):
        o_ref[...] = (m_ref[...].astype(F32) + s_ref[...].astype(F32)).astype(BF16)

    blk = pl.BlockSpec((1, ta, b), lambda j, r: (j, r, 0))
    return pl.pallas_call(
        body, name=name,
        out_shape=jax.ShapeDtypeStruct((nchip, a, b), BF16),
        grid=(nchip, a // ta),
        in_specs=[blk, blk], out_specs=blk,
        compiler_params=_cparams(("parallel", "parallel")),
    )(mine, sib)


def _total_sum(own, recv, *, name):
    a, b = own.shape
    ta = _row_tile(a)

    def body(o_ref, r_ref, t_ref):
        acc = o_ref[...].astype(F32)
        for k in range(3):
            acc = acc + r_ref[k].astype(F32)
        t_ref[...] = acc

    blk = pl.BlockSpec((ta, b), lambda r: (r, 0))
    return pl.pallas_call(
        body, name=name,
        out_shape=jax.ShapeDtypeStruct((a, b), F32),
        grid=(a // ta,),
        in_specs=[blk, pl.BlockSpec((3, ta, b), lambda r: (0, r, 0))], out_specs=blk,
        compiler_params=_cparams(("parallel",)),
    )(own, recv)


def _all_reduce_small(packed):
    rows, width = packed.shape

    def body(x_ref, out_ref, gath, send_sems, recv_sems, local_sem):
        x, y, c, chips = _position()
        me, sibling = (x, y, c), (x, y, 1 - c)

        def slot(px, py, pc):
            return gath.at[4 * px + 2 * py + pc]

        def copy(k, block, to, src=None):
            return pltpu.make_async_remote_copy(
                src_ref=slot(*block) if src is None else src, dst_ref=slot(*block),
                send_sem=send_sems.at[k], recv_sem=recv_sems.at[k], device_id=to, device_id_type=MESH_ID)

        mine = pltpu.make_async_copy(x_ref, slot(*me), local_sem)
        mine.start()
        first = [copy(0, me, sibling, src=x_ref)]
        first += [copy(1 + j, me, (*chip, c), src=x_ref) for j, chip in enumerate(chips)]
        for cp in first:
            cp.start()
        passed = [copy(4 + j, (*chip, c), sibling) for j, chip in enumerate(chips)]
        for j, chip in enumerate(chips):
            copy(1 + j, (*chip, c), me).wait_recv()
            passed[j].start()
        copy(0, sibling, me).wait_recv()
        for j, chip in enumerate(chips):
            copy(4 + j, (*chip, 1 - c), me).wait_recv()
        for cp in first + passed:
            cp.wait_send()
        mine.wait()
        acc = gath[0]
        for d in range(1, N_DEV):
            acc = acc + gath[d]
        out_ref[...] = acc

    return pl.pallas_call(
        body, name="all_reduce_small",
        out_shape=jax.ShapeDtypeStruct((rows, width), F32),
        in_specs=[pl.BlockSpec(memory_space=pltpu.VMEM)],
        out_specs=pl.BlockSpec(memory_space=pltpu.VMEM),
        scratch_shapes=[pltpu.VMEM((N_DEV, rows, width), F32), pltpu.SemaphoreType.DMA((7,)),
                        pltpu.SemaphoreType.DMA((7,)), pltpu.SemaphoreType.DMA],
    )(packed)


def _mixer_forward_branches(x0, w, s, dep=None):
    h = _rms_fwd(x0, w["norm1_g"], name="rms1_fwd")
    proj = _matmul(h, w["w_in_t"], mode="nt", out_dtype=BF16, tm=1024, tn=1664, tk=1024, name="proj_fwd",
                   dep=dep)
    oa, states = _gla_fwd(proj, w["w_fg2"], w["b_fg"], w["gla_norm_g"], s=s)
    cb_in = _conv_fwd(proj, w["conv_wb"], s=s)
    return dict(x0=x0, h=h, proj=proj, oa=oa, states=states, cb_in=cb_in)


def _mixer_forward_out(sv, w, s, dep=None):
    ya = _matmul(sv["oa"], w["w_oa"], mode="nn", out_dtype=BF16, tm=1024, tn=1024, tk=1024, name="ya_fwd",
                 dep=dep)
    yb = _matmul(sv["cb_in"], w["w_ob"], mode="nn", out_dtype=BF16, tm=1024, tn=1024, tk=1024, name="yb_fwd")
    mix = _mix_fwd(sv["proj"], ya, yb, s=s)
    x1 = _matmul(mix, w["w_o"], mode="nn", out_dtype=F32, tm=1024, tn=1024, tk=1024, name="wo_fwd",
                 add=sv["x0"])
    return x1, dict(sv, ya=ya, yb=yb, mix=mix)


def _ffn_forward_hidden(x1, w, s, dep=None):
    h2 = _rms_fwd(x1, w["norm2_g"], name="rms2_fwd")
    gt = _matmul(h2, w["w_gate_t"], mode="nt", out_dtype=BF16, tm=1024, tn=1408, tk=1024, name="ffn_gate_fwd",
                 dep=dep)
    up = _matmul(h2, w["w_up_t"], mode="nt", out_dtype=BF16, tm=1024, tn=1408, tk=1024, name="ffn_up_fwd")
    return dict(x1=x1, h2=h2, gt=gt, up=up, hid=_swiglu_fwd(gt, up, s=s))


def _ffn_forward_out(sv, w, dep=None):
    return _matmul(sv["hid"], w["w_ffn_down"], mode="nn", out_dtype=F32, tm=1024, tn=1024, tk=FFN_HIDDEN,
                   name="ffn_out_fwd", add=sv["x1"], dep=dep)


def _ffn_backward_grads(dx2b, w, sv, s):
    g = {}
    dhid = _matmul(dx2b, w["w_ffn_down"], mode="nt", out_dtype=BF16, tm=1024, tn=FFN_HIDDEN, tk=1024,
                   name="ffn_out_bwd")
    g["w_ffn_down"] = _matmul(sv["hid"], dx2b, mode="tn", out_dtype=BF16, tm=1408, tn=1024, tk=1024,
                              name="ffn_out_wgrad")
    dgt, dup = _swiglu_bwd(sv["gt"], sv["up"], dhid, s=s)
    g["w_gate_t"] = _matmul(dgt, sv["h2"], mode="tn", out_dtype=BF16, tm=1408, tn=1024, tk=1024,
                            name="ffn_gate_wgrad")
    g["w_up_t"] = _matmul(dup, sv["h2"], mode="tn", out_dtype=BF16, tm=1408, tn=1024, tk=1024,
                          name="ffn_up_wgrad")
    return g, dgt, dup, dhid


def _ffn_backward_input(dgt, dup, dx2, w, sv, dep=None):
    dh2 = _matmul(dgt, w["w_gate_t"], mode="nn", out_dtype=F32, tm=1024, tn=1024, tk=FFN_HIDDEN,
                  name="ffn_gate_bwd", dep=dep)
    dh2 = _matmul(dup, w["w_up_t"], mode="nn", out_dtype=F32, tm=1024, tn=1024, tk=FFN_HIDDEN,
                  name="ffn_up_bwd", add=dh2)
    return _rms_bwd(sv["x1"], w["norm2_g"], dh2, dx2, name="rms2_bwd")


def _mixer_backward_branches(dx1b, w, sv, s, dep=None):
    g = {}
    dmix = _matmul(dx1b, w["w_o"], mode="nt", out_dtype=BF16, tm=1024, tn=1024, tk=1024, name="wo_bwd",
                   dep=dep)
    g["w_o"] = _matmul(sv["mix"], dx1b, mode="tn", out_dtype=BF16, tm=1024, tn=1024, tk=2048, name="wo_wgrad")
    dya, dyb, dproj = _mix_bwd(sv["proj"], sv["ya"], sv["yb"], dmix, s=s)
    dcb = _matmul(dyb, w["w_ob"], mode="nt", out_dtype=BF16, tm=1024, tn=1024, tk=1024, name="yb_bwd")
    g["w_ob"] = _matmul(sv["cb_in"], dyb, mode="tn", out_dtype=BF16, tm=1024, tn=1024, tk=2048, name="yb_wgrad")
    doa = _matmul(dya, w["w_oa"], mode="nt", out_dtype=BF16, tm=1024, tn=1024, tk=1024, name="ya_bwd")
    g["w_oa"] = _matmul(sv["oa"], dya, mode="tn", out_dtype=BF16, tm=1024, tn=1024, tk=2048, name="ya_wgrad")
    dproj, g["conv_wb"] = _conv_bwd(sv["proj"], dcb, w["conv_wb"], dproj, s=s)
    dproj, dfz, g["gla_norm_g"], g["w_fg2"], g["b_fg"] = _gla_bwd(
        sv["proj"], doa, sv["states"], w["w_fg2"], w["b_fg"], w["gla_norm_g"], dproj, s=s)
    return g, _put_fz(dproj, dfz, s=s)


def _proj_wgrad(dproj, sv, dep=None):
    return _matmul(dproj, sv["h"], mode="tn", out_dtype=BF16, tm=1664, tn=1024, tk=1024, name="proj_wgrad",
                   dep=dep)


def _proj_bwd(dproj, w, dep=None):
    return _matmul(dproj, w["w_in_t"], mode="nn", out_dtype=F32, tm=1024, tn=1024, tk=1664, name="proj_bwd",
                   dep=dep)


def _cols_from_chips(t):
    return jnp.transpose(t, (1, 0, 2)).reshape(t.shape[1], -1)


W_IN_ROWS = IN_WIDTH // N_CHIPS
W_IN_ROWS_PAD = -(-W_IN_ROWS // BF16_SUBLANES) * BF16_SUBLANES


def _w_in_t_shard(t):
    return jnp.pad(jnp.transpose(t, (0, 2, 1)), ((0, 0), (0, W_IN_ROWS_PAD - W_IN_ROWS), (0, 0)))


def _w_in_t_unshard(t):
    return jnp.transpose(t[:, :W_IN_ROWS], (0, 2, 1))


def _w_in_row_segments():
    runs = [(0, 0, FZ_ORIG), (FZ_ORIG, FZ_ORIG + GATE_RANK, COL_FZ - FZ_ORIG), (COL_FZ, FZ_ORIG, GATE_RANK)]
    out = []
    for kernel_row, ref_row, length in runs:
        while length:
            chip, local = divmod(ref_row, W_IN_ROWS)
            n = min(length, W_IN_ROWS - local)
            out.append((kernel_row, chip * W_IN_ROWS_PAD + local, n))
            kernel_row, ref_row, length = kernel_row + n, ref_row + n, length - n
    return out


def _permute_rows(src, n_out, segments, *, name):
    n_src, width = src.shape
    block = _row_tile(n_out)
    window = block + BF16_SUBLANES
    assert n_src >= window and src.dtype == BF16

    def body(src_ref, out_ref):
        rows = lax.broadcasted_iota(jnp.int32, (block, window), 0)
        cols = lax.broadcasted_iota(jnp.int32, (block, window), 1)
        for b0 in range(0, n_out, block):
            acc = None
            for o, s, n in segments:
                lo, hi = max(o, b0), min(o + n, b0 + block)
                if lo >= hi:
                    continue
                first = s + (lo - o)
                base = max(0, min(first // BF16_SUBLANES * BF16_SUBLANES, n_src - window))
                shift = (first - base) - (lo - b0)
                pick = (cols == rows + shift) & (rows >= lo - b0) & (rows < hi - b0)
                part = jnp.dot(jnp.where(pick, 1.0, 0.0).astype(BF16), src_ref[base:base + window, :],
                               preferred_element_type=F32)
                acc = part if acc is None else acc + part
            out_ref[b0:b0 + block, :] = (jnp.zeros((block, width), F32) if acc is None else acc).astype(BF16)

    return pl.pallas_call(
        body, name=name,
        out_shape=jax.ShapeDtypeStruct((n_out, width), BF16),
        in_specs=[pl.BlockSpec(memory_space=pltpu.VMEM)],
        out_specs=pl.BlockSpec(memory_space=pltpu.VMEM),
        compiler_params=_cparams(),
    )(src)


def _w_in_t_to_kernel(t):
    return _permute_rows(t.reshape(N_CHIPS * W_IN_ROWS_PAD, D_MODEL), PROJ_W, _w_in_row_segments(),
                         name="w_in_to_kernel_rows")


def _w_in_t_from_kernel(g):
    segments = [(slab, kernel_row, n) for kernel_row, slab, n in _w_in_row_segments()]
    out = _permute_rows(g, N_CHIPS * W_IN_ROWS_PAD, segments, name="w_in_to_chip_rows")
    return out.reshape(N_CHIPS, W_IN_ROWS_PAD, D_MODEL)


def kernel(x, norm1_g, w_in, w_fg2, b_fg, gla_norm_g, w_oa, conv_w, conv_b, w_ob, w_o, norm2_g, w_ffn_gate, w_ffn_up, w_ffn_down, final_g, loss_target, m_norm1_g, m_w_in, m_w_fg2, m_b_fg, m_gla_norm_g, m_w_oa, m_conv_w, m_conv_b, m_w_ob, m_w_o, m_norm2_g, m_w_ffn_gate, m_w_ffn_up, m_w_ffn_down, m_final_g, v_norm1_g, v_w_in, v_w_fg2, v_b_fg, v_gla_norm_g, v_w_oa, v_conv_w, v_conv_b, v_w_ob, v_w_o, v_norm2_g, v_w_ffn_gate, v_w_ffn_up, v_w_ffn_down, v_final_g):
    cx_ = lax.axis_index("x")
    cy_ = lax.axis_index("y")
    cc_ = lax.axis_index("c")
    me = 2 * cx_ + cy_
    on_core0 = jnp.where(cc_ == 0, 1.0, 0.0).astype(F32).reshape(1, 1)

    def swap(t):
        return jnp.swapaxes(t, 1, 2)

    big_names = ["w_in", "w_oa", "w_ob", "w_o", "w_ffn_gate", "w_ffn_up", "w_ffn_down"]
    views = dict(
        w_in=tuple(_w_in_t_shard(t) for t in (w_in, m_w_in, v_w_in)),
        w_oa=(w_oa, m_w_oa, v_w_oa), w_ob=(w_ob, m_w_ob, v_w_ob), w_o=(w_o, m_w_o, v_w_o),
        w_ffn_gate=tuple(swap(t) for t in (w_ffn_gate, m_w_ffn_gate, v_w_ffn_gate)),
        w_ffn_up=tuple(swap(t) for t in (w_ffn_up, m_w_ffn_up, v_w_ffn_up)),
        w_ffn_down=(w_ffn_down, m_w_ffn_down, v_w_ffn_down))
    from_view = dict(w_in=_w_in_t_unshard, w_ffn_gate=swap, w_ffn_up=swap)

    s = x.shape[1]

    conv_w_p = jnp.pad(conv_w, ((0, 0), (0, 8 - conv_w.shape[1]), (0, 0)))
    w_fg2_all, conv_w_all = _all_gather_weights([w_fg2, conv_w_p])

    proj_names = ["w_in"]
    rest_names = ["w_oa", "w_ob", "w_o", "w_ffn_gate", "w_ffn_up", "w_ffn_down"]
    mixer_names = ["w_oa", "w_ob", "w_o"]
    ffn_names = ["w_ffn_gate", "w_ffn_up", "w_ffn_down"]
    weight_key = dict(w_in="w_in_t", w_ffn_gate="w_gate_t", w_ffn_up="w_up_t")
    full_shape = dict(w_oa=(VAL_WIDTH, D_MODEL), w_ob=(CONV_CH, D_MODEL), w_o=(D_MODEL, D_MODEL),
                      w_ffn_gate=(FFN_HIDDEN, D_MODEL), w_ffn_up=(FFN_HIDDEN, D_MODEL),
                      w_ffn_down=(FFN_HIDDEN, D_MODEL))

    def shards_of(l, names):
        return [views[n][0][l].astype(BF16) for n in names]

    def small_weights(l):
        w_fg2_full = _cols_from_chips(w_fg2_all[l])
        conv_w_full = _cols_from_chips(conv_w_all[l])[:3]
        return dict(
            norm1_g=norm1_g[l], norm2_g=norm2_g[l],
            w_fg2=jnp.pad(w_fg2_full, ((0, FZ_PAD - GATE_RANK), (0, 0))).astype(BF16),
            b_fg=b_fg[l].reshape(1, KEY_WIDTH), gla_norm_g=gla_norm_g[l].reshape(1, HEAD_V),
            conv_wb=(jnp.pad(conv_w_full, ((0, 5), (0, 0)))
                     + jnp.pad(conv_b[l].reshape(1, CONV_CH), ((3, 4), (0, 0)))))

    def full_weights(names, gathered, shards):
        out = {}
        for n, g, t in zip(names, gathered, shards):
            g = lax.dynamic_update_index_in_dim(g, t[None], me, axis=0)
            out[weight_key.get(n, n)] = _w_in_t_to_kernel(g) if n == "w_in" else g.reshape(full_shape[n])
        return out

    def per_chip(g, names):
        out = []
        for n in names:
            t = g[weight_key.get(n, n)]
            out.append(_w_in_t_from_kernel(t) if n == "w_in"
                       else t.reshape(N_CHIPS, t.shape[0] // N_CHIPS, t.shape[1]))
        return out

    def gather(l, names, tag, after=None):
        shards = shards_of(l, names)
        return shards, _gather_start(l, shards, name=f"gather_{tag}_start", after=after)

    def gathered_to_sibling(l, started, after, tag):
        return _handoff_start(l, _gather_wait(l, started, after, name=f"gather_{tag}_wait"),
                              name=f"handoff_{tag}_start")

    def feed(l, g, names, tag):
        return _sibling_push_start(l, False, per_chip(g, names), name=f"feed_{tag}_start")

    def pair_and_exchange(l, fed, after, names, tag):
        mine, sib = _sibling_push_wait(l, False, fed, after, name=f"feed_{tag}_wait")
        parts = [_pair_sum(a, b, name=f"pair_sum_{tag}_{n}") for n, a, b in zip(names, mine, sib)]
        return _chip_exchange_start(l, parts, name=f"exchange_{tag}_start")

    def total_and_share(l, swapped, after, names, tag):
        parts, recvs = _chip_exchange_wait(l, swapped, after, name=f"exchange_{tag}_wait")
        owns = [lax.dynamic_index_in_dim(p, me, 0, keepdims=False) for p in parts]
        totals = [_total_sum(o, rc, name=f"total_{tag}_{n}") for n, o, rc in zip(names, owns, recvs)]
        return _sibling_push_start(l, True, totals, name=f"share_{tag}_start")

    def shared(l, sharing, after, names, tag):
        totals, others = _sibling_push_wait(l, True, sharing, after, name=f"share_{tag}_wait")
        return {n: (t, o) for n, t, o in zip(names, totals, others)}

    shards_p, started = gather(0, proj_names, "p0")
    started = gathered_to_sibling(0, started, started[3], "p0")
    w0 = small_weights(0)
    w0.update(full_weights(proj_names, _handoff_wait(0, started, started[3], name="handoff_p0_wait"), shards_p))
    shards_r, started = gather(0, rest_names, "r0", after=w0["w_in_t"])
    sv0 = _mixer_forward_branches(x[0], w0, s, dep=started[3])
    started = gathered_to_sibling(0, started, sv0["oa"], "r0")
    w0.update(full_weights(rest_names, _handoff_wait(0, started, started[3], name="handoff_r0_wait"), shards_r))

    shards_1, started = gather(1, big_names, "l1", after=w0["w_o"])
    x1, sv0m = _mixer_forward_out(sv0, w0, s, dep=started[3])
    sv0f = _ffn_forward_hidden(x1, w0, s)
    started = gathered_to_sibling(1, started, sv0f["hid"], "l1")
    x2 = _ffn_forward_out(sv0f, w0, dep=started[3])
    w1 = small_weights(1)
    w1.update(full_weights(big_names, _handoff_wait(1, started, x2, name="handoff_l1_wait"), shards_1))

    x3, sv1m = _mixer_forward_out(_mixer_forward_branches(x2, w1, s), w1, s)
    sv1f = _ffn_forward_hidden(x3, w1, s)
    x4 = _ffn_forward_out(sv1f, w1)
    loss_local, dx, dxb, dgf = _loss_head(x4, final_g, loss_target[0])
    loss = lax.psum(loss_local[0, 0], ("x", "y", "c"))

    g1, dgt, dup, _ = _ffn_backward_grads(dxb, w1, sv1f, s)
    dx_mid, dxb_mid, g1["norm2_g"] = _ffn_backward_input(dgt, dup, dx, w1, sv1f)
    gm, dproj = _mixer_backward_branches(dxb_mid, w1, sv1m, s)
    g1.update(gm)
    g1["w_in_t"] = _proj_wgrad(dproj, sv1m)
    fed_1 = feed(1, g1, big_names, "l1")
    dh = _proj_bwd(dproj, w1, dep=fed_1[3])
    dx, dxb, g1["norm1_g"] = _rms_bwd(sv1m["x0"], norm1_g[1], dh, dx_mid, name="rms1_bwd")

    g0, dgt, dup, dhid = _ffn_backward_grads(dxb, w0, sv0f, s)
    swap_1 = pair_and_exchange(1, fed_1, dhid, big_names, "l1")
    fed_f = feed(0, g0, ffn_names, "f0")
    dx_mid, dxb_mid, g0["norm2_g"] = _ffn_backward_input(dgt, dup, dx, w0, sv0f, dep=swap_1[3] + fed_f[3])
    swap_f = pair_and_exchange(0, fed_f, dx_mid, ffn_names, "f0")
    gm, dproj = _mixer_backward_branches(dxb_mid, w0, sv0m, s, dep=swap_f[3])
    g0.update(gm)
    share_1 = total_and_share(1, swap_1, dproj, big_names, "l1")
    share_f = total_and_share(0, swap_f, dproj, ffn_names, "f0")
    fed_m = feed(0, g0, mixer_names, "m0")
    g0["w_in_t"] = _proj_wgrad(dproj, sv0m, dep=share_1[3] + share_f[3] + fed_m[3])
    fed_p = feed(0, g0, proj_names, "p0")
    swap_m = pair_and_exchange(0, fed_m, g0["w_in_t"], mixer_names, "m0")
    dh = _proj_bwd(dproj, w0, dep=fed_p[3] + swap_m[3])
    reduced1 = shared(1, share_1, dh, big_names, "l1")
    reduced0 = shared(0, share_f, dh, ffn_names, "f0")
    grad_x, _, g0["norm1_g"] = _rms_bwd(sv0m["x0"], norm1_g[0], dh, dx_mid, name="rms1_bwd")

    swap_p = pair_and_exchange(0, fed_p, grad_x, proj_names, "p0")
    share_m = total_and_share(0, swap_m, swap_p[3], mixer_names, "m0")
    share_p = total_and_share(0, swap_p, share_m[3], proj_names, "p0")
    reduced0.update(shared(0, share_m, share_p[3], mixer_names, "m0"))
    reduced0.update(shared(0, share_p, share_p[3], proj_names, "p0"))
    grads = [g0, g1]

    def small_rows(t):
        return t.reshape(-1, D_MODEL)

    def tile_rows(t):
        return jnp.pad(t, ((0, -t.shape[0] % F32_SUBLANES), (0, 0)))

    g0, g1 = grads
    pieces = [
        jnp.concatenate([g0["norm1_g"], g1["norm1_g"]], axis=0),
        jnp.concatenate([g0["norm2_g"], g1["norm2_g"]], axis=0),
        dgf,
        small_rows(jnp.concatenate([g0["b_fg"], g1["b_fg"]], axis=1)),
        small_rows(jnp.concatenate([g0["gla_norm_g"], g1["gla_norm_g"],
                                    jnp.zeros((1, D_MODEL - 2 * HEAD_V), F32)], axis=1)),
        jnp.concatenate([g0["conv_wb"][3:4], g1["conv_wb"][3:4]], axis=0),
        jnp.concatenate([g0["conv_wb"][:3], g1["conv_wb"][:3]], axis=0),
        small_rows(jnp.stack([g0["w_fg2"][:GATE_RANK], g1["w_fg2"][:GATE_RANK]])),
    ]
    small = _all_reduce_small(jnp.concatenate([tile_rows(p) for p in pieces], axis=0))
    sg = dict(
        norm1_g=small[0:2], norm2_g=small[8:10], final_g=small[16],
        b_fg=small[24].reshape(DEPTH, KEY_WIDTH), gla_norm_g=small[32, :DEPTH * HEAD_V].reshape(DEPTH, HEAD_V),
        conv_b=small[40:42],
        conv_w=lax.dynamic_slice_in_dim(small[48:54].reshape(DEPTH, 3, CONV_CH), me * (CONV_CH // N_CHIPS),
                                        CONV_CH // N_CHIPS, axis=2),
        w_fg2=lax.dynamic_slice_in_dim(small[56:72].reshape(DEPTH, GATE_RANK, KEY_WIDTH),
                                       me * (KEY_WIDTH // N_CHIPS), KEY_WIDTH // N_CHIPS, axis=2),
    )

    small_params = dict(norm1_g=(norm1_g, m_norm1_g, v_norm1_g), w_fg2=(w_fg2, m_w_fg2, v_w_fg2),
                        b_fg=(b_fg, m_b_fg, v_b_fg), gla_norm_g=(gla_norm_g, m_gla_norm_g, v_gla_norm_g),
                        conv_w=(conv_w, m_conv_w, v_conv_w), conv_b=(conv_b, m_conv_b, v_conv_b),
                        norm2_g=(norm2_g, m_norm2_g, v_norm2_g), final_g=(final_g, m_final_g, v_final_g))
    order = ["norm1_g", "w_in", "w_fg2", "b_fg", "gla_norm_g", "w_oa", "conv_w", "conv_b", "w_ob", "w_o",
             "norm2_g", "w_ffn_gate", "w_ffn_up", "w_ffn_down", "final_g"]
    results = {}
    for i, n in enumerate(big_names):
        w_, m_, v_ = views[n]
        outs = _adamw_layers(w_, (reduced0[n][0], reduced1[n][0]), (reduced0[n][1], reduced1[n][1]), on_core0,
                             m_, v_, name="adamw_" + n)
        back = from_view.get(n)
        results[n] = tuple(back(r) for r in outs) if back else outs
    for n, (w_, m_, v_) in small_params.items():
        results[n] = (sg[n],) + _adamw(w_, sg[n], m_, v_, name="adamw_" + n)
    return (loss, grad_x[None], *[results[n][0] for n in order], *[results[n][1] for n in order],
            *[results[n][2] for n in order], *[results[n][3] for n in order])
```

```python
import functools

import jax
import jax.numpy as jnp
from jax import lax
from jax.experimental import pallas as pl
from jax.experimental.pallas import tpu as pltpu

F32 = jnp.float32
BF16 = jnp.bfloat16

D_MODEL = 1024
DEPTH = 2
CHUNK = 64
GLA_HEADS = 4
KEY_WIDTH = 512
VAL_WIDTH = 1024
HEAD_K = 128
HEAD_V = 256
GATE_RANK = 16
GATE_TAU = 16.0
CONV_CH = 1024
FFN_HIDDEN = 2816
IN_WIDTH = 8208
NORM_EPS = 1e-6
Q_SCALE = HEAD_K ** -0.5
ADAM_LR = 0.001
ADAM_B1 = 0.9
ADAM_B2 = 0.999
ADAM_EPS = 1e-08
ADAM_WD = 0.01
ADAM_STEP = 10

N_CHIPS = 4
N_DEV = 8

LANE = 128
FZ_PAD = LANE
PROJ_W = 8192 + FZ_PAD
COL_Q, COL_K, COL_V, COL_R, COL_GBI, COL_GCI, COL_CX, COL_GA, COL_GB, COL_FZ = (
    0, 512, 1024, 2048, 3072, 4096, 5120, 6144, 7168, 8192)
FZ_ORIG = 3072

VMEM_LIMIT = 52 * 1024 * 1024
HALO = 16


def _cparams(sem=None):
    return pltpu.CompilerParams(dimension_semantics=sem, vmem_limit_bytes=VMEM_LIMIT)


def _sigmoid(x):
    return jax.nn.sigmoid(x)


def _log_sigmoid(x):
    return jnp.minimum(x, 0.0) - jnp.log1p(jnp.exp(-jnp.abs(x)))


_DOT_DIMS = {
    "nn": (((1,), (0,)), ((), ())),
    "nt": (((1,), (1,)), ((), ())),
    "tn": (((0,), (0,)), ((), ())),
}


def _matmul(a, b, *, mode, out_dtype, tm, tn, tk, name, add=None, dep=None):
    if mode == "nn":
        (m, k), n = a.shape, b.shape[1]
    elif mode == "nt":
        (m, k), n = a.shape, b.shape[0]
    else:
        (k, m), n = a.shape, b.shape[1]
    tm, tn, tk = min(tm, m), min(tn, n), min(tk, k)
    assert m % tm == 0 and n % tn == 0 and k % tk == 0, (name, m, n, k, tm, tn, tk)
    nk = k // tk
    has_add = add is not None
    has_dep = dep is not None

    def body(*refs):
        if has_dep:
            refs = refs[1:]
        if has_add:
            a_ref, b_ref, add_ref, o_ref = refs[:4]
            scratch = refs[4:]
        else:
            a_ref, b_ref, o_ref = refs[:3]
            add_ref = None
            scratch = refs[3:]
        part = lax.dot_general(a_ref[...], b_ref[...], _DOT_DIMS[mode], preferred_element_type=F32)

        def finish(acc):
            if add_ref is not None:
                acc = acc + add_ref[...].astype(F32)
            o_ref[...] = acc.astype(o_ref.dtype)

        if nk == 1:
            finish(part)
        else:
            acc_ref = scratch[0]
            kk = pl.program_id(2)

            @pl.when(kk == 0)
            def _():
                acc_ref[...] = part

            @pl.when(kk > 0)
            def _():
                acc_ref[...] += part

            @pl.when(kk == nk - 1)
            def _():
                finish(acc_ref[...])

    if mode == "nn":
        a_spec = pl.BlockSpec((tm, tk), lambda i, j, kk: (i, kk))
        b_spec = pl.BlockSpec((tk, tn), lambda i, j, kk: (kk, j))
    elif mode == "nt":
        a_spec = pl.BlockSpec((tm, tk), lambda i, j, kk: (i, kk))
        b_spec = pl.BlockSpec((tn, tk), lambda i, j, kk: (j, kk))
    else:
        a_spec = pl.BlockSpec((tk, tm), lambda i, j, kk: (kk, i))
        b_spec = pl.BlockSpec((tk, tn), lambda i, j, kk: (kk, j))
    o_spec = pl.BlockSpec((tm, tn), lambda i, j, kk: (i, j))
    in_specs = [a_spec, b_spec] + ([o_spec] if has_add else [])
    operands = (a, b) + ((add,) if has_add else ())
    if has_dep:
        in_specs = [pl.BlockSpec(dep.shape, lambda i, j, kk: (0, 0))] + in_specs
        operands = (dep,) + operands
    return pl.pallas_call(
        body,
        name=name,
        out_shape=jax.ShapeDtypeStruct((m, n), out_dtype),
        grid=(m // tm, n // tn, nk),
        in_specs=in_specs,
        out_specs=o_spec,
        scratch_shapes=[pltpu.VMEM((tm, tn), F32)] if nk > 1 else [],
        compiler_params=_cparams(("parallel", "parallel", "arbitrary")),
    )(*operands)


def _rms_fwd(x, g, *, name):
    s, d = x.shape
    tr = min(512, s)

    def body(x_ref, g_ref, h_ref):
        xv = x_ref[...]
        rs = lax.rsqrt(jnp.mean(xv * xv, axis=-1, keepdims=True) + NORM_EPS)
        h_ref[...] = (xv * rs * g_ref[...]).astype(BF16)

    return pl.pallas_call(
        body, name=name,
        out_shape=jax.ShapeDtypeStruct((s, d), BF16),
        grid=(s // tr,),
        in_specs=[pl.BlockSpec((tr, d), lambda i: (i, 0)), pl.BlockSpec((1, d), lambda i: (0, 0))],
        out_specs=pl.BlockSpec((tr, d), lambda i: (i, 0)),
        compiler_params=_cparams(("parallel",)),
    )(x, g.reshape(1, d))


def _rms_bwd(x, g, dh, dres, *, name):
    s, d = x.shape
    tr = min(512, s)

    def body(x_ref, g_ref, dh_ref, dres_ref, dx_ref, dxb_ref, dg_ref):
        i = pl.program_id(0)
        xv = x_ref[...]
        rs = lax.rsqrt(jnp.mean(xv * xv, axis=-1, keepdims=True) + NORM_EPS)
        n = xv * rs
        dhv = dh_ref[...].astype(F32)
        dn = dhv * g_ref[...]
        dx = dres_ref[...] + rs * (dn - n * jnp.mean(dn * n, axis=-1, keepdims=True))
        dx_ref[...] = dx
        dxb_ref[...] = dx.astype(BF16)
        part = jnp.sum(dhv * n, axis=0, keepdims=True)

        @pl.when(i == 0)
        def _():
            dg_ref[...] = part

        @pl.when(i > 0)
        def _():
            dg_ref[...] += part

    row = pl.BlockSpec((tr, d), lambda i: (i, 0))
    vec = pl.BlockSpec((1, d), lambda i: (0, 0))
    return pl.pallas_call(
        body, name=name,
        out_shape=(jax.ShapeDtypeStruct((s, d), F32), jax.ShapeDtypeStruct((s, d), BF16),
                   jax.ShapeDtypeStruct((1, d), F32)),
        grid=(s // tr,),
        in_specs=[row, vec, row, row],
        out_specs=(row, row, vec),
        compiler_params=_cparams(("arbitrary",)),
    )(x, g.reshape(1, d), dh, dres)


def _loss_head(x, g, target):
    s, d = x.shape
    tr = min(512, s)

    def body(x_ref, g_ref, t_ref, loss_ref, dx_ref, dxb_ref, dg_ref):
        i = pl.program_id(0)
        xv = x_ref[...]
        rs = lax.rsqrt(jnp.mean(xv * xv, axis=-1, keepdims=True) + NORM_EPS)
        n = xv * rs
        gv = g_ref[...]
        err = n * gv - t_ref[...]
        row_loss = jnp.mean(err * err, axis=-1, keepdims=True)
        loss_part = 0.5 * jnp.sum(row_loss, axis=0, keepdims=True)
        dy = err * (1.0 / d)
        dn = dy * gv
        dx = rs * (dn - n * jnp.mean(dn * n, axis=-1, keepdims=True))
        dx_ref[...] = dx
        dxb_ref[...] = dx.astype(BF16)
        dg_part = jnp.sum(dy * n, axis=0, keepdims=True)

        @pl.when(i == 0)
        def _():
            loss_ref[...] = loss_part
            dg_ref[...] = dg_part

        @pl.when(i > 0)
        def _():
            loss_ref[...] += loss_part
            dg_ref[...] += dg_part

    row = pl.BlockSpec((tr, d), lambda i: (i, 0))
    vec = pl.BlockSpec((1, d), lambda i: (0, 0))
    one = pl.BlockSpec((1, 1), lambda i: (0, 0))
    return pl.pallas_call(
        body, name="loss_head",
        out_shape=(jax.ShapeDtypeStruct((1, 1), F32), jax.ShapeDtypeStruct((s, d), F32),
                   jax.ShapeDtypeStruct((s, d), BF16), jax.ShapeDtypeStruct((1, d), F32)),
        grid=(s // tr,),
        in_specs=[row, vec, row],
        out_specs=(one, row, row, vec),
        compiler_params=_cparams(("arbitrary",)),
    )(x, g.reshape(1, d), target)


def _conv_taps(u_prev, u, w_ref, rows):
    ext = jnp.concatenate([u_prev, u], axis=0)
    u1 = pltpu.roll(ext, 1, 0)[HALO:HALO + rows]
    u2 = pltpu.roll(ext, 2, 0)[HALO:HALO + rows]
    conv = w_ref[0:1, :] * u2 + w_ref[1:2, :] * u1 + w_ref[2:3, :] * u + w_ref[3:4, :]
    return conv, u1, u2


def _conv_fwd(proj, conv_wb, *, s):
    tr = min(512, s)
    c = CONV_CH
    hb = tr // HALO

    def body(gbi_ref, gci_ref, cx_ref, gci_h_ref, cx_h_ref, w_ref, o_ref):
        i = pl.program_id(0)
        u = gci_ref[...].astype(F32) * cx_ref[...].astype(F32)
        u_prev = gci_h_ref[...].astype(F32) * cx_h_ref[...].astype(F32)
        u_prev = jnp.where(i == 0, 0.0, u_prev)
        conv, _, _ = _conv_taps(u_prev, u, w_ref, tr)
        o_ref[...] = (gbi_ref[...].astype(F32) * conv).astype(BF16)

    def seg(col):
        return pl.BlockSpec((tr, c), lambda i: (i, col // c))

    def halo(col):
        return pl.BlockSpec((HALO, c), lambda i: (jnp.maximum(i * hb - 1, 0), col // c))

    return pl.pallas_call(
        body, name="conv_fwd",
        out_shape=jax.ShapeDtypeStruct((s, c), BF16),
        grid=(s // tr,),
        in_specs=[seg(COL_GBI), seg(COL_GCI), seg(COL_CX), halo(COL_GCI), halo(COL_CX),
                  pl.BlockSpec((8, c), lambda i: (0, 0))],
        out_specs=pl.BlockSpec((tr, c), lambda i: (i, 0)),
        compiler_params=_cparams(("parallel",)),
    )(proj, proj, proj, proj, proj, conv_wb)


def _conv_bwd(proj, dcb, conv_wb, dproj, *, s):
    tr = min(512, s)
    c = CONV_CH
    hb = tr // HALO
    nb = s // tr

    def body(gbi_ref, gci_ref, cx_ref, dcb_ref, gci_h_ref, cx_h_ref, gbi_n_ref, dcb_n_ref, w_ref,
             dproj_in_ref, dp_ref, dw_ref):
        i = pl.program_id(0)
        gbi = gbi_ref[...].astype(F32)
        gci = gci_ref[...].astype(F32)
        cx = cx_ref[...].astype(F32)
        dcb_v = dcb_ref[...].astype(F32)
        u = gci * cx
        u_prev = jnp.where(i == 0, 0.0, gci_h_ref[...].astype(F32) * cx_h_ref[...].astype(F32))
        conv, u1, u2 = _conv_taps(u_prev, u, w_ref, tr)
        dconv = dcb_v * gbi
        dconv_next = jnp.where(i == nb - 1, 0.0, dcb_n_ref[...].astype(F32) * gbi_n_ref[...].astype(F32))
        ext = jnp.concatenate([dconv, dconv_next], axis=0)
        n_ext = tr + HALO
        d1 = pltpu.roll(ext, n_ext - 1, 0)[0:tr]
        d2 = pltpu.roll(ext, n_ext - 2, 0)[0:tr]
        du = w_ref[2:3, :] * dconv + w_ref[1:2, :] * d1 + w_ref[0:1, :] * d2
        dp_ref[:, 0:c] = (dcb_v * conv).astype(BF16)
        dp_ref[:, c:2 * c] = (du * cx).astype(BF16)
        dp_ref[:, 2 * c:3 * c] = (du * gci).astype(BF16)
        part = jnp.concatenate([
            jnp.sum(dconv * u2, axis=0, keepdims=True),
            jnp.sum(dconv * u1, axis=0, keepdims=True),
            jnp.sum(dconv * u, axis=0, keepdims=True),
            jnp.sum(dconv, axis=0, keepdims=True),
            jnp.zeros((4, c), F32)], axis=0)

        @pl.when(i == 0)
        def _():
            dw_ref[...] = part

        @pl.when(i > 0)
        def _():
            dw_ref[...] += part

    def seg(col):
        return pl.BlockSpec((tr, c), lambda i: (i, col // c))

    def halo_prev(col):
        return pl.BlockSpec((HALO, c), lambda i: (jnp.maximum(i * hb - 1, 0), col // c))

    def halo_next(col):
        return pl.BlockSpec((HALO, c), lambda i: (jnp.minimum((i + 1) * hb, nb * hb - 1), col // c))

    return pl.pallas_call(
        body, name="conv_bwd",
        out_shape=(jax.ShapeDtypeStruct((s, PROJ_W), BF16), jax.ShapeDtypeStruct((8, c), F32)),
        grid=(nb,),
        in_specs=[seg(COL_GBI), seg(COL_GCI), seg(COL_CX), pl.BlockSpec((tr, c), lambda i: (i, 0)),
                  halo_prev(COL_GCI), halo_prev(COL_CX), halo_next(COL_GBI),
                  pl.BlockSpec((HALO, c), lambda i: (jnp.minimum((i + 1) * hb, nb * hb - 1), 0)),
                  pl.BlockSpec((8, c), lambda i: (0, 0)), ANY],
        out_specs=(pl.BlockSpec((tr, 3 * c), lambda i: (i, COL_GBI // (3 * c))),
                   pl.BlockSpec((8, c), lambda i: (0, 0))),
        input_output_aliases={9: 0},
        compiler_params=_cparams(("arbitrary",)),
    )(proj, proj, proj, dcb, proj, proj, proj, dcb, conv_wb, dproj)


def _mix_fwd(proj, ya, yb, *, s):
    tr = min(512, s)
    d = D_MODEL

    def body(ga_ref, gb_ref, ya_ref, yb_ref, o_ref):
        sa = _sigmoid(ga_ref[...].astype(F32))
        sb = _sigmoid(gb_ref[...].astype(F32))
        o_ref[...] = (sa * ya_ref[...].astype(F32) + sb * yb_ref[...].astype(F32)).astype(BF16)

    row = pl.BlockSpec((tr, d), lambda i: (i, 0))
    return pl.pallas_call(
        body, name="mix_fwd",
        out_shape=jax.ShapeDtypeStruct((s, d), BF16),
        grid=(s // tr,),
        in_specs=[pl.BlockSpec((tr, d), lambda i: (i, COL_GA // d)),
                  pl.BlockSpec((tr, d), lambda i: (i, COL_GB // d)), row, row],
        out_specs=row,
        compiler_params=_cparams(("parallel",)),
    )(proj, proj, ya, yb)


def _mix_bwd(proj, ya, yb, dmix, *, s):
    tr = min(512, s)
    d = D_MODEL

    def body(ga_ref, gb_ref, ya_ref, yb_ref, dm_ref, dya_ref, dyb_ref, dg_ref):
        sa = _sigmoid(ga_ref[...].astype(F32))
        sb = _sigmoid(gb_ref[...].astype(F32))
        dm = dm_ref[...].astype(F32)
        dya_ref[...] = (dm * sa).astype(BF16)
        dyb_ref[...] = (dm * sb).astype(BF16)
        dg_ref[:, 0:d] = (dm * ya_ref[...].astype(F32) * sa * (1.0 - sa)).astype(BF16)
        dg_ref[:, d:2 * d] = (dm * yb_ref[...].astype(F32) * sb * (1.0 - sb)).astype(BF16)

    row = pl.BlockSpec((tr, d), lambda i: (i, 0))
    return pl.pallas_call(
        body, name="mix_bwd",
        out_shape=(jax.ShapeDtypeStruct((s, d), BF16), jax.ShapeDtypeStruct((s, d), BF16),
                   jax.ShapeDtypeStruct((s, PROJ_W), BF16)),
        grid=(s // tr,),
        in_specs=[pl.BlockSpec((tr, d), lambda i: (i, COL_GA // d)),
                  pl.BlockSpec((tr, d), lambda i: (i, COL_GB // d)), row, row, row],
        out_specs=(row, row, pl.BlockSpec((tr, 2 * d), lambda i: (i, COL_GA // (2 * d)))),
        compiler_params=_cparams(("parallel",)),
    )(proj, proj, ya, yb, dmix)


def _swiglu_fwd(gt, up, *, s):
    tr = min(256, s)
    f = FFN_HIDDEN

    def body(gt_ref, up_ref, o_ref):
        gv = gt_ref[...].astype(F32)
        o_ref[...] = (gv * _sigmoid(gv) * up_ref[...].astype(F32)).astype(BF16)

    row = pl.BlockSpec((tr, f), lambda i: (i, 0))
    return pl.pallas_call(
        body, name="swiglu_fwd",
        out_shape=jax.ShapeDtypeStruct((s, f), BF16),
        grid=(s // tr,),
        in_specs=[row, row], out_specs=row,
        compiler_params=_cparams(("parallel",)),
    )(gt, up)


def _swiglu_bwd(gt, up, dhid, *, s):
    tr = min(256, s)
    f = FFN_HIDDEN

    def body(gt_ref, up_ref, dh_ref, dgt_ref, dup_ref):
        gv = gt_ref[...].astype(F32)
        uv = up_ref[...].astype(F32)
        dh = dh_ref[...].astype(F32)
        sg = _sigmoid(gv)
        dgt_ref[...] = (dh * uv * sg * (1.0 + gv * (1.0 - sg))).astype(BF16)
        dup_ref[...] = (dh * gv * sg).astype(BF16)

    row = pl.BlockSpec((tr, f), lambda i: (i, 0))
    return pl.pallas_call(
        body, name="swiglu_bwd",
        out_shape=(jax.ShapeDtypeStruct((s, f), BF16), jax.ShapeDtypeStruct((s, f), BF16)),
        grid=(s // tr,),
        in_specs=[row, row, row], out_specs=(row, row),
        compiler_params=_cparams(("parallel",)),
    )(gt, up, dhid)


def _tri(strict):
    r = lax.broadcasted_iota(jnp.int32, (CHUNK, CHUNK), 0)
    c = lax.broadcasted_iota(jnp.int32, (CHUNK, CHUNK), 1)
    return jnp.where((c < r) if strict else (c <= r), 1.0, 0.0).astype(F32)


def _gla_gate_terms(la_c, tri):
    cum = jnp.dot(tri, la_c, precision=lax.Precision.HIGHEST, preferred_element_type=F32)
    cend = cum[CHUNK - 1:CHUNK, :]
    return jnp.exp(cend - cum), jnp.exp(cend)


def _hk(h):
    return slice(h * HEAD_K, (h + 1) * HEAD_K)


def _hv(h):
    return slice(h * HEAD_V, (h + 1) * HEAD_V)


def _gla_fwd(proj, wfg2p, bfg, gn, *, s):
    tr = min(512, s)
    nb = s // tr
    nc = tr // CHUNK

    def body(q_ref, k_ref, v_ref, r_ref, fz_ref, w_ref, b_ref, gn_ref, oa_ref, st_ref,
             state, la_scr, o_scr):
        i = pl.program_id(0)

        @pl.when(i == 0)
        def _():
            state[...] = jnp.zeros_like(state)

        fg = jnp.dot(fz_ref[...], w_ref[...], preferred_element_type=F32) + b_ref[...]
        la_scr[...] = _log_sigmoid(fg) * (1.0 / GATE_TAU)
        tri = _tri(False)

        def chunk(ci, carry):
            r0 = pl.multiple_of(ci * CHUNK, CHUNK)
            rows = pl.ds(r0, CHUNK)
            e, gam = _gla_gate_terms(la_scr[rows, :], tri)
            kd = (k_ref[rows, :].astype(F32) * e).astype(BF16)
            qs = (q_ref[rows, :].astype(F32) * Q_SCALE).astype(BF16)
            v_c = v_ref[rows, :]
            for h in range(GLA_HEADS):
                upd = lax.dot_general(v_c[:, _hv(h)], kd[:, _hk(h)], _DOT_DIMS["tn"],
                                      preferred_element_type=F32)
                st_h = state[:, _hk(h)] * gam[:, _hk(h)] + upd
                state[:, _hk(h)] = st_h
                o_scr[rows, _hv(h)] = lax.dot_general(qs[:, _hk(h)], st_h.astype(BF16), _DOT_DIMS["nt"],
                                                       preferred_element_type=F32)
            st_ref[ci] = state[...]
            return carry

        lax.fori_loop(0, nc, chunk, 0, unroll=True)
        for h in range(GLA_HEADS):
            o = o_scr[:, _hv(h)]
            rs = lax.rsqrt(jnp.mean(o * o, axis=-1, keepdims=True) + NORM_EPS)
            rv = r_ref[:, _hv(h)].astype(F32)
            oa_ref[:, _hv(h)] = ((o * rs * gn_ref[...]).astype(F32) * (rv * _sigmoid(rv))).astype(BF16)

    return pl.pallas_call(
        body, name="gla_fwd",
        out_shape=(jax.ShapeDtypeStruct((s, VAL_WIDTH), BF16),
                   jax.ShapeDtypeStruct((s // CHUNK, HEAD_V, KEY_WIDTH), F32)),
        grid=(nb,),
        in_specs=[pl.BlockSpec((tr, KEY_WIDTH), lambda i: (i, COL_Q // KEY_WIDTH)),
                  pl.BlockSpec((tr, KEY_WIDTH), lambda i: (i, COL_K // KEY_WIDTH)),
                  pl.BlockSpec((tr, VAL_WIDTH), lambda i: (i, COL_V // VAL_WIDTH)),
                  pl.BlockSpec((tr, VAL_WIDTH), lambda i: (i, COL_R // VAL_WIDTH)),
                  pl.BlockSpec((tr, FZ_PAD), lambda i: (i, COL_FZ // FZ_PAD)),
                  pl.BlockSpec((FZ_PAD, KEY_WIDTH), lambda i: (0, 0)),
                  pl.BlockSpec((1, KEY_WIDTH), lambda i: (0, 0)),
                  pl.BlockSpec((1, HEAD_V), lambda i: (0, 0))],
        out_specs=(pl.BlockSpec((tr, VAL_WIDTH), lambda i: (i, 0)),
                   pl.BlockSpec((nc, HEAD_V, KEY_WIDTH), lambda i: (i, 0, 0))),
        scratch_shapes=[pltpu.VMEM((HEAD_V, KEY_WIDTH), F32), pltpu.VMEM((tr, KEY_WIDTH), F32),
                        pltpu.VMEM((tr, VAL_WIDTH), F32)],
        compiler_params=_cparams(("arbitrary",)),
    )(proj, proj, proj, proj, proj, wfg2p, bfg, gn)


def _gla_bwd(proj, doa, states, wfg2p, bfg, gn, dproj, *, s):
    tr = min(512, s)
    nb = s // tr
    nc = tr // CHUNK

    def body(q_ref, k_ref, v_ref, r_ref, fz_ref, doa_ref, st_ref, stp_ref, w_ref, b_ref, gn_ref, dproj_in_ref,
             dp_ref, dfz_ref, dgn_ref, dw_ref, db_ref,
             carry, fg_scr, la_scr, dla_scr, o_scr, do_scr):
        i = pl.program_id(0)

        @pl.when(i == 0)
        def _():
            carry[...] = jnp.zeros_like(carry)
            dgn_ref[...] = jnp.zeros_like(dgn_ref)
            dw_ref[...] = jnp.zeros_like(dw_ref)
            db_ref[...] = jnp.zeros_like(db_ref)

        fz = fz_ref[...]
        fg = jnp.dot(fz, w_ref[...], preferred_element_type=F32) + b_ref[...]
        fg_scr[...] = fg
        la_scr[...] = _log_sigmoid(fg) * (1.0 / GATE_TAU)

        def fwd_chunk(ci, c0):
            r0 = pl.multiple_of(ci * CHUNK, CHUNK)
            rows = pl.ds(r0, CHUNK)
            qs = (q_ref[rows, :].astype(F32) * Q_SCALE).astype(BF16)
            st = st_ref[ci].astype(BF16)
            for h in range(GLA_HEADS):
                o_scr[rows, _hv(h)] = lax.dot_general(qs[:, _hk(h)], st[:, _hk(h)], _DOT_DIMS["nt"],
                                                       preferred_element_type=F32)
            return c0

        lax.fori_loop(0, nc, fwd_chunk, 0, unroll=True)

        gnv = gn_ref[...]
        dgn_part = jnp.zeros((1, HEAD_V), F32)
        for h in range(GLA_HEADS):
            o = o_scr[:, _hv(h)]
            rs = lax.rsqrt(jnp.mean(o * o, axis=-1, keepdims=True) + NORM_EPS)
            nrm = o * rs
            rv = r_ref[:, _hv(h)].astype(F32)
            sg = _sigmoid(rv)
            doa_h = doa_ref[:, _hv(h)].astype(F32)
            don = doa_h * (rv * sg)
            dp_ref[:, COL_R + h * HEAD_V:COL_R + (h + 1) * HEAD_V] = (
                doa_h * (nrm * gnv) * (sg * (1.0 + rv * (1.0 - sg)))).astype(BF16)
            dgn_part = dgn_part + jnp.sum(don * nrm, axis=0, keepdims=True)
            dn = don * gnv
            do_scr[:, _hv(h)] = (rs * (dn - nrm * jnp.mean(dn * nrm, axis=-1, keepdims=True))).astype(BF16)
        dgn_ref[...] += dgn_part

        tri = _tri(False)
        tri_s = _tri(True)
        first_block = i == nb - 1

        def bwd_chunk(cc, c0):
            ci = nc - 1 - cc
            r0 = pl.multiple_of(ci * CHUNK, CHUNK)
            rows = pl.ds(r0, CHUNK)
            e, gam = _gla_gate_terms(la_scr[rows, :], tri)
            k_c = k_ref[rows, :].astype(F32)
            kd = k_c * e
            kd_b = kd.astype(BF16)
            qs = (q_ref[rows, :].astype(F32) * Q_SCALE).astype(BF16)
            v_c = v_ref[rows, :]
            do_c = do_scr[rows, :]
            st = st_ref[ci]
            st_b = st.astype(BF16)
            st_prev_in = st_ref[jnp.maximum(ci - 1, 0)]
            st_prev_edge = jnp.where(first_block, 0.0, stp_ref[0])
            st_prev = jnp.where(ci > 0, st_prev_in, st_prev_edge)
            dkd_parts = []
            dgam_parts = []
            for h in range(GLA_HEADS):
                dst = lax.dot_general(do_c[:, _hv(h)], qs[:, _hk(h)], _DOT_DIMS["tn"],
                                      preferred_element_type=F32) + carry[:, _hk(h)]
                dst_b = dst.astype(BF16)
                dqs = jnp.dot(do_c[:, _hv(h)], st_b[:, _hk(h)], preferred_element_type=F32)
                dp_ref[rows, COL_Q + h * HEAD_K:COL_Q + (h + 1) * HEAD_K] = (dqs * Q_SCALE).astype(BF16)
                dkd_parts.append(jnp.dot(v_c[:, _hv(h)], dst_b, preferred_element_type=F32))
                dp_ref[rows, COL_V + h * HEAD_V:COL_V + (h + 1) * HEAD_V] = lax.dot_general(
                    kd_b[:, _hk(h)], dst_b, _DOT_DIMS["nt"], preferred_element_type=F32).astype(BF16)
                dgam_parts.append(jnp.sum(dst * st_prev[:, _hk(h)], axis=0, keepdims=True))
                carry[:, _hk(h)] = dst * gam[:, _hk(h)]
            dkd = jnp.concatenate(dkd_parts, axis=1)
            dgam = jnp.concatenate(dgam_parts, axis=1)
            dp_ref[rows, COL_K:COL_K + KEY_WIDTH] = (dkd * e).astype(BF16)
            dz = dkd * kd
            dla_scr[rows, :] = gam * dgam + jnp.dot(tri_s, dz, precision=lax.Precision.HIGHEST,
                                                    preferred_element_type=F32)
            return c0

        lax.fori_loop(0, nc, bwd_chunk, 0, unroll=True)

        dfg = dla_scr[...] * (1.0 / GATE_TAU) * _sigmoid(-fg_scr[...])
        dfg_b = dfg.astype(BF16)
        dfz_ref[...] = lax.dot_general(dfg_b, w_ref[...], _DOT_DIMS["nt"],
                                       preferred_element_type=F32).astype(BF16)
        dw_ref[...] += lax.dot_general(fz, dfg_b, _DOT_DIMS["tn"], preferred_element_type=F32)
        db_ref[...] += jnp.sum(dfg, axis=0, keepdims=True)

    def rev(i):
        return nb - 1 - i

    return pl.pallas_call(
        body, name="gla_bwd",
        out_shape=(jax.ShapeDtypeStruct((s, PROJ_W), BF16),
                   jax.ShapeDtypeStruct((s, FZ_PAD), BF16),
                   jax.ShapeDtypeStruct((1, HEAD_V), F32),
                   jax.ShapeDtypeStruct((FZ_PAD, KEY_WIDTH), F32),
                   jax.ShapeDtypeStruct((1, KEY_WIDTH), F32)),
        grid=(nb,),
        in_specs=[pl.BlockSpec((tr, KEY_WIDTH), lambda i: (rev(i), COL_Q // KEY_WIDTH)),
                  pl.BlockSpec((tr, KEY_WIDTH), lambda i: (rev(i), COL_K // KEY_WIDTH)),
                  pl.BlockSpec((tr, VAL_WIDTH), lambda i: (rev(i), COL_V // VAL_WIDTH)),
                  pl.BlockSpec((tr, VAL_WIDTH), lambda i: (rev(i), COL_R // VAL_WIDTH)),
                  pl.BlockSpec((tr, FZ_PAD), lambda i: (rev(i), COL_FZ // FZ_PAD)),
                  pl.BlockSpec((tr, VAL_WIDTH), lambda i: (rev(i), 0)),
                  pl.BlockSpec((nc, HEAD_V, KEY_WIDTH), lambda i: (rev(i), 0, 0)),
                  pl.BlockSpec((1, HEAD_V, KEY_WIDTH), lambda i: (jnp.maximum(rev(i) * nc - 1, 0), 0, 0)),
                  pl.BlockSpec((FZ_PAD, KEY_WIDTH), lambda i: (0, 0)),
                  pl.BlockSpec((1, KEY_WIDTH), lambda i: (0, 0)),
                  pl.BlockSpec((1, HEAD_V), lambda i: (0, 0)), ANY],
        out_specs=(pl.BlockSpec((tr, 3 * VAL_WIDTH), lambda i: (rev(i), 0)),
                   pl.BlockSpec((tr, FZ_PAD), lambda i: (rev(i), 0)),
                   pl.BlockSpec((1, HEAD_V), lambda i: (0, 0)),
                   pl.BlockSpec((FZ_PAD, KEY_WIDTH), lambda i: (0, 0)),
                   pl.BlockSpec((1, KEY_WIDTH), lambda i: (0, 0))),
        scratch_shapes=[pltpu.VMEM((HEAD_V, KEY_WIDTH), F32),
                        pltpu.VMEM((tr, KEY_WIDTH), F32),
                        pltpu.VMEM((tr, KEY_WIDTH), F32),
                        pltpu.VMEM((tr, KEY_WIDTH), F32),
                        pltpu.VMEM((tr, VAL_WIDTH), F32),
                        pltpu.VMEM((tr, VAL_WIDTH), BF16)],
        input_output_aliases={11: 0},
        compiler_params=_cparams(("arbitrary",)),
    )(proj, proj, proj, proj, proj, doa, states, states, wfg2p, bfg, gn, dproj)


def _put_fz(dproj, dfz, *, s):
    tr = min(512, s)

    def body(dfz_ref, dproj_in_ref, o_ref):
        o_ref[...] = dfz_ref[...]

    return pl.pallas_call(
        body, name="put_dfz",
        out_shape=jax.ShapeDtypeStruct((s, PROJ_W), BF16),
        grid=(s // tr,),
        in_specs=[pl.BlockSpec((tr, FZ_PAD), lambda i: (i, 0)), ANY],
        out_specs=pl.BlockSpec((tr, FZ_PAD), lambda i: (i, COL_FZ // FZ_PAD)),
        input_output_aliases={1: 0},
        compiler_params=_cparams(("parallel",)),
    )(dfz, dproj)


_ADAM_C1 = 1.0 / (1.0 - ADAM_B1 ** ADAM_STEP)
_ADAM_C2 = 1.0 / (1.0 - ADAM_B2 ** ADAM_STEP)


def _adamw_math(wv, gv, mv, vv):
    nm = ADAM_B1 * mv + (1.0 - ADAM_B1) * gv
    nv = ADAM_B2 * vv + (1.0 - ADAM_B2) * (gv * gv)
    delta = -ADAM_LR * ((nm * _ADAM_C1) / (jnp.sqrt(nv * _ADAM_C2) + ADAM_EPS) + ADAM_WD * wv)
    return delta, nm, nv


def _adamw(w, g, m, v, *, name):
    shape = w.shape
    if w.ndim == 1:
        w, g, m, v = (t.reshape(1, 1, -1) for t in (w, g, m, v))
    elif w.ndim == 2:
        w, g, m, v = (t.reshape((1,) + t.shape) for t in (w, g, m, v))
    l, a, b = w.shape

    def body(w_ref, g_ref, m_ref, v_ref, d_ref, nm_ref, nv_ref):
        d_ref[...], nm_ref[...], nv_ref[...] = _adamw_math(w_ref[...], g_ref[...], m_ref[...], v_ref[...])

    blk = pl.BlockSpec((1, a, b), lambda li: (li, 0, 0))
    outs = pl.pallas_call(
        body, name=name,
        out_shape=tuple(jax.ShapeDtypeStruct((l, a, b), F32) for _ in range(3)),
        grid=(l,),
        in_specs=[blk, blk, blk, blk],
        out_specs=(blk, blk, blk),
        compiler_params=_cparams(("parallel",)),
    )(w, g, m, v)
    return tuple(o.reshape(shape) for o in outs)


def _adamw_layers(w, reduced, received, on_core0, m, v, *, name):
    l, a, b = w.shape
    assert l == DEPTH == 2
    ta = _row_tile(a, F32_SUBLANES, 384)

    def body(flag_ref, w_ref, r0_ref, o0_ref, r1_ref, o1_ref, m_ref, v_ref, g_ref, d_ref, nm_ref, nv_ref):
        core0 = flag_ref[...] > 0.5
        g0 = jnp.where(core0, r0_ref[...], o0_ref[...])
        g1 = jnp.where(core0, o1_ref[...], r1_ref[...])
        gv = jnp.where(pl.program_id(0) == 0, g0, g1)
        g_ref[0] = gv
        d_ref[0], nm_ref[0], nv_ref[0] = _adamw_math(w_ref[0], gv, m_ref[0], v_ref[0])

    blk = pl.BlockSpec((1, ta, b), lambda li, ai: (li, ai, 0))
    gblk = pl.BlockSpec((ta, b), lambda li, ai: (ai, 0))
    return pl.pallas_call(
        body, name=name,
        out_shape=tuple(jax.ShapeDtypeStruct((l, a, b), F32) for _ in range(4)),
        grid=(l, a // ta),
        in_specs=[pl.BlockSpec((1, 1), lambda li, ai: (0, 0)), blk, gblk, gblk, gblk, gblk, blk, blk],
        out_specs=(blk, blk, blk, blk),
        compiler_params=_cparams(("parallel", "parallel")),
    )(on_core0, w, reduced[0], received[0], reduced[1], received[1], m, v)


MESH_ID = pl.DeviceIdType.MESH
ANY = pl.BlockSpec(memory_space=pl.ANY)


def _position():
    x, y, c = lax.axis_index("x"), lax.axis_index("y"), lax.axis_index("c")
    chips = [(1 - x, y), (x, 1 - y), (1 - x, 1 - y)]
    return x, y, c, chips


def _chip_index(xy):
    return 2 * xy[0] + xy[1]


def _all_gather_weights(shards):
    n = len(shards)

    def body(*refs):
        ins, outs = refs[:n], refs[n:2 * n]
        send_sems, recv_sems, pass_send, pass_recv = refs[2 * n:]
        x, y, c, chips = _position()
        sibling = (x, y, 1 - c)

        def ici(a, k, chip_from):
            return pltpu.make_async_remote_copy(
                src_ref=ins[a].at[c], dst_ref=outs[a].at[c, _chip_index(chip_from)],
                send_sem=send_sems.at[a, k], recv_sem=recv_sems.at[a, k],
                device_id=(chips[k][0], chips[k][1], c), device_id_type=MESH_ID)

        def handoff(a, k, layer):
            slot = outs[a].at[layer, _chip_index(chips[k])]
            return pltpu.make_async_remote_copy(
                src_ref=slot, dst_ref=slot, send_sem=pass_send.at[a, k], recv_sem=pass_recv.at[a, k],
                device_id=sibling, device_id_type=MESH_ID)

        for a in range(n):
            for k in range(3):
                ici(a, k, (x, y)).start()
        for k in range(3):
            for a in range(n):
                ici(a, k, chips[k]).wait_recv()
                handoff(a, k, c).start()
        for k in range(3):
            for a in range(n):
                handoff(a, k, 1 - c).wait_recv()
        for a in range(n):
            for k in range(3):
                ici(a, k, (x, y)).wait_send()
                handoff(a, k, c).wait_send()

    gathered = pl.pallas_call(
        body, name="all_gather_weights",
        out_shape=tuple(jax.ShapeDtypeStruct((DEPTH, N_CHIPS) + t.shape[1:], t.dtype) for t in shards),
        in_specs=[ANY] * n, out_specs=tuple([ANY] * n),
        scratch_shapes=[pltpu.SemaphoreType.DMA((n, 3)), pltpu.SemaphoreType.DMA((n, 3)),
                        pltpu.SemaphoreType.DMA((n, 3)), pltpu.SemaphoreType.DMA((n, 3))],
    )(*shards)
    me = _chip_index((lax.axis_index("x"), lax.axis_index("y")))
    return [lax.dynamic_update_index_in_dim(g, t[:, None], me, axis=1) for g, t in zip(gathered, shards)]


HBM = pl.BlockSpec(memory_space=pltpu.HBM)
SEM = pl.BlockSpec(memory_space=pltpu.SEMAPHORE)
DATAFLOW_EFFECT = pltpu.SideEffectType.DATAFLOW_SIDE_EFFECTING
TOKEN_SHAPE = (8, LANE)


def _landing(shape, dtype):
    return pltpu.with_memory_space_constraint(lax.empty(shape, dtype), pltpu.HBM)


def _split_start(bufs, sem_shape, issue, *, name, after=None):
    n = len(bufs)
    n_in = n + (after is not None)

    def body(*refs):
        issue(refs[:n], refs[n_in], refs[n_in + 1])
        token = refs[-1]
        token[...] = jnp.zeros_like(token)

    operands = [pltpu.with_memory_space_constraint(t, pltpu.HBM) for t in bufs]
    outs = pl.pallas_call(
        body, name=name,
        out_shape=(pltpu.SemaphoreType.DMA(sem_shape), pltpu.SemaphoreType.DMA(sem_shape),
                   *[pltpu.HBM(t.shape, t.dtype) for t in bufs], jax.ShapeDtypeStruct(TOKEN_SHAPE, F32)),
        in_specs=[HBM] * n + [ANY] * (after is not None),
        out_specs=(SEM, SEM, *[HBM] * n, pl.BlockSpec(memory_space=pltpu.VMEM)),
        input_output_aliases={i: 2 + i for i in range(n)},
        compiler_params=pltpu.CompilerParams(has_side_effects=DATAFLOW_EFFECT),
    )(*operands, *([after] if after is not None else []))
    return outs[0], outs[1], list(outs[2:2 + n]), outs[-1]


def _split_wait(started, after, settle, *, name):
    send_sems, recv_sems, bufs, _ = started
    n = len(bufs)
    afters = tuple(after) if isinstance(after, (tuple, list)) else (after,)

    def body(*refs):
        settle(refs[:n], refs[n], refs[n + 1])

    outs = pl.pallas_call(
        body, name=name,
        out_shape=tuple(pltpu.HBM(t.shape, t.dtype) for t in bufs),
        in_specs=[HBM] * n + [SEM, SEM] + [ANY] * len(afters),
        out_specs=tuple([HBM] * n),
        input_output_aliases={i: i for i in range(n)},
        compiler_params=pltpu.CompilerParams(has_side_effects=DATAFLOW_EFFECT),
    )(*bufs, send_sems, recv_sems, *afters)
    return list(outs)


def _to_sibling(bufs, n, send_sems, recv_sems):
    x, y, c, _ = _position()
    return [pltpu.make_async_remote_copy(
        src_ref=bufs[a], dst_ref=bufs[n + a], send_sem=send_sems.at[a], recv_sem=recv_sems.at[a],
        device_id=(x, y, 1 - c), device_id_type=MESH_ID) for a in range(n)]


def _sibling_push_start(layer, sender_is_reducer, arrays, *, name):
    n = len(arrays)
    sender = layer if sender_is_reducer else 1 - layer

    def issue(bufs, send_sems, recv_sems):
        @pl.when(lax.axis_index("c") == sender)
        def _():
            for cp in _to_sibling(bufs, n, send_sems, recv_sems):
                cp.start()

    lands = [_landing(t.shape, t.dtype) for t in arrays]
    return _split_start(list(arrays) + lands, (n,), issue, name=name)


def _sibling_push_wait(layer, sender_is_reducer, started, after, *, name):
    n = len(started[2]) // 2
    sender = layer if sender_is_reducer else 1 - layer

    def settle(bufs, send_sems, recv_sems):
        c = lax.axis_index("c")

        @pl.when(c == sender)
        def _():
            for cp in _to_sibling(bufs, n, send_sems, recv_sems):
                cp.wait_send()

        @pl.when(c != sender)
        def _():
            for cp in _to_sibling(bufs, n, send_sems, recv_sems):
                cp.wait_recv()

    outs = _split_wait(started, after, settle, name=name)
    return outs[:n], outs[n:]


def _chip_copies(bufs, n, send_sems, recv_sems):
    x, y, c, chips = _position()
    return [pltpu.make_async_remote_copy(
        src_ref=bufs[a].at[_chip_index(chips[k])], dst_ref=bufs[n + a].at[k],
        send_sem=send_sems.at[3 * a + k], recv_sem=recv_sems.at[3 * a + k],
        device_id=(chips[k][0], chips[k][1], c), device_id_type=MESH_ID)
        for a in range(n) for k in range(3)]


def _chip_exchange_start(layer, parts, *, name):
    n = len(parts)

    def issue(bufs, send_sems, recv_sems):
        @pl.when(lax.axis_index("c") == layer)
        def _():
            for cp in _chip_copies(bufs, n, send_sems, recv_sems):
                cp.start()

    lands = [_landing((3,) + t.shape[1:], t.dtype) for t in parts]
    return _split_start(list(parts) + lands, (3 * n,), issue, name=name)


def _chip_exchange_wait(layer, started, after, *, name):
    n = len(started[2]) // 2

    def settle(bufs, send_sems, recv_sems):
        @pl.when(lax.axis_index("c") == layer)
        def _():
            for cp in _chip_copies(bufs, n, send_sems, recv_sems):
                cp.wait()

    outs = _split_wait(started, after, settle, name=name)
    return outs[:n], outs[n:]


def _gather_copies(layer, bufs, n, send_sems, recv_sems, arriving):
    x, y, c, chips = _position()
    me = _chip_index((x, y))
    return [pltpu.make_async_remote_copy(
        src_ref=bufs[a], dst_ref=bufs[n + a].at[_chip_index(chips[k]) if arriving else me],
        send_sem=send_sems.at[3 * a + k], recv_sem=recv_sems.at[3 * a + k],
        device_id=(chips[k][0], chips[k][1], c), device_id_type=MESH_ID)
        for a in range(n) for k in range(3)]


def _gather_start(layer, shards, *, name, after=None):
    n = len(shards)

    def issue(bufs, send_sems, recv_sems):
        @pl.when(lax.axis_index("c") == layer)
        def _():
            for cp in _gather_copies(layer, bufs, n, send_sems, recv_sems, False):
                cp.start()

    lands = [_landing((N_CHIPS,) + t.shape, t.dtype) for t in shards]
    return _split_start(list(shards) + lands, (3 * n,), issue, name=name, after=after)


def _gather_wait(layer, started, after, *, name):
    n = len(started[2]) // 2

    def settle(bufs, send_sems, recv_sems):
        @pl.when(lax.axis_index("c") == layer)
        def _():
            for cp in _gather_copies(layer, bufs, n, send_sems, recv_sems, False):
                cp.wait_send()
            for cp in _gather_copies(layer, bufs, n, send_sems, recv_sems, True):
                cp.wait_recv()

    return _split_wait(started, after, settle, name=name)[n:]


def _handoff_copies(bufs, n, send_sems, recv_sems):
    x, y, c, chips = _position()
    out = []
    for a in range(n):
        for k in range(3):
            slot = bufs[a].at[_chip_index(chips[k])]
            out.append(pltpu.make_async_remote_copy(
                src_ref=slot, dst_ref=slot, send_sem=send_sems.at[3 * a + k], recv_sem=recv_sems.at[3 * a + k],
                device_id=(x, y, 1 - c), device_id_type=MESH_ID))
    return out


def _handoff_start(layer, gathered, *, name):
    n = len(gathered)

    def issue(bufs, send_sems, recv_sems):
        @pl.when(lax.axis_index("c") == layer)
        def _():
            for cp in _handoff_copies(bufs, n, send_sems, recv_sems):
                cp.start()

    return _split_start(list(gathered), (3 * n,), issue, name=name)


def _handoff_wait(layer, started, after, *, name):
    n = len(started[2])

    def settle(bufs, send_sems, recv_sems):
        c = lax.axis_index("c")

        @pl.when(c == layer)
        def _():
            for cp in _handoff_copies(bufs, n, send_sems, recv_sems):
                cp.wait_send()

        @pl.when(c != layer)
        def _():
            for cp in _handoff_copies(bufs, n, send_sems, recv_sems):
                cp.wait_recv()

    return _split_wait(started, after, settle, name=name)


F32_SUBLANES = 8
BF16_SUBLANES = 16


def _row_tile(a, sublanes=BF16_SUBLANES, max_rows=704):
    best = None
    for cand in range(sublanes, min(a, max_rows) + 1, sublanes):
        if a % cand == 0:
            best = cand
    assert best is not None, a
    return best


def _pair_sum(mine, sib, *, name):
    nchip, a, b = sib.shape
    ta = _row_tile(a)

    def body(m_ref, s_ref, o_ref):
        o_ref[...] = (m_ref[...].astype(F32) + s_ref[...].astype(F32)).astype(BF16)

    blk = pl.BlockSpec((1, ta, b), lambda j, r: (j, r, 0))
    return pl.pallas_call(
        body, name=name,
        out_shape=jax.ShapeDtypeStruct((nchip, a, b), BF16),
        grid=(nchip, a // ta),
        in_specs=[blk, blk], out_specs=blk,
        compiler_params=_cparams(("parallel", "parallel")),
    )(mine, sib)


def _total_sum(own, recv, *, name):
    a, b = own.shape
    ta = _row_tile(a)

    def body(o_ref, r_ref, t_ref):
        acc = o_ref[...].astype(F32)
        for k in range(3):
            acc = acc + r_ref[k].astype(F32)
        t_ref[...] = acc

    blk = pl.BlockSpec((ta, b), lambda r: (r, 0))
    return pl.pallas_call(
        body, name=name,
        out_shape=jax.ShapeDtypeStruct((a, b), F32),
        grid=(a // ta,),
        in_specs=[blk, pl.BlockSpec((3, ta, b), lambda r: (0, r, 0))], out_specs=blk,
        compiler_params=_cparams(("parallel",)),
    )(own, recv)


def _all_reduce_small(packed):
    rows, width = packed.shape

    def body(x_ref, out_ref, gath, send_sems, recv_sems, local_sem):
        x, y, c, chips = _position()
        me, sibling = (x, y, c), (x, y, 1 - c)

        def slot(px, py, pc):
            return gath.at[4 * px + 2 * py + pc]

        def copy(k, block, to, src=None):
            return pltpu.make_async_remote_copy(
                src_ref=slot(*block) if src is None else src, dst_ref=slot(*block),
                send_sem=send_sems.at[k], recv_sem=recv_sems.at[k], device_id=to, device_id_type=MESH_ID)

        mine = pltpu.make_async_copy(x_ref, slot(*me), local_sem)
        mine.start()
        first = [copy(0, me, sibling, src=x_ref)]
        first += [copy(1 + j, me, (*chip, c), src=x_ref) for j, chip in enumerate(chips)]
        for cp in first:
            cp.start()
        passed = [copy(4 + j, (*chip, c), sibling) for j, chip in enumerate(chips)]
        for j, chip in enumerate(chips):
            copy(1 + j, (*chip, c), me).wait_recv()
            passed[j].start()
        copy(0, sibling, me).wait_recv()
        for j, chip in enumerate(chips):
            copy(4 + j, (*chip, 1 - c), me).wait_recv()
        for cp in first + passed:
            cp.wait_send()
        mine.wait()
        acc = gath[0]
        for d in range(1, N_DEV):
            acc = acc + gath[d]
        out_ref[...] = acc

    return pl.pallas_call(
        body, name="all_reduce_small",
        out_shape=jax.ShapeDtypeStruct((rows, width), F32),
        in_specs=[pl.BlockSpec(memory_space=pltpu.VMEM)],
        out_specs=pl.BlockSpec(memory_space=pltpu.VMEM),
        scratch_shapes=[pltpu.VMEM((N_DEV, rows, width), F32), pltpu.SemaphoreType.DMA((7,)),
                        pltpu.SemaphoreType.DMA((7,)), pltpu.SemaphoreType.DMA],
    )(packed)


def _mixer_forward_branches(x0, h, w, s, dep=None, before_conv=None):
    proj = _matmul(h, w["w_in_t"], mode="nt", out_dtype=BF16, tm=1024, tn=1664, tk=1024, name="proj_fwd",
                   dep=dep)
    oa, states = _gla_fwd(proj, w["w_fg2"], w["b_fg"], w["gla_norm_g"], s=s)
    conv_wb = w["conv_wb"]
    if before_conv is not None:
        conv_wb = conv_wb + before_conv(oa)[0, 0]
    cb_in = _conv_fwd(proj, conv_wb, s=s)
    return dict(x0=x0, h=h, proj=proj, oa=oa, states=states, cb_in=cb_in)


def _mixer_forward_out(sv, w, s, dep=None):
    ya = _matmul(sv["oa"], w["w_oa"], mode="nn", out_dtype=BF16, tm=1024, tn=1024, tk=1024, name="ya_fwd",
                 dep=dep)
    yb = _matmul(sv["cb_in"], w["w_ob"], mode="nn", out_dtype=BF16, tm=1024, tn=1024, tk=1024, name="yb_fwd")
    mix = _mix_fwd(sv["proj"], ya, yb, s=s)
    x1 = _matmul(mix, w["w_o"], mode="nn", out_dtype=F32, tm=1024, tn=1024, tk=1024, name="wo_fwd",
                 add=sv["x0"])
    return x1, dict(sv, ya=ya, yb=yb, mix=mix)


def _ffn_forward_hidden(x1, w, s, after_gate=None):
    h2 = _rms_fwd(x1, w["norm2_g"], name="rms2_fwd")
    gt = _matmul(h2, w["w_gate_t"], mode="nt", out_dtype=BF16, tm=1024, tn=1408, tk=1024, name="ffn_gate_fwd")
    up = _matmul(h2, w["w_up_t"], mode="nt", out_dtype=BF16, tm=1024, tn=1408, tk=1024, name="ffn_up_fwd",
                 dep=None if after_gate is None else after_gate(gt))
    return dict(x1=x1, h2=h2, gt=gt, up=up, hid=_swiglu_fwd(gt, up, s=s))


def _ffn_forward_out(sv, w, dep=None):
    return _matmul(sv["hid"], w["w_ffn_down"], mode="nn", out_dtype=F32, tm=1024, tn=1024, tk=FFN_HIDDEN,
                   name="ffn_out_fwd", add=sv["x1"], dep=dep)


def _ffn_backward_grads(dx2b, w, sv, s):
    g = {}
    dhid = _matmul(dx2b, w["w_ffn_down"], mode="nt", out_dtype=BF16, tm=1024, tn=FFN_HIDDEN, tk=1024,
                   name="ffn_out_bwd")
    g["w_ffn_down"] = _matmul(sv["hid"], dx2b, mode="tn", out_dtype=BF16, tm=1408, tn=1024, tk=1024,
                              name="ffn_out_wgrad")
    dgt, dup = _swiglu_bwd(sv["gt"], sv["up"], dhid, s=s)
    g["w_gate_t"] = _matmul(dgt, sv["h2"], mode="tn", out_dtype=BF16, tm=1408, tn=1024, tk=1024,
                            name="ffn_gate_wgrad")
    g["w_up_t"] = _matmul(dup, sv["h2"], mode="tn", out_dtype=BF16, tm=1408, tn=1024, tk=1024,
                          name="ffn_up_wgrad")
    return g, dgt, dup, dhid


def _ffn_backward_input(dgt, dup, dx2, w, sv, dep=None):
    dh2 = _matmul(dgt, w["w_gate_t"], mode="nn", out_dtype=F32, tm=1024, tn=1024, tk=FFN_HIDDEN,
                  name="ffn_gate_bwd", dep=dep)
    dh2 = _matmul(dup, w["w_up_t"], mode="nn", out_dtype=F32, tm=1024, tn=1024, tk=FFN_HIDDEN,
                  name="ffn_up_bwd", add=dh2)
    return _rms_bwd(sv["x1"], w["norm2_g"], dh2, dx2, name="rms2_bwd")


def _mixer_backward_branches(dx1b, w, sv, s, dep=None):
    g = {}
    dmix = _matmul(dx1b, w["w_o"], mode="nt", out_dtype=BF16, tm=1024, tn=1024, tk=1024, name="wo_bwd",
                   dep=dep)
    g["w_o"] = _matmul(sv["mix"], dx1b, mode="tn", out_dtype=BF16, tm=1024, tn=1024, tk=2048, name="wo_wgrad")
    dya, dyb, dproj = _mix_bwd(sv["proj"], sv["ya"], sv["yb"], dmix, s=s)
    dcb = _matmul(dyb, w["w_ob"], mode="nt", out_dtype=BF16, tm=1024, tn=1024, tk=1024, name="yb_bwd")
    g["w_ob"] = _matmul(sv["cb_in"], dyb, mode="tn", out_dtype=BF16, tm=1024, tn=1024, tk=2048, name="yb_wgrad")
    doa = _matmul(dya, w["w_oa"], mode="nt", out_dtype=BF16, tm=1024, tn=1024, tk=1024, name="ya_bwd")
    g["w_oa"] = _matmul(sv["oa"], dya, mode="tn", out_dtype=BF16, tm=1024, tn=1024, tk=2048, name="ya_wgrad")
    dproj, g["conv_wb"] = _conv_bwd(sv["proj"], dcb, w["conv_wb"], dproj, s=s)
    dproj, dfz, g["gla_norm_g"], g["w_fg2"], g["b_fg"] = _gla_bwd(
        sv["proj"], doa, sv["states"], w["w_fg2"], w["b_fg"], w["gla_norm_g"], dproj, s=s)
    return g, _put_fz(dproj, dfz, s=s)


def _proj_wgrad(dproj, sv, dep=None):
    return _matmul(dproj, sv["h"], mode="tn", out_dtype=BF16, tm=1664, tn=1024, tk=1024, name="proj_wgrad",
                   dep=dep)


def _proj_bwd(dproj, w, dep=None):
    return _matmul(dproj, w["w_in_t"], mode="nn", out_dtype=F32, tm=1024, tn=1024, tk=1664, name="proj_bwd",
                   dep=dep)


def _cols_from_chips(t):
    return jnp.transpose(t, (1, 0, 2)).reshape(t.shape[1], -1)


W_IN_ROWS = IN_WIDTH // N_CHIPS
W_IN_ROWS_PAD = -(-W_IN_ROWS // BF16_SUBLANES) * BF16_SUBLANES


def _w_in_t_shard(t):
    return jnp.pad(jnp.transpose(t, (0, 2, 1)), ((0, 0), (0, W_IN_ROWS_PAD - W_IN_ROWS), (0, 0)))


def _w_in_t_unshard(t):
    return jnp.transpose(t[:, :W_IN_ROWS], (0, 2, 1))


def _w_in_row_segments():
    runs = [(0, 0, FZ_ORIG), (FZ_ORIG, FZ_ORIG + GATE_RANK, COL_FZ - FZ_ORIG), (COL_FZ, FZ_ORIG, GATE_RANK)]
    out = []
    for kernel_row, ref_row, length in runs:
        while length:
            chip, local = divmod(ref_row, W_IN_ROWS)
            n = min(length, W_IN_ROWS - local)
            out.append((kernel_row, chip * W_IN_ROWS_PAD + local, n))
            kernel_row, ref_row, length = kernel_row + n, ref_row + n, length - n
    return out


def _permute_rows(src, n_out, segments, *, name):
    n_src, width = src.shape
    block = _row_tile(n_out)
    window = block + BF16_SUBLANES
    assert n_src >= window and src.dtype == BF16

    def body(src_ref, out_ref):
        rows = lax.broadcasted_iota(jnp.int32, (block, window), 0)
        cols = lax.broadcasted_iota(jnp.int32, (block, window), 1)
        for b0 in range(0, n_out, block):
            acc = None
            for o, s, n in segments:
                lo, hi = max(o, b0), min(o + n, b0 + block)
                if lo >= hi:
                    continue
                first = s + (lo - o)
                base = max(0, min(first // BF16_SUBLANES * BF16_SUBLANES, n_src - window))
                shift = (first - base) - (lo - b0)
                pick = (cols == rows + shift) & (rows >= lo - b0) & (rows < hi - b0)
                part = jnp.dot(jnp.where(pick, 1.0, 0.0).astype(BF16), src_ref[base:base + window, :],
                               preferred_element_type=F32)
                acc = part if acc is None else acc + part
            out_ref[b0:b0 + block, :] = (jnp.zeros((block, width), F32) if acc is None else acc).astype(BF16)

    return pl.pallas_call(
        body, name=name,
        out_shape=jax.ShapeDtypeStruct((n_out, width), BF16),
        in_specs=[pl.BlockSpec(memory_space=pltpu.VMEM)],
        out_specs=pl.BlockSpec(memory_space=pltpu.VMEM),
        compiler_params=_cparams(),
    )(src)


def _w_in_t_to_kernel(t):
    return _permute_rows(t.reshape(N_CHIPS * W_IN_ROWS_PAD, D_MODEL), PROJ_W, _w_in_row_segments(),
                         name="w_in_to_kernel_rows")


def _w_in_t_from_kernel(g):
    segments = [(slab, kernel_row, n) for kernel_row, slab, n in _w_in_row_segments()]
    out = _permute_rows(g, N_CHIPS * W_IN_ROWS_PAD, segments, name="w_in_to_chip_rows")
    return out.reshape(N_CHIPS, W_IN_ROWS_PAD, D_MODEL)


def kernel(x, norm1_g, w_in, w_fg2, b_fg, gla_norm_g, w_oa, conv_w, conv_b, w_ob, w_o, norm2_g, w_ffn_gate, w_ffn_up, w_ffn_down, final_g, loss_target, m_norm1_g, m_w_in, m_w_fg2, m_b_fg, m_gla_norm_g, m_w_oa, m_conv_w, m_conv_b, m_w_ob, m_w_o, m_norm2_g, m_w_ffn_gate, m_w_ffn_up, m_w_ffn_down, m_final_g, v_norm1_g, v_w_in, v_w_fg2, v_b_fg, v_gla_norm_g, v_w_oa, v_conv_w, v_conv_b, v_w_ob, v_w_o, v_norm2_g, v_w_ffn_gate, v_w_ffn_up, v_w_ffn_down, v_final_g):
    cx_ = lax.axis_index("x")
    cy_ = lax.axis_index("y")
    cc_ = lax.axis_index("c")
    me = 2 * cx_ + cy_
    on_core0 = jnp.where(cc_ == 0, 1.0, 0.0).astype(F32).reshape(1, 1)

    def swap(t):
        return jnp.swapaxes(t, 1, 2)

    big_names = ["w_in", "w_oa", "w_ob", "w_o", "w_ffn_gate", "w_ffn_up", "w_ffn_down"]
    views = dict(
        w_in=tuple(_w_in_t_shard(t) for t in (w_in, m_w_in, v_w_in)),
        w_oa=(w_oa, m_w_oa, v_w_oa), w_ob=(w_ob, m_w_ob, v_w_ob), w_o=(w_o, m_w_o, v_w_o),
        w_ffn_gate=tuple(swap(t) for t in (w_ffn_gate, m_w_ffn_gate, v_w_ffn_gate)),
        w_ffn_up=tuple(swap(t) for t in (w_ffn_up, m_w_ffn_up, v_w_ffn_up)),
        w_ffn_down=(w_ffn_down, m_w_ffn_down, v_w_ffn_down))
    from_view = dict(w_in=_w_in_t_unshard, w_ffn_gate=swap, w_ffn_up=swap)

    s = x.shape[1]

    conv_w_p = jnp.pad(conv_w, ((0, 0), (0, 8 - conv_w.shape[1]), (0, 0)))
    w_fg2_all, conv_w_all = _all_gather_weights([w_fg2, conv_w_p])

    proj_names = ["w_in"]
    rest_names = ["w_oa", "w_ob", "w_o", "w_ffn_gate", "w_ffn_up", "w_ffn_down"]
    mixer_names = ["w_oa", "w_ob", "w_o"]
    ffn_names = ["w_ffn_gate", "w_ffn_up", "w_ffn_down"]
    weight_key = dict(w_in="w_in_t", w_ffn_gate="w_gate_t", w_ffn_up="w_up_t")
    full_shape = dict(w_oa=(VAL_WIDTH, D_MODEL), w_ob=(CONV_CH, D_MODEL), w_o=(D_MODEL, D_MODEL),
                      w_ffn_gate=(FFN_HIDDEN, D_MODEL), w_ffn_up=(FFN_HIDDEN, D_MODEL),
                      w_ffn_down=(FFN_HIDDEN, D_MODEL))

    def shards_of(l, names):
        return [views[n][0][l].astype(BF16) for n in names]

    def small_weights(l):
        w_fg2_full = _cols_from_chips(w_fg2_all[l])
        conv_w_full = _cols_from_chips(conv_w_all[l])[:3]
        return dict(
            norm1_g=norm1_g[l], norm2_g=norm2_g[l],
            w_fg2=jnp.pad(w_fg2_full, ((0, FZ_PAD - GATE_RANK), (0, 0))).astype(BF16),
            b_fg=b_fg[l].reshape(1, KEY_WIDTH), gla_norm_g=gla_norm_g[l].reshape(1, HEAD_V),
            conv_wb=(jnp.pad(conv_w_full, ((0, 5), (0, 0)))
                     + jnp.pad(conv_b[l].reshape(1, CONV_CH), ((3, 4), (0, 0)))))

    def full_weights(names, gathered, shards):
        out = {}
        for n, g, t in zip(names, gathered, shards):
            g = lax.dynamic_update_index_in_dim(g, t[None], me, axis=0)
            out[weight_key.get(n, n)] = _w_in_t_to_kernel(g) if n == "w_in" else g.reshape(full_shape[n])
        return out

    def per_chip(g, names):
        out = []
        for n in names:
            t = g[weight_key.get(n, n)]
            out.append(_w_in_t_from_kernel(t) if n == "w_in"
                       else t.reshape(N_CHIPS, t.shape[0] // N_CHIPS, t.shape[1]))
        return out

    def gather(l, names, tag, after=None):
        shards = shards_of(l, names)
        return shards, _gather_start(l, shards, name=f"gather_{tag}_start", after=after)

    def gathered_to_sibling(l, started, after, tag):
        return _handoff_start(l, _gather_wait(l, started, after, name=f"gather_{tag}_wait"),
                              name=f"handoff_{tag}_start")

    def feed(l, g, names, tag):
        return _sibling_push_start(l, False, per_chip(g, names), name=f"feed_{tag}_start")

    def pair_and_exchange(l, fed, after, names, tag):
        mine, sib = _sibling_push_wait(l, False, fed, after, name=f"feed_{tag}_wait")
        parts = [_pair_sum(a, b, name=f"pair_sum_{tag}_{n}") for n, a, b in zip(names, mine, sib)]
        return _chip_exchange_start(l, parts, name=f"exchange_{tag}_start")

    def total_and_share(l, swapped, after, names, tag):
        parts, recvs = _chip_exchange_wait(l, swapped, after, name=f"exchange_{tag}_wait")
        owns = [lax.dynamic_index_in_dim(p, me, 0, keepdims=False) for p in parts]
        totals = [_total_sum(o, rc, name=f"total_{tag}_{n}") for n, o, rc in zip(names, owns, recvs)]
        return _sibling_push_start(l, True, totals, name=f"share_{tag}_start")

    def shared(l, sharing, after, names, tag):
        totals, others = _sibling_push_wait(l, True, sharing, after, name=f"share_{tag}_wait")
        return {n: (t, o) for n, t, o in zip(names, totals, others)}

    def branches_under_gather(l, xin, h, w, tag, shards_r, started_r):
        box = {}

        def before_conv(oa):
            box["handoff"] = gathered_to_sibling(l, started_r, oa, tag)
            return box["handoff"][3]

        sv = _mixer_forward_branches(xin, h, w, s, dep=started_r[3], before_conv=before_conv)
        rest = _handoff_wait(l, box["handoff"], sv["cb_in"], name=f"handoff_{tag}_wait")
        w.update(full_weights(rest_names, rest, shards_r))
        return sv

    shards_p, started = gather(0, proj_names, "p0")
    h0 = _rms_fwd(x[0], norm1_g[0], name="rms1_fwd")
    started = gathered_to_sibling(0, started, (h0, views["w_in"][1]), "p0")
    w0 = small_weights(0)
    w0.update(full_weights(proj_names, _handoff_wait(0, started, views["w_in"][2], name="handoff_p0_wait"),
                           shards_p))
    sv0 = branches_under_gather(0, x[0], h0, w0, "r0", *gather(0, rest_names, "r0", after=w0["w_in_t"]))

    shards_p1, started = gather(1, proj_names, "p1", after=w0["w_o"])
    x1, sv0m = _mixer_forward_out(sv0, w0, s, dep=started[3])
    box1 = {}

    def after_gate(gt):
        box1["handoff"] = gathered_to_sibling(1, started, gt, "p1")
        box1["rest"] = gather(1, rest_names, "r1", after=box1["handoff"][2][0])
        return box1["handoff"][3] + box1["rest"][1][3]

    sv0f = _ffn_forward_hidden(x1, w0, s, after_gate=after_gate)
    x2 = _ffn_forward_out(sv0f, w0)
    w1 = small_weights(1)
    w1.update(full_weights(proj_names, _handoff_wait(1, box1["handoff"], x2, name="handoff_p1_wait"), shards_p1))

    h1 = _rms_fwd(x2, norm1_g[1], name="rms1_fwd")
    x3, sv1m = _mixer_forward_out(branches_under_gather(1, x2, h1, w1, "r1", *box1["rest"]), w1, s)
    sv1f = _ffn_forward_hidden(x3, w1, s)
    x4 = _ffn_forward_out(sv1f, w1)
    loss_local, dx, dxb, dgf = _loss_head(x4, final_g, loss_target[0])
    loss = lax.psum(loss_local[0, 0], ("x", "y", "c"))

    g1, dgt, dup, _ = _ffn_backward_grads(dxb, w1, sv1f, s)
    dx_mid, dxb_mid, g1["norm2_g"] = _ffn_backward_input(dgt, dup, dx, w1, sv1f)
    gm, dproj = _mixer_backward_branches(dxb_mid, w1, sv1m, s)
    g1.update(gm)
    g1["w_in_t"] = _proj_wgrad(dproj, sv1m)
    fed_1 = feed(1, g1, big_names, "l1")
    dh = _proj_bwd(dproj, w1, dep=fed_1[3])
    dx, dxb, g1["norm1_g"] = _rms_bwd(sv1m["x0"], norm1_g[1], dh, dx_mid, name="rms1_bwd")

    g0, dgt, dup, dhid = _ffn_backward_grads(dxb, w0, sv0f, s)
    swap_1 = pair_and_exchange(1, fed_1, dhid, big_names, "l1")
    fed_f = feed(0, g0, ffn_names, "f0")
    dx_mid, dxb_mid, g0["norm2_g"] = _ffn_backward_input(dgt, dup, dx, w0, sv0f, dep=swap_1[3] + fed_f[3])
    swap_f = pair_and_exchange(0, fed_f, dx_mid, ffn_names, "f0")
    gm, dproj = _mixer_backward_branches(dxb_mid, w0, sv0m, s, dep=swap_f[3])
    g0.update(gm)
    share_1 = total_and_share(1, swap_1, dproj, big_names, "l1")
    share_f = total_and_share(0, swap_f, dproj, ffn_names, "f0")
    fed_m = feed(0, g0, mixer_names, "m0")
    g0["w_in_t"] = _proj_wgrad(dproj, sv0m, dep=share_1[3] + share_f[3] + fed_m[3])
    fed_p = feed(0, g0, proj_names, "p0")
    swap_m = pair_and_exchange(0, fed_m, g0["w_in_t"], mixer_names, "m0")
    dh = _proj_bwd(dproj, w0, dep=fed_p[3] + swap_m[3])
    reduced1 = shared(1, share_1, dh, big_names, "l1")
    reduced0 = shared(0, share_f, dh, ffn_names, "f0")
    grad_x, _, g0["norm1_g"] = _rms_bwd(sv0m["x0"], norm1_g[0], dh, dx_mid, name="rms1_bwd")

    swap_p = pair_and_exchange(0, fed_p, grad_x, proj_names, "p0")
    share_m = total_and_share(0, swap_m, swap_p[3], mixer_names, "m0")
    reduced0.update(shared(0, share_m, share_m[3], mixer_names, "m0"))
    grads = [g0, g1]

    def small_rows(t):
        return t.reshape(-1, D_MODEL)

    def tile_rows(t):
        return jnp.pad(t, ((0, -t.shape[0] % F32_SUBLANES), (0, 0)))

    g0, g1 = grads
    pieces = [
        jnp.concatenate([g0["norm1_g"], g1["norm1_g"]], axis=0),
        jnp.concatenate([g0["norm2_g"], g1["norm2_g"]], axis=0),
        dgf,
        small_rows(jnp.concatenate([g0["b_fg"], g1["b_fg"]], axis=1)),
        small_rows(jnp.concatenate([g0["gla_norm_g"], g1["gla_norm_g"],
                                    jnp.zeros((1, D_MODEL - 2 * HEAD_V), F32)], axis=1)),
        jnp.concatenate([g0["conv_wb"][3:4], g1["conv_wb"][3:4]], axis=0),
        jnp.concatenate([g0["conv_wb"][:3], g1["conv_wb"][:3]], axis=0),
        small_rows(jnp.stack([g0["w_fg2"][:GATE_RANK], g1["w_fg2"][:GATE_RANK]])),
    ]
    small = _all_reduce_small(jnp.concatenate([tile_rows(p) for p in pieces], axis=0))
    sg = dict(
        norm1_g=small[0:2], norm2_g=small[8:10], final_g=small[16],
        b_fg=small[24].reshape(DEPTH, KEY_WIDTH), gla_norm_g=small[32, :DEPTH * HEAD_V].reshape(DEPTH, HEAD_V),
        conv_b=small[40:42],
        conv_w=lax.dynamic_slice_in_dim(small[48:54].reshape(DEPTH, 3, CONV_CH), me * (CONV_CH // N_CHIPS),
                                        CONV_CH // N_CHIPS, axis=2),
        w_fg2=lax.dynamic_slice_in_dim(small[56:72].reshape(DEPTH, GATE_RANK, KEY_WIDTH),
                                       me * (KEY_WIDTH // N_CHIPS), KEY_WIDTH // N_CHIPS, axis=2),
    )

    small_params = dict(norm1_g=(norm1_g, m_norm1_g, v_norm1_g), w_fg2=(w_fg2, m_w_fg2, v_w_fg2),
                        b_fg=(b_fg, m_b_fg, v_b_fg), gla_norm_g=(gla_norm_g, m_gla_norm_g, v_gla_norm_g),
                        conv_w=(conv_w, m_conv_w, v_conv_w), conv_b=(conv_b, m_conv_b, v_conv_b),
                        norm2_g=(norm2_g, m_norm2_g, v_norm2_g), final_g=(final_g, m_final_g, v_final_g))
    order = ["norm1_g", "w_in", "w_fg2", "b_fg", "gla_norm_g", "w_oa", "conv_w", "conv_b", "w_ob", "w_o",
             "norm2_g", "w_ffn_gate", "w_ffn_up", "w_ffn_down", "final_g"]
    results = {}

    def update_large(n):
        w_, m_, v_ = views[n]
        outs = _adamw_layers(w_, (reduced0[n][0], reduced1[n][0]), (reduced0[n][1], reduced1[n][1]), on_core0,
                             m_, v_, name="adamw_" + n)
        back = from_view.get(n)
        results[n] = tuple(back(r) for r in outs) if back else outs
        return outs[1]

    for n, (w_, m_, v_) in small_params.items():
        results[n] = (sg[n],) + _adamw(w_, sg[n], m_, v_, name="adamw_" + n)
    updated = tuple(update_large(n) for n in rest_names)
    share_p = total_and_share(0, swap_p, updated, proj_names, "p0")
    reduced0.update(shared(0, share_p, share_p[3], proj_names, "p0"))
    update_large("w_in")
    return (loss, grad_x[None], *[results[n][0] for n in order], *[results[n][1] for n in order],
            *[results[n][2] for n in order], *[results[n][3] for n in order])
```

```python
import functools

import jax
import jax.numpy as jnp
from jax import lax
from jax.experimental import pallas as pl
from jax.experimental.pallas import tpu as pltpu

F32 = jnp.float32
BF16 = jnp.bfloat16

D_MODEL = 1024
DEPTH = 2
CHUNK = 64
GLA_HEADS = 4
KEY_WIDTH = 512
VAL_WIDTH = 1024
HEAD_K = 128
HEAD_V = 256
GATE_RANK = 16
GATE_TAU = 16.0
CONV_CH = 1024
FFN_HIDDEN = 2816
IN_WIDTH = 8208
NORM_EPS = 1e-6
Q_SCALE = HEAD_K ** -0.5
ADAM_LR = 0.001
ADAM_B1 = 0.9
ADAM_B2 = 0.999
ADAM_EPS = 1e-08
ADAM_WD = 0.01
ADAM_STEP = 10

N_CHIPS = 4
N_DEV = 8

LANE = 128
FZ_PAD = LANE
PROJ_W = 8192 + FZ_PAD
COL_Q, COL_K, COL_V, COL_R, COL_GBI, COL_GCI, COL_CX, COL_GA, COL_GB, COL_FZ = (
    0, 512, 1024, 2048, 3072, 4096, 5120, 6144, 7168, 8192)
FZ_ORIG = 3072

VMEM_LIMIT = 52 * 1024 * 1024
HALO = 16


def _cparams(sem=None):
    return pltpu.CompilerParams(dimension_semantics=sem, vmem_limit_bytes=VMEM_LIMIT)


def _sigmoid(x):
    return jax.nn.sigmoid(x)


def _log_sigmoid(x):
    return jnp.minimum(x, 0.0) - jnp.log1p(jnp.exp(-jnp.abs(x)))


_DOT_DIMS = {
    "nn": (((1,), (0,)), ((), ())),
    "nt": (((1,), (1,)), ((), ())),
    "tn": (((0,), (0,)), ((), ())),
}


def _matmul(a, b, *, mode, out_dtype, tm, tn, tk, name, add=None, dep=None):
    if mode == "nn":
        (m, k), n = a.shape, b.shape[1]
    elif mode == "nt":
        (m, k), n = a.shape, b.shape[0]
    else:
        (k, m), n = a.shape, b.shape[1]
    tm, tn, tk = min(tm, m), min(tn, n), min(tk, k)
    assert m % tm == 0 and n % tn == 0 and k % tk == 0, (name, m, n, k, tm, tn, tk)
    nk = k // tk
    has_add = add is not None
    has_dep = dep is not None

    def body(*refs):
        if has_dep:
            refs = refs[1:]
        if has_add:
            a_ref, b_ref, add_ref, o_ref = refs[:4]
            scratch = refs[4:]
        else:
            a_ref, b_ref, o_ref = refs[:3]
            add_ref = None
            scratch = refs[3:]
        part = lax.dot_general(a_ref[...], b_ref[...], _DOT_DIMS[mode], preferred_element_type=F32)

        def finish(acc):
            if add_ref is not None:
                acc = acc + add_ref[...].astype(F32)
            o_ref[...] = acc.astype(o_ref.dtype)

        if nk == 1:
            finish(part)
        else:
            acc_ref = scratch[0]
            kk = pl.program_id(2)

            @pl.when(kk == 0)
            def _():
                acc_ref[...] = part

            @pl.when(kk > 0)
            def _():
                acc_ref[...] += part

            @pl.when(kk == nk - 1)
            def _():
                finish(acc_ref[...])

    if mode == "nn":
        a_spec = pl.BlockSpec((tm, tk), lambda i, j, kk: (i, kk))
        b_spec = pl.BlockSpec((tk, tn), lambda i, j, kk: (kk, j))
    elif mode == "nt":
        a_spec = pl.BlockSpec((tm, tk), lambda i, j, kk: (i, kk))
        b_spec = pl.BlockSpec((tn, tk), lambda i, j, kk: (j, kk))
    else:
        a_spec = pl.BlockSpec((tk, tm), lambda i, j, kk: (kk, i))
        b_spec = pl.BlockSpec((tk, tn), lambda i, j, kk: (kk, j))
    o_spec = pl.BlockSpec((tm, tn), lambda i, j, kk: (i, j))
    in_specs = [a_spec, b_spec] + ([o_spec] if has_add else [])
    operands = (a, b) + ((add,) if has_add else ())
    if has_dep:
        in_specs = [pl.BlockSpec(dep.shape, lambda i, j, kk: (0, 0))] + in_specs
        operands = (dep,) + operands
    return pl.pallas_call(
        body,
        name=name,
        out_shape=jax.ShapeDtypeStruct((m, n), out_dtype),
        grid=(m // tm, n // tn, nk),
        in_specs=in_specs,
        out_specs=o_spec,
        scratch_shapes=[pltpu.VMEM((tm, tn), F32)] if nk > 1 else [],
        compiler_params=_cparams(("parallel", "parallel", "arbitrary")),
    )(*operands)


def _rms_fwd(x, g, *, name):
    s, d = x.shape
    tr = min(512, s)

    def body(x_ref, g_ref, h_ref):
        xv = x_ref[...]
        rs = lax.rsqrt(jnp.mean(xv * xv, axis=-1, keepdims=True) + NORM_EPS)
        h_ref[...] = (xv * rs * g_ref[...]).astype(BF16)

    return pl.pallas_call(
        body, name=name,
        out_shape=jax.ShapeDtypeStruct((s, d), BF16),
        grid=(s // tr,),
        in_specs=[pl.BlockSpec((tr, d), lambda i: (i, 0)), pl.BlockSpec((1, d), lambda i: (0, 0))],
        out_specs=pl.BlockSpec((tr, d), lambda i: (i, 0)),
        compiler_params=_cparams(("parallel",)),
    )(x, g.reshape(1, d))


def _rms_bwd(x, g, dh, dres, *, name):
    s, d = x.shape
    tr = min(512, s)

    def body(x_ref, g_ref, dh_ref, dres_ref, dx_ref, dxb_ref, dg_ref):
        i = pl.program_id(0)
        xv = x_ref[...]
        rs = lax.rsqrt(jnp.mean(xv * xv, axis=-1, keepdims=True) + NORM_EPS)
        n = xv * rs
        dhv = dh_ref[...].astype(F32)
        dn = dhv * g_ref[...]
        dx = dres_ref[...] + rs * (dn - n * jnp.mean(dn * n, axis=-1, keepdims=True))
        dx_ref[...] = dx
        dxb_ref[...] = dx.astype(BF16)
        part = jnp.sum(dhv * n, axis=0, keepdims=True)

        @pl.when(i == 0)
        def _():
            dg_ref[...] = part

        @pl.when(i > 0)
        def _():
            dg_ref[...] += part

    row = pl.BlockSpec((tr, d), lambda i: (i, 0))
    vec = pl.BlockSpec((1, d), lambda i: (0, 0))
    return pl.pallas_call(
        body, name=name,
        out_shape=(jax.ShapeDtypeStruct((s, d), F32), jax.ShapeDtypeStruct((s, d), BF16),
                   jax.ShapeDtypeStruct((1, d), F32)),
        grid=(s // tr,),
        in_specs=[row, vec, row, row],
        out_specs=(row, row, vec),
        compiler_params=_cparams(("arbitrary",)),
    )(x, g.reshape(1, d), dh, dres)


def _loss_head(x, g, target):
    s, d = x.shape
    tr = min(512, s)

    def body(x_ref, g_ref, t_ref, loss_ref, dx_ref, dxb_ref, dg_ref):
        i = pl.program_id(0)
        xv = x_ref[...]
        rs = lax.rsqrt(jnp.mean(xv * xv, axis=-1, keepdims=True) + NORM_EPS)
        n = xv * rs
        gv = g_ref[...]
        err = n * gv - t_ref[...]
        row_loss = jnp.mean(err * err, axis=-1, keepdims=True)
        loss_part = 0.5 * jnp.sum(row_loss, axis=0, keepdims=True)
        dy = err * (1.0 / d)
        dn = dy * gv
        dx = rs * (dn - n * jnp.mean(dn * n, axis=-1, keepdims=True))
        dx_ref[...] = dx
        dxb_ref[...] = dx.astype(BF16)
        dg_part = jnp.sum(dy * n, axis=0, keepdims=True)

        @pl.when(i == 0)
        def _():
            loss_ref[...] = loss_part
            dg_ref[...] = dg_part

        @pl.when(i > 0)
        def _():
            loss_ref[...] += loss_part
            dg_ref[...] += dg_part

    row = pl.BlockSpec((tr, d), lambda i: (i, 0))
    vec = pl.BlockSpec((1, d), lambda i: (0, 0))
    one = pl.BlockSpec((1, 1), lambda i: (0, 0))
    return pl.pallas_call(
        body, name="loss_head",
        out_shape=(jax.ShapeDtypeStruct((1, 1), F32), jax.ShapeDtypeStruct((s, d), F32),
                   jax.ShapeDtypeStruct((s, d), BF16), jax.ShapeDtypeStruct((1, d), F32)),
        grid=(s // tr,),
        in_specs=[row, vec, row],
        out_specs=(one, row, row, vec),
        compiler_params=_cparams(("arbitrary",)),
    )(x, g.reshape(1, d), target)


def _conv_taps(u_prev, u, w_ref, rows):
    ext = jnp.concatenate([u_prev, u], axis=0)
    u1 = pltpu.roll(ext, 1, 0)[HALO:HALO + rows]
    u2 = pltpu.roll(ext, 2, 0)[HALO:HALO + rows]
    conv = w_ref[0:1, :] * u2 + w_ref[1:2, :] * u1 + w_ref[2:3, :] * u + w_ref[3:4, :]
    return conv, u1, u2


def _conv_fwd(proj, conv_wb, *, s):
    tr = min(512, s)
    c = CONV_CH
    hb = tr // HALO

    def body(gbi_ref, gci_ref, cx_ref, gci_h_ref, cx_h_ref, w_ref, o_ref):
        i = pl.program_id(0)
        u = gci_ref[...].astype(F32) * cx_ref[...].astype(F32)
        u_prev = gci_h_ref[...].astype(F32) * cx_h_ref[...].astype(F32)
        u_prev = jnp.where(i == 0, 0.0, u_prev)
        conv, _, _ = _conv_taps(u_prev, u, w_ref, tr)
        o_ref[...] = (gbi_ref[...].astype(F32) * conv).astype(BF16)

    def seg(col):
        return pl.BlockSpec((tr, c), lambda i: (i, col // c))

    def halo(col):
        return pl.BlockSpec((HALO, c), lambda i: (jnp.maximum(i * hb - 1, 0), col // c))

    return pl.pallas_call(
        body, name="conv_fwd",
        out_shape=jax.ShapeDtypeStruct((s, c), BF16),
        grid=(s // tr,),
        in_specs=[seg(COL_GBI), seg(COL_GCI), seg(COL_CX), halo(COL_GCI), halo(COL_CX),
                  pl.BlockSpec((8, c), lambda i: (0, 0))],
        out_specs=pl.BlockSpec((tr, c), lambda i: (i, 0)),
        compiler_params=_cparams(("parallel",)),
    )(proj, proj, proj, proj, proj, conv_wb)


def _conv_bwd(proj, dcb, conv_wb, dproj, *, s):
    tr = min(512, s)
    c = CONV_CH
    hb = tr // HALO
    nb = s // tr

    def body(gbi_ref, gci_ref, cx_ref, dcb_ref, gci_h_ref, cx_h_ref, gbi_n_ref, dcb_n_ref, w_ref,
             dproj_in_ref, dp_ref, dw_ref):
        i = pl.program_id(0)
        gbi = gbi_ref[...].astype(F32)
        gci = gci_ref[...].astype(F32)
        cx = cx_ref[...].astype(F32)
        dcb_v = dcb_ref[...].astype(F32)
        u = gci * cx
        u_prev = jnp.where(i == 0, 0.0, gci_h_ref[...].astype(F32) * cx_h_ref[...].astype(F32))
        conv, u1, u2 = _conv_taps(u_prev, u, w_ref, tr)
        dconv = dcb_v * gbi
        dconv_next = jnp.where(i == nb - 1, 0.0, dcb_n_ref[...].astype(F32) * gbi_n_ref[...].astype(F32))
        ext = jnp.concatenate([dconv, dconv_next], axis=0)
        n_ext = tr + HALO
        d1 = pltpu.roll(ext, n_ext - 1, 0)[0:tr]
        d2 = pltpu.roll(ext, n_ext - 2, 0)[0:tr]
        du = w_ref[2:3, :] * dconv + w_ref[1:2, :] * d1 + w_ref[0:1, :] * d2
        dp_ref[:, 0:c] = (dcb_v * conv).astype(BF16)
        dp_ref[:, c:2 * c] = (du * cx).astype(BF16)
        dp_ref[:, 2 * c:3 * c] = (du * gci).astype(BF16)
        part = jnp.concatenate([
            jnp.sum(dconv * u2, axis=0, keepdims=True),
            jnp.sum(dconv * u1, axis=0, keepdims=True),
            jnp.sum(dconv * u, axis=0, keepdims=True),
            jnp.sum(dconv, axis=0, keepdims=True),
            jnp.zeros((4, c), F32)], axis=0)

        @pl.when(i == 0)
        def _():
            dw_ref[...] = part

        @pl.when(i > 0)
        def _():
            dw_ref[...] += part

    def seg(col):
        return pl.BlockSpec((tr, c), lambda i: (i, col // c))

    def halo_prev(col):
        return pl.BlockSpec((HALO, c), lambda i: (jnp.maximum(i * hb - 1, 0), col // c))

    def halo_next(col):
        return pl.BlockSpec((HALO, c), lambda i: (jnp.minimum((i + 1) * hb, nb * hb - 1), col // c))

    return pl.pallas_call(
        body, name="conv_bwd",
        out_shape=(jax.ShapeDtypeStruct((s, PROJ_W), BF16), jax.ShapeDtypeStruct((8, c), F32)),
        grid=(nb,),
        in_specs=[seg(COL_GBI), seg(COL_GCI), seg(COL_CX), pl.BlockSpec((tr, c), lambda i: (i, 0)),
                  halo_prev(COL_GCI), halo_prev(COL_CX), halo_next(COL_GBI),
                  pl.BlockSpec((HALO, c), lambda i: (jnp.minimum((i + 1) * hb, nb * hb - 1), 0)),
                  pl.BlockSpec((8, c), lambda i: (0, 0)), ANY],
        out_specs=(pl.BlockSpec((tr, 3 * c), lambda i: (i, COL_GBI // (3 * c))),
                   pl.BlockSpec((8, c), lambda i: (0, 0))),
        input_output_aliases={9: 0},
        compiler_params=_cparams(("arbitrary",)),
    )(proj, proj, proj, dcb, proj, proj, proj, dcb, conv_wb, dproj)


def _mix_fwd(proj, ya, yb, *, s):
    tr = min(512, s)
    d = D_MODEL

    def body(ga_ref, gb_ref, ya_ref, yb_ref, o_ref):
        sa = _sigmoid(ga_ref[...].astype(F32))
        sb = _sigmoid(gb_ref[...].astype(F32))
        o_ref[...] = (sa * ya_ref[...].astype(F32) + sb * yb_ref[...].astype(F32)).astype(BF16)

    row = pl.BlockSpec((tr, d), lambda i: (i, 0))
    return pl.pallas_call(
        body, name="mix_fwd",
        out_shape=jax.ShapeDtypeStruct((s, d), BF16),
        grid=(s // tr,),
        in_specs=[pl.BlockSpec((tr, d), lambda i: (i, COL_GA // d)),
                  pl.BlockSpec((tr, d), lambda i: (i, COL_GB // d)), row, row],
        out_specs=row,
        compiler_params=_cparams(("parallel",)),
    )(proj, proj, ya, yb)


def _mix_bwd(proj, ya, yb, dmix, *, s):
    tr = min(512, s)
    d = D_MODEL

    def body(ga_ref, gb_ref, ya_ref, yb_ref, dm_ref, dya_ref, dyb_ref, dg_ref):
        sa = _sigmoid(ga_ref[...].astype(F32))
        sb = _sigmoid(gb_ref[...].astype(F32))
        dm = dm_ref[...].astype(F32)
        dya_ref[...] = (dm * sa).astype(BF16)
        dyb_ref[...] = (dm * sb).astype(BF16)
        dg_ref[:, 0:d] = (dm * ya_ref[...].astype(F32) * sa * (1.0 - sa)).astype(BF16)
        dg_ref[:, d:2 * d] = (dm * yb_ref[...].astype(F32) * sb * (1.0 - sb)).astype(BF16)

    row = pl.BlockSpec((tr, d), lambda i: (i, 0))
    return pl.pallas_call(
        body, name="mix_bwd",
        out_shape=(jax.ShapeDtypeStruct((s, d), BF16), jax.ShapeDtypeStruct((s, d), BF16),
                   jax.ShapeDtypeStruct((s, PROJ_W), BF16)),
        grid=(s // tr,),
        in_specs=[pl.BlockSpec((tr, d), lambda i: (i, COL_GA // d)),
                  pl.BlockSpec((tr, d), lambda i: (i, COL_GB // d)), row, row, row],
        out_specs=(row, row, pl.BlockSpec((tr, 2 * d), lambda i: (i, COL_GA // (2 * d)))),
        compiler_params=_cparams(("parallel",)),
    )(proj, proj, ya, yb, dmix)


def _swiglu_fwd(gt, up, *, s):
    tr = min(256, s)
    f = FFN_HIDDEN

    def body(gt_ref, up_ref, o_ref):
        gv = gt_ref[...].astype(F32)
        o_ref[...] = (gv * _sigmoid(gv) * up_ref[...].astype(F32)).astype(BF16)

    row = pl.BlockSpec((tr, f), lambda i: (i, 0))
    return pl.pallas_call(
        body, name="swiglu_fwd",
        out_shape=jax.ShapeDtypeStruct((s, f), BF16),
        grid=(s // tr,),
        in_specs=[row, row], out_specs=row,
        compiler_params=_cparams(("parallel",)),
    )(gt, up)


def _swiglu_bwd(gt, up, dhid, *, s):
    tr = min(256, s)
    f = FFN_HIDDEN

    def body(gt_ref, up_ref, dh_ref, dgt_ref, dup_ref):
        gv = gt_ref[...].astype(F32)
        uv = up_ref[...].astype(F32)
        dh = dh_ref[...].astype(F32)
        sg = _sigmoid(gv)
        dgt_ref[...] = (dh * uv * sg * (1.0 + gv * (1.0 - sg))).astype(BF16)
        dup_ref[...] = (dh * gv * sg).astype(BF16)

    row = pl.BlockSpec((tr, f), lambda i: (i, 0))
    return pl.pallas_call(
        body, name="swiglu_bwd",
        out_shape=(jax.ShapeDtypeStruct((s, f), BF16), jax.ShapeDtypeStruct((s, f), BF16)),
        grid=(s // tr,),
        in_specs=[row, row, row], out_specs=(row, row),
        compiler_params=_cparams(("parallel",)),
    )(gt, up, dhid)


def _tri(strict):
    r = lax.broadcasted_iota(jnp.int32, (CHUNK, CHUNK), 0)
    c = lax.broadcasted_iota(jnp.int32, (CHUNK, CHUNK), 1)
    return jnp.where((c < r) if strict else (c <= r), 1.0, 0.0).astype(F32)


def _gla_gate_terms(la_c, tri):
    cum = jnp.dot(tri, la_c, precision=lax.Precision.HIGHEST, preferred_element_type=F32)
    cend = cum[CHUNK - 1:CHUNK, :]
    return jnp.exp(cend - cum), jnp.exp(cend)


def _hk(h):
    return slice(h * HEAD_K, (h + 1) * HEAD_K)


def _hv(h):
    return slice(h * HEAD_V, (h + 1) * HEAD_V)


def _gla_fwd(proj, wfg2p, bfg, gn, *, s):
    tr = min(512, s)
    nb = s // tr
    nc = tr // CHUNK

    def body(q_ref, k_ref, v_ref, r_ref, fz_ref, w_ref, b_ref, gn_ref, oa_ref, st_ref,
             state, la_scr, o_scr):
        i = pl.program_id(0)

        @pl.when(i == 0)
        def _():
            state[...] = jnp.zeros_like(state)

        fg = jnp.dot(fz_ref[...], w_ref[...], preferred_element_type=F32) + b_ref[...]
        la_scr[...] = _log_sigmoid(fg) * (1.0 / GATE_TAU)
        tri = _tri(False)

        def chunk(ci, carry):
            r0 = pl.multiple_of(ci * CHUNK, CHUNK)
            rows = pl.ds(r0, CHUNK)
            e, gam = _gla_gate_terms(la_scr[rows, :], tri)
            kd = (k_ref[rows, :].astype(F32) * e).astype(BF16)
            qs = (q_ref[rows, :].astype(F32) * Q_SCALE).astype(BF16)
            v_c = v_ref[rows, :]
            for h in range(GLA_HEADS):
                upd = lax.dot_general(v_c[:, _hv(h)], kd[:, _hk(h)], _DOT_DIMS["tn"],
                                      preferred_element_type=F32)
                st_h = state[:, _hk(h)] * gam[:, _hk(h)] + upd
                state[:, _hk(h)] = st_h
                o_scr[rows, _hv(h)] = lax.dot_general(qs[:, _hk(h)], st_h.astype(BF16), _DOT_DIMS["nt"],
                                                       preferred_element_type=F32)
            st_ref[ci] = state[...]
            return carry

        lax.fori_loop(0, nc, chunk, 0, unroll=True)
        for h in range(GLA_HEADS):
            o = o_scr[:, _hv(h)]
            rs = lax.rsqrt(jnp.mean(o * o, axis=-1, keepdims=True) + NORM_EPS)
            rv = r_ref[:, _hv(h)].astype(F32)
            oa_ref[:, _hv(h)] = ((o * rs * gn_ref[...]).astype(F32) * (rv * _sigmoid(rv))).astype(BF16)

    return pl.pallas_call(
        body, name="gla_fwd",
        out_shape=(jax.ShapeDtypeStruct((s, VAL_WIDTH), BF16),
                   jax.ShapeDtypeStruct((s // CHUNK, HEAD_V, KEY_WIDTH), F32)),
        grid=(nb,),
        in_specs=[pl.BlockSpec((tr, KEY_WIDTH), lambda i: (i, COL_Q // KEY_WIDTH)),
                  pl.BlockSpec((tr, KEY_WIDTH), lambda i: (i, COL_K // KEY_WIDTH)),
                  pl.BlockSpec((tr, VAL_WIDTH), lambda i: (i, COL_V // VAL_WIDTH)),
                  pl.BlockSpec((tr, VAL_WIDTH), lambda i: (i, COL_R // VAL_WIDTH)),
                  pl.BlockSpec((tr, FZ_PAD), lambda i: (i, COL_FZ // FZ_PAD)),
                  pl.BlockSpec((FZ_PAD, KEY_WIDTH), lambda i: (0, 0)),
                  pl.BlockSpec((1, KEY_WIDTH), lambda i: (0, 0)),
                  pl.BlockSpec((1, HEAD_V), lambda i: (0, 0))],
        out_specs=(pl.BlockSpec((tr, VAL_WIDTH), lambda i: (i, 0)),
                   pl.BlockSpec((nc, HEAD_V, KEY_WIDTH), lambda i: (i, 0, 0))),
        scratch_shapes=[pltpu.VMEM((HEAD_V, KEY_WIDTH), F32), pltpu.VMEM((tr, KEY_WIDTH), F32),
                        pltpu.VMEM((tr, VAL_WIDTH), F32)],
        compiler_params=_cparams(("arbitrary",)),
    )(proj, proj, proj, proj, proj, wfg2p, bfg, gn)


def _gla_bwd(proj, doa, states, wfg2p, bfg, gn, dproj, *, s):
    tr = min(512, s)
    nb = s // tr
    nc = tr // CHUNK

    def body(q_ref, k_ref, v_ref, r_ref, fz_ref, doa_ref, st_ref, stp_ref, w_ref, b_ref, gn_ref, dproj_in_ref,
             dp_ref, dfz_ref, dgn_ref, dw_ref, db_ref,
             carry, fg_scr, la_scr, dla_scr, o_scr, do_scr):
        i = pl.program_id(0)

        @pl.when(i == 0)
        def _():
            carry[...] = jnp.zeros_like(carry)
            dgn_ref[...] = jnp.zeros_like(dgn_ref)
            dw_ref[...] = jnp.zeros_like(dw_ref)
            db_ref[...] = jnp.zeros_like(db_ref)

        fz = fz_ref[...]
        fg = jnp.dot(fz, w_ref[...], preferred_element_type=F32) + b_ref[...]
        fg_scr[...] = fg
        la_scr[...] = _log_sigmoid(fg) * (1.0 / GATE_TAU)

        def fwd_chunk(ci, c0):
            r0 = pl.multiple_of(ci * CHUNK, CHUNK)
            rows = pl.ds(r0, CHUNK)
            qs = (q_ref[rows, :].astype(F32) * Q_SCALE).astype(BF16)
            st = st_ref[ci].astype(BF16)
            for h in range(GLA_HEADS):
                o_scr[rows, _hv(h)] = lax.dot_general(qs[:, _hk(h)], st[:, _hk(h)], _DOT_DIMS["nt"],
                                                       preferred_element_type=F32)
            return c0

        lax.fori_loop(0, nc, fwd_chunk, 0, unroll=True)

        gnv = gn_ref[...]
        dgn_part = jnp.zeros((1, HEAD_V), F32)
        for h in range(GLA_HEADS):
            o = o_scr[:, _hv(h)]
            rs = lax.rsqrt(jnp.mean(o * o, axis=-1, keepdims=True) + NORM_EPS)
            nrm = o * rs
            rv = r_ref[:, _hv(h)].astype(F32)
            sg = _sigmoid(rv)
            doa_h = doa_ref[:, _hv(h)].astype(F32)
            don = doa_h * (rv * sg)
            dp_ref[:, COL_R + h * HEAD_V:COL_R + (h + 1) * HEAD_V] = (
                doa_h * (nrm * gnv) * (sg * (1.0 + rv * (1.0 - sg)))).astype(BF16)
            dgn_part = dgn_part + jnp.sum(don * nrm, axis=0, keepdims=True)
            dn = don * gnv
            do_scr[:, _hv(h)] = (rs * (dn - nrm * jnp.mean(dn * nrm, axis=-1, keepdims=True))).astype(BF16)
        dgn_ref[...] += dgn_part

        tri = _tri(False)
        tri_s = _tri(True)
        first_block = i == nb - 1

        def bwd_chunk(cc, c0):
            ci = nc - 1 - cc
            r0 = pl.multiple_of(ci * CHUNK, CHUNK)
            rows = pl.ds(r0, CHUNK)
            e, gam = _gla_gate_terms(la_scr[rows, :], tri)
            k_c = k_ref[rows, :].astype(F32)
            kd = k_c * e
            kd_b = kd.astype(BF16)
            qs = (q_ref[rows, :].astype(F32) * Q_SCALE).astype(BF16)
            v_c = v_ref[rows, :]
            do_c = do_scr[rows, :]
            st = st_ref[ci]
            st_b = st.astype(BF16)
            st_prev_in = st_ref[jnp.maximum(ci - 1, 0)]
            st_prev_edge = jnp.where(first_block, 0.0, stp_ref[0])
            st_prev = jnp.where(ci > 0, st_prev_in, st_prev_edge)
            dkd_parts = []
            dgam_parts = []
            for h in range(GLA_HEADS):
                dst = lax.dot_general(do_c[:, _hv(h)], qs[:, _hk(h)], _DOT_DIMS["tn"],
                                      preferred_element_type=F32) + carry[:, _hk(h)]
                dst_b = dst.astype(BF16)
                dqs = jnp.dot(do_c[:, _hv(h)], st_b[:, _hk(h)], preferred_element_type=F32)
                dp_ref[rows, COL_Q + h * HEAD_K:COL_Q + (h + 1) * HEAD_K] = (dqs * Q_SCALE).astype(BF16)
                dkd_parts.append(jnp.dot(v_c[:, _hv(h)], dst_b, preferred_element_type=F32))
                dp_ref[rows, COL_V + h * HEAD_V:COL_V + (h + 1) * HEAD_V] = lax.dot_general(
                    kd_b[:, _hk(h)], dst_b, _DOT_DIMS["nt"], preferred_element_type=F32).astype(BF16)
                dgam_parts.append(jnp.sum(dst * st_prev[:, _hk(h)], axis=0, keepdims=True))
                carry[:, _hk(h)] = dst * gam[:, _hk(h)]
            dkd = jnp.concatenate(dkd_parts, axis=1)
            dgam = jnp.concatenate(dgam_parts, axis=1)
            dp_ref[rows, COL_K:COL_K + KEY_WIDTH] = (dkd * e).astype(BF16)
            dz = dkd * kd
            dla_scr[rows, :] = gam * dgam + jnp.dot(tri_s, dz, precision=lax.Precision.HIGHEST,
                                                    preferred_element_type=F32)
            return c0

        lax.fori_loop(0, nc, bwd_chunk, 0, unroll=True)

        dfg = dla_scr[...] * (1.0 / GATE_TAU) * _sigmoid(-fg_scr[...])
        dfg_b = dfg.astype(BF16)
        dfz_ref[...] = lax.dot_general(dfg_b, w_ref[...], _DOT_DIMS["nt"],
                                       preferred_element_type=F32).astype(BF16)
        dw_ref[...] += lax.dot_general(fz, dfg_b, _DOT_DIMS["tn"], preferred_element_type=F32)
        db_ref[...] += jnp.sum(dfg, axis=0, keepdims=True)

    def rev(i):
        return nb - 1 - i

    return pl.pallas_call(
        body, name="gla_bwd",
        out_shape=(jax.ShapeDtypeStruct((s, PROJ_W), BF16),
                   jax.ShapeDtypeStruct((s, FZ_PAD), BF16),
                   jax.ShapeDtypeStruct((1, HEAD_V), F32),
                   jax.ShapeDtypeStruct((FZ_PAD, KEY_WIDTH), F32),
                   jax.ShapeDtypeStruct((1, KEY_WIDTH), F32)),
        grid=(nb,),
        in_specs=[pl.BlockSpec((tr, KEY_WIDTH), lambda i: (rev(i), COL_Q // KEY_WIDTH)),
                  pl.BlockSpec((tr, KEY_WIDTH), lambda i: (rev(i), COL_K // KEY_WIDTH)),
                  pl.BlockSpec((tr, VAL_WIDTH), lambda i: (rev(i), COL_V // VAL_WIDTH)),
                  pl.BlockSpec((tr, VAL_WIDTH), lambda i: (rev(i), COL_R // VAL_WIDTH)),
                  pl.BlockSpec((tr, FZ_PAD), lambda i: (rev(i), COL_FZ // FZ_PAD)),
                  pl.BlockSpec((tr, VAL_WIDTH), lambda i: (rev(i), 0)),
                  pl.BlockSpec((nc, HEAD_V, KEY_WIDTH), lambda i: (rev(i), 0, 0)),
                  pl.BlockSpec((1, HEAD_V, KEY_WIDTH), lambda i: (jnp.maximum(rev(i) * nc - 1, 0), 0, 0)),
                  pl.BlockSpec((FZ_PAD, KEY_WIDTH), lambda i: (0, 0)),
                  pl.BlockSpec((1, KEY_WIDTH), lambda i: (0, 0)),
                  pl.BlockSpec((1, HEAD_V), lambda i: (0, 0)), ANY],
        out_specs=(pl.BlockSpec((tr, 3 * VAL_WIDTH), lambda i: (rev(i), 0)),
                   pl.BlockSpec((tr, FZ_PAD), lambda i: (rev(i), 0)),
                   pl.BlockSpec((1, HEAD_V), lambda i: (0, 0)),
                   pl.BlockSpec((FZ_PAD, KEY_WIDTH), lambda i: (0, 0)),
                   pl.BlockSpec((1, KEY_WIDTH), lambda i: (0, 0))),
        scratch_shapes=[pltpu.VMEM((HEAD_V, KEY_WIDTH), F32),
                        pltpu.VMEM((tr, KEY_WIDTH), F32),
                        pltpu.VMEM((tr, KEY_WIDTH), F32),
                        pltpu.VMEM((tr, KEY_WIDTH), F32),
                        pltpu.VMEM((tr, VAL_WIDTH), F32),
                        pltpu.VMEM((tr, VAL_WIDTH), BF16)],
        input_output_aliases={11: 0},
        compiler_params=_cparams(("arbitrary",)),
    )(proj, proj, proj, proj, proj, doa, states, states, wfg2p, bfg, gn, dproj)


def _put_fz(dproj, dfz, *, s):
    tr = min(512, s)

    def body(dfz_ref, dproj_in_ref, o_ref):
        o_ref[...] = dfz_ref[...]

    return pl.pallas_call(
        body, name="put_dfz",
        out_shape=jax.ShapeDtypeStruct((s, PROJ_W), BF16),
        grid=(s // tr,),
        in_specs=[pl.BlockSpec((tr, FZ_PAD), lambda i: (i, 0)), ANY],
        out_specs=pl.BlockSpec((tr, FZ_PAD), lambda i: (i, COL_FZ // FZ_PAD)),
        input_output_aliases={1: 0},
        compiler_params=_cparams(("parallel",)),
    )(dfz, dproj)


_ADAM_C1 = 1.0 / (1.0 - ADAM_B1 ** ADAM_STEP)
_ADAM_C2 = 1.0 / (1.0 - ADAM_B2 ** ADAM_STEP)


def _adamw_math(wv, gv, mv, vv):
    nm = ADAM_B1 * mv + (1.0 - ADAM_B1) * gv
    nv = ADAM_B2 * vv + (1.0 - ADAM_B2) * (gv * gv)
    delta = -ADAM_LR * ((nm * _ADAM_C1) / (jnp.sqrt(nv * _ADAM_C2) + ADAM_EPS) + ADAM_WD * wv)
    return delta, nm, nv


def _adamw(w, g, m, v, *, name):
    shape = w.shape
    if w.ndim == 1:
        w, g, m, v = (t.reshape(1, 1, -1) for t in (w, g, m, v))
    elif w.ndim == 2:
        w, g, m, v = (t.reshape((1,) + t.shape) for t in (w, g, m, v))
    l, a, b = w.shape

    def body(w_ref, g_ref, m_ref, v_ref, d_ref, nm_ref, nv_ref):
        d_ref[...], nm_ref[...], nv_ref[...] = _adamw_math(w_ref[...], g_ref[...], m_ref[...], v_ref[...])

    blk = pl.BlockSpec((1, a, b), lambda li: (li, 0, 0))
    outs = pl.pallas_call(
        body, name=name,
        out_shape=tuple(jax.ShapeDtypeStruct((l, a, b), F32) for _ in range(3)),
        grid=(l,),
        in_specs=[blk, blk, blk, blk],
        out_specs=(blk, blk, blk),
        compiler_params=_cparams(("parallel",)),
    )(w, g, m, v)
    return tuple(o.reshape(shape) for o in outs)


def _adamw_layers(w, reduced, received, on_core0, m, v, *, name):
    l, a, b = w.shape
    assert l == DEPTH == 2
    ta = _row_tile(a, F32_SUBLANES, 384)

    def body(flag_ref, w_ref, r0_ref, o0_ref, r1_ref, o1_ref, m_ref, v_ref, g_ref, d_ref, nm_ref, nv_ref):
        core0 = flag_ref[...] > 0.5
        g0 = jnp.where(core0, r0_ref[...], o0_ref[...])
        g1 = jnp.where(core0, o1_ref[...], r1_ref[...])
        gv = jnp.where(pl.program_id(0) == 0, g0, g1)
        g_ref[0] = gv
        d_ref[0], nm_ref[0], nv_ref[0] = _adamw_math(w_ref[0], gv, m_ref[0], v_ref[0])

    blk = pl.BlockSpec((1, ta, b), lambda li, ai: (li, ai, 0))
    gblk = pl.BlockSpec((ta, b), lambda li, ai: (ai, 0))
    return pl.pallas_call(
        body, name=name,
        out_shape=tuple(jax.ShapeDtypeStruct((l, a, b), F32) for _ in range(4)),
        grid=(l, a // ta),
        in_specs=[pl.BlockSpec((1, 1), lambda li, ai: (0, 0)), blk, gblk, gblk, gblk, gblk, blk, blk],
        out_specs=(blk, blk, blk, blk),
        compiler_params=_cparams(("parallel", "parallel")),
    )(on_core0, w, reduced[0], received[0], reduced[1], received[1], m, v)


MESH_ID = pl.DeviceIdType.MESH
ANY = pl.BlockSpec(memory_space=pl.ANY)


def _position():
    x, y, c = lax.axis_index("x"), lax.axis_index("y"), lax.axis_index("c")
    chips = [(1 - x, y), (x, 1 - y), (1 - x, 1 - y)]
    return x, y, c, chips


def _chip_index(xy):
    return 2 * xy[0] + xy[1]


def _all_gather_weights(shards):
    n = len(shards)

    def body(*refs):
        ins, outs = refs[:n], refs[n:2 * n]
        send_sems, recv_sems, pass_send, pass_recv = refs[2 * n:]
        x, y, c, chips = _position()
        sibling = (x, y, 1 - c)

        def ici(a, k, chip_from):
            return pltpu.make_async_remote_copy(
                src_ref=ins[a].at[c], dst_ref=outs[a].at[c, _chip_index(chip_from)],
                send_sem=send_sems.at[a, k], recv_sem=recv_sems.at[a, k],
                device_id=(chips[k][0], chips[k][1], c), device_id_type=MESH_ID)

        def handoff(a, k, layer):
            slot = outs[a].at[layer, _chip_index(chips[k])]
            return pltpu.make_async_remote_copy(
                src_ref=slot, dst_ref=slot, send_sem=pass_send.at[a, k], recv_sem=pass_recv.at[a, k],
                device_id=sibling, device_id_type=MESH_ID)

        for a in range(n):
            for k in range(3):
                ici(a, k, (x, y)).start()
        for k in range(3):
            for a in range(n):
                ici(a, k, chips[k]).wait_recv()
                handoff(a, k, c).start()
        for k in range(3):
            for a in range(n):
                handoff(a, k, 1 - c).wait_recv()
        for a in range(n):
            for k in range(3):
                ici(a, k, (x, y)).wait_send()
                handoff(a, k, c).wait_send()

    gathered = pl.pallas_call(
        body, name="all_gather_weights",
        out_shape=tuple(jax.ShapeDtypeStruct((DEPTH, N_CHIPS) + t.shape[1:], t.dtype) for t in shards),
        in_specs=[ANY] * n, out_specs=tuple([ANY] * n),
        scratch_shapes=[pltpu.SemaphoreType.DMA((n, 3)), pltpu.SemaphoreType.DMA((n, 3)),
                        pltpu.SemaphoreType.DMA((n, 3)), pltpu.SemaphoreType.DMA((n, 3))],
    )(*shards)
    me = _chip_index((lax.axis_index("x"), lax.axis_index("y")))
    return [lax.dynamic_update_index_in_dim(g, t[:, None], me, axis=1) for g, t in zip(gathered, shards)]


HBM = pl.BlockSpec(memory_space=pltpu.HBM)
SEM = pl.BlockSpec(memory_space=pltpu.SEMAPHORE)
DATAFLOW_EFFECT = pltpu.SideEffectType.DATAFLOW_SIDE_EFFECTING
TOKEN_SHAPE = (8, LANE)


def _landing(shape, dtype):
    return pltpu.with_memory_space_constraint(lax.empty(shape, dtype), pltpu.HBM)


def _split_start(bufs, sem_shape, issue, *, name, after=None):
    n = len(bufs)
    n_in = n + (after is not None)

    def body(*refs):
        issue(refs[:n], refs[n_in], refs[n_in + 1])
        token = refs[-1]
        token[...] = jnp.zeros_like(token)

    operands = [pltpu.with_memory_space_constraint(t, pltpu.HBM) for t in bufs]
    outs = pl.pallas_call(
        body, name=name,
        out_shape=(pltpu.SemaphoreType.DMA(sem_shape), pltpu.SemaphoreType.DMA(sem_shape),
                   *[pltpu.HBM(t.shape, t.dtype) for t in bufs], jax.ShapeDtypeStruct(TOKEN_SHAPE, F32)),
        in_specs=[HBM] * n + [ANY] * (after is not None),
        out_specs=(SEM, SEM, *[HBM] * n, pl.BlockSpec(memory_space=pltpu.VMEM)),
        input_output_aliases={i: 2 + i for i in range(n)},
        compiler_params=pltpu.CompilerParams(has_side_effects=DATAFLOW_EFFECT),
    )(*operands, *([after] if after is not None else []))
    return outs[0], outs[1], list(outs[2:2 + n]), outs[-1]


def _split_wait(started, after, settle, *, name):
    send_sems, recv_sems, bufs, _ = started
    n = len(bufs)
    afters = tuple(after) if isinstance(after, (tuple, list)) else (after,)

    def body(*refs):
        settle(refs[:n], refs[n], refs[n + 1])

    outs = pl.pallas_call(
        body, name=name,
        out_shape=tuple(pltpu.HBM(t.shape, t.dtype) for t in bufs),
        in_specs=[HBM] * n + [SEM, SEM] + [ANY] * len(afters),
        out_specs=tuple([HBM] * n),
        input_output_aliases={i: i for i in range(n)},
        compiler_params=pltpu.CompilerParams(has_side_effects=DATAFLOW_EFFECT),
    )(*bufs, send_sems, recv_sems, *afters)
    return list(outs)


def _to_sibling(bufs, n, send_sems, recv_sems):
    x, y, c, _ = _position()
    return [pltpu.make_async_remote_copy(
        src_ref=bufs[a], dst_ref=bufs[n + a], send_sem=send_sems.at[a], recv_sem=recv_sems.at[a],
        device_id=(x, y, 1 - c), device_id_type=MESH_ID) for a in range(n)]


def _sibling_push_start(layer, sender_is_reducer, arrays, *, name):
    n = len(arrays)
    sender = layer if sender_is_reducer else 1 - layer

    def issue(bufs, send_sems, recv_sems):
        @pl.when(lax.axis_index("c") == sender)
        def _():
            for cp in _to_sibling(bufs, n, send_sems, recv_sems):
                cp.start()

    lands = [_landing(t.shape, t.dtype) for t in arrays]
    return _split_start(list(arrays) + lands, (n,), issue, name=name)


def _sibling_push_wait(layer, sender_is_reducer, started, after, *, name):
    n = len(started[2]) // 2
    sender = layer if sender_is_reducer else 1 - layer

    def settle(bufs, send_sems, recv_sems):
        c = lax.axis_index("c")

        @pl.when(c == sender)
        def _():
            for cp in _to_sibling(bufs, n, send_sems, recv_sems):
                cp.wait_send()

        @pl.when(c != sender)
        def _():
            for cp in _to_sibling(bufs, n, send_sems, recv_sems):
                cp.wait_recv()

    outs = _split_wait(started, after, settle, name=name)
    return outs[:n], outs[n:]


def _chip_copies(bufs, n, send_sems, recv_sems):
    x, y, c, chips = _position()
    return [pltpu.make_async_remote_copy(
        src_ref=bufs[a].at[_chip_index(chips[k])], dst_ref=bufs[n + a].at[k],
        send_sem=send_sems.at[3 * a + k], recv_sem=recv_sems.at[3 * a + k],
        device_id=(chips[k][0], chips[k][1], c), device_id_type=MESH_ID)
        for a in range(n) for k in range(3)]


def _chip_exchange_start(layer, parts, *, name):
    n = len(parts)

    def issue(bufs, send_sems, recv_sems):
        @pl.when(lax.axis_index("c") == layer)
        def _():
            for cp in _chip_copies(bufs, n, send_sems, recv_sems):
                cp.start()

    lands = [_landing((3,) + t.shape[1:], t.dtype) for t in parts]
    return _split_start(list(parts) + lands, (3 * n,), issue, name=name)


def _chip_exchange_wait(layer, started, after, *, name):
    n = len(started[2]) // 2

    def settle(bufs, send_sems, recv_sems):
        @pl.when(lax.axis_index("c") == layer)
        def _():
            for cp in _chip_copies(bufs, n, send_sems, recv_sems):
                cp.wait()

    outs = _split_wait(started, after, settle, name=name)
    return outs[:n], outs[n:]


def _gather_copies(layer, bufs, n, send_sems, recv_sems, arriving):
    x, y, c, chips = _position()
    me = _chip_index((x, y))
    return [pltpu.make_async_remote_copy(
        src_ref=bufs[a], dst_ref=bufs[n + a].at[_chip_index(chips[k]) if arriving else me],
        send_sem=send_sems.at[3 * a + k], recv_sem=recv_sems.at[3 * a + k],
        device_id=(chips[k][0], chips[k][1], c), device_id_type=MESH_ID)
        for a in range(n) for k in range(3)]


def _gather_start(layer, shards, *, name, after=None):
    n = len(shards)

    def issue(bufs, send_sems, recv_sems):
        @pl.when(lax.axis_index("c") == layer)
        def _():
            for cp in _gather_copies(layer, bufs, n, send_sems, recv_sems, False):
                cp.start()

    lands = [_landing((N_CHIPS,) + t.shape, t.dtype) for t in shards]
    return _split_start(list(shards) + lands, (3 * n,), issue, name=name, after=after)


def _gather_wait(layer, started, after, *, name):
    n = len(started[2]) // 2

    def settle(bufs, send_sems, recv_sems):
        @pl.when(lax.axis_index("c") == layer)
        def _():
            for cp in _gather_copies(layer, bufs, n, send_sems, recv_sems, False):
                cp.wait_send()
            for cp in _gather_copies(layer, bufs, n, send_sems, recv_sems, True):
                cp.wait_recv()

    return _split_wait(started, after, settle, name=name)[n:]


def _handoff_copies(bufs, n, send_sems, recv_sems):
    x, y, c, chips = _position()
    out = []
    for a in range(n):
        for k in range(3):
            slot = bufs[a].at[_chip_index(chips[k])]
            out.append(pltpu.make_async_remote_copy(
                src_ref=slot, dst_ref=slot, send_sem=send_sems.at[3 * a + k], recv_sem=recv_sems.at[3 * a + k],
                device_id=(x, y, 1 - c), device_id_type=MESH_ID))
    return out


def _handoff_start(layer, gathered, *, name):
    n = len(gathered)

    def issue(bufs, send_sems, recv_sems):
        @pl.when(lax.axis_index("c") == layer)
        def _():
            for cp in _handoff_copies(bufs, n, send_sems, recv_sems):
                cp.start()

    return _split_start(list(gathered), (3 * n,), issue, name=name)


def _handoff_wait(layer, started, after, *, name):
    n = len(started[2])

    def settle(bufs, send_sems, recv_sems):
        c = lax.axis_index("c")

        @pl.when(c == layer)
        def _():
            for cp in _handoff_copies(bufs, n, send_sems, recv_sems):
                cp.wait_send()

        @pl.when(c != layer)
        def _():
            for cp in _handoff_copies(bufs, n, send_sems, recv_sems):
                cp.wait_recv()

    return _split_wait(started, after, settle, name=name)


F32_SUBLANES = 8
BF16_SUBLANES = 16


def _row_tile(a, sublanes=BF16_SUBLANES, max_rows=704):
    best = None
    for cand in range(sublanes, min(a, max_rows) + 1, sublanes):
        if a % cand == 0:
            best = cand
    assert best is not None, a
    return best


def _pair_sum(mine, sib, *, name):
    nchip, a, b = sib.shape
    ta = _row_tile(a)

    def body(m_ref, s_ref, o_ref):
        o_ref[...] = (m_ref[...].astype(F32) + s_ref[...].astype(F32)).astype(BF16)

    blk = pl.BlockSpec((1, ta, b), lambda j, r: (j, r, 0))
    return pl.pallas_call(
        body, name=name,
        out_shape=jax.ShapeDtypeStruct((nchip, a, b), BF16),
        grid=(nchip, a // ta),
        in_specs=[blk, blk], out_specs=blk,
        compiler_params=_cparams(("parallel", "parallel")),
    )(mine, sib)


def _total_sum(own, recv, *, name):
    a, b = own.shape
    ta = _row_tile(a)

    def body(o_ref, r_ref, t_ref):
        acc = o_ref[...].astype(F32)
        for k in range(3):
            acc = acc + r_ref[k].astype(F32)
        t_ref[...] = acc

    blk = pl.BlockSpec((ta, b), lambda r: (r, 0))
    return pl.pallas_call(
        body, name=name,
        out_shape=jax.ShapeDtypeStruct((a, b), F32),
        grid=(a // ta,),
        in_specs=[blk, pl.BlockSpec((3, ta, b), lambda r: (0, r, 0))], out_specs=blk,
        compiler_params=_cparams(("parallel",)),
    )(own, recv)


def _all_reduce_small(packed):
    rows, width = packed.shape

    def body(x_ref, out_ref, gath, send_sems, recv_sems, local_sem):
        x, y, c, chips = _position()
        me, sibling = (x, y, c), (x, y, 1 - c)

        def slot(px, py, pc):
            return gath.at[4 * px + 2 * py + pc]

        def copy(k, block, to, src=None):
            return pltpu.make_async_remote_copy(
                src_ref=slot(*block) if src is None else src, dst_ref=slot(*block),
                send_sem=send_sems.at[k], recv_sem=recv_sems.at[k], device_id=to, device_id_type=MESH_ID)

        mine = pltpu.make_async_copy(x_ref, slot(*me), local_sem)
        mine.start()
        first = [copy(0, me, sibling, src=x_ref)]
        first += [copy(1 + j, me, (*chip, c), src=x_ref) for j, chip in enumerate(chips)]
        for cp in first:
            cp.start()
        passed = [copy(4 + j, (*chip, c), sibling) for j, chip in enumerate(chips)]
        for j, chip in enumerate(chips):
            copy(1 + j, (*chip, c), me).wait_recv()
            passed[j].start()
        copy(0, sibling, me).wait_recv()
        for j, chip in enumerate(chips):
            copy(4 + j, (*chip, 1 - c), me).wait_recv()
        for cp in first + passed:
            cp.wait_send()
        mine.wait()
        acc = gath[0]
        for d in range(1, N_DEV):
            acc = acc + gath[d]
        out_ref[...] = acc

    return pl.pallas_call(
        body, name="all_reduce_small",
        out_shape=jax.ShapeDtypeStruct((rows, width), F32),
        in_specs=[pl.BlockSpec(memory_space=pltpu.VMEM)],
        out_specs=pl.BlockSpec(memory_space=pltpu.VMEM),
        scratch_shapes=[pltpu.VMEM((N_DEV, rows, width), F32), pltpu.SemaphoreType.DMA((7,)),
                        pltpu.SemaphoreType.DMA((7,)), pltpu.SemaphoreType.DMA],
    )(packed)


def _mixer_forward_branches(x0, h, w, s, dep=None, before_conv=None):
    proj = _matmul(h, w["w_in_t"], mode="nt", out_dtype=BF16, tm=1024, tn=1664, tk=1024, name="proj_fwd",
                   dep=dep)
    oa, states = _gla_fwd(proj, w["w_fg2"], w["b_fg"], w["gla_norm_g"], s=s)
    conv_wb = w["conv_wb"]
    if before_conv is not None:
        conv_wb = conv_wb + before_conv(oa)[0, 0]
    cb_in = _conv_fwd(proj, conv_wb, s=s)
    return dict(x0=x0, h=h, proj=proj, oa=oa, states=states, cb_in=cb_in)


def _mixer_forward_out(sv, w, s, dep=None):
    ya = _matmul(sv["oa"], w["w_oa"], mode="nn", out_dtype=BF16, tm=1024, tn=1024, tk=1024, name="ya_fwd",
                 dep=dep)
    yb = _matmul(sv["cb_in"], w["w_ob"], mode="nn", out_dtype=BF16, tm=1024, tn=1024, tk=1024, name="yb_fwd")
    mix = _mix_fwd(sv["proj"], ya, yb, s=s)
    x1 = _matmul(mix, w["w_o"], mode="nn", out_dtype=F32, tm=1024, tn=1024, tk=1024, name="wo_fwd",
                 add=sv["x0"])
    return x1, dict(sv, ya=ya, yb=yb, mix=mix)


def _ffn_forward_hidden(x1, w, s, after_gate=None):
    h2 = _rms_fwd(x1, w["norm2_g"], name="rms2_fwd")
    gt = _matmul(h2, w["w_gate_t"], mode="nt", out_dtype=BF16, tm=1024, tn=1408, tk=1024, name="ffn_gate_fwd")
    up = _matmul(h2, w["w_up_t"], mode="nt", out_dtype=BF16, tm=1024, tn=1408, tk=1024, name="ffn_up_fwd",
                 dep=None if after_gate is None else after_gate(gt))
    return dict(x1=x1, h2=h2, gt=gt, up=up, hid=_swiglu_fwd(gt, up, s=s))


def _ffn_forward_out(sv, w, dep=None):
    return _matmul(sv["hid"], w["w_ffn_down"], mode="nn", out_dtype=F32, tm=1024, tn=1024, tk=FFN_HIDDEN,
                   name="ffn_out_fwd", add=sv["x1"], dep=dep)


def _ffn_backward_grads(dx2b, w, sv, s):
    g = {}
    dhid = _matmul(dx2b, w["w_ffn_down"], mode="nt", out_dtype=BF16, tm=1024, tn=FFN_HIDDEN, tk=1024,
                   name="ffn_out_bwd")
    g["w_ffn_down"] = _matmul(sv["hid"], dx2b, mode="tn", out_dtype=BF16, tm=1408, tn=1024, tk=1024,
                              name="ffn_out_wgrad")
    dgt, dup = _swiglu_bwd(sv["gt"], sv["up"], dhid, s=s)
    g["w_gate_t"] = _matmul(dgt, sv["h2"], mode="tn", out_dtype=BF16, tm=1408, tn=1024, tk=1024,
                            name="ffn_gate_wgrad")
    g["w_up_t"] = _matmul(dup, sv["h2"], mode="tn", out_dtype=BF16, tm=1408, tn=1024, tk=1024,
                          name="ffn_up_wgrad")
    return g, dgt, dup, dhid


def _ffn_backward_input(dgt, dup, dx2, w, sv, dep=None):
    dh2 = _matmul(dgt, w["w_gate_t"], mode="nn", out_dtype=BF16, tm=1024, tn=1024, tk=FFN_HIDDEN,
                  name="ffn_gate_bwd", dep=dep)
    dh2 = _matmul(dup, w["w_up_t"], mode="nn", out_dtype=BF16, tm=1024, tn=1024, tk=FFN_HIDDEN,
                  name="ffn_up_bwd", add=dh2)
    return _rms_bwd(sv["x1"], w["norm2_g"], dh2, dx2, name="rms2_bwd")


def _mixer_backward_branches(dx1b, w, sv, s, dep=None):
    g = {}
    dmix = _matmul(dx1b, w["w_o"], mode="nt", out_dtype=BF16, tm=1024, tn=1024, tk=1024, name="wo_bwd",
                   dep=dep)
    g["w_o"] = _matmul(sv["mix"], dx1b, mode="tn", out_dtype=BF16, tm=1024, tn=1024, tk=2048, name="wo_wgrad")
    dya, dyb, dproj = _mix_bwd(sv["proj"], sv["ya"], sv["yb"], dmix, s=s)
    dcb = _matmul(dyb, w["w_ob"], mode="nt", out_dtype=BF16, tm=1024, tn=1024, tk=1024, name="yb_bwd")
    g["w_ob"] = _matmul(sv["cb_in"], dyb, mode="tn", out_dtype=BF16, tm=1024, tn=1024, tk=2048, name="yb_wgrad")
    doa = _matmul(dya, w["w_oa"], mode="nt", out_dtype=BF16, tm=1024, tn=1024, tk=1024, name="ya_bwd")
    g["w_oa"] = _matmul(sv["oa"], dya, mode="tn", out_dtype=BF16, tm=1024, tn=1024, tk=2048, name="ya_wgrad")
    dproj, g["conv_wb"] = _conv_bwd(sv["proj"], dcb, w["conv_wb"], dproj, s=s)
    dproj, dfz, g["gla_norm_g"], g["w_fg2"], g["b_fg"] = _gla_bwd(
        sv["proj"], doa, sv["states"], w["w_fg2"], w["b_fg"], w["gla_norm_g"], dproj, s=s)
    return g, _put_fz(dproj, dfz, s=s)


def _proj_wgrad(dproj, sv, dep=None):
    return _matmul(dproj, sv["h"], mode="tn", out_dtype=BF16, tm=1664, tn=1024, tk=1024, name="proj_wgrad",
                   dep=dep)


def _proj_bwd(dproj, w, dep=None):
    return _matmul(dproj, w["w_in_t"], mode="nn", out_dtype=BF16, tm=1024, tn=1024, tk=1664, name="proj_bwd",
                   dep=dep)


def _cols_from_chips(t):
    return jnp.transpose(t, (1, 0, 2)).reshape(t.shape[1], -1)


W_IN_ROWS = IN_WIDTH // N_CHIPS
W_IN_ROWS_PAD = -(-W_IN_ROWS // BF16_SUBLANES) * BF16_SUBLANES


def _w_in_t_shard(t):
    return jnp.pad(jnp.transpose(t, (0, 2, 1)), ((0, 0), (0, W_IN_ROWS_PAD - W_IN_ROWS), (0, 0)))


def _w_in_t_unshard(t):
    return jnp.transpose(t[:, :W_IN_ROWS], (0, 2, 1))


def _w_in_row_segments():
    runs = [(0, 0, FZ_ORIG), (FZ_ORIG, FZ_ORIG + GATE_RANK, COL_FZ - FZ_ORIG), (COL_FZ, FZ_ORIG, GATE_RANK)]
    out = []
    for kernel_row, ref_row, length in runs:
        while length:
            chip, local = divmod(ref_row, W_IN_ROWS)
            n = min(length, W_IN_ROWS - local)
            out.append((kernel_row, chip * W_IN_ROWS_PAD + local, n))
            kernel_row, ref_row, length = kernel_row + n, ref_row + n, length - n
    return out


def _permute_rows(src, n_out, segments, *, name):
    n_src, width = src.shape
    block = _row_tile(n_out)
    window = block + BF16_SUBLANES
    assert n_src >= window and src.dtype == BF16

    def body(src_ref, out_ref):
        rows = lax.broadcasted_iota(jnp.int32, (block, window), 0)
        cols = lax.broadcasted_iota(jnp.int32, (block, window), 1)
        for b0 in range(0, n_out, block):
            acc = None
            for o, s, n in segments:
                lo, hi = max(o, b0), min(o + n, b0 + block)
                if lo >= hi:
                    continue
                first = s + (lo - o)
                base = max(0, min(first // BF16_SUBLANES * BF16_SUBLANES, n_src - window))
                shift = (first - base) - (lo - b0)
                pick = (cols == rows + shift) & (rows >= lo - b0) & (rows < hi - b0)
                part = jnp.dot(jnp.where(pick, 1.0, 0.0).astype(BF16), src_ref[base:base + window, :],
                               preferred_element_type=F32)
                acc = part if acc is None else acc + part
            out_ref[b0:b0 + block, :] = (jnp.zeros((block, width), F32) if acc is None else acc).astype(BF16)

    return pl.pallas_call(
        body, name=name,
        out_shape=jax.ShapeDtypeStruct((n_out, width), BF16),
        in_specs=[pl.BlockSpec(memory_space=pltpu.VMEM)],
        out_specs=pl.BlockSpec(memory_space=pltpu.VMEM),
        compiler_params=_cparams(),
    )(src)


def _w_in_t_to_kernel(t):
    return _permute_rows(t.reshape(N_CHIPS * W_IN_ROWS_PAD, D_MODEL), PROJ_W, _w_in_row_segments(),
                         name="w_in_to_kernel_rows")


def _w_in_t_from_kernel(g):
    segments = [(slab, kernel_row, n) for kernel_row, slab, n in _w_in_row_segments()]
    out = _permute_rows(g, N_CHIPS * W_IN_ROWS_PAD, segments, name="w_in_to_chip_rows")
    return out.reshape(N_CHIPS, W_IN_ROWS_PAD, D_MODEL)


def kernel(x, norm1_g, w_in, w_fg2, b_fg, gla_norm_g, w_oa, conv_w, conv_b, w_ob, w_o, norm2_g, w_ffn_gate, w_ffn_up, w_ffn_down, final_g, loss_target, m_norm1_g, m_w_in, m_w_fg2, m_b_fg, m_gla_norm_g, m_w_oa, m_conv_w, m_conv_b, m_w_ob, m_w_o, m_norm2_g, m_w_ffn_gate, m_w_ffn_up, m_w_ffn_down, m_final_g, v_norm1_g, v_w_in, v_w_fg2, v_b_fg, v_gla_norm_g, v_w_oa, v_conv_w, v_conv_b, v_w_ob, v_w_o, v_norm2_g, v_w_ffn_gate, v_w_ffn_up, v_w_ffn_down, v_final_g):
    cx_ = lax.axis_index("x")
    cy_ = lax.axis_index("y")
    cc_ = lax.axis_index("c")
    me = 2 * cx_ + cy_
    on_core0 = jnp.where(cc_ == 0, 1.0, 0.0).astype(F32).reshape(1, 1)

    def swap(t):
        return jnp.swapaxes(t, 1, 2)

    big_names = ["w_in", "w_oa", "w_ob", "w_o", "w_ffn_gate", "w_ffn_up", "w_ffn_down"]
    views = dict(
        w_in=tuple(_w_in_t_shard(t) for t in (w_in, m_w_in, v_w_in)),
        w_oa=(w_oa, m_w_oa, v_w_oa), w_ob=(w_ob, m_w_ob, v_w_ob), w_o=(w_o, m_w_o, v_w_o),
        w_ffn_gate=tuple(swap(t) for t in (w_ffn_gate, m_w_ffn_gate, v_w_ffn_gate)),
        w_ffn_up=tuple(swap(t) for t in (w_ffn_up, m_w_ffn_up, v_w_ffn_up)),
        w_ffn_down=(w_ffn_down, m_w_ffn_down, v_w_ffn_down))
    from_view = dict(w_in=_w_in_t_unshard, w_ffn_gate=swap, w_ffn_up=swap)

    s = x.shape[1]

    proj_names = ["w_in"]
    rest_names = ["w_oa", "w_ob", "w_o", "w_ffn_gate", "w_ffn_up", "w_ffn_down"]
    mixer_names = ["w_oa", "w_ob", "w_o"]
    ffn_names = ["w_ffn_gate", "w_ffn_up", "w_ffn_down"]
    weight_key = dict(w_in="w_in_t", w_ffn_gate="w_gate_t", w_ffn_up="w_up_t")
    full_shape = dict(w_oa=(VAL_WIDTH, D_MODEL), w_ob=(CONV_CH, D_MODEL), w_o=(D_MODEL, D_MODEL),
                      w_ffn_gate=(FFN_HIDDEN, D_MODEL), w_ffn_up=(FFN_HIDDEN, D_MODEL),
                      w_ffn_down=(FFN_HIDDEN, D_MODEL))

    def shards_of(l, names):
        return [views[n][0][l].astype(BF16) for n in names]

    def small_weights(l):
        w_fg2_full = _cols_from_chips(w_fg2_all[l])
        conv_w_full = _cols_from_chips(conv_w_all[l])[:3]
        return dict(
            norm1_g=norm1_g[l], norm2_g=norm2_g[l],
            w_fg2=jnp.pad(w_fg2_full, ((0, FZ_PAD - GATE_RANK), (0, 0))).astype(BF16),
            b_fg=b_fg[l].reshape(1, KEY_WIDTH), gla_norm_g=gla_norm_g[l].reshape(1, HEAD_V),
            conv_wb=(jnp.pad(conv_w_full, ((0, 5), (0, 0)))
                     + jnp.pad(conv_b[l].reshape(1, CONV_CH), ((3, 4), (0, 0)))))

    def full_weights(names, gathered, shards):
        out = {}
        for n, g, t in zip(names, gathered, shards):
            g = lax.dynamic_update_index_in_dim(g, t[None], me, axis=0)
            out[weight_key.get(n, n)] = _w_in_t_to_kernel(g) if n == "w_in" else g.reshape(full_shape[n])
        return out

    def per_chip(g, names):
        out = []
        for n in names:
            t = g[weight_key.get(n, n)]
            out.append(_w_in_t_from_kernel(t) if n == "w_in"
                       else t.reshape(N_CHIPS, t.shape[0] // N_CHIPS, t.shape[1]))
        return out

    def gather(l, names, tag, after=None):
        shards = shards_of(l, names)
        return shards, _gather_start(l, shards, name=f"gather_{tag}_start", after=after)

    def gathered_to_sibling(l, started, after, tag):
        return _handoff_start(l, _gather_wait(l, started, after, name=f"gather_{tag}_wait"),
                              name=f"handoff_{tag}_start")

    def feed(l, g, names, tag):
        return _sibling_push_start(l, False, per_chip(g, names), name=f"feed_{tag}_start")

    def pair_and_exchange(l, fed, after, names, tag):
        mine, sib = _sibling_push_wait(l, False, fed, after, name=f"feed_{tag}_wait")
        parts = [_pair_sum(a, b, name=f"pair_sum_{tag}_{n}") for n, a, b in zip(names, mine, sib)]
        return _chip_exchange_start(l, parts, name=f"exchange_{tag}_start")

    def total_and_share(l, swapped, after, names, tag):
        parts, recvs = _chip_exchange_wait(l, swapped, after, name=f"exchange_{tag}_wait")
        owns = [lax.dynamic_index_in_dim(p, me, 0, keepdims=False) for p in parts]
        totals = [_total_sum(o, rc, name=f"total_{tag}_{n}") for n, o, rc in zip(names, owns, recvs)]
        return _sibling_push_start(l, True, totals, name=f"share_{tag}_start")

    def shared(l, sharing, after, names, tag):
        totals, others = _sibling_push_wait(l, True, sharing, after, name=f"share_{tag}_wait")
        return {n: (t, o) for n, t, o in zip(names, totals, others)}

    def branches_under_gather(l, xin, h, w, tag, shards_r, started_r):
        box = {}

        def before_conv(oa):
            box["handoff"] = gathered_to_sibling(l, started_r, oa, tag)
            return box["handoff"][3]

        sv = _mixer_forward_branches(xin, h, w, s, dep=started_r[3], before_conv=before_conv)
        rest = _handoff_wait(l, box["handoff"], sv["cb_in"], name=f"handoff_{tag}_wait")
        w.update(full_weights(rest_names, rest, shards_r))
        return sv

    shards_p, started = gather(0, proj_names, "p0")
    conv_w_p = jnp.pad(conv_w, ((0, 0), (0, 8 - conv_w.shape[1]), (0, 0)))
    w_fg2_all, conv_w_all = _all_gather_weights([w_fg2 + started[3][0, 0], conv_w_p])
    h0 = _rms_fwd(x[0], norm1_g[0], name="rms1_fwd")
    started = gathered_to_sibling(0, started, (h0, views["w_in"][1], w_fg2_all), "p0")
    w0 = small_weights(0)
    w0.update(full_weights(proj_names, _handoff_wait(0, started, views["w_in"][2], name="handoff_p0_wait"),
                           shards_p))
    sv0 = branches_under_gather(0, x[0], h0, w0, "r0", *gather(0, rest_names, "r0", after=w0["w_in_t"]))

    shards_p1, started = gather(1, proj_names, "p1", after=w0["w_o"])
    x1, sv0m = _mixer_forward_out(sv0, w0, s, dep=started[3])
    box1 = {}

    def after_gate(gt):
        box1["handoff"] = gathered_to_sibling(1, started, gt, "p1")
        box1["rest"] = gather(1, rest_names, "r1", after=box1["handoff"][2][0])
        return box1["handoff"][3] + box1["rest"][1][3]

    sv0f = _ffn_forward_hidden(x1, w0, s, after_gate=after_gate)
    x2 = _ffn_forward_out(sv0f, w0)
    w1 = small_weights(1)
    w1.update(full_weights(proj_names, _handoff_wait(1, box1["handoff"], x2, name="handoff_p1_wait"), shards_p1))

    h1 = _rms_fwd(x2, norm1_g[1], name="rms1_fwd")
    x3, sv1m = _mixer_forward_out(branches_under_gather(1, x2, h1, w1, "r1", *box1["rest"]), w1, s)
    sv1f = _ffn_forward_hidden(x3, w1, s)
    x4 = _ffn_forward_out(sv1f, w1)
    loss_local, dx, dxb, dgf = _loss_head(x4, final_g, loss_target[0])
    loss = lax.psum(loss_local[0, 0], ("x", "y", "c"))

    g1, dgt, dup, _ = _ffn_backward_grads(dxb, w1, sv1f, s)
    dx_mid, dxb_mid, g1["norm2_g"] = _ffn_backward_input(dgt, dup, dx, w1, sv1f)
    gm, dproj = _mixer_backward_branches(dxb_mid, w1, sv1m, s)
    g1.update(gm)
    g1["w_in_t"] = _proj_wgrad(dproj, sv1m)
    fed_1 = feed(1, g1, big_names, "l1")
    dh = _proj_bwd(dproj, w1, dep=fed_1[3])
    dx, dxb, g1["norm1_g"] = _rms_bwd(sv1m["x0"], norm1_g[1], dh, dx_mid, name="rms1_bwd")

    g0, dgt, dup, dhid = _ffn_backward_grads(dxb, w0, sv0f, s)
    swap_1 = pair_and_exchange(1, fed_1, dhid, big_names, "l1")
    fed_f = feed(0, g0, ffn_names, "f0")
    dx_mid, dxb_mid, g0["norm2_g"] = _ffn_backward_input(dgt, dup, dx, w0, sv0f, dep=swap_1[3] + fed_f[3])
    swap_f = pair_and_exchange(0, fed_f, dx_mid, ffn_names, "f0")
    gm, dproj = _mixer_backward_branches(dxb_mid, w0, sv0m, s, dep=swap_f[3])
    g0.update(gm)
    share_1 = total_and_share(1, swap_1, dproj, big_names, "l1")
    share_f = total_and_share(0, swap_f, dproj, ffn_names, "f0")
    fed_m = feed(0, g0, mixer_names, "m0")
    g0["w_in_t"] = _proj_wgrad(dproj, sv0m, dep=share_1[3] + share_f[3] + fed_m[3])
    fed_p = feed(0, g0, proj_names, "p0")
    swap_m = pair_and_exchange(0, fed_m, g0["w_in_t"], mixer_names, "m0")
    dh = _proj_bwd(dproj, w0, dep=fed_p[3] + swap_m[3])
    reduced1 = shared(1, share_1, dh, big_names, "l1")
    reduced0 = shared(0, share_f, dh, ffn_names, "f0")
    grad_x, _, g0["norm1_g"] = _rms_bwd(sv0m["x0"], norm1_g[0], dh, dx_mid, name="rms1_bwd")

    swap_p = pair_and_exchange(0, fed_p, grad_x, proj_names, "p0")
    share_m = total_and_share(0, swap_m, swap_p[3], mixer_names, "m0")
    reduced0.update(shared(0, share_m, share_m[3], mixer_names, "m0"))
    grads = [g0, g1]

    def small_rows(t):
        return t.reshape(-1, D_MODEL)

    def tile_rows(t):
        return jnp.pad(t, ((0, -t.shape[0] % F32_SUBLANES), (0, 0)))

    g0, g1 = grads
    pieces = [
        jnp.concatenate([g0["norm1_g"], g1["norm1_g"]], axis=0),
        jnp.concatenate([g0["norm2_g"], g1["norm2_g"]], axis=0),
        dgf,
        small_rows(jnp.concatenate([g0["b_fg"], g1["b_fg"]], axis=1)),
        small_rows(jnp.concatenate([g0["gla_norm_g"], g1["gla_norm_g"],
                                    jnp.zeros((1, D_MODEL - 2 * HEAD_V), F32)], axis=1)),
        jnp.concatenate([g0["conv_wb"][3:4], g1["conv_wb"][3:4]], axis=0),
        jnp.concatenate([g0["conv_wb"][:3], g1["conv_wb"][:3]], axis=0),
        small_rows(jnp.stack([g0["w_fg2"][:GATE_RANK], g1["w_fg2"][:GATE_RANK]])),
    ]
    small = _all_reduce_small(jnp.concatenate([tile_rows(p) for p in pieces], axis=0))
    sg = dict(
        norm1_g=small[0:2], norm2_g=small[8:10], final_g=small[16],
        b_fg=small[24].reshape(DEPTH, KEY_WIDTH), gla_norm_g=small[32, :DEPTH * HEAD_V].reshape(DEPTH, HEAD_V),
        conv_b=small[40:42],
        conv_w=lax.dynamic_slice_in_dim(small[48:54].reshape(DEPTH, 3, CONV_CH), me * (CONV_CH // N_CHIPS),
                                        CONV_CH // N_CHIPS, axis=2),
        w_fg2=lax.dynamic_slice_in_dim(small[56:72].reshape(DEPTH, GATE_RANK, KEY_WIDTH),
                                       me * (KEY_WIDTH // N_CHIPS), KEY_WIDTH // N_CHIPS, axis=2),
    )

    small_params = dict(norm1_g=(norm1_g, m_norm1_g, v_norm1_g), w_fg2=(w_fg2, m_w_fg2, v_w_fg2),
                        b_fg=(b_fg, m_b_fg, v_b_fg), gla_norm_g=(gla_norm_g, m_gla_norm_g, v_gla_norm_g),
                        conv_w=(conv_w, m_conv_w, v_conv_w), conv_b=(conv_b, m_conv_b, v_conv_b),
                        norm2_g=(norm2_g, m_norm2_g, v_norm2_g), final_g=(final_g, m_final_g, v_final_g))
    order = ["norm1_g", "w_in", "w_fg2", "b_fg", "gla_norm_g", "w_oa", "conv_w", "conv_b", "w_ob", "w_o",
             "norm2_g", "w_ffn_gate", "w_ffn_up", "w_ffn_down", "final_g"]
    results = {}

    def update_large(n):
        w_, m_, v_ = views[n]
        outs = _adamw_layers(w_, (reduced0[n][0], reduced1[n][0]), (reduced0[n][1], reduced1[n][1]), on_core0,
                             m_, v_, name="adamw_" + n)
        back = from_view.get(n)
        results[n] = tuple(back(r) for r in outs) if back else outs
        return outs[1]

    for n, (w_, m_, v_) in small_params.items():
        results[n] = (sg[n],) + _adamw(w_, sg[n], m_, v_, name="adamw_" + n)
    updated = tuple(update_large(n) for n in rest_names)
    share_p = total_and_share(0, swap_p, updated + (small,), proj_names, "p0")
    reduced0.update(shared(0, share_p, share_p[3], proj_names, "p0"))
    update_large("w_in")
    return (loss, grad_x[None], *[results[n][0] for n in order], *[results[n][1] for n in order],
            *[results[n][2] for n in order], *[results[n][3] for n in order])
```

```python
import functools

import jax
import jax.numpy as jnp
from jax import lax
from jax.experimental import pallas as pl
from jax.experimental.pallas import tpu as pltpu

F32 = jnp.float32
BF16 = jnp.bfloat16

D_MODEL = 1024
DEPTH = 2
CHUNK = 64
GLA_HEADS = 4
KEY_WIDTH = 512
VAL_WIDTH = 1024
HEAD_K = 128
HEAD_V = 256
GATE_RANK = 16
GATE_TAU = 16.0
CONV_CH = 1024
FFN_HIDDEN = 2816
IN_WIDTH = 8208
NORM_EPS = 1e-6
Q_SCALE = HEAD_K ** -0.5
ADAM_LR = 0.001
ADAM_B1 = 0.9
ADAM_B2 = 0.999
ADAM_EPS = 1e-08
ADAM_WD = 0.01
ADAM_STEP = 10

N_CHIPS = 4
N_DEV = 8

LANE = 128
FZ_PAD = LANE
PROJ_W = 8192 + FZ_PAD
COL_Q, COL_K, COL_V, COL_R, COL_GBI, COL_GCI, COL_CX, COL_GA, COL_GB, COL_FZ = (
    0, 512, 1024, 2048, 3072, 4096, 5120, 6144, 7168, 8192)
FZ_ORIG = 3072

VMEM_LIMIT = 52 * 1024 * 1024
HALO = 16


def _cparams(sem=None):
    return pltpu.CompilerParams(dimension_semantics=sem, vmem_limit_bytes=VMEM_LIMIT)


def _sigmoid(x):
    return jax.nn.sigmoid(x)


def _log_sigmoid(x):
    return jnp.minimum(x, 0.0) - jnp.log1p(jnp.exp(-jnp.abs(x)))


_DOT_DIMS = {
    "nn": (((1,), (0,)), ((), ())),
    "nt": (((1,), (1,)), ((), ())),
    "tn": (((0,), (0,)), ((), ())),
}


def _matmul(a, b, *, mode, out_dtype, tm, tn, tk, name, add=None, dep=None):
    if mode == "nn":
        (m, k), n = a.shape, b.shape[1]
    elif mode == "nt":
        (m, k), n = a.shape, b.shape[0]
    else:
        (k, m), n = a.shape, b.shape[1]
    tm, tn, tk = min(tm, m), min(tn, n), min(tk, k)
    assert m % tm == 0 and n % tn == 0 and k % tk == 0, (name, m, n, k, tm, tn, tk)
    nk = k // tk
    has_add = add is not None
    has_dep = dep is not None

    def body(*refs):
        if has_dep:
            refs = refs[1:]
        if has_add:
            a_ref, b_ref, add_ref, o_ref = refs[:4]
            scratch = refs[4:]
        else:
            a_ref, b_ref, o_ref = refs[:3]
            add_ref = None
            scratch = refs[3:]
        part = lax.dot_general(a_ref[...], b_ref[...], _DOT_DIMS[mode], preferred_element_type=F32)

        def finish(acc):
            if add_ref is not None:
                acc = acc + add_ref[...].astype(F32)
            o_ref[...] = acc.astype(o_ref.dtype)

        if nk == 1:
            finish(part)
        else:
            acc_ref = scratch[0]
            kk = pl.program_id(2)

            @pl.when(kk == 0)
            def _():
                acc_ref[...] = part

            @pl.when(kk > 0)
            def _():
                acc_ref[...] += part

            @pl.when(kk == nk - 1)
            def _():
                finish(acc_ref[...])

    if mode == "nn":
        a_spec = pl.BlockSpec((tm, tk), lambda i, j, kk: (i, kk))
        b_spec = pl.BlockSpec((tk, tn), lambda i, j, kk: (kk, j))
    elif mode == "nt":
        a_spec = pl.BlockSpec((tm, tk), lambda i, j, kk: (i, kk))
        b_spec = pl.BlockSpec((tn, tk), lambda i, j, kk: (j, kk))
    else:
        a_spec = pl.BlockSpec((tk, tm), lambda i, j, kk: (kk, i))
        b_spec = pl.BlockSpec((tk, tn), lambda i, j, kk: (kk, j))
    o_spec = pl.BlockSpec((tm, tn), lambda i, j, kk: (i, j))
    in_specs = [a_spec, b_spec] + ([o_spec] if has_add else [])
    operands = (a, b) + ((add,) if has_add else ())
    if has_dep:
        in_specs = [pl.BlockSpec(dep.shape, lambda i, j, kk: (0, 0))] + in_specs
        operands = (dep,) + operands
    return pl.pallas_call(
        body,
        name=name,
        out_shape=jax.ShapeDtypeStruct((m, n), out_dtype),
        grid=(m // tm, n // tn, nk),
        in_specs=in_specs,
        out_specs=o_spec,
        scratch_shapes=[pltpu.VMEM((tm, tn), F32)] if nk > 1 else [],
        compiler_params=_cparams(("parallel", "parallel", "arbitrary")),
    )(*operands)


def _rms_fwd(x, g, *, name):
    s, d = x.shape
    tr = min(512, s)

    def body(x_ref, g_ref, h_ref):
        xv = x_ref[...]
        rs = lax.rsqrt(jnp.mean(xv * xv, axis=-1, keepdims=True) + NORM_EPS)
        h_ref[...] = (xv * rs * g_ref[...]).astype(BF16)

    return pl.pallas_call(
        body, name=name,
        out_shape=jax.ShapeDtypeStruct((s, d), BF16),
        grid=(s // tr,),
        in_specs=[pl.BlockSpec((tr, d), lambda i: (i, 0)), pl.BlockSpec((1, d), lambda i: (0, 0))],
        out_specs=pl.BlockSpec((tr, d), lambda i: (i, 0)),
        compiler_params=_cparams(("parallel",)),
    )(x, g.reshape(1, d))


def _rms_bwd(x, g, dh, dres, *, name):
    s, d = x.shape
    tr = min(512, s)

    def body(x_ref, g_ref, dh_ref, dres_ref, dx_ref, dxb_ref, dg_ref):
        i = pl.program_id(0)
        xv = x_ref[...]
        rs = lax.rsqrt(jnp.mean(xv * xv, axis=-1, keepdims=True) + NORM_EPS)
        n = xv * rs
        dhv = dh_ref[...].astype(F32)
        dn = dhv * g_ref[...]
        dx = dres_ref[...] + rs * (dn - n * jnp.mean(dn * n, axis=-1, keepdims=True))
        dx_ref[...] = dx
        dxb_ref[...] = dx.astype(BF16)
        part = jnp.sum(dhv * n, axis=0, keepdims=True)

        @pl.when(i == 0)
        def _():
            dg_ref[...] = part

        @pl.when(i > 0)
        def _():
            dg_ref[...] += part

    row = pl.BlockSpec((tr, d), lambda i: (i, 0))
    vec = pl.BlockSpec((1, d), lambda i: (0, 0))
    return pl.pallas_call(
        body, name=name,
        out_shape=(jax.ShapeDtypeStruct((s, d), F32), jax.ShapeDtypeStruct((s, d), BF16),
                   jax.ShapeDtypeStruct((1, d), F32)),
        grid=(s // tr,),
        in_specs=[row, vec, row, row],
        out_specs=(row, row, vec),
        compiler_params=_cparams(("arbitrary",)),
    )(x, g.reshape(1, d), dh, dres)


def _loss_head(x, g, target):
    s, d = x.shape
    tr = min(512, s)

    def body(x_ref, g_ref, t_ref, loss_ref, dx_ref, dxb_ref, dg_ref):
        i = pl.program_id(0)
        xv = x_ref[...]
        rs = lax.rsqrt(jnp.mean(xv * xv, axis=-1, keepdims=True) + NORM_EPS)
        n = xv * rs
        gv = g_ref[...]
        err = n * gv - t_ref[...]
        row_loss = jnp.mean(err * err, axis=-1, keepdims=True)
        loss_part = 0.5 * jnp.sum(row_loss, axis=0, keepdims=True)
        dy = err * (1.0 / d)
        dn = dy * gv
        dx = rs * (dn - n * jnp.mean(dn * n, axis=-1, keepdims=True))
        dx_ref[...] = dx
        dxb_ref[...] = dx.astype(BF16)
        dg_part = jnp.sum(dy * n, axis=0, keepdims=True)

        @pl.when(i == 0)
        def _():
            loss_ref[...] = loss_part
            dg_ref[...] = dg_part

        @pl.when(i > 0)
        def _():
            loss_ref[...] += loss_part
            dg_ref[...] += dg_part

    row = pl.BlockSpec((tr, d), lambda i: (i, 0))
    vec = pl.BlockSpec((1, d), lambda i: (0, 0))
    one = pl.BlockSpec((1, 1), lambda i: (0, 0))
    return pl.pallas_call(
        body, name="loss_head",
        out_shape=(jax.ShapeDtypeStruct((1, 1), F32), jax.ShapeDtypeStruct((s, d), F32),
                   jax.ShapeDtypeStruct((s, d), BF16), jax.ShapeDtypeStruct((1, d), F32)),
        grid=(s // tr,),
        in_specs=[row, vec, row],
        out_specs=(one, row, row, vec),
        compiler_params=_cparams(("arbitrary",)),
    )(x, g.reshape(1, d), target)


def _conv_taps(u_prev, u, w_ref, rows):
    ext = jnp.concatenate([u_prev, u], axis=0)
    u1 = pltpu.roll(ext, 1, 0)[HALO:HALO + rows]
    u2 = pltpu.roll(ext, 2, 0)[HALO:HALO + rows]
    conv = w_ref[0:1, :] * u2 + w_ref[1:2, :] * u1 + w_ref[2:3, :] * u + w_ref[3:4, :]
    return conv, u1, u2


def _conv_fwd(proj, conv_wb, *, s):
    tr = min(512, s)
    c = CONV_CH
    hb = tr // HALO

    def body(gbi_ref, gci_ref, cx_ref, gci_h_ref, cx_h_ref, w_ref, o_ref):
        i = pl.program_id(0)
        u = gci_ref[...].astype(F32) * cx_ref[...].astype(F32)
        u_prev = gci_h_ref[...].astype(F32) * cx_h_ref[...].astype(F32)
        u_prev = jnp.where(i == 0, 0.0, u_prev)
        conv, _, _ = _conv_taps(u_prev, u, w_ref, tr)
        o_ref[...] = (gbi_ref[...].astype(F32) * conv).astype(BF16)

    def seg(col):
        return pl.BlockSpec((tr, c), lambda i: (i, col // c))

    def halo(col):
        return pl.BlockSpec((HALO, c), lambda i: (jnp.maximum(i * hb - 1, 0), col // c))

    return pl.pallas_call(
        body, name="conv_fwd",
        out_shape=jax.ShapeDtypeStruct((s, c), BF16),
        grid=(s // tr,),
        in_specs=[seg(COL_GBI), seg(COL_GCI), seg(COL_CX), halo(COL_GCI), halo(COL_CX),
                  pl.BlockSpec((8, c), lambda i: (0, 0))],
        out_specs=pl.BlockSpec((tr, c), lambda i: (i, 0)),
        compiler_params=_cparams(("parallel",)),
    )(proj, proj, proj, proj, proj, conv_wb)


def _conv_bwd(proj, dcb, conv_wb, dproj, *, s):
    tr = min(512, s)
    c = CONV_CH
    hb = tr // HALO
    nb = s // tr

    def body(gbi_ref, gci_ref, cx_ref, dcb_ref, gci_h_ref, cx_h_ref, gbi_n_ref, dcb_n_ref, w_ref,
             dproj_in_ref, dp_ref, dw_ref):
        i = pl.program_id(0)
        gbi = gbi_ref[...].astype(F32)
        gci = gci_ref[...].astype(F32)
        cx = cx_ref[...].astype(F32)
        dcb_v = dcb_ref[...].astype(F32)
        u = gci * cx
        u_prev = jnp.where(i == 0, 0.0, gci_h_ref[...].astype(F32) * cx_h_ref[...].astype(F32))
        conv, u1, u2 = _conv_taps(u_prev, u, w_ref, tr)
        dconv = dcb_v * gbi
        dconv_next = jnp.where(i == nb - 1, 0.0, dcb_n_ref[...].astype(F32) * gbi_n_ref[...].astype(F32))
        ext = jnp.concatenate([dconv, dconv_next], axis=0)
        n_ext = tr + HALO
        d1 = pltpu.roll(ext, n_ext - 1, 0)[0:tr]
        d2 = pltpu.roll(ext, n_ext - 2, 0)[0:tr]
        du = w_ref[2:3, :] * dconv + w_ref[1:2, :] * d1 + w_ref[0:1, :] * d2
        dp_ref[:, 0:c] = (dcb_v * conv).astype(BF16)
        dp_ref[:, c:2 * c] = (du * cx).astype(BF16)
        dp_ref[:, 2 * c:3 * c] = (du * gci).astype(BF16)
        part = jnp.concatenate([
            jnp.sum(dconv * u2, axis=0, keepdims=True),
            jnp.sum(dconv * u1, axis=0, keepdims=True),
            jnp.sum(dconv * u, axis=0, keepdims=True),
            jnp.sum(dconv, axis=0, keepdims=True),
            jnp.zeros((4, c), F32)], axis=0)

        @pl.when(i == 0)
        def _():
            dw_ref[...] = part

        @pl.when(i > 0)
        def _():
            dw_ref[...] += part

    def seg(col):
        return pl.BlockSpec((tr, c), lambda i: (i, col // c))

    def halo_prev(col):
        return pl.BlockSpec((HALO, c), lambda i: (jnp.maximum(i * hb - 1, 0), col // c))

    def halo_next(col):
        return pl.BlockSpec((HALO, c), lambda i: (jnp.minimum((i + 1) * hb, nb * hb - 1), col // c))

    return pl.pallas_call(
        body, name="conv_bwd",
        out_shape=(jax.ShapeDtypeStruct((s, PROJ_W), BF16), jax.ShapeDtypeStruct((8, c), F32)),
        grid=(nb,),
        in_specs=[seg(COL_GBI), seg(COL_GCI), seg(COL_CX), pl.BlockSpec((tr, c), lambda i: (i, 0)),
                  halo_prev(COL_GCI), halo_prev(COL_CX), halo_next(COL_GBI),
                  pl.BlockSpec((HALO, c), lambda i: (jnp.minimum((i + 1) * hb, nb * hb - 1), 0)),
                  pl.BlockSpec((8, c), lambda i: (0, 0)), ANY],
        out_specs=(pl.BlockSpec((tr, 3 * c), lambda i: (i, COL_GBI // (3 * c))),
                   pl.BlockSpec((8, c), lambda i: (0, 0))),
        input_output_aliases={9: 0},
        compiler_params=_cparams(("arbitrary",)),
    )(proj, proj, proj, dcb, proj, proj, proj, dcb, conv_wb, dproj)


def _mix_fwd(proj, ya, yb, *, s):
    tr = min(512, s)
    d = D_MODEL

    def body(ga_ref, gb_ref, ya_ref, yb_ref, o_ref):
        sa = _sigmoid(ga_ref[...].astype(F32))
        sb = _sigmoid(gb_ref[...].astype(F32))
        o_ref[...] = (sa * ya_ref[...].astype(F32) + sb * yb_ref[...].astype(F32)).astype(BF16)

    row = pl.BlockSpec((tr, d), lambda i: (i, 0))
    return pl.pallas_call(
        body, name="mix_fwd",
        out_shape=jax.ShapeDtypeStruct((s, d), BF16),
        grid=(s // tr,),
        in_specs=[pl.BlockSpec((tr, d), lambda i: (i, COL_GA // d)),
                  pl.BlockSpec((tr, d), lambda i: (i, COL_GB // d)), row, row],
        out_specs=row,
        compiler_params=_cparams(("parallel",)),
    )(proj, proj, ya, yb)


def _mix_bwd(proj, ya, yb, dmix, *, s):
    tr = min(512, s)
    d = D_MODEL

    def body(ga_ref, gb_ref, ya_ref, yb_ref, dm_ref, dya_ref, dyb_ref, dg_ref):
        sa = _sigmoid(ga_ref[...].astype(F32))
        sb = _sigmoid(gb_ref[...].astype(F32))
        dm = dm_ref[...].astype(F32)
        dya_ref[...] = (dm * sa).astype(BF16)
        dyb_ref[...] = (dm * sb).astype(BF16)
        dg_ref[:, 0:d] = (dm * ya_ref[...].astype(F32) * sa * (1.0 - sa)).astype(BF16)
        dg_ref[:, d:2 * d] = (dm * yb_ref[...].astype(F32) * sb * (1.0 - sb)).astype(BF16)

    row = pl.BlockSpec((tr, d), lambda i: (i, 0))
    return pl.pallas_call(
        body, name="mix_bwd",
        out_shape=(jax.ShapeDtypeStruct((s, d), BF16), jax.ShapeDtypeStruct((s, d), BF16),
                   jax.ShapeDtypeStruct((s, PROJ_W), BF16)),
        grid=(s // tr,),
        in_specs=[pl.BlockSpec((tr, d), lambda i: (i, COL_GA // d)),
                  pl.BlockSpec((tr, d), lambda i: (i, COL_GB // d)), row, row, row],
        out_specs=(row, row, pl.BlockSpec((tr, 2 * d), lambda i: (i, COL_GA // (2 * d)))),
        compiler_params=_cparams(("parallel",)),
    )(proj, proj, ya, yb, dmix)


def _swiglu_fwd(gt, up, *, s):
    tr = min(256, s)
    f = FFN_HIDDEN

    def body(gt_ref, up_ref, o_ref):
        gv = gt_ref[...].astype(F32)
        o_ref[...] = (gv * _sigmoid(gv) * up_ref[...].astype(F32)).astype(BF16)

    row = pl.BlockSpec((tr, f), lambda i: (i, 0))
    return pl.pallas_call(
        body, name="swiglu_fwd",
        out_shape=jax.ShapeDtypeStruct((s, f), BF16),
        grid=(s // tr,),
        in_specs=[row, row], out_specs=row,
        compiler_params=_cparams(("parallel",)),
    )(gt, up)


def _swiglu_bwd(gt, up, dhid, *, s):
    tr = min(256, s)
    f = FFN_HIDDEN

    def body(gt_ref, up_ref, dh_ref, dgt_ref, dup_ref):
        gv = gt_ref[...].astype(F32)
        uv = up_ref[...].astype(F32)
        dh = dh_ref[...].astype(F32)
        sg = _sigmoid(gv)
        dgt_ref[...] = (dh * uv * sg * (1.0 + gv * (1.0 - sg))).astype(BF16)
        dup_ref[...] = (dh * gv * sg).astype(BF16)

    row = pl.BlockSpec((tr, f), lambda i: (i, 0))
    return pl.pallas_call(
        body, name="swiglu_bwd",
        out_shape=(jax.ShapeDtypeStruct((s, f), BF16), jax.ShapeDtypeStruct((s, f), BF16)),
        grid=(s // tr,),
        in_specs=[row, row, row], out_specs=(row, row),
        compiler_params=_cparams(("parallel",)),
    )(gt, up, dhid)


def _tri(strict):
    r = lax.broadcasted_iota(jnp.int32, (CHUNK, CHUNK), 0)
    c = lax.broadcasted_iota(jnp.int32, (CHUNK, CHUNK), 1)
    return jnp.where((c < r) if strict else (c <= r), 1.0, 0.0).astype(F32)


def _gla_gate_terms(la_c, tri):
    cum = jnp.dot(tri, la_c, precision=lax.Precision.HIGHEST, preferred_element_type=F32)
    cend = cum[CHUNK - 1:CHUNK, :]
    return jnp.exp(cend - cum), jnp.exp(cend)


def _hk(h):
    return slice(h * HEAD_K, (h + 1) * HEAD_K)


def _hv(h):
    return slice(h * HEAD_V, (h + 1) * HEAD_V)


def _gla_fwd(proj, wfg2p, bfg, gn, *, s):
    tr = min(512, s)
    nb = s // tr
    nc = tr // CHUNK

    def body(q_ref, k_ref, v_ref, r_ref, fz_ref, w_ref, b_ref, gn_ref, oa_ref, st_ref,
             state, la_scr, o_scr):
        i = pl.program_id(0)

        @pl.when(i == 0)
        def _():
            state[...] = jnp.zeros_like(state)

        fg = jnp.dot(fz_ref[...], w_ref[...], preferred_element_type=F32) + b_ref[...]
        la_scr[...] = _log_sigmoid(fg) * (1.0 / GATE_TAU)
        tri = _tri(False)

        def chunk(ci, carry):
            r0 = pl.multiple_of(ci * CHUNK, CHUNK)
            rows = pl.ds(r0, CHUNK)
            e, gam = _gla_gate_terms(la_scr[rows, :], tri)
            kd = (k_ref[rows, :].astype(F32) * e).astype(BF16)
            qs = (q_ref[rows, :].astype(F32) * Q_SCALE).astype(BF16)
            v_c = v_ref[rows, :]
            for h in range(GLA_HEADS):
                upd = lax.dot_general(v_c[:, _hv(h)], kd[:, _hk(h)], _DOT_DIMS["tn"],
                                      preferred_element_type=F32)
                st_h = state[:, _hk(h)] * gam[:, _hk(h)] + upd
                state[:, _hk(h)] = st_h
                o_scr[rows, _hv(h)] = lax.dot_general(qs[:, _hk(h)], st_h.astype(BF16), _DOT_DIMS["nt"],
                                                       preferred_element_type=F32)
            st_ref[ci] = state[...]
            return carry

        lax.fori_loop(0, nc, chunk, 0, unroll=True)
        for h in range(GLA_HEADS):
            o = o_scr[:, _hv(h)]
            rs = lax.rsqrt(jnp.mean(o * o, axis=-1, keepdims=True) + NORM_EPS)
            rv = r_ref[:, _hv(h)].astype(F32)
            oa_ref[:, _hv(h)] = ((o * rs * gn_ref[...]).astype(F32) * (rv * _sigmoid(rv))).astype(BF16)

    return pl.pallas_call(
        body, name="gla_fwd",
        out_shape=(jax.ShapeDtypeStruct((s, VAL_WIDTH), BF16),
                   jax.ShapeDtypeStruct((s // CHUNK, HEAD_V, KEY_WIDTH), F32)),
        grid=(nb,),
        in_specs=[pl.BlockSpec((tr, KEY_WIDTH), lambda i: (i, COL_Q // KEY_WIDTH)),
                  pl.BlockSpec((tr, KEY_WIDTH), lambda i: (i, COL_K // KEY_WIDTH)),
                  pl.BlockSpec((tr, VAL_WIDTH), lambda i: (i, COL_V // VAL_WIDTH)),
                  pl.BlockSpec((tr, VAL_WIDTH), lambda i: (i, COL_R // VAL_WIDTH)),
                  pl.BlockSpec((tr, FZ_PAD), lambda i: (i, COL_FZ // FZ_PAD)),
                  pl.BlockSpec((FZ_PAD, KEY_WIDTH), lambda i: (0, 0)),
                  pl.BlockSpec((1, KEY_WIDTH), lambda i: (0, 0)),
                  pl.BlockSpec((1, HEAD_V), lambda i: (0, 0))],
        out_specs=(pl.BlockSpec((tr, VAL_WIDTH), lambda i: (i, 0)),
                   pl.BlockSpec((nc, HEAD_V, KEY_WIDTH), lambda i: (i, 0, 0))),
        scratch_shapes=[pltpu.VMEM((HEAD_V, KEY_WIDTH), F32), pltpu.VMEM((tr, KEY_WIDTH), F32),
                        pltpu.VMEM((tr, VAL_WIDTH), F32)],
        compiler_params=_cparams(("arbitrary",)),
    )(proj, proj, proj, proj, proj, wfg2p, bfg, gn)


def _gla_bwd(proj, doa, states, wfg2p, bfg, gn, dproj, *, s):
    tr = min(512, s)
    nb = s // tr
    nc = tr // CHUNK

    def body(q_ref, k_ref, v_ref, r_ref, fz_ref, doa_ref, st_ref, stp_ref, w_ref, b_ref, gn_ref, dproj_in_ref,
             dp_ref, dfz_ref, dgn_ref, dw_ref, db_ref,
             carry, fg_scr, la_scr, dla_scr, o_scr, do_scr):
        i = pl.program_id(0)

        @pl.when(i == 0)
        def _():
            carry[...] = jnp.zeros_like(carry)
            dgn_ref[...] = jnp.zeros_like(dgn_ref)
            dw_ref[...] = jnp.zeros_like(dw_ref)
            db_ref[...] = jnp.zeros_like(db_ref)

        fz = fz_ref[...]
        fg = jnp.dot(fz, w_ref[...], preferred_element_type=F32) + b_ref[...]
        fg_scr[...] = fg
        la_scr[...] = _log_sigmoid(fg) * (1.0 / GATE_TAU)

        def fwd_chunk(ci, c0):
            r0 = pl.multiple_of(ci * CHUNK, CHUNK)
            rows = pl.ds(r0, CHUNK)
            qs = (q_ref[rows, :].astype(F32) * Q_SCALE).astype(BF16)
            st = st_ref[ci].astype(BF16)
            for h in range(GLA_HEADS):
                o_scr[rows, _hv(h)] = lax.dot_general(qs[:, _hk(h)], st[:, _hk(h)], _DOT_DIMS["nt"],
                                                       preferred_element_type=F32)
            return c0

        lax.fori_loop(0, nc, fwd_chunk, 0, unroll=True)

        gnv = gn_ref[...]
        dgn_part = jnp.zeros((1, HEAD_V), F32)
        for h in range(GLA_HEADS):
            o = o_scr[:, _hv(h)]
            rs = lax.rsqrt(jnp.mean(o * o, axis=-1, keepdims=True) + NORM_EPS)
            nrm = o * rs
            rv = r_ref[:, _hv(h)].astype(F32)
            sg = _sigmoid(rv)
            doa_h = doa_ref[:, _hv(h)].astype(F32)
            don = doa_h * (rv * sg)
            dp_ref[:, COL_R + h * HEAD_V:COL_R + (h + 1) * HEAD_V] = (
                doa_h * (nrm * gnv) * (sg * (1.0 + rv * (1.0 - sg)))).astype(BF16)
            dgn_part = dgn_part + jnp.sum(don * nrm, axis=0, keepdims=True)
            dn = don * gnv
            do_scr[:, _hv(h)] = (rs * (dn - nrm * jnp.mean(dn * nrm, axis=-1, keepdims=True))).astype(BF16)
        dgn_ref[...] += dgn_part

        tri = _tri(False)
        tri_s = _tri(True)
        first_block = i == nb - 1

        def bwd_chunk(cc, c0):
            ci = nc - 1 - cc
            r0 = pl.multiple_of(ci * CHUNK, CHUNK)
            rows = pl.ds(r0, CHUNK)
            e, gam = _gla_gate_terms(la_scr[rows, :], tri)
            k_c = k_ref[rows, :].astype(F32)
            kd = k_c * e
            kd_b = kd.astype(BF16)
            qs = (q_ref[rows, :].astype(F32) * Q_SCALE).astype(BF16)
            v_c = v_ref[rows, :]
            do_c = do_scr[rows, :]
            st = st_ref[ci]
            st_b = st.astype(BF16)
            st_prev_in = st_ref[jnp.maximum(ci - 1, 0)]
            st_prev_edge = jnp.where(first_block, 0.0, stp_ref[0])
            st_prev = jnp.where(ci > 0, st_prev_in, st_prev_edge)
            dkd_parts = []
            dgam_parts = []
            for h in range(GLA_HEADS):
                dst = lax.dot_general(do_c[:, _hv(h)], qs[:, _hk(h)], _DOT_DIMS["tn"],
                                      preferred_element_type=F32) + carry[:, _hk(h)]
                dst_b = dst.astype(BF16)
                dqs = jnp.dot(do_c[:, _hv(h)], st_b[:, _hk(h)], preferred_element_type=F32)
                dp_ref[rows, COL_Q + h * HEAD_K:COL_Q + (h + 1) * HEAD_K] = (dqs * Q_SCALE).astype(BF16)
                dkd_parts.append(jnp.dot(v_c[:, _hv(h)], dst_b, preferred_element_type=F32))
                dp_ref[rows, COL_V + h * HEAD_V:COL_V + (h + 1) * HEAD_V] = lax.dot_general(
                    kd_b[:, _hk(h)], dst_b, _DOT_DIMS["nt"], preferred_element_type=F32).astype(BF16)
                dgam_parts.append(jnp.sum(dst * st_prev[:, _hk(h)], axis=0, keepdims=True))
                carry[:, _hk(h)] = dst * gam[:, _hk(h)]
            dkd = jnp.concatenate(dkd_parts, axis=1)
            dgam = jnp.concatenate(dgam_parts, axis=1)
            dp_ref[rows, COL_K:COL_K + KEY_WIDTH] = (dkd * e).astype(BF16)
            dz = dkd * kd
            dla_scr[rows, :] = gam * dgam + jnp.dot(tri_s, dz, precision=lax.Precision.HIGHEST,
                                                    preferred_element_type=F32)
            return c0

        lax.fori_loop(0, nc, bwd_chunk, 0, unroll=True)

        dfg = dla_scr[...] * (1.0 / GATE_TAU) * _sigmoid(-fg_scr[...])
        dfg_b = dfg.astype(BF16)
        dfz_ref[...] = lax.dot_general(dfg_b, w_ref[...], _DOT_DIMS["nt"],
                                       preferred_element_type=F32).astype(BF16)
        dw_ref[...] += lax.dot_general(fz, dfg_b, _DOT_DIMS["tn"], preferred_element_type=F32)
        db_ref[...] += jnp.sum(dfg, axis=0, keepdims=True)

    def rev(i):
        return nb - 1 - i

    return pl.pallas_call(
        body, name="gla_bwd",
        out_shape=(jax.ShapeDtypeStruct((s, PROJ_W), BF16),
                   jax.ShapeDtypeStruct((s, FZ_PAD), BF16),
                   jax.ShapeDtypeStruct((1, HEAD_V), F32),
                   jax.ShapeDtypeStruct((FZ_PAD, KEY_WIDTH), F32),
                   jax.ShapeDtypeStruct((1, KEY_WIDTH), F32)),
        grid=(nb,),
        in_specs=[pl.BlockSpec((tr, KEY_WIDTH), lambda i: (rev(i), COL_Q // KEY_WIDTH)),
                  pl.BlockSpec((tr, KEY_WIDTH), lambda i: (rev(i), COL_K // KEY_WIDTH)),
                  pl.BlockSpec((tr, VAL_WIDTH), lambda i: (rev(i), COL_V // VAL_WIDTH)),
                  pl.BlockSpec((tr, VAL_WIDTH), lambda i: (rev(i), COL_R // VAL_WIDTH)),
                  pl.BlockSpec((tr, FZ_PAD), lambda i: (rev(i), COL_FZ // FZ_PAD)),
                  pl.BlockSpec((tr, VAL_WIDTH), lambda i: (rev(i), 0)),
                  pl.BlockSpec((nc, HEAD_V, KEY_WIDTH), lambda i: (rev(i), 0, 0)),
                  pl.BlockSpec((1, HEAD_V, KEY_WIDTH), lambda i: (jnp.maximum(rev(i) * nc - 1, 0), 0, 0)),
                  pl.BlockSpec((FZ_PAD, KEY_WIDTH), lambda i: (0, 0)),
                  pl.BlockSpec((1, KEY_WIDTH), lambda i: (0, 0)),
                  pl.BlockSpec((1, HEAD_V), lambda i: (0, 0)), ANY],
        out_specs=(pl.BlockSpec((tr, 3 * VAL_WIDTH), lambda i: (rev(i), 0)),
                   pl.BlockSpec((tr, FZ_PAD), lambda i: (rev(i), 0)),
                   pl.BlockSpec((1, HEAD_V), lambda i: (0, 0)),
                   pl.BlockSpec((FZ_PAD, KEY_WIDTH), lambda i: (0, 0)),
                   pl.BlockSpec((1, KEY_WIDTH), lambda i: (0, 0))),
        scratch_shapes=[pltpu.VMEM((HEAD_V, KEY_WIDTH), F32),
                        pltpu.VMEM((tr, KEY_WIDTH), F32),
                        pltpu.VMEM((tr, KEY_WIDTH), F32),
                        pltpu.VMEM((tr, KEY_WIDTH), F32),
                        pltpu.VMEM((tr, VAL_WIDTH), F32),
                        pltpu.VMEM((tr, VAL_WIDTH), BF16)],
        input_output_aliases={11: 0},
        compiler_params=_cparams(("arbitrary",)),
    )(proj, proj, proj, proj, proj, doa, states, states, wfg2p, bfg, gn, dproj)


def _put_fz(dproj, dfz, *, s):
    tr = min(512, s)

    def body(dfz_ref, dproj_in_ref, o_ref):
        o_ref[...] = dfz_ref[...]

    return pl.pallas_call(
        body, name="put_dfz",
        out_shape=jax.ShapeDtypeStruct((s, PROJ_W), BF16),
        grid=(s // tr,),
        in_specs=[pl.BlockSpec((tr, FZ_PAD), lambda i: (i, 0)), ANY],
        out_specs=pl.BlockSpec((tr, FZ_PAD), lambda i: (i, COL_FZ // FZ_PAD)),
        input_output_aliases={1: 0},
        compiler_params=_cparams(("parallel",)),
    )(dfz, dproj)


_ADAM_C1 = 1.0 / (1.0 - ADAM_B1 ** ADAM_STEP)
_ADAM_C2 = 1.0 / (1.0 - ADAM_B2 ** ADAM_STEP)


def _adamw_math(wv, gv, mv, vv):
    nm = ADAM_B1 * mv + (1.0 - ADAM_B1) * gv
    nv = ADAM_B2 * vv + (1.0 - ADAM_B2) * (gv * gv)
    delta = -ADAM_LR * ((nm * _ADAM_C1) / (jnp.sqrt(nv * _ADAM_C2) + ADAM_EPS) + ADAM_WD * wv)
    return delta, nm, nv


def _adamw(w, g, m, v, *, name):
    shape = w.shape
    if w.ndim == 1:
        w, g, m, v = (t.reshape(1, 1, -1) for t in (w, g, m, v))
    elif w.ndim == 2:
        w, g, m, v = (t.reshape((1,) + t.shape) for t in (w, g, m, v))
    l, a, b = w.shape

    def body(w_ref, g_ref, m_ref, v_ref, d_ref, nm_ref, nv_ref):
        d_ref[...], nm_ref[...], nv_ref[...] = _adamw_math(w_ref[...], g_ref[...], m_ref[...], v_ref[...])

    blk = pl.BlockSpec((1, a, b), lambda li: (li, 0, 0))
    outs = pl.pallas_call(
        body, name=name,
        out_shape=tuple(jax.ShapeDtypeStruct((l, a, b), F32) for _ in range(3)),
        grid=(l,),
        in_specs=[blk, blk, blk, blk],
        out_specs=(blk, blk, blk),
        compiler_params=_cparams(("parallel",)),
    )(w, g, m, v)
    return tuple(o.reshape(shape) for o in outs)


def _adamw_layers(w, reduced, received, on_core0, m, v, *, name):
    l, a, b = w.shape
    assert l == DEPTH == 2
    ta = _row_tile(a, F32_SUBLANES, 384)

    def body(flag_ref, w_ref, r0_ref, o0_ref, r1_ref, o1_ref, m_ref, v_ref, g_ref, d_ref, nm_ref, nv_ref):
        core0 = flag_ref[...] > 0.5
        g0 = jnp.where(core0, r0_ref[...], o0_ref[...])
        g1 = jnp.where(core0, o1_ref[...], r1_ref[...])
        gv = jnp.where(pl.program_id(0) == 0, g0, g1)
        g_ref[0] = gv
        d_ref[0], nm_ref[0], nv_ref[0] = _adamw_math(w_ref[0], gv, m_ref[0], v_ref[0])

    blk = pl.BlockSpec((1, ta, b), lambda li, ai: (li, ai, 0))
    gblk = pl.BlockSpec((ta, b), lambda li, ai: (ai, 0))
    return pl.pallas_call(
        body, name=name,
        out_shape=tuple(jax.ShapeDtypeStruct((l, a, b), F32) for _ in range(4)),
        grid=(l, a // ta),
        in_specs=[pl.BlockSpec((1, 1), lambda li, ai: (0, 0)), blk, gblk, gblk, gblk, gblk, blk, blk],
        out_specs=(blk, blk, blk, blk),
        compiler_params=_cparams(("parallel", "parallel")),
    )(on_core0, w, reduced[0], received[0], reduced[1], received[1], m, v)


MESH_ID = pl.DeviceIdType.MESH
ANY = pl.BlockSpec(memory_space=pl.ANY)


def _position():
    x, y, c = lax.axis_index("x"), lax.axis_index("y"), lax.axis_index("c")
    chips = [(1 - x, y), (x, 1 - y), (1 - x, 1 - y)]
    return x, y, c, chips


def _chip_index(xy):
    return 2 * xy[0] + xy[1]


HBM = pl.BlockSpec(memory_space=pltpu.HBM)
SEM = pl.BlockSpec(memory_space=pltpu.SEMAPHORE)
DATAFLOW_EFFECT = pltpu.SideEffectType.DATAFLOW_SIDE_EFFECTING
TOKEN_SHAPE = (8, LANE)


def _landing(shape, dtype):
    return pltpu.with_memory_space_constraint(lax.empty(shape, dtype), pltpu.HBM)


def _split_start(bufs, sem_shape, issue, *, name, after=None):
    n = len(bufs)
    n_in = n + (after is not None)

    def body(*refs):
        issue(refs[:n], refs[n_in], refs[n_in + 1])
        token = refs[-1]
        token[...] = jnp.zeros_like(token)

    operands = [pltpu.with_memory_space_constraint(t, pltpu.HBM) for t in bufs]
    outs = pl.pallas_call(
        body, name=name,
        out_shape=(pltpu.SemaphoreType.DMA(sem_shape), pltpu.SemaphoreType.DMA(sem_shape),
                   *[pltpu.HBM(t.shape, t.dtype) for t in bufs], jax.ShapeDtypeStruct(TOKEN_SHAPE, F32)),
        in_specs=[HBM] * n + [ANY] * (after is not None),
        out_specs=(SEM, SEM, *[HBM] * n, pl.BlockSpec(memory_space=pltpu.VMEM)),
        input_output_aliases={i: 2 + i for i in range(n)},
        compiler_params=pltpu.CompilerParams(has_side_effects=DATAFLOW_EFFECT),
    )(*operands, *([after] if after is not None else []))
    return outs[0], outs[1], list(outs[2:2 + n]), outs[-1]


def _split_wait(started, after, settle, *, name):
    send_sems, recv_sems, bufs, _ = started
    n = len(bufs)
    afters = tuple(after) if isinstance(after, (tuple, list)) else (after,)

    def body(*refs):
        settle(refs[:n], refs[n], refs[n + 1])

    outs = pl.pallas_call(
        body, name=name,
        out_shape=tuple(pltpu.HBM(t.shape, t.dtype) for t in bufs),
        in_specs=[HBM] * n + [SEM, SEM] + [ANY] * len(afters),
        out_specs=tuple([HBM] * n),
        input_output_aliases={i: i for i in range(n)},
        compiler_params=pltpu.CompilerParams(has_side_effects=DATAFLOW_EFFECT),
    )(*bufs, send_sems, recv_sems, *afters)
    return list(outs)


def _to_sibling(bufs, n, send_sems, recv_sems):
    x, y, c, _ = _position()
    return [pltpu.make_async_remote_copy(
        src_ref=bufs[a], dst_ref=bufs[n + a], send_sem=send_sems.at[a], recv_sem=recv_sems.at[a],
        device_id=(x, y, 1 - c), device_id_type=MESH_ID) for a in range(n)]


def _sibling_push_start(layer, sender_is_reducer, arrays, *, name):
    n = len(arrays)
    sender = layer if sender_is_reducer else 1 - layer

    def issue(bufs, send_sems, recv_sems):
        @pl.when(lax.axis_index("c") == sender)
        def _():
            for cp in _to_sibling(bufs, n, send_sems, recv_sems):
                cp.start()

    lands = [_landing(t.shape, t.dtype) for t in arrays]
    return _split_start(list(arrays) + lands, (n,), issue, name=name)


def _sibling_push_wait(layer, sender_is_reducer, started, after, *, name):
    n = len(started[2]) // 2
    sender = layer if sender_is_reducer else 1 - layer

    def settle(bufs, send_sems, recv_sems):
        c = lax.axis_index("c")

        @pl.when(c == sender)
        def _():
            for cp in _to_sibling(bufs, n, send_sems, recv_sems):
                cp.wait_send()

        @pl.when(c != sender)
        def _():
            for cp in _to_sibling(bufs, n, send_sems, recv_sems):
                cp.wait_recv()

    outs = _split_wait(started, after, settle, name=name)
    return outs[:n], outs[n:]


def _chip_copies(bufs, n, send_sems, recv_sems):
    x, y, c, chips = _position()
    return [pltpu.make_async_remote_copy(
        src_ref=bufs[a].at[_chip_index(chips[k])], dst_ref=bufs[n + a].at[k],
        send_sem=send_sems.at[3 * a + k], recv_sem=recv_sems.at[3 * a + k],
        device_id=(chips[k][0], chips[k][1], c), device_id_type=MESH_ID)
        for a in range(n) for k in range(3)]


def _chip_exchange_start(layer, parts, *, name):
    n = len(parts)

    def issue(bufs, send_sems, recv_sems):
        @pl.when(lax.axis_index("c") == layer)
        def _():
            for cp in _chip_copies(bufs, n, send_sems, recv_sems):
                cp.start()

    lands = [_landing((3,) + t.shape[1:], t.dtype) for t in parts]
    return _split_start(list(parts) + lands, (3 * n,), issue, name=name)


def _chip_exchange_wait(layer, started, after, *, name):
    n = len(started[2]) // 2

    def settle(bufs, send_sems, recv_sems):
        @pl.when(lax.axis_index("c") == layer)
        def _():
            for cp in _chip_copies(bufs, n, send_sems, recv_sems):
                cp.wait()

    outs = _split_wait(started, after, settle, name=name)
    return outs[:n], outs[n:]


def _gather_copies(layer, bufs, n, send_sems, recv_sems, arriving):
    x, y, c, chips = _position()
    me = _chip_index((x, y))
    return [pltpu.make_async_remote_copy(
        src_ref=bufs[a], dst_ref=bufs[n + a].at[_chip_index(chips[k]) if arriving else me],
        send_sem=send_sems.at[3 * a + k], recv_sem=recv_sems.at[3 * a + k],
        device_id=(chips[k][0], chips[k][1], c), device_id_type=MESH_ID)
        for a in range(n) for k in range(3)]


def _gather_start(layer, shards, *, name, after=None):
    n = len(shards)

    def issue(bufs, send_sems, recv_sems):
        @pl.when(lax.axis_index("c") == layer)
        def _():
            for cp in _gather_copies(layer, bufs, n, send_sems, recv_sems, False):
                cp.start()

    lands = [_landing((N_CHIPS,) + t.shape, t.dtype) for t in shards]
    return _split_start(list(shards) + lands, (3 * n,), issue, name=name, after=after)


def _gather_wait(layer, started, after, *, name):
    n = len(started[2]) // 2

    def settle(bufs, send_sems, recv_sems):
        @pl.when(lax.axis_index("c") == layer)
        def _():
            for cp in _gather_copies(layer, bufs, n, send_sems, recv_sems, False):
                cp.wait_send()
            for cp in _gather_copies(layer, bufs, n, send_sems, recv_sems, True):
                cp.wait_recv()

    return _split_wait(started, after, settle, name=name)[n:]


def _handoff_copies(bufs, n, send_sems, recv_sems):
    x, y, c, chips = _position()
    out = []
    for a in range(n):
        for k in range(3):
            slot = bufs[a].at[_chip_index(chips[k])]
            out.append(pltpu.make_async_remote_copy(
                src_ref=slot, dst_ref=slot, send_sem=send_sems.at[3 * a + k], recv_sem=recv_sems.at[3 * a + k],
                device_id=(x, y, 1 - c), device_id_type=MESH_ID))
    return out


def _handoff_start(layer, gathered, *, name):
    n = len(gathered)

    def issue(bufs, send_sems, recv_sems):
        @pl.when(lax.axis_index("c") == layer)
        def _():
            for cp in _handoff_copies(bufs, n, send_sems, recv_sems):
                cp.start()

    return _split_start(list(gathered), (3 * n,), issue, name=name)


def _handoff_wait(layer, started, after, *, name):
    n = len(started[2])

    def settle(bufs, send_sems, recv_sems):
        c = lax.axis_index("c")

        @pl.when(c == layer)
        def _():
            for cp in _handoff_copies(bufs, n, send_sems, recv_sems):
                cp.wait_send()

        @pl.when(c != layer)
        def _():
            for cp in _handoff_copies(bufs, n, send_sems, recv_sems):
                cp.wait_recv()

    return _split_wait(started, after, settle, name=name)


F32_SUBLANES = 8
BF16_SUBLANES = 16


def _row_tile(a, sublanes=BF16_SUBLANES, max_rows=704):
    best = None
    for cand in range(sublanes, min(a, max_rows) + 1, sublanes):
        if a % cand == 0:
            best = cand
    assert best is not None, a
    return best


def _pair_sum(mine, sib, *, name):
    nchip, a, b = sib.shape
    ta = _row_tile(a)

    def body(m_ref, s_ref, o_ref):
        o_ref[...] = (m_ref[...].astype(F32) + s_ref[...].astype(F32)).astype(BF16)

    blk = pl.BlockSpec((1, ta, b), lambda j, r: (j, r, 0))
    return pl.pallas_call(
        body, name=name,
        out_shape=jax.ShapeDtypeStruct((nchip, a, b), BF16),
        grid=(nchip, a // ta),
        in_specs=[blk, blk], out_specs=blk,
        compiler_params=_cparams(("parallel", "parallel")),
    )(mine, sib)


def _total_sum(own, recv, *, name):
    a, b = own.shape
    ta = _row_tile(a)

    def body(o_ref, r_ref, t_ref):
        acc = o_ref[...].astype(F32)
        for k in range(3):
            acc = acc + r_ref[k].astype(F32)
        t_ref[...] = acc

    blk = pl.BlockSpec((ta, b), lambda r: (r, 0))
    return pl.pallas_call(
        body, name=name,
        out_shape=jax.ShapeDtypeStruct((a, b), F32),
        grid=(a // ta,),
        in_specs=[blk, pl.BlockSpec((3, ta, b), lambda r: (0, r, 0))], out_specs=blk,
        compiler_params=_cparams(("parallel",)),
    )(own, recv)


def _all_reduce_small(packed):
    rows, width = packed.shape

    def body(x_ref, out_ref, gath, send_sems, recv_sems, local_sem):
        x, y, c, chips = _position()
        me, sibling = (x, y, c), (x, y, 1 - c)

        def slot(px, py, pc):
            return gath.at[4 * px + 2 * py + pc]

        def copy(k, block, to, src=None):
            return pltpu.make_async_remote_copy(
                src_ref=slot(*block) if src is None else src, dst_ref=slot(*block),
                send_sem=send_sems.at[k], recv_sem=recv_sems.at[k], device_id=to, device_id_type=MESH_ID)

        mine = pltpu.make_async_copy(x_ref, slot(*me), local_sem)
        mine.start()
        first = [copy(0, me, sibling, src=x_ref)]
        first += [copy(1 + j, me, (*chip, c), src=x_ref) for j, chip in enumerate(chips)]
        for cp in first:
            cp.start()
        passed = [copy(4 + j, (*chip, c), sibling) for j, chip in enumerate(chips)]
        for j, chip in enumerate(chips):
            copy(1 + j, (*chip, c), me).wait_recv()
            passed[j].start()
        copy(0, sibling, me).wait_recv()
        for j, chip in enumerate(chips):
            copy(4 + j, (*chip, 1 - c), me).wait_recv()
        for cp in first + passed:
            cp.wait_send()
        mine.wait()
        acc = gath[0]
        for d in range(1, N_DEV):
            acc = acc + gath[d]
        out_ref[...] = acc

    return pl.pallas_call(
        body, name="all_reduce_small",
        out_shape=jax.ShapeDtypeStruct((rows, width), F32),
        in_specs=[pl.BlockSpec(memory_space=pltpu.VMEM)],
        out_specs=pl.BlockSpec(memory_space=pltpu.VMEM),
        scratch_shapes=[pltpu.VMEM((N_DEV, rows, width), F32), pltpu.SemaphoreType.DMA((7,)),
                        pltpu.SemaphoreType.DMA((7,)), pltpu.SemaphoreType.DMA],
    )(packed)


def _mixer_forward_branches(x0, h, w, s, dep=None, before_conv=None):
    proj = _matmul(h, w["w_in_t"], mode="nt", out_dtype=BF16, tm=1024, tn=1664, tk=1024, name="proj_fwd",
                   dep=dep)
    oa, states = _gla_fwd(proj, w["w_fg2"], w["b_fg"], w["gla_norm_g"], s=s)
    conv_wb = w["conv_wb"]
    if before_conv is not None:
        conv_wb = conv_wb + before_conv(oa)[0, 0]
    cb_in = _conv_fwd(proj, conv_wb, s=s)
    return dict(x0=x0, h=h, proj=proj, oa=oa, states=states, cb_in=cb_in)


def _mixer_forward_out(sv, w, s, dep=None):
    ya = _matmul(sv["oa"], w["w_oa"], mode="nn", out_dtype=BF16, tm=1024, tn=1024, tk=1024, name="ya_fwd",
                 dep=dep)
    yb = _matmul(sv["cb_in"], w["w_ob"], mode="nn", out_dtype=BF16, tm=1024, tn=1024, tk=1024, name="yb_fwd")
    mix = _mix_fwd(sv["proj"], ya, yb, s=s)
    x1 = _matmul(mix, w["w_o"], mode="nn", out_dtype=F32, tm=1024, tn=1024, tk=1024, name="wo_fwd",
                 add=sv["x0"])
    return x1, dict(sv, ya=ya, yb=yb, mix=mix)


def _ffn_forward_hidden(x1, w, s, after_gate=None):
    h2 = _rms_fwd(x1, w["norm2_g"], name="rms2_fwd")
    gt = _matmul(h2, w["w_gate_t"], mode="nt", out_dtype=BF16, tm=1024, tn=1408, tk=1024, name="ffn_gate_fwd")
    up = _matmul(h2, w["w_up_t"], mode="nt", out_dtype=BF16, tm=1024, tn=1408, tk=1024, name="ffn_up_fwd",
                 dep=None if after_gate is None else after_gate(gt))
    return dict(x1=x1, h2=h2, gt=gt, up=up, hid=_swiglu_fwd(gt, up, s=s))


def _ffn_forward_out(sv, w, dep=None):
    return _matmul(sv["hid"], w["w_ffn_down"], mode="nn", out_dtype=F32, tm=1024, tn=1024, tk=FFN_HIDDEN,
                   name="ffn_out_fwd", add=sv["x1"], dep=dep)


def _ffn_backward_grads(dx2b, w, sv, s):
    g = {}
    dhid = _matmul(dx2b, w["w_ffn_down"], mode="nt", out_dtype=BF16, tm=1024, tn=FFN_HIDDEN, tk=1024,
                   name="ffn_out_bwd")
    g["w_ffn_down"] = _matmul(sv["hid"], dx2b, mode="tn", out_dtype=BF16, tm=1408, tn=1024, tk=1024,
                              name="ffn_out_wgrad")
    dgt, dup = _swiglu_bwd(sv["gt"], sv["up"], dhid, s=s)
    g["w_gate_t"] = _matmul(dgt, sv["h2"], mode="tn", out_dtype=BF16, tm=1408, tn=1024, tk=1024,
                            name="ffn_gate_wgrad")
    g["w_up_t"] = _matmul(dup, sv["h2"], mode="tn", out_dtype=BF16, tm=1408, tn=1024, tk=1024,
                          name="ffn_up_wgrad")
    return g, dgt, dup, dhid


def _ffn_backward_input(dgt, dup, dx2, w, sv, dep=None):
    dh2 = _matmul(dgt, w["w_gate_t"], mode="nn", out_dtype=BF16, tm=1024, tn=1024, tk=FFN_HIDDEN,
                  name="ffn_gate_bwd", dep=dep)
    dh2 = _matmul(dup, w["w_up_t"], mode="nn", out_dtype=BF16, tm=1024, tn=1024, tk=FFN_HIDDEN,
                  name="ffn_up_bwd", add=dh2)
    return _rms_bwd(sv["x1"], w["norm2_g"], dh2, dx2, name="rms2_bwd")


def _mixer_backward_branches(dx1b, w, sv, s, dep=None):
    g = {}
    dmix = _matmul(dx1b, w["w_o"], mode="nt", out_dtype=BF16, tm=1024, tn=1024, tk=1024, name="wo_bwd",
                   dep=dep)
    g["w_o"] = _matmul(sv["mix"], dx1b, mode="tn", out_dtype=BF16, tm=1024, tn=1024, tk=2048, name="wo_wgrad")
    dya, dyb, dproj = _mix_bwd(sv["proj"], sv["ya"], sv["yb"], dmix, s=s)
    dcb = _matmul(dyb, w["w_ob"], mode="nt", out_dtype=BF16, tm=1024, tn=1024, tk=1024, name="yb_bwd")
    g["w_ob"] = _matmul(sv["cb_in"], dyb, mode="tn", out_dtype=BF16, tm=1024, tn=1024, tk=2048, name="yb_wgrad")
    doa = _matmul(dya, w["w_oa"], mode="nt", out_dtype=BF16, tm=1024, tn=1024, tk=1024, name="ya_bwd")
    g["w_oa"] = _matmul(sv["oa"], dya, mode="tn", out_dtype=BF16, tm=1024, tn=1024, tk=2048, name="ya_wgrad")
    dproj, g["conv_wb"] = _conv_bwd(sv["proj"], dcb, w["conv_wb"], dproj, s=s)
    dproj, dfz, g["gla_norm_g"], g["w_fg2"], g["b_fg"] = _gla_bwd(
        sv["proj"], doa, sv["states"], w["w_fg2"], w["b_fg"], w["gla_norm_g"], dproj, s=s)
    return g, _put_fz(dproj, dfz, s=s)


def _proj_wgrad(dproj, sv, dep=None):
    return _matmul(dproj, sv["h"], mode="tn", out_dtype=BF16, tm=1664, tn=1024, tk=1024, name="proj_wgrad",
                   dep=dep)


def _proj_bwd(dproj, w, dep=None):
    return _matmul(dproj, w["w_in_t"], mode="nn", out_dtype=BF16, tm=1024, tn=1024, tk=1664, name="proj_bwd",
                   dep=dep)


def _cols_from_chips(t):
    return jnp.transpose(t, (1, 0, 2)).reshape(t.shape[1], -1)


W_IN_ROWS = IN_WIDTH // N_CHIPS
W_IN_ROWS_PAD = -(-W_IN_ROWS // BF16_SUBLANES) * BF16_SUBLANES


def _w_in_t_shard(t):
    return jnp.pad(jnp.transpose(t, (0, 2, 1)), ((0, 0), (0, W_IN_ROWS_PAD - W_IN_ROWS), (0, 0)))


def _w_in_t_unshard(t):
    return jnp.transpose(t[:, :W_IN_ROWS], (0, 2, 1))


def _w_in_row_segments():
    runs = [(0, 0, FZ_ORIG), (FZ_ORIG, FZ_ORIG + GATE_RANK, COL_FZ - FZ_ORIG), (COL_FZ, FZ_ORIG, GATE_RANK)]
    out = []
    for kernel_row, ref_row, length in runs:
        while length:
            chip, local = divmod(ref_row, W_IN_ROWS)
            n = min(length, W_IN_ROWS - local)
            out.append((kernel_row, chip * W_IN_ROWS_PAD + local, n))
            kernel_row, ref_row, length = kernel_row + n, ref_row + n, length - n
    return out


def _permute_rows(src, n_out, segments, *, name):
    n_src, width = src.shape
    block = _row_tile(n_out)
    window = block + BF16_SUBLANES
    assert n_src >= window and src.dtype == BF16

    def body(src_ref, out_ref):
        rows = lax.broadcasted_iota(jnp.int32, (block, window), 0)
        cols = lax.broadcasted_iota(jnp.int32, (block, window), 1)
        for b0 in range(0, n_out, block):
            acc = None
            for o, s, n in segments:
                lo, hi = max(o, b0), min(o + n, b0 + block)
                if lo >= hi:
                    continue
                first = s + (lo - o)
                base = max(0, min(first // BF16_SUBLANES * BF16_SUBLANES, n_src - window))
                shift = (first - base) - (lo - b0)
                pick = (cols == rows + shift) & (rows >= lo - b0) & (rows < hi - b0)
                part = jnp.dot(jnp.where(pick, 1.0, 0.0).astype(BF16), src_ref[base:base + window, :],
                               preferred_element_type=F32)
                acc = part if acc is None else acc + part
            out_ref[b0:b0 + block, :] = (jnp.zeros((block, width), F32) if acc is None else acc).astype(BF16)

    return pl.pallas_call(
        body, name=name,
        out_shape=jax.ShapeDtypeStruct((n_out, width), BF16),
        in_specs=[pl.BlockSpec(memory_space=pltpu.VMEM)],
        out_specs=pl.BlockSpec(memory_space=pltpu.VMEM),
        compiler_params=_cparams(),
    )(src)


def _w_in_t_to_kernel(t):
    return _permute_rows(t.reshape(N_CHIPS * W_IN_ROWS_PAD, D_MODEL), PROJ_W, _w_in_row_segments(),
                         name="w_in_to_kernel_rows")


def _w_in_t_from_kernel(g):
    segments = [(slab, kernel_row, n) for kernel_row, slab, n in _w_in_row_segments()]
    out = _permute_rows(g, N_CHIPS * W_IN_ROWS_PAD, segments, name="w_in_to_chip_rows")
    return out.reshape(N_CHIPS, W_IN_ROWS_PAD, D_MODEL)


def kernel(x, norm1_g, w_in, w_fg2, b_fg, gla_norm_g, w_oa, conv_w, conv_b, w_ob, w_o, norm2_g, w_ffn_gate, w_ffn_up, w_ffn_down, final_g, loss_target, m_norm1_g, m_w_in, m_w_fg2, m_b_fg, m_gla_norm_g, m_w_oa, m_conv_w, m_conv_b, m_w_ob, m_w_o, m_norm2_g, m_w_ffn_gate, m_w_ffn_up, m_w_ffn_down, m_final_g, v_norm1_g, v_w_in, v_w_fg2, v_b_fg, v_gla_norm_g, v_w_oa, v_conv_w, v_conv_b, v_w_ob, v_w_o, v_norm2_g, v_w_ffn_gate, v_w_ffn_up, v_w_ffn_down, v_final_g):
    cx_ = lax.axis_index("x")
    cy_ = lax.axis_index("y")
    cc_ = lax.axis_index("c")
    me = 2 * cx_ + cy_
    on_core0 = jnp.where(cc_ == 0, 1.0, 0.0).astype(F32).reshape(1, 1)

    def swap(t):
        return jnp.swapaxes(t, 1, 2)

    big_names = ["w_in", "w_oa", "w_ob", "w_o", "w_ffn_gate", "w_ffn_up", "w_ffn_down"]
    views = dict(
        w_in=tuple(_w_in_t_shard(t) for t in (w_in, m_w_in, v_w_in)),
        w_oa=(w_oa, m_w_oa, v_w_oa), w_ob=(w_ob, m_w_ob, v_w_ob), w_o=(w_o, m_w_o, v_w_o),
        w_ffn_gate=tuple(swap(t) for t in (w_ffn_gate, m_w_ffn_gate, v_w_ffn_gate)),
        w_ffn_up=tuple(swap(t) for t in (w_ffn_up, m_w_ffn_up, v_w_ffn_up)),
        w_ffn_down=(w_ffn_down, m_w_ffn_down, v_w_ffn_down))
    from_view = dict(w_in=_w_in_t_unshard, w_ffn_gate=swap, w_ffn_up=swap)

    s = x.shape[1]

    proj_names = ["w_in"]
    first_names = ["w_in", "w_fg2", "conv_w"]
    rest_names = ["w_oa", "w_ob", "w_o", "w_ffn_gate", "w_ffn_up", "w_ffn_down"]
    mixer_names = ["w_oa", "w_ob", "w_o"]
    ffn_names = ["w_ffn_gate", "w_ffn_up", "w_ffn_down"]
    weight_key = dict(w_in="w_in_t", w_ffn_gate="w_gate_t", w_ffn_up="w_up_t")
    full_shape = dict(w_oa=(VAL_WIDTH, D_MODEL), w_ob=(CONV_CH, D_MODEL), w_o=(D_MODEL, D_MODEL),
                      w_ffn_gate=(FFN_HIDDEN, D_MODEL), w_ffn_up=(FFN_HIDDEN, D_MODEL),
                      w_ffn_down=(FFN_HIDDEN, D_MODEL))

    conv_w_p = jnp.pad(conv_w, ((0, 0), (0, 8 - conv_w.shape[1]), (0, 0)))
    small_shards = dict(w_fg2=w_fg2, conv_w=conv_w_p)

    def shards_of(l, names):
        return [small_shards[n][l] if n in small_shards else views[n][0][l].astype(BF16) for n in names]

    def small_weights(l):
        return dict(norm1_g=norm1_g[l], norm2_g=norm2_g[l], b_fg=b_fg[l].reshape(1, KEY_WIDTH),
                    gla_norm_g=gla_norm_g[l].reshape(1, HEAD_V))

    def full_weights(l, names, gathered, shards):
        out = {}
        for n, g, t in zip(names, gathered, shards):
            g = lax.dynamic_update_index_in_dim(g, t[None], me, axis=0)
            if n == "w_in":
                out["w_in_t"] = _w_in_t_to_kernel(g)
            elif n == "w_fg2":
                out[n] = jnp.pad(_cols_from_chips(g), ((0, FZ_PAD - GATE_RANK), (0, 0))).astype(BF16)
            elif n == "conv_w":
                out["conv_wb"] = (jnp.pad(_cols_from_chips(g)[:3], ((0, 5), (0, 0)))
                                  + jnp.pad(conv_b[l].reshape(1, CONV_CH), ((3, 4), (0, 0))))
            else:
                out[weight_key.get(n, n)] = g.reshape(full_shape[n])
        return out

    def per_chip(g, names):
        out = []
        for n in names:
            t = g[weight_key.get(n, n)]
            out.append(_w_in_t_from_kernel(t) if n == "w_in"
                       else t.reshape(N_CHIPS, t.shape[0] // N_CHIPS, t.shape[1]))
        return out

    def gather(l, names, tag, after=None):
        shards = shards_of(l, names)
        return shards, _gather_start(l, shards, name=f"gather_{tag}_start", after=after)

    def gathered_to_sibling(l, started, after, tag):
        return _handoff_start(l, _gather_wait(l, started, after, name=f"gather_{tag}_wait"),
                              name=f"handoff_{tag}_start")

    def feed(l, g, names, tag):
        return _sibling_push_start(l, False, per_chip(g, names), name=f"feed_{tag}_start")

    def pair_and_exchange(l, fed, after, names, tag):
        mine, sib = _sibling_push_wait(l, False, fed, after, name=f"feed_{tag}_wait")
        parts = [_pair_sum(a, b, name=f"pair_sum_{tag}_{n}") for n, a, b in zip(names, mine, sib)]
        return _chip_exchange_start(l, parts, name=f"exchange_{tag}_start")

    def total_and_share(l, swapped, after, names, tag):
        parts, recvs = _chip_exchange_wait(l, swapped, after, name=f"exchange_{tag}_wait")
        owns = [lax.dynamic_index_in_dim(p, me, 0, keepdims=False) for p in parts]
        totals = [_total_sum(o, rc, name=f"total_{tag}_{n}") for n, o, rc in zip(names, owns, recvs)]
        return _sibling_push_start(l, True, totals, name=f"share_{tag}_start")

    def shared(l, sharing, after, names, tag):
        totals, others = _sibling_push_wait(l, True, sharing, after, name=f"share_{tag}_wait")
        return {n: (t, o) for n, t, o in zip(names, totals, others)}

    def branches_under_gather(l, xin, h, w, tag, shards_r, started_r):
        box = {}

        def before_conv(oa):
            box["handoff"] = gathered_to_sibling(l, started_r, oa, tag)
            return box["handoff"][3]

        sv = _mixer_forward_branches(xin, h, w, s, dep=started_r[3], before_conv=before_conv)
        rest = _handoff_wait(l, box["handoff"], sv["cb_in"], name=f"handoff_{tag}_wait")
        w.update(full_weights(l, rest_names, rest, shards_r))
        return sv

    shards_p, started = gather(0, first_names, "p0")
    h0 = _rms_fwd(x[0], norm1_g[0], name="rms1_fwd")
    started = gathered_to_sibling(0, started, (h0, views["w_in"][1]), "p0")
    w0 = small_weights(0)
    w0.update(full_weights(0, first_names, _handoff_wait(0, started, views["w_in"][2], name="handoff_p0_wait"),
                           shards_p))
    sv0 = branches_under_gather(0, x[0], h0, w0, "r0", *gather(0, rest_names, "r0", after=w0["w_in_t"]))

    shards_p1, started = gather(1, first_names, "p1", after=w0["w_o"])
    x1, sv0m = _mixer_forward_out(sv0, w0, s, dep=started[3])
    box1 = {}

    def after_gate(gt):
        box1["handoff"] = gathered_to_sibling(1, started, gt, "p1")
        box1["rest"] = gather(1, rest_names, "r1", after=box1["handoff"][2][0])
        return box1["handoff"][3] + box1["rest"][1][3]

    sv0f = _ffn_forward_hidden(x1, w0, s, after_gate=after_gate)
    x2 = _ffn_forward_out(sv0f, w0)
    w1 = small_weights(1)
    w1.update(full_weights(1, first_names, _handoff_wait(1, box1["handoff"], x2, name="handoff_p1_wait"),
                           shards_p1))

    h1 = _rms_fwd(x2, norm1_g[1], name="rms1_fwd")
    x3, sv1m = _mixer_forward_out(branches_under_gather(1, x2, h1, w1, "r1", *box1["rest"]), w1, s)
    sv1f = _ffn_forward_hidden(x3, w1, s)
    x4 = _ffn_forward_out(sv1f, w1)
    loss_local, dx, dxb, dgf = _loss_head(x4, final_g, loss_target[0])
    loss = lax.psum(loss_local[0, 0], ("x", "y", "c"))

    g1, dgt, dup, _ = _ffn_backward_grads(dxb, w1, sv1f, s)
    dx_mid, dxb_mid, g1["norm2_g"] = _ffn_backward_input(dgt, dup, dx, w1, sv1f)
    gm, dproj = _mixer_backward_branches(dxb_mid, w1, sv1m, s)
    g1.update(gm)
    g1["w_in_t"] = _proj_wgrad(dproj, sv1m)
    fed_1 = feed(1, g1, big_names, "l1")
    dh = _proj_bwd(dproj, w1, dep=fed_1[3])
    dx, dxb, g1["norm1_g"] = _rms_bwd(sv1m["x0"], norm1_g[1], dh, dx_mid, name="rms1_bwd")

    g0, dgt, dup, dhid = _ffn_backward_grads(dxb, w0, sv0f, s)
    swap_1 = pair_and_exchange(1, fed_1, dhid, big_names, "l1")
    fed_f = feed(0, g0, ffn_names, "f0")
    dx_mid, dxb_mid, g0["norm2_g"] = _ffn_backward_input(dgt, dup, dx, w0, sv0f, dep=swap_1[3] + fed_f[3])
    swap_f = pair_and_exchange(0, fed_f, dx_mid, ffn_names, "f0")
    gm, dproj = _mixer_backward_branches(dxb_mid, w0, sv0m, s, dep=swap_f[3])
    g0.update(gm)
    share_1 = total_and_share(1, swap_1, dproj, big_names, "l1")
    share_f = total_and_share(0, swap_f, dproj, ffn_names, "f0")
    fed_m = feed(0, g0, mixer_names, "m0")
    g0["w_in_t"] = _proj_wgrad(dproj, sv0m, dep=share_1[3] + share_f[3] + fed_m[3])
    fed_p = feed(0, g0, proj_names, "p0")
    swap_m = pair_and_exchange(0, fed_m, g0["w_in_t"], mixer_names, "m0")
    dh = _proj_bwd(dproj, w0, dep=fed_p[3] + swap_m[3])
    reduced1 = shared(1, share_1, dh, big_names, "l1")
    reduced0 = shared(0, share_f, dh, ffn_names, "f0")
    grad_x, _, g0["norm1_g"] = _rms_bwd(sv0m["x0"], norm1_g[0], dh, dx_mid, name="rms1_bwd")

    swap_p = pair_and_exchange(0, fed_p, grad_x, proj_names, "p0")
    share_m = total_and_share(0, swap_m, swap_p[3], mixer_names, "m0")
    reduced0.update(shared(0, share_m, share_m[3], mixer_names, "m0"))
    grads = [g0, g1]

    def small_rows(t):
        return t.reshape(-1, D_MODEL)

    def tile_rows(t):
        return jnp.pad(t, ((0, -t.shape[0] % F32_SUBLANES), (0, 0)))

    g0, g1 = grads
    pieces = [
        jnp.concatenate([g0["norm1_g"], g1["norm1_g"]], axis=0),
        jnp.concatenate([g0["norm2_g"], g1["norm2_g"]], axis=0),
        dgf,
        small_rows(jnp.concatenate([g0["b_fg"], g1["b_fg"]], axis=1)),
        small_rows(jnp.concatenate([g0["gla_norm_g"], g1["gla_norm_g"],
                                    jnp.zeros((1, D_MODEL - 2 * HEAD_V), F32)], axis=1)),
        jnp.concatenate([g0["conv_wb"][3:4], g1["conv_wb"][3:4]], axis=0),
        jnp.concatenate([g0["conv_wb"][:3], g1["conv_wb"][:3]], axis=0),
        small_rows(jnp.stack([g0["w_fg2"][:GATE_RANK], g1["w_fg2"][:GATE_RANK]])),
    ]
    small = _all_reduce_small(jnp.concatenate([tile_rows(p) for p in pieces], axis=0))
    sg = dict(
        norm1_g=small[0:2], norm2_g=small[8:10], final_g=small[16],
        b_fg=small[24].reshape(DEPTH, KEY_WIDTH), gla_norm_g=small[32, :DEPTH * HEAD_V].reshape(DEPTH, HEAD_V),
        conv_b=small[40:42],
        conv_w=lax.dynamic_slice_in_dim(small[48:54].reshape(DEPTH, 3, CONV_CH), me * (CONV_CH // N_CHIPS),
                                        CONV_CH // N_CHIPS, axis=2),
        w_fg2=lax.dynamic_slice_in_dim(small[56:72].reshape(DEPTH, GATE_RANK, KEY_WIDTH),
                                       me * (KEY_WIDTH // N_CHIPS), KEY_WIDTH // N_CHIPS, axis=2),
    )

    small_params = dict(norm1_g=(norm1_g, m_norm1_g, v_norm1_g), w_fg2=(w_fg2, m_w_fg2, v_w_fg2),
                        b_fg=(b_fg, m_b_fg, v_b_fg), gla_norm_g=(gla_norm_g, m_gla_norm_g, v_gla_norm_g),
                        conv_w=(conv_w, m_conv_w, v_conv_w), conv_b=(conv_b, m_conv_b, v_conv_b),
                        norm2_g=(norm2_g, m_norm2_g, v_norm2_g), final_g=(final_g, m_final_g, v_final_g))
    order = ["norm1_g", "w_in", "w_fg2", "b_fg", "gla_norm_g", "w_oa", "conv_w", "conv_b", "w_ob", "w_o",
             "norm2_g", "w_ffn_gate", "w_ffn_up", "w_ffn_down", "final_g"]
    results = {}

    def update_large(n):
        w_, m_, v_ = views[n]
        outs = _adamw_layers(w_, (reduced0[n][0], reduced1[n][0]), (reduced0[n][1], reduced1[n][1]), on_core0,
                             m_, v_, name="adamw_" + n)
        back = from_view.get(n)
        results[n] = tuple(back(r) for r in outs) if back else outs
        return outs[1]

    for n, (w_, m_, v_) in small_params.items():
        results[n] = (sg[n],) + _adamw(w_, sg[n], m_, v_, name="adamw_" + n)
    updated = tuple(update_large(n) for n in rest_names)
    share_p = total_and_share(0, swap_p, updated, proj_names, "p0")
    reduced0.update(shared(0, share_p, share_p[3], proj_names, "p0"))
    update_large("w_in")
    return (loss, grad_x[None], *[results[n][0] for n in order], *[results[n][1] for n in order],
            *[results[n][2] for n in order], *[results[n][3] for n in order])
```

```python
import functools

import jax
import jax.numpy as jnp
from jax import lax
from jax.experimental import pallas as pl
from jax.experimental.pallas import tpu as pltpu

F32 = jnp.float32
BF16 = jnp.bfloat16

D_MODEL = 1024
DEPTH = 2
CHUNK = 64
GLA_HEADS = 4
KEY_WIDTH = 512
VAL_WIDTH = 1024
HEAD_K = 128
HEAD_V = 256
GATE_RANK = 16
GATE_TAU = 16.0
CONV_CH = 1024
FFN_HIDDEN = 2816
IN_WIDTH = 8208
NORM_EPS = 1e-6
Q_SCALE = HEAD_K ** -0.5
ADAM_LR = 0.001
ADAM_B1 = 0.9
ADAM_B2 = 0.999
ADAM_EPS = 1e-08
ADAM_WD = 0.01
ADAM_STEP = 10

N_CHIPS = 4
N_DEV = 8

LANE = 128
FZ_PAD = LANE
PROJ_W = 8192 + FZ_PAD
COL_Q, COL_K, COL_V, COL_R, COL_GBI, COL_GCI, COL_CX, COL_GA, COL_GB, COL_FZ = (
    0, 512, 1024, 2048, 3072, 4096, 5120, 6144, 7168, 8192)
FZ_ORIG = 3072

VMEM_LIMIT = 52 * 1024 * 1024
HALO = 16


def _cparams(sem=None):
    return pltpu.CompilerParams(dimension_semantics=sem, vmem_limit_bytes=VMEM_LIMIT)


def _sigmoid(x):
    return jax.nn.sigmoid(x)


def _log_sigmoid(x):
    return jnp.minimum(x, 0.0) - jnp.log1p(jnp.exp(-jnp.abs(x)))


_DOT_DIMS = {
    "nn": (((1,), (0,)), ((), ())),
    "nt": (((1,), (1,)), ((), ())),
    "tn": (((0,), (0,)), ((), ())),
}


def _matmul(a, b, *, mode, out_dtype, tm, tn, tk, name, add=None, dep=None):
    if mode == "nn":
        (m, k), n = a.shape, b.shape[1]
    elif mode == "nt":
        (m, k), n = a.shape, b.shape[0]
    else:
        (k, m), n = a.shape, b.shape[1]
    tm, tn, tk = min(tm, m), min(tn, n), min(tk, k)
    assert m % tm == 0 and n % tn == 0 and k % tk == 0, (name, m, n, k, tm, tn, tk)
    nk = k // tk
    has_add = add is not None
    has_dep = dep is not None

    def body(*refs):
        if has_dep:
            refs = refs[1:]
        if has_add:
            a_ref, b_ref, add_ref, o_ref = refs[:4]
            scratch = refs[4:]
        else:
            a_ref, b_ref, o_ref = refs[:3]
            add_ref = None
            scratch = refs[3:]
        part = lax.dot_general(a_ref[...], b_ref[...], _DOT_DIMS[mode], preferred_element_type=F32)

        def finish(acc):
            if add_ref is not None:
                acc = acc + add_ref[...].astype(F32)
            o_ref[...] = acc.astype(o_ref.dtype)

        if nk == 1:
            finish(part)
        else:
            acc_ref = scratch[0]
            kk = pl.program_id(2)

            @pl.when(kk == 0)
            def _():
                acc_ref[...] = part

            @pl.when(kk > 0)
            def _():
                acc_ref[...] += part

            @pl.when(kk == nk - 1)
            def _():
                finish(acc_ref[...])

    if mode == "nn":
        a_spec = pl.BlockSpec((tm, tk), lambda i, j, kk: (i, kk))
        b_spec = pl.BlockSpec((tk, tn), lambda i, j, kk: (kk, j))
    elif mode == "nt":
        a_spec = pl.BlockSpec((tm, tk), lambda i, j, kk: (i, kk))
        b_spec = pl.BlockSpec((tn, tk), lambda i, j, kk: (j, kk))
    else:
        a_spec = pl.BlockSpec((tk, tm), lambda i, j, kk: (kk, i))
        b_spec = pl.BlockSpec((tk, tn), lambda i, j, kk: (kk, j))
    o_spec = pl.BlockSpec((tm, tn), lambda i, j, kk: (i, j))
    in_specs = [a_spec, b_spec] + ([o_spec] if has_add else [])
    operands = (a, b) + ((add,) if has_add else ())
    if has_dep:
        in_specs = [pl.BlockSpec(dep.shape, lambda i, j, kk: (0, 0))] + in_specs
        operands = (dep,) + operands
    return pl.pallas_call(
        body,
        name=name,
        out_shape=jax.ShapeDtypeStruct((m, n), out_dtype),
        grid=(m // tm, n // tn, nk),
        in_specs=in_specs,
        out_specs=o_spec,
        scratch_shapes=[pltpu.VMEM((tm, tn), F32)] if nk > 1 else [],
        compiler_params=_cparams(("parallel", "parallel", "arbitrary")),
    )(*operands)


def _rms_fwd(x, g, *, name):
    s, d = x.shape
    tr = min(512, s)

    def body(x_ref, g_ref, h_ref):
        xv = x_ref[...]
        rs = lax.rsqrt(jnp.mean(xv * xv, axis=-1, keepdims=True) + NORM_EPS)
        h_ref[...] = (xv * rs * g_ref[...]).astype(BF16)

    return pl.pallas_call(
        body, name=name,
        out_shape=jax.ShapeDtypeStruct((s, d), BF16),
        grid=(s // tr,),
        in_specs=[pl.BlockSpec((tr, d), lambda i: (i, 0)), pl.BlockSpec((1, d), lambda i: (0, 0))],
        out_specs=pl.BlockSpec((tr, d), lambda i: (i, 0)),
        compiler_params=_cparams(("parallel",)),
    )(x, g.reshape(1, d))


def _rms_bwd(x, g, dh, dres, *, name):
    s, d = x.shape
    tr = min(512, s)

    def body(x_ref, g_ref, dh_ref, dres_ref, dx_ref, dxb_ref, dg_ref):
        i = pl.program_id(0)
        xv = x_ref[...]
        rs = lax.rsqrt(jnp.mean(xv * xv, axis=-1, keepdims=True) + NORM_EPS)
        n = xv * rs
        dhv = dh_ref[...].astype(F32)
        dn = dhv * g_ref[...]
        dx = dres_ref[...] + rs * (dn - n * jnp.mean(dn * n, axis=-1, keepdims=True))
        dx_ref[...] = dx
        dxb_ref[...] = dx.astype(BF16)
        part = jnp.sum(dhv * n, axis=0, keepdims=True)

        @pl.when(i == 0)
        def _():
            dg_ref[...] = part

        @pl.when(i > 0)
        def _():
            dg_ref[...] += part

    row = pl.BlockSpec((tr, d), lambda i: (i, 0))
    vec = pl.BlockSpec((1, d), lambda i: (0, 0))
    return pl.pallas_call(
        body, name=name,
        out_shape=(jax.ShapeDtypeStruct((s, d), F32), jax.ShapeDtypeStruct((s, d), BF16),
                   jax.ShapeDtypeStruct((1, d), F32)),
        grid=(s // tr,),
        in_specs=[row, vec, row, row],
        out_specs=(row, row, vec),
        compiler_params=_cparams(("arbitrary",)),
    )(x, g.reshape(1, d), dh, dres)


def _loss_head(x, g, target):
    s, d = x.shape
    tr = min(512, s)

    def body(x_ref, g_ref, t_ref, loss_ref, dx_ref, dxb_ref, dg_ref):
        i = pl.program_id(0)
        xv = x_ref[...]
        rs = lax.rsqrt(jnp.mean(xv * xv, axis=-1, keepdims=True) + NORM_EPS)
        n = xv * rs
        gv = g_ref[...]
        err = n * gv - t_ref[...]
        row_loss = jnp.mean(err * err, axis=-1, keepdims=True)
        loss_part = 0.5 * jnp.sum(row_loss, axis=0, keepdims=True)
        dy = err * (1.0 / d)
        dn = dy * gv
        dx = rs * (dn - n * jnp.mean(dn * n, axis=-1, keepdims=True))
        dx_ref[...] = dx
        dxb_ref[...] = dx.astype(BF16)
        dg_part = jnp.sum(dy * n, axis=0, keepdims=True)

        @pl.when(i == 0)
        def _():
            loss_ref[...] = loss_part
            dg_ref[...] = dg_part

        @pl.when(i > 0)
        def _():
            loss_ref[...] += loss_part
            dg_ref[...] += dg_part

    row = pl.BlockSpec((tr, d), lambda i: (i, 0))
    vec = pl.BlockSpec((1, d), lambda i: (0, 0))
    one = pl.BlockSpec((1, 1), lambda i: (0, 0))
    return pl.pallas_call(
        body, name="loss_head",
        out_shape=(jax.ShapeDtypeStruct((1, 1), F32), jax.ShapeDtypeStruct((s, d), F32),
                   jax.ShapeDtypeStruct((s, d), BF16), jax.ShapeDtypeStruct((1, d), F32)),
        grid=(s // tr,),
        in_specs=[row, vec, row],
        out_specs=(one, row, row, vec),
        compiler_params=_cparams(("arbitrary",)),
    )(x, g.reshape(1, d), target)


def _conv_taps(u_prev, u, w_ref, rows):
    ext = jnp.concatenate([u_prev, u], axis=0)
    u1 = pltpu.roll(ext, 1, 0)[HALO:HALO + rows]
    u2 = pltpu.roll(ext, 2, 0)[HALO:HALO + rows]
    conv = w_ref[0:1, :] * u2 + w_ref[1:2, :] * u1 + w_ref[2:3, :] * u + w_ref[3:4, :]
    return conv, u1, u2


def _conv_fwd(proj, conv_wb, *, s):
    tr = min(512, s)
    c = CONV_CH
    hb = tr // HALO

    def body(gbi_ref, gci_ref, cx_ref, gci_h_ref, cx_h_ref, w_ref, o_ref):
        i = pl.program_id(0)
        u = gci_ref[...].astype(F32) * cx_ref[...].astype(F32)
        u_prev = gci_h_ref[...].astype(F32) * cx_h_ref[...].astype(F32)
        u_prev = jnp.where(i == 0, 0.0, u_prev)
        conv, _, _ = _conv_taps(u_prev, u, w_ref, tr)
        o_ref[...] = (gbi_ref[...].astype(F32) * conv).astype(BF16)

    def seg(col):
        return pl.BlockSpec((tr, c), lambda i: (i, col // c))

    def halo(col):
        return pl.BlockSpec((HALO, c), lambda i: (jnp.maximum(i * hb - 1, 0), col // c))

    return pl.pallas_call(
        body, name="conv_fwd",
        out_shape=jax.ShapeDtypeStruct((s, c), BF16),
        grid=(s // tr,),
        in_specs=[seg(COL_GBI), seg(COL_GCI), seg(COL_CX), halo(COL_GCI), halo(COL_CX),
                  pl.BlockSpec((8, c), lambda i: (0, 0))],
        out_specs=pl.BlockSpec((tr, c), lambda i: (i, 0)),
        compiler_params=_cparams(("parallel",)),
    )(proj, proj, proj, proj, proj, conv_wb)


def _conv_bwd(proj, dcb, conv_wb, dproj, *, s):
    tr = min(512, s)
    c = CONV_CH
    hb = tr // HALO
    nb = s // tr

    def body(gbi_ref, gci_ref, cx_ref, dcb_ref, gci_h_ref, cx_h_ref, gbi_n_ref, dcb_n_ref, w_ref,
             dproj_in_ref, dp_ref, dw_ref):
        i = pl.program_id(0)
        gbi = gbi_ref[...].astype(F32)
        gci = gci_ref[...].astype(F32)
        cx = cx_ref[...].astype(F32)
        dcb_v = dcb_ref[...].astype(F32)
        u = gci * cx
        u_prev = jnp.where(i == 0, 0.0, gci_h_ref[...].astype(F32) * cx_h_ref[...].astype(F32))
        conv, u1, u2 = _conv_taps(u_prev, u, w_ref, tr)
        dconv = dcb_v * gbi
        dconv_next = jnp.where(i == nb - 1, 0.0, dcb_n_ref[...].astype(F32) * gbi_n_ref[...].astype(F32))
        ext = jnp.concatenate([dconv, dconv_next], axis=0)
        n_ext = tr + HALO
        d1 = pltpu.roll(ext, n_ext - 1, 0)[0:tr]
        d2 = pltpu.roll(ext, n_ext - 2, 0)[0:tr]
        du = w_ref[2:3, :] * dconv + w_ref[1:2, :] * d1 + w_ref[0:1, :] * d2
        dp_ref[:, 0:c] = (dcb_v * conv).astype(BF16)
        dp_ref[:, c:2 * c] = (du * cx).astype(BF16)
        dp_ref[:, 2 * c:3 * c] = (du * gci).astype(BF16)
        part = jnp.concatenate([
            jnp.sum(dconv * u2, axis=0, keepdims=True),
            jnp.sum(dconv * u1, axis=0, keepdims=True),
            jnp.sum(dconv * u, axis=0, keepdims=True),
            jnp.sum(dconv, axis=0, keepdims=True),
            jnp.zeros((4, c), F32)], axis=0)

        @pl.when(i == 0)
        def _():
            dw_ref[...] = part

        @pl.when(i > 0)
        def _():
            dw_ref[...] += part

    def seg(col):
        return pl.BlockSpec((tr, c), lambda i: (i, col // c))

    def halo_prev(col):
        return pl.BlockSpec((HALO, c), lambda i: (jnp.maximum(i * hb - 1, 0), col // c))

    def halo_next(col):
        return pl.BlockSpec((HALO, c), lambda i: (jnp.minimum((i + 1) * hb, nb * hb - 1), col // c))

    return pl.pallas_call(
        body, name="conv_bwd",
        out_shape=(jax.ShapeDtypeStruct((s, PROJ_W), BF16), jax.ShapeDtypeStruct((8, c), F32)),
        grid=(nb,),
        in_specs=[seg(COL_GBI), seg(COL_GCI), seg(COL_CX), pl.BlockSpec((tr, c), lambda i: (i, 0)),
                  halo_prev(COL_GCI), halo_prev(COL_CX), halo_next(COL_GBI),
                  pl.BlockSpec((HALO, c), lambda i: (jnp.minimum((i + 1) * hb, nb * hb - 1), 0)),
                  pl.BlockSpec((8, c), lambda i: (0, 0)), ANY],
        out_specs=(pl.BlockSpec((tr, 3 * c), lambda i: (i, COL_GBI // (3 * c))),
                   pl.BlockSpec((8, c), lambda i: (0, 0))),
        input_output_aliases={9: 0},
        compiler_params=_cparams(("arbitrary",)),
    )(proj, proj, proj, dcb, proj, proj, proj, dcb, conv_wb, dproj)


def _mix_fwd(proj, ya, yb, *, s):
    tr = min(512, s)
    d = D_MODEL

    def body(ga_ref, gb_ref, ya_ref, yb_ref, o_ref):
        sa = _sigmoid(ga_ref[...].astype(F32))
        sb = _sigmoid(gb_ref[...].astype(F32))
        o_ref[...] = (sa * ya_ref[...].astype(F32) + sb * yb_ref[...].astype(F32)).astype(BF16)

    row = pl.BlockSpec((tr, d), lambda i: (i, 0))
    return pl.pallas_call(
        body, name="mix_fwd",
        out_shape=jax.ShapeDtypeStruct((s, d), BF16),
        grid=(s // tr,),
        in_specs=[pl.BlockSpec((tr, d), lambda i: (i, COL_GA // d)),
                  pl.BlockSpec((tr, d), lambda i: (i, COL_GB // d)), row, row],
        out_specs=row,
        compiler_params=_cparams(("parallel",)),
    )(proj, proj, ya, yb)


def _mix_bwd(proj, ya, yb, dmix, *, s):
    tr = min(512, s)
    d = D_MODEL

    def body(ga_ref, gb_ref, ya_ref, yb_ref, dm_ref, dya_ref, dyb_ref, dg_ref):
        sa = _sigmoid(ga_ref[...].astype(F32))
        sb = _sigmoid(gb_ref[...].astype(F32))
        dm = dm_ref[...].astype(F32)
        dya_ref[...] = (dm * sa).astype(BF16)
        dyb_ref[...] = (dm * sb).astype(BF16)
        dg_ref[:, 0:d] = (dm * ya_ref[...].astype(F32) * sa * (1.0 - sa)).astype(BF16)
        dg_ref[:, d:2 * d] = (dm * yb_ref[...].astype(F32) * sb * (1.0 - sb)).astype(BF16)

    row = pl.BlockSpec((tr, d), lambda i: (i, 0))
    return pl.pallas_call(
        body, name="mix_bwd",
        out_shape=(jax.ShapeDtypeStruct((s, d), BF16), jax.ShapeDtypeStruct((s, d), BF16),
                   jax.ShapeDtypeStruct((s, PROJ_W), BF16)),
        grid=(s // tr,),
        in_specs=[pl.BlockSpec((tr, d), lambda i: (i, COL_GA // d)),
                  pl.BlockSpec((tr, d), lambda i: (i, COL_GB // d)), row, row, row],
        out_specs=(row, row, pl.BlockSpec((tr, 2 * d), lambda i: (i, COL_GA // (2 * d)))),
        compiler_params=_cparams(("parallel",)),
    )(proj, proj, ya, yb, dmix)


def _swiglu_fwd(gt, up, *, s):
    tr = min(256, s)
    f = FFN_HIDDEN

    def body(gt_ref, up_ref, o_ref):
        gv = gt_ref[...].astype(F32)
        o_ref[...] = (gv * _sigmoid(gv) * up_ref[...].astype(F32)).astype(BF16)

    row = pl.BlockSpec((tr, f), lambda i: (i, 0))
    return pl.pallas_call(
        body, name="swiglu_fwd",
        out_shape=jax.ShapeDtypeStruct((s, f), BF16),
        grid=(s // tr,),
        in_specs=[row, row], out_specs=row,
        compiler_params=_cparams(("parallel",)),
    )(gt, up)


def _swiglu_bwd(gt, up, dhid, *, s):
    tr = min(256, s)
    f = FFN_HIDDEN

    def body(gt_ref, up_ref, dh_ref, dgt_ref, dup_ref):
        gv = gt_ref[...].astype(F32)
        uv = up_ref[...].astype(F32)
        dh = dh_ref[...].astype(F32)
        sg = _sigmoid(gv)
        dgt_ref[...] = (dh * uv * sg * (1.0 + gv * (1.0 - sg))).astype(BF16)
        dup_ref[...] = (dh * gv * sg).astype(BF16)

    row = pl.BlockSpec((tr, f), lambda i: (i, 0))
    return pl.pallas_call(
        body, name="swiglu_bwd",
        out_shape=(jax.ShapeDtypeStruct((s, f), BF16), jax.ShapeDtypeStruct((s, f), BF16)),
        grid=(s // tr,),
        in_specs=[row, row, row], out_specs=(row, row),
        compiler_params=_cparams(("parallel",)),
    )(gt, up, dhid)


def _tri(strict):
    r = lax.broadcasted_iota(jnp.int32, (CHUNK, CHUNK), 0)
    c = lax.broadcasted_iota(jnp.int32, (CHUNK, CHUNK), 1)
    return jnp.where((c < r) if strict else (c <= r), 1.0, 0.0).astype(F32)


def _gla_gate_terms(la_c, tri):
    cum = jnp.dot(tri, la_c, precision=lax.Precision.HIGHEST, preferred_element_type=F32)
    cend = cum[CHUNK - 1:CHUNK, :]
    return jnp.exp(cend - cum), jnp.exp(cend)


def _hk(h):
    return slice(h * HEAD_K, (h + 1) * HEAD_K)


def _hv(h):
    return slice(h * HEAD_V, (h + 1) * HEAD_V)


def _gla_fwd(proj, wfg2p, bfg, gn, *, s):
    tr = min(512, s)
    nb = s // tr
    nc = tr // CHUNK

    def body(q_ref, k_ref, v_ref, r_ref, fz_ref, w_ref, b_ref, gn_ref, oa_ref, st_ref,
             state, la_scr, o_scr):
        i = pl.program_id(0)

        @pl.when(i == 0)
        def _():
            state[...] = jnp.zeros_like(state)

        fg = jnp.dot(fz_ref[...], w_ref[...], preferred_element_type=F32) + b_ref[...]
        la_scr[...] = _log_sigmoid(fg) * (1.0 / GATE_TAU)
        tri = _tri(False)

        def chunk(ci, carry):
            r0 = pl.multiple_of(ci * CHUNK, CHUNK)
            rows = pl.ds(r0, CHUNK)
            e, gam = _gla_gate_terms(la_scr[rows, :], tri)
            kd = (k_ref[rows, :].astype(F32) * e).astype(BF16)
            qs = (q_ref[rows, :].astype(F32) * Q_SCALE).astype(BF16)
            v_c = v_ref[rows, :]
            for h in range(GLA_HEADS):
                upd = lax.dot_general(v_c[:, _hv(h)], kd[:, _hk(h)], _DOT_DIMS["tn"],
                                      preferred_element_type=F32)
                st_h = state[:, _hk(h)] * gam[:, _hk(h)] + upd
                state[:, _hk(h)] = st_h
                o_scr[rows, _hv(h)] = lax.dot_general(qs[:, _hk(h)], st_h.astype(BF16), _DOT_DIMS["nt"],
                                                       preferred_element_type=F32)
            st_ref[ci] = state[...]
            return carry

        lax.fori_loop(0, nc, chunk, 0, unroll=True)
        for h in range(GLA_HEADS):
            o = o_scr[:, _hv(h)]
            rs = lax.rsqrt(jnp.mean(o * o, axis=-1, keepdims=True) + NORM_EPS)
            rv = r_ref[:, _hv(h)].astype(F32)
            oa_ref[:, _hv(h)] = ((o * rs * gn_ref[...]).astype(F32) * (rv * _sigmoid(rv))).astype(BF16)

    return pl.pallas_call(
        body, name="gla_fwd",
        out_shape=(jax.ShapeDtypeStruct((s, VAL_WIDTH), BF16),
                   jax.ShapeDtypeStruct((s // CHUNK, HEAD_V, KEY_WIDTH), F32)),
        grid=(nb,),
        in_specs=[pl.BlockSpec((tr, KEY_WIDTH), lambda i: (i, COL_Q // KEY_WIDTH)),
                  pl.BlockSpec((tr, KEY_WIDTH), lambda i: (i, COL_K // KEY_WIDTH)),
                  pl.BlockSpec((tr, VAL_WIDTH), lambda i: (i, COL_V // VAL_WIDTH)),
                  pl.BlockSpec((tr, VAL_WIDTH), lambda i: (i, COL_R // VAL_WIDTH)),
                  pl.BlockSpec((tr, FZ_PAD), lambda i: (i, COL_FZ // FZ_PAD)),
                  pl.BlockSpec((FZ_PAD, KEY_WIDTH), lambda i: (0, 0)),
                  pl.BlockSpec((1, KEY_WIDTH), lambda i: (0, 0)),
                  pl.BlockSpec((1, HEAD_V), lambda i: (0, 0))],
        out_specs=(pl.BlockSpec((tr, VAL_WIDTH), lambda i: (i, 0)),
                   pl.BlockSpec((nc, HEAD_V, KEY_WIDTH), lambda i: (i, 0, 0))),
        scratch_shapes=[pltpu.VMEM((HEAD_V, KEY_WIDTH), F32), pltpu.VMEM((tr, KEY_WIDTH), F32),
                        pltpu.VMEM((tr, VAL_WIDTH), F32)],
        compiler_params=_cparams(("arbitrary",)),
    )(proj, proj, proj, proj, proj, wfg2p, bfg, gn)


def _gla_bwd(proj, doa, states, wfg2p, bfg, gn, dproj, *, s):
    tr = min(512, s)
    nb = s // tr
    nc = tr // CHUNK

    def body(q_ref, k_ref, v_ref, r_ref, fz_ref, doa_ref, st_ref, stp_ref, w_ref, b_ref, gn_ref, dproj_in_ref,
             dp_ref, dfz_ref, dgn_ref, dw_ref, db_ref,
             carry, fg_scr, la_scr, dla_scr, o_scr, do_scr):
        i = pl.program_id(0)

        @pl.when(i == 0)
        def _():
            carry[...] = jnp.zeros_like(carry)
            dgn_ref[...] = jnp.zeros_like(dgn_ref)
            dw_ref[...] = jnp.zeros_like(dw_ref)
            db_ref[...] = jnp.zeros_like(db_ref)

        fz = fz_ref[...]
        fg = jnp.dot(fz, w_ref[...], preferred_element_type=F32) + b_ref[...]
        fg_scr[...] = fg
        la_scr[...] = _log_sigmoid(fg) * (1.0 / GATE_TAU)

        def fwd_chunk(ci, c0):
            r0 = pl.multiple_of(ci * CHUNK, CHUNK)
            rows = pl.ds(r0, CHUNK)
            qs = (q_ref[rows, :].astype(F32) * Q_SCALE).astype(BF16)
            st = st_ref[ci].astype(BF16)
            for h in range(GLA_HEADS):
                o_scr[rows, _hv(h)] = lax.dot_general(qs[:, _hk(h)], st[:, _hk(h)], _DOT_DIMS["nt"],
                                                       preferred_element_type=F32)
            return c0

        lax.fori_loop(0, nc, fwd_chunk, 0, unroll=True)

        gnv = gn_ref[...]
        dgn_part = jnp.zeros((1, HEAD_V), F32)
        for h in range(GLA_HEADS):
            o = o_scr[:, _hv(h)]
            rs = lax.rsqrt(jnp.mean(o * o, axis=-1, keepdims=True) + NORM_EPS)
            nrm = o * rs
            rv = r_ref[:, _hv(h)].astype(F32)
            sg = _sigmoid(rv)
            doa_h = doa_ref[:, _hv(h)].astype(F32)
            don = doa_h * (rv * sg)
            dp_ref[:, COL_R + h * HEAD_V:COL_R + (h + 1) * HEAD_V] = (
                doa_h * (nrm * gnv) * (sg * (1.0 + rv * (1.0 - sg)))).astype(BF16)
            dgn_part = dgn_part + jnp.sum(don * nrm, axis=0, keepdims=True)
            dn = don * gnv
            do_scr[:, _hv(h)] = (rs * (dn - nrm * jnp.mean(dn * nrm, axis=-1, keepdims=True))).astype(BF16)
        dgn_ref[...] += dgn_part

        tri = _tri(False)
        tri_s = _tri(True)
        first_block = i == nb - 1

        def bwd_chunk(cc, c0):
            ci = nc - 1 - cc
            r0 = pl.multiple_of(ci * CHUNK, CHUNK)
            rows = pl.ds(r0, CHUNK)
            e, gam = _gla_gate_terms(la_scr[rows, :], tri)
            k_c = k_ref[rows, :].astype(F32)
            kd = k_c * e
            kd_b = kd.astype(BF16)
            qs = (q_ref[rows, :].astype(F32) * Q_SCALE).astype(BF16)
            v_c = v_ref[rows, :]
            do_c = do_scr[rows, :]
            st = st_ref[ci]
            st_b = st.astype(BF16)
            st_prev_in = st_ref[jnp.maximum(ci - 1, 0)]
            st_prev_edge = jnp.where(first_block, 0.0, stp_ref[0])
            st_prev = jnp.where(ci > 0, st_prev_in, st_prev_edge)
            dkd_parts = []
            dgam_parts = []
            for h in range(GLA_HEADS):
                dst = lax.dot_general(do_c[:, _hv(h)], qs[:, _hk(h)], _DOT_DIMS["tn"],
                                      preferred_element_type=F32) + carry[:, _hk(h)]
                dst_b = dst.astype(BF16)
                dqs = jnp.dot(do_c[:, _hv(h)], st_b[:, _hk(h)], preferred_element_type=F32)
                dp_ref[rows, COL_Q + h * HEAD_K:COL_Q + (h + 1) * HEAD_K] = (dqs * Q_SCALE).astype(BF16)
                dkd_parts.append(jnp.dot(v_c[:, _hv(h)], dst_b, preferred_element_type=F32))
                dp_ref[rows, COL_V + h * HEAD_V:COL_V + (h + 1) * HEAD_V] = lax.dot_general(
                    kd_b[:, _hk(h)], dst_b, _DOT_DIMS["nt"], preferred_element_type=F32).astype(BF16)
                dgam_parts.append(jnp.sum(dst * st_prev[:, _hk(h)], axis=0, keepdims=True))
                carry[:, _hk(h)] = dst * gam[:, _hk(h)]
            dkd = jnp.concatenate(dkd_parts, axis=1)
            dgam = jnp.concatenate(dgam_parts, axis=1)
            dp_ref[rows, COL_K:COL_K + KEY_WIDTH] = (dkd * e).astype(BF16)
            dz = dkd * kd
            dla_scr[rows, :] = gam * dgam + jnp.dot(tri_s, dz, precision=lax.Precision.HIGHEST,
                                                    preferred_element_type=F32)
            return c0

        lax.fori_loop(0, nc, bwd_chunk, 0, unroll=True)

        dfg = dla_scr[...] * (1.0 / GATE_TAU) * _sigmoid(-fg_scr[...])
        dfg_b = dfg.astype(BF16)
        dfz_ref[...] = lax.dot_general(dfg_b, w_ref[...], _DOT_DIMS["nt"],
                                       preferred_element_type=F32).astype(BF16)
        dw_ref[...] += lax.dot_general(fz, dfg_b, _DOT_DIMS["tn"], preferred_element_type=F32)
        db_ref[...] += jnp.sum(dfg, axis=0, keepdims=True)

    def rev(i):
        return nb - 1 - i

    return pl.pallas_call(
        body, name="gla_bwd",
        out_shape=(jax.ShapeDtypeStruct((s, PROJ_W), BF16),
                   jax.ShapeDtypeStruct((s, FZ_PAD), BF16),
                   jax.ShapeDtypeStruct((1, HEAD_V), F32),
                   jax.ShapeDtypeStruct((FZ_PAD, KEY_WIDTH), F32),
                   jax.ShapeDtypeStruct((1, KEY_WIDTH), F32)),
        grid=(nb,),
        in_specs=[pl.BlockSpec((tr, KEY_WIDTH), lambda i: (rev(i), COL_Q // KEY_WIDTH)),
                  pl.BlockSpec((tr, KEY_WIDTH), lambda i: (rev(i), COL_K // KEY_WIDTH)),
                  pl.BlockSpec((tr, VAL_WIDTH), lambda i: (rev(i), COL_V // VAL_WIDTH)),
                  pl.BlockSpec((tr, VAL_WIDTH), lambda i: (rev(i), COL_R // VAL_WIDTH)),
                  pl.BlockSpec((tr, FZ_PAD), lambda i: (rev(i), COL_FZ // FZ_PAD)),
                  pl.BlockSpec((tr, VAL_WIDTH), lambda i: (rev(i), 0)),
                  pl.BlockSpec((nc, HEAD_V, KEY_WIDTH), lambda i: (rev(i), 0, 0)),
                  pl.BlockSpec((1, HEAD_V, KEY_WIDTH), lambda i: (jnp.maximum(rev(i) * nc - 1, 0), 0, 0)),
                  pl.BlockSpec((FZ_PAD, KEY_WIDTH), lambda i: (0, 0)),
                  pl.BlockSpec((1, KEY_WIDTH), lambda i: (0, 0)),
                  pl.BlockSpec((1, HEAD_V), lambda i: (0, 0)), ANY],
        out_specs=(pl.BlockSpec((tr, 3 * VAL_WIDTH), lambda i: (rev(i), 0)),
                   pl.BlockSpec((tr, FZ_PAD), lambda i: (rev(i), 0)),
                   pl.BlockSpec((1, HEAD_V), lambda i: (0, 0)),
                   pl.BlockSpec((FZ_PAD, KEY_WIDTH), lambda i: (0, 0)),
                   pl.BlockSpec((1, KEY_WIDTH), lambda i: (0, 0))),
        scratch_shapes=[pltpu.VMEM((HEAD_V, KEY_WIDTH), F32),
                        pltpu.VMEM((tr, KEY_WIDTH), F32),
                        pltpu.VMEM((tr, KEY_WIDTH), F32),
                        pltpu.VMEM((tr, KEY_WIDTH), F32),
                        pltpu.VMEM((tr, VAL_WIDTH), F32),
                        pltpu.VMEM((tr, VAL_WIDTH), BF16)],
        input_output_aliases={11: 0},
        compiler_params=_cparams(("arbitrary",)),
    )(proj, proj, proj, proj, proj, doa, states, states, wfg2p, bfg, gn, dproj)


def _put_fz(dproj, dfz, *, s):
    tr = min(512, s)

    def body(dfz_ref, dproj_in_ref, o_ref):
        o_ref[...] = dfz_ref[...]

    return pl.pallas_call(
        body, name="put_dfz",
        out_shape=jax.ShapeDtypeStruct((s, PROJ_W), BF16),
        grid=(s // tr,),
        in_specs=[pl.BlockSpec((tr, FZ_PAD), lambda i: (i, 0)), ANY],
        out_specs=pl.BlockSpec((tr, FZ_PAD), lambda i: (i, COL_FZ // FZ_PAD)),
        input_output_aliases={1: 0},
        compiler_params=_cparams(("parallel",)),
    )(dfz, dproj)


_ADAM_C1 = 1.0 / (1.0 - ADAM_B1 ** ADAM_STEP)
_ADAM_C2 = 1.0 / (1.0 - ADAM_B2 ** ADAM_STEP)


def _adamw_math(wv, gv, mv, vv):
    nm = ADAM_B1 * mv + (1.0 - ADAM_B1) * gv
    nv = ADAM_B2 * vv + (1.0 - ADAM_B2) * (gv * gv)
    delta = -ADAM_LR * ((nm * _ADAM_C1) / (jnp.sqrt(nv * _ADAM_C2) + ADAM_EPS) + ADAM_WD * wv)
    return delta, nm, nv


def _adamw(w, g, m, v, *, name):
    shape = w.shape
    if w.ndim == 1:
        w, g, m, v = (t.reshape(1, 1, -1) for t in (w, g, m, v))
    elif w.ndim == 2:
        w, g, m, v = (t.reshape((1,) + t.shape) for t in (w, g, m, v))
    l, a, b = w.shape

    def body(w_ref, g_ref, m_ref, v_ref, d_ref, nm_ref, nv_ref):
        d_ref[...], nm_ref[...], nv_ref[...] = _adamw_math(w_ref[...], g_ref[...], m_ref[...], v_ref[...])

    blk = pl.BlockSpec((1, a, b), lambda li: (li, 0, 0))
    outs = pl.pallas_call(
        body, name=name,
        out_shape=tuple(jax.ShapeDtypeStruct((l, a, b), F32) for _ in range(3)),
        grid=(l,),
        in_specs=[blk, blk, blk, blk],
        out_specs=(blk, blk, blk),
        compiler_params=_cparams(("parallel",)),
    )(w, g, m, v)
    return tuple(o.reshape(shape) for o in outs)


def _adamw_layers(w, reduced, received, on_core0, m, v, *, name):
    l, a, b = w.shape
    assert l == DEPTH == 2
    ta = _row_tile(a, F32_SUBLANES, 384)

    def body(flag_ref, w_ref, r0_ref, o0_ref, r1_ref, o1_ref, m_ref, v_ref, g_ref, d_ref, nm_ref, nv_ref):
        core0 = flag_ref[...] > 0.5
        g0 = jnp.where(core0, r0_ref[...], o0_ref[...])
        g1 = jnp.where(core0, o1_ref[...], r1_ref[...])
        gv = jnp.where(pl.program_id(0) == 0, g0, g1)
        g_ref[0] = gv
        d_ref[0], nm_ref[0], nv_ref[0] = _adamw_math(w_ref[0], gv, m_ref[0], v_ref[0])

    blk = pl.BlockSpec((1, ta, b), lambda li, ai: (li, ai, 0))

    def grad_block(layer):
        return pl.BlockSpec((ta, b), lambda li, ai: (jnp.where(li == layer, ai, 0), 0))

    g0blk, g1blk = grad_block(0), grad_block(1)
    return pl.pallas_call(
        body, name=name,
        out_shape=tuple(jax.ShapeDtypeStruct((l, a, b), F32) for _ in range(4)),
        grid=(l, a // ta),
        in_specs=[pl.BlockSpec((1, 1), lambda li, ai: (0, 0)), blk, g0blk, g0blk, g1blk, g1blk, blk, blk],
        out_specs=(blk, blk, blk, blk),
        compiler_params=_cparams(("parallel", "parallel")),
    )(on_core0, w, reduced[0], received[0], reduced[1], received[1], m, v)


MESH_ID = pl.DeviceIdType.MESH
ANY = pl.BlockSpec(memory_space=pl.ANY)


def _position():
    x, y, c = lax.axis_index("x"), lax.axis_index("y"), lax.axis_index("c")
    chips = [(1 - x, y), (x, 1 - y), (1 - x, 1 - y)]
    return x, y, c, chips


def _chip_index(xy):
    return 2 * xy[0] + xy[1]


HBM = pl.BlockSpec(memory_space=pltpu.HBM)
SEM = pl.BlockSpec(memory_space=pltpu.SEMAPHORE)
DATAFLOW_EFFECT = pltpu.SideEffectType.DATAFLOW_SIDE_EFFECTING
TOKEN_SHAPE = (8, LANE)


def _landing(shape, dtype):
    return pltpu.with_memory_space_constraint(lax.empty(shape, dtype), pltpu.HBM)


def _split_start(bufs, sem_shape, issue, *, name, after=None):
    n = len(bufs)
    n_in = n + (after is not None)

    def body(*refs):
        issue(refs[:n], refs[n_in], refs[n_in + 1])
        token = refs[-1]
        token[...] = jnp.zeros_like(token)

    operands = [pltpu.with_memory_space_constraint(t, pltpu.HBM) for t in bufs]
    outs = pl.pallas_call(
        body, name=name,
        out_shape=(pltpu.SemaphoreType.DMA(sem_shape), pltpu.SemaphoreType.DMA(sem_shape),
                   *[pltpu.HBM(t.shape, t.dtype) for t in bufs], jax.ShapeDtypeStruct(TOKEN_SHAPE, F32)),
        in_specs=[HBM] * n + [ANY] * (after is not None),
        out_specs=(SEM, SEM, *[HBM] * n, pl.BlockSpec(memory_space=pltpu.VMEM)),
        input_output_aliases={i: 2 + i for i in range(n)},
        compiler_params=pltpu.CompilerParams(has_side_effects=DATAFLOW_EFFECT),
    )(*operands, *([after] if after is not None else []))
    return outs[0], outs[1], list(outs[2:2 + n]), outs[-1]


def _split_wait(started, after, settle, *, name):
    send_sems, recv_sems, bufs, _ = started
    n = len(bufs)
    afters = tuple(after) if isinstance(after, (tuple, list)) else (after,)

    def body(*refs):
        settle(refs[:n], refs[n], refs[n + 1])

    outs = pl.pallas_call(
        body, name=name,
        out_shape=tuple(pltpu.HBM(t.shape, t.dtype) for t in bufs),
        in_specs=[HBM] * n + [SEM, SEM] + [ANY] * len(afters),
        out_specs=tuple([HBM] * n),
        input_output_aliases={i: i for i in range(n)},
        compiler_params=pltpu.CompilerParams(has_side_effects=DATAFLOW_EFFECT),
    )(*bufs, send_sems, recv_sems, *afters)
    return list(outs)


def _to_sibling(bufs, n, send_sems, recv_sems):
    x, y, c, _ = _position()
    return [pltpu.make_async_remote_copy(
        src_ref=bufs[a], dst_ref=bufs[n + a], send_sem=send_sems.at[a], recv_sem=recv_sems.at[a],
        device_id=(x, y, 1 - c), device_id_type=MESH_ID) for a in range(n)]


def _sibling_push_start(layer, sender_is_reducer, arrays, *, name):
    n = len(arrays)
    sender = layer if sender_is_reducer else 1 - layer

    def issue(bufs, send_sems, recv_sems):
        @pl.when(lax.axis_index("c") == sender)
        def _():
            for cp in _to_sibling(bufs, n, send_sems, recv_sems):
                cp.start()

    lands = [_landing(t.shape, t.dtype) for t in arrays]
    return _split_start(list(arrays) + lands, (n,), issue, name=name)


def _sibling_push_wait(layer, sender_is_reducer, started, after, *, name):
    n = len(started[2]) // 2
    sender = layer if sender_is_reducer else 1 - layer

    def settle(bufs, send_sems, recv_sems):
        c = lax.axis_index("c")

        @pl.when(c == sender)
        def _():
            for cp in _to_sibling(bufs, n, send_sems, recv_sems):
                cp.wait_send()

        @pl.when(c != sender)
        def _():
            for cp in _to_sibling(bufs, n, send_sems, recv_sems):
                cp.wait_recv()

    outs = _split_wait(started, after, settle, name=name)
    return outs[:n], outs[n:]


def _chip_copies(bufs, n, send_sems, recv_sems):
    x, y, c, chips = _position()
    return [pltpu.make_async_remote_copy(
        src_ref=bufs[a].at[_chip_index(chips[k])], dst_ref=bufs[n + a].at[k],
        send_sem=send_sems.at[3 * a + k], recv_sem=recv_sems.at[3 * a + k],
        device_id=(chips[k][0], chips[k][1], c), device_id_type=MESH_ID)
        for a in range(n) for k in range(3)]


def _chip_exchange_start(layer, parts, *, name):
    n = len(parts)

    def issue(bufs, send_sems, recv_sems):
        @pl.when(lax.axis_index("c") == layer)
        def _():
            for cp in _chip_copies(bufs, n, send_sems, recv_sems):
                cp.start()

    lands = [_landing((3,) + t.shape[1:], t.dtype) for t in parts]
    return _split_start(list(parts) + lands, (3 * n,), issue, name=name)


def _chip_exchange_wait(layer, started, after, *, name):
    n = len(started[2]) // 2

    def settle(bufs, send_sems, recv_sems):
        @pl.when(lax.axis_index("c") == layer)
        def _():
            for cp in _chip_copies(bufs, n, send_sems, recv_sems):
                cp.wait()

    outs = _split_wait(started, after, settle, name=name)
    return outs[:n], outs[n:]


def _gather_copies(layer, bufs, n, send_sems, recv_sems, arriving):
    x, y, c, chips = _position()
    me = _chip_index((x, y))
    return [pltpu.make_async_remote_copy(
        src_ref=bufs[a], dst_ref=bufs[n + a].at[_chip_index(chips[k]) if arriving else me],
        send_sem=send_sems.at[3 * a + k], recv_sem=recv_sems.at[3 * a + k],
        device_id=(chips[k][0], chips[k][1], c), device_id_type=MESH_ID)
        for a in range(n) for k in range(3)]


def _gather_start(layer, shards, *, name, after=None):
    n = len(shards)

    def issue(bufs, send_sems, recv_sems):
        @pl.when(lax.axis_index("c") == layer)
        def _():
            for cp in _gather_copies(layer, bufs, n, send_sems, recv_sems, False):
                cp.start()

    lands = [_landing((N_CHIPS,) + t.shape, t.dtype) for t in shards]
    return _split_start(list(shards) + lands, (3 * n,), issue, name=name, after=after)


def _gather_wait(layer, started, after, *, name):
    n = len(started[2]) // 2

    def settle(bufs, send_sems, recv_sems):
        @pl.when(lax.axis_index("c") == layer)
        def _():
            for cp in _gather_copies(layer, bufs, n, send_sems, recv_sems, False):
                cp.wait_send()
            for cp in _gather_copies(layer, bufs, n, send_sems, recv_sems, True):
                cp.wait_recv()

    return _split_wait(started, after, settle, name=name)[n:]


def _handoff_copies(bufs, n, send_sems, recv_sems):
    x, y, c, chips = _position()
    out = []
    for a in range(n):
        for k in range(3):
            slot = bufs[a].at[_chip_index(chips[k])]
            out.append(pltpu.make_async_remote_copy(
                src_ref=slot, dst_ref=slot, send_sem=send_sems.at[3 * a + k], recv_sem=recv_sems.at[3 * a + k],
                device_id=(x, y, 1 - c), device_id_type=MESH_ID))
    return out


def _handoff_start(layer, gathered, *, name):
    n = len(gathered)

    def issue(bufs, send_sems, recv_sems):
        @pl.when(lax.axis_index("c") == layer)
        def _():
            for cp in _handoff_copies(bufs, n, send_sems, recv_sems):
                cp.start()

    return _split_start(list(gathered), (3 * n,), issue, name=name)


def _handoff_wait(layer, started, after, *, name):
    n = len(started[2])

    def settle(bufs, send_sems, recv_sems):
        c = lax.axis_index("c")

        @pl.when(c == layer)
        def _():
            for cp in _handoff_copies(bufs, n, send_sems, recv_sems):
                cp.wait_send()

        @pl.when(c != layer)
        def _():
            for cp in _handoff_copies(bufs, n, send_sems, recv_sems):
                cp.wait_recv()

    return _split_wait(started, after, settle, name=name)


F32_SUBLANES = 8
BF16_SUBLANES = 16


def _row_tile(a, sublanes=BF16_SUBLANES, max_rows=704):
    best = None
    for cand in range(sublanes, min(a, max_rows) + 1, sublanes):
        if a % cand == 0:
            best = cand
    assert best is not None, a
    return best


def _pair_sum(mine, sib, *, name):
    nchip, a, b = sib.shape
    ta = _row_tile(a)

    def body(m_ref, s_ref, o_ref):
        o_ref[...] = (m_ref[...].astype(F32) + s_ref[...].astype(F32)).astype(BF16)

    blk = pl.BlockSpec((1, ta, b), lambda j, r: (j, r, 0))
    return pl.pallas_call(
        body, name=name,
        out_shape=jax.ShapeDtypeStruct((nchip, a, b), BF16),
        grid=(nchip, a // ta),
        in_specs=[blk, blk], out_specs=blk,
        compiler_params=_cparams(("parallel", "parallel")),
    )(mine, sib)


def _total_sum(own, recv, *, name):
    a, b = own.shape
    ta = _row_tile(a)

    def body(o_ref, r_ref, t_ref):
        acc = o_ref[...].astype(F32)
        for k in range(3):
            acc = acc + r_ref[k].astype(F32)
        t_ref[...] = acc

    blk = pl.BlockSpec((ta, b), lambda r: (r, 0))
    return pl.pallas_call(
        body, name=name,
        out_shape=jax.ShapeDtypeStruct((a, b), F32),
        grid=(a // ta,),
        in_specs=[blk, pl.BlockSpec((3, ta, b), lambda r: (0, r, 0))], out_specs=blk,
        compiler_params=_cparams(("parallel",)),
    )(own, recv)


def _all_reduce_small(packed):
    rows, width = packed.shape

    def body(x_ref, out_ref, gath, send_sems, recv_sems, local_sem):
        x, y, c, chips = _position()
        me, sibling = (x, y, c), (x, y, 1 - c)

        def slot(px, py, pc):
            return gath.at[4 * px + 2 * py + pc]

        def copy(k, block, to, src=None):
            return pltpu.make_async_remote_copy(
                src_ref=slot(*block) if src is None else src, dst_ref=slot(*block),
                send_sem=send_sems.at[k], recv_sem=recv_sems.at[k], device_id=to, device_id_type=MESH_ID)

        mine = pltpu.make_async_copy(x_ref, slot(*me), local_sem)
        mine.start()
        first = [copy(0, me, sibling, src=x_ref)]
        first += [copy(1 + j, me, (*chip, c), src=x_ref) for j, chip in enumerate(chips)]
        for cp in first:
            cp.start()
        passed = [copy(4 + j, (*chip, c), sibling) for j, chip in enumerate(chips)]
        for j, chip in enumerate(chips):
            copy(1 + j, (*chip, c), me).wait_recv()
            passed[j].start()
        copy(0, sibling, me).wait_recv()
        for j, chip in enumerate(chips):
            copy(4 + j, (*chip, 1 - c), me).wait_recv()
        for cp in first + passed:
            cp.wait_send()
        mine.wait()
        acc = gath[0]
        for d in range(1, N_DEV):
            acc = acc + gath[d]
        out_ref[...] = acc

    return pl.pallas_call(
        body, name="all_reduce_small",
        out_shape=jax.ShapeDtypeStruct((rows, width), F32),
        in_specs=[pl.BlockSpec(memory_space=pltpu.VMEM)],
        out_specs=pl.BlockSpec(memory_space=pltpu.VMEM),
        scratch_shapes=[pltpu.VMEM((N_DEV, rows, width), F32), pltpu.SemaphoreType.DMA((7,)),
                        pltpu.SemaphoreType.DMA((7,)), pltpu.SemaphoreType.DMA],
    )(packed)


def _mixer_forward_branches(x0, h, w, s, dep=None, before_conv=None):
    proj = _matmul(h, w["w_in_t"], mode="nt", out_dtype=BF16, tm=1024, tn=1664, tk=1024, name="proj_fwd",
                   dep=dep)
    oa, states = _gla_fwd(proj, w["w_fg2"], w["b_fg"], w["gla_norm_g"], s=s)
    conv_wb = w["conv_wb"]
    if before_conv is not None:
        conv_wb = conv_wb + before_conv(oa)[0, 0]
    cb_in = _conv_fwd(proj, conv_wb, s=s)
    return dict(x0=x0, h=h, proj=proj, oa=oa, states=states, cb_in=cb_in)


def _mixer_forward_out(sv, w, s, dep=None):
    ya = _matmul(sv["oa"], w["w_oa"], mode="nn", out_dtype=BF16, tm=1024, tn=1024, tk=1024, name="ya_fwd",
                 dep=dep)
    yb = _matmul(sv["cb_in"], w["w_ob"], mode="nn", out_dtype=BF16, tm=1024, tn=1024, tk=1024, name="yb_fwd")
    mix = _mix_fwd(sv["proj"], ya, yb, s=s)
    x1 = _matmul(mix, w["w_o"], mode="nn", out_dtype=F32, tm=1024, tn=1024, tk=1024, name="wo_fwd",
                 add=sv["x0"])
    return x1, dict(sv, ya=ya, yb=yb, mix=mix)


def _ffn_forward_hidden(x1, w, s, after_gate=None):
    h2 = _rms_fwd(x1, w["norm2_g"], name="rms2_fwd")
    gt = _matmul(h2, w["w_gate_t"], mode="nt", out_dtype=BF16, tm=1024, tn=1408, tk=1024, name="ffn_gate_fwd")
    up = _matmul(h2, w["w_up_t"], mode="nt", out_dtype=BF16, tm=1024, tn=1408, tk=1024, name="ffn_up_fwd",
                 dep=None if after_gate is None else after_gate(gt))
    return dict(x1=x1, h2=h2, gt=gt, up=up, hid=_swiglu_fwd(gt, up, s=s))


def _ffn_forward_out(sv, w, dep=None):
    return _matmul(sv["hid"], w["w_ffn_down"], mode="nn", out_dtype=F32, tm=1024, tn=1024, tk=FFN_HIDDEN,
                   name="ffn_out_fwd", add=sv["x1"], dep=dep)


def _ffn_backward_grads(dx2b, w, sv, s):
    g = {}
    dhid = _matmul(dx2b, w["w_ffn_down"], mode="nt", out_dtype=BF16, tm=1024, tn=FFN_HIDDEN, tk=1024,
                   name="ffn_out_bwd")
    g["w_ffn_down"] = _matmul(sv["hid"], dx2b, mode="tn", out_dtype=BF16, tm=1408, tn=1024, tk=1024,
                              name="ffn_out_wgrad")
    dgt, dup = _swiglu_bwd(sv["gt"], sv["up"], dhid, s=s)
    g["w_gate_t"] = _matmul(dgt, sv["h2"], mode="tn", out_dtype=BF16, tm=1408, tn=1024, tk=1024,
                            name="ffn_gate_wgrad")
    g["w_up_t"] = _matmul(dup, sv["h2"], mode="tn", out_dtype=BF16, tm=1408, tn=1024, tk=1024,
                          name="ffn_up_wgrad")
    return g, dgt, dup, dhid


def _ffn_backward_input(dgt, dup, dx2, w, sv, dep=None):
    dh2 = _matmul(dgt, w["w_gate_t"], mode="nn", out_dtype=BF16, tm=1024, tn=1024, tk=FFN_HIDDEN,
                  name="ffn_gate_bwd", dep=dep)
    dh2 = _matmul(dup, w["w_up_t"], mode="nn", out_dtype=BF16, tm=1024, tn=1024, tk=FFN_HIDDEN,
                  name="ffn_up_bwd", add=dh2)
    return _rms_bwd(sv["x1"], w["norm2_g"], dh2, dx2, name="rms2_bwd")


def _mixer_backward_branches(dx1b, w, sv, s, dep=None):
    g = {}
    dmix = _matmul(dx1b, w["w_o"], mode="nt", out_dtype=BF16, tm=1024, tn=1024, tk=1024, name="wo_bwd",
                   dep=dep)
    g["w_o"] = _matmul(sv["mix"], dx1b, mode="tn", out_dtype=BF16, tm=1024, tn=1024, tk=2048, name="wo_wgrad")
    dya, dyb, dproj = _mix_bwd(sv["proj"], sv["ya"], sv["yb"], dmix, s=s)
    dcb = _matmul(dyb, w["w_ob"], mode="nt", out_dtype=BF16, tm=1024, tn=1024, tk=1024, name="yb_bwd")
    g["w_ob"] = _matmul(sv["cb_in"], dyb, mode="tn", out_dtype=BF16, tm=1024, tn=1024, tk=2048, name="yb_wgrad")
    doa = _matmul(dya, w["w_oa"], mode="nt", out_dtype=BF16, tm=1024, tn=1024, tk=1024, name="ya_bwd")
    g["w_oa"] = _matmul(sv["oa"], dya, mode="tn", out_dtype=BF16, tm=1024, tn=1024, tk=2048, name="ya_wgrad")
    dproj, g["conv_wb"] = _conv_bwd(sv["proj"], dcb, w["conv_wb"], dproj, s=s)
    dproj, dfz, g["gla_norm_g"], g["w_fg2"], g["b_fg"] = _gla_bwd(
        sv["proj"], doa, sv["states"], w["w_fg2"], w["b_fg"], w["gla_norm_g"], dproj, s=s)
    return g, _put_fz(dproj, dfz, s=s)


def _proj_wgrad(dproj, sv, dep=None):
    return _matmul(dproj, sv["h"], mode="tn", out_dtype=BF16, tm=1664, tn=1024, tk=1024, name="proj_wgrad",
                   dep=dep)


def _proj_bwd(dproj, w, dep=None):
    return _matmul(dproj, w["w_in_t"], mode="nn", out_dtype=BF16, tm=1024, tn=1024, tk=1664, name="proj_bwd",
                   dep=dep)


def _cols_from_chips(t):
    return jnp.transpose(t, (1, 0, 2)).reshape(t.shape[1], -1)


W_IN_ROWS = IN_WIDTH // N_CHIPS
W_IN_ROWS_PAD = -(-W_IN_ROWS // BF16_SUBLANES) * BF16_SUBLANES


def _w_in_t_shard(t):
    return jnp.pad(jnp.transpose(t, (0, 2, 1)), ((0, 0), (0, W_IN_ROWS_PAD - W_IN_ROWS), (0, 0)))


def _w_in_t_unshard(t):
    return jnp.transpose(t[:, :W_IN_ROWS], (0, 2, 1))


def _w_in_row_segments():
    runs = [(0, 0, FZ_ORIG), (FZ_ORIG, FZ_ORIG + GATE_RANK, COL_FZ - FZ_ORIG), (COL_FZ, FZ_ORIG, GATE_RANK)]
    out = []
    for kernel_row, ref_row, length in runs:
        while length:
            chip, local = divmod(ref_row, W_IN_ROWS)
            n = min(length, W_IN_ROWS - local)
            out.append((kernel_row, chip * W_IN_ROWS_PAD + local, n))
            kernel_row, ref_row, length = kernel_row + n, ref_row + n, length - n
    return out


def _permute_rows(src, n_out, segments, *, name):
    n_src, width = src.shape
    block = _row_tile(n_out)
    window = block + BF16_SUBLANES
    assert n_src >= window and src.dtype == BF16

    def body(src_ref, out_ref):
        rows = lax.broadcasted_iota(jnp.int32, (block, window), 0)
        cols = lax.broadcasted_iota(jnp.int32, (block, window), 1)
        for b0 in range(0, n_out, block):
            acc = None
            for o, s, n in segments:
                lo, hi = max(o, b0), min(o + n, b0 + block)
                if lo >= hi:
                    continue
                first = s + (lo - o)
                base = max(0, min(first // BF16_SUBLANES * BF16_SUBLANES, n_src - window))
                shift = (first - base) - (lo - b0)
                pick = (cols == rows + shift) & (rows >= lo - b0) & (rows < hi - b0)
                part = jnp.dot(jnp.where(pick, 1.0, 0.0).astype(BF16), src_ref[base:base + window, :],
                               preferred_element_type=F32)
                acc = part if acc is None else acc + part
            out_ref[b0:b0 + block, :] = (jnp.zeros((block, width), F32) if acc is None else acc).astype(BF16)

    return pl.pallas_call(
        body, name=name,
        out_shape=jax.ShapeDtypeStruct((n_out, width), BF16),
        in_specs=[pl.BlockSpec(memory_space=pltpu.VMEM)],
        out_specs=pl.BlockSpec(memory_space=pltpu.VMEM),
        compiler_params=_cparams(),
    )(src)


def _w_in_t_to_kernel(t):
    return _permute_rows(t.reshape(N_CHIPS * W_IN_ROWS_PAD, D_MODEL), PROJ_W, _w_in_row_segments(),
                         name="w_in_to_kernel_rows")


def _w_in_t_from_kernel(g):
    segments = [(slab, kernel_row, n) for kernel_row, slab, n in _w_in_row_segments()]
    out = _permute_rows(g, N_CHIPS * W_IN_ROWS_PAD, segments, name="w_in_to_chip_rows")
    return out.reshape(N_CHIPS, W_IN_ROWS_PAD, D_MODEL)


def kernel(x, norm1_g, w_in, w_fg2, b_fg, gla_norm_g, w_oa, conv_w, conv_b, w_ob, w_o, norm2_g, w_ffn_gate, w_ffn_up, w_ffn_down, final_g, loss_target, m_norm1_g, m_w_in, m_w_fg2, m_b_fg, m_gla_norm_g, m_w_oa, m_conv_w, m_conv_b, m_w_ob, m_w_o, m_norm2_g, m_w_ffn_gate, m_w_ffn_up, m_w_ffn_down, m_final_g, v_norm1_g, v_w_in, v_w_fg2, v_b_fg, v_gla_norm_g, v_w_oa, v_conv_w, v_conv_b, v_w_ob, v_w_o, v_norm2_g, v_w_ffn_gate, v_w_ffn_up, v_w_ffn_down, v_final_g):
    cx_ = lax.axis_index("x")
    cy_ = lax.axis_index("y")
    cc_ = lax.axis_index("c")
    me = 2 * cx_ + cy_
    on_core0 = jnp.where(cc_ == 0, 1.0, 0.0).astype(F32).reshape(1, 1)

    def swap(t):
        return jnp.swapaxes(t, 1, 2)

    big_names = ["w_in", "w_oa", "w_ob", "w_o", "w_ffn_gate", "w_ffn_up", "w_ffn_down"]
    views = dict(
        w_in=tuple(_w_in_t_shard(t) for t in (w_in, m_w_in, v_w_in)),
        w_oa=(w_oa, m_w_oa, v_w_oa), w_ob=(w_ob, m_w_ob, v_w_ob), w_o=(w_o, m_w_o, v_w_o),
        w_ffn_gate=tuple(swap(t) for t in (w_ffn_gate, m_w_ffn_gate, v_w_ffn_gate)),
        w_ffn_up=tuple(swap(t) for t in (w_ffn_up, m_w_ffn_up, v_w_ffn_up)),
        w_ffn_down=(w_ffn_down, m_w_ffn_down, v_w_ffn_down))
    from_view = dict(w_in=_w_in_t_unshard, w_ffn_gate=swap, w_ffn_up=swap)

    s = x.shape[1]

    proj_names = ["w_in"]
    first_names = ["w_in", "w_fg2", "conv_w"]
    rest_names = ["w_oa", "w_ob", "w_o", "w_ffn_gate", "w_ffn_up", "w_ffn_down"]
    mixer_names = ["w_oa", "w_ob", "w_o"]
    ffn_names = ["w_ffn_gate", "w_ffn_up", "w_ffn_down"]
    weight_key = dict(w_in="w_in_t", w_ffn_gate="w_gate_t", w_ffn_up="w_up_t")
    full_shape = dict(w_oa=(VAL_WIDTH, D_MODEL), w_ob=(CONV_CH, D_MODEL), w_o=(D_MODEL, D_MODEL),
                      w_ffn_gate=(FFN_HIDDEN, D_MODEL), w_ffn_up=(FFN_HIDDEN, D_MODEL),
                      w_ffn_down=(FFN_HIDDEN, D_MODEL))

    conv_w_p = jnp.pad(conv_w, ((0, 0), (0, 8 - conv_w.shape[1]), (0, 0)))
    small_shards = dict(w_fg2=w_fg2, conv_w=conv_w_p)

    def shards_of(l, names):
        return [small_shards[n][l] if n in small_shards else views[n][0][l].astype(BF16) for n in names]

    def small_weights(l):
        return dict(norm1_g=norm1_g[l], norm2_g=norm2_g[l], b_fg=b_fg[l].reshape(1, KEY_WIDTH),
                    gla_norm_g=gla_norm_g[l].reshape(1, HEAD_V))

    def full_weights(l, names, gathered, shards):
        out = {}
        for n, g, t in zip(names, gathered, shards):
            g = lax.dynamic_update_index_in_dim(g, t[None], me, axis=0)
            if n == "w_in":
                out["w_in_t"] = _w_in_t_to_kernel(g)
            elif n == "w_fg2":
                out[n] = jnp.pad(_cols_from_chips(g), ((0, FZ_PAD - GATE_RANK), (0, 0))).astype(BF16)
            elif n == "conv_w":
                out["conv_wb"] = (jnp.pad(_cols_from_chips(g)[:3], ((0, 5), (0, 0)))
                                  + jnp.pad(conv_b[l].reshape(1, CONV_CH), ((3, 4), (0, 0))))
            else:
                out[weight_key.get(n, n)] = g.reshape(full_shape[n])
        return out

    def per_chip(g, names):
        out = []
        for n in names:
            t = g[weight_key.get(n, n)]
            out.append(_w_in_t_from_kernel(t) if n == "w_in"
                       else t.reshape(N_CHIPS, t.shape[0] // N_CHIPS, t.shape[1]))
        return out

    def gather(l, names, tag, after=None):
        shards = shards_of(l, names)
        return shards, _gather_start(l, shards, name=f"gather_{tag}_start", after=after)

    def gathered_to_sibling(l, started, after, tag):
        return _handoff_start(l, _gather_wait(l, started, after, name=f"gather_{tag}_wait"),
                              name=f"handoff_{tag}_start")

    def feed(l, g, names, tag):
        return _sibling_push_start(l, False, per_chip(g, names), name=f"feed_{tag}_start")

    def pair_and_exchange(l, fed, after, names, tag):
        mine, sib = _sibling_push_wait(l, False, fed, after, name=f"feed_{tag}_wait")
        parts = [_pair_sum(a, b, name=f"pair_sum_{tag}_{n}") for n, a, b in zip(names, mine, sib)]
        return _chip_exchange_start(l, parts, name=f"exchange_{tag}_start")

    def total_and_share(l, swapped, after, names, tag):
        parts, recvs = _chip_exchange_wait(l, swapped, after, name=f"exchange_{tag}_wait")
        owns = [lax.dynamic_index_in_dim(p, me, 0, keepdims=False) for p in parts]
        totals = [_total_sum(o, rc, name=f"total_{tag}_{n}") for n, o, rc in zip(names, owns, recvs)]
        return _sibling_push_start(l, True, totals, name=f"share_{tag}_start")

    def shared(l, sharing, after, names, tag):
        totals, others = _sibling_push_wait(l, True, sharing, after, name=f"share_{tag}_wait")
        return {n: (t, o) for n, t, o in zip(names, totals, others)}

    def branches_under_gather(l, xin, h, w, tag, shards_r, started_r):
        box = {}

        def before_conv(oa):
            box["handoff"] = gathered_to_sibling(l, started_r, oa, tag)
            return box["handoff"][3]

        sv = _mixer_forward_branches(xin, h, w, s, dep=started_r[3], before_conv=before_conv)
        rest = _handoff_wait(l, box["handoff"], sv["cb_in"], name=f"handoff_{tag}_wait")
        w.update(full_weights(l, rest_names, rest, shards_r))
        return sv

    shards_p, started = gather(0, first_names, "p0")
    h0 = _rms_fwd(x[0], norm1_g[0], name="rms1_fwd")
    later_shards = shards_of(0, rest_names) + shards_of(1, first_names) + shards_of(1, rest_names)
    started = gathered_to_sibling(0, started, (h0, views["w_in"][1], *later_shards), "p0")
    w0 = small_weights(0)
    w0.update(full_weights(0, first_names, _handoff_wait(0, started, views["w_in"][2], name="handoff_p0_wait"),
                           shards_p))
    sv0 = branches_under_gather(0, x[0], h0, w0, "r0", *gather(0, rest_names, "r0", after=w0["w_in_t"]))

    shards_p1, started = gather(1, first_names, "p1", after=w0["w_o"])
    x1, sv0m = _mixer_forward_out(sv0, w0, s, dep=started[3])
    box1 = {}

    def after_gate(gt):
        box1["handoff"] = gathered_to_sibling(1, started, gt, "p1")
        box1["rest"] = gather(1, rest_names, "r1", after=box1["handoff"][2][0])
        return box1["handoff"][3] + box1["rest"][1][3]

    sv0f = _ffn_forward_hidden(x1, w0, s, after_gate=after_gate)
    x2 = _ffn_forward_out(sv0f, w0)
    w1 = small_weights(1)
    w1.update(full_weights(1, first_names, _handoff_wait(1, box1["handoff"], x2, name="handoff_p1_wait"),
                           shards_p1))

    h1 = _rms_fwd(x2, norm1_g[1], name="rms1_fwd")
    x3, sv1m = _mixer_forward_out(branches_under_gather(1, x2, h1, w1, "r1", *box1["rest"]), w1, s)
    sv1f = _ffn_forward_hidden(x3, w1, s)
    x4 = _ffn_forward_out(sv1f, w1)
    loss_local, dx, dxb, dgf = _loss_head(x4, final_g, loss_target[0])
    loss = lax.psum(loss_local[0, 0], ("x", "y", "c"))

    g1, dgt, dup, _ = _ffn_backward_grads(dxb, w1, sv1f, s)
    dx_mid, dxb_mid, g1["norm2_g"] = _ffn_backward_input(dgt, dup, dx, w1, sv1f)
    gm, dproj = _mixer_backward_branches(dxb_mid, w1, sv1m, s)
    g1.update(gm)
    g1["w_in_t"] = _proj_wgrad(dproj, sv1m)
    fed_1 = feed(1, g1, big_names, "l1")
    dh = _proj_bwd(dproj, w1, dep=fed_1[3])
    dx, dxb, g1["norm1_g"] = _rms_bwd(sv1m["x0"], norm1_g[1], dh, dx_mid, name="rms1_bwd")

    g0, dgt, dup, dhid = _ffn_backward_grads(dxb, w0, sv0f, s)
    swap_1 = pair_and_exchange(1, fed_1, dhid, big_names, "l1")
    fed_f = feed(0, g0, ffn_names, "f0")
    dx_mid, dxb_mid, g0["norm2_g"] = _ffn_backward_input(dgt, dup, dx, w0, sv0f, dep=swap_1[3] + fed_f[3])
    swap_f = pair_and_exchange(0, fed_f, dx_mid, ffn_names, "f0")
    gm, dproj = _mixer_backward_branches(dxb_mid, w0, sv0m, s, dep=swap_f[3])
    g0.update(gm)
    share_1 = total_and_share(1, swap_1, dproj, big_names, "l1")
    share_f = total_and_share(0, swap_f, dproj, ffn_names, "f0")
    fed_m = feed(0, g0, mixer_names, "m0")
    g0["w_in_t"] = _proj_wgrad(dproj, sv0m, dep=share_1[3] + share_f[3] + fed_m[3])
    fed_p = feed(0, g0, proj_names, "p0")
    swap_m = pair_and_exchange(0, fed_m, g0["w_in_t"], mixer_names, "m0")
    dh = _proj_bwd(dproj, w0, dep=fed_p[3] + swap_m[3])
    reduced1 = shared(1, share_1, dh, big_names, "l1")
    reduced0 = shared(0, share_f, dh, ffn_names, "f0")
    grad_x, _, g0["norm1_g"] = _rms_bwd(sv0m["x0"], norm1_g[0], dh, dx_mid, name="rms1_bwd")

    swap_p = pair_and_exchange(0, fed_p, grad_x, proj_names, "p0")
    share_m = total_and_share(0, swap_m, swap_p[3], mixer_names, "m0")
    reduced0.update(shared(0, share_m, share_m[3], mixer_names, "m0"))
    grads = [g0, g1]

    def small_rows(t):
        return t.reshape(-1, D_MODEL)

    def tile_rows(t):
        return jnp.pad(t, ((0, -t.shape[0] % F32_SUBLANES), (0, 0)))

    g0, g1 = grads
    pieces = [
        jnp.concatenate([g0["norm1_g"], g1["norm1_g"]], axis=0),
        jnp.concatenate([g0["norm2_g"], g1["norm2_g"]], axis=0),
        dgf,
        small_rows(jnp.concatenate([g0["b_fg"], g1["b_fg"]], axis=1)),
        small_rows(jnp.concatenate([g0["gla_norm_g"], g1["gla_norm_g"],
                                    jnp.zeros((1, D_MODEL - 2 * HEAD_V), F32)], axis=1)),
        jnp.concatenate([g0["conv_wb"][3:4], g1["conv_wb"][3:4]], axis=0),
        jnp.concatenate([g0["conv_wb"][:3], g1["conv_wb"][:3]], axis=0),
        small_rows(jnp.stack([g0["w_fg2"][:GATE_RANK], g1["w_fg2"][:GATE_RANK]])),
    ]
    small = _all_reduce_small(jnp.concatenate([tile_rows(p) for p in pieces], axis=0))
    sg = dict(
        norm1_g=small[0:2], norm2_g=small[8:10], final_g=small[16],
        b_fg=small[24].reshape(DEPTH, KEY_WIDTH), gla_norm_g=small[32, :DEPTH * HEAD_V].reshape(DEPTH, HEAD_V),
        conv_b=small[40:42],
        conv_w=lax.dynamic_slice_in_dim(small[48:54].reshape(DEPTH, 3, CONV_CH), me * (CONV_CH // N_CHIPS),
                                        CONV_CH // N_CHIPS, axis=2),
        w_fg2=lax.dynamic_slice_in_dim(small[56:72].reshape(DEPTH, GATE_RANK, KEY_WIDTH),
                                       me * (KEY_WIDTH // N_CHIPS), KEY_WIDTH // N_CHIPS, axis=2),
    )

    small_params = dict(norm1_g=(norm1_g, m_norm1_g, v_norm1_g), w_fg2=(w_fg2, m_w_fg2, v_w_fg2),
                        b_fg=(b_fg, m_b_fg, v_b_fg), gla_norm_g=(gla_norm_g, m_gla_norm_g, v_gla_norm_g),
                        conv_w=(conv_w, m_conv_w, v_conv_w), conv_b=(conv_b, m_conv_b, v_conv_b),
                        norm2_g=(norm2_g, m_norm2_g, v_norm2_g), final_g=(final_g, m_final_g, v_final_g))
    order = ["norm1_g", "w_in", "w_fg2", "b_fg", "gla_norm_g", "w_oa", "conv_w", "conv_b", "w_ob", "w_o",
             "norm2_g", "w_ffn_gate", "w_ffn_up", "w_ffn_down", "final_g"]
    results = {}

    def update_large(n):
        w_, m_, v_ = views[n]
        outs = _adamw_layers(w_, (reduced0[n][0], reduced1[n][0]), (reduced0[n][1], reduced1[n][1]), on_core0,
                             m_, v_, name="adamw_" + n)
        back = from_view.get(n)
        results[n] = tuple(back(r) for r in outs) if back else outs
        return outs[1]

    for n, (w_, m_, v_) in small_params.items():
        results[n] = (sg[n],) + _adamw(w_, sg[n], m_, v_, name="adamw_" + n)
    updated = tuple(update_large(n) for n in rest_names)
    share_p = total_and_share(0, swap_p, updated, proj_names, "p0")
    reduced0.update(shared(0, share_p, share_p[3], proj_names, "p0"))
    update_large("w_in")
    return (loss, grad_x[None], *[results[n][0] for n in order], *[results[n][1] for n in order],
            *[results[n][2] for n in order], *[results[n][3] for n in order])
```

```python
import functools

import jax
import jax.numpy as jnp
from jax import lax
from jax.experimental import pallas as pl
from jax.experimental.pallas import tpu as pltpu

F32 = jnp.float32
BF16 = jnp.bfloat16

D_MODEL = 1024
DEPTH = 2
CHUNK = 64
GLA_HEADS = 4
KEY_WIDTH = 512
VAL_WIDTH = 1024
HEAD_K = 128
HEAD_V = 256
GATE_RANK = 16
GATE_TAU = 16.0
CONV_CH = 1024
FFN_HIDDEN = 2816
IN_WIDTH = 8208
NORM_EPS = 1e-6
Q_SCALE = HEAD_K ** -0.5
ADAM_LR = 0.001
ADAM_B1 = 0.9
ADAM_B2 = 0.999
ADAM_EPS = 1e-08
ADAM_WD = 0.01
ADAM_STEP = 10

N_CHIPS = 4
N_DEV = 8

LANE = 128
FZ_PAD = LANE
PROJ_W = 8192 + FZ_PAD
COL_Q, COL_K, COL_V, COL_R, COL_GBI, COL_GCI, COL_CX, COL_GA, COL_GB, COL_FZ = (
    0, 512, 1024, 2048, 3072, 4096, 5120, 6144, 7168, 8192)
FZ_ORIG = 3072

VMEM_LIMIT = 52 * 1024 * 1024
HALO = 16


def _cparams(sem=None):
    return pltpu.CompilerParams(dimension_semantics=sem, vmem_limit_bytes=VMEM_LIMIT)


def _sigmoid(x):
    return jax.nn.sigmoid(x)


def _log_sigmoid(x):
    return jnp.minimum(x, 0.0) - jnp.log1p(jnp.exp(-jnp.abs(x)))


_DOT_DIMS = {
    "nn": (((1,), (0,)), ((), ())),
    "nt": (((1,), (1,)), ((), ())),
    "tn": (((0,), (0,)), ((), ())),
}


def _matmul(a, b, *, mode, out_dtype, tm, tn, tk, name, add=None, dep=None):
    if mode == "nn":
        (m, k), n = a.shape, b.shape[1]
    elif mode == "nt":
        (m, k), n = a.shape, b.shape[0]
    else:
        (k, m), n = a.shape, b.shape[1]
    tm, tn, tk = min(tm, m), min(tn, n), min(tk, k)
    assert m % tm == 0 and n % tn == 0 and k % tk == 0, (name, m, n, k, tm, tn, tk)
    nk = k // tk
    has_add = add is not None
    has_dep = dep is not None

    def body(*refs):
        if has_dep:
            refs = refs[1:]
        if has_add:
            a_ref, b_ref, add_ref, o_ref = refs[:4]
            scratch = refs[4:]
        else:
            a_ref, b_ref, o_ref = refs[:3]
            add_ref = None
            scratch = refs[3:]
        part = lax.dot_general(a_ref[...], b_ref[...], _DOT_DIMS[mode], preferred_element_type=F32)

        def finish(acc):
            if add_ref is not None:
                acc = acc + add_ref[...].astype(F32)
            o_ref[...] = acc.astype(o_ref.dtype)

        if nk == 1:
            finish(part)
        else:
            acc_ref = scratch[0]
            kk = pl.program_id(2)

            @pl.when(kk == 0)
            def _():
                acc_ref[...] = part

            @pl.when(kk > 0)
            def _():
                acc_ref[...] += part

            @pl.when(kk == nk - 1)
            def _():
                finish(acc_ref[...])

    if mode == "nn":
        a_spec = pl.BlockSpec((tm, tk), lambda i, j, kk: (i, kk))
        b_spec = pl.BlockSpec((tk, tn), lambda i, j, kk: (kk, j))
    elif mode == "nt":
        a_spec = pl.BlockSpec((tm, tk), lambda i, j, kk: (i, kk))
        b_spec = pl.BlockSpec((tn, tk), lambda i, j, kk: (j, kk))
    else:
        a_spec = pl.BlockSpec((tk, tm), lambda i, j, kk: (kk, i))
        b_spec = pl.BlockSpec((tk, tn), lambda i, j, kk: (kk, j))
    o_spec = pl.BlockSpec((tm, tn), lambda i, j, kk: (i, j))
    in_specs = [a_spec, b_spec] + ([o_spec] if has_add else [])
    operands = (a, b) + ((add,) if has_add else ())
    if has_dep:
        in_specs = [pl.BlockSpec(dep.shape, lambda i, j, kk: (0, 0))] + in_specs
        operands = (dep,) + operands
    return pl.pallas_call(
        body,
        name=name,
        out_shape=jax.ShapeDtypeStruct((m, n), out_dtype),
        grid=(m // tm, n // tn, nk),
        in_specs=in_specs,
        out_specs=o_spec,
        scratch_shapes=[pltpu.VMEM((tm, tn), F32)] if nk > 1 else [],
        compiler_params=_cparams(("parallel", "parallel", "arbitrary")),
    )(*operands)


def _rms_fwd(x, g, *, name):
    s, d = x.shape
    tr = min(512, s)

    def body(x_ref, g_ref, h_ref):
        xv = x_ref[...]
        rs = lax.rsqrt(jnp.mean(xv * xv, axis=-1, keepdims=True) + NORM_EPS)
        h_ref[...] = (xv * rs * g_ref[...]).astype(BF16)

    return pl.pallas_call(
        body, name=name,
        out_shape=jax.ShapeDtypeStruct((s, d), BF16),
        grid=(s // tr,),
        in_specs=[pl.BlockSpec((tr, d), lambda i: (i, 0)), pl.BlockSpec((1, d), lambda i: (0, 0))],
        out_specs=pl.BlockSpec((tr, d), lambda i: (i, 0)),
        compiler_params=_cparams(("parallel",)),
    )(x, g.reshape(1, d))


def _rms_bwd(x, g, dh, dres, *, name):
    s, d = x.shape
    tr = min(512, s)

    def body(x_ref, g_ref, dh_ref, dres_ref, dx_ref, dxb_ref, dg_ref):
        i = pl.program_id(0)
        xv = x_ref[...]
        rs = lax.rsqrt(jnp.mean(xv * xv, axis=-1, keepdims=True) + NORM_EPS)
        n = xv * rs
        dhv = dh_ref[...].astype(F32)
        dn = dhv * g_ref[...]
        dx = dres_ref[...] + rs * (dn - n * jnp.mean(dn * n, axis=-1, keepdims=True))
        dx_ref[...] = dx
        dxb_ref[...] = dx.astype(BF16)
        part = jnp.sum(dhv * n, axis=0, keepdims=True)

        @pl.when(i == 0)
        def _():
            dg_ref[...] = part

        @pl.when(i > 0)
        def _():
            dg_ref[...] += part

    row = pl.BlockSpec((tr, d), lambda i: (i, 0))
    vec = pl.BlockSpec((1, d), lambda i: (0, 0))
    return pl.pallas_call(
        body, name=name,
        out_shape=(jax.ShapeDtypeStruct((s, d), F32), jax.ShapeDtypeStruct((s, d), BF16),
                   jax.ShapeDtypeStruct((1, d), F32)),
        grid=(s // tr,),
        in_specs=[row, vec, row, row],
        out_specs=(row, row, vec),
        compiler_params=_cparams(("arbitrary",)),
    )(x, g.reshape(1, d), dh, dres)


def _loss_head(x, g, target):
    s, d = x.shape
    tr = min(512, s)

    def body(x_ref, g_ref, t_ref, loss_ref, dx_ref, dxb_ref, dg_ref):
        i = pl.program_id(0)
        xv = x_ref[...]
        rs = lax.rsqrt(jnp.mean(xv * xv, axis=-1, keepdims=True) + NORM_EPS)
        n = xv * rs
        gv = g_ref[...]
        err = n * gv - t_ref[...]
        row_loss = jnp.mean(err * err, axis=-1, keepdims=True)
        loss_part = 0.5 * jnp.sum(row_loss, axis=0, keepdims=True)
        dy = err * (1.0 / d)
        dn = dy * gv
        dx = rs * (dn - n * jnp.mean(dn * n, axis=-1, keepdims=True))
        dx_ref[...] = dx
        dxb_ref[...] = dx.astype(BF16)
        dg_part = jnp.sum(dy * n, axis=0, keepdims=True)

        @pl.when(i == 0)
        def _():
            loss_ref[...] = loss_part
            dg_ref[...] = dg_part

        @pl.when(i > 0)
        def _():
            loss_ref[...] += loss_part
            dg_ref[...] += dg_part

    row = pl.BlockSpec((tr, d), lambda i: (i, 0))
    vec = pl.BlockSpec((1, d), lambda i: (0, 0))
    one = pl.BlockSpec((1, 1), lambda i: (0, 0))
    return pl.pallas_call(
        body, name="loss_head",
        out_shape=(jax.ShapeDtypeStruct((1, 1), F32), jax.ShapeDtypeStruct((s, d), F32),
                   jax.ShapeDtypeStruct((s, d), BF16), jax.ShapeDtypeStruct((1, d), F32)),
        grid=(s // tr,),
        in_specs=[row, vec, row],
        out_specs=(one, row, row, vec),
        compiler_params=_cparams(("arbitrary",)),
    )(x, g.reshape(1, d), target)


def _conv_taps(u_prev, u, w_ref, rows):
    ext = jnp.concatenate([u_prev, u], axis=0)
    u1 = pltpu.roll(ext, 1, 0)[HALO:HALO + rows]
    u2 = pltpu.roll(ext, 2, 0)[HALO:HALO + rows]
    conv = w_ref[0:1, :] * u2 + w_ref[1:2, :] * u1 + w_ref[2:3, :] * u + w_ref[3:4, :]
    return conv, u1, u2


def _conv_fwd(proj, conv_wb, *, s):
    tr = min(512, s)
    c = CONV_CH
    hb = tr // HALO

    def body(gbi_ref, gci_ref, cx_ref, gci_h_ref, cx_h_ref, w_ref, o_ref):
        i = pl.program_id(0)
        u = gci_ref[...].astype(F32) * cx_ref[...].astype(F32)
        u_prev = gci_h_ref[...].astype(F32) * cx_h_ref[...].astype(F32)
        u_prev = jnp.where(i == 0, 0.0, u_prev)
        conv, _, _ = _conv_taps(u_prev, u, w_ref, tr)
        o_ref[...] = (gbi_ref[...].astype(F32) * conv).astype(BF16)

    def seg(col):
        return pl.BlockSpec((tr, c), lambda i: (i, col // c))

    def halo(col):
        return pl.BlockSpec((HALO, c), lambda i: (jnp.maximum(i * hb - 1, 0), col // c))

    return pl.pallas_call(
        body, name="conv_fwd",
        out_shape=jax.ShapeDtypeStruct((s, c), BF16),
        grid=(s // tr,),
        in_specs=[seg(COL_GBI), seg(COL_GCI), seg(COL_CX), halo(COL_GCI), halo(COL_CX),
                  pl.BlockSpec((8, c), lambda i: (0, 0))],
        out_specs=pl.BlockSpec((tr, c), lambda i: (i, 0)),
        compiler_params=_cparams(("parallel",)),
    )(proj, proj, proj, proj, proj, conv_wb)


def _conv_bwd(proj, dcb, conv_wb, dproj, *, s):
    tr = min(512, s)
    c = CONV_CH
    hb = tr // HALO
    nb = s // tr

    def body(gbi_ref, gci_ref, cx_ref, dcb_ref, gci_h_ref, cx_h_ref, gbi_n_ref, dcb_n_ref, w_ref,
             dproj_in_ref, dp_ref, dw_ref):
        i = pl.program_id(0)
        gbi = gbi_ref[...].astype(F32)
        gci = gci_ref[...].astype(F32)
        cx = cx_ref[...].astype(F32)
        dcb_v = dcb_ref[...].astype(F32)
        u = gci * cx
        u_prev = jnp.where(i == 0, 0.0, gci_h_ref[...].astype(F32) * cx_h_ref[...].astype(F32))
        conv, u1, u2 = _conv_taps(u_prev, u, w_ref, tr)
        dconv = dcb_v * gbi
        dconv_next = jnp.where(i == nb - 1, 0.0, dcb_n_ref[...].astype(F32) * gbi_n_ref[...].astype(F32))
        ext = jnp.concatenate([dconv, dconv_next], axis=0)
        n_ext = tr + HALO
        d1 = pltpu.roll(ext, n_ext - 1, 0)[0:tr]
        d2 = pltpu.roll(ext, n_ext - 2, 0)[0:tr]
        du = w_ref[2:3, :] * dconv + w_ref[1:2, :] * d1 + w_ref[0:1, :] * d2
        dp_ref[:, 0:c] = (dcb_v * conv).astype(BF16)
        dp_ref[:, c:2 * c] = (du * cx).astype(BF16)
        dp_ref[:, 2 * c:3 * c] = (du * gci).astype(BF16)
        part = jnp.concatenate([
            jnp.sum(dconv * u2, axis=0, keepdims=True),
            jnp.sum(dconv * u1, axis=0, keepdims=True),
            jnp.sum(dconv * u, axis=0, keepdims=True),
            jnp.sum(dconv, axis=0, keepdims=True),
            jnp.zeros((4, c), F32)], axis=0)

        @pl.when(i == 0)
        def _():
            dw_ref[...] = part

        @pl.when(i > 0)
        def _():
            dw_ref[...] += part

    def seg(col):
        return pl.BlockSpec((tr, c), lambda i: (i, col // c))

    def halo_prev(col):
        return pl.BlockSpec((HALO, c), lambda i: (jnp.maximum(i * hb - 1, 0), col // c))

    def halo_next(col):
        return pl.BlockSpec((HALO, c), lambda i: (jnp.minimum((i + 1) * hb, nb * hb - 1), col // c))

    return pl.pallas_call(
        body, name="conv_bwd",
        out_shape=(jax.ShapeDtypeStruct((s, PROJ_W), BF16), jax.ShapeDtypeStruct((8, c), F32)),
        grid=(nb,),
        in_specs=[seg(COL_GBI), seg(COL_GCI), seg(COL_CX), pl.BlockSpec((tr, c), lambda i: (i, 0)),
                  halo_prev(COL_GCI), halo_prev(COL_CX), halo_next(COL_GBI),
                  pl.BlockSpec((HALO, c), lambda i: (jnp.minimum((i + 1) * hb, nb * hb - 1), 0)),
                  pl.BlockSpec((8, c), lambda i: (0, 0)), ANY],
        out_specs=(pl.BlockSpec((tr, 3 * c), lambda i: (i, COL_GBI // (3 * c))),
                   pl.BlockSpec((8, c), lambda i: (0, 0))),
        input_output_aliases={9: 0},
        compiler_params=_cparams(("arbitrary",)),
    )(proj, proj, proj, dcb, proj, proj, proj, dcb, conv_wb, dproj)


def _token_spec(dep):
    return ([], []) if dep is None else ([dep], [pl.BlockSpec(dep.shape, lambda i: (0, 0))])


def _mixer_out_fwd(oa, cb_in, proj, x0, w_oa, w_ob, w_o, *, s, dep=None):
    tr = min(512, s)
    d = D_MODEL
    n_dep = dep is not None

    def body(*refs):
        oa_ref, cb_ref, ga_ref, gb_ref, x0_ref, woa_ref, wob_ref, wo_ref, ya_ref, yb_ref, mix_ref, x1_ref = (
            refs[n_dep:])
        ya = jnp.dot(oa_ref[...], woa_ref[...], preferred_element_type=F32).astype(BF16)
        yb = jnp.dot(cb_ref[...], wob_ref[...], preferred_element_type=F32).astype(BF16)
        ya_ref[...] = ya
        yb_ref[...] = yb
        sa = _sigmoid(ga_ref[...].astype(F32))
        sb = _sigmoid(gb_ref[...].astype(F32))
        mix = (sa * ya.astype(F32) + sb * yb.astype(F32)).astype(BF16)
        mix_ref[...] = mix
        x1_ref[...] = x0_ref[...] + jnp.dot(mix, wo_ref[...], preferred_element_type=F32)

    row = pl.BlockSpec((tr, d), lambda i: (i, 0))
    full = pl.BlockSpec((d, d), lambda i: (0, 0))
    tok, tok_specs = _token_spec(dep)
    return pl.pallas_call(
        body, name="mixer_out_fwd",
        out_shape=(jax.ShapeDtypeStruct((s, d), BF16), jax.ShapeDtypeStruct((s, d), BF16),
                   jax.ShapeDtypeStruct((s, d), BF16), jax.ShapeDtypeStruct((s, d), F32)),
        grid=(s // tr,),
        in_specs=tok_specs + [row, row, pl.BlockSpec((tr, d), lambda i: (i, COL_GA // d)),
                              pl.BlockSpec((tr, d), lambda i: (i, COL_GB // d)), row, full, full, full],
        out_specs=(row, row, row, row),
        compiler_params=_cparams(("parallel",)),
    )(*tok, oa, cb_in, proj, proj, x0, w_oa, w_ob, w_o)


def _mixer_out_bwd(dx1b, proj, ya, yb, w_oa, w_ob, w_o, *, s, dep=None):
    tr = min(512, s)
    d = D_MODEL
    n_dep = dep is not None

    def body(*refs):
        (dx_ref, ga_ref, gb_ref, ya_ref, yb_ref, woa_ref, wob_ref, wo_ref,
         dya_ref, dyb_ref, dcb_ref, doa_ref, dg_ref) = refs[n_dep:]
        dm = lax.dot_general(dx_ref[...], wo_ref[...], _DOT_DIMS["nt"], preferred_element_type=F32)
        dm = dm.astype(BF16).astype(F32)
        sa = _sigmoid(ga_ref[...].astype(F32))
        sb = _sigmoid(gb_ref[...].astype(F32))
        dya = (dm * sa).astype(BF16)
        dyb = (dm * sb).astype(BF16)
        dya_ref[...] = dya
        dyb_ref[...] = dyb
        dg_ref[:, 0:d] = (dm * ya_ref[...].astype(F32) * sa * (1.0 - sa)).astype(BF16)
        dg_ref[:, d:2 * d] = (dm * yb_ref[...].astype(F32) * sb * (1.0 - sb)).astype(BF16)
        dcb_ref[...] = lax.dot_general(dyb, wob_ref[...], _DOT_DIMS["nt"],
                                       preferred_element_type=F32).astype(BF16)
        doa_ref[...] = lax.dot_general(dya, woa_ref[...], _DOT_DIMS["nt"],
                                       preferred_element_type=F32).astype(BF16)

    row = pl.BlockSpec((tr, d), lambda i: (i, 0))
    full = pl.BlockSpec((d, d), lambda i: (0, 0))
    tok, tok_specs = _token_spec(dep)
    return pl.pallas_call(
        body, name="mixer_out_bwd",
        out_shape=tuple(jax.ShapeDtypeStruct((s, d), BF16) for _ in range(4))
        + (jax.ShapeDtypeStruct((s, PROJ_W), BF16),),
        grid=(s // tr,),
        in_specs=tok_specs + [row, pl.BlockSpec((tr, d), lambda i: (i, COL_GA // d)),
                              pl.BlockSpec((tr, d), lambda i: (i, COL_GB // d)), row, row, full, full, full],
        out_specs=(row, row, row, row, pl.BlockSpec((tr, 2 * d), lambda i: (i, COL_GA // (2 * d)))),
        compiler_params=_cparams(("parallel",)),
    )(*tok, dx1b, proj, proj, ya, yb, w_oa, w_ob, w_o)


def _swiglu_fwd(gt, up, *, s):
    tr = min(256, s)
    f = FFN_HIDDEN

    def body(gt_ref, up_ref, o_ref):
        gv = gt_ref[...].astype(F32)
        o_ref[...] = (gv * _sigmoid(gv) * up_ref[...].astype(F32)).astype(BF16)

    row = pl.BlockSpec((tr, f), lambda i: (i, 0))
    return pl.pallas_call(
        body, name="swiglu_fwd",
        out_shape=jax.ShapeDtypeStruct((s, f), BF16),
        grid=(s // tr,),
        in_specs=[row, row], out_specs=row,
        compiler_params=_cparams(("parallel",)),
    )(gt, up)


def _swiglu_bwd(gt, up, dhid, *, s):
    tr = min(256, s)
    f = FFN_HIDDEN

    def body(gt_ref, up_ref, dh_ref, dgt_ref, dup_ref):
        gv = gt_ref[...].astype(F32)
        uv = up_ref[...].astype(F32)
        dh = dh_ref[...].astype(F32)
        sg = _sigmoid(gv)
        dgt_ref[...] = (dh * uv * sg * (1.0 + gv * (1.0 - sg))).astype(BF16)
        dup_ref[...] = (dh * gv * sg).astype(BF16)

    row = pl.BlockSpec((tr, f), lambda i: (i, 0))
    return pl.pallas_call(
        body, name="swiglu_bwd",
        out_shape=(jax.ShapeDtypeStruct((s, f), BF16), jax.ShapeDtypeStruct((s, f), BF16)),
        grid=(s // tr,),
        in_specs=[row, row, row], out_specs=(row, row),
        compiler_params=_cparams(("parallel",)),
    )(gt, up, dhid)


def _tri(strict):
    r = lax.broadcasted_iota(jnp.int32, (CHUNK, CHUNK), 0)
    c = lax.broadcasted_iota(jnp.int32, (CHUNK, CHUNK), 1)
    return jnp.where((c < r) if strict else (c <= r), 1.0, 0.0).astype(F32)


def _gla_gate_terms(la_c, tri):
    cum = jnp.dot(tri, la_c, precision=lax.Precision.HIGHEST, preferred_element_type=F32)
    cend = cum[CHUNK - 1:CHUNK, :]
    return jnp.exp(cend - cum), jnp.exp(cend)


def _hk(h):
    return slice(h * HEAD_K, (h + 1) * HEAD_K)


def _hv(h):
    return slice(h * HEAD_V, (h + 1) * HEAD_V)


def _gla_fwd(proj, wfg2p, bfg, gn, *, s):
    tr = min(512, s)
    nb = s // tr
    nc = tr // CHUNK

    def body(q_ref, k_ref, v_ref, r_ref, fz_ref, w_ref, b_ref, gn_ref, oa_ref, st_ref,
             state, la_scr, o_scr):
        i = pl.program_id(0)

        @pl.when(i == 0)
        def _():
            state[...] = jnp.zeros_like(state)

        fg = jnp.dot(fz_ref[...], w_ref[...], preferred_element_type=F32) + b_ref[...]
        la_scr[...] = _log_sigmoid(fg) * (1.0 / GATE_TAU)
        tri = _tri(False)

        def chunk(ci, carry):
            r0 = pl.multiple_of(ci * CHUNK, CHUNK)
            rows = pl.ds(r0, CHUNK)
            e, gam = _gla_gate_terms(la_scr[rows, :], tri)
            kd = (k_ref[rows, :].astype(F32) * e).astype(BF16)
            qs = (q_ref[rows, :].astype(F32) * Q_SCALE).astype(BF16)
            v_c = v_ref[rows, :]
            for h in range(GLA_HEADS):
                upd = lax.dot_general(v_c[:, _hv(h)], kd[:, _hk(h)], _DOT_DIMS["tn"],
                                      preferred_element_type=F32)
                st_h = state[:, _hk(h)] * gam[:, _hk(h)] + upd
                state[:, _hk(h)] = st_h
                o_scr[rows, _hv(h)] = lax.dot_general(qs[:, _hk(h)], st_h.astype(BF16), _DOT_DIMS["nt"],
                                                       preferred_element_type=F32)
            st_ref[ci] = state[...]
            return carry

        lax.fori_loop(0, nc, chunk, 0, unroll=True)
        for h in range(GLA_HEADS):
            o = o_scr[:, _hv(h)]
            rs = lax.rsqrt(jnp.mean(o * o, axis=-1, keepdims=True) + NORM_EPS)
            rv = r_ref[:, _hv(h)].astype(F32)
            oa_ref[:, _hv(h)] = ((o * rs * gn_ref[...]).astype(F32) * (rv * _sigmoid(rv))).astype(BF16)

    return pl.pallas_call(
        body, name="gla_fwd",
        out_shape=(jax.ShapeDtypeStruct((s, VAL_WIDTH), BF16),
                   jax.ShapeDtypeStruct((s // CHUNK, HEAD_V, KEY_WIDTH), F32)),
        grid=(nb,),
        in_specs=[pl.BlockSpec((tr, KEY_WIDTH), lambda i: (i, COL_Q // KEY_WIDTH)),
                  pl.BlockSpec((tr, KEY_WIDTH), lambda i: (i, COL_K // KEY_WIDTH)),
                  pl.BlockSpec((tr, VAL_WIDTH), lambda i: (i, COL_V // VAL_WIDTH)),
                  pl.BlockSpec((tr, VAL_WIDTH), lambda i: (i, COL_R // VAL_WIDTH)),
                  pl.BlockSpec((tr, FZ_PAD), lambda i: (i, COL_FZ // FZ_PAD)),
                  pl.BlockSpec((FZ_PAD, KEY_WIDTH), lambda i: (0, 0)),
                  pl.BlockSpec((1, KEY_WIDTH), lambda i: (0, 0)),
                  pl.BlockSpec((1, HEAD_V), lambda i: (0, 0))],
        out_specs=(pl.BlockSpec((tr, VAL_WIDTH), lambda i: (i, 0)),
                   pl.BlockSpec((nc, HEAD_V, KEY_WIDTH), lambda i: (i, 0, 0))),
        scratch_shapes=[pltpu.VMEM((HEAD_V, KEY_WIDTH), F32), pltpu.VMEM((tr, KEY_WIDTH), F32),
                        pltpu.VMEM((tr, VAL_WIDTH), F32)],
        compiler_params=_cparams(("arbitrary",)),
    )(proj, proj, proj, proj, proj, wfg2p, bfg, gn)


def _gla_bwd(proj, doa, states, wfg2p, bfg, gn, dproj, *, s):
    tr = min(512, s)
    nb = s // tr
    nc = tr // CHUNK

    def body(q_ref, k_ref, v_ref, r_ref, fz_ref, doa_ref, st_ref, stp_ref, w_ref, b_ref, gn_ref, dproj_in_ref,
             dp_ref, dfz_ref, dgn_ref, dw_ref, db_ref,
             carry, fg_scr, la_scr, dla_scr, o_scr, do_scr):
        i = pl.program_id(0)

        @pl.when(i == 0)
        def _():
            carry[...] = jnp.zeros_like(carry)
            dgn_ref[...] = jnp.zeros_like(dgn_ref)
            dw_ref[...] = jnp.zeros_like(dw_ref)
            db_ref[...] = jnp.zeros_like(db_ref)

        fz = fz_ref[...]
        fg = jnp.dot(fz, w_ref[...], preferred_element_type=F32) + b_ref[...]
        fg_scr[...] = fg
        la_scr[...] = _log_sigmoid(fg) * (1.0 / GATE_TAU)

        def fwd_chunk(ci, c0):
            r0 = pl.multiple_of(ci * CHUNK, CHUNK)
            rows = pl.ds(r0, CHUNK)
            qs = (q_ref[rows, :].astype(F32) * Q_SCALE).astype(BF16)
            st = st_ref[ci].astype(BF16)
            for h in range(GLA_HEADS):
                o_scr[rows, _hv(h)] = lax.dot_general(qs[:, _hk(h)], st[:, _hk(h)], _DOT_DIMS["nt"],
                                                       preferred_element_type=F32)
            return c0

        lax.fori_loop(0, nc, fwd_chunk, 0, unroll=True)

        gnv = gn_ref[...]
        dgn_part = jnp.zeros((1, HEAD_V), F32)
        for h in range(GLA_HEADS):
            o = o_scr[:, _hv(h)]
            rs = lax.rsqrt(jnp.mean(o * o, axis=-1, keepdims=True) + NORM_EPS)
            nrm = o * rs
            rv = r_ref[:, _hv(h)].astype(F32)
            sg = _sigmoid(rv)
            doa_h = doa_ref[:, _hv(h)].astype(F32)
            don = doa_h * (rv * sg)
            dp_ref[:, COL_R + h * HEAD_V:COL_R + (h + 1) * HEAD_V] = (
                doa_h * (nrm * gnv) * (sg * (1.0 + rv * (1.0 - sg)))).astype(BF16)
            dgn_part = dgn_part + jnp.sum(don * nrm, axis=0, keepdims=True)
            dn = don * gnv
            do_scr[:, _hv(h)] = (rs * (dn - nrm * jnp.mean(dn * nrm, axis=-1, keepdims=True))).astype(BF16)
        dgn_ref[...] += dgn_part

        tri = _tri(False)
        tri_s = _tri(True)
        first_block = i == nb - 1

        def bwd_chunk(cc, c0):
            ci = nc - 1 - cc
            r0 = pl.multiple_of(ci * CHUNK, CHUNK)
            rows = pl.ds(r0, CHUNK)
            e, gam = _gla_gate_terms(la_scr[rows, :], tri)
            k_c = k_ref[rows, :].astype(F32)
            kd = k_c * e
            kd_b = kd.astype(BF16)
            qs = (q_ref[rows, :].astype(F32) * Q_SCALE).astype(BF16)
            v_c = v_ref[rows, :]
            do_c = do_scr[rows, :]
            st = st_ref[ci]
            st_b = st.astype(BF16)
            st_prev_in = st_ref[jnp.maximum(ci - 1, 0)]
            st_prev_edge = jnp.where(first_block, 0.0, stp_ref[0])
            st_prev = jnp.where(ci > 0, st_prev_in, st_prev_edge)
            dkd_parts = []
            dgam_parts = []
            for h in range(GLA_HEADS):
                dst = lax.dot_general(do_c[:, _hv(h)], qs[:, _hk(h)], _DOT_DIMS["tn"],
                                      preferred_element_type=F32) + carry[:, _hk(h)]
                dst_b = dst.astype(BF16)
                dqs = jnp.dot(do_c[:, _hv(h)], st_b[:, _hk(h)], preferred_element_type=F32)
                dp_ref[rows, COL_Q + h * HEAD_K:COL_Q + (h + 1) * HEAD_K] = (dqs * Q_SCALE).astype(BF16)
                dkd_parts.append(jnp.dot(v_c[:, _hv(h)], dst_b, preferred_element_type=F32))
                dp_ref[rows, COL_V + h * HEAD_V:COL_V + (h + 1) * HEAD_V] = lax.dot_general(
                    kd_b[:, _hk(h)], dst_b, _DOT_DIMS["nt"], preferred_element_type=F32).astype(BF16)
                dgam_parts.append(jnp.sum(dst * st_prev[:, _hk(h)], axis=0, keepdims=True))
                carry[:, _hk(h)] = dst * gam[:, _hk(h)]
            dkd = jnp.concatenate(dkd_parts, axis=1)
            dgam = jnp.concatenate(dgam_parts, axis=1)
            dp_ref[rows, COL_K:COL_K + KEY_WIDTH] = (dkd * e).astype(BF16)
            dz = dkd * kd
            dla_scr[rows, :] = gam * dgam + jnp.dot(tri_s, dz, precision=lax.Precision.HIGHEST,
                                                    preferred_element_type=F32)
            return c0

        lax.fori_loop(0, nc, bwd_chunk, 0, unroll=True)

        dfg = dla_scr[...] * (1.0 / GATE_TAU) * _sigmoid(-fg_scr[...])
        dfg_b = dfg.astype(BF16)
        dfz_ref[...] = lax.dot_general(dfg_b, w_ref[...], _DOT_DIMS["nt"],
                                       preferred_element_type=F32).astype(BF16)
        dw_ref[...] += lax.dot_general(fz, dfg_b, _DOT_DIMS["tn"], preferred_element_type=F32)
        db_ref[...] += jnp.sum(dfg, axis=0, keepdims=True)

    def rev(i):
        return nb - 1 - i

    return pl.pallas_call(
        body, name="gla_bwd",
        out_shape=(jax.ShapeDtypeStruct((s, PROJ_W), BF16),
                   jax.ShapeDtypeStruct((s, FZ_PAD), BF16),
                   jax.ShapeDtypeStruct((1, HEAD_V), F32),
                   jax.ShapeDtypeStruct((FZ_PAD, KEY_WIDTH), F32),
                   jax.ShapeDtypeStruct((1, KEY_WIDTH), F32)),
        grid=(nb,),
        in_specs=[pl.BlockSpec((tr, KEY_WIDTH), lambda i: (rev(i), COL_Q // KEY_WIDTH)),
                  pl.BlockSpec((tr, KEY_WIDTH), lambda i: (rev(i), COL_K // KEY_WIDTH)),
                  pl.BlockSpec((tr, VAL_WIDTH), lambda i: (rev(i), COL_V // VAL_WIDTH)),
                  pl.BlockSpec((tr, VAL_WIDTH), lambda i: (rev(i), COL_R // VAL_WIDTH)),
                  pl.BlockSpec((tr, FZ_PAD), lambda i: (rev(i), COL_FZ // FZ_PAD)),
                  pl.BlockSpec((tr, VAL_WIDTH), lambda i: (rev(i), 0)),
                  pl.BlockSpec((nc, HEAD_V, KEY_WIDTH), lambda i: (rev(i), 0, 0)),
                  pl.BlockSpec((1, HEAD_V, KEY_WIDTH), lambda i: (jnp.maximum(rev(i) * nc - 1, 0), 0, 0)),
                  pl.BlockSpec((FZ_PAD, KEY_WIDTH), lambda i: (0, 0)),
                  pl.BlockSpec((1, KEY_WIDTH), lambda i: (0, 0)),
                  pl.BlockSpec((1, HEAD_V), lambda i: (0, 0)), ANY],
        out_specs=(pl.BlockSpec((tr, 3 * VAL_WIDTH), lambda i: (rev(i), 0)),
                   pl.BlockSpec((tr, FZ_PAD), lambda i: (rev(i), 0)),
                   pl.BlockSpec((1, HEAD_V), lambda i: (0, 0)),
                   pl.BlockSpec((FZ_PAD, KEY_WIDTH), lambda i: (0, 0)),
                   pl.BlockSpec((1, KEY_WIDTH), lambda i: (0, 0))),
        scratch_shapes=[pltpu.VMEM((HEAD_V, KEY_WIDTH), F32),
                        pltpu.VMEM((tr, KEY_WIDTH), F32),
                        pltpu.VMEM((tr, KEY_WIDTH), F32),
                        pltpu.VMEM((tr, KEY_WIDTH), F32),
                        pltpu.VMEM((tr, VAL_WIDTH), F32),
                        pltpu.VMEM((tr, VAL_WIDTH), BF16)],
        input_output_aliases={11: 0},
        compiler_params=_cparams(("arbitrary",)),
    )(proj, proj, proj, proj, proj, doa, states, states, wfg2p, bfg, gn, dproj)


def _put_fz(dproj, dfz, *, s):
    tr = min(512, s)

    def body(dfz_ref, dproj_in_ref, o_ref):
        o_ref[...] = dfz_ref[...]

    return pl.pallas_call(
        body, name="put_dfz",
        out_shape=jax.ShapeDtypeStruct((s, PROJ_W), BF16),
        grid=(s // tr,),
        in_specs=[pl.BlockSpec((tr, FZ_PAD), lambda i: (i, 0)), ANY],
        out_specs=pl.BlockSpec((tr, FZ_PAD), lambda i: (i, COL_FZ // FZ_PAD)),
        input_output_aliases={1: 0},
        compiler_params=_cparams(("parallel",)),
    )(dfz, dproj)


_ADAM_C1 = 1.0 / (1.0 - ADAM_B1 ** ADAM_STEP)
_ADAM_C2 = 1.0 / (1.0 - ADAM_B2 ** ADAM_STEP)


def _adamw_math(wv, gv, mv, vv):
    nm = ADAM_B1 * mv + (1.0 - ADAM_B1) * gv
    nv = ADAM_B2 * vv + (1.0 - ADAM_B2) * (gv * gv)
    delta = -ADAM_LR * ((nm * _ADAM_C1) / (jnp.sqrt(nv * _ADAM_C2) + ADAM_EPS) + ADAM_WD * wv)
    return delta, nm, nv


def _adamw(w, g, m, v, *, name):
    shape = w.shape
    if w.ndim == 1:
        w, g, m, v = (t.reshape(1, 1, -1) for t in (w, g, m, v))
    elif w.ndim == 2:
        w, g, m, v = (t.reshape((1,) + t.shape) for t in (w, g, m, v))
    l, a, b = w.shape

    def body(w_ref, g_ref, m_ref, v_ref, d_ref, nm_ref, nv_ref):
        d_ref[...], nm_ref[...], nv_ref[...] = _adamw_math(w_ref[...], g_ref[...], m_ref[...], v_ref[...])

    blk = pl.BlockSpec((1, a, b), lambda li: (li, 0, 0))
    outs = pl.pallas_call(
        body, name=name,
        out_shape=tuple(jax.ShapeDtypeStruct((l, a, b), F32) for _ in range(3)),
        grid=(l,),
        in_specs=[blk, blk, blk, blk],
        out_specs=(blk, blk, blk),
        compiler_params=_cparams(("parallel",)),
    )(w, g, m, v)
    return tuple(o.reshape(shape) for o in outs)


def _adamw_layers(w, reduced, received, on_core0, m, v, *, name):
    l, a, b = w.shape
    assert l == DEPTH == 2
    ta = _row_tile(a, F32_SUBLANES, 384)

    def body(flag_ref, w_ref, r0_ref, o0_ref, r1_ref, o1_ref, m_ref, v_ref, g_ref, d_ref, nm_ref, nv_ref):
        core0 = flag_ref[...] > 0.5
        g0 = jnp.where(core0, r0_ref[...], o0_ref[...])
        g1 = jnp.where(core0, o1_ref[...], r1_ref[...])
        gv = jnp.where(pl.program_id(0) == 0, g0, g1)
        g_ref[0] = gv
        d_ref[0], nm_ref[0], nv_ref[0] = _adamw_math(w_ref[0], gv, m_ref[0], v_ref[0])

    blk = pl.BlockSpec((1, ta, b), lambda li, ai: (li, ai, 0))

    def grad_block(layer):
        return pl.BlockSpec((ta, b), lambda li, ai: (jnp.where(li == layer, ai, 0), 0))

    g0blk, g1blk = grad_block(0), grad_block(1)
    return pl.pallas_call(
        body, name=name,
        out_shape=tuple(jax.ShapeDtypeStruct((l, a, b), F32) for _ in range(4)),
        grid=(l, a // ta),
        in_specs=[pl.BlockSpec((1, 1), lambda li, ai: (0, 0)), blk, g0blk, g0blk, g1blk, g1blk, blk, blk],
        out_specs=(blk, blk, blk, blk),
        compiler_params=_cparams(("parallel", "parallel")),
    )(on_core0, w, reduced[0], received[0], reduced[1], received[1], m, v)


MESH_ID = pl.DeviceIdType.MESH
ANY = pl.BlockSpec(memory_space=pl.ANY)


def _position():
    x, y, c = lax.axis_index("x"), lax.axis_index("y"), lax.axis_index("c")
    chips = [(1 - x, y), (x, 1 - y), (1 - x, 1 - y)]
    return x, y, c, chips


def _chip_index(xy):
    return 2 * xy[0] + xy[1]


HBM = pl.BlockSpec(memory_space=pltpu.HBM)
SEM = pl.BlockSpec(memory_space=pltpu.SEMAPHORE)
DATAFLOW_EFFECT = pltpu.SideEffectType.DATAFLOW_SIDE_EFFECTING
TOKEN_SHAPE = (8, LANE)


def _landing(shape, dtype):
    return pltpu.with_memory_space_constraint(lax.empty(shape, dtype), pltpu.HBM)


def _split_start(bufs, sem_shape, issue, *, name, after=None):
    n = len(bufs)
    n_in = n + (after is not None)

    def body(*refs):
        issue(refs[:n], refs[n_in], refs[n_in + 1])
        token = refs[-1]
        token[...] = jnp.zeros_like(token)

    operands = [pltpu.with_memory_space_constraint(t, pltpu.HBM) for t in bufs]
    outs = pl.pallas_call(
        body, name=name,
        out_shape=(pltpu.SemaphoreType.DMA(sem_shape), pltpu.SemaphoreType.DMA(sem_shape),
                   *[pltpu.HBM(t.shape, t.dtype) for t in bufs], jax.ShapeDtypeStruct(TOKEN_SHAPE, F32)),
        in_specs=[HBM] * n + [ANY] * (after is not None),
        out_specs=(SEM, SEM, *[HBM] * n, pl.BlockSpec(memory_space=pltpu.VMEM)),
        input_output_aliases={i: 2 + i for i in range(n)},
        compiler_params=pltpu.CompilerParams(has_side_effects=DATAFLOW_EFFECT),
    )(*operands, *([after] if after is not None else []))
    return outs[0], outs[1], list(outs[2:2 + n]), outs[-1]


def _split_wait(started, after, settle, *, name):
    send_sems, recv_sems, bufs, _ = started
    n = len(bufs)
    afters = tuple(after) if isinstance(after, (tuple, list)) else (after,)

    def body(*refs):
        settle(refs[:n], refs[n], refs[n + 1])

    outs = pl.pallas_call(
        body, name=name,
        out_shape=tuple(pltpu.HBM(t.shape, t.dtype) for t in bufs),
        in_specs=[HBM] * n + [SEM, SEM] + [ANY] * len(afters),
        out_specs=tuple([HBM] * n),
        input_output_aliases={i: i for i in range(n)},
        compiler_params=pltpu.CompilerParams(has_side_effects=DATAFLOW_EFFECT),
    )(*bufs, send_sems, recv_sems, *afters)
    return list(outs)


def _to_sibling(bufs, n, send_sems, recv_sems):
    x, y, c, _ = _position()
    return [pltpu.make_async_remote_copy(
        src_ref=bufs[a], dst_ref=bufs[n + a], send_sem=send_sems.at[a], recv_sem=recv_sems.at[a],
        device_id=(x, y, 1 - c), device_id_type=MESH_ID) for a in range(n)]


def _sibling_push_start(layer, sender_is_reducer, arrays, *, name):
    n = len(arrays)
    sender = layer if sender_is_reducer else 1 - layer

    def issue(bufs, send_sems, recv_sems):
        @pl.when(lax.axis_index("c") == sender)
        def _():
            for cp in _to_sibling(bufs, n, send_sems, recv_sems):
                cp.start()

    lands = [_landing(t.shape, t.dtype) for t in arrays]
    return _split_start(list(arrays) + lands, (n,), issue, name=name)


def _sibling_push_wait(layer, sender_is_reducer, started, after, *, name):
    n = len(started[2]) // 2
    sender = layer if sender_is_reducer else 1 - layer

    def settle(bufs, send_sems, recv_sems):
        c = lax.axis_index("c")

        @pl.when(c == sender)
        def _():
            for cp in _to_sibling(bufs, n, send_sems, recv_sems):
                cp.wait_send()

        @pl.when(c != sender)
        def _():
            for cp in _to_sibling(bufs, n, send_sems, recv_sems):
                cp.wait_recv()

    outs = _split_wait(started, after, settle, name=name)
    return outs[:n], outs[n:]


def _chip_copies(bufs, n, send_sems, recv_sems):
    x, y, c, chips = _position()
    return [pltpu.make_async_remote_copy(
        src_ref=bufs[a].at[_chip_index(chips[k])], dst_ref=bufs[n + a].at[k],
        send_sem=send_sems.at[3 * a + k], recv_sem=recv_sems.at[3 * a + k],
        device_id=(chips[k][0], chips[k][1], c), device_id_type=MESH_ID)
        for a in range(n) for k in range(3)]


def _chip_exchange_start(layer, parts, *, name):
    n = len(parts)

    def issue(bufs, send_sems, recv_sems):
        @pl.when(lax.axis_index("c") == layer)
        def _():
            for cp in _chip_copies(bufs, n, send_sems, recv_sems):
                cp.start()

    lands = [_landing((3,) + t.shape[1:], t.dtype) for t in parts]
    return _split_start(list(parts) + lands, (3 * n,), issue, name=name)


def _chip_exchange_wait(layer, started, after, *, name):
    n = len(started[2]) // 2

    def settle(bufs, send_sems, recv_sems):
        @pl.when(lax.axis_index("c") == layer)
        def _():
            for cp in _chip_copies(bufs, n, send_sems, recv_sems):
                cp.wait()

    outs = _split_wait(started, after, settle, name=name)
    return outs[:n], outs[n:]


def _gather_copies(layer, bufs, n, send_sems, recv_sems, arriving):
    x, y, c, chips = _position()
    me = _chip_index((x, y))
    return [pltpu.make_async_remote_copy(
        src_ref=bufs[a], dst_ref=bufs[n + a].at[_chip_index(chips[k]) if arriving else me],
        send_sem=send_sems.at[3 * a + k], recv_sem=recv_sems.at[3 * a + k],
        device_id=(chips[k][0], chips[k][1], c), device_id_type=MESH_ID)
        for a in range(n) for k in range(3)]


def _gather_start(layer, shards, *, name, after=None):
    n = len(shards)

    def issue(bufs, send_sems, recv_sems):
        @pl.when(lax.axis_index("c") == layer)
        def _():
            for cp in _gather_copies(layer, bufs, n, send_sems, recv_sems, False):
                cp.start()

    lands = [_landing((N_CHIPS,) + t.shape, t.dtype) for t in shards]
    return _split_start(list(shards) + lands, (3 * n,), issue, name=name, after=after)


def _gather_wait(layer, started, after, *, name):
    n = len(started[2]) // 2

    def settle(bufs, send_sems, recv_sems):
        @pl.when(lax.axis_index("c") == layer)
        def _():
            for cp in _gather_copies(layer, bufs, n, send_sems, recv_sems, False):
                cp.wait_send()
            for cp in _gather_copies(layer, bufs, n, send_sems, recv_sems, True):
                cp.wait_recv()

    return _split_wait(started, after, settle, name=name)[n:]


def _handoff_copies(bufs, n, send_sems, recv_sems):
    x, y, c, chips = _position()
    out = []
    for a in range(n):
        for k in range(3):
            slot = bufs[a].at[_chip_index(chips[k])]
            out.append(pltpu.make_async_remote_copy(
                src_ref=slot, dst_ref=slot, send_sem=send_sems.at[3 * a + k], recv_sem=recv_sems.at[3 * a + k],
                device_id=(x, y, 1 - c), device_id_type=MESH_ID))
    return out


def _handoff_start(layer, gathered, *, name):
    n = len(gathered)

    def issue(bufs, send_sems, recv_sems):
        @pl.when(lax.axis_index("c") == layer)
        def _():
            for cp in _handoff_copies(bufs, n, send_sems, recv_sems):
                cp.start()

    return _split_start(list(gathered), (3 * n,), issue, name=name)


def _handoff_wait(layer, started, after, *, name):
    n = len(started[2])

    def settle(bufs, send_sems, recv_sems):
        c = lax.axis_index("c")

        @pl.when(c == layer)
        def _():
            for cp in _handoff_copies(bufs, n, send_sems, recv_sems):
                cp.wait_send()

        @pl.when(c != layer)
        def _():
            for cp in _handoff_copies(bufs, n, send_sems, recv_sems):
                cp.wait_recv()

    return _split_wait(started, after, settle, name=name)


F32_SUBLANES = 8
BF16_SUBLANES = 16


def _row_tile(a, sublanes=BF16_SUBLANES, max_rows=704):
    best = None
    for cand in range(sublanes, min(a, max_rows) + 1, sublanes):
        if a % cand == 0:
            best = cand
    assert best is not None, a
    return best


def _pair_sum(mine, sib, *, name):
    nchip, a, b = sib.shape
    ta = _row_tile(a)

    def body(m_ref, s_ref, o_ref):
        o_ref[...] = (m_ref[...].astype(F32) + s_ref[...].astype(F32)).astype(BF16)

    blk = pl.BlockSpec((1, ta, b), lambda j, r: (j, r, 0))
    return pl.pallas_call(
        body, name=name,
        out_shape=jax.ShapeDtypeStruct((nchip, a, b), BF16),
        grid=(nchip, a // ta),
        in_specs=[blk, blk], out_specs=blk,
        compiler_params=_cparams(("parallel", "parallel")),
    )(mine, sib)


def _total_sum(own, recv, *, name):
    a, b = own.shape
    ta = _row_tile(a)

    def body(o_ref, r_ref, t_ref):
        acc = o_ref[...].astype(F32)
        for k in range(3):
            acc = acc + r_ref[k].astype(F32)
        t_ref[...] = acc

    blk = pl.BlockSpec((ta, b), lambda r: (r, 0))
    return pl.pallas_call(
        body, name=name,
        out_shape=jax.ShapeDtypeStruct((a, b), F32),
        grid=(a // ta,),
        in_specs=[blk, pl.BlockSpec((3, ta, b), lambda r: (0, r, 0))], out_specs=blk,
        compiler_params=_cparams(("parallel",)),
    )(own, recv)


def _all_reduce_small(packed):
    rows, width = packed.shape

    def body(x_ref, out_ref, gath, send_sems, recv_sems, local_sem):
        x, y, c, chips = _position()
        me, sibling = (x, y, c), (x, y, 1 - c)

        def slot(px, py, pc):
            return gath.at[4 * px + 2 * py + pc]

        def copy(k, block, to, src=None):
            return pltpu.make_async_remote_copy(
                src_ref=slot(*block) if src is None else src, dst_ref=slot(*block),
                send_sem=send_sems.at[k], recv_sem=recv_sems.at[k], device_id=to, device_id_type=MESH_ID)

        mine = pltpu.make_async_copy(x_ref, slot(*me), local_sem)
        mine.start()
        first = [copy(0, me, sibling, src=x_ref)]
        first += [copy(1 + j, me, (*chip, c), src=x_ref) for j, chip in enumerate(chips)]
        for cp in first:
            cp.start()
        passed = [copy(4 + j, (*chip, c), sibling) for j, chip in enumerate(chips)]
        for j, chip in enumerate(chips):
            copy(1 + j, (*chip, c), me).wait_recv()
            passed[j].start()
        copy(0, sibling, me).wait_recv()
        for j, chip in enumerate(chips):
            copy(4 + j, (*chip, 1 - c), me).wait_recv()
        for cp in first + passed:
            cp.wait_send()
        mine.wait()
        acc = gath[0]
        for d in range(1, N_DEV):
            acc = acc + gath[d]
        out_ref[...] = acc

    return pl.pallas_call(
        body, name="all_reduce_small",
        out_shape=jax.ShapeDtypeStruct((rows, width), F32),
        in_specs=[pl.BlockSpec(memory_space=pltpu.VMEM)],
        out_specs=pl.BlockSpec(memory_space=pltpu.VMEM),
        scratch_shapes=[pltpu.VMEM((N_DEV, rows, width), F32), pltpu.SemaphoreType.DMA((7,)),
                        pltpu.SemaphoreType.DMA((7,)), pltpu.SemaphoreType.DMA],
    )(packed)


def _mixer_forward_branches(x0, h, w, s, dep=None, before_conv=None):
    proj = _matmul(h, w["w_in_t"], mode="nt", out_dtype=BF16, tm=1024, tn=1664, tk=1024, name="proj_fwd",
                   dep=dep)
    oa, states = _gla_fwd(proj, w["w_fg2"], w["b_fg"], w["gla_norm_g"], s=s)
    conv_wb = w["conv_wb"]
    if before_conv is not None:
        conv_wb = conv_wb + before_conv(oa)[0, 0]
    cb_in = _conv_fwd(proj, conv_wb, s=s)
    return dict(x0=x0, h=h, proj=proj, oa=oa, states=states, cb_in=cb_in)


def _mixer_forward_out(sv, w, s, dep=None):
    ya, yb, mix, x1 = _mixer_out_fwd(sv["oa"], sv["cb_in"], sv["proj"], sv["x0"], w["w_oa"], w["w_ob"], w["w_o"],
                                     s=s, dep=dep)
    return x1, dict(sv, ya=ya, yb=yb, mix=mix)


def _ffn_forward_hidden(x1, w, s, after_gate=None):
    h2 = _rms_fwd(x1, w["norm2_g"], name="rms2_fwd")
    gt = _matmul(h2, w["w_gate_t"], mode="nt", out_dtype=BF16, tm=1024, tn=1408, tk=1024, name="ffn_gate_fwd")
    up = _matmul(h2, w["w_up_t"], mode="nt", out_dtype=BF16, tm=1024, tn=1408, tk=1024, name="ffn_up_fwd",
                 dep=None if after_gate is None else after_gate(gt))
    return dict(x1=x1, h2=h2, gt=gt, up=up, hid=_swiglu_fwd(gt, up, s=s))


def _ffn_forward_out(sv, w, dep=None):
    return _matmul(sv["hid"], w["w_ffn_down"], mode="nn", out_dtype=F32, tm=1024, tn=1024, tk=FFN_HIDDEN,
                   name="ffn_out_fwd", add=sv["x1"], dep=dep)


def _ffn_backward_grads(dx2b, w, sv, s):
    g = {}
    dhid = _matmul(dx2b, w["w_ffn_down"], mode="nt", out_dtype=BF16, tm=1024, tn=FFN_HIDDEN, tk=1024,
                   name="ffn_out_bwd")
    g["w_ffn_down"] = _matmul(sv["hid"], dx2b, mode="tn", out_dtype=BF16, tm=1408, tn=1024, tk=1024,
                              name="ffn_out_wgrad")
    dgt, dup = _swiglu_bwd(sv["gt"], sv["up"], dhid, s=s)
    g["w_gate_t"] = _matmul(dgt, sv["h2"], mode="tn", out_dtype=BF16, tm=1408, tn=1024, tk=1024,
                            name="ffn_gate_wgrad")
    g["w_up_t"] = _matmul(dup, sv["h2"], mode="tn", out_dtype=BF16, tm=1408, tn=1024, tk=1024,
                          name="ffn_up_wgrad")
    return g, dgt, dup, dhid


def _ffn_backward_input(dgt, dup, dx2, w, sv, dep=None):
    dh2 = _matmul(dgt, w["w_gate_t"], mode="nn", out_dtype=BF16, tm=1024, tn=1024, tk=FFN_HIDDEN,
                  name="ffn_gate_bwd", dep=dep)
    dh2 = _matmul(dup, w["w_up_t"], mode="nn", out_dtype=BF16, tm=1024, tn=1024, tk=FFN_HIDDEN,
                  name="ffn_up_bwd", add=dh2)
    return _rms_bwd(sv["x1"], w["norm2_g"], dh2, dx2, name="rms2_bwd")


def _mixer_backward_branches(dx1b, w, sv, s, dep=None):
    g = {}
    dya, dyb, dcb, doa, dproj = _mixer_out_bwd(dx1b, sv["proj"], sv["ya"], sv["yb"], w["w_oa"], w["w_ob"],
                                               w["w_o"], s=s, dep=dep)
    g["w_o"] = _matmul(sv["mix"], dx1b, mode="tn", out_dtype=BF16, tm=1024, tn=1024, tk=2048, name="wo_wgrad")
    g["w_ob"] = _matmul(sv["cb_in"], dyb, mode="tn", out_dtype=BF16, tm=1024, tn=1024, tk=2048, name="yb_wgrad")
    g["w_oa"] = _matmul(sv["oa"], dya, mode="tn", out_dtype=BF16, tm=1024, tn=1024, tk=2048, name="ya_wgrad")
    dproj, g["conv_wb"] = _conv_bwd(sv["proj"], dcb, w["conv_wb"], dproj, s=s)
    dproj, dfz, g["gla_norm_g"], g["w_fg2"], g["b_fg"] = _gla_bwd(
        sv["proj"], doa, sv["states"], w["w_fg2"], w["b_fg"], w["gla_norm_g"], dproj, s=s)
    return g, _put_fz(dproj, dfz, s=s)


def _proj_wgrad(dproj, sv, dep=None):
    return _matmul(dproj, sv["h"], mode="tn", out_dtype=BF16, tm=1664, tn=1024, tk=1024, name="proj_wgrad",
                   dep=dep)


def _proj_bwd(dproj, w, dep=None):
    return _matmul(dproj, w["w_in_t"], mode="nn", out_dtype=BF16, tm=1024, tn=1024, tk=1664, name="proj_bwd",
                   dep=dep)


def _cols_from_chips(t):
    return jnp.transpose(t, (1, 0, 2)).reshape(t.shape[1], -1)


W_IN_ROWS = IN_WIDTH // N_CHIPS
W_IN_ROWS_PAD = -(-W_IN_ROWS // BF16_SUBLANES) * BF16_SUBLANES


def _w_in_t_shard(t):
    return jnp.pad(jnp.transpose(t, (0, 2, 1)), ((0, 0), (0, W_IN_ROWS_PAD - W_IN_ROWS), (0, 0)))


def _w_in_t_unshard(t):
    return jnp.transpose(t[:, :W_IN_ROWS], (0, 2, 1))


def _w_in_row_segments():
    runs = [(0, 0, FZ_ORIG), (FZ_ORIG, FZ_ORIG + GATE_RANK, COL_FZ - FZ_ORIG), (COL_FZ, FZ_ORIG, GATE_RANK)]
    out = []
    for kernel_row, ref_row, length in runs:
        while length:
            chip, local = divmod(ref_row, W_IN_ROWS)
            n = min(length, W_IN_ROWS - local)
            out.append((kernel_row, chip * W_IN_ROWS_PAD + local, n))
            kernel_row, ref_row, length = kernel_row + n, ref_row + n, length - n
    return out


def _permute_rows(src, n_out, segments, *, name):
    n_src, width = src.shape
    block = _row_tile(n_out)
    window = block + BF16_SUBLANES
    assert n_src >= window and src.dtype == BF16

    def body(src_ref, out_ref):
        rows = lax.broadcasted_iota(jnp.int32, (block, window), 0)
        cols = lax.broadcasted_iota(jnp.int32, (block, window), 1)
        for b0 in range(0, n_out, block):
            acc = None
            for o, s, n in segments:
                lo, hi = max(o, b0), min(o + n, b0 + block)
                if lo >= hi:
                    continue
                first = s + (lo - o)
                base = max(0, min(first // BF16_SUBLANES * BF16_SUBLANES, n_src - window))
                shift = (first - base) - (lo - b0)
                pick = (cols == rows + shift) & (rows >= lo - b0) & (rows < hi - b0)
                part = jnp.dot(jnp.where(pick, 1.0, 0.0).astype(BF16), src_ref[base:base + window, :],
                               preferred_element_type=F32)
                acc = part if acc is None else acc + part
            out_ref[b0:b0 + block, :] = (jnp.zeros((block, width), F32) if acc is None else acc).astype(BF16)

    return pl.pallas_call(
        body, name=name,
        out_shape=jax.ShapeDtypeStruct((n_out, width), BF16),
        in_specs=[pl.BlockSpec(memory_space=pltpu.VMEM)],
        out_specs=pl.BlockSpec(memory_space=pltpu.VMEM),
        compiler_params=_cparams(),
    )(src)


def _w_in_t_to_kernel(t):
    return _permute_rows(t.reshape(N_CHIPS * W_IN_ROWS_PAD, D_MODEL), PROJ_W, _w_in_row_segments(),
                         name="w_in_to_kernel_rows")


def _w_in_t_from_kernel(g):
    segments = [(slab, kernel_row, n) for kernel_row, slab, n in _w_in_row_segments()]
    out = _permute_rows(g, N_CHIPS * W_IN_ROWS_PAD, segments, name="w_in_to_chip_rows")
    return out.reshape(N_CHIPS, W_IN_ROWS_PAD, D_MODEL)


def kernel(x, norm1_g, w_in, w_fg2, b_fg, gla_norm_g, w_oa, conv_w, conv_b, w_ob, w_o, norm2_g, w_ffn_gate, w_ffn_up, w_ffn_down, final_g, loss_target, m_norm1_g, m_w_in, m_w_fg2, m_b_fg, m_gla_norm_g, m_w_oa, m_conv_w, m_conv_b, m_w_ob, m_w_o, m_norm2_g, m_w_ffn_gate, m_w_ffn_up, m_w_ffn_down, m_final_g, v_norm1_g, v_w_in, v_w_fg2, v_b_fg, v_gla_norm_g, v_w_oa, v_conv_w, v_conv_b, v_w_ob, v_w_o, v_norm2_g, v_w_ffn_gate, v_w_ffn_up, v_w_ffn_down, v_final_g):
    cx_ = lax.axis_index("x")
    cy_ = lax.axis_index("y")
    cc_ = lax.axis_index("c")
    me = 2 * cx_ + cy_
    on_core0 = jnp.where(cc_ == 0, 1.0, 0.0).astype(F32).reshape(1, 1)

    def swap(t):
        return jnp.swapaxes(t, 1, 2)

    big_names = ["w_in", "w_oa", "w_ob", "w_o", "w_ffn_gate", "w_ffn_up", "w_ffn_down"]
    views = dict(
        w_in=tuple(_w_in_t_shard(t) for t in (w_in, m_w_in, v_w_in)),
        w_oa=(w_oa, m_w_oa, v_w_oa), w_ob=(w_ob, m_w_ob, v_w_ob), w_o=(w_o, m_w_o, v_w_o),
        w_ffn_gate=tuple(swap(t) for t in (w_ffn_gate, m_w_ffn_gate, v_w_ffn_gate)),
        w_ffn_up=tuple(swap(t) for t in (w_ffn_up, m_w_ffn_up, v_w_ffn_up)),
        w_ffn_down=(w_ffn_down, m_w_ffn_down, v_w_ffn_down))
    from_view = dict(w_in=_w_in_t_unshard, w_ffn_gate=swap, w_ffn_up=swap)

    s = x.shape[1]

    proj_names = ["w_in"]
    first_names = ["w_in", "w_fg2", "conv_w"]
    rest_names = ["w_oa", "w_ob", "w_o", "w_ffn_gate", "w_ffn_up", "w_ffn_down"]
    mixer_names = ["w_oa", "w_ob", "w_o"]
    ffn_names = ["w_ffn_gate", "w_ffn_up", "w_ffn_down"]
    weight_key = dict(w_in="w_in_t", w_ffn_gate="w_gate_t", w_ffn_up="w_up_t")
    full_shape = dict(w_oa=(VAL_WIDTH, D_MODEL), w_ob=(CONV_CH, D_MODEL), w_o=(D_MODEL, D_MODEL),
                      w_ffn_gate=(FFN_HIDDEN, D_MODEL), w_ffn_up=(FFN_HIDDEN, D_MODEL),
                      w_ffn_down=(FFN_HIDDEN, D_MODEL))

    conv_w_p = jnp.pad(conv_w, ((0, 0), (0, 8 - conv_w.shape[1]), (0, 0)))
    small_shards = dict(w_fg2=w_fg2, conv_w=conv_w_p)

    def shards_of(l, names):
        return [small_shards[n][l] if n in small_shards else views[n][0][l].astype(BF16) for n in names]

    def small_weights(l):
        return dict(norm1_g=norm1_g[l], norm2_g=norm2_g[l], b_fg=b_fg[l].reshape(1, KEY_WIDTH),
                    gla_norm_g=gla_norm_g[l].reshape(1, HEAD_V))

    def full_weights(l, names, gathered, shards):
        out = {}
        for n, g, t in zip(names, gathered, shards):
            g = lax.dynamic_update_index_in_dim(g, t[None], me, axis=0)
            if n == "w_in":
                out["w_in_t"] = _w_in_t_to_kernel(g)
            elif n == "w_fg2":
                out[n] = jnp.pad(_cols_from_chips(g), ((0, FZ_PAD - GATE_RANK), (0, 0))).astype(BF16)
            elif n == "conv_w":
                out["conv_wb"] = (jnp.pad(_cols_from_chips(g)[:3], ((0, 5), (0, 0)))
                                  + jnp.pad(conv_b[l].reshape(1, CONV_CH), ((3, 4), (0, 0))))
            else:
                out[weight_key.get(n, n)] = g.reshape(full_shape[n])
        return out

    def per_chip(g, names):
        out = []
        for n in names:
            t = g[weight_key.get(n, n)]
            out.append(_w_in_t_from_kernel(t) if n == "w_in"
                       else t.reshape(N_CHIPS, t.shape[0] // N_CHIPS, t.shape[1]))
        return out

    def gather(l, names, tag, after=None):
        shards = shards_of(l, names)
        return shards, _gather_start(l, shards, name=f"gather_{tag}_start", after=after)

    def gathered_to_sibling(l, started, after, tag):
        return _handoff_start(l, _gather_wait(l, started, after, name=f"gather_{tag}_wait"),
                              name=f"handoff_{tag}_start")

    def feed(l, g, names, tag):
        return _sibling_push_start(l, False, per_chip(g, names), name=f"feed_{tag}_start")

    def pair_and_exchange(l, fed, after, names, tag):
        mine, sib = _sibling_push_wait(l, False, fed, after, name=f"feed_{tag}_wait")
        parts = [_pair_sum(a, b, name=f"pair_sum_{tag}_{n}") for n, a, b in zip(names, mine, sib)]
        return _chip_exchange_start(l, parts, name=f"exchange_{tag}_start")

    def total_and_share(l, swapped, after, names, tag):
        parts, recvs = _chip_exchange_wait(l, swapped, after, name=f"exchange_{tag}_wait")
        owns = [lax.dynamic_index_in_dim(p, me, 0, keepdims=False) for p in parts]
        totals = [_total_sum(o, rc, name=f"total_{tag}_{n}") for n, o, rc in zip(names, owns, recvs)]
        return _sibling_push_start(l, True, totals, name=f"share_{tag}_start")

    def shared(l, sharing, after, names, tag):
        totals, others = _sibling_push_wait(l, True, sharing, after, name=f"share_{tag}_wait")
        return {n: (t, o) for n, t, o in zip(names, totals, others)}

    def branches_under_gather(l, xin, h, w, tag, shards_r, started_r):
        box = {}

        def before_conv(oa):
            box["handoff"] = gathered_to_sibling(l, started_r, oa, tag)
            return box["handoff"][3]

        sv = _mixer_forward_branches(xin, h, w, s, dep=started_r[3], before_conv=before_conv)
        rest = _handoff_wait(l, box["handoff"], sv["cb_in"], name=f"handoff_{tag}_wait")
        w.update(full_weights(l, rest_names, rest, shards_r))
        return sv

    shards_p, started = gather(0, first_names, "p0")
    h0 = _rms_fwd(x[0], norm1_g[0], name="rms1_fwd")
    later_shards = shards_of(0, rest_names) + shards_of(1, first_names) + shards_of(1, rest_names)
    started = gathered_to_sibling(0, started, (h0, views["w_in"][1], *later_shards), "p0")
    w0 = small_weights(0)
    w0.update(full_weights(0, first_names, _handoff_wait(0, started, views["w_in"][2], name="handoff_p0_wait"),
                           shards_p))
    sv0 = branches_under_gather(0, x[0], h0, w0, "r0", *gather(0, rest_names, "r0", after=w0["w_in_t"]))

    shards_p1, started = gather(1, first_names, "p1", after=w0["w_o"])
    x1, sv0m = _mixer_forward_out(sv0, w0, s, dep=started[3])
    box1 = {}

    def after_gate(gt):
        box1["handoff"] = gathered_to_sibling(1, started, gt, "p1")
        box1["rest"] = gather(1, rest_names, "r1", after=box1["handoff"][2][0])
        return box1["handoff"][3] + box1["rest"][1][3]

    sv0f = _ffn_forward_hidden(x1, w0, s, after_gate=after_gate)
    x2 = _ffn_forward_out(sv0f, w0)
    w1 = small_weights(1)
    w1.update(full_weights(1, first_names, _handoff_wait(1, box1["handoff"], x2, name="handoff_p1_wait"),
                           shards_p1))

    h1 = _rms_fwd(x2, norm1_g[1], name="rms1_fwd")
    x3, sv1m = _mixer_forward_out(branches_under_gather(1, x2, h1, w1, "r1", *box1["rest"]), w1, s)
    sv1f = _ffn_forward_hidden(x3, w1, s)
    x4 = _ffn_forward_out(sv1f, w1)
    loss_local, dx, dxb, dgf = _loss_head(x4, final_g, loss_target[0])
    loss = lax.psum(loss_local[0, 0], ("x", "y", "c"))

    g1, dgt, dup, _ = _ffn_backward_grads(dxb, w1, sv1f, s)
    dx_mid, dxb_mid, g1["norm2_g"] = _ffn_backward_input(dgt, dup, dx, w1, sv1f)
    gm, dproj = _mixer_backward_branches(dxb_mid, w1, sv1m, s)
    g1.update(gm)
    g1["w_in_t"] = _proj_wgrad(dproj, sv1m)
    fed_1 = feed(1, g1, big_names, "l1")
    dh = _proj_bwd(dproj, w1, dep=fed_1[3])
    dx, dxb, g1["norm1_g"] = _rms_bwd(sv1m["x0"], norm1_g[1], dh, dx_mid, name="rms1_bwd")

    g0, dgt, dup, dhid = _ffn_backward_grads(dxb, w0, sv0f, s)
    swap_1 = pair_and_exchange(1, fed_1, dhid, big_names, "l1")
    fed_f = feed(0, g0, ffn_names, "f0")
    dx_mid, dxb_mid, g0["norm2_g"] = _ffn_backward_input(dgt, dup, dx, w0, sv0f, dep=swap_1[3] + fed_f[3])
    swap_f = pair_and_exchange(0, fed_f, dx_mid, ffn_names, "f0")
    gm, dproj = _mixer_backward_branches(dxb_mid, w0, sv0m, s, dep=swap_f[3])
    g0.update(gm)
    share_1 = total_and_share(1, swap_1, dproj, big_names, "l1")
    share_f = total_and_share(0, swap_f, dproj, ffn_names, "f0")
    fed_m = feed(0, g0, mixer_names, "m0")
    g0["w_in_t"] = _proj_wgrad(dproj, sv0m, dep=share_1[3] + share_f[3] + fed_m[3])
    fed_p = feed(0, g0, proj_names, "p0")
    swap_m = pair_and_exchange(0, fed_m, g0["w_in_t"], mixer_names, "m0")
    dh = _proj_bwd(dproj, w0, dep=fed_p[3] + swap_m[3])
    reduced1 = shared(1, share_1, dh, big_names, "l1")
    reduced0 = shared(0, share_f, dh, ffn_names, "f0")
    grad_x, _, g0["norm1_g"] = _rms_bwd(sv0m["x0"], norm1_g[0], dh, dx_mid, name="rms1_bwd")

    swap_p = pair_and_exchange(0, fed_p, grad_x, proj_names, "p0")
    share_m = total_and_share(0, swap_m, swap_p[3], mixer_names, "m0")
    reduced0.update(shared(0, share_m, share_m[3], mixer_names, "m0"))
    grads = [g0, g1]

    def small_rows(t):
        return t.reshape(-1, D_MODEL)

    def tile_rows(t):
        return jnp.pad(t, ((0, -t.shape[0] % F32_SUBLANES), (0, 0)))

    g0, g1 = grads
    pieces = [
        jnp.concatenate([g0["norm1_g"], g1["norm1_g"]], axis=0),
        jnp.concatenate([g0["norm2_g"], g1["norm2_g"]], axis=0),
        dgf,
        small_rows(jnp.concatenate([g0["b_fg"], g1["b_fg"]], axis=1)),
        small_rows(jnp.concatenate([g0["gla_norm_g"], g1["gla_norm_g"],
                                    jnp.zeros((1, D_MODEL - 2 * HEAD_V), F32)], axis=1)),
        jnp.concatenate([g0["conv_wb"][3:4], g1["conv_wb"][3:4]], axis=0),
        jnp.concatenate([g0["conv_wb"][:3], g1["conv_wb"][:3]], axis=0),
        small_rows(jnp.stack([g0["w_fg2"][:GATE_RANK], g1["w_fg2"][:GATE_RANK]])),
    ]
    small = _all_reduce_small(jnp.concatenate([tile_rows(p) for p in pieces], axis=0))
    sg = dict(
        norm1_g=small[0:2], norm2_g=small[8:10], final_g=small[16],
        b_fg=small[24].reshape(DEPTH, KEY_WIDTH), gla_norm_g=small[32, :DEPTH * HEAD_V].reshape(DEPTH, HEAD_V),
        conv_b=small[40:42],
        conv_w=lax.dynamic_slice_in_dim(small[48:54].reshape(DEPTH, 3, CONV_CH), me * (CONV_CH // N_CHIPS),
                                        CONV_CH // N_CHIPS, axis=2),
        w_fg2=lax.dynamic_slice_in_dim(small[56:72].reshape(DEPTH, GATE_RANK, KEY_WIDTH),
                                       me * (KEY_WIDTH // N_CHIPS), KEY_WIDTH // N_CHIPS, axis=2),
    )

    small_params = dict(norm1_g=(norm1_g, m_norm1_g, v_norm1_g), w_fg2=(w_fg2, m_w_fg2, v_w_fg2),
                        b_fg=(b_fg, m_b_fg, v_b_fg), gla_norm_g=(gla_norm_g, m_gla_norm_g, v_gla_norm_g),
                        conv_w=(conv_w, m_conv_w, v_conv_w), conv_b=(conv_b, m_conv_b, v_conv_b),
                        norm2_g=(norm2_g, m_norm2_g, v_norm2_g), final_g=(final_g, m_final_g, v_final_g))
    order = ["norm1_g", "w_in", "w_fg2", "b_fg", "gla_norm_g", "w_oa", "conv_w", "conv_b", "w_ob", "w_o",
             "norm2_g", "w_ffn_gate", "w_ffn_up", "w_ffn_down", "final_g"]
    results = {}

    def update_large(n):
        w_, m_, v_ = views[n]
        outs = _adamw_layers(w_, (reduced0[n][0], reduced1[n][0]), (reduced0[n][1], reduced1[n][1]), on_core0,
                             m_, v_, name="adamw_" + n)
        back = from_view.get(n)
        results[n] = tuple(back(r) for r in outs) if back else outs
        return outs[1]

    for n, (w_, m_, v_) in small_params.items():
        results[n] = (sg[n],) + _adamw(w_, sg[n], m_, v_, name="adamw_" + n)
    updated = tuple(update_large(n) for n in rest_names)
    share_p = total_and_share(0, swap_p, updated, proj_names, "p0")
    reduced0.update(shared(0, share_p, share_p[3], proj_names, "p0"))
    update_large("w_in")
    return (loss, grad_x[None], *[results[n][0] for n in order], *[results[n][1] for n in order],
            *[results[n][2] for n in order], *[results[n][3] for n in order])
```

```python
import functools

import jax
import jax.numpy as jnp
from jax import lax
from jax.experimental import pallas as pl
from jax.experimental.pallas import tpu as pltpu

F32 = jnp.float32
BF16 = jnp.bfloat16

D_MODEL = 1024
DEPTH = 2
CHUNK = 64
GLA_HEADS = 4
KEY_WIDTH = 512
VAL_WIDTH = 1024
HEAD_K = 128
HEAD_V = 256
GATE_RANK = 16
GATE_TAU = 16.0
CONV_CH = 1024
FFN_HIDDEN = 2816
IN_WIDTH = 8208
NORM_EPS = 1e-6
Q_SCALE = HEAD_K ** -0.5
ADAM_LR = 0.001
ADAM_B1 = 0.9
ADAM_B2 = 0.999
ADAM_EPS = 1e-08
ADAM_WD = 0.01
ADAM_STEP = 10

N_CHIPS = 4
N_DEV = 8

LANE = 128
FZ_PAD = LANE
PROJ_W = 8192 + FZ_PAD
COL_Q, COL_K, COL_V, COL_R, COL_GBI, COL_GCI, COL_CX, COL_GA, COL_GB, COL_FZ = (
    0, 512, 1024, 2048, 3072, 4096, 5120, 6144, 7168, 8192)
FZ_ORIG = 3072

VMEM_LIMIT = 52 * 1024 * 1024
HALO = 16


def _cparams(sem=None):
    return pltpu.CompilerParams(dimension_semantics=sem, vmem_limit_bytes=VMEM_LIMIT)


def _sigmoid(x):
    return jax.nn.sigmoid(x)


def _log_sigmoid(x):
    return jnp.minimum(x, 0.0) - jnp.log1p(jnp.exp(-jnp.abs(x)))


_DOT_DIMS = {
    "nn": (((1,), (0,)), ((), ())),
    "nt": (((1,), (1,)), ((), ())),
    "tn": (((0,), (0,)), ((), ())),
}


def _matmul(a, b, *, mode, out_dtype, tm, tn, tk, name, add=None, dep=None):
    if mode == "nn":
        (m, k), n = a.shape, b.shape[1]
    elif mode == "nt":
        (m, k), n = a.shape, b.shape[0]
    else:
        (k, m), n = a.shape, b.shape[1]
    tm, tn, tk = min(tm, m), min(tn, n), min(tk, k)
    assert m % tm == 0 and n % tn == 0 and k % tk == 0, (name, m, n, k, tm, tn, tk)
    nk = k // tk
    has_add = add is not None
    has_dep = dep is not None

    def body(*refs):
        if has_dep:
            refs = refs[1:]
        if has_add:
            a_ref, b_ref, add_ref, o_ref = refs[:4]
            scratch = refs[4:]
        else:
            a_ref, b_ref, o_ref = refs[:3]
            add_ref = None
            scratch = refs[3:]
        part = lax.dot_general(a_ref[...], b_ref[...], _DOT_DIMS[mode], preferred_element_type=F32)

        def finish(acc):
            if add_ref is not None:
                acc = acc + add_ref[...].astype(F32)
            o_ref[...] = acc.astype(o_ref.dtype)

        if nk == 1:
            finish(part)
        else:
            acc_ref = scratch[0]
            kk = pl.program_id(2)

            @pl.when(kk == 0)
            def _():
                acc_ref[...] = part

            @pl.when(kk > 0)
            def _():
                acc_ref[...] += part

            @pl.when(kk == nk - 1)
            def _():
                finish(acc_ref[...])

    if mode == "nn":
        a_spec = pl.BlockSpec((tm, tk), lambda i, j, kk: (i, kk))
        b_spec = pl.BlockSpec((tk, tn), lambda i, j, kk: (kk, j))
    elif mode == "nt":
        a_spec = pl.BlockSpec((tm, tk), lambda i, j, kk: (i, kk))
        b_spec = pl.BlockSpec((tn, tk), lambda i, j, kk: (j, kk))
    else:
        a_spec = pl.BlockSpec((tk, tm), lambda i, j, kk: (kk, i))
        b_spec = pl.BlockSpec((tk, tn), lambda i, j, kk: (kk, j))
    o_spec = pl.BlockSpec((tm, tn), lambda i, j, kk: (i, j))
    in_specs = [a_spec, b_spec] + ([o_spec] if has_add else [])
    operands = (a, b) + ((add,) if has_add else ())
    if has_dep:
        in_specs = [pl.BlockSpec(dep.shape, lambda i, j, kk: (0, 0))] + in_specs
        operands = (dep,) + operands
    return pl.pallas_call(
        body,
        name=name,
        out_shape=jax.ShapeDtypeStruct((m, n), out_dtype),
        grid=(m // tm, n // tn, nk),
        in_specs=in_specs,
        out_specs=o_spec,
        scratch_shapes=[pltpu.VMEM((tm, tn), F32)] if nk > 1 else [],
        compiler_params=_cparams(("parallel", "parallel", "arbitrary")),
    )(*operands)


def _matmul_norm_bwd(a, b, x, g, dres, *, tm, tk, name, add=None, dep=None):
    (m, k), d = a.shape, b.shape[1]
    tm, tk = min(tm, m), min(tk, k)
    assert m % tm == 0 and k % tk == 0 and x.shape == (m, d), (name, m, k, d, tm, tk)
    nk = k // tk
    has_add = add is not None
    has_dep = dep is not None

    def body(*refs):
        refs = refs[has_dep:]
        a_ref, b_ref = refs[:2]
        add_ref = refs[2] if has_add else None
        x_ref, g_ref, dres_ref, dx_ref, dxb_ref, dg_ref = refs[2 + has_add:8 + has_add]
        scratch = refs[8 + has_add:]
        i = pl.program_id(0)
        part = jnp.dot(a_ref[...], b_ref[...], preferred_element_type=F32)

        def finish(dh):
            if add_ref is not None:
                dh = dh + add_ref[...].astype(F32)
            xv = x_ref[...]
            rs = lax.rsqrt(jnp.mean(xv * xv, axis=-1, keepdims=True) + NORM_EPS)
            nrm = xv * rs
            dn = dh * g_ref[...]
            dx = dres_ref[...] + rs * (dn - nrm * jnp.mean(dn * nrm, axis=-1, keepdims=True))
            dx_ref[...] = dx
            dxb_ref[...] = dx.astype(BF16)
            dg_part = jnp.sum(dh * nrm, axis=0, keepdims=True)

            @pl.when(i == 0)
            def _():
                dg_ref[...] = dg_part

            @pl.when(i > 0)
            def _():
                dg_ref[...] += dg_part

        if nk == 1:
            finish(part)
        else:
            acc_ref = scratch[0]
            kk = pl.program_id(1)

            @pl.when(kk == 0)
            def _():
                acc_ref[...] = part

            @pl.when(kk > 0)
            def _():
                acc_ref[...] += part

            @pl.when(kk == nk - 1)
            def _():
                finish(acc_ref[...])

    row = pl.BlockSpec((tm, d), lambda i, kk: (i, 0))
    vec = pl.BlockSpec((1, d), lambda i, kk: (0, 0))
    in_specs = [pl.BlockSpec((tm, tk), lambda i, kk: (i, kk)), pl.BlockSpec((tk, d), lambda i, kk: (kk, 0))]
    operands = [a, b]
    if has_add:
        in_specs.append(row)
        operands.append(add)
    in_specs += [row, vec, row]
    operands += [x, g.reshape(1, d), dres]
    if has_dep:
        in_specs = [pl.BlockSpec(dep.shape, lambda i, kk: (0, 0))] + in_specs
        operands = [dep] + operands
    return pl.pallas_call(
        body, name=name,
        out_shape=(jax.ShapeDtypeStruct((m, d), F32), jax.ShapeDtypeStruct((m, d), BF16),
                   jax.ShapeDtypeStruct((1, d), F32)),
        grid=(m // tm, nk),
        in_specs=in_specs,
        out_specs=(row, row, vec),
        scratch_shapes=[pltpu.VMEM((tm, d), F32)] if nk > 1 else [],
        compiler_params=_cparams(("arbitrary", "arbitrary")),
    )(*operands)


def _rms_fwd(x, g, *, name):
    s, d = x.shape
    tr = min(512, s)

    def body(x_ref, g_ref, h_ref):
        xv = x_ref[...]
        rs = lax.rsqrt(jnp.mean(xv * xv, axis=-1, keepdims=True) + NORM_EPS)
        h_ref[...] = (xv * rs * g_ref[...]).astype(BF16)

    return pl.pallas_call(
        body, name=name,
        out_shape=jax.ShapeDtypeStruct((s, d), BF16),
        grid=(s // tr,),
        in_specs=[pl.BlockSpec((tr, d), lambda i: (i, 0)), pl.BlockSpec((1, d), lambda i: (0, 0))],
        out_specs=pl.BlockSpec((tr, d), lambda i: (i, 0)),
        compiler_params=_cparams(("parallel",)),
    )(x, g.reshape(1, d))


def _loss_head(x, g, target):
    s, d = x.shape
    tr = min(512, s)

    def body(x_ref, g_ref, t_ref, loss_ref, dx_ref, dxb_ref, dg_ref):
        i = pl.program_id(0)
        xv = x_ref[...]
        rs = lax.rsqrt(jnp.mean(xv * xv, axis=-1, keepdims=True) + NORM_EPS)
        n = xv * rs
        gv = g_ref[...]
        err = n * gv - t_ref[...]
        row_loss = jnp.mean(err * err, axis=-1, keepdims=True)
        loss_part = 0.5 * jnp.sum(row_loss, axis=0, keepdims=True)
        dy = err * (1.0 / d)
        dn = dy * gv
        dx = rs * (dn - n * jnp.mean(dn * n, axis=-1, keepdims=True))
        dx_ref[...] = dx
        dxb_ref[...] = dx.astype(BF16)
        dg_part = jnp.sum(dy * n, axis=0, keepdims=True)

        @pl.when(i == 0)
        def _():
            loss_ref[...] = loss_part
            dg_ref[...] = dg_part

        @pl.when(i > 0)
        def _():
            loss_ref[...] += loss_part
            dg_ref[...] += dg_part

    row = pl.BlockSpec((tr, d), lambda i: (i, 0))
    vec = pl.BlockSpec((1, d), lambda i: (0, 0))
    one = pl.BlockSpec((1, 1), lambda i: (0, 0))
    return pl.pallas_call(
        body, name="loss_head",
        out_shape=(jax.ShapeDtypeStruct((1, 1), F32), jax.ShapeDtypeStruct((s, d), F32),
                   jax.ShapeDtypeStruct((s, d), BF16), jax.ShapeDtypeStruct((1, d), F32)),
        grid=(s // tr,),
        in_specs=[row, vec, row],
        out_specs=(one, row, row, vec),
        compiler_params=_cparams(("arbitrary",)),
    )(x, g.reshape(1, d), target)


def _conv_taps(u_prev, u, w_ref, rows):
    ext = jnp.concatenate([u_prev, u], axis=0)
    u1 = pltpu.roll(ext, 1, 0)[HALO:HALO + rows]
    u2 = pltpu.roll(ext, 2, 0)[HALO:HALO + rows]
    conv = w_ref[0:1, :] * u2 + w_ref[1:2, :] * u1 + w_ref[2:3, :] * u + w_ref[3:4, :]
    return conv, u1, u2


def _conv_fwd(proj, conv_wb, *, s):
    tr = min(512, s)
    c = CONV_CH
    hb = tr // HALO

    def body(gbi_ref, gci_ref, cx_ref, gci_h_ref, cx_h_ref, w_ref, o_ref):
        i = pl.program_id(0)
        u = gci_ref[...].astype(F32) * cx_ref[...].astype(F32)
        u_prev = gci_h_ref[...].astype(F32) * cx_h_ref[...].astype(F32)
        u_prev = jnp.where(i == 0, 0.0, u_prev)
        conv, _, _ = _conv_taps(u_prev, u, w_ref, tr)
        o_ref[...] = (gbi_ref[...].astype(F32) * conv).astype(BF16)

    def seg(col):
        return pl.BlockSpec((tr, c), lambda i: (i, col // c))

    def halo(col):
        return pl.BlockSpec((HALO, c), lambda i: (jnp.maximum(i * hb - 1, 0), col // c))

    return pl.pallas_call(
        body, name="conv_fwd",
        out_shape=jax.ShapeDtypeStruct((s, c), BF16),
        grid=(s // tr,),
        in_specs=[seg(COL_GBI), seg(COL_GCI), seg(COL_CX), halo(COL_GCI), halo(COL_CX),
                  pl.BlockSpec((8, c), lambda i: (0, 0))],
        out_specs=pl.BlockSpec((tr, c), lambda i: (i, 0)),
        compiler_params=_cparams(("parallel",)),
    )(proj, proj, proj, proj, proj, conv_wb)


def _conv_bwd(proj, dcb, conv_wb, dproj, *, s):
    tr = min(512, s)
    c = CONV_CH
    hb = tr // HALO
    nb = s // tr

    def body(gbi_ref, gci_ref, cx_ref, dcb_ref, gci_h_ref, cx_h_ref, gbi_n_ref, dcb_n_ref, w_ref,
             dproj_in_ref, dp_ref, dw_ref):
        i = pl.program_id(0)
        gbi = gbi_ref[...].astype(F32)
        gci = gci_ref[...].astype(F32)
        cx = cx_ref[...].astype(F32)
        dcb_v = dcb_ref[...].astype(F32)
        u = gci * cx
        u_prev = jnp.where(i == 0, 0.0, gci_h_ref[...].astype(F32) * cx_h_ref[...].astype(F32))
        conv, u1, u2 = _conv_taps(u_prev, u, w_ref, tr)
        dconv = dcb_v * gbi
        dconv_next = jnp.where(i == nb - 1, 0.0, dcb_n_ref[...].astype(F32) * gbi_n_ref[...].astype(F32))
        ext = jnp.concatenate([dconv, dconv_next], axis=0)
        n_ext = tr + HALO
        d1 = pltpu.roll(ext, n_ext - 1, 0)[0:tr]
        d2 = pltpu.roll(ext, n_ext - 2, 0)[0:tr]
        du = w_ref[2:3, :] * dconv + w_ref[1:2, :] * d1 + w_ref[0:1, :] * d2
        dp_ref[:, 0:c] = (dcb_v * conv).astype(BF16)
        dp_ref[:, c:2 * c] = (du * cx).astype(BF16)
        dp_ref[:, 2 * c:3 * c] = (du * gci).astype(BF16)
        part = jnp.concatenate([
            jnp.sum(dconv * u2, axis=0, keepdims=True),
            jnp.sum(dconv * u1, axis=0, keepdims=True),
            jnp.sum(dconv * u, axis=0, keepdims=True),
            jnp.sum(dconv, axis=0, keepdims=True),
            jnp.zeros((4, c), F32)], axis=0)

        @pl.when(i == 0)
        def _():
            dw_ref[...] = part

        @pl.when(i > 0)
        def _():
            dw_ref[...] += part

    def seg(col):
        return pl.BlockSpec((tr, c), lambda i: (i, col // c))

    def halo_prev(col):
        return pl.BlockSpec((HALO, c), lambda i: (jnp.maximum(i * hb - 1, 0), col // c))

    def halo_next(col):
        return pl.BlockSpec((HALO, c), lambda i: (jnp.minimum((i + 1) * hb, nb * hb - 1), col // c))

    return pl.pallas_call(
        body, name="conv_bwd",
        out_shape=(jax.ShapeDtypeStruct((s, PROJ_W), BF16), jax.ShapeDtypeStruct((8, c), F32)),
        grid=(nb,),
        in_specs=[seg(COL_GBI), seg(COL_GCI), seg(COL_CX), pl.BlockSpec((tr, c), lambda i: (i, 0)),
                  halo_prev(COL_GCI), halo_prev(COL_CX), halo_next(COL_GBI),
                  pl.BlockSpec((HALO, c), lambda i: (jnp.minimum((i + 1) * hb, nb * hb - 1), 0)),
                  pl.BlockSpec((8, c), lambda i: (0, 0)), ANY],
        out_specs=(pl.BlockSpec((tr, 3 * c), lambda i: (i, COL_GBI // (3 * c))),
                   pl.BlockSpec((8, c), lambda i: (0, 0))),
        input_output_aliases={9: 0},
        compiler_params=_cparams(("arbitrary",)),
    )(proj, proj, proj, dcb, proj, proj, proj, dcb, conv_wb, dproj)


def _token_spec(dep):
    return ([], []) if dep is None else ([dep], [pl.BlockSpec(dep.shape, lambda i: (0, 0))])


def _mixer_out_fwd(oa, cb_in, proj, x0, w_oa, w_ob, w_o, *, s, dep=None):
    tr = min(512, s)
    d = D_MODEL
    n_dep = dep is not None

    def body(*refs):
        oa_ref, cb_ref, ga_ref, gb_ref, x0_ref, woa_ref, wob_ref, wo_ref, ya_ref, yb_ref, mix_ref, x1_ref = (
            refs[n_dep:])
        ya = jnp.dot(oa_ref[...], woa_ref[...], preferred_element_type=F32).astype(BF16)
        yb = jnp.dot(cb_ref[...], wob_ref[...], preferred_element_type=F32).astype(BF16)
        ya_ref[...] = ya
        yb_ref[...] = yb
        sa = _sigmoid(ga_ref[...].astype(F32))
        sb = _sigmoid(gb_ref[...].astype(F32))
        mix = (sa * ya.astype(F32) + sb * yb.astype(F32)).astype(BF16)
        mix_ref[...] = mix
        x1_ref[...] = x0_ref[...] + jnp.dot(mix, wo_ref[...], preferred_element_type=F32)

    row = pl.BlockSpec((tr, d), lambda i: (i, 0))
    full = pl.BlockSpec((d, d), lambda i: (0, 0))
    tok, tok_specs = _token_spec(dep)
    return pl.pallas_call(
        body, name="mixer_out_fwd",
        out_shape=(jax.ShapeDtypeStruct((s, d), BF16), jax.ShapeDtypeStruct((s, d), BF16),
                   jax.ShapeDtypeStruct((s, d), BF16), jax.ShapeDtypeStruct((s, d), F32)),
        grid=(s // tr,),
        in_specs=tok_specs + [row, row, pl.BlockSpec((tr, d), lambda i: (i, COL_GA // d)),
                              pl.BlockSpec((tr, d), lambda i: (i, COL_GB // d)), row, full, full, full],
        out_specs=(row, row, row, row),
        compiler_params=_cparams(("parallel",)),
    )(*tok, oa, cb_in, proj, proj, x0, w_oa, w_ob, w_o)


def _mixer_out_bwd(dx1b, proj, ya, yb, w_oa, w_ob, w_o, *, s, dep=None):
    tr = min(512, s)
    d = D_MODEL
    n_dep = dep is not None

    def body(*refs):
        (dx_ref, ga_ref, gb_ref, ya_ref, yb_ref, woa_ref, wob_ref, wo_ref,
         dya_ref, dyb_ref, dcb_ref, doa_ref, dg_ref) = refs[n_dep:]
        dm = lax.dot_general(dx_ref[...], wo_ref[...], _DOT_DIMS["nt"], preferred_element_type=F32)
        dm = dm.astype(BF16).astype(F32)
        sa = _sigmoid(ga_ref[...].astype(F32))
        sb = _sigmoid(gb_ref[...].astype(F32))
        dya = (dm * sa).astype(BF16)
        dyb = (dm * sb).astype(BF16)
        dya_ref[...] = dya
        dyb_ref[...] = dyb
        dg_ref[:, 0:d] = (dm * ya_ref[...].astype(F32) * sa * (1.0 - sa)).astype(BF16)
        dg_ref[:, d:2 * d] = (dm * yb_ref[...].astype(F32) * sb * (1.0 - sb)).astype(BF16)
        dcb_ref[...] = lax.dot_general(dyb, wob_ref[...], _DOT_DIMS["nt"],
                                       preferred_element_type=F32).astype(BF16)
        doa_ref[...] = lax.dot_general(dya, woa_ref[...], _DOT_DIMS["nt"],
                                       preferred_element_type=F32).astype(BF16)

    row = pl.BlockSpec((tr, d), lambda i: (i, 0))
    full = pl.BlockSpec((d, d), lambda i: (0, 0))
    tok, tok_specs = _token_spec(dep)
    return pl.pallas_call(
        body, name="mixer_out_bwd",
        out_shape=tuple(jax.ShapeDtypeStruct((s, d), BF16) for _ in range(4))
        + (jax.ShapeDtypeStruct((s, PROJ_W), BF16),),
        grid=(s // tr,),
        in_specs=tok_specs + [row, pl.BlockSpec((tr, d), lambda i: (i, COL_GA // d)),
                              pl.BlockSpec((tr, d), lambda i: (i, COL_GB // d)), row, row, full, full, full],
        out_specs=(row, row, row, row, pl.BlockSpec((tr, 2 * d), lambda i: (i, COL_GA // (2 * d)))),
        compiler_params=_cparams(("parallel",)),
    )(*tok, dx1b, proj, proj, ya, yb, w_oa, w_ob, w_o)


def _swiglu_fwd(gt, up, *, s):
    tr = min(256, s)
    f = FFN_HIDDEN

    def body(gt_ref, up_ref, o_ref):
        gv = gt_ref[...].astype(F32)
        o_ref[...] = (gv * _sigmoid(gv) * up_ref[...].astype(F32)).astype(BF16)

    row = pl.BlockSpec((tr, f), lambda i: (i, 0))
    return pl.pallas_call(
        body, name="swiglu_fwd",
        out_shape=jax.ShapeDtypeStruct((s, f), BF16),
        grid=(s // tr,),
        in_specs=[row, row], out_specs=row,
        compiler_params=_cparams(("parallel",)),
    )(gt, up)


def _swiglu_bwd(gt, up, dhid, *, s):
    tr = min(256, s)
    f = FFN_HIDDEN

    def body(gt_ref, up_ref, dh_ref, dgt_ref, dup_ref):
        gv = gt_ref[...].astype(F32)
        uv = up_ref[...].astype(F32)
        dh = dh_ref[...].astype(F32)
        sg = _sigmoid(gv)
        dgt_ref[...] = (dh * uv * sg * (1.0 + gv * (1.0 - sg))).astype(BF16)
        dup_ref[...] = (dh * gv * sg).astype(BF16)

    row = pl.BlockSpec((tr, f), lambda i: (i, 0))
    return pl.pallas_call(
        body, name="swiglu_bwd",
        out_shape=(jax.ShapeDtypeStruct((s, f), BF16), jax.ShapeDtypeStruct((s, f), BF16)),
        grid=(s // tr,),
        in_specs=[row, row, row], out_specs=(row, row),
        compiler_params=_cparams(("parallel",)),
    )(gt, up, dhid)


def _tri(strict):
    r = lax.broadcasted_iota(jnp.int32, (CHUNK, CHUNK), 0)
    c = lax.broadcasted_iota(jnp.int32, (CHUNK, CHUNK), 1)
    return jnp.where((c < r) if strict else (c <= r), 1.0, 0.0).astype(F32)


def _gla_gate_terms(la_c, tri):
    cum = jnp.dot(tri, la_c, precision=lax.Precision.HIGHEST, preferred_element_type=F32)
    cend = cum[CHUNK - 1:CHUNK, :]
    return jnp.exp(cend - cum), jnp.exp(cend)


def _hk(h):
    return slice(h * HEAD_K, (h + 1) * HEAD_K)


def _hv(h):
    return slice(h * HEAD_V, (h + 1) * HEAD_V)


def _gla_fwd(proj, wfg2p, bfg, gn, *, s):
    tr = min(512, s)
    nb = s // tr
    nc = tr // CHUNK

    def body(q_ref, k_ref, v_ref, r_ref, fz_ref, w_ref, b_ref, gn_ref, oa_ref, st_ref,
             state, la_scr, o_scr):
        i = pl.program_id(0)

        @pl.when(i == 0)
        def _():
            state[...] = jnp.zeros_like(state)

        fg = jnp.dot(fz_ref[...], w_ref[...], preferred_element_type=F32) + b_ref[...]
        la_scr[...] = _log_sigmoid(fg) * (1.0 / GATE_TAU)
        tri = _tri(False)

        def chunk(ci, carry):
            r0 = pl.multiple_of(ci * CHUNK, CHUNK)
            rows = pl.ds(r0, CHUNK)
            e, gam = _gla_gate_terms(la_scr[rows, :], tri)
            kd = (k_ref[rows, :].astype(F32) * e).astype(BF16)
            qs = (q_ref[rows, :].astype(F32) * Q_SCALE).astype(BF16)
            v_c = v_ref[rows, :]
            for h in range(GLA_HEADS):
                upd = lax.dot_general(v_c[:, _hv(h)], kd[:, _hk(h)], _DOT_DIMS["tn"],
                                      preferred_element_type=F32)
                st_h = state[:, _hk(h)] * gam[:, _hk(h)] + upd
                state[:, _hk(h)] = st_h
                o_scr[rows, _hv(h)] = lax.dot_general(qs[:, _hk(h)], st_h.astype(BF16), _DOT_DIMS["nt"],
                                                       preferred_element_type=F32)
            st_ref[ci] = state[...]
            return carry

        lax.fori_loop(0, nc, chunk, 0, unroll=True)
        for h in range(GLA_HEADS):
            o = o_scr[:, _hv(h)]
            rs = lax.rsqrt(jnp.mean(o * o, axis=-1, keepdims=True) + NORM_EPS)
            rv = r_ref[:, _hv(h)].astype(F32)
            oa_ref[:, _hv(h)] = ((o * rs * gn_ref[...]).astype(F32) * (rv * _sigmoid(rv))).astype(BF16)

    return pl.pallas_call(
        body, name="gla_fwd",
        out_shape=(jax.ShapeDtypeStruct((s, VAL_WIDTH), BF16),
                   jax.ShapeDtypeStruct((s // CHUNK, HEAD_V, KEY_WIDTH), F32)),
        grid=(nb,),
        in_specs=[pl.BlockSpec((tr, KEY_WIDTH), lambda i: (i, COL_Q // KEY_WIDTH)),
                  pl.BlockSpec((tr, KEY_WIDTH), lambda i: (i, COL_K // KEY_WIDTH)),
                  pl.BlockSpec((tr, VAL_WIDTH), lambda i: (i, COL_V // VAL_WIDTH)),
                  pl.BlockSpec((tr, VAL_WIDTH), lambda i: (i, COL_R // VAL_WIDTH)),
                  pl.BlockSpec((tr, FZ_PAD), lambda i: (i, COL_FZ // FZ_PAD)),
                  pl.BlockSpec((FZ_PAD, KEY_WIDTH), lambda i: (0, 0)),
                  pl.BlockSpec((1, KEY_WIDTH), lambda i: (0, 0)),
                  pl.BlockSpec((1, HEAD_V), lambda i: (0, 0))],
        out_specs=(pl.BlockSpec((tr, VAL_WIDTH), lambda i: (i, 0)),
                   pl.BlockSpec((nc, HEAD_V, KEY_WIDTH), lambda i: (i, 0, 0))),
        scratch_shapes=[pltpu.VMEM((HEAD_V, KEY_WIDTH), F32), pltpu.VMEM((tr, KEY_WIDTH), F32),
                        pltpu.VMEM((tr, VAL_WIDTH), F32)],
        compiler_params=_cparams(("arbitrary",)),
    )(proj, proj, proj, proj, proj, wfg2p, bfg, gn)


def _gla_bwd(proj, doa, states, wfg2p, bfg, gn, dproj, *, s):
    tr = min(512, s)
    nb = s // tr
    nc = tr // CHUNK

    def body(q_ref, k_ref, v_ref, r_ref, fz_ref, doa_ref, st_ref, stp_ref, w_ref, b_ref, gn_ref, dproj_in_ref,
             dp_ref, dfz_ref, dgn_ref, dw_ref, db_ref,
             carry, fg_scr, la_scr, dla_scr, o_scr, do_scr):
        i = pl.program_id(0)

        @pl.when(i == 0)
        def _():
            carry[...] = jnp.zeros_like(carry)
            dgn_ref[...] = jnp.zeros_like(dgn_ref)
            dw_ref[...] = jnp.zeros_like(dw_ref)
            db_ref[...] = jnp.zeros_like(db_ref)

        fz = fz_ref[...]
        fg = jnp.dot(fz, w_ref[...], preferred_element_type=F32) + b_ref[...]
        fg_scr[...] = fg
        la_scr[...] = _log_sigmoid(fg) * (1.0 / GATE_TAU)

        def fwd_chunk(ci, c0):
            r0 = pl.multiple_of(ci * CHUNK, CHUNK)
            rows = pl.ds(r0, CHUNK)
            qs = (q_ref[rows, :].astype(F32) * Q_SCALE).astype(BF16)
            st = st_ref[ci].astype(BF16)
            for h in range(GLA_HEADS):
                o_scr[rows, _hv(h)] = lax.dot_general(qs[:, _hk(h)], st[:, _hk(h)], _DOT_DIMS["nt"],
                                                       preferred_element_type=F32)
            return c0

        lax.fori_loop(0, nc, fwd_chunk, 0, unroll=True)

        gnv = gn_ref[...]
        dgn_part = jnp.zeros((1, HEAD_V), F32)
        for h in range(GLA_HEADS):
            o = o_scr[:, _hv(h)]
            rs = lax.rsqrt(jnp.mean(o * o, axis=-1, keepdims=True) + NORM_EPS)
            nrm = o * rs
            rv = r_ref[:, _hv(h)].astype(F32)
            sg = _sigmoid(rv)
            doa_h = doa_ref[:, _hv(h)].astype(F32)
            don = doa_h * (rv * sg)
            dp_ref[:, COL_R + h * HEAD_V:COL_R + (h + 1) * HEAD_V] = (
                doa_h * (nrm * gnv) * (sg * (1.0 + rv * (1.0 - sg)))).astype(BF16)
            dgn_part = dgn_part + jnp.sum(don * nrm, axis=0, keepdims=True)
            dn = don * gnv
            do_scr[:, _hv(h)] = (rs * (dn - nrm * jnp.mean(dn * nrm, axis=-1, keepdims=True))).astype(BF16)
        dgn_ref[...] += dgn_part

        tri = _tri(False)
        tri_s = _tri(True)
        first_block = i == nb - 1

        def bwd_chunk(cc, c0):
            ci = nc - 1 - cc
            r0 = pl.multiple_of(ci * CHUNK, CHUNK)
            rows = pl.ds(r0, CHUNK)
            e, gam = _gla_gate_terms(la_scr[rows, :], tri)
            k_c = k_ref[rows, :].astype(F32)
            kd = k_c * e
            kd_b = kd.astype(BF16)
            qs = (q_ref[rows, :].astype(F32) * Q_SCALE).astype(BF16)
            v_c = v_ref[rows, :]
            do_c = do_scr[rows, :]
            st = st_ref[ci]
            st_b = st.astype(BF16)
            st_prev_in = st_ref[jnp.maximum(ci - 1, 0)]
            st_prev_edge = jnp.where(first_block, 0.0, stp_ref[0])
            st_prev = jnp.where(ci > 0, st_prev_in, st_prev_edge)
            dkd_parts = []
            dgam_parts = []
            for h in range(GLA_HEADS):
                dst = lax.dot_general(do_c[:, _hv(h)], qs[:, _hk(h)], _DOT_DIMS["tn"],
                                      preferred_element_type=F32) + carry[:, _hk(h)]
                dst_b = dst.astype(BF16)
                dqs = jnp.dot(do_c[:, _hv(h)], st_b[:, _hk(h)], preferred_element_type=F32)
                dp_ref[rows, COL_Q + h * HEAD_K:COL_Q + (h + 1) * HEAD_K] = (dqs * Q_SCALE).astype(BF16)
                dkd_parts.append(jnp.dot(v_c[:, _hv(h)], dst_b, preferred_element_type=F32))
                dp_ref[rows, COL_V + h * HEAD_V:COL_V + (h + 1) * HEAD_V] = lax.dot_general(
                    kd_b[:, _hk(h)], dst_b, _DOT_DIMS["nt"], preferred_element_type=F32).astype(BF16)
                dgam_parts.append(jnp.sum(dst * st_prev[:, _hk(h)], axis=0, keepdims=True))
                carry[:, _hk(h)] = dst * gam[:, _hk(h)]
            dkd = jnp.concatenate(dkd_parts, axis=1)
            dgam = jnp.concatenate(dgam_parts, axis=1)
            dp_ref[rows, COL_K:COL_K + KEY_WIDTH] = (dkd * e).astype(BF16)
            dz = dkd * kd
            dla_scr[rows, :] = gam * dgam + jnp.dot(tri_s, dz, precision=lax.Precision.HIGHEST,
                                                    preferred_element_type=F32)
            return c0

        lax.fori_loop(0, nc, bwd_chunk, 0, unroll=True)

        dfg = dla_scr[...] * (1.0 / GATE_TAU) * _sigmoid(-fg_scr[...])
        dfg_b = dfg.astype(BF16)
        dfz_ref[...] = lax.dot_general(dfg_b, w_ref[...], _DOT_DIMS["nt"],
                                       preferred_element_type=F32).astype(BF16)
        dw_ref[...] += lax.dot_general(fz, dfg_b, _DOT_DIMS["tn"], preferred_element_type=F32)
        db_ref[...] += jnp.sum(dfg, axis=0, keepdims=True)

    def rev(i):
        return nb - 1 - i

    return pl.pallas_call(
        body, name="gla_bwd",
        out_shape=(jax.ShapeDtypeStruct((s, PROJ_W), BF16),
                   jax.ShapeDtypeStruct((s, FZ_PAD), BF16),
                   jax.ShapeDtypeStruct((1, HEAD_V), F32),
                   jax.ShapeDtypeStruct((FZ_PAD, KEY_WIDTH), F32),
                   jax.ShapeDtypeStruct((1, KEY_WIDTH), F32)),
        grid=(nb,),
        in_specs=[pl.BlockSpec((tr, KEY_WIDTH), lambda i: (rev(i), COL_Q // KEY_WIDTH)),
                  pl.BlockSpec((tr, KEY_WIDTH), lambda i: (rev(i), COL_K // KEY_WIDTH)),
                  pl.BlockSpec((tr, VAL_WIDTH), lambda i: (rev(i), COL_V // VAL_WIDTH)),
                  pl.BlockSpec((tr, VAL_WIDTH), lambda i: (rev(i), COL_R // VAL_WIDTH)),
                  pl.BlockSpec((tr, FZ_PAD), lambda i: (rev(i), COL_FZ // FZ_PAD)),
                  pl.BlockSpec((tr, VAL_WIDTH), lambda i: (rev(i), 0)),
                  pl.BlockSpec((nc, HEAD_V, KEY_WIDTH), lambda i: (rev(i), 0, 0)),
                  pl.BlockSpec((1, HEAD_V, KEY_WIDTH), lambda i: (jnp.maximum(rev(i) * nc - 1, 0), 0, 0)),
                  pl.BlockSpec((FZ_PAD, KEY_WIDTH), lambda i: (0, 0)),
                  pl.BlockSpec((1, KEY_WIDTH), lambda i: (0, 0)),
                  pl.BlockSpec((1, HEAD_V), lambda i: (0, 0)), ANY],
        out_specs=(pl.BlockSpec((tr, 3 * VAL_WIDTH), lambda i: (rev(i), 0)),
                   pl.BlockSpec((tr, FZ_PAD), lambda i: (rev(i), 0)),
                   pl.BlockSpec((1, HEAD_V), lambda i: (0, 0)),
                   pl.BlockSpec((FZ_PAD, KEY_WIDTH), lambda i: (0, 0)),
                   pl.BlockSpec((1, KEY_WIDTH), lambda i: (0, 0))),
        scratch_shapes=[pltpu.VMEM((HEAD_V, KEY_WIDTH), F32),
                        pltpu.VMEM((tr, KEY_WIDTH), F32),
                        pltpu.VMEM((tr, KEY_WIDTH), F32),
                        pltpu.VMEM((tr, KEY_WIDTH), F32),
                        pltpu.VMEM((tr, VAL_WIDTH), F32),
                        pltpu.VMEM((tr, VAL_WIDTH), BF16)],
        input_output_aliases={11: 0},
        compiler_params=_cparams(("arbitrary",)),
    )(proj, proj, proj, proj, proj, doa, states, states, wfg2p, bfg, gn, dproj)


def _put_fz(dproj, dfz, *, s):
    tr = min(512, s)

    def body(dfz_ref, dproj_in_ref, o_ref):
        o_ref[...] = dfz_ref[...]

    return pl.pallas_call(
        body, name="put_dfz",
        out_shape=jax.ShapeDtypeStruct((s, PROJ_W), BF16),
        grid=(s // tr,),
        in_specs=[pl.BlockSpec((tr, FZ_PAD), lambda i: (i, 0)), ANY],
        out_specs=pl.BlockSpec((tr, FZ_PAD), lambda i: (i, COL_FZ // FZ_PAD)),
        input_output_aliases={1: 0},
        compiler_params=_cparams(("parallel",)),
    )(dfz, dproj)


_ADAM_C1 = 1.0 / (1.0 - ADAM_B1 ** ADAM_STEP)
_ADAM_C2 = 1.0 / (1.0 - ADAM_B2 ** ADAM_STEP)


def _adamw_math(wv, gv, mv, vv):
    nm = ADAM_B1 * mv + (1.0 - ADAM_B1) * gv
    nv = ADAM_B2 * vv + (1.0 - ADAM_B2) * (gv * gv)
    delta = -ADAM_LR * ((nm * _ADAM_C1) / (jnp.sqrt(nv * _ADAM_C2) + ADAM_EPS) + ADAM_WD * wv)
    return delta, nm, nv


def _adamw(w, g, m, v, *, name):
    shape = w.shape
    if w.ndim == 1:
        w, g, m, v = (t.reshape(1, 1, -1) for t in (w, g, m, v))
    elif w.ndim == 2:
        w, g, m, v = (t.reshape((1,) + t.shape) for t in (w, g, m, v))
    l, a, b = w.shape

    def body(w_ref, g_ref, m_ref, v_ref, d_ref, nm_ref, nv_ref):
        d_ref[...], nm_ref[...], nv_ref[...] = _adamw_math(w_ref[...], g_ref[...], m_ref[...], v_ref[...])

    blk = pl.BlockSpec((1, a, b), lambda li: (li, 0, 0))
    outs = pl.pallas_call(
        body, name=name,
        out_shape=tuple(jax.ShapeDtypeStruct((l, a, b), F32) for _ in range(3)),
        grid=(l,),
        in_specs=[blk, blk, blk, blk],
        out_specs=(blk, blk, blk),
        compiler_params=_cparams(("parallel",)),
    )(w, g, m, v)
    return tuple(o.reshape(shape) for o in outs)


def _adamw_layers(w, reduced, received, on_core0, m, v, *, name):
    l, a, b = w.shape
    assert l == DEPTH == 2
    ta = _row_tile(a, F32_SUBLANES, 384)

    def body(flag_ref, w_ref, r0_ref, o0_ref, r1_ref, o1_ref, m_ref, v_ref, g_ref, d_ref, nm_ref, nv_ref):
        core0 = flag_ref[...] > 0.5
        g0 = jnp.where(core0, r0_ref[...], o0_ref[...])
        g1 = jnp.where(core0, o1_ref[...], r1_ref[...])
        gv = jnp.where(pl.program_id(0) == 0, g0, g1)
        g_ref[0] = gv
        d_ref[0], nm_ref[0], nv_ref[0] = _adamw_math(w_ref[0], gv, m_ref[0], v_ref[0])

    blk = pl.BlockSpec((1, ta, b), lambda li, ai: (li, ai, 0))

    def grad_block(layer):
        return pl.BlockSpec((ta, b), lambda li, ai: (jnp.where(li == layer, ai, 0), 0))

    g0blk, g1blk = grad_block(0), grad_block(1)
    return pl.pallas_call(
        body, name=name,
        out_shape=tuple(jax.ShapeDtypeStruct((l, a, b), F32) for _ in range(4)),
        grid=(l, a // ta),
        in_specs=[pl.BlockSpec((1, 1), lambda li, ai: (0, 0)), blk, g0blk, g0blk, g1blk, g1blk, blk, blk],
        out_specs=(blk, blk, blk, blk),
        compiler_params=_cparams(("parallel", "parallel")),
    )(on_core0, w, reduced[0], received[0], reduced[1], received[1], m, v)


MESH_ID = pl.DeviceIdType.MESH
ANY = pl.BlockSpec(memory_space=pl.ANY)


def _position():
    x, y, c = lax.axis_index("x"), lax.axis_index("y"), lax.axis_index("c")
    chips = [(1 - x, y), (x, 1 - y), (1 - x, 1 - y)]
    return x, y, c, chips


def _chip_index(xy):
    return 2 * xy[0] + xy[1]


HBM = pl.BlockSpec(memory_space=pltpu.HBM)
SEM = pl.BlockSpec(memory_space=pltpu.SEMAPHORE)
DATAFLOW_EFFECT = pltpu.SideEffectType.DATAFLOW_SIDE_EFFECTING
TOKEN_SHAPE = (8, LANE)


def _landing(shape, dtype):
    return pltpu.with_memory_space_constraint(lax.empty(shape, dtype), pltpu.HBM)


def _split_start(bufs, sem_shape, issue, *, name, after=None):
    n = len(bufs)
    n_in = n + (after is not None)

    def body(*refs):
        issue(refs[:n], refs[n_in], refs[n_in + 1])
        token = refs[-1]
        token[...] = jnp.zeros_like(token)

    operands = [pltpu.with_memory_space_constraint(t, pltpu.HBM) for t in bufs]
    outs = pl.pallas_call(
        body, name=name,
        out_shape=(pltpu.SemaphoreType.DMA(sem_shape), pltpu.SemaphoreType.DMA(sem_shape),
                   *[pltpu.HBM(t.shape, t.dtype) for t in bufs], jax.ShapeDtypeStruct(TOKEN_SHAPE, F32)),
        in_specs=[HBM] * n + [ANY] * (after is not None),
        out_specs=(SEM, SEM, *[HBM] * n, pl.BlockSpec(memory_space=pltpu.VMEM)),
        input_output_aliases={i: 2 + i for i in range(n)},
        compiler_params=pltpu.CompilerParams(has_side_effects=DATAFLOW_EFFECT),
    )(*operands, *([after] if after is not None else []))
    return outs[0], outs[1], list(outs[2:2 + n]), outs[-1]


def _split_wait(started, after, settle, *, name):
    send_sems, recv_sems, bufs, _ = started
    n = len(bufs)
    afters = tuple(after) if isinstance(after, (tuple, list)) else (after,)

    def body(*refs):
        settle(refs[:n], refs[n], refs[n + 1])

    outs = pl.pallas_call(
        body, name=name,
        out_shape=tuple(pltpu.HBM(t.shape, t.dtype) for t in bufs),
        in_specs=[HBM] * n + [SEM, SEM] + [ANY] * len(afters),
        out_specs=tuple([HBM] * n),
        input_output_aliases={i: i for i in range(n)},
        compiler_params=pltpu.CompilerParams(has_side_effects=DATAFLOW_EFFECT),
    )(*bufs, send_sems, recv_sems, *afters)
    return list(outs)


def _to_sibling(bufs, n, send_sems, recv_sems):
    x, y, c, _ = _position()
    return [pltpu.make_async_remote_copy(
        src_ref=bufs[a], dst_ref=bufs[n + a], send_sem=send_sems.at[a], recv_sem=recv_sems.at[a],
        device_id=(x, y, 1 - c), device_id_type=MESH_ID) for a in range(n)]


def _sibling_push_start(layer, sender_is_reducer, arrays, *, name):
    n = len(arrays)
    sender = layer if sender_is_reducer else 1 - layer

    def issue(bufs, send_sems, recv_sems):
        @pl.when(lax.axis_index("c") == sender)
        def _():
            for cp in _to_sibling(bufs, n, send_sems, recv_sems):
                cp.start()

    lands = [_landing(t.shape, t.dtype) for t in arrays]
    return _split_start(list(arrays) + lands, (n,), issue, name=name)


def _sibling_push_wait(layer, sender_is_reducer, started, after, *, name):
    n = len(started[2]) // 2
    sender = layer if sender_is_reducer else 1 - layer

    def settle(bufs, send_sems, recv_sems):
        c = lax.axis_index("c")

        @pl.when(c == sender)
        def _():
            for cp in _to_sibling(bufs, n, send_sems, recv_sems):
                cp.wait_send()

        @pl.when(c != sender)
        def _():
            for cp in _to_sibling(bufs, n, send_sems, recv_sems):
                cp.wait_recv()

    outs = _split_wait(started, after, settle, name=name)
    return outs[:n], outs[n:]


def _chip_copies(bufs, n, send_sems, recv_sems):
    x, y, c, chips = _position()
    return [pltpu.make_async_remote_copy(
        src_ref=bufs[a].at[_chip_index(chips[k])], dst_ref=bufs[n + a].at[k],
        send_sem=send_sems.at[3 * a + k], recv_sem=recv_sems.at[3 * a + k],
        device_id=(chips[k][0], chips[k][1], c), device_id_type=MESH_ID)
        for a in range(n) for k in range(3)]


def _chip_exchange_start(layer, parts, *, name):
    n = len(parts)

    def issue(bufs, send_sems, recv_sems):
        @pl.when(lax.axis_index("c") == layer)
        def _():
            for cp in _chip_copies(bufs, n, send_sems, recv_sems):
                cp.start()

    lands = [_landing((3,) + t.shape[1:], t.dtype) for t in parts]
    return _split_start(list(parts) + lands, (3 * n,), issue, name=name)


def _chip_exchange_wait(layer, started, after, *, name):
    n = len(started[2]) // 2

    def settle(bufs, send_sems, recv_sems):
        @pl.when(lax.axis_index("c") == layer)
        def _():
            for cp in _chip_copies(bufs, n, send_sems, recv_sems):
                cp.wait()

    outs = _split_wait(started, after, settle, name=name)
    return outs[:n], outs[n:]


def _gather_copies(layer, bufs, n, send_sems, recv_sems, arriving):
    x, y, c, chips = _position()
    me = _chip_index((x, y))
    return [pltpu.make_async_remote_copy(
        src_ref=bufs[a], dst_ref=bufs[n + a].at[_chip_index(chips[k]) if arriving else me],
        send_sem=send_sems.at[3 * a + k], recv_sem=recv_sems.at[3 * a + k],
        device_id=(chips[k][0], chips[k][1], c), device_id_type=MESH_ID)
        for a in range(n) for k in range(3)]


def _gather_start(layer, shards, *, name, after=None):
    n = len(shards)

    def issue(bufs, send_sems, recv_sems):
        @pl.when(lax.axis_index("c") == layer)
        def _():
            for cp in _gather_copies(layer, bufs, n, send_sems, recv_sems, False):
                cp.start()

    lands = [_landing((N_CHIPS,) + t.shape, t.dtype) for t in shards]
    return _split_start(list(shards) + lands, (3 * n,), issue, name=name, after=after)


def _gather_wait(layer, started, after, *, name):
    n = len(started[2]) // 2

    def settle(bufs, send_sems, recv_sems):
        @pl.when(lax.axis_index("c") == layer)
        def _():
            for cp in _gather_copies(layer, bufs, n, send_sems, recv_sems, False):
                cp.wait_send()
            for cp in _gather_copies(layer, bufs, n, send_sems, recv_sems, True):
                cp.wait_recv()

    return _split_wait(started, after, settle, name=name)[n:]


def _handoff_copies(bufs, n, send_sems, recv_sems):
    x, y, c, chips = _position()
    out = []
    for a in range(n):
        for k in range(3):
            slot = bufs[a].at[_chip_index(chips[k])]
            out.append(pltpu.make_async_remote_copy(
                src_ref=slot, dst_ref=slot, send_sem=send_sems.at[3 * a + k], recv_sem=recv_sems.at[3 * a + k],
                device_id=(x, y, 1 - c), device_id_type=MESH_ID))
    return out


def _handoff_start(layer, gathered, *, name):
    n = len(gathered)

    def issue(bufs, send_sems, recv_sems):
        @pl.when(lax.axis_index("c") == layer)
        def _():
            for cp in _handoff_copies(bufs, n, send_sems, recv_sems):
                cp.start()

    return _split_start(list(gathered), (3 * n,), issue, name=name)


def _handoff_wait(layer, started, after, *, name):
    n = len(started[2])

    def settle(bufs, send_sems, recv_sems):
        c = lax.axis_index("c")

        @pl.when(c == layer)
        def _():
            for cp in _handoff_copies(bufs, n, send_sems, recv_sems):
                cp.wait_send()

        @pl.when(c != layer)
        def _():
            for cp in _handoff_copies(bufs, n, send_sems, recv_sems):
                cp.wait_recv()

    return _split_wait(started, after, settle, name=name)


F32_SUBLANES = 8
BF16_SUBLANES = 16


def _row_tile(a, sublanes=BF16_SUBLANES, max_rows=704):
    best = None
    for cand in range(sublanes, min(a, max_rows) + 1, sublanes):
        if a % cand == 0:
            best = cand
    assert best is not None, a
    return best


def _pair_sum(mine, sib, *, name):
    nchip, a, b = sib.shape
    ta = _row_tile(a)

    def body(m_ref, s_ref, o_ref):
        o_ref[...] = (m_ref[...].astype(F32) + s_ref[...].astype(F32)).astype(BF16)

    blk = pl.BlockSpec((1, ta, b), lambda j, r: (j, r, 0))
    return pl.pallas_call(
        body, name=name,
        out_shape=jax.ShapeDtypeStruct((nchip, a, b), BF16),
        grid=(nchip, a // ta),
        in_specs=[blk, blk], out_specs=blk,
        compiler_params=_cparams(("parallel", "parallel")),
    )(mine, sib)


def _total_sum(own, recv, *, name):
    a, b = own.shape
    ta = _row_tile(a)

    def body(o_ref, r_ref, t_ref):
        acc = o_ref[...].astype(F32)
        for k in range(3):
            acc = acc + r_ref[k].astype(F32)
        t_ref[...] = acc

    blk = pl.BlockSpec((ta, b), lambda r: (r, 0))
    return pl.pallas_call(
        body, name=name,
        out_shape=jax.ShapeDtypeStruct((a, b), F32),
        grid=(a // ta,),
        in_specs=[blk, pl.BlockSpec((3, ta, b), lambda r: (0, r, 0))], out_specs=blk,
        compiler_params=_cparams(("parallel",)),
    )(own, recv)


def _all_reduce_small(packed):
    rows, width = packed.shape

    def body(x_ref, out_ref, gath, send_sems, recv_sems, local_sem):
        x, y, c, chips = _position()
        me, sibling = (x, y, c), (x, y, 1 - c)

        def slot(px, py, pc):
            return gath.at[4 * px + 2 * py + pc]

        def copy(k, block, to, src=None):
            return pltpu.make_async_remote_copy(
                src_ref=slot(*block) if src is None else src, dst_ref=slot(*block),
                send_sem=send_sems.at[k], recv_sem=recv_sems.at[k], device_id=to, device_id_type=MESH_ID)

        mine = pltpu.make_async_copy(x_ref, slot(*me), local_sem)
        mine.start()
        first = [copy(0, me, sibling, src=x_ref)]
        first += [copy(1 + j, me, (*chip, c), src=x_ref) for j, chip in enumerate(chips)]
        for cp in first:
            cp.start()
        passed = [copy(4 + j, (*chip, c), sibling) for j, chip in enumerate(chips)]
        for j, chip in enumerate(chips):
            copy(1 + j, (*chip, c), me).wait_recv()
            passed[j].start()
        copy(0, sibling, me).wait_recv()
        for j, chip in enumerate(chips):
            copy(4 + j, (*chip, 1 - c), me).wait_recv()
        for cp in first + passed:
            cp.wait_send()
        mine.wait()
        acc = gath[0]
        for d in range(1, N_DEV):
            acc = acc + gath[d]
        out_ref[...] = acc

    return pl.pallas_call(
        body, name="all_reduce_small",
        out_shape=jax.ShapeDtypeStruct((rows, width), F32),
        in_specs=[pl.BlockSpec(memory_space=pltpu.VMEM)],
        out_specs=pl.BlockSpec(memory_space=pltpu.VMEM),
        scratch_shapes=[pltpu.VMEM((N_DEV, rows, width), F32), pltpu.SemaphoreType.DMA((7,)),
                        pltpu.SemaphoreType.DMA((7,)), pltpu.SemaphoreType.DMA],
    )(packed)


def _mixer_forward_branches(x0, h, w, s, dep=None, before_conv=None):
    proj = _matmul(h, w["w_in_t"], mode="nt", out_dtype=BF16, tm=1024, tn=1664, tk=1024, name="proj_fwd",
                   dep=dep)
    oa, states = _gla_fwd(proj, w["w_fg2"], w["b_fg"], w["gla_norm_g"], s=s)
    conv_wb = w["conv_wb"]
    if before_conv is not None:
        conv_wb = conv_wb + before_conv(oa)[0, 0]
    cb_in = _conv_fwd(proj, conv_wb, s=s)
    return dict(x0=x0, h=h, proj=proj, oa=oa, states=states, cb_in=cb_in)


def _mixer_forward_out(sv, w, s, dep=None):
    ya, yb, mix, x1 = _mixer_out_fwd(sv["oa"], sv["cb_in"], sv["proj"], sv["x0"], w["w_oa"], w["w_ob"], w["w_o"],
                                     s=s, dep=dep)
    return x1, dict(sv, ya=ya, yb=yb, mix=mix)


def _ffn_forward_hidden(x1, w, s, after_gate=None):
    h2 = _rms_fwd(x1, w["norm2_g"], name="rms2_fwd")
    gt = _matmul(h2, w["w_gate_t"], mode="nt", out_dtype=BF16, tm=1024, tn=1408, tk=1024, name="ffn_gate_fwd")
    up = _matmul(h2, w["w_up_t"], mode="nt", out_dtype=BF16, tm=1024, tn=1408, tk=1024, name="ffn_up_fwd",
                 dep=None if after_gate is None else after_gate(gt))
    return dict(x1=x1, h2=h2, gt=gt, up=up, hid=_swiglu_fwd(gt, up, s=s))


def _ffn_forward_out(sv, w, dep=None):
    return _matmul(sv["hid"], w["w_ffn_down"], mode="nn", out_dtype=F32, tm=1024, tn=1024, tk=FFN_HIDDEN,
                   name="ffn_out_fwd", add=sv["x1"], dep=dep)


def _ffn_backward_grads(dx2b, w, sv, s):
    g = {}
    dhid = _matmul(dx2b, w["w_ffn_down"], mode="nt", out_dtype=BF16, tm=1024, tn=FFN_HIDDEN, tk=1024,
                   name="ffn_out_bwd")
    g["w_ffn_down"] = _matmul(sv["hid"], dx2b, mode="tn", out_dtype=BF16, tm=1408, tn=1024, tk=1024,
                              name="ffn_out_wgrad")
    dgt, dup = _swiglu_bwd(sv["gt"], sv["up"], dhid, s=s)
    g["w_gate_t"] = _matmul(dgt, sv["h2"], mode="tn", out_dtype=BF16, tm=1408, tn=1024, tk=1024,
                            name="ffn_gate_wgrad")
    g["w_up_t"] = _matmul(dup, sv["h2"], mode="tn", out_dtype=BF16, tm=1408, tn=1024, tk=1024,
                          name="ffn_up_wgrad")
    return g, dgt, dup, dhid


def _ffn_backward_input(dgt, dup, dx2, w, sv, dep=None):
    dh2 = _matmul(dgt, w["w_gate_t"], mode="nn", out_dtype=BF16, tm=1024, tn=1024, tk=FFN_HIDDEN,
                  name="ffn_gate_bwd", dep=dep)
    return _matmul_norm_bwd(dup, w["w_up_t"], sv["x1"], w["norm2_g"], dx2, tm=512, tk=FFN_HIDDEN,
                            name="ffn_up_bwd_norm2_bwd", add=dh2)


def _mixer_backward_branches(dx1b, w, sv, s, dep=None):
    g = {}
    dya, dyb, dcb, doa, dproj = _mixer_out_bwd(dx1b, sv["proj"], sv["ya"], sv["yb"], w["w_oa"], w["w_ob"],
                                               w["w_o"], s=s, dep=dep)
    g["w_o"] = _matmul(sv["mix"], dx1b, mode="tn", out_dtype=BF16, tm=1024, tn=1024, tk=2048, name="wo_wgrad")
    g["w_ob"] = _matmul(sv["cb_in"], dyb, mode="tn", out_dtype=BF16, tm=1024, tn=1024, tk=2048, name="yb_wgrad")
    g["w_oa"] = _matmul(sv["oa"], dya, mode="tn", out_dtype=BF16, tm=1024, tn=1024, tk=2048, name="ya_wgrad")
    dproj, g["conv_wb"] = _conv_bwd(sv["proj"], dcb, w["conv_wb"], dproj, s=s)
    dproj, dfz, g["gla_norm_g"], g["w_fg2"], g["b_fg"] = _gla_bwd(
        sv["proj"], doa, sv["states"], w["w_fg2"], w["b_fg"], w["gla_norm_g"], dproj, s=s)
    return g, _put_fz(dproj, dfz, s=s)


def _proj_wgrad(dproj, sv, dep=None):
    return _matmul(dproj, sv["h"], mode="tn", out_dtype=BF16, tm=1664, tn=1024, tk=1024, name="proj_wgrad",
                   dep=dep)


def _proj_bwd(dproj, w, x0, dres, dep=None):
    return _matmul_norm_bwd(dproj, w["w_in_t"], x0, w["norm1_g"], dres, tm=512, tk=1664,
                            name="proj_bwd_norm1_bwd", dep=dep)


def _cols_from_chips(t):
    return jnp.transpose(t, (1, 0, 2)).reshape(t.shape[1], -1)


W_IN_ROWS = IN_WIDTH // N_CHIPS
W_IN_ROWS_PAD = -(-W_IN_ROWS // BF16_SUBLANES) * BF16_SUBLANES


def _w_in_t_shard(t):
    return jnp.pad(jnp.transpose(t, (0, 2, 1)), ((0, 0), (0, W_IN_ROWS_PAD - W_IN_ROWS), (0, 0)))


def _w_in_t_unshard(t):
    return jnp.transpose(t[:, :W_IN_ROWS], (0, 2, 1))


def _w_in_row_segments():
    runs = [(0, 0, FZ_ORIG), (FZ_ORIG, FZ_ORIG + GATE_RANK, COL_FZ - FZ_ORIG), (COL_FZ, FZ_ORIG, GATE_RANK)]
    out = []
    for kernel_row, ref_row, length in runs:
        while length:
            chip, local = divmod(ref_row, W_IN_ROWS)
            n = min(length, W_IN_ROWS - local)
            out.append((kernel_row, chip * W_IN_ROWS_PAD + local, n))
            kernel_row, ref_row, length = kernel_row + n, ref_row + n, length - n
    return out


def _permute_rows(src, n_out, segments, *, name):
    n_src, width = src.shape
    block = _row_tile(n_out)
    window = block + BF16_SUBLANES
    assert n_src >= window and src.dtype == BF16

    def body(src_ref, out_ref):
        rows = lax.broadcasted_iota(jnp.int32, (block, window), 0)
        cols = lax.broadcasted_iota(jnp.int32, (block, window), 1)
        for b0 in range(0, n_out, block):
            acc = None
            for o, s, n in segments:
                lo, hi = max(o, b0), min(o + n, b0 + block)
                if lo >= hi:
                    continue
                first = s + (lo - o)
                base = max(0, min(first // BF16_SUBLANES * BF16_SUBLANES, n_src - window))
                shift = (first - base) - (lo - b0)
                pick = (cols == rows + shift) & (rows >= lo - b0) & (rows < hi - b0)
                part = jnp.dot(jnp.where(pick, 1.0, 0.0).astype(BF16), src_ref[base:base + window, :],
                               preferred_element_type=F32)
                acc = part if acc is None else acc + part
            out_ref[b0:b0 + block, :] = (jnp.zeros((block, width), F32) if acc is None else acc).astype(BF16)

    return pl.pallas_call(
        body, name=name,
        out_shape=jax.ShapeDtypeStruct((n_out, width), BF16),
        in_specs=[pl.BlockSpec(memory_space=pltpu.VMEM)],
        out_specs=pl.BlockSpec(memory_space=pltpu.VMEM),
        compiler_params=_cparams(),
    )(src)


def _w_in_t_to_kernel(t):
    return _permute_rows(t.reshape(N_CHIPS * W_IN_ROWS_PAD, D_MODEL), PROJ_W, _w_in_row_segments(),
                         name="w_in_to_kernel_rows")


def _w_in_t_from_kernel(g):
    segments = [(slab, kernel_row, n) for kernel_row, slab, n in _w_in_row_segments()]
    out = _permute_rows(g, N_CHIPS * W_IN_ROWS_PAD, segments, name="w_in_to_chip_rows")
    return out.reshape(N_CHIPS, W_IN_ROWS_PAD, D_MODEL)


def kernel(x, norm1_g, w_in, w_fg2, b_fg, gla_norm_g, w_oa, conv_w, conv_b, w_ob, w_o, norm2_g, w_ffn_gate, w_ffn_up, w_ffn_down, final_g, loss_target, m_norm1_g, m_w_in, m_w_fg2, m_b_fg, m_gla_norm_g, m_w_oa, m_conv_w, m_conv_b, m_w_ob, m_w_o, m_norm2_g, m_w_ffn_gate, m_w_ffn_up, m_w_ffn_down, m_final_g, v_norm1_g, v_w_in, v_w_fg2, v_b_fg, v_gla_norm_g, v_w_oa, v_conv_w, v_conv_b, v_w_ob, v_w_o, v_norm2_g, v_w_ffn_gate, v_w_ffn_up, v_w_ffn_down, v_final_g):
    cx_ = lax.axis_index("x")
    cy_ = lax.axis_index("y")
    cc_ = lax.axis_index("c")
    me = 2 * cx_ + cy_
    on_core0 = jnp.where(cc_ == 0, 1.0, 0.0).astype(F32).reshape(1, 1)

    def swap(t):
        return jnp.swapaxes(t, 1, 2)

    big_names = ["w_in", "w_oa", "w_ob", "w_o", "w_ffn_gate", "w_ffn_up", "w_ffn_down"]
    views = dict(
        w_in=tuple(_w_in_t_shard(t) for t in (w_in, m_w_in, v_w_in)),
        w_oa=(w_oa, m_w_oa, v_w_oa), w_ob=(w_ob, m_w_ob, v_w_ob), w_o=(w_o, m_w_o, v_w_o),
        w_ffn_gate=tuple(swap(t) for t in (w_ffn_gate, m_w_ffn_gate, v_w_ffn_gate)),
        w_ffn_up=tuple(swap(t) for t in (w_ffn_up, m_w_ffn_up, v_w_ffn_up)),
        w_ffn_down=(w_ffn_down, m_w_ffn_down, v_w_ffn_down))
    from_view = dict(w_in=_w_in_t_unshard, w_ffn_gate=swap, w_ffn_up=swap)

    s = x.shape[1]

    proj_names = ["w_in"]
    first_names = ["w_in", "w_fg2", "conv_w"]
    rest_names = ["w_oa", "w_ob", "w_o", "w_ffn_gate", "w_ffn_up", "w_ffn_down"]
    mixer_names = ["w_oa", "w_ob", "w_o"]
    ffn_names = ["w_ffn_gate", "w_ffn_up", "w_ffn_down"]
    weight_key = dict(w_in="w_in_t", w_ffn_gate="w_gate_t", w_ffn_up="w_up_t")
    full_shape = dict(w_oa=(VAL_WIDTH, D_MODEL), w_ob=(CONV_CH, D_MODEL), w_o=(D_MODEL, D_MODEL),
                      w_ffn_gate=(FFN_HIDDEN, D_MODEL), w_ffn_up=(FFN_HIDDEN, D_MODEL),
                      w_ffn_down=(FFN_HIDDEN, D_MODEL))

    conv_w_p = jnp.pad(conv_w, ((0, 0), (0, 8 - conv_w.shape[1]), (0, 0)))
    small_shards = dict(w_fg2=w_fg2, conv_w=conv_w_p)

    def shards_of(l, names):
        return [small_shards[n][l] if n in small_shards else views[n][0][l].astype(BF16) for n in names]

    def small_weights(l):
        return dict(norm1_g=norm1_g[l], norm2_g=norm2_g[l], b_fg=b_fg[l].reshape(1, KEY_WIDTH),
                    gla_norm_g=gla_norm_g[l].reshape(1, HEAD_V))

    def full_weights(l, names, gathered, shards):
        out = {}
        for n, g, t in zip(names, gathered, shards):
            g = lax.dynamic_update_index_in_dim(g, t[None], me, axis=0)
            if n == "w_in":
                out["w_in_t"] = _w_in_t_to_kernel(g)
            elif n == "w_fg2":
                out[n] = jnp.pad(_cols_from_chips(g), ((0, FZ_PAD - GATE_RANK), (0, 0))).astype(BF16)
            elif n == "conv_w":
                out["conv_wb"] = (jnp.pad(_cols_from_chips(g)[:3], ((0, 5), (0, 0)))
                                  + jnp.pad(conv_b[l].reshape(1, CONV_CH), ((3, 4), (0, 0))))
            else:
                out[weight_key.get(n, n)] = g.reshape(full_shape[n])
        return out

    def per_chip(g, names):
        out = []
        for n in names:
            t = g[weight_key.get(n, n)]
            out.append(_w_in_t_from_kernel(t) if n == "w_in"
                       else t.reshape(N_CHIPS, t.shape[0] // N_CHIPS, t.shape[1]))
        return out

    def gather(l, names, tag, after=None):
        shards = shards_of(l, names)
        return shards, _gather_start(l, shards, name=f"gather_{tag}_start", after=after)

    def gathered_to_sibling(l, started, after, tag):
        return _handoff_start(l, _gather_wait(l, started, after, name=f"gather_{tag}_wait"),
                              name=f"handoff_{tag}_start")

    def feed(l, g, names, tag):
        return _sibling_push_start(l, False, per_chip(g, names), name=f"feed_{tag}_start")

    def pair_and_exchange(l, fed, after, names, tag):
        mine, sib = _sibling_push_wait(l, False, fed, after, name=f"feed_{tag}_wait")
        parts = [_pair_sum(a, b, name=f"pair_sum_{tag}_{n}") for n, a, b in zip(names, mine, sib)]
        return _chip_exchange_start(l, parts, name=f"exchange_{tag}_start")

    def total_and_share(l, swapped, after, names, tag):
        parts, recvs = _chip_exchange_wait(l, swapped, after, name=f"exchange_{tag}_wait")
        owns = [lax.dynamic_index_in_dim(p, me, 0, keepdims=False) for p in parts]
        totals = [_total_sum(o, rc, name=f"total_{tag}_{n}") for n, o, rc in zip(names, owns, recvs)]
        return _sibling_push_start(l, True, totals, name=f"share_{tag}_start")

    def shared(l, sharing, after, names, tag):
        totals, others = _sibling_push_wait(l, True, sharing, after, name=f"share_{tag}_wait")
        return {n: (t, o) for n, t, o in zip(names, totals, others)}

    def branches_under_gather(l, xin, h, w, tag, shards_r, started_r):
        box = {}

        def before_conv(oa):
            box["handoff"] = gathered_to_sibling(l, started_r, oa, tag)
            return box["handoff"][3]

        sv = _mixer_forward_branches(xin, h, w, s, dep=started_r[3], before_conv=before_conv)
        rest = _handoff_wait(l, box["handoff"], sv["cb_in"], name=f"handoff_{tag}_wait")
        w.update(full_weights(l, rest_names, rest, shards_r))
        return sv

    shards_p, started = gather(0, first_names, "p0")
    h0 = _rms_fwd(x[0], norm1_g[0], name="rms1_fwd")
    later_shards = shards_of(0, rest_names) + shards_of(1, first_names) + shards_of(1, rest_names)
    started = gathered_to_sibling(0, started, (h0, views["w_in"][1], *later_shards), "p0")
    w0 = small_weights(0)
    w0.update(full_weights(0, first_names, _handoff_wait(0, started, views["w_in"][2], name="handoff_p0_wait"),
                           shards_p))
    sv0 = branches_under_gather(0, x[0], h0, w0, "r0", *gather(0, rest_names, "r0", after=w0["w_in_t"]))

    shards_p1, started = gather(1, first_names, "p1", after=w0["w_o"])
    x1, sv0m = _mixer_forward_out(sv0, w0, s, dep=started[3])
    box1 = {}

    def after_gate(gt):
        box1["handoff"] = gathered_to_sibling(1, started, gt, "p1")
        box1["rest"] = gather(1, rest_names, "r1", after=box1["handoff"][2][0])
        return box1["handoff"][3] + box1["rest"][1][3]

    sv0f = _ffn_forward_hidden(x1, w0, s, after_gate=after_gate)
    x2 = _ffn_forward_out(sv0f, w0)
    w1 = small_weights(1)
    w1.update(full_weights(1, first_names, _handoff_wait(1, box1["handoff"], x2, name="handoff_p1_wait"),
                           shards_p1))

    h1 = _rms_fwd(x2, norm1_g[1], name="rms1_fwd")
    x3, sv1m = _mixer_forward_out(branches_under_gather(1, x2, h1, w1, "r1", *box1["rest"]), w1, s)
    sv1f = _ffn_forward_hidden(x3, w1, s)
    x4 = _ffn_forward_out(sv1f, w1)
    loss_local, dx, dxb, dgf = _loss_head(x4, final_g, loss_target[0])
    loss = lax.psum(loss_local[0, 0], ("x", "y", "c"))

    g1, dgt, dup, _ = _ffn_backward_grads(dxb, w1, sv1f, s)
    dx_mid, dxb_mid, g1["norm2_g"] = _ffn_backward_input(dgt, dup, dx, w1, sv1f)
    gm, dproj = _mixer_backward_branches(dxb_mid, w1, sv1m, s)
    g1.update(gm)
    g1["w_in_t"] = _proj_wgrad(dproj, sv1m)
    fed_1 = feed(1, g1, big_names, "l1")
    dx, dxb, g1["norm1_g"] = _proj_bwd(dproj, w1, sv1m["x0"], dx_mid, dep=fed_1[3])

    g0, dgt, dup, dhid = _ffn_backward_grads(dxb, w0, sv0f, s)
    swap_1 = pair_and_exchange(1, fed_1, dhid, big_names, "l1")
    fed_f = feed(0, g0, ffn_names, "f0")
    dx_mid, dxb_mid, g0["norm2_g"] = _ffn_backward_input(dgt, dup, dx, w0, sv0f, dep=swap_1[3] + fed_f[3])
    swap_f = pair_and_exchange(0, fed_f, dx_mid, ffn_names, "f0")
    gm, dproj = _mixer_backward_branches(dxb_mid, w0, sv0m, s, dep=swap_f[3])
    g0.update(gm)
    share_1 = total_and_share(1, swap_1, dproj, big_names, "l1")
    share_f = total_and_share(0, swap_f, dproj, ffn_names, "f0")
    fed_m = feed(0, g0, mixer_names, "m0")
    g0["w_in_t"] = _proj_wgrad(dproj, sv0m, dep=share_1[3] + share_f[3] + fed_m[3])
    fed_p = feed(0, g0, proj_names, "p0")
    swap_m = pair_and_exchange(0, fed_m, g0["w_in_t"], mixer_names, "m0")
    grad_x, _, g0["norm1_g"] = _proj_bwd(dproj, w0, sv0m["x0"], dx_mid, dep=fed_p[3] + swap_m[3])
    reduced1 = shared(1, share_1, grad_x, big_names, "l1")
    reduced0 = shared(0, share_f, grad_x, ffn_names, "f0")

    swap_p = pair_and_exchange(0, fed_p, grad_x, proj_names, "p0")
    share_m = total_and_share(0, swap_m, swap_p[3], mixer_names, "m0")
    reduced0.update(shared(0, share_m, share_m[3], mixer_names, "m0"))
    grads = [g0, g1]

    def small_rows(t):
        return t.reshape(-1, D_MODEL)

    def tile_rows(t):
        return jnp.pad(t, ((0, -t.shape[0] % F32_SUBLANES), (0, 0)))

    g0, g1 = grads
    pieces = [
        jnp.concatenate([g0["norm1_g"], g1["norm1_g"]], axis=0),
        jnp.concatenate([g0["norm2_g"], g1["norm2_g"]], axis=0),
        dgf,
        small_rows(jnp.concatenate([g0["b_fg"], g1["b_fg"]], axis=1)),
        small_rows(jnp.concatenate([g0["gla_norm_g"], g1["gla_norm_g"],
                                    jnp.zeros((1, D_MODEL - 2 * HEAD_V), F32)], axis=1)),
        jnp.concatenate([g0["conv_wb"][3:4], g1["conv_wb"][3:4]], axis=0),
        jnp.concatenate([g0["conv_wb"][:3], g1["conv_wb"][:3]], axis=0),
        small_rows(jnp.stack([g0["w_fg2"][:GATE_RANK], g1["w_fg2"][:GATE_RANK]])),
    ]
    small = _all_reduce_small(jnp.concatenate([tile_rows(p) for p in pieces], axis=0))
    sg = dict(
        norm1_g=small[0:2], norm2_g=small[8:10], final_g=small[16],
        b_fg=small[24].reshape(DEPTH, KEY_WIDTH), gla_norm_g=small[32, :DEPTH * HEAD_V].reshape(DEPTH, HEAD_V),
        conv_b=small[40:42],
        conv_w=lax.dynamic_slice_in_dim(small[48:54].reshape(DEPTH, 3, CONV_CH), me * (CONV_CH // N_CHIPS),
                                        CONV_CH // N_CHIPS, axis=2),
        w_fg2=lax.dynamic_slice_in_dim(small[56:72].reshape(DEPTH, GATE_RANK, KEY_WIDTH),
                                       me * (KEY_WIDTH // N_CHIPS), KEY_WIDTH // N_CHIPS, axis=2),
    )

    small_params = dict(norm1_g=(norm1_g, m_norm1_g, v_norm1_g), w_fg2=(w_fg2, m_w_fg2, v_w_fg2),
                        b_fg=(b_fg, m_b_fg, v_b_fg), gla_norm_g=(gla_norm_g, m_gla_norm_g, v_gla_norm_g),
                        conv_w=(conv_w, m_conv_w, v_conv_w), conv_b=(conv_b, m_conv_b, v_conv_b),
                        norm2_g=(norm2_g, m_norm2_g, v_norm2_g), final_g=(final_g, m_final_g, v_final_g))
    order = ["norm1_g", "w_in", "w_fg2", "b_fg", "gla_norm_g", "w_oa", "conv_w", "conv_b", "w_ob", "w_o",
             "norm2_g", "w_ffn_gate", "w_ffn_up", "w_ffn_down", "final_g"]
    results = {}

    def update_large(n):
        w_, m_, v_ = views[n]
        outs = _adamw_layers(w_, (reduced0[n][0], reduced1[n][0]), (reduced0[n][1], reduced1[n][1]), on_core0,
                             m_, v_, name="adamw_" + n)
        back = from_view.get(n)
        results[n] = tuple(back(r) for r in outs) if back else outs
        return outs[1]

    for n, (w_, m_, v_) in small_params.items():
        results[n] = (sg[n],) + _adamw(w_, sg[n], m_, v_, name="adamw_" + n)
    updated = tuple(update_large(n) for n in rest_names)
    share_p = total_and_share(0, swap_p, updated, proj_names, "p0")
    reduced0.update(shared(0, share_p, share_p[3], proj_names, "p0"))
    update_large("w_in")
    return (loss, grad_x[None], *[results[n][0] for n in order], *[results[n][1] for n in order],
            *[results[n][2] for n in order], *[results[n][3] for n in order])
```

```python
import functools

import jax
import jax.numpy as jnp
from jax import lax
from jax.experimental import pallas as pl
from jax.experimental.pallas import tpu as pltpu

F32 = jnp.float32
BF16 = jnp.bfloat16

D_MODEL = 1024
DEPTH = 2
CHUNK = 64
GLA_HEADS = 4
KEY_WIDTH = 512
VAL_WIDTH = 1024
HEAD_K = 128
HEAD_V = 256
GATE_RANK = 16
GATE_TAU = 16.0
CONV_CH = 1024
FFN_HIDDEN = 2816
IN_WIDTH = 8208
NORM_EPS = 1e-6
Q_SCALE = HEAD_K ** -0.5
ADAM_LR = 0.001
ADAM_B1 = 0.9
ADAM_B2 = 0.999
ADAM_EPS = 1e-08
ADAM_WD = 0.01
ADAM_STEP = 10

N_CHIPS = 4
N_DEV = 8

LANE = 128
FZ_PAD = LANE
PROJ_W = 8192 + FZ_PAD
COL_Q, COL_K, COL_V, COL_R, COL_GBI, COL_GCI, COL_CX, COL_GA, COL_GB, COL_FZ = (
    0, 512, 1024, 2048, 3072, 4096, 5120, 6144, 7168, 8192)
FZ_ORIG = 3072

VMEM_LIMIT = 52 * 1024 * 1024
HALO = 16


def _cparams(sem=None):
    return pltpu.CompilerParams(dimension_semantics=sem, vmem_limit_bytes=VMEM_LIMIT)


def _sigmoid(x):
    return jax.nn.sigmoid(x)


def _log_sigmoid(x):
    return jnp.minimum(x, 0.0) - jnp.log1p(jnp.exp(-jnp.abs(x)))


_DOT_DIMS = {
    "nn": (((1,), (0,)), ((), ())),
    "nt": (((1,), (1,)), ((), ())),
    "tn": (((0,), (0,)), ((), ())),
}


def _matmul(a, b, *, mode, out_dtype, tm, tn, tk, name, add=None, dep=None):
    if mode == "nn":
        (m, k), n = a.shape, b.shape[1]
    elif mode == "nt":
        (m, k), n = a.shape, b.shape[0]
    else:
        (k, m), n = a.shape, b.shape[1]
    tm, tn, tk = min(tm, m), min(tn, n), min(tk, k)
    assert m % tm == 0 and n % tn == 0 and k % tk == 0, (name, m, n, k, tm, tn, tk)
    nk = k // tk
    has_add = add is not None
    has_dep = dep is not None

    def body(*refs):
        if has_dep:
            refs = refs[1:]
        if has_add:
            a_ref, b_ref, add_ref, o_ref = refs[:4]
            scratch = refs[4:]
        else:
            a_ref, b_ref, o_ref = refs[:3]
            add_ref = None
            scratch = refs[3:]
        part = lax.dot_general(a_ref[...], b_ref[...], _DOT_DIMS[mode], preferred_element_type=F32)

        def finish(acc):
            if add_ref is not None:
                acc = acc + add_ref[...].astype(F32)
            o_ref[...] = acc.astype(o_ref.dtype)

        if nk == 1:
            finish(part)
        else:
            acc_ref = scratch[0]
            kk = pl.program_id(2)

            @pl.when(kk == 0)
            def _():
                acc_ref[...] = part

            @pl.when(kk > 0)
            def _():
                acc_ref[...] += part

            @pl.when(kk == nk - 1)
            def _():
                finish(acc_ref[...])

    if mode == "nn":
        a_spec = pl.BlockSpec((tm, tk), lambda i, j, kk: (i, kk))
        b_spec = pl.BlockSpec((tk, tn), lambda i, j, kk: (kk, j))
    elif mode == "nt":
        a_spec = pl.BlockSpec((tm, tk), lambda i, j, kk: (i, kk))
        b_spec = pl.BlockSpec((tn, tk), lambda i, j, kk: (j, kk))
    else:
        a_spec = pl.BlockSpec((tk, tm), lambda i, j, kk: (kk, i))
        b_spec = pl.BlockSpec((tk, tn), lambda i, j, kk: (kk, j))
    o_spec = pl.BlockSpec((tm, tn), lambda i, j, kk: (i, j))
    in_specs = [a_spec, b_spec] + ([o_spec] if has_add else [])
    operands = (a, b) + ((add,) if has_add else ())
    if has_dep:
        in_specs = [pl.BlockSpec(dep.shape, lambda i, j, kk: (0, 0))] + in_specs
        operands = (dep,) + operands
    return pl.pallas_call(
        body,
        name=name,
        out_shape=jax.ShapeDtypeStruct((m, n), out_dtype),
        grid=(m // tm, n // tn, nk),
        in_specs=in_specs,
        out_specs=o_spec,
        scratch_shapes=[pltpu.VMEM((tm, tn), F32)] if nk > 1 else [],
        compiler_params=_cparams(("parallel", "parallel", "arbitrary")),
    )(*operands)


def _matmul_norm_bwd(a, b, x, g, dres, *, tm, tk, name, add=None, dep=None):
    (m, k), d = a.shape, b.shape[1]
    tm, tk = min(tm, m), min(tk, k)
    assert m % tm == 0 and k % tk == 0 and x.shape == (m, d), (name, m, k, d, tm, tk)
    nk = k // tk
    has_add = add is not None
    has_dep = dep is not None

    def body(*refs):
        refs = refs[has_dep:]
        a_ref, b_ref = refs[:2]
        add_ref = refs[2] if has_add else None
        x_ref, g_ref, dres_ref, dx_ref, dxb_ref, dg_ref = refs[2 + has_add:8 + has_add]
        scratch = refs[8 + has_add:]
        i = pl.program_id(0)
        part = jnp.dot(a_ref[...], b_ref[...], preferred_element_type=F32)

        def finish(dh):
            if add_ref is not None:
                dh = dh + add_ref[...].astype(F32)
            xv = x_ref[...]
            rs = lax.rsqrt(jnp.mean(xv * xv, axis=-1, keepdims=True) + NORM_EPS)
            nrm = xv * rs
            dn = dh * g_ref[...]
            dx = dres_ref[...] + rs * (dn - nrm * jnp.mean(dn * nrm, axis=-1, keepdims=True))
            dx_ref[...] = dx
            dxb_ref[...] = dx.astype(BF16)
            dg_part = jnp.sum(dh * nrm, axis=0, keepdims=True)

            @pl.when(i == 0)
            def _():
                dg_ref[...] = dg_part

            @pl.when(i > 0)
            def _():
                dg_ref[...] += dg_part

        if nk == 1:
            finish(part)
        else:
            acc_ref = scratch[0]
            kk = pl.program_id(1)

            @pl.when(kk == 0)
            def _():
                acc_ref[...] = part

            @pl.when(kk > 0)
            def _():
                acc_ref[...] += part

            @pl.when(kk == nk - 1)
            def _():
                finish(acc_ref[...])

    row = pl.BlockSpec((tm, d), lambda i, kk: (i, 0))
    vec = pl.BlockSpec((1, d), lambda i, kk: (0, 0))
    in_specs = [pl.BlockSpec((tm, tk), lambda i, kk: (i, kk)), pl.BlockSpec((tk, d), lambda i, kk: (kk, 0))]
    operands = [a, b]
    if has_add:
        in_specs.append(row)
        operands.append(add)
    in_specs += [row, vec, row]
    operands += [x, g.reshape(1, d), dres]
    if has_dep:
        in_specs = [pl.BlockSpec(dep.shape, lambda i, kk: (0, 0))] + in_specs
        operands = [dep] + operands
    return pl.pallas_call(
        body, name=name,
        out_shape=(jax.ShapeDtypeStruct((m, d), F32), jax.ShapeDtypeStruct((m, d), BF16),
                   jax.ShapeDtypeStruct((1, d), F32)),
        grid=(m // tm, nk),
        in_specs=in_specs,
        out_specs=(row, row, vec),
        scratch_shapes=[pltpu.VMEM((tm, d), F32)] if nk > 1 else [],
        compiler_params=_cparams(("arbitrary", "arbitrary")),
    )(*operands)


def _rms_fwd(x, g, *, name):
    s, d = x.shape
    tr = min(512, s)

    def body(x_ref, g_ref, h_ref):
        xv = x_ref[...]
        rs = lax.rsqrt(jnp.mean(xv * xv, axis=-1, keepdims=True) + NORM_EPS)
        h_ref[...] = (xv * rs * g_ref[...]).astype(BF16)

    return pl.pallas_call(
        body, name=name,
        out_shape=jax.ShapeDtypeStruct((s, d), BF16),
        grid=(s // tr,),
        in_specs=[pl.BlockSpec((tr, d), lambda i: (i, 0)), pl.BlockSpec((1, d), lambda i: (0, 0))],
        out_specs=pl.BlockSpec((tr, d), lambda i: (i, 0)),
        compiler_params=_cparams(("parallel",)),
    )(x, g.reshape(1, d))


def _loss_head(x, g, target):
    s, d = x.shape
    tr = min(512, s)

    def body(x_ref, g_ref, t_ref, loss_ref, dx_ref, dxb_ref, dg_ref):
        i = pl.program_id(0)
        xv = x_ref[...]
        rs = lax.rsqrt(jnp.mean(xv * xv, axis=-1, keepdims=True) + NORM_EPS)
        n = xv * rs
        gv = g_ref[...]
        err = n * gv - t_ref[...]
        row_loss = jnp.mean(err * err, axis=-1, keepdims=True)
        loss_part = 0.5 * jnp.sum(row_loss, axis=0, keepdims=True)
        dy = err * (1.0 / d)
        dn = dy * gv
        dx = rs * (dn - n * jnp.mean(dn * n, axis=-1, keepdims=True))
        dx_ref[...] = dx
        dxb_ref[...] = dx.astype(BF16)
        dg_part = jnp.sum(dy * n, axis=0, keepdims=True)

        @pl.when(i == 0)
        def _():
            loss_ref[...] = loss_part
            dg_ref[...] = dg_part

        @pl.when(i > 0)
        def _():
            loss_ref[...] += loss_part
            dg_ref[...] += dg_part

    row = pl.BlockSpec((tr, d), lambda i: (i, 0))
    vec = pl.BlockSpec((1, d), lambda i: (0, 0))
    one = pl.BlockSpec((1, 1), lambda i: (0, 0))
    return pl.pallas_call(
        body, name="loss_head",
        out_shape=(jax.ShapeDtypeStruct((1, 1), F32), jax.ShapeDtypeStruct((s, d), F32),
                   jax.ShapeDtypeStruct((s, d), BF16), jax.ShapeDtypeStruct((1, d), F32)),
        grid=(s // tr,),
        in_specs=[row, vec, row],
        out_specs=(one, row, row, vec),
        compiler_params=_cparams(("arbitrary",)),
    )(x, g.reshape(1, d), target)


def _conv_taps(u_prev, u, w_ref, rows):
    ext = jnp.concatenate([u_prev, u], axis=0)
    u1 = pltpu.roll(ext, 1, 0)[HALO:HALO + rows]
    u2 = pltpu.roll(ext, 2, 0)[HALO:HALO + rows]
    conv = w_ref[0:1, :] * u2 + w_ref[1:2, :] * u1 + w_ref[2:3, :] * u + w_ref[3:4, :]
    return conv, u1, u2


def _conv_fwd(proj, conv_wb, *, s):
    tr = min(512, s)
    c = CONV_CH
    hb = tr // HALO

    def body(gbi_ref, gci_ref, cx_ref, gci_h_ref, cx_h_ref, w_ref, o_ref):
        i = pl.program_id(0)
        u = gci_ref[...].astype(F32) * cx_ref[...].astype(F32)
        u_prev = gci_h_ref[...].astype(F32) * cx_h_ref[...].astype(F32)
        u_prev = jnp.where(i == 0, 0.0, u_prev)
        conv, _, _ = _conv_taps(u_prev, u, w_ref, tr)
        o_ref[...] = (gbi_ref[...].astype(F32) * conv).astype(BF16)

    def seg(col):
        return pl.BlockSpec((tr, c), lambda i: (i, col // c))

    def halo(col):
        return pl.BlockSpec((HALO, c), lambda i: (jnp.maximum(i * hb - 1, 0), col // c))

    return pl.pallas_call(
        body, name="conv_fwd",
        out_shape=jax.ShapeDtypeStruct((s, c), BF16),
        grid=(s // tr,),
        in_specs=[seg(COL_GBI), seg(COL_GCI), seg(COL_CX), halo(COL_GCI), halo(COL_CX),
                  pl.BlockSpec((8, c), lambda i: (0, 0))],
        out_specs=pl.BlockSpec((tr, c), lambda i: (i, 0)),
        compiler_params=_cparams(("parallel",)),
    )(proj, proj, proj, proj, proj, conv_wb)


def _conv_bwd(proj, dcb, conv_wb, dproj, *, s):
    tr = min(512, s)
    c = CONV_CH
    hb = tr // HALO
    nb = s // tr

    def body(gbi_ref, gci_ref, cx_ref, dcb_ref, gci_h_ref, cx_h_ref, gbi_n_ref, dcb_n_ref, w_ref,
             dproj_in_ref, dp_ref, dw_ref):
        i = pl.program_id(0)
        gbi = gbi_ref[...].astype(F32)
        gci = gci_ref[...].astype(F32)
        cx = cx_ref[...].astype(F32)
        dcb_v = dcb_ref[...].astype(F32)
        u = gci * cx
        u_prev = jnp.where(i == 0, 0.0, gci_h_ref[...].astype(F32) * cx_h_ref[...].astype(F32))
        conv, u1, u2 = _conv_taps(u_prev, u, w_ref, tr)
        dconv = dcb_v * gbi
        dconv_next = jnp.where(i == nb - 1, 0.0, dcb_n_ref[...].astype(F32) * gbi_n_ref[...].astype(F32))
        ext = jnp.concatenate([dconv, dconv_next], axis=0)
        n_ext = tr + HALO
        d1 = pltpu.roll(ext, n_ext - 1, 0)[0:tr]
        d2 = pltpu.roll(ext, n_ext - 2, 0)[0:tr]
        du = w_ref[2:3, :] * dconv + w_ref[1:2, :] * d1 + w_ref[0:1, :] * d2
        dp_ref[:, 0:c] = (dcb_v * conv).astype(BF16)
        dp_ref[:, c:2 * c] = (du * cx).astype(BF16)
        dp_ref[:, 2 * c:3 * c] = (du * gci).astype(BF16)
        part = jnp.concatenate([
            jnp.sum(dconv * u2, axis=0, keepdims=True),
            jnp.sum(dconv * u1, axis=0, keepdims=True),
            jnp.sum(dconv * u, axis=0, keepdims=True),
            jnp.sum(dconv, axis=0, keepdims=True),
            jnp.zeros((4, c), F32)], axis=0)

        @pl.when(i == 0)
        def _():
            dw_ref[...] = part

        @pl.when(i > 0)
        def _():
            dw_ref[...] += part

    def seg(col):
        return pl.BlockSpec((tr, c), lambda i: (i, col // c))

    def halo_prev(col):
        return pl.BlockSpec((HALO, c), lambda i: (jnp.maximum(i * hb - 1, 0), col // c))

    def halo_next(col):
        return pl.BlockSpec((HALO, c), lambda i: (jnp.minimum((i + 1) * hb, nb * hb - 1), col // c))

    return pl.pallas_call(
        body, name="conv_bwd",
        out_shape=(jax.ShapeDtypeStruct((s, PROJ_W), BF16), jax.ShapeDtypeStruct((8, c), F32)),
        grid=(nb,),
        in_specs=[seg(COL_GBI), seg(COL_GCI), seg(COL_CX), pl.BlockSpec((tr, c), lambda i: (i, 0)),
                  halo_prev(COL_GCI), halo_prev(COL_CX), halo_next(COL_GBI),
                  pl.BlockSpec((HALO, c), lambda i: (jnp.minimum((i + 1) * hb, nb * hb - 1), 0)),
                  pl.BlockSpec((8, c), lambda i: (0, 0)), ANY],
        out_specs=(pl.BlockSpec((tr, 3 * c), lambda i: (i, COL_GBI // (3 * c))),
                   pl.BlockSpec((8, c), lambda i: (0, 0))),
        input_output_aliases={9: 0},
        compiler_params=_cparams(("arbitrary",)),
    )(proj, proj, proj, dcb, proj, proj, proj, dcb, conv_wb, dproj)


def _token_spec(dep):
    return ([], []) if dep is None else ([dep], [pl.BlockSpec(dep.shape, lambda i: (0, 0))])


def _mixer_out_fwd(oa, cb_in, proj, x0, w_oa, w_ob, w_o, *, s, dep=None):
    tr = min(512, s)
    d = D_MODEL
    n_dep = dep is not None

    def body(*refs):
        oa_ref, cb_ref, ga_ref, gb_ref, x0_ref, woa_ref, wob_ref, wo_ref, ya_ref, yb_ref, mix_ref, x1_ref = (
            refs[n_dep:])
        ya = jnp.dot(oa_ref[...], woa_ref[...], preferred_element_type=F32).astype(BF16)
        yb = jnp.dot(cb_ref[...], wob_ref[...], preferred_element_type=F32).astype(BF16)
        ya_ref[...] = ya
        yb_ref[...] = yb
        sa = _sigmoid(ga_ref[...].astype(F32))
        sb = _sigmoid(gb_ref[...].astype(F32))
        mix = (sa * ya.astype(F32) + sb * yb.astype(F32)).astype(BF16)
        mix_ref[...] = mix
        x1_ref[...] = x0_ref[...] + jnp.dot(mix, wo_ref[...], preferred_element_type=F32)

    row = pl.BlockSpec((tr, d), lambda i: (i, 0))
    full = pl.BlockSpec((d, d), lambda i: (0, 0))
    tok, tok_specs = _token_spec(dep)
    return pl.pallas_call(
        body, name="mixer_out_fwd",
        out_shape=(jax.ShapeDtypeStruct((s, d), BF16), jax.ShapeDtypeStruct((s, d), BF16),
                   jax.ShapeDtypeStruct((s, d), BF16), jax.ShapeDtypeStruct((s, d), F32)),
        grid=(s // tr,),
        in_specs=tok_specs + [row, row, pl.BlockSpec((tr, d), lambda i: (i, COL_GA // d)),
                              pl.BlockSpec((tr, d), lambda i: (i, COL_GB // d)), row, full, full, full],
        out_specs=(row, row, row, row),
        compiler_params=_cparams(("parallel",)),
    )(*tok, oa, cb_in, proj, proj, x0, w_oa, w_ob, w_o)


def _mixer_out_bwd(dx1b, proj, ya, yb, w_oa, w_ob, w_o, *, s, dep=None):
    tr = min(512, s)
    d = D_MODEL
    n_dep = dep is not None

    def body(*refs):
        (dx_ref, ga_ref, gb_ref, ya_ref, yb_ref, woa_ref, wob_ref, wo_ref,
         dya_ref, dyb_ref, dcb_ref, doa_ref, dg_ref) = refs[n_dep:]
        dm = lax.dot_general(dx_ref[...], wo_ref[...], _DOT_DIMS["nt"], preferred_element_type=F32)
        dm = dm.astype(BF16).astype(F32)
        sa = _sigmoid(ga_ref[...].astype(F32))
        sb = _sigmoid(gb_ref[...].astype(F32))
        dya = (dm * sa).astype(BF16)
        dyb = (dm * sb).astype(BF16)
        dya_ref[...] = dya
        dyb_ref[...] = dyb
        dg_ref[:, 0:d] = (dm * ya_ref[...].astype(F32) * sa * (1.0 - sa)).astype(BF16)
        dg_ref[:, d:2 * d] = (dm * yb_ref[...].astype(F32) * sb * (1.0 - sb)).astype(BF16)
        dcb_ref[...] = lax.dot_general(dyb, wob_ref[...], _DOT_DIMS["nt"],
                                       preferred_element_type=F32).astype(BF16)
        doa_ref[...] = lax.dot_general(dya, woa_ref[...], _DOT_DIMS["nt"],
                                       preferred_element_type=F32).astype(BF16)

    row = pl.BlockSpec((tr, d), lambda i: (i, 0))
    full = pl.BlockSpec((d, d), lambda i: (0, 0))
    tok, tok_specs = _token_spec(dep)
    return pl.pallas_call(
        body, name="mixer_out_bwd",
        out_shape=tuple(jax.ShapeDtypeStruct((s, d), BF16) for _ in range(4))
        + (jax.ShapeDtypeStruct((s, PROJ_W), BF16),),
        grid=(s // tr,),
        in_specs=tok_specs + [row, pl.BlockSpec((tr, d), lambda i: (i, COL_GA // d)),
                              pl.BlockSpec((tr, d), lambda i: (i, COL_GB // d)), row, row, full, full, full],
        out_specs=(row, row, row, row, pl.BlockSpec((tr, 2 * d), lambda i: (i, COL_GA // (2 * d)))),
        compiler_params=_cparams(("parallel",)),
    )(*tok, dx1b, proj, proj, ya, yb, w_oa, w_ob, w_o)


def _swiglu_fwd(gt, up, *, s):
    tr = min(256, s)
    f = FFN_HIDDEN

    def body(gt_ref, up_ref, o_ref):
        gv = gt_ref[...].astype(F32)
        o_ref[...] = (gv * _sigmoid(gv) * up_ref[...].astype(F32)).astype(BF16)

    row = pl.BlockSpec((tr, f), lambda i: (i, 0))
    return pl.pallas_call(
        body, name="swiglu_fwd",
        out_shape=jax.ShapeDtypeStruct((s, f), BF16),
        grid=(s // tr,),
        in_specs=[row, row], out_specs=row,
        compiler_params=_cparams(("parallel",)),
    )(gt, up)


def _swiglu_bwd(gt, up, dhid, *, s):
    tr = min(256, s)
    f = FFN_HIDDEN

    def body(gt_ref, up_ref, dh_ref, dgt_ref, dup_ref):
        gv = gt_ref[...].astype(F32)
        uv = up_ref[...].astype(F32)
        dh = dh_ref[...].astype(F32)
        sg = _sigmoid(gv)
        dgt_ref[...] = (dh * uv * sg * (1.0 + gv * (1.0 - sg))).astype(BF16)
        dup_ref[...] = (dh * gv * sg).astype(BF16)

    row = pl.BlockSpec((tr, f), lambda i: (i, 0))
    return pl.pallas_call(
        body, name="swiglu_bwd",
        out_shape=(jax.ShapeDtypeStruct((s, f), BF16), jax.ShapeDtypeStruct((s, f), BF16)),
        grid=(s // tr,),
        in_specs=[row, row, row], out_specs=(row, row),
        compiler_params=_cparams(("parallel",)),
    )(gt, up, dhid)


def _tri(strict):
    r = lax.broadcasted_iota(jnp.int32, (CHUNK, CHUNK), 0)
    c = lax.broadcasted_iota(jnp.int32, (CHUNK, CHUNK), 1)
    return jnp.where((c < r) if strict else (c <= r), 1.0, 0.0).astype(F32)


def _gla_gate_terms(la_c, tri):
    cum = jnp.dot(tri, la_c, precision=lax.Precision.HIGHEST, preferred_element_type=F32)
    cend = cum[CHUNK - 1:CHUNK, :]
    return jnp.exp(cend - cum), jnp.exp(cend)


def _hk(h):
    return slice(h * HEAD_K, (h + 1) * HEAD_K)


def _hv(h):
    return slice(h * HEAD_V, (h + 1) * HEAD_V)


def _gla_fwd(proj, wfg2p, bfg, gn, *, s):
    tr = min(512, s)
    nb = s // tr
    nc = tr // CHUNK

    def body(q_ref, k_ref, v_ref, r_ref, fz_ref, w_ref, b_ref, gn_ref, oa_ref, st_ref,
             state, la_scr, o_scr):
        i = pl.program_id(0)

        @pl.when(i == 0)
        def _():
            state[...] = jnp.zeros_like(state)

        fg = jnp.dot(fz_ref[...], w_ref[...], preferred_element_type=F32) + b_ref[...]
        la_scr[...] = _log_sigmoid(fg) * (1.0 / GATE_TAU)
        tri = _tri(False)

        def chunk(ci, carry):
            r0 = pl.multiple_of(ci * CHUNK, CHUNK)
            rows = pl.ds(r0, CHUNK)
            e, gam = _gla_gate_terms(la_scr[rows, :], tri)
            kd = (k_ref[rows, :].astype(F32) * e).astype(BF16)
            qs = (q_ref[rows, :].astype(F32) * Q_SCALE).astype(BF16)
            v_c = v_ref[rows, :]
            for h in range(GLA_HEADS):
                upd = lax.dot_general(v_c[:, _hv(h)], kd[:, _hk(h)], _DOT_DIMS["tn"],
                                      preferred_element_type=F32)
                st_h = state[:, _hk(h)] * gam[:, _hk(h)] + upd
                state[:, _hk(h)] = st_h
                o_scr[rows, _hv(h)] = lax.dot_general(qs[:, _hk(h)], st_h.astype(BF16), _DOT_DIMS["nt"],
                                                       preferred_element_type=F32)
            st_ref[ci] = state[...].astype(BF16)
            return carry

        lax.fori_loop(0, nc, chunk, 0, unroll=True)
        for h in range(GLA_HEADS):
            o = o_scr[:, _hv(h)]
            rs = lax.rsqrt(jnp.mean(o * o, axis=-1, keepdims=True) + NORM_EPS)
            rv = r_ref[:, _hv(h)].astype(F32)
            oa_ref[:, _hv(h)] = ((o * rs * gn_ref[...]).astype(F32) * (rv * _sigmoid(rv))).astype(BF16)

    return pl.pallas_call(
        body, name="gla_fwd",
        out_shape=(jax.ShapeDtypeStruct((s, VAL_WIDTH), BF16),
                   jax.ShapeDtypeStruct((s // CHUNK, HEAD_V, KEY_WIDTH), BF16)),
        grid=(nb,),
        in_specs=[pl.BlockSpec((tr, KEY_WIDTH), lambda i: (i, COL_Q // KEY_WIDTH)),
                  pl.BlockSpec((tr, KEY_WIDTH), lambda i: (i, COL_K // KEY_WIDTH)),
                  pl.BlockSpec((tr, VAL_WIDTH), lambda i: (i, COL_V // VAL_WIDTH)),
                  pl.BlockSpec((tr, VAL_WIDTH), lambda i: (i, COL_R // VAL_WIDTH)),
                  pl.BlockSpec((tr, FZ_PAD), lambda i: (i, COL_FZ // FZ_PAD)),
                  pl.BlockSpec((FZ_PAD, KEY_WIDTH), lambda i: (0, 0)),
                  pl.BlockSpec((1, KEY_WIDTH), lambda i: (0, 0)),
                  pl.BlockSpec((1, HEAD_V), lambda i: (0, 0))],
        out_specs=(pl.BlockSpec((tr, VAL_WIDTH), lambda i: (i, 0)),
                   pl.BlockSpec((nc, HEAD_V, KEY_WIDTH), lambda i: (i, 0, 0))),
        scratch_shapes=[pltpu.VMEM((HEAD_V, KEY_WIDTH), F32), pltpu.VMEM((tr, KEY_WIDTH), F32),
                        pltpu.VMEM((tr, VAL_WIDTH), F32)],
        compiler_params=_cparams(("arbitrary",)),
    )(proj, proj, proj, proj, proj, wfg2p, bfg, gn)


def _gla_bwd(proj, doa, states, wfg2p, bfg, gn, dproj, *, s):
    tr = min(512, s)
    nb = s // tr
    nc = tr // CHUNK

    def body(q_ref, k_ref, v_ref, r_ref, fz_ref, doa_ref, st_ref, stp_ref, w_ref, b_ref, gn_ref, dproj_in_ref,
             dp_ref, dfz_ref, dgn_ref, dw_ref, db_ref,
             carry, fg_scr, la_scr, dla_scr, o_scr, do_scr):
        i = pl.program_id(0)

        @pl.when(i == 0)
        def _():
            carry[...] = jnp.zeros_like(carry)
            dgn_ref[...] = jnp.zeros_like(dgn_ref)
            dw_ref[...] = jnp.zeros_like(dw_ref)
            db_ref[...] = jnp.zeros_like(db_ref)

        fz = fz_ref[...]
        fg = jnp.dot(fz, w_ref[...], preferred_element_type=F32) + b_ref[...]
        fg_scr[...] = fg
        la_scr[...] = _log_sigmoid(fg) * (1.0 / GATE_TAU)

        def fwd_chunk(ci, c0):
            r0 = pl.multiple_of(ci * CHUNK, CHUNK)
            rows = pl.ds(r0, CHUNK)
            qs = (q_ref[rows, :].astype(F32) * Q_SCALE).astype(BF16)
            st = st_ref[ci].astype(BF16)
            for h in range(GLA_HEADS):
                o_scr[rows, _hv(h)] = lax.dot_general(qs[:, _hk(h)], st[:, _hk(h)], _DOT_DIMS["nt"],
                                                       preferred_element_type=F32)
            return c0

        lax.fori_loop(0, nc, fwd_chunk, 0, unroll=True)

        gnv = gn_ref[...]
        dgn_part = jnp.zeros((1, HEAD_V), F32)
        for h in range(GLA_HEADS):
            o = o_scr[:, _hv(h)]
            rs = lax.rsqrt(jnp.mean(o * o, axis=-1, keepdims=True) + NORM_EPS)
            nrm = o * rs
            rv = r_ref[:, _hv(h)].astype(F32)
            sg = _sigmoid(rv)
            doa_h = doa_ref[:, _hv(h)].astype(F32)
            don = doa_h * (rv * sg)
            dp_ref[:, COL_R + h * HEAD_V:COL_R + (h + 1) * HEAD_V] = (
                doa_h * (nrm * gnv) * (sg * (1.0 + rv * (1.0 - sg)))).astype(BF16)
            dgn_part = dgn_part + jnp.sum(don * nrm, axis=0, keepdims=True)
            dn = don * gnv
            do_scr[:, _hv(h)] = (rs * (dn - nrm * jnp.mean(dn * nrm, axis=-1, keepdims=True))).astype(BF16)
        dgn_ref[...] += dgn_part

        tri = _tri(False)
        tri_s = _tri(True)
        first_block = i == nb - 1

        def bwd_chunk(cc, c0):
            ci = nc - 1 - cc
            r0 = pl.multiple_of(ci * CHUNK, CHUNK)
            rows = pl.ds(r0, CHUNK)
            e, gam = _gla_gate_terms(la_scr[rows, :], tri)
            k_c = k_ref[rows, :].astype(F32)
            kd = k_c * e
            kd_b = kd.astype(BF16)
            qs = (q_ref[rows, :].astype(F32) * Q_SCALE).astype(BF16)
            v_c = v_ref[rows, :]
            do_c = do_scr[rows, :]
            st_b = st_ref[ci]
            st_prev_in = st_ref[jnp.maximum(ci - 1, 0)].astype(F32)
            st_prev_edge = jnp.where(first_block, 0.0, stp_ref[0].astype(F32))
            st_prev = jnp.where(ci > 0, st_prev_in, st_prev_edge)
            dkd_parts = []
            dgam_parts = []
            for h in range(GLA_HEADS):
                dst = lax.dot_general(do_c[:, _hv(h)], qs[:, _hk(h)], _DOT_DIMS["tn"],
                                      preferred_element_type=F32) + carry[:, _hk(h)]
                dst_b = dst.astype(BF16)
                dqs = jnp.dot(do_c[:, _hv(h)], st_b[:, _hk(h)], preferred_element_type=F32)
                dp_ref[rows, COL_Q + h * HEAD_K:COL_Q + (h + 1) * HEAD_K] = (dqs * Q_SCALE).astype(BF16)
                dkd_parts.append(jnp.dot(v_c[:, _hv(h)], dst_b, preferred_element_type=F32))
                dp_ref[rows, COL_V + h * HEAD_V:COL_V + (h + 1) * HEAD_V] = lax.dot_general(
                    kd_b[:, _hk(h)], dst_b, _DOT_DIMS["nt"], preferred_element_type=F32).astype(BF16)
                dgam_parts.append(jnp.sum(dst * st_prev[:, _hk(h)], axis=0, keepdims=True))
                carry[:, _hk(h)] = dst * gam[:, _hk(h)]
            dkd = jnp.concatenate(dkd_parts, axis=1)
            dgam = jnp.concatenate(dgam_parts, axis=1)
            dp_ref[rows, COL_K:COL_K + KEY_WIDTH] = (dkd * e).astype(BF16)
            dz = dkd * kd
            dla_scr[rows, :] = gam * dgam + jnp.dot(tri_s, dz, precision=lax.Precision.HIGHEST,
                                                    preferred_element_type=F32)
            return c0

        lax.fori_loop(0, nc, bwd_chunk, 0, unroll=True)

        dfg = dla_scr[...] * (1.0 / GATE_TAU) * _sigmoid(-fg_scr[...])
        dfg_b = dfg.astype(BF16)
        dfz_ref[...] = lax.dot_general(dfg_b, w_ref[...], _DOT_DIMS["nt"],
                                       preferred_element_type=F32).astype(BF16)
        dw_ref[...] += lax.dot_general(fz, dfg_b, _DOT_DIMS["tn"], preferred_element_type=F32)
        db_ref[...] += jnp.sum(dfg, axis=0, keepdims=True)

    def rev(i):
        return nb - 1 - i

    return pl.pallas_call(
        body, name="gla_bwd",
        out_shape=(jax.ShapeDtypeStruct((s, PROJ_W), BF16),
                   jax.ShapeDtypeStruct((s, FZ_PAD), BF16),
                   jax.ShapeDtypeStruct((1, HEAD_V), F32),
                   jax.ShapeDtypeStruct((FZ_PAD, KEY_WIDTH), F32),
                   jax.ShapeDtypeStruct((1, KEY_WIDTH), F32)),
        grid=(nb,),
        in_specs=[pl.BlockSpec((tr, KEY_WIDTH), lambda i: (rev(i), COL_Q // KEY_WIDTH)),
                  pl.BlockSpec((tr, KEY_WIDTH), lambda i: (rev(i), COL_K // KEY_WIDTH)),
                  pl.BlockSpec((tr, VAL_WIDTH), lambda i: (rev(i), COL_V // VAL_WIDTH)),
                  pl.BlockSpec((tr, VAL_WIDTH), lambda i: (rev(i), COL_R // VAL_WIDTH)),
                  pl.BlockSpec((tr, FZ_PAD), lambda i: (rev(i), COL_FZ // FZ_PAD)),
                  pl.BlockSpec((tr, VAL_WIDTH), lambda i: (rev(i), 0)),
                  pl.BlockSpec((nc, HEAD_V, KEY_WIDTH), lambda i: (rev(i), 0, 0)),
                  pl.BlockSpec((1, HEAD_V, KEY_WIDTH), lambda i: (jnp.maximum(rev(i) * nc - 1, 0), 0, 0)),
                  pl.BlockSpec((FZ_PAD, KEY_WIDTH), lambda i: (0, 0)),
                  pl.BlockSpec((1, KEY_WIDTH), lambda i: (0, 0)),
                  pl.BlockSpec((1, HEAD_V), lambda i: (0, 0)), ANY],
        out_specs=(pl.BlockSpec((tr, 3 * VAL_WIDTH), lambda i: (rev(i), 0)),
                   pl.BlockSpec((tr, FZ_PAD), lambda i: (rev(i), 0)),
                   pl.BlockSpec((1, HEAD_V), lambda i: (0, 0)),
                   pl.BlockSpec((FZ_PAD, KEY_WIDTH), lambda i: (0, 0)),
                   pl.BlockSpec((1, KEY_WIDTH), lambda i: (0, 0))),
        scratch_shapes=[pltpu.VMEM((HEAD_V, KEY_WIDTH), F32),
                        pltpu.VMEM((tr, KEY_WIDTH), F32),
                        pltpu.VMEM((tr, KEY_WIDTH), F32),
                        pltpu.VMEM((tr, KEY_WIDTH), F32),
                        pltpu.VMEM((tr, VAL_WIDTH), F32),
                        pltpu.VMEM((tr, VAL_WIDTH), BF16)],
        input_output_aliases={11: 0},
        compiler_params=_cparams(("arbitrary",)),
    )(proj, proj, proj, proj, proj, doa, states, states, wfg2p, bfg, gn, dproj)


def _put_fz(dproj, dfz, *, s):
    tr = min(512, s)

    def body(dfz_ref, dproj_in_ref, o_ref):
        o_ref[...] = dfz_ref[...]

    return pl.pallas_call(
        body, name="put_dfz",
        out_shape=jax.ShapeDtypeStruct((s, PROJ_W), BF16),
        grid=(s // tr,),
        in_specs=[pl.BlockSpec((tr, FZ_PAD), lambda i: (i, 0)), ANY],
        out_specs=pl.BlockSpec((tr, FZ_PAD), lambda i: (i, COL_FZ // FZ_PAD)),
        input_output_aliases={1: 0},
        compiler_params=_cparams(("parallel",)),
    )(dfz, dproj)


_ADAM_C1 = 1.0 / (1.0 - ADAM_B1 ** ADAM_STEP)
_ADAM_C2 = 1.0 / (1.0 - ADAM_B2 ** ADAM_STEP)


def _adamw_math(wv, gv, mv, vv):
    nm = ADAM_B1 * mv + (1.0 - ADAM_B1) * gv
    nv = ADAM_B2 * vv + (1.0 - ADAM_B2) * (gv * gv)
    delta = -ADAM_LR * ((nm * _ADAM_C1) / (jnp.sqrt(nv * _ADAM_C2) + ADAM_EPS) + ADAM_WD * wv)
    return delta, nm, nv


def _adamw(w, g, m, v, *, name):
    shape = w.shape
    if w.ndim == 1:
        w, g, m, v = (t.reshape(1, 1, -1) for t in (w, g, m, v))
    elif w.ndim == 2:
        w, g, m, v = (t.reshape((1,) + t.shape) for t in (w, g, m, v))
    l, a, b = w.shape

    def body(w_ref, g_ref, m_ref, v_ref, d_ref, nm_ref, nv_ref):
        d_ref[...], nm_ref[...], nv_ref[...] = _adamw_math(w_ref[...], g_ref[...], m_ref[...], v_ref[...])

    blk = pl.BlockSpec((1, a, b), lambda li: (li, 0, 0))
    outs = pl.pallas_call(
        body, name=name,
        out_shape=tuple(jax.ShapeDtypeStruct((l, a, b), F32) for _ in range(3)),
        grid=(l,),
        in_specs=[blk, blk, blk, blk],
        out_specs=(blk, blk, blk),
        compiler_params=_cparams(("parallel",)),
    )(w, g, m, v)
    return tuple(o.reshape(shape) for o in outs)


def _adamw_layers(w, reduced, received, on_core0, m, v, *, name):
    l, a, b = w.shape
    assert l == DEPTH == 2
    ta = _row_tile(a, F32_SUBLANES, 384)

    def body(flag_ref, w_ref, r0_ref, o0_ref, r1_ref, o1_ref, m_ref, v_ref, g_ref, d_ref, nm_ref, nv_ref):
        core0 = flag_ref[...] > 0.5
        g0 = jnp.where(core0, r0_ref[...], o0_ref[...])
        g1 = jnp.where(core0, o1_ref[...], r1_ref[...])
        gv = jnp.where(pl.program_id(0) == 0, g0, g1)
        g_ref[0] = gv
        d_ref[0], nm_ref[0], nv_ref[0] = _adamw_math(w_ref[0], gv, m_ref[0], v_ref[0])

    blk = pl.BlockSpec((1, ta, b), lambda li, ai: (li, ai, 0))

    def grad_block(layer):
        return pl.BlockSpec((ta, b), lambda li, ai: (jnp.where(li == layer, ai, 0), 0))

    g0blk, g1blk = grad_block(0), grad_block(1)
    return pl.pallas_call(
        body, name=name,
        out_shape=tuple(jax.ShapeDtypeStruct((l, a, b), F32) for _ in range(4)),
        grid=(l, a // ta),
        in_specs=[pl.BlockSpec((1, 1), lambda li, ai: (0, 0)), blk, g0blk, g0blk, g1blk, g1blk, blk, blk],
        out_specs=(blk, blk, blk, blk),
        compiler_params=_cparams(("parallel", "parallel")),
    )(on_core0, w, reduced[0], received[0], reduced[1], received[1], m, v)


MESH_ID = pl.DeviceIdType.MESH
ANY = pl.BlockSpec(memory_space=pl.ANY)


def _position():
    x, y, c = lax.axis_index("x"), lax.axis_index("y"), lax.axis_index("c")
    chips = [(1 - x, y), (x, 1 - y), (1 - x, 1 - y)]
    return x, y, c, chips


def _chip_index(xy):
    return 2 * xy[0] + xy[1]


HBM = pl.BlockSpec(memory_space=pltpu.HBM)
SEM = pl.BlockSpec(memory_space=pltpu.SEMAPHORE)
DATAFLOW_EFFECT = pltpu.SideEffectType.DATAFLOW_SIDE_EFFECTING
TOKEN_SHAPE = (8, LANE)


def _landing(shape, dtype):
    return pltpu.with_memory_space_constraint(lax.empty(shape, dtype), pltpu.HBM)


def _split_start(bufs, sem_shape, issue, *, name, after=None):
    n = len(bufs)
    n_in = n + (after is not None)

    def body(*refs):
        issue(refs[:n], refs[n_in], refs[n_in + 1])
        token = refs[-1]
        token[...] = jnp.zeros_like(token)

    operands = [pltpu.with_memory_space_constraint(t, pltpu.HBM) for t in bufs]
    outs = pl.pallas_call(
        body, name=name,
        out_shape=(pltpu.SemaphoreType.DMA(sem_shape), pltpu.SemaphoreType.DMA(sem_shape),
                   *[pltpu.HBM(t.shape, t.dtype) for t in bufs], jax.ShapeDtypeStruct(TOKEN_SHAPE, F32)),
        in_specs=[HBM] * n + [ANY] * (after is not None),
        out_specs=(SEM, SEM, *[HBM] * n, pl.BlockSpec(memory_space=pltpu.VMEM)),
        input_output_aliases={i: 2 + i for i in range(n)},
        compiler_params=pltpu.CompilerParams(has_side_effects=DATAFLOW_EFFECT),
    )(*operands, *([after] if after is not None else []))
    return outs[0], outs[1], list(outs[2:2 + n]), outs[-1]


def _split_wait(started, after, settle, *, name):
    send_sems, recv_sems, bufs, _ = started
    n = len(bufs)
    afters = tuple(after) if isinstance(after, (tuple, list)) else (after,)

    def body(*refs):
        settle(refs[:n], refs[n], refs[n + 1])

    outs = pl.pallas_call(
        body, name=name,
        out_shape=tuple(pltpu.HBM(t.shape, t.dtype) for t in bufs),
        in_specs=[HBM] * n + [SEM, SEM] + [ANY] * len(afters),
        out_specs=tuple([HBM] * n),
        input_output_aliases={i: i for i in range(n)},
        compiler_params=pltpu.CompilerParams(has_side_effects=DATAFLOW_EFFECT),
    )(*bufs, send_sems, recv_sems, *afters)
    return list(outs)


def _to_sibling(bufs, n, send_sems, recv_sems):
    x, y, c, _ = _position()
    return [pltpu.make_async_remote_copy(
        src_ref=bufs[a], dst_ref=bufs[n + a], send_sem=send_sems.at[a], recv_sem=recv_sems.at[a],
        device_id=(x, y, 1 - c), device_id_type=MESH_ID) for a in range(n)]


def _sibling_push_start(layer, sender_is_reducer, arrays, *, name):
    n = len(arrays)
    sender = layer if sender_is_reducer else 1 - layer

    def issue(bufs, send_sems, recv_sems):
        @pl.when(lax.axis_index("c") == sender)
        def _():
            for cp in _to_sibling(bufs, n, send_sems, recv_sems):
                cp.start()

    lands = [_landing(t.shape, t.dtype) for t in arrays]
    return _split_start(list(arrays) + lands, (n,), issue, name=name)


def _sibling_push_wait(layer, sender_is_reducer, started, after, *, name):
    n = len(started[2]) // 2
    sender = layer if sender_is_reducer else 1 - layer

    def settle(bufs, send_sems, recv_sems):
        c = lax.axis_index("c")

        @pl.when(c == sender)
        def _():
            for cp in _to_sibling(bufs, n, send_sems, recv_sems):
                cp.wait_send()

        @pl.when(c != sender)
        def _():
            for cp in _to_sibling(bufs, n, send_sems, recv_sems):
                cp.wait_recv()

    outs = _split_wait(started, after, settle, name=name)
    return outs[:n], outs[n:]


def _chip_copies(bufs, n, send_sems, recv_sems):
    x, y, c, chips = _position()
    return [pltpu.make_async_remote_copy(
        src_ref=bufs[a].at[_chip_index(chips[k])], dst_ref=bufs[n + a].at[k],
        send_sem=send_sems.at[3 * a + k], recv_sem=recv_sems.at[3 * a + k],
        device_id=(chips[k][0], chips[k][1], c), device_id_type=MESH_ID)
        for a in range(n) for k in range(3)]


def _chip_exchange_start(layer, parts, *, name):
    n = len(parts)

    def issue(bufs, send_sems, recv_sems):
        @pl.when(lax.axis_index("c") == layer)
        def _():
            for cp in _chip_copies(bufs, n, send_sems, recv_sems):
                cp.start()

    lands = [_landing((3,) + t.shape[1:], t.dtype) for t in parts]
    return _split_start(list(parts) + lands, (3 * n,), issue, name=name)


def _chip_exchange_wait(layer, started, after, *, name):
    n = len(started[2]) // 2

    def settle(bufs, send_sems, recv_sems):
        @pl.when(lax.axis_index("c") == layer)
        def _():
            for cp in _chip_copies(bufs, n, send_sems, recv_sems):
                cp.wait()

    outs = _split_wait(started, after, settle, name=name)
    return outs[:n], outs[n:]


def _gather_copies(layer, bufs, n, send_sems, recv_sems, arriving):
    x, y, c, chips = _position()
    me = _chip_index((x, y))
    return [pltpu.make_async_remote_copy(
        src_ref=bufs[a], dst_ref=bufs[n + a].at[_chip_index(chips[k]) if arriving else me],
        send_sem=send_sems.at[3 * a + k], recv_sem=recv_sems.at[3 * a + k],
        device_id=(chips[k][0], chips[k][1], c), device_id_type=MESH_ID)
        for a in range(n) for k in range(3)]


def _gather_start(layer, shards, *, name, after=None):
    n = len(shards)

    def issue(bufs, send_sems, recv_sems):
        @pl.when(lax.axis_index("c") == layer)
        def _():
            for cp in _gather_copies(layer, bufs, n, send_sems, recv_sems, False):
                cp.start()

    lands = [_landing((N_CHIPS,) + t.shape, t.dtype) for t in shards]
    return _split_start(list(shards) + lands, (3 * n,), issue, name=name, after=after)


def _gather_wait(layer, started, after, *, name):
    n = len(started[2]) // 2

    def settle(bufs, send_sems, recv_sems):
        @pl.when(lax.axis_index("c") == layer)
        def _():
            for cp in _gather_copies(layer, bufs, n, send_sems, recv_sems, False):
                cp.wait_send()
            for cp in _gather_copies(layer, bufs, n, send_sems, recv_sems, True):
                cp.wait_recv()

    return _split_wait(started, after, settle, name=name)[n:]


def _handoff_copies(bufs, n, send_sems, recv_sems):
    x, y, c, chips = _position()
    out = []
    for a in range(n):
        for k in range(3):
            slot = bufs[a].at[_chip_index(chips[k])]
            out.append(pltpu.make_async_remote_copy(
                src_ref=slot, dst_ref=slot, send_sem=send_sems.at[3 * a + k], recv_sem=recv_sems.at[3 * a + k],
                device_id=(x, y, 1 - c), device_id_type=MESH_ID))
    return out


def _handoff_start(layer, gathered, *, name):
    n = len(gathered)

    def issue(bufs, send_sems, recv_sems):
        @pl.when(lax.axis_index("c") == layer)
        def _():
            for cp in _handoff_copies(bufs, n, send_sems, recv_sems):
                cp.start()

    return _split_start(list(gathered), (3 * n,), issue, name=name)


def _handoff_wait(layer, started, after, *, name):
    n = len(started[2])

    def settle(bufs, send_sems, recv_sems):
        c = lax.axis_index("c")

        @pl.when(c == layer)
        def _():
            for cp in _handoff_copies(bufs, n, send_sems, recv_sems):
                cp.wait_send()

        @pl.when(c != layer)
        def _():
            for cp in _handoff_copies(bufs, n, send_sems, recv_sems):
                cp.wait_recv()

    return _split_wait(started, after, settle, name=name)


F32_SUBLANES = 8
BF16_SUBLANES = 16


def _row_tile(a, sublanes=BF16_SUBLANES, max_rows=704):
    best = None
    for cand in range(sublanes, min(a, max_rows) + 1, sublanes):
        if a % cand == 0:
            best = cand
    assert best is not None, a
    return best


def _pair_sum(mine, sib, active, *, name):
    nchip, a, b = sib.shape
    ta = _row_tile(a)

    def body(act_ref, m_ref, s_ref, o_ref):
        @pl.when(act_ref[0] == 1)
        def _():
            o_ref[...] = (m_ref[...].astype(F32) + s_ref[...].astype(F32)).astype(BF16)

    blk = pl.BlockSpec((1, ta, b), lambda j, r, act: (j * act[0], r * act[0], 0))
    return pl.pallas_call(
        body, name=name,
        out_shape=jax.ShapeDtypeStruct((nchip, a, b), BF16),
        grid_spec=pltpu.PrefetchScalarGridSpec(
            num_scalar_prefetch=1, grid=(nchip, a // ta), in_specs=[blk, blk], out_specs=blk),
        compiler_params=_cparams(("arbitrary", "arbitrary")),
    )(active, mine, sib)


def _total_sum(own, recv, active, *, name):
    a, b = own.shape
    ta = _row_tile(a)

    def body(act_ref, o_ref, r_ref, t_ref):
        @pl.when(act_ref[0] == 1)
        def _():
            acc = o_ref[...].astype(F32)
            for k in range(3):
                acc = acc + r_ref[k].astype(F32)
            t_ref[...] = acc

    blk = pl.BlockSpec((ta, b), lambda r, act: (r * act[0], 0))
    return pl.pallas_call(
        body, name=name,
        out_shape=jax.ShapeDtypeStruct((a, b), F32),
        grid_spec=pltpu.PrefetchScalarGridSpec(
            num_scalar_prefetch=1, grid=(a // ta,),
            in_specs=[blk, pl.BlockSpec((3, ta, b), lambda r, act: (0, r * act[0], 0))], out_specs=blk),
        compiler_params=_cparams(("arbitrary",)),
    )(active, own, recv)


def _all_reduce_small(packed):
    rows, width = packed.shape

    def body(x_ref, out_ref, gath, send_sems, recv_sems, local_sem):
        x, y, c, chips = _position()
        me, sibling = (x, y, c), (x, y, 1 - c)

        def slot(px, py, pc):
            return gath.at[4 * px + 2 * py + pc]

        def copy(k, block, to, src=None):
            return pltpu.make_async_remote_copy(
                src_ref=slot(*block) if src is None else src, dst_ref=slot(*block),
                send_sem=send_sems.at[k], recv_sem=recv_sems.at[k], device_id=to, device_id_type=MESH_ID)

        mine = pltpu.make_async_copy(x_ref, slot(*me), local_sem)
        mine.start()
        first = [copy(0, me, sibling, src=x_ref)]
        first += [copy(1 + j, me, (*chip, c), src=x_ref) for j, chip in enumerate(chips)]
        for cp in first:
            cp.start()
        passed = [copy(4 + j, (*chip, c), sibling) for j, chip in enumerate(chips)]
        for j, chip in enumerate(chips):
            copy(1 + j, (*chip, c), me).wait_recv()
            passed[j].start()
        copy(0, sibling, me).wait_recv()
        for j, chip in enumerate(chips):
            copy(4 + j, (*chip, 1 - c), me).wait_recv()
        for cp in first + passed:
            cp.wait_send()
        mine.wait()
        acc = gath[0]
        for d in range(1, N_DEV):
            acc = acc + gath[d]
        out_ref[...] = acc

    return pl.pallas_call(
        body, name="all_reduce_small",
        out_shape=jax.ShapeDtypeStruct((rows, width), F32),
        in_specs=[pl.BlockSpec(memory_space=pltpu.VMEM)],
        out_specs=pl.BlockSpec(memory_space=pltpu.VMEM),
        scratch_shapes=[pltpu.VMEM((N_DEV, rows, width), F32), pltpu.SemaphoreType.DMA((7,)),
                        pltpu.SemaphoreType.DMA((7,)), pltpu.SemaphoreType.DMA],
    )(packed)


def _mixer_forward_branches(x0, h, w, s, dep=None, before_conv=None):
    proj = _matmul(h, w["w_in_t"], mode="nt", out_dtype=BF16, tm=1024, tn=1664, tk=1024, name="proj_fwd",
                   dep=dep)
    oa, states = _gla_fwd(proj, w["w_fg2"], w["b_fg"], w["gla_norm_g"], s=s)
    conv_wb = w["conv_wb"]
    if before_conv is not None:
        conv_wb = conv_wb + before_conv(oa)[0, 0]
    cb_in = _conv_fwd(proj, conv_wb, s=s)
    return dict(x0=x0, h=h, proj=proj, oa=oa, states=states, cb_in=cb_in)


def _mixer_forward_out(sv, w, s, dep=None):
    ya, yb, mix, x1 = _mixer_out_fwd(sv["oa"], sv["cb_in"], sv["proj"], sv["x0"], w["w_oa"], w["w_ob"], w["w_o"],
                                     s=s, dep=dep)
    return x1, dict(sv, ya=ya, yb=yb, mix=mix)


def _ffn_forward_hidden(x1, w, s, after_gate=None):
    h2 = _rms_fwd(x1, w["norm2_g"], name="rms2_fwd")
    gt = _matmul(h2, w["w_gate_t"], mode="nt", out_dtype=BF16, tm=1024, tn=1408, tk=1024, name="ffn_gate_fwd")
    up = _matmul(h2, w["w_up_t"], mode="nt", out_dtype=BF16, tm=1024, tn=1408, tk=1024, name="ffn_up_fwd",
                 dep=None if after_gate is None else after_gate(gt))
    return dict(x1=x1, h2=h2, gt=gt, up=up, hid=_swiglu_fwd(gt, up, s=s))


def _ffn_forward_out(sv, w, dep=None):
    return _matmul(sv["hid"], w["w_ffn_down"], mode="nn", out_dtype=F32, tm=1024, tn=1024, tk=FFN_HIDDEN,
                   name="ffn_out_fwd", add=sv["x1"], dep=dep)


def _ffn_backward_grads(dx2b, w, sv, s):
    g = {}
    dhid = _matmul(dx2b, w["w_ffn_down"], mode="nt", out_dtype=BF16, tm=1024, tn=FFN_HIDDEN, tk=1024,
                   name="ffn_out_bwd")
    g["w_ffn_down"] = _matmul(sv["hid"], dx2b, mode="tn", out_dtype=BF16, tm=1408, tn=1024, tk=1024,
                              name="ffn_out_wgrad")
    dgt, dup = _swiglu_bwd(sv["gt"], sv["up"], dhid, s=s)
    g["w_gate_t"] = _matmul(dgt, sv["h2"], mode="tn", out_dtype=BF16, tm=1408, tn=1024, tk=1024,
                            name="ffn_gate_wgrad")
    g["w_up_t"] = _matmul(dup, sv["h2"], mode="tn", out_dtype=BF16, tm=1408, tn=1024, tk=1024,
                          name="ffn_up_wgrad")
    return g, dgt, dup, dhid


def _ffn_backward_input(dgt, dup, dx2, w, sv, dep=None):
    dh2 = _matmul(dgt, w["w_gate_t"], mode="nn", out_dtype=BF16, tm=1024, tn=1024, tk=FFN_HIDDEN,
                  name="ffn_gate_bwd", dep=dep)
    return _matmul_norm_bwd(dup, w["w_up_t"], sv["x1"], w["norm2_g"], dx2, tm=512, tk=FFN_HIDDEN,
                            name="ffn_up_bwd_norm2_bwd", add=dh2)


def _mixer_backward_branches(dx1b, w, sv, s, dep=None):
    g = {}
    dya, dyb, dcb, doa, dproj = _mixer_out_bwd(dx1b, sv["proj"], sv["ya"], sv["yb"], w["w_oa"], w["w_ob"],
                                               w["w_o"], s=s, dep=dep)
    g["w_o"] = _matmul(sv["mix"], dx1b, mode="tn", out_dtype=BF16, tm=1024, tn=1024, tk=2048, name="wo_wgrad")
    g["w_ob"] = _matmul(sv["cb_in"], dyb, mode="tn", out_dtype=BF16, tm=1024, tn=1024, tk=2048, name="yb_wgrad")
    g["w_oa"] = _matmul(sv["oa"], dya, mode="tn", out_dtype=BF16, tm=1024, tn=1024, tk=2048, name="ya_wgrad")
    dproj, g["conv_wb"] = _conv_bwd(sv["proj"], dcb, w["conv_wb"], dproj, s=s)
    dproj, dfz, g["gla_norm_g"], g["w_fg2"], g["b_fg"] = _gla_bwd(
        sv["proj"], doa, sv["states"], w["w_fg2"], w["b_fg"], w["gla_norm_g"], dproj, s=s)
    return g, _put_fz(dproj, dfz, s=s)


def _proj_wgrad(dproj, sv, dep=None):
    return _matmul(dproj, sv["h"], mode="tn", out_dtype=BF16, tm=1664, tn=1024, tk=1024, name="proj_wgrad",
                   dep=dep)


def _proj_bwd(dproj, w, x0, dres, dep=None):
    return _matmul_norm_bwd(dproj, w["w_in_t"], x0, w["norm1_g"], dres, tm=512, tk=1664,
                            name="proj_bwd_norm1_bwd", dep=dep)


def _cols_from_chips(t):
    return jnp.transpose(t, (1, 0, 2)).reshape(t.shape[1], -1)


W_IN_ROWS = IN_WIDTH // N_CHIPS
W_IN_ROWS_PAD = -(-W_IN_ROWS // BF16_SUBLANES) * BF16_SUBLANES


def _w_in_t_shard(t):
    return jnp.pad(jnp.transpose(t, (0, 2, 1)), ((0, 0), (0, W_IN_ROWS_PAD - W_IN_ROWS), (0, 0)))


def _w_in_t_unshard(t):
    return jnp.transpose(t[:, :W_IN_ROWS], (0, 2, 1))


def _w_in_row_segments():
    runs = [(0, 0, FZ_ORIG), (FZ_ORIG, FZ_ORIG + GATE_RANK, COL_FZ - FZ_ORIG), (COL_FZ, FZ_ORIG, GATE_RANK)]
    out = []
    for kernel_row, ref_row, length in runs:
        while length:
            chip, local = divmod(ref_row, W_IN_ROWS)
            n = min(length, W_IN_ROWS - local)
            out.append((kernel_row, chip * W_IN_ROWS_PAD + local, n))
            kernel_row, ref_row, length = kernel_row + n, ref_row + n, length - n
    return out


def _permute_rows(src, n_out, segments, *, name):
    n_src, width = src.shape
    block = _row_tile(n_out)
    window = block + BF16_SUBLANES
    assert n_src >= window and src.dtype == BF16

    def body(src_ref, out_ref):
        rows = lax.broadcasted_iota(jnp.int32, (block, window), 0)
        cols = lax.broadcasted_iota(jnp.int32, (block, window), 1)
        for b0 in range(0, n_out, block):
            acc = None
            for o, s, n in segments:
                lo, hi = max(o, b0), min(o + n, b0 + block)
                if lo >= hi:
                    continue
                first = s + (lo - o)
                base = max(0, min(first // BF16_SUBLANES * BF16_SUBLANES, n_src - window))
                shift = (first - base) - (lo - b0)
                pick = (cols == rows + shift) & (rows >= lo - b0) & (rows < hi - b0)
                part = jnp.dot(jnp.where(pick, 1.0, 0.0).astype(BF16), src_ref[base:base + window, :],
                               preferred_element_type=F32)
                acc = part if acc is None else acc + part
            out_ref[b0:b0 + block, :] = (jnp.zeros((block, width), F32) if acc is None else acc).astype(BF16)

    return pl.pallas_call(
        body, name=name,
        out_shape=jax.ShapeDtypeStruct((n_out, width), BF16),
        in_specs=[pl.BlockSpec(memory_space=pltpu.VMEM)],
        out_specs=pl.BlockSpec(memory_space=pltpu.VMEM),
        compiler_params=_cparams(),
    )(src)


def _w_in_t_to_kernel(t):
    return _permute_rows(t.reshape(N_CHIPS * W_IN_ROWS_PAD, D_MODEL), PROJ_W, _w_in_row_segments(),
                         name="w_in_to_kernel_rows")


def _w_in_t_from_kernel(g):
    segments = [(slab, kernel_row, n) for kernel_row, slab, n in _w_in_row_segments()]
    out = _permute_rows(g, N_CHIPS * W_IN_ROWS_PAD, segments, name="w_in_to_chip_rows")
    return out.reshape(N_CHIPS, W_IN_ROWS_PAD, D_MODEL)


def kernel(x, norm1_g, w_in, w_fg2, b_fg, gla_norm_g, w_oa, conv_w, conv_b, w_ob, w_o, norm2_g, w_ffn_gate, w_ffn_up, w_ffn_down, final_g, loss_target, m_norm1_g, m_w_in, m_w_fg2, m_b_fg, m_gla_norm_g, m_w_oa, m_conv_w, m_conv_b, m_w_ob, m_w_o, m_norm2_g, m_w_ffn_gate, m_w_ffn_up, m_w_ffn_down, m_final_g, v_norm1_g, v_w_in, v_w_fg2, v_b_fg, v_gla_norm_g, v_w_oa, v_conv_w, v_conv_b, v_w_ob, v_w_o, v_norm2_g, v_w_ffn_gate, v_w_ffn_up, v_w_ffn_down, v_final_g):
    cx_ = lax.axis_index("x")
    cy_ = lax.axis_index("y")
    cc_ = lax.axis_index("c")
    me = 2 * cx_ + cy_
    on_core0 = jnp.where(cc_ == 0, 1.0, 0.0).astype(F32).reshape(1, 1)

    def swap(t):
        return jnp.swapaxes(t, 1, 2)

    big_names = ["w_in", "w_oa", "w_ob", "w_o", "w_ffn_gate", "w_ffn_up", "w_ffn_down"]
    views = dict(
        w_in=tuple(_w_in_t_shard(t) for t in (w_in, m_w_in, v_w_in)),
        w_oa=(w_oa, m_w_oa, v_w_oa), w_ob=(w_ob, m_w_ob, v_w_ob), w_o=(w_o, m_w_o, v_w_o),
        w_ffn_gate=tuple(swap(t) for t in (w_ffn_gate, m_w_ffn_gate, v_w_ffn_gate)),
        w_ffn_up=tuple(swap(t) for t in (w_ffn_up, m_w_ffn_up, v_w_ffn_up)),
        w_ffn_down=(w_ffn_down, m_w_ffn_down, v_w_ffn_down))
    from_view = dict(w_in=_w_in_t_unshard, w_ffn_gate=swap, w_ffn_up=swap)

    s = x.shape[1]

    proj_names = ["w_in"]
    first_names = ["w_in", "w_fg2", "conv_w"]
    rest_names = ["w_oa", "w_ob", "w_o", "w_ffn_gate", "w_ffn_up", "w_ffn_down"]
    mixer_names = ["w_oa", "w_ob", "w_o"]
    ffn_names = ["w_ffn_gate", "w_ffn_up", "w_ffn_down"]
    weight_key = dict(w_in="w_in_t", w_ffn_gate="w_gate_t", w_ffn_up="w_up_t")
    full_shape = dict(w_oa=(VAL_WIDTH, D_MODEL), w_ob=(CONV_CH, D_MODEL), w_o=(D_MODEL, D_MODEL),
                      w_ffn_gate=(FFN_HIDDEN, D_MODEL), w_ffn_up=(FFN_HIDDEN, D_MODEL),
                      w_ffn_down=(FFN_HIDDEN, D_MODEL))

    conv_w_p = jnp.pad(conv_w, ((0, 0), (0, 8 - conv_w.shape[1]), (0, 0)))
    small_shards = dict(w_fg2=w_fg2, conv_w=conv_w_p)

    def shards_of(l, names):
        return [small_shards[n][l] if n in small_shards else views[n][0][l].astype(BF16) for n in names]

    def small_weights(l):
        return dict(norm1_g=norm1_g[l], norm2_g=norm2_g[l], b_fg=b_fg[l].reshape(1, KEY_WIDTH),
                    gla_norm_g=gla_norm_g[l].reshape(1, HEAD_V))

    def full_weights(l, names, gathered, shards):
        out = {}
        for n, g, t in zip(names, gathered, shards):
            g = lax.dynamic_update_index_in_dim(g, t[None], me, axis=0)
            if n == "w_in":
                out["w_in_t"] = _w_in_t_to_kernel(g)
            elif n == "w_fg2":
                out[n] = jnp.pad(_cols_from_chips(g), ((0, FZ_PAD - GATE_RANK), (0, 0))).astype(BF16)
            elif n == "conv_w":
                out["conv_wb"] = (jnp.pad(_cols_from_chips(g)[:3], ((0, 5), (0, 0)))
                                  + jnp.pad(conv_b[l].reshape(1, CONV_CH), ((3, 4), (0, 0))))
            else:
                out[weight_key.get(n, n)] = g.reshape(full_shape[n])
        return out

    def per_chip(g, names):
        out = []
        for n in names:
            t = g[weight_key.get(n, n)]
            out.append(_w_in_t_from_kernel(t) if n == "w_in"
                       else t.reshape(N_CHIPS, t.shape[0] // N_CHIPS, t.shape[1]))
        return out

    def gather(l, names, tag, after=None):
        shards = shards_of(l, names)
        return shards, _gather_start(l, shards, name=f"gather_{tag}_start", after=after)

    def gathered_to_sibling(l, started, after, tag):
        return _handoff_start(l, _gather_wait(l, started, after, name=f"gather_{tag}_wait"),
                              name=f"handoff_{tag}_start")

    def feed(l, g, names, tag):
        return _sibling_push_start(l, False, per_chip(g, names), name=f"feed_{tag}_start")

    def reduces(l):
        return jnp.where(cc_ == l, 1, 0).astype(jnp.int32).reshape(1)

    def pair_and_exchange(l, fed, after, names, tag):
        mine, sib = _sibling_push_wait(l, False, fed, after, name=f"feed_{tag}_wait")
        parts = [_pair_sum(a, b, reduces(l), name=f"pair_sum_{tag}_{n}") for n, a, b in zip(names, mine, sib)]
        return _chip_exchange_start(l, parts, name=f"exchange_{tag}_start")

    def total_and_share(l, swapped, after, names, tag):
        parts, recvs = _chip_exchange_wait(l, swapped, after, name=f"exchange_{tag}_wait")
        owns = [lax.dynamic_index_in_dim(p, me, 0, keepdims=False) for p in parts]
        totals = [_total_sum(o, rc, reduces(l), name=f"total_{tag}_{n}") for n, o, rc in zip(names, owns, recvs)]
        return _sibling_push_start(l, True, totals, name=f"share_{tag}_start")

    def shared(l, sharing, after, names, tag):
        totals, others = _sibling_push_wait(l, True, sharing, after, name=f"share_{tag}_wait")
        return {n: (t, o) for n, t, o in zip(names, totals, others)}

    def branches_under_gather(l, xin, h, w, tag, shards_r, started_r):
        box = {}

        def before_conv(oa):
            box["handoff"] = gathered_to_sibling(l, started_r, oa, tag)
            return box["handoff"][3]

        sv = _mixer_forward_branches(xin, h, w, s, dep=started_r[3], before_conv=before_conv)
        rest = _handoff_wait(l, box["handoff"], sv["cb_in"], name=f"handoff_{tag}_wait")
        w.update(full_weights(l, rest_names, rest, shards_r))
        return sv

    shards_p, started = gather(0, first_names, "p0")
    h0 = _rms_fwd(x[0], norm1_g[0], name="rms1_fwd")
    later_shards = shards_of(0, rest_names) + shards_of(1, first_names) + shards_of(1, rest_names)
    started = gathered_to_sibling(0, started, (h0, views["w_in"][1], *later_shards), "p0")
    w0 = small_weights(0)
    w0.update(full_weights(0, first_names, _handoff_wait(0, started, views["w_in"][2], name="handoff_p0_wait"),
                           shards_p))
    sv0 = branches_under_gather(0, x[0], h0, w0, "r0", *gather(0, rest_names, "r0", after=w0["w_in_t"]))

    shards_p1, started = gather(1, first_names, "p1", after=w0["w_o"])
    x1, sv0m = _mixer_forward_out(sv0, w0, s, dep=started[3])
    box1 = {}

    def after_gate(gt):
        box1["handoff"] = gathered_to_sibling(1, started, gt, "p1")
        box1["rest"] = gather(1, rest_names, "r1", after=box1["handoff"][2][0])
        return box1["handoff"][3] + box1["rest"][1][3]

    sv0f = _ffn_forward_hidden(x1, w0, s, after_gate=after_gate)
    x2 = _ffn_forward_out(sv0f, w0)
    w1 = small_weights(1)
    w1.update(full_weights(1, first_names, _handoff_wait(1, box1["handoff"], x2, name="handoff_p1_wait"),
                           shards_p1))

    h1 = _rms_fwd(x2, norm1_g[1], name="rms1_fwd")
    x3, sv1m = _mixer_forward_out(branches_under_gather(1, x2, h1, w1, "r1", *box1["rest"]), w1, s)
    sv1f = _ffn_forward_hidden(x3, w1, s)
    x4 = _ffn_forward_out(sv1f, w1)
    loss_local, dx, dxb, dgf = _loss_head(x4, final_g, loss_target[0])
    loss = lax.psum(loss_local[0, 0], ("x", "y", "c"))

    g1, dgt, dup, _ = _ffn_backward_grads(dxb, w1, sv1f, s)
    dx_mid, dxb_mid, g1["norm2_g"] = _ffn_backward_input(dgt, dup, dx, w1, sv1f)
    gm, dproj = _mixer_backward_branches(dxb_mid, w1, sv1m, s)
    g1.update(gm)
    g1["w_in_t"] = _proj_wgrad(dproj, sv1m)
    fed_1 = feed(1, g1, big_names, "l1")
    dx, dxb, g1["norm1_g"] = _proj_bwd(dproj, w1, sv1m["x0"], dx_mid, dep=fed_1[3])

    g0, dgt, dup, dhid = _ffn_backward_grads(dxb, w0, sv0f, s)
    swap_1 = pair_and_exchange(1, fed_1, dhid, big_names, "l1")
    fed_f = feed(0, g0, ffn_names, "f0")
    dx_mid, dxb_mid, g0["norm2_g"] = _ffn_backward_input(dgt, dup, dx, w0, sv0f, dep=swap_1[3] + fed_f[3])
    swap_f = pair_and_exchange(0, fed_f, dx_mid, ffn_names, "f0")
    gm, dproj = _mixer_backward_branches(dxb_mid, w0, sv0m, s, dep=swap_f[3])
    g0.update(gm)
    share_1 = total_and_share(1, swap_1, dproj, big_names, "l1")
    share_f = total_and_share(0, swap_f, dproj, ffn_names, "f0")
    fed_m = feed(0, g0, mixer_names, "m0")
    g0["w_in_t"] = _proj_wgrad(dproj, sv0m, dep=share_1[3] + share_f[3] + fed_m[3])
    fed_p = feed(0, g0, proj_names, "p0")
    swap_m = pair_and_exchange(0, fed_m, g0["w_in_t"], mixer_names, "m0")
    grad_x, _, g0["norm1_g"] = _proj_bwd(dproj, w0, sv0m["x0"], dx_mid, dep=fed_p[3] + swap_m[3])
    reduced1 = shared(1, share_1, grad_x, big_names, "l1")
    reduced0 = shared(0, share_f, grad_x, ffn_names, "f0")

    swap_p = pair_and_exchange(0, fed_p, grad_x, proj_names, "p0")
    share_m = total_and_share(0, swap_m, swap_p[3], mixer_names, "m0")
    reduced0.update(shared(0, share_m, share_m[3], mixer_names, "m0"))
    grads = [g0, g1]

    def small_rows(t):
        return t.reshape(-1, D_MODEL)

    def tile_rows(t):
        return jnp.pad(t, ((0, -t.shape[0] % F32_SUBLANES), (0, 0)))

    g0, g1 = grads
    pieces = [
        jnp.concatenate([g0["norm1_g"], g1["norm1_g"]], axis=0),
        jnp.concatenate([g0["norm2_g"], g1["norm2_g"]], axis=0),
        dgf,
        small_rows(jnp.concatenate([g0["b_fg"], g1["b_fg"]], axis=1)),
        small_rows(jnp.concatenate([g0["gla_norm_g"], g1["gla_norm_g"],
                                    jnp.zeros((1, D_MODEL - 2 * HEAD_V), F32)], axis=1)),
        jnp.concatenate([g0["conv_wb"][3:4], g1["conv_wb"][3:4]], axis=0),
        jnp.concatenate([g0["conv_wb"][:3], g1["conv_wb"][:3]], axis=0),
        small_rows(jnp.stack([g0["w_fg2"][:GATE_RANK], g1["w_fg2"][:GATE_RANK]])),
    ]
    small = _all_reduce_small(jnp.concatenate([tile_rows(p) for p in pieces], axis=0))
    sg = dict(
        norm1_g=small[0:2], norm2_g=small[8:10], final_g=small[16],
        b_fg=small[24].reshape(DEPTH, KEY_WIDTH), gla_norm_g=small[32, :DEPTH * HEAD_V].reshape(DEPTH, HEAD_V),
        conv_b=small[40:42],
        conv_w=lax.dynamic_slice_in_dim(small[48:54].reshape(DEPTH, 3, CONV_CH), me * (CONV_CH // N_CHIPS),
                                        CONV_CH // N_CHIPS, axis=2),
        w_fg2=lax.dynamic_slice_in_dim(small[56:72].reshape(DEPTH, GATE_RANK, KEY_WIDTH),
                                       me * (KEY_WIDTH // N_CHIPS), KEY_WIDTH // N_CHIPS, axis=2),
    )

    small_params = dict(norm1_g=(norm1_g, m_norm1_g, v_norm1_g), w_fg2=(w_fg2, m_w_fg2, v_w_fg2),
                        b_fg=(b_fg, m_b_fg, v_b_fg), gla_norm_g=(gla_norm_g, m_gla_norm_g, v_gla_norm_g),
                        conv_w=(conv_w, m_conv_w, v_conv_w), conv_b=(conv_b, m_conv_b, v_conv_b),
                        norm2_g=(norm2_g, m_norm2_g, v_norm2_g), final_g=(final_g, m_final_g, v_final_g))
    order = ["norm1_g", "w_in", "w_fg2", "b_fg", "gla_norm_g", "w_oa", "conv_w", "conv_b", "w_ob", "w_o",
             "norm2_g", "w_ffn_gate", "w_ffn_up", "w_ffn_down", "final_g"]
    results = {}

    def update_large(n):
        w_, m_, v_ = views[n]
        outs = _adamw_layers(w_, (reduced0[n][0], reduced1[n][0]), (reduced0[n][1], reduced1[n][1]), on_core0,
                             m_, v_, name="adamw_" + n)
        back = from_view.get(n)
        results[n] = tuple(back(r) for r in outs) if back else outs
        return outs[1]

    for n, (w_, m_, v_) in small_params.items():
        results[n] = (sg[n],) + _adamw(w_, sg[n], m_, v_, name="adamw_" + n)
    updated = tuple(update_large(n) for n in rest_names)
    share_p = total_and_share(0, swap_p, updated, proj_names, "p0")
    reduced0.update(shared(0, share_p, share_p[3], proj_names, "p0"))
    update_large("w_in")
    return (loss, grad_x[None], *[results[n][0] for n in order], *[results[n][1] for n in order],
            *[results[n][2] for n in order], *[results[n][3] for n in order])
```

```python
import functools

import jax
import jax.numpy as jnp
from jax import lax
from jax.experimental import pallas as pl
from jax.experimental.pallas import tpu as pltpu

F32 = jnp.float32
BF16 = jnp.bfloat16

D_MODEL = 1024
DEPTH = 2
CHUNK = 64
GLA_HEADS = 4
KEY_WIDTH = 512
VAL_WIDTH = 1024
HEAD_K = 128
HEAD_V = 256
GATE_RANK = 16
GATE_TAU = 16.0
CONV_CH = 1024
FFN_HIDDEN = 2816
IN_WIDTH = 8208
NORM_EPS = 1e-6
Q_SCALE = HEAD_K ** -0.5
ADAM_LR = 0.001
ADAM_B1 = 0.9
ADAM_B2 = 0.999
ADAM_EPS = 1e-08
ADAM_WD = 0.01
ADAM_STEP = 10

N_CHIPS = 4
N_DEV = 8

LANE = 128
FZ_PAD = LANE
PROJ_W = 8192 + FZ_PAD
COL_Q, COL_K, COL_V, COL_R, COL_GBI, COL_GCI, COL_CX, COL_GA, COL_GB, COL_FZ = (
    0, 512, 1024, 2048, 3072, 4096, 5120, 6144, 7168, 8192)
FZ_ORIG = 3072

VMEM_LIMIT = 52 * 1024 * 1024
HALO = 16


def _cparams(sem=None):
    return pltpu.CompilerParams(dimension_semantics=sem, vmem_limit_bytes=VMEM_LIMIT)


def _sigmoid(x):
    return jax.nn.sigmoid(x)


def _log_sigmoid(x):
    return jnp.minimum(x, 0.0) - jnp.log1p(jnp.exp(-jnp.abs(x)))


_DOT_DIMS = {
    "nn": (((1,), (0,)), ((), ())),
    "nt": (((1,), (1,)), ((), ())),
    "tn": (((0,), (0,)), ((), ())),
}


def _matmul(a, b, *, mode, out_dtype, tm, tn, tk, name, add=None, dep=None):
    if mode == "nn":
        (m, k), n = a.shape, b.shape[1]
    elif mode == "nt":
        (m, k), n = a.shape, b.shape[0]
    else:
        (k, m), n = a.shape, b.shape[1]
    tm, tn, tk = min(tm, m), min(tn, n), min(tk, k)
    assert m % tm == 0 and n % tn == 0 and k % tk == 0, (name, m, n, k, tm, tn, tk)
    nk = k // tk
    has_add = add is not None
    has_dep = dep is not None

    def body(*refs):
        if has_dep:
            refs = refs[1:]
        if has_add:
            a_ref, b_ref, add_ref, o_ref = refs[:4]
            scratch = refs[4:]
        else:
            a_ref, b_ref, o_ref = refs[:3]
            add_ref = None
            scratch = refs[3:]
        part = lax.dot_general(a_ref[...], b_ref[...], _DOT_DIMS[mode], preferred_element_type=F32)

        def finish(acc):
            if add_ref is not None:
                acc = acc + add_ref[...].astype(F32)
            o_ref[...] = acc.astype(o_ref.dtype)

        if nk == 1:
            finish(part)
        else:
            acc_ref = scratch[0]
            kk = pl.program_id(2)

            @pl.when(kk == 0)
            def _():
                acc_ref[...] = part

            @pl.when(kk > 0)
            def _():
                acc_ref[...] += part

            @pl.when(kk == nk - 1)
            def _():
                finish(acc_ref[...])

    if mode == "nn":
        a_spec = pl.BlockSpec((tm, tk), lambda i, j, kk: (i, kk))
        b_spec = pl.BlockSpec((tk, tn), lambda i, j, kk: (kk, j))
    elif mode == "nt":
        a_spec = pl.BlockSpec((tm, tk), lambda i, j, kk: (i, kk))
        b_spec = pl.BlockSpec((tn, tk), lambda i, j, kk: (j, kk))
    else:
        a_spec = pl.BlockSpec((tk, tm), lambda i, j, kk: (kk, i))
        b_spec = pl.BlockSpec((tk, tn), lambda i, j, kk: (kk, j))
    o_spec = pl.BlockSpec((tm, tn), lambda i, j, kk: (i, j))
    in_specs = [a_spec, b_spec] + ([o_spec] if has_add else [])
    operands = (a, b) + ((add,) if has_add else ())
    if has_dep:
        in_specs = [pl.BlockSpec(dep.shape, lambda i, j, kk: (0, 0))] + in_specs
        operands = (dep,) + operands
    return pl.pallas_call(
        body,
        name=name,
        out_shape=jax.ShapeDtypeStruct((m, n), out_dtype),
        grid=(m // tm, n // tn, nk),
        in_specs=in_specs,
        out_specs=o_spec,
        scratch_shapes=[pltpu.VMEM((tm, tn), F32)] if nk > 1 else [],
        compiler_params=_cparams(("parallel", "parallel", "arbitrary")),
    )(*operands)


def _matmul_norm_bwd(a, b, x, g, dres, *, tm, tk, name, add=None, dep=None):
    (m, k), d = a.shape, b.shape[1]
    tm, tk = min(tm, m), min(tk, k)
    assert m % tm == 0 and k % tk == 0 and x.shape == (m, d), (name, m, k, d, tm, tk)
    nk = k // tk
    has_add = add is not None
    has_dep = dep is not None

    def body(*refs):
        refs = refs[has_dep:]
        a_ref, b_ref = refs[:2]
        add_ref = refs[2] if has_add else None
        x_ref, g_ref, dres_ref, dx_ref, dxb_ref, dg_ref = refs[2 + has_add:8 + has_add]
        scratch = refs[8 + has_add:]
        i = pl.program_id(0)
        part = jnp.dot(a_ref[...], b_ref[...], preferred_element_type=F32)

        def finish(dh):
            if add_ref is not None:
                dh = dh + add_ref[...].astype(F32)
            xv = x_ref[...]
            rs = lax.rsqrt(jnp.mean(xv * xv, axis=-1, keepdims=True) + NORM_EPS)
            nrm = xv * rs
            dn = dh * g_ref[...]
            dx = dres_ref[...] + rs * (dn - nrm * jnp.mean(dn * nrm, axis=-1, keepdims=True))
            dx_ref[...] = dx
            dxb_ref[...] = dx.astype(BF16)
            dg_part = jnp.sum(dh * nrm, axis=0, keepdims=True)

            @pl.when(i == 0)
            def _():
                dg_ref[...] = dg_part

            @pl.when(i > 0)
            def _():
                dg_ref[...] += dg_part

        if nk == 1:
            finish(part)
        else:
            acc_ref = scratch[0]
            kk = pl.program_id(1)

            @pl.when(kk == 0)
            def _():
                acc_ref[...] = part

            @pl.when(kk > 0)
            def _():
                acc_ref[...] += part

            @pl.when(kk == nk - 1)
            def _():
                finish(acc_ref[...])

    row = pl.BlockSpec((tm, d), lambda i, kk: (i, 0))
    vec = pl.BlockSpec((1, d), lambda i, kk: (0, 0))
    in_specs = [pl.BlockSpec((tm, tk), lambda i, kk: (i, kk)), pl.BlockSpec((tk, d), lambda i, kk: (kk, 0))]
    operands = [a, b]
    if has_add:
        in_specs.append(row)
        operands.append(add)
    in_specs += [row, vec, row]
    operands += [x, g.reshape(1, d), dres]
    if has_dep:
        in_specs = [pl.BlockSpec(dep.shape, lambda i, kk: (0, 0))] + in_specs
        operands = [dep] + operands
    return pl.pallas_call(
        body, name=name,
        out_shape=(jax.ShapeDtypeStruct((m, d), F32), jax.ShapeDtypeStruct((m, d), BF16),
                   jax.ShapeDtypeStruct((1, d), F32)),
        grid=(m // tm, nk),
        in_specs=in_specs,
        out_specs=(row, row, vec),
        scratch_shapes=[pltpu.VMEM((tm, d), F32)] if nk > 1 else [],
        compiler_params=_cparams(("arbitrary", "arbitrary")),
    )(*operands)


def _rms_fwd(x, g, *, name):
    s, d = x.shape
    tr = min(512, s)

    def body(x_ref, g_ref, h_ref):
        xv = x_ref[...]
        rs = lax.rsqrt(jnp.mean(xv * xv, axis=-1, keepdims=True) + NORM_EPS)
        h_ref[...] = (xv * rs * g_ref[...]).astype(BF16)

    return pl.pallas_call(
        body, name=name,
        out_shape=jax.ShapeDtypeStruct((s, d), BF16),
        grid=(s // tr,),
        in_specs=[pl.BlockSpec((tr, d), lambda i: (i, 0)), pl.BlockSpec((1, d), lambda i: (0, 0))],
        out_specs=pl.BlockSpec((tr, d), lambda i: (i, 0)),
        compiler_params=_cparams(("parallel",)),
    )(x, g.reshape(1, d))


def _loss_head(x, g, target):
    s, d = x.shape
    tr = min(512, s)

    def body(x_ref, g_ref, t_ref, loss_ref, dx_ref, dxb_ref, dg_ref):
        i = pl.program_id(0)
        xv = x_ref[...]
        rs = lax.rsqrt(jnp.mean(xv * xv, axis=-1, keepdims=True) + NORM_EPS)
        n = xv * rs
        gv = g_ref[...]
        err = n * gv - t_ref[...]
        row_loss = jnp.mean(err * err, axis=-1, keepdims=True)
        loss_part = 0.5 * jnp.sum(row_loss, axis=0, keepdims=True)
        dy = err * (1.0 / d)
        dn = dy * gv
        dx = rs * (dn - n * jnp.mean(dn * n, axis=-1, keepdims=True))
        dx_ref[...] = dx
        dxb_ref[...] = dx.astype(BF16)
        dg_part = jnp.sum(dy * n, axis=0, keepdims=True)

        @pl.when(i == 0)
        def _():
            loss_ref[...] = loss_part
            dg_ref[...] = dg_part

        @pl.when(i > 0)
        def _():
            loss_ref[...] += loss_part
            dg_ref[...] += dg_part

    row = pl.BlockSpec((tr, d), lambda i: (i, 0))
    vec = pl.BlockSpec((1, d), lambda i: (0, 0))
    one = pl.BlockSpec((1, 1), lambda i: (0, 0))
    return pl.pallas_call(
        body, name="loss_head",
        out_shape=(jax.ShapeDtypeStruct((1, 1), F32), jax.ShapeDtypeStruct((s, d), F32),
                   jax.ShapeDtypeStruct((s, d), BF16), jax.ShapeDtypeStruct((1, d), F32)),
        grid=(s // tr,),
        in_specs=[row, vec, row],
        out_specs=(one, row, row, vec),
        compiler_params=_cparams(("arbitrary",)),
    )(x, g.reshape(1, d), target)


def _conv_taps(u_prev, u, w_ref, rows):
    ext = jnp.concatenate([u_prev, u], axis=0)
    u1 = pltpu.roll(ext, 1, 0)[HALO:HALO + rows]
    u2 = pltpu.roll(ext, 2, 0)[HALO:HALO + rows]
    conv = w_ref[0:1, :] * u2 + w_ref[1:2, :] * u1 + w_ref[2:3, :] * u + w_ref[3:4, :]
    return conv, u1, u2


def _conv_fwd(proj, conv_wb, *, s):
    tr = min(512, s)
    c = CONV_CH
    hb = tr // HALO

    def body(gbi_ref, gci_ref, cx_ref, gci_h_ref, cx_h_ref, w_ref, o_ref):
        i = pl.program_id(0)
        u = gci_ref[...].astype(F32) * cx_ref[...].astype(F32)
        u_prev = gci_h_ref[...].astype(F32) * cx_h_ref[...].astype(F32)
        u_prev = jnp.where(i == 0, 0.0, u_prev)
        conv, _, _ = _conv_taps(u_prev, u, w_ref, tr)
        o_ref[...] = (gbi_ref[...].astype(F32) * conv).astype(BF16)

    def seg(col):
        return pl.BlockSpec((tr, c), lambda i: (i, col // c))

    def halo(col):
        return pl.BlockSpec((HALO, c), lambda i: (jnp.maximum(i * hb - 1, 0), col // c))

    return pl.pallas_call(
        body, name="conv_fwd",
        out_shape=jax.ShapeDtypeStruct((s, c), BF16),
        grid=(s // tr,),
        in_specs=[seg(COL_GBI), seg(COL_GCI), seg(COL_CX), halo(COL_GCI), halo(COL_CX),
                  pl.BlockSpec((8, c), lambda i: (0, 0))],
        out_specs=pl.BlockSpec((tr, c), lambda i: (i, 0)),
        compiler_params=_cparams(("parallel",)),
    )(proj, proj, proj, proj, proj, conv_wb)


def _conv_bwd(proj, dcb, conv_wb, dproj, *, s):
    tr = min(512, s)
    c = CONV_CH
    hb = tr // HALO
    nb = s // tr

    def body(gbi_ref, gci_ref, cx_ref, dcb_ref, gci_h_ref, cx_h_ref, gbi_n_ref, dcb_n_ref, w_ref,
             dproj_in_ref, dp_ref, dw_ref):
        i = pl.program_id(0)
        gbi = gbi_ref[...].astype(F32)
        gci = gci_ref[...].astype(F32)
        cx = cx_ref[...].astype(F32)
        dcb_v = dcb_ref[...].astype(F32)
        u = gci * cx
        u_prev = jnp.where(i == 0, 0.0, gci_h_ref[...].astype(F32) * cx_h_ref[...].astype(F32))
        conv, u1, u2 = _conv_taps(u_prev, u, w_ref, tr)
        dconv = dcb_v * gbi
        dconv_next = jnp.where(i == nb - 1, 0.0, dcb_n_ref[...].astype(F32) * gbi_n_ref[...].astype(F32))
        ext = jnp.concatenate([dconv, dconv_next], axis=0)
        n_ext = tr + HALO
        d1 = pltpu.roll(ext, n_ext - 1, 0)[0:tr]
        d2 = pltpu.roll(ext, n_ext - 2, 0)[0:tr]
        du = w_ref[2:3, :] * dconv + w_ref[1:2, :] * d1 + w_ref[0:1, :] * d2
        dp_ref[:, 0:c] = (dcb_v * conv).astype(BF16)
        dp_ref[:, c:2 * c] = (du * cx).astype(BF16)
        dp_ref[:, 2 * c:3 * c] = (du * gci).astype(BF16)
        part = jnp.concatenate([
            jnp.sum(dconv * u2, axis=0, keepdims=True),
            jnp.sum(dconv * u1, axis=0, keepdims=True),
            jnp.sum(dconv * u, axis=0, keepdims=True),
            jnp.sum(dconv, axis=0, keepdims=True),
            jnp.zeros((4, c), F32)], axis=0)

        @pl.when(i == 0)
        def _():
            dw_ref[...] = part

        @pl.when(i > 0)
        def _():
            dw_ref[...] += part

    def seg(col):
        return pl.BlockSpec((tr, c), lambda i: (i, col // c))

    def halo_prev(col):
        return pl.BlockSpec((HALO, c), lambda i: (jnp.maximum(i * hb - 1, 0), col // c))

    def halo_next(col):
        return pl.BlockSpec((HALO, c), lambda i: (jnp.minimum((i + 1) * hb, nb * hb - 1), col // c))

    return pl.pallas_call(
        body, name="conv_bwd",
        out_shape=(jax.ShapeDtypeStruct((s, PROJ_W), BF16), jax.ShapeDtypeStruct((8, c), F32)),
        grid=(nb,),
        in_specs=[seg(COL_GBI), seg(COL_GCI), seg(COL_CX), pl.BlockSpec((tr, c), lambda i: (i, 0)),
                  halo_prev(COL_GCI), halo_prev(COL_CX), halo_next(COL_GBI),
                  pl.BlockSpec((HALO, c), lambda i: (jnp.minimum((i + 1) * hb, nb * hb - 1), 0)),
                  pl.BlockSpec((8, c), lambda i: (0, 0)), ANY],
        out_specs=(pl.BlockSpec((tr, 3 * c), lambda i: (i, COL_GBI // (3 * c))),
                   pl.BlockSpec((8, c), lambda i: (0, 0))),
        input_output_aliases={9: 0},
        compiler_params=_cparams(("arbitrary",)),
    )(proj, proj, proj, dcb, proj, proj, proj, dcb, conv_wb, dproj)


def _token_spec(dep):
    return ([], []) if dep is None else ([dep], [pl.BlockSpec(dep.shape, lambda i: (0, 0))])


def _mixer_out_fwd(oa, cb_in, proj, x0, w_oa, w_ob, w_o, *, s, dep=None):
    tr = min(512, s)
    d = D_MODEL
    n_dep = dep is not None

    def body(*refs):
        oa_ref, cb_ref, ga_ref, gb_ref, x0_ref, woa_ref, wob_ref, wo_ref, ya_ref, yb_ref, mix_ref, x1_ref = (
            refs[n_dep:])
        ya = jnp.dot(oa_ref[...], woa_ref[...], preferred_element_type=F32).astype(BF16)
        yb = jnp.dot(cb_ref[...], wob_ref[...], preferred_element_type=F32).astype(BF16)
        ya_ref[...] = ya
        yb_ref[...] = yb
        sa = _sigmoid(ga_ref[...].astype(F32))
        sb = _sigmoid(gb_ref[...].astype(F32))
        mix = (sa * ya.astype(F32) + sb * yb.astype(F32)).astype(BF16)
        mix_ref[...] = mix
        x1_ref[...] = x0_ref[...] + jnp.dot(mix, wo_ref[...], preferred_element_type=F32)

    row = pl.BlockSpec((tr, d), lambda i: (i, 0))
    full = pl.BlockSpec((d, d), lambda i: (0, 0))
    tok, tok_specs = _token_spec(dep)
    return pl.pallas_call(
        body, name="mixer_out_fwd",
        out_shape=(jax.ShapeDtypeStruct((s, d), BF16), jax.ShapeDtypeStruct((s, d), BF16),
                   jax.ShapeDtypeStruct((s, d), BF16), jax.ShapeDtypeStruct((s, d), F32)),
        grid=(s // tr,),
        in_specs=tok_specs + [row, row, pl.BlockSpec((tr, d), lambda i: (i, COL_GA // d)),
                              pl.BlockSpec((tr, d), lambda i: (i, COL_GB // d)), row, full, full, full],
        out_specs=(row, row, row, row),
        compiler_params=_cparams(("parallel",)),
    )(*tok, oa, cb_in, proj, proj, x0, w_oa, w_ob, w_o)


def _mixer_out_bwd(dx1b, proj, ya, yb, w_oa, w_ob, w_o, *, s, dep=None):
    tr = min(512, s)
    d = D_MODEL
    n_dep = dep is not None

    def body(*refs):
        (dx_ref, ga_ref, gb_ref, ya_ref, yb_ref, woa_ref, wob_ref, wo_ref,
         dya_ref, dyb_ref, dcb_ref, doa_ref, dg_ref) = refs[n_dep:]
        dm = lax.dot_general(dx_ref[...], wo_ref[...], _DOT_DIMS["nt"], preferred_element_type=F32)
        dm = dm.astype(BF16).astype(F32)
        sa = _sigmoid(ga_ref[...].astype(F32))
        sb = _sigmoid(gb_ref[...].astype(F32))
        dya = (dm * sa).astype(BF16)
        dyb = (dm * sb).astype(BF16)
        dya_ref[...] = dya
        dyb_ref[...] = dyb
        dg_ref[:, 0:d] = (dm * ya_ref[...].astype(F32) * sa * (1.0 - sa)).astype(BF16)
        dg_ref[:, d:2 * d] = (dm * yb_ref[...].astype(F32) * sb * (1.0 - sb)).astype(BF16)
        dcb_ref[...] = lax.dot_general(dyb, wob_ref[...], _DOT_DIMS["nt"],
                                       preferred_element_type=F32).astype(BF16)
        doa_ref[...] = lax.dot_general(dya, woa_ref[...], _DOT_DIMS["nt"],
                                       preferred_element_type=F32).astype(BF16)

    row = pl.BlockSpec((tr, d), lambda i: (i, 0))
    full = pl.BlockSpec((d, d), lambda i: (0, 0))
    tok, tok_specs = _token_spec(dep)
    return pl.pallas_call(
        body, name="mixer_out_bwd",
        out_shape=tuple(jax.ShapeDtypeStruct((s, d), BF16) for _ in range(4))
        + (jax.ShapeDtypeStruct((s, PROJ_W), BF16),),
        grid=(s // tr,),
        in_specs=tok_specs + [row, pl.BlockSpec((tr, d), lambda i: (i, COL_GA // d)),
                              pl.BlockSpec((tr, d), lambda i: (i, COL_GB // d)), row, row, full, full, full],
        out_specs=(row, row, row, row, pl.BlockSpec((tr, 2 * d), lambda i: (i, COL_GA // (2 * d)))),
        compiler_params=_cparams(("parallel",)),
    )(*tok, dx1b, proj, proj, ya, yb, w_oa, w_ob, w_o)


def _swiglu_bwd(gt, up, dhid, *, s):
    tr = min(256, s)
    f = FFN_HIDDEN

    def body(gt_ref, up_ref, dh_ref, dgt_ref, dup_ref):
        gv = gt_ref[...].astype(F32)
        uv = up_ref[...].astype(F32)
        dh = dh_ref[...].astype(F32)
        sg = _sigmoid(gv)
        dgt_ref[...] = (dh * uv * sg * (1.0 + gv * (1.0 - sg))).astype(BF16)
        dup_ref[...] = (dh * gv * sg).astype(BF16)

    row = pl.BlockSpec((tr, f), lambda i: (i, 0))
    return pl.pallas_call(
        body, name="swiglu_bwd",
        out_shape=(jax.ShapeDtypeStruct((s, f), BF16), jax.ShapeDtypeStruct((s, f), BF16)),
        grid=(s // tr,),
        in_specs=[row, row, row], out_specs=(row, row),
        compiler_params=_cparams(("parallel",)),
    )(gt, up, dhid)


def _tri(strict):
    r = lax.broadcasted_iota(jnp.int32, (CHUNK, CHUNK), 0)
    c = lax.broadcasted_iota(jnp.int32, (CHUNK, CHUNK), 1)
    return jnp.where((c < r) if strict else (c <= r), 1.0, 0.0).astype(F32)


def _gla_gate_terms(la_c, tri):
    cum = jnp.dot(tri, la_c, precision=lax.Precision.HIGHEST, preferred_element_type=F32)
    cend = cum[CHUNK - 1:CHUNK, :]
    return jnp.exp(cend - cum), jnp.exp(cend)


def _hk(h):
    return slice(h * HEAD_K, (h + 1) * HEAD_K)


def _hv(h):
    return slice(h * HEAD_V, (h + 1) * HEAD_V)


def _gla_fwd(proj, wfg2p, bfg, gn, *, s):
    tr = min(512, s)
    nb = s // tr
    nc = tr // CHUNK

    def body(q_ref, k_ref, v_ref, r_ref, fz_ref, w_ref, b_ref, gn_ref, oa_ref, st_ref,
             state, la_scr, o_scr):
        i = pl.program_id(0)

        @pl.when(i == 0)
        def _():
            state[...] = jnp.zeros_like(state)

        fg = jnp.dot(fz_ref[...], w_ref[...], preferred_element_type=F32) + b_ref[...]
        la_scr[...] = _log_sigmoid(fg) * (1.0 / GATE_TAU)
        tri = _tri(False)

        def chunk(ci, carry):
            r0 = pl.multiple_of(ci * CHUNK, CHUNK)
            rows = pl.ds(r0, CHUNK)
            e, gam = _gla_gate_terms(la_scr[rows, :], tri)
            kd = (k_ref[rows, :].astype(F32) * e).astype(BF16)
            qs = (q_ref[rows, :].astype(F32) * Q_SCALE).astype(BF16)
            v_c = v_ref[rows, :]
            for h in range(GLA_HEADS):
                upd = lax.dot_general(v_c[:, _hv(h)], kd[:, _hk(h)], _DOT_DIMS["tn"],
                                      preferred_element_type=F32)
                st_h = state[:, _hk(h)] * gam[:, _hk(h)] + upd
                state[:, _hk(h)] = st_h
                o_scr[rows, _hv(h)] = lax.dot_general(qs[:, _hk(h)], st_h.astype(BF16), _DOT_DIMS["nt"],
                                                       preferred_element_type=F32)
            st_ref[ci] = state[...].astype(BF16)
            return carry

        lax.fori_loop(0, nc, chunk, 0, unroll=True)
        for h in range(GLA_HEADS):
            o = o_scr[:, _hv(h)]
            rs = lax.rsqrt(jnp.mean(o * o, axis=-1, keepdims=True) + NORM_EPS)
            rv = r_ref[:, _hv(h)].astype(F32)
            oa_ref[:, _hv(h)] = ((o * rs * gn_ref[...]).astype(F32) * (rv * _sigmoid(rv))).astype(BF16)

    return pl.pallas_call(
        body, name="gla_fwd",
        out_shape=(jax.ShapeDtypeStruct((s, VAL_WIDTH), BF16),
                   jax.ShapeDtypeStruct((s // CHUNK, HEAD_V, KEY_WIDTH), BF16)),
        grid=(nb,),
        in_specs=[pl.BlockSpec((tr, KEY_WIDTH), lambda i: (i, COL_Q // KEY_WIDTH)),
                  pl.BlockSpec((tr, KEY_WIDTH), lambda i: (i, COL_K // KEY_WIDTH)),
                  pl.BlockSpec((tr, VAL_WIDTH), lambda i: (i, COL_V // VAL_WIDTH)),
                  pl.BlockSpec((tr, VAL_WIDTH), lambda i: (i, COL_R // VAL_WIDTH)),
                  pl.BlockSpec((tr, FZ_PAD), lambda i: (i, COL_FZ // FZ_PAD)),
                  pl.BlockSpec((FZ_PAD, KEY_WIDTH), lambda i: (0, 0)),
                  pl.BlockSpec((1, KEY_WIDTH), lambda i: (0, 0)),
                  pl.BlockSpec((1, HEAD_V), lambda i: (0, 0))],
        out_specs=(pl.BlockSpec((tr, VAL_WIDTH), lambda i: (i, 0)),
                   pl.BlockSpec((nc, HEAD_V, KEY_WIDTH), lambda i: (i, 0, 0))),
        scratch_shapes=[pltpu.VMEM((HEAD_V, KEY_WIDTH), F32), pltpu.VMEM((tr, KEY_WIDTH), F32),
                        pltpu.VMEM((tr, VAL_WIDTH), F32)],
        compiler_params=_cparams(("arbitrary",)),
    )(proj, proj, proj, proj, proj, wfg2p, bfg, gn)


def _gla_bwd(proj, doa, states, wfg2p, bfg, gn, dproj, *, s):
    tr = min(512, s)
    nb = s // tr
    nc = tr // CHUNK

    def body(q_ref, k_ref, v_ref, r_ref, fz_ref, doa_ref, st_ref, stp_ref, w_ref, b_ref, gn_ref, dproj_in_ref,
             dp_ref, dfz_ref, dgn_ref, dw_ref, db_ref,
             carry, fg_scr, la_scr, dla_scr, o_scr, do_scr):
        i = pl.program_id(0)

        @pl.when(i == 0)
        def _():
            carry[...] = jnp.zeros_like(carry)
            dgn_ref[...] = jnp.zeros_like(dgn_ref)
            dw_ref[...] = jnp.zeros_like(dw_ref)
            db_ref[...] = jnp.zeros_like(db_ref)

        fz = fz_ref[...]
        fg = jnp.dot(fz, w_ref[...], preferred_element_type=F32) + b_ref[...]
        fg_scr[...] = fg
        la_scr[...] = _log_sigmoid(fg) * (1.0 / GATE_TAU)

        def fwd_chunk(ci, c0):
            r0 = pl.multiple_of(ci * CHUNK, CHUNK)
            rows = pl.ds(r0, CHUNK)
            qs = (q_ref[rows, :].astype(F32) * Q_SCALE).astype(BF16)
            st = st_ref[ci].astype(BF16)
            for h in range(GLA_HEADS):
                o_scr[rows, _hv(h)] = lax.dot_general(qs[:, _hk(h)], st[:, _hk(h)], _DOT_DIMS["nt"],
                                                       preferred_element_type=F32)
            return c0

        lax.fori_loop(0, nc, fwd_chunk, 0, unroll=True)

        gnv = gn_ref[...]
        dgn_part = jnp.zeros((1, HEAD_V), F32)
        for h in range(GLA_HEADS):
            o = o_scr[:, _hv(h)]
            rs = lax.rsqrt(jnp.mean(o * o, axis=-1, keepdims=True) + NORM_EPS)
            nrm = o * rs
            rv = r_ref[:, _hv(h)].astype(F32)
            sg = _sigmoid(rv)
            doa_h = doa_ref[:, _hv(h)].astype(F32)
            don = doa_h * (rv * sg)
            dp_ref[:, COL_R + h * HEAD_V:COL_R + (h + 1) * HEAD_V] = (
                doa_h * (nrm * gnv) * (sg * (1.0 + rv * (1.0 - sg)))).astype(BF16)
            dgn_part = dgn_part + jnp.sum(don * nrm, axis=0, keepdims=True)
            dn = don * gnv
            do_scr[:, _hv(h)] = (rs * (dn - nrm * jnp.mean(dn * nrm, axis=-1, keepdims=True))).astype(BF16)
        dgn_ref[...] += dgn_part

        tri = _tri(False)
        tri_s = _tri(True)
        first_block = i == nb - 1

        def bwd_chunk(cc, c0):
            ci = nc - 1 - cc
            r0 = pl.multiple_of(ci * CHUNK, CHUNK)
            rows = pl.ds(r0, CHUNK)
            e, gam = _gla_gate_terms(la_scr[rows, :], tri)
            k_c = k_ref[rows, :].astype(F32)
            kd = k_c * e
            kd_b = kd.astype(BF16)
            qs = (q_ref[rows, :].astype(F32) * Q_SCALE).astype(BF16)
            v_c = v_ref[rows, :]
            do_c = do_scr[rows, :]
            st_b = st_ref[ci]
            st_prev_in = st_ref[jnp.maximum(ci - 1, 0)].astype(F32)
            st_prev_edge = jnp.where(first_block, 0.0, stp_ref[0].astype(F32))
            st_prev = jnp.where(ci > 0, st_prev_in, st_prev_edge)
            dkd_parts = []
            dgam_parts = []
            for h in range(GLA_HEADS):
                dst = lax.dot_general(do_c[:, _hv(h)], qs[:, _hk(h)], _DOT_DIMS["tn"],
                                      preferred_element_type=F32) + carry[:, _hk(h)]
                dst_b = dst.astype(BF16)
                dqs = jnp.dot(do_c[:, _hv(h)], st_b[:, _hk(h)], preferred_element_type=F32)
                dp_ref[rows, COL_Q + h * HEAD_K:COL_Q + (h + 1) * HEAD_K] = (dqs * Q_SCALE).astype(BF16)
                dkd_parts.append(jnp.dot(v_c[:, _hv(h)], dst_b, preferred_element_type=F32))
                dp_ref[rows, COL_V + h * HEAD_V:COL_V + (h + 1) * HEAD_V] = lax.dot_general(
                    kd_b[:, _hk(h)], dst_b, _DOT_DIMS["nt"], preferred_element_type=F32).astype(BF16)
                dgam_parts.append(jnp.sum(dst * st_prev[:, _hk(h)], axis=0, keepdims=True))
                carry[:, _hk(h)] = dst * gam[:, _hk(h)]
            dkd = jnp.concatenate(dkd_parts, axis=1)
            dgam = jnp.concatenate(dgam_parts, axis=1)
            dp_ref[rows, COL_K:COL_K + KEY_WIDTH] = (dkd * e).astype(BF16)
            dz = dkd * kd
            dla_scr[rows, :] = gam * dgam + jnp.dot(tri_s, dz, precision=lax.Precision.HIGHEST,
                                                    preferred_element_type=F32)
            return c0

        lax.fori_loop(0, nc, bwd_chunk, 0, unroll=True)

        dfg = dla_scr[...] * (1.0 / GATE_TAU) * _sigmoid(-fg_scr[...])
        dfg_b = dfg.astype(BF16)
        dfz_ref[...] = lax.dot_general(dfg_b, w_ref[...], _DOT_DIMS["nt"],
                                       preferred_element_type=F32).astype(BF16)
        dw_ref[...] += lax.dot_general(fz, dfg_b, _DOT_DIMS["tn"], preferred_element_type=F32)
        db_ref[...] += jnp.sum(dfg, axis=0, keepdims=True)

    def rev(i):
        return nb - 1 - i

    return pl.pallas_call(
        body, name="gla_bwd",
        out_shape=(jax.ShapeDtypeStruct((s, PROJ_W), BF16),
                   jax.ShapeDtypeStruct((s, FZ_PAD), BF16),
                   jax.ShapeDtypeStruct((1, HEAD_V), F32),
                   jax.ShapeDtypeStruct((FZ_PAD, KEY_WIDTH), F32),
                   jax.ShapeDtypeStruct((1, KEY_WIDTH), F32)),
        grid=(nb,),
        in_specs=[pl.BlockSpec((tr, KEY_WIDTH), lambda i: (rev(i), COL_Q // KEY_WIDTH)),
                  pl.BlockSpec((tr, KEY_WIDTH), lambda i: (rev(i), COL_K // KEY_WIDTH)),
                  pl.BlockSpec((tr, VAL_WIDTH), lambda i: (rev(i), COL_V // VAL_WIDTH)),
                  pl.BlockSpec((tr, VAL_WIDTH), lambda i: (rev(i), COL_R // VAL_WIDTH)),
                  pl.BlockSpec((tr, FZ_PAD), lambda i: (rev(i), COL_FZ // FZ_PAD)),
                  pl.BlockSpec((tr, VAL_WIDTH), lambda i: (rev(i), 0)),
                  pl.BlockSpec((nc, HEAD_V, KEY_WIDTH), lambda i: (rev(i), 0, 0)),
                  pl.BlockSpec((1, HEAD_V, KEY_WIDTH), lambda i: (jnp.maximum(rev(i) * nc - 1, 0), 0, 0)),
                  pl.BlockSpec((FZ_PAD, KEY_WIDTH), lambda i: (0, 0)),
                  pl.BlockSpec((1, KEY_WIDTH), lambda i: (0, 0)),
                  pl.BlockSpec((1, HEAD_V), lambda i: (0, 0)), ANY],
        out_specs=(pl.BlockSpec((tr, 3 * VAL_WIDTH), lambda i: (rev(i), 0)),
                   pl.BlockSpec((tr, FZ_PAD), lambda i: (rev(i), 0)),
                   pl.BlockSpec((1, HEAD_V), lambda i: (0, 0)),
                   pl.BlockSpec((FZ_PAD, KEY_WIDTH), lambda i: (0, 0)),
                   pl.BlockSpec((1, KEY_WIDTH), lambda i: (0, 0))),
        scratch_shapes=[pltpu.VMEM((HEAD_V, KEY_WIDTH), F32),
                        pltpu.VMEM((tr, KEY_WIDTH), F32),
                        pltpu.VMEM((tr, KEY_WIDTH), F32),
                        pltpu.VMEM((tr, KEY_WIDTH), F32),
                        pltpu.VMEM((tr, VAL_WIDTH), F32),
                        pltpu.VMEM((tr, VAL_WIDTH), BF16)],
        input_output_aliases={11: 0},
        compiler_params=_cparams(("arbitrary",)),
    )(proj, proj, proj, proj, proj, doa, states, states, wfg2p, bfg, gn, dproj)


def _put_fz(dproj, dfz, *, s):
    tr = min(512, s)

    def body(dfz_ref, dproj_in_ref, o_ref):
        o_ref[...] = dfz_ref[...]

    return pl.pallas_call(
        body, name="put_dfz",
        out_shape=jax.ShapeDtypeStruct((s, PROJ_W), BF16),
        grid=(s // tr,),
        in_specs=[pl.BlockSpec((tr, FZ_PAD), lambda i: (i, 0)), ANY],
        out_specs=pl.BlockSpec((tr, FZ_PAD), lambda i: (i, COL_FZ // FZ_PAD)),
        input_output_aliases={1: 0},
        compiler_params=_cparams(("parallel",)),
    )(dfz, dproj)


_ADAM_C1 = 1.0 / (1.0 - ADAM_B1 ** ADAM_STEP)
_ADAM_C2 = 1.0 / (1.0 - ADAM_B2 ** ADAM_STEP)


def _adamw_math(wv, gv, mv, vv):
    nm = ADAM_B1 * mv + (1.0 - ADAM_B1) * gv
    nv = ADAM_B2 * vv + (1.0 - ADAM_B2) * (gv * gv)
    delta = -ADAM_LR * ((nm * _ADAM_C1) / (jnp.sqrt(nv * _ADAM_C2) + ADAM_EPS) + ADAM_WD * wv)
    return delta, nm, nv


def _adamw(w, g, m, v, *, name):
    shape = w.shape
    if w.ndim == 1:
        w, g, m, v = (t.reshape(1, 1, -1) for t in (w, g, m, v))
    elif w.ndim == 2:
        w, g, m, v = (t.reshape((1,) + t.shape) for t in (w, g, m, v))
    l, a, b = w.shape

    def body(w_ref, g_ref, m_ref, v_ref, d_ref, nm_ref, nv_ref):
        d_ref[...], nm_ref[...], nv_ref[...] = _adamw_math(w_ref[...], g_ref[...], m_ref[...], v_ref[...])

    blk = pl.BlockSpec((1, a, b), lambda li: (li, 0, 0))
    outs = pl.pallas_call(
        body, name=name,
        out_shape=tuple(jax.ShapeDtypeStruct((l, a, b), F32) for _ in range(3)),
        grid=(l,),
        in_specs=[blk, blk, blk, blk],
        out_specs=(blk, blk, blk),
        compiler_params=_cparams(("parallel",)),
    )(w, g, m, v)
    return tuple(o.reshape(shape) for o in outs)


def _adamw_layers(w, reduced, received, on_core0, m, v, *, name):
    l, a, b = w.shape
    assert l == DEPTH == 2
    ta = _row_tile(a, F32_SUBLANES, MAX_ADAMW_ROWS)

    def body(flag_ref, w_ref, r0_ref, o0_ref, r1_ref, o1_ref, m_ref, v_ref, g_ref, d_ref, nm_ref, nv_ref):
        core0 = flag_ref[...] > 0.5
        g0 = jnp.where(core0, r0_ref[...], o0_ref[...])
        g1 = jnp.where(core0, o1_ref[...], r1_ref[...])
        gv = jnp.where(pl.program_id(0) == 0, g0, g1)
        g_ref[0] = gv
        d_ref[0], nm_ref[0], nv_ref[0] = _adamw_math(w_ref[0], gv, m_ref[0], v_ref[0])

    blk = pl.BlockSpec((1, ta, b), lambda li, ai: (li, ai, 0))

    def grad_block(layer):
        return pl.BlockSpec((ta, b), lambda li, ai: (jnp.where(li == layer, ai, 0), 0))

    g0blk, g1blk = grad_block(0), grad_block(1)
    return pl.pallas_call(
        body, name=name,
        out_shape=tuple(jax.ShapeDtypeStruct((l, a, b), F32) for _ in range(4)),
        grid=(l, a // ta),
        in_specs=[pl.BlockSpec((1, 1), lambda li, ai: (0, 0)), blk, g0blk, g0blk, g1blk, g1blk, blk, blk],
        out_specs=(blk, blk, blk, blk),
        compiler_params=_cparams(("parallel", "parallel")),
    )(on_core0, w, reduced[0], received[0], reduced[1], received[1], m, v)


MESH_ID = pl.DeviceIdType.MESH
ANY = pl.BlockSpec(memory_space=pl.ANY)


def _position():
    x, y, c = lax.axis_index("x"), lax.axis_index("y"), lax.axis_index("c")
    chips = [(1 - x, y), (x, 1 - y), (1 - x, 1 - y)]
    return x, y, c, chips


def _chip_index(xy):
    return 2 * xy[0] + xy[1]


HBM = pl.BlockSpec(memory_space=pltpu.HBM)
SEM = pl.BlockSpec(memory_space=pltpu.SEMAPHORE)
DATAFLOW_EFFECT = pltpu.SideEffectType.DATAFLOW_SIDE_EFFECTING
TOKEN_SHAPE = (8, LANE)


def _landing(shape, dtype):
    return pltpu.with_memory_space_constraint(lax.empty(shape, dtype), pltpu.HBM)


def _split_start(bufs, sem_shape, issue, *, name, after=None):
    n = len(bufs)
    n_in = n + (after is not None)

    def body(*refs):
        issue(refs[:n], refs[n_in], refs[n_in + 1])
        token = refs[-1]
        token[...] = jnp.zeros_like(token)

    operands = [pltpu.with_memory_space_constraint(t, pltpu.HBM) for t in bufs]
    outs = pl.pallas_call(
        body, name=name,
        out_shape=(pltpu.SemaphoreType.DMA(sem_shape), pltpu.SemaphoreType.DMA(sem_shape),
                   *[pltpu.HBM(t.shape, t.dtype) for t in bufs], jax.ShapeDtypeStruct(TOKEN_SHAPE, F32)),
        in_specs=[HBM] * n + [ANY] * (after is not None),
        out_specs=(SEM, SEM, *[HBM] * n, pl.BlockSpec(memory_space=pltpu.VMEM)),
        input_output_aliases={i: 2 + i for i in range(n)},
        compiler_params=pltpu.CompilerParams(has_side_effects=DATAFLOW_EFFECT),
    )(*operands, *([after] if after is not None else []))
    return outs[0], outs[1], list(outs[2:2 + n]), outs[-1]


def _split_wait(started, after, settle, *, name):
    send_sems, recv_sems, bufs, _ = started
    n = len(bufs)
    afters = tuple(after) if isinstance(after, (tuple, list)) else (after,)

    def body(*refs):
        settle(refs[:n], refs[n], refs[n + 1])

    outs = pl.pallas_call(
        body, name=name,
        out_shape=tuple(pltpu.HBM(t.shape, t.dtype) for t in bufs),
        in_specs=[HBM] * n + [SEM, SEM] + [ANY] * len(afters),
        out_specs=tuple([HBM] * n),
        input_output_aliases={i: i for i in range(n)},
        compiler_params=pltpu.CompilerParams(has_side_effects=DATAFLOW_EFFECT),
    )(*bufs, send_sems, recv_sems, *afters)
    return list(outs)


def _to_sibling(bufs, n, send_sems, recv_sems):
    x, y, c, _ = _position()
    return [pltpu.make_async_remote_copy(
        src_ref=bufs[a], dst_ref=bufs[n + a], send_sem=send_sems.at[a], recv_sem=recv_sems.at[a],
        device_id=(x, y, 1 - c), device_id_type=MESH_ID) for a in range(n)]


def _sibling_push_start(layer, sender_is_reducer, arrays, *, name):
    n = len(arrays)
    sender = layer if sender_is_reducer else 1 - layer

    def issue(bufs, send_sems, recv_sems):
        @pl.when(lax.axis_index("c") == sender)
        def _():
            for cp in _to_sibling(bufs, n, send_sems, recv_sems):
                cp.start()

    lands = [_landing(t.shape, t.dtype) for t in arrays]
    return _split_start(list(arrays) + lands, (n,), issue, name=name)


def _sibling_push_wait(layer, sender_is_reducer, started, after, *, name):
    n = len(started[2]) // 2
    sender = layer if sender_is_reducer else 1 - layer

    def settle(bufs, send_sems, recv_sems):
        c = lax.axis_index("c")

        @pl.when(c == sender)
        def _():
            for cp in _to_sibling(bufs, n, send_sems, recv_sems):
                cp.wait_send()

        @pl.when(c != sender)
        def _():
            for cp in _to_sibling(bufs, n, send_sems, recv_sems):
                cp.wait_recv()

    outs = _split_wait(started, after, settle, name=name)
    return outs[:n], outs[n:]


def _chip_copies(bufs, n, send_sems, recv_sems):
    x, y, c, chips = _position()
    return [pltpu.make_async_remote_copy(
        src_ref=bufs[a].at[_chip_index(chips[k])], dst_ref=bufs[n + a].at[k],
        send_sem=send_sems.at[3 * a + k], recv_sem=recv_sems.at[3 * a + k],
        device_id=(chips[k][0], chips[k][1], c), device_id_type=MESH_ID)
        for a in range(n) for k in range(3)]


def _chip_exchange_start(layer, parts, *, name):
    n = len(parts)

    def issue(bufs, send_sems, recv_sems):
        @pl.when(lax.axis_index("c") == layer)
        def _():
            for cp in _chip_copies(bufs, n, send_sems, recv_sems):
                cp.start()

    lands = [_landing((3,) + t.shape[1:], t.dtype) for t in parts]
    return _split_start(list(parts) + lands, (3 * n,), issue, name=name)


def _chip_exchange_wait(layer, started, after, *, name):
    n = len(started[2]) // 2

    def settle(bufs, send_sems, recv_sems):
        @pl.when(lax.axis_index("c") == layer)
        def _():
            for cp in _chip_copies(bufs, n, send_sems, recv_sems):
                cp.wait()

    outs = _split_wait(started, after, settle, name=name)
    return outs[:n], outs[n:]


def _gather_copies(layer, bufs, n, send_sems, recv_sems, arriving):
    x, y, c, chips = _position()
    me = _chip_index((x, y))
    return [pltpu.make_async_remote_copy(
        src_ref=bufs[a], dst_ref=bufs[n + a].at[_chip_index(chips[k]) if arriving else me],
        send_sem=send_sems.at[3 * a + k], recv_sem=recv_sems.at[3 * a + k],
        device_id=(chips[k][0], chips[k][1], c), device_id_type=MESH_ID)
        for a in range(n) for k in range(3)]


def _gather_start(layer, shards, *, name, after=None):
    n = len(shards)

    def issue(bufs, send_sems, recv_sems):
        @pl.when(lax.axis_index("c") == layer)
        def _():
            for cp in _gather_copies(layer, bufs, n, send_sems, recv_sems, False):
                cp.start()

    lands = [_landing((N_CHIPS,) + t.shape, t.dtype) for t in shards]
    return _split_start(list(shards) + lands, (3 * n,), issue, name=name, after=after)


def _gather_wait(layer, started, after, *, name):
    n = len(started[2]) // 2

    def settle(bufs, send_sems, recv_sems):
        @pl.when(lax.axis_index("c") == layer)
        def _():
            for cp in _gather_copies(layer, bufs, n, send_sems, recv_sems, False):
                cp.wait_send()
            for cp in _gather_copies(layer, bufs, n, send_sems, recv_sems, True):
                cp.wait_recv()

    return _split_wait(started, after, settle, name=name)[n:]


def _handoff_copies(bufs, n, send_sems, recv_sems):
    x, y, c, chips = _position()
    out = []
    for a in range(n):
        for k in range(3):
            slot = bufs[a].at[_chip_index(chips[k])]
            out.append(pltpu.make_async_remote_copy(
                src_ref=slot, dst_ref=slot, send_sem=send_sems.at[3 * a + k], recv_sem=recv_sems.at[3 * a + k],
                device_id=(x, y, 1 - c), device_id_type=MESH_ID))
    return out


def _handoff_start(layer, gathered, *, name):
    n = len(gathered)

    def issue(bufs, send_sems, recv_sems):
        @pl.when(lax.axis_index("c") == layer)
        def _():
            for cp in _handoff_copies(bufs, n, send_sems, recv_sems):
                cp.start()

    return _split_start(list(gathered), (3 * n,), issue, name=name)


def _handoff_wait(layer, started, after, *, name):
    n = len(started[2])

    def settle(bufs, send_sems, recv_sems):
        c = lax.axis_index("c")

        @pl.when(c == layer)
        def _():
            for cp in _handoff_copies(bufs, n, send_sems, recv_sems):
                cp.wait_send()

        @pl.when(c != layer)
        def _():
            for cp in _handoff_copies(bufs, n, send_sems, recv_sems):
                cp.wait_recv()

    return _split_wait(started, after, settle, name=name)


F32_SUBLANES = 8
BF16_SUBLANES = 16
MAX_STREAM_ROWS = 704
MAX_ADAMW_ROWS = 384


def _row_tile(a, sublanes=BF16_SUBLANES, max_rows=MAX_STREAM_ROWS):
    best = None
    for cand in range(sublanes, min(a, max_rows) + 1, sublanes):
        if a % cand == 0:
            best = cand
    assert best is not None, a
    return best


def _pair_sum(mine, sib, active, *, name):
    nchip, a, b = sib.shape
    ta = _row_tile(a)

    def body(act_ref, m_ref, s_ref, o_ref):
        @pl.when(act_ref[0] == 1)
        def _():
            o_ref[...] = (m_ref[...].astype(F32) + s_ref[...].astype(F32)).astype(BF16)

    blk = pl.BlockSpec((1, ta, b), lambda j, r, act: (j * act[0], r * act[0], 0))
    return pl.pallas_call(
        body, name=name,
        out_shape=jax.ShapeDtypeStruct((nchip, a, b), BF16),
        grid_spec=pltpu.PrefetchScalarGridSpec(
            num_scalar_prefetch=1, grid=(nchip, a // ta), in_specs=[blk, blk], out_specs=blk),
        compiler_params=_cparams(("arbitrary", "arbitrary")),
    )(active, mine, sib)


def _total_sum(own, recv, active, *, name):
    a, b = own.shape
    ta = _row_tile(a)

    def body(act_ref, o_ref, r_ref, t_ref):
        @pl.when(act_ref[0] == 1)
        def _():
            acc = o_ref[...].astype(F32)
            for k in range(3):
                acc = acc + r_ref[k].astype(F32)
            t_ref[...] = acc

    blk = pl.BlockSpec((ta, b), lambda r, act: (r * act[0], 0))
    return pl.pallas_call(
        body, name=name,
        out_shape=jax.ShapeDtypeStruct((a, b), F32),
        grid_spec=pltpu.PrefetchScalarGridSpec(
            num_scalar_prefetch=1, grid=(a // ta,),
            in_specs=[blk, pl.BlockSpec((3, ta, b), lambda r, act: (0, r * act[0], 0))], out_specs=blk),
        compiler_params=_cparams(("arbitrary",)),
    )(active, own, recv)


def _all_reduce_small(packed):
    rows, width = packed.shape

    def body(x_ref, out_ref, gath, send_sems, recv_sems, local_sem):
        x, y, c, chips = _position()
        me, sibling = (x, y, c), (x, y, 1 - c)

        def slot(px, py, pc):
            return gath.at[4 * px + 2 * py + pc]

        def copy(k, block, to, src=None):
            return pltpu.make_async_remote_copy(
                src_ref=slot(*block) if src is None else src, dst_ref=slot(*block),
                send_sem=send_sems.at[k], recv_sem=recv_sems.at[k], device_id=to, device_id_type=MESH_ID)

        mine = pltpu.make_async_copy(x_ref, slot(*me), local_sem)
        mine.start()
        first = [copy(0, me, sibling, src=x_ref)]
        first += [copy(1 + j, me, (*chip, c), src=x_ref) for j, chip in enumerate(chips)]
        for cp in first:
            cp.start()
        passed = [copy(4 + j, (*chip, c), sibling) for j, chip in enumerate(chips)]
        for j, chip in enumerate(chips):
            copy(1 + j, (*chip, c), me).wait_recv()
            passed[j].start()
        copy(0, sibling, me).wait_recv()
        for j, chip in enumerate(chips):
            copy(4 + j, (*chip, 1 - c), me).wait_recv()
        for cp in first + passed:
            cp.wait_send()
        mine.wait()
        acc = gath[0]
        for d in range(1, N_DEV):
            acc = acc + gath[d]
        out_ref[...] = acc

    return pl.pallas_call(
        body, name="all_reduce_small",
        out_shape=jax.ShapeDtypeStruct((rows, width), F32),
        in_specs=[pl.BlockSpec(memory_space=pltpu.VMEM)],
        out_specs=pl.BlockSpec(memory_space=pltpu.VMEM),
        scratch_shapes=[pltpu.VMEM((N_DEV, rows, width), F32), pltpu.SemaphoreType.DMA((7,)),
                        pltpu.SemaphoreType.DMA((7,)), pltpu.SemaphoreType.DMA],
    )(packed)


def _mixer_forward_branches(x0, h, w, s, dep=None, before_conv=None):
    proj = _matmul(h, w["w_in_t"], mode="nt", out_dtype=BF16, tm=1024, tn=1664, tk=1024, name="proj_fwd",
                   dep=dep)
    oa, states = _gla_fwd(proj, w["w_fg2"], w["b_fg"], w["gla_norm_g"], s=s)
    conv_wb = w["conv_wb"]
    if before_conv is not None:
        conv_wb = conv_wb + before_conv(oa)[0, 0]
    cb_in = _conv_fwd(proj, conv_wb, s=s)
    return dict(x0=x0, h=h, proj=proj, oa=oa, states=states, cb_in=cb_in)


def _mixer_forward_out(sv, w, s, dep=None):
    ya, yb, mix, x1 = _mixer_out_fwd(sv["oa"], sv["cb_in"], sv["proj"], sv["x0"], w["w_oa"], w["w_ob"], w["w_o"],
                                     s=s, dep=dep)
    return x1, dict(sv, ya=ya, yb=yb, mix=mix)


def _ffn_forward_hidden(x1, w, s, dep=None):
    tr = min(256, s)
    d, f = D_MODEL, FFN_HIDDEN
    n_dep = dep is not None

    def body(*refs):
        x_ref, g_ref, wg_ref, wu_ref, h2_ref, gt_ref, up_ref, hid_ref = refs[n_dep:]
        xv = x_ref[...]
        rs = lax.rsqrt(jnp.mean(xv * xv, axis=-1, keepdims=True) + NORM_EPS)
        h2 = (xv * rs * g_ref[...]).astype(BF16)
        h2_ref[...] = h2
        gt = lax.dot_general(h2, wg_ref[...], _DOT_DIMS["nt"], preferred_element_type=F32).astype(BF16)
        up = lax.dot_general(h2, wu_ref[...], _DOT_DIMS["nt"], preferred_element_type=F32).astype(BF16)
        gt_ref[...] = gt
        up_ref[...] = up
        gv = gt.astype(F32)
        hid_ref[...] = (gv * _sigmoid(gv) * up.astype(F32)).astype(BF16)

    row_d = pl.BlockSpec((tr, d), lambda i: (i, 0))
    row_f = pl.BlockSpec((tr, f), lambda i: (i, 0))
    weight = pl.BlockSpec((f, d), lambda i: (0, 0))
    tok, tok_specs = _token_spec(dep)
    h2, gt, up, hid = pl.pallas_call(
        body, name="ffn_hidden_fwd",
        out_shape=(jax.ShapeDtypeStruct((s, d), BF16),) + tuple(jax.ShapeDtypeStruct((s, f), BF16) for _ in range(3)),
        grid=(s // tr,),
        in_specs=tok_specs + [row_d, pl.BlockSpec((1, d), lambda i: (0, 0)), weight, weight],
        out_specs=(row_d, row_f, row_f, row_f),
        compiler_params=_cparams(("parallel",)),
    )(*tok, x1, w["norm2_g"].reshape(1, d), w["w_gate_t"], w["w_up_t"])
    return dict(x1=x1, h2=h2, gt=gt, up=up, hid=hid)


def _ffn_forward_out(sv, w, dep=None):
    return _matmul(sv["hid"], w["w_ffn_down"], mode="nn", out_dtype=F32, tm=1024, tn=1024, tk=FFN_HIDDEN,
                   name="ffn_out_fwd", add=sv["x1"], dep=dep)


def _ffn_backward_grads(dx2b, w, sv, s):
    g = {}
    dhid = _matmul(dx2b, w["w_ffn_down"], mode="nt", out_dtype=BF16, tm=1024, tn=FFN_HIDDEN, tk=1024,
                   name="ffn_out_bwd")
    g["w_ffn_down"] = _matmul(sv["hid"], dx2b, mode="tn", out_dtype=BF16, tm=1408, tn=1024, tk=1024,
                              name="ffn_out_wgrad")
    dgt, dup = _swiglu_bwd(sv["gt"], sv["up"], dhid, s=s)
    g["w_gate_t"] = _matmul(dgt, sv["h2"], mode="tn", out_dtype=BF16, tm=1408, tn=1024, tk=1024,
                            name="ffn_gate_wgrad")
    g["w_up_t"] = _matmul(dup, sv["h2"], mode="tn", out_dtype=BF16, tm=1408, tn=1024, tk=1024,
                          name="ffn_up_wgrad")
    return g, dgt, dup, dhid


def _ffn_backward_input(dgt, dup, dx2, w, sv, dep=None):
    dh2 = _matmul(dgt, w["w_gate_t"], mode="nn", out_dtype=BF16, tm=1024, tn=1024, tk=FFN_HIDDEN,
                  name="ffn_gate_bwd", dep=dep)
    return _matmul_norm_bwd(dup, w["w_up_t"], sv["x1"], w["norm2_g"], dx2, tm=512, tk=FFN_HIDDEN,
                            name="ffn_up_bwd_norm2_bwd", add=dh2)


def _mixer_backward_branches(dx1b, w, sv, s, dep=None):
    g = {}
    dya, dyb, dcb, doa, dproj = _mixer_out_bwd(dx1b, sv["proj"], sv["ya"], sv["yb"], w["w_oa"], w["w_ob"],
                                               w["w_o"], s=s, dep=dep)
    g["w_o"] = _matmul(sv["mix"], dx1b, mode="tn", out_dtype=BF16, tm=1024, tn=1024, tk=2048, name="wo_wgrad")
    g["w_ob"] = _matmul(sv["cb_in"], dyb, mode="tn", out_dtype=BF16, tm=1024, tn=1024, tk=2048, name="yb_wgrad")
    g["w_oa"] = _matmul(sv["oa"], dya, mode="tn", out_dtype=BF16, tm=1024, tn=1024, tk=2048, name="ya_wgrad")
    dproj, g["conv_wb"] = _conv_bwd(sv["proj"], dcb, w["conv_wb"], dproj, s=s)
    dproj, dfz, g["gla_norm_g"], g["w_fg2"], g["b_fg"] = _gla_bwd(
        sv["proj"], doa, sv["states"], w["w_fg2"], w["b_fg"], w["gla_norm_g"], dproj, s=s)
    return g, _put_fz(dproj, dfz, s=s)


def _proj_wgrad(dproj, sv, dep=None):
    return _matmul(dproj, sv["h"], mode="tn", out_dtype=BF16, tm=1664, tn=1024, tk=1024, name="proj_wgrad",
                   dep=dep)


def _proj_bwd(dproj, w, x0, dres, dep=None):
    return _matmul_norm_bwd(dproj, w["w_in_t"], x0, w["norm1_g"], dres, tm=512, tk=1664,
                            name="proj_bwd_norm1_bwd", dep=dep)


def _cols_from_chips(t):
    return jnp.transpose(t, (1, 0, 2)).reshape(t.shape[1], -1)


W_IN_ROWS = IN_WIDTH // N_CHIPS
W_IN_ROWS_PAD = -(-W_IN_ROWS // BF16_SUBLANES) * BF16_SUBLANES


def _w_in_t_shard(t):
    return jnp.pad(jnp.transpose(t, (0, 2, 1)), ((0, 0), (0, W_IN_ROWS_PAD - W_IN_ROWS), (0, 0)))


def _w_in_t_unshard(t):
    return jnp.transpose(t[:, :W_IN_ROWS], (0, 2, 1))


def _w_in_row_segments():
    runs = [(0, 0, FZ_ORIG), (FZ_ORIG, FZ_ORIG + GATE_RANK, COL_FZ - FZ_ORIG), (COL_FZ, FZ_ORIG, GATE_RANK)]
    out = []
    for kernel_row, ref_row, length in runs:
        while length:
            chip, local = divmod(ref_row, W_IN_ROWS)
            n = min(length, W_IN_ROWS - local)
            out.append((kernel_row, chip * W_IN_ROWS_PAD + local, n))
            kernel_row, ref_row, length = kernel_row + n, ref_row + n, length - n
    return out


def _permute_rows(src, n_out, segments, *, name):
    n_src, width = src.shape
    block = _row_tile(n_out)
    window = block + BF16_SUBLANES
    assert n_src >= window and src.dtype == BF16

    def body(src_ref, out_ref):
        rows = lax.broadcasted_iota(jnp.int32, (block, window), 0)
        cols = lax.broadcasted_iota(jnp.int32, (block, window), 1)
        for b0 in range(0, n_out, block):
            acc = None
            for o, s, n in segments:
                lo, hi = max(o, b0), min(o + n, b0 + block)
                if lo >= hi:
                    continue
                first = s + (lo - o)
                base = max(0, min(first // BF16_SUBLANES * BF16_SUBLANES, n_src - window))
                shift = (first - base) - (lo - b0)
                pick = (cols == rows + shift) & (rows >= lo - b0) & (rows < hi - b0)
                part = jnp.dot(jnp.where(pick, 1.0, 0.0).astype(BF16), src_ref[base:base + window, :],
                               preferred_element_type=F32)
                acc = part if acc is None else acc + part
            out_ref[b0:b0 + block, :] = (jnp.zeros((block, width), F32) if acc is None else acc).astype(BF16)

    return pl.pallas_call(
        body, name=name,
        out_shape=jax.ShapeDtypeStruct((n_out, width), BF16),
        in_specs=[pl.BlockSpec(memory_space=pltpu.VMEM)],
        out_specs=pl.BlockSpec(memory_space=pltpu.VMEM),
        compiler_params=_cparams(),
    )(src)


def _w_in_t_to_kernel(t):
    return _permute_rows(t.reshape(N_CHIPS * W_IN_ROWS_PAD, D_MODEL), PROJ_W, _w_in_row_segments(),
                         name="w_in_to_kernel_rows")


def _w_in_t_from_kernel(g):
    segments = [(slab, kernel_row, n) for kernel_row, slab, n in _w_in_row_segments()]
    out = _permute_rows(g, N_CHIPS * W_IN_ROWS_PAD, segments, name="w_in_to_chip_rows")
    return out.reshape(N_CHIPS, W_IN_ROWS_PAD, D_MODEL)


def kernel(x, norm1_g, w_in, w_fg2, b_fg, gla_norm_g, w_oa, conv_w, conv_b, w_ob, w_o, norm2_g, w_ffn_gate, w_ffn_up, w_ffn_down, final_g, loss_target, m_norm1_g, m_w_in, m_w_fg2, m_b_fg, m_gla_norm_g, m_w_oa, m_conv_w, m_conv_b, m_w_ob, m_w_o, m_norm2_g, m_w_ffn_gate, m_w_ffn_up, m_w_ffn_down, m_final_g, v_norm1_g, v_w_in, v_w_fg2, v_b_fg, v_gla_norm_g, v_w_oa, v_conv_w, v_conv_b, v_w_ob, v_w_o, v_norm2_g, v_w_ffn_gate, v_w_ffn_up, v_w_ffn_down, v_final_g):
    cx_ = lax.axis_index("x")
    cy_ = lax.axis_index("y")
    cc_ = lax.axis_index("c")
    me = 2 * cx_ + cy_
    on_core0 = jnp.where(cc_ == 0, 1.0, 0.0).astype(F32).reshape(1, 1)

    def swap(t):
        return jnp.swapaxes(t, 1, 2)

    big_names = ["w_in", "w_oa", "w_ob", "w_o", "w_ffn_gate", "w_ffn_up", "w_ffn_down"]
    views = dict(
        w_in=tuple(_w_in_t_shard(t) for t in (w_in, m_w_in, v_w_in)),
        w_oa=(w_oa, m_w_oa, v_w_oa), w_ob=(w_ob, m_w_ob, v_w_ob), w_o=(w_o, m_w_o, v_w_o),
        w_ffn_gate=tuple(swap(t) for t in (w_ffn_gate, m_w_ffn_gate, v_w_ffn_gate)),
        w_ffn_up=tuple(swap(t) for t in (w_ffn_up, m_w_ffn_up, v_w_ffn_up)),
        w_ffn_down=(w_ffn_down, m_w_ffn_down, v_w_ffn_down))
    from_view = dict(w_in=_w_in_t_unshard, w_ffn_gate=swap, w_ffn_up=swap)

    s = x.shape[1]

    proj_names = ["w_in"]
    first_names = ["w_in", "w_fg2", "conv_w"]
    rest_names = ["w_oa", "w_ob", "w_o", "w_ffn_gate", "w_ffn_up", "w_ffn_down"]
    mixer_names = ["w_oa", "w_ob", "w_o"]
    ffn_names = ["w_ffn_gate", "w_ffn_up", "w_ffn_down"]
    weight_key = dict(w_in="w_in_t", w_ffn_gate="w_gate_t", w_ffn_up="w_up_t")
    full_shape = dict(w_oa=(VAL_WIDTH, D_MODEL), w_ob=(CONV_CH, D_MODEL), w_o=(D_MODEL, D_MODEL),
                      w_ffn_gate=(FFN_HIDDEN, D_MODEL), w_ffn_up=(FFN_HIDDEN, D_MODEL),
                      w_ffn_down=(FFN_HIDDEN, D_MODEL))

    conv_w_p = jnp.pad(conv_w, ((0, 0), (0, 8 - conv_w.shape[1]), (0, 0)))
    small_shards = dict(w_fg2=w_fg2, conv_w=conv_w_p)

    def shards_of(l, names):
        return [small_shards[n][l] if n in small_shards else views[n][0][l].astype(BF16) for n in names]

    def small_weights(l):
        return dict(norm1_g=norm1_g[l], norm2_g=norm2_g[l], b_fg=b_fg[l].reshape(1, KEY_WIDTH),
                    gla_norm_g=gla_norm_g[l].reshape(1, HEAD_V))

    def full_weights(l, names, gathered, shards):
        out = {}
        for n, g, t in zip(names, gathered, shards):
            g = lax.dynamic_update_index_in_dim(g, t[None], me, axis=0)
            if n == "w_in":
                out["w_in_t"] = _w_in_t_to_kernel(g)
            elif n == "w_fg2":
                out[n] = jnp.pad(_cols_from_chips(g), ((0, FZ_PAD - GATE_RANK), (0, 0))).astype(BF16)
            elif n == "conv_w":
                out["conv_wb"] = (jnp.pad(_cols_from_chips(g)[:3], ((0, 5), (0, 0)))
                                  + jnp.pad(conv_b[l].reshape(1, CONV_CH), ((3, 4), (0, 0))))
            else:
                out[weight_key.get(n, n)] = g.reshape(full_shape[n])
        return out

    def per_chip(g, names):
        out = []
        for n in names:
            t = g[weight_key.get(n, n)]
            out.append(_w_in_t_from_kernel(t) if n == "w_in"
                       else t.reshape(N_CHIPS, t.shape[0] // N_CHIPS, t.shape[1]))
        return out

    def gather(l, names, tag, after=None):
        shards = shards_of(l, names)
        return shards, _gather_start(l, shards, name=f"gather_{tag}_start", after=after)

    def gathered_to_sibling(l, started, after, tag):
        return _handoff_start(l, _gather_wait(l, started, after, name=f"gather_{tag}_wait"),
                              name=f"handoff_{tag}_start")

    def feed(l, g, names, tag):
        return _sibling_push_start(l, False, per_chip(g, names), name=f"feed_{tag}_start")

    def reduces(l):
        return jnp.where(cc_ == l, 1, 0).astype(jnp.int32).reshape(1)

    def pair_and_exchange(l, fed, after, names, tag):
        mine, sib = _sibling_push_wait(l, False, fed, after, name=f"feed_{tag}_wait")
        parts = [_pair_sum(a, b, reduces(l), name=f"pair_sum_{tag}_{n}") for n, a, b in zip(names, mine, sib)]
        return _chip_exchange_start(l, parts, name=f"exchange_{tag}_start")

    def total_and_share(l, swapped, after, names, tag):
        parts, recvs = _chip_exchange_wait(l, swapped, after, name=f"exchange_{tag}_wait")
        owns = [lax.dynamic_index_in_dim(p, me, 0, keepdims=False) for p in parts]
        totals = [_total_sum(o, rc, reduces(l), name=f"total_{tag}_{n}") for n, o, rc in zip(names, owns, recvs)]
        return _sibling_push_start(l, True, totals, name=f"share_{tag}_start")

    def shared(l, sharing, after, names, tag):
        totals, others = _sibling_push_wait(l, True, sharing, after, name=f"share_{tag}_wait")
        return {n: (t, o) for n, t, o in zip(names, totals, others)}

    def branches_under_gather(l, xin, h, w, tag, shards_r, started_r):
        box = {}

        def before_conv(oa):
            box["handoff"] = gathered_to_sibling(l, started_r, oa, tag)
            return box["handoff"][3]

        sv = _mixer_forward_branches(xin, h, w, s, dep=started_r[3], before_conv=before_conv)
        rest = _handoff_wait(l, box["handoff"], sv["cb_in"], name=f"handoff_{tag}_wait")
        w.update(full_weights(l, rest_names, rest, shards_r))
        return sv

    shards_p, started = gather(0, first_names, "p0")
    h0 = _rms_fwd(x[0], norm1_g[0], name="rms1_fwd")
    later_shards = shards_of(0, rest_names) + shards_of(1, first_names) + shards_of(1, rest_names)
    started = gathered_to_sibling(0, started, (h0, views["w_in"][1], *later_shards), "p0")
    w0 = small_weights(0)
    w0.update(full_weights(0, first_names, _handoff_wait(0, started, views["w_in"][2], name="handoff_p0_wait"),
                           shards_p))
    sv0 = branches_under_gather(0, x[0], h0, w0, "r0", *gather(0, rest_names, "r0", after=w0["w_in_t"]))

    shards_p1, started = gather(1, first_names, "p1", after=w0["w_o"])
    x1, sv0m = _mixer_forward_out(sv0, w0, s, dep=started[3])
    sv0f = _ffn_forward_hidden(x1, w0, s)
    handoff_1 = gathered_to_sibling(1, started, sv0f["hid"], "p1")
    rest_1 = gather(1, rest_names, "r1", after=handoff_1[2][0])
    x2 = _ffn_forward_out(sv0f, w0, dep=handoff_1[3] + rest_1[1][3])
    w1 = small_weights(1)
    w1.update(full_weights(1, first_names, _handoff_wait(1, handoff_1, x2, name="handoff_p1_wait"), shards_p1))

    h1 = _rms_fwd(x2, norm1_g[1], name="rms1_fwd")
    x3, sv1m = _mixer_forward_out(branches_under_gather(1, x2, h1, w1, "r1", *rest_1), w1, s)
    sv1f = _ffn_forward_hidden(x3, w1, s)
    x4 = _ffn_forward_out(sv1f, w1)
    loss_local, dx, dxb, dgf = _loss_head(x4, final_g, loss_target[0])
    loss = lax.psum(loss_local[0, 0], ("x", "y", "c"))

    g1, dgt, dup, _ = _ffn_backward_grads(dxb, w1, sv1f, s)
    dx_mid, dxb_mid, g1["norm2_g"] = _ffn_backward_input(dgt, dup, dx, w1, sv1f)
    gm, dproj = _mixer_backward_branches(dxb_mid, w1, sv1m, s)
    g1.update(gm)
    g1["w_in_t"] = _proj_wgrad(dproj, sv1m)
    fed_1 = feed(1, g1, big_names, "l1")
    dx, dxb, g1["norm1_g"] = _proj_bwd(dproj, w1, sv1m["x0"], dx_mid, dep=fed_1[3])

    g0, dgt, dup, dhid = _ffn_backward_grads(dxb, w0, sv0f, s)
    swap_1 = pair_and_exchange(1, fed_1, dhid, big_names, "l1")
    fed_f = feed(0, g0, ffn_names, "f0")
    dx_mid, dxb_mid, g0["norm2_g"] = _ffn_backward_input(dgt, dup, dx, w0, sv0f, dep=swap_1[3] + fed_f[3])
    swap_f = pair_and_exchange(0, fed_f, dx_mid, ffn_names, "f0")
    gm, dproj = _mixer_backward_branches(dxb_mid, w0, sv0m, s, dep=swap_f[3])
    g0.update(gm)
    share_1 = total_and_share(1, swap_1, dproj, big_names, "l1")
    share_f = total_and_share(0, swap_f, dproj, ffn_names, "f0")
    fed_m = feed(0, g0, mixer_names, "m0")
    g0["w_in_t"] = _proj_wgrad(dproj, sv0m, dep=share_1[3] + share_f[3] + fed_m[3])
    fed_p = feed(0, g0, proj_names, "p0")
    swap_m = pair_and_exchange(0, fed_m, g0["w_in_t"], mixer_names, "m0")
    grad_x, _, g0["norm1_g"] = _proj_bwd(dproj, w0, sv0m["x0"], dx_mid, dep=fed_p[3] + swap_m[3])
    reduced1 = shared(1, share_1, grad_x, big_names, "l1")
    reduced0 = shared(0, share_f, grad_x, ffn_names, "f0")

    swap_p = pair_and_exchange(0, fed_p, grad_x, proj_names, "p0")
    share_m = total_and_share(0, swap_m, swap_p[3], mixer_names, "m0")
    reduced0.update(shared(0, share_m, share_m[3], mixer_names, "m0"))
    grads = [g0, g1]

    def small_rows(t):
        return t.reshape(-1, D_MODEL)

    def tile_rows(t):
        return jnp.pad(t, ((0, -t.shape[0] % F32_SUBLANES), (0, 0)))

    g0, g1 = grads
    pieces = [
        jnp.concatenate([g0["norm1_g"], g1["norm1_g"]], axis=0),
        jnp.concatenate([g0["norm2_g"], g1["norm2_g"]], axis=0),
        dgf,
        small_rows(jnp.concatenate([g0["b_fg"], g1["b_fg"]], axis=1)),
        small_rows(jnp.concatenate([g0["gla_norm_g"], g1["gla_norm_g"],
                                    jnp.zeros((1, D_MODEL - 2 * HEAD_V), F32)], axis=1)),
        jnp.concatenate([g0["conv_wb"][3:4], g1["conv_wb"][3:4]], axis=0),
        jnp.concatenate([g0["conv_wb"][:3], g1["conv_wb"][:3]], axis=0),
        small_rows(jnp.stack([g0["w_fg2"][:GATE_RANK], g1["w_fg2"][:GATE_RANK]])),
    ]
    small = _all_reduce_small(jnp.concatenate([tile_rows(p) for p in pieces], axis=0))
    sg = dict(
        norm1_g=small[0:2], norm2_g=small[8:10], final_g=small[16],
        b_fg=small[24].reshape(DEPTH, KEY_WIDTH), gla_norm_g=small[32, :DEPTH * HEAD_V].reshape(DEPTH, HEAD_V),
        conv_b=small[40:42],
        conv_w=lax.dynamic_slice_in_dim(small[48:54].reshape(DEPTH, 3, CONV_CH), me * (CONV_CH // N_CHIPS),
                                        CONV_CH // N_CHIPS, axis=2),
        w_fg2=lax.dynamic_slice_in_dim(small[56:72].reshape(DEPTH, GATE_RANK, KEY_WIDTH),
                                       me * (KEY_WIDTH // N_CHIPS), KEY_WIDTH // N_CHIPS, axis=2),
    )

    small_params = dict(norm1_g=(norm1_g, m_norm1_g, v_norm1_g), w_fg2=(w_fg2, m_w_fg2, v_w_fg2),
                        b_fg=(b_fg, m_b_fg, v_b_fg), gla_norm_g=(gla_norm_g, m_gla_norm_g, v_gla_norm_g),
                        conv_w=(conv_w, m_conv_w, v_conv_w), conv_b=(conv_b, m_conv_b, v_conv_b),
                        norm2_g=(norm2_g, m_norm2_g, v_norm2_g), final_g=(final_g, m_final_g, v_final_g))
    order = ["norm1_g", "w_in", "w_fg2", "b_fg", "gla_norm_g", "w_oa", "conv_w", "conv_b", "w_ob", "w_o",
             "norm2_g", "w_ffn_gate", "w_ffn_up", "w_ffn_down", "final_g"]
    results = {}

    def update_large(n):
        w_, m_, v_ = views[n]
        outs = _adamw_layers(w_, (reduced0[n][0], reduced1[n][0]), (reduced0[n][1], reduced1[n][1]), on_core0,
                             m_, v_, name="adamw_" + n)
        back = from_view.get(n)
        results[n] = tuple(back(r) for r in outs) if back else outs
        return outs[1]

    for n, (w_, m_, v_) in small_params.items():
        results[n] = (sg[n],) + _adamw(w_, sg[n], m_, v_, name="adamw_" + n)
    updated = tuple(update_large(n) for n in rest_names)
    share_p = total_and_share(0, swap_p, updated, proj_names, "p0")
    reduced0.update(shared(0, share_p, share_p[3], proj_names, "p0"))
    update_large("w_in")
    return (loss, grad_x[None], *[results[n][0] for n in order], *[results[n][1] for n in order],
            *[results[n][2] for n in order], *[results[n][3] for n in order])
```

```python
import functools

import jax
import jax.numpy as jnp
from jax import lax
from jax.experimental import pallas as pl
from jax.experimental.pallas import tpu as pltpu

F32 = jnp.float32
BF16 = jnp.bfloat16

D_MODEL = 1024
DEPTH = 2
CHUNK = 64
GLA_HEADS = 4
KEY_WIDTH = 512
VAL_WIDTH = 1024
HEAD_K = 128
HEAD_V = 256
GATE_RANK = 16
GATE_TAU = 16.0
CONV_CH = 1024
FFN_HIDDEN = 2816
IN_WIDTH = 8208
NORM_EPS = 1e-6
Q_SCALE = HEAD_K ** -0.5
ADAM_LR = 0.001
ADAM_B1 = 0.9
ADAM_B2 = 0.999
ADAM_EPS = 1e-08
ADAM_WD = 0.01
ADAM_STEP = 10

N_CHIPS = 4
N_DEV = 8

LANE = 128
FZ_PAD = LANE
PROJ_W = 8192 + FZ_PAD
COL_Q, COL_K, COL_V, COL_R, COL_GBI, COL_GCI, COL_CX, COL_GA, COL_GB, COL_FZ = (
    0, 512, 1024, 2048, 3072, 4096, 5120, 6144, 7168, 8192)
FZ_ORIG = 3072

VMEM_LIMIT = 52 * 1024 * 1024
HALO = 16


def _cparams(sem=None):
    return pltpu.CompilerParams(dimension_semantics=sem, vmem_limit_bytes=VMEM_LIMIT)


def _sigmoid(x):
    return jax.nn.sigmoid(x)


def _log_sigmoid(x):
    return jnp.minimum(x, 0.0) - jnp.log1p(jnp.exp(-jnp.abs(x)))


_DOT_DIMS = {
    "nn": (((1,), (0,)), ((), ())),
    "nt": (((1,), (1,)), ((), ())),
    "tn": (((0,), (0,)), ((), ())),
}


def _matmul(a, b, *, mode, out_dtype, tm, tn, tk, name, add=None, dep=None):
    if mode == "nn":
        (m, k), n = a.shape, b.shape[1]
    elif mode == "nt":
        (m, k), n = a.shape, b.shape[0]
    else:
        (k, m), n = a.shape, b.shape[1]
    tm, tn, tk = min(tm, m), min(tn, n), min(tk, k)
    assert m % tm == 0 and n % tn == 0 and k % tk == 0, (name, m, n, k, tm, tn, tk)
    nk = k // tk
    has_add = add is not None
    has_dep = dep is not None

    def body(*refs):
        if has_dep:
            refs = refs[1:]
        if has_add:
            a_ref, b_ref, add_ref, o_ref = refs[:4]
            scratch = refs[4:]
        else:
            a_ref, b_ref, o_ref = refs[:3]
            add_ref = None
            scratch = refs[3:]
        part = lax.dot_general(a_ref[...], b_ref[...], _DOT_DIMS[mode], preferred_element_type=F32)

        def finish(acc):
            if add_ref is not None:
                acc = acc + add_ref[...].astype(F32)
            o_ref[...] = acc.astype(o_ref.dtype)

        if nk == 1:
            finish(part)
        else:
            acc_ref = scratch[0]
            kk = pl.program_id(2)

            @pl.when(kk == 0)
            def _():
                acc_ref[...] = part

            @pl.when(kk > 0)
            def _():
                acc_ref[...] += part

            @pl.when(kk == nk - 1)
            def _():
                finish(acc_ref[...])

    if mode == "nn":
        a_spec = pl.BlockSpec((tm, tk), lambda i, j, kk: (i, kk))
        b_spec = pl.BlockSpec((tk, tn), lambda i, j, kk: (kk, j))
    elif mode == "nt":
        a_spec = pl.BlockSpec((tm, tk), lambda i, j, kk: (i, kk))
        b_spec = pl.BlockSpec((tn, tk), lambda i, j, kk: (j, kk))
    else:
        a_spec = pl.BlockSpec((tk, tm), lambda i, j, kk: (kk, i))
        b_spec = pl.BlockSpec((tk, tn), lambda i, j, kk: (kk, j))
    o_spec = pl.BlockSpec((tm, tn), lambda i, j, kk: (i, j))
    in_specs = [a_spec, b_spec] + ([o_spec] if has_add else [])
    operands = (a, b) + ((add,) if has_add else ())
    if has_dep:
        in_specs = [pl.BlockSpec(dep.shape, lambda i, j, kk: (0, 0))] + in_specs
        operands = (dep,) + operands
    return pl.pallas_call(
        body,
        name=name,
        out_shape=jax.ShapeDtypeStruct((m, n), out_dtype),
        grid=(m // tm, n // tn, nk),
        in_specs=in_specs,
        out_specs=o_spec,
        scratch_shapes=[pltpu.VMEM((tm, tn), F32)] if nk > 1 else [],
        compiler_params=_cparams(("parallel", "parallel", "arbitrary")),
    )(*operands)


def _matmul_norm_bwd(a, b, x, g, dres, *, tm, tk, name, add=None, dep=None):
    (m, k), d = a.shape, b.shape[1]
    tm, tk = min(tm, m), min(tk, k)
    assert m % tm == 0 and k % tk == 0 and x.shape == (m, d), (name, m, k, d, tm, tk)
    nk = k // tk
    has_add = add is not None
    has_dep = dep is not None

    def body(*refs):
        refs = refs[has_dep:]
        a_ref, b_ref = refs[:2]
        add_ref = refs[2] if has_add else None
        x_ref, g_ref, dres_ref, dx_ref, dxb_ref, dg_ref = refs[2 + has_add:8 + has_add]
        scratch = refs[8 + has_add:]
        i = pl.program_id(0)
        part = jnp.dot(a_ref[...], b_ref[...], preferred_element_type=F32)

        def finish(dh):
            if add_ref is not None:
                dh = dh + add_ref[...].astype(F32)
            xv = x_ref[...]
            rs = lax.rsqrt(jnp.mean(xv * xv, axis=-1, keepdims=True) + NORM_EPS)
            nrm = xv * rs
            dn = dh * g_ref[...]
            dx = dres_ref[...] + rs * (dn - nrm * jnp.mean(dn * nrm, axis=-1, keepdims=True))
            dx_ref[...] = dx
            dxb_ref[...] = dx.astype(BF16)
            dg_part = jnp.sum(dh * nrm, axis=0, keepdims=True)

            @pl.when(i == 0)
            def _():
                dg_ref[...] = dg_part

            @pl.when(i > 0)
            def _():
                dg_ref[...] += dg_part

        if nk == 1:
            finish(part)
        else:
            acc_ref = scratch[0]
            kk = pl.program_id(1)

            @pl.when(kk == 0)
            def _():
                acc_ref[...] = part

            @pl.when(kk > 0)
            def _():
                acc_ref[...] += part

            @pl.when(kk == nk - 1)
            def _():
                finish(acc_ref[...])

    row = pl.BlockSpec((tm, d), lambda i, kk: (i, 0))
    vec = pl.BlockSpec((1, d), lambda i, kk: (0, 0))
    in_specs = [pl.BlockSpec((tm, tk), lambda i, kk: (i, kk)), pl.BlockSpec((tk, d), lambda i, kk: (kk, 0))]
    operands = [a, b]
    if has_add:
        in_specs.append(row)
        operands.append(add)
    in_specs += [row, vec, row]
    operands += [x, g.reshape(1, d), dres]
    if has_dep:
        in_specs = [pl.BlockSpec(dep.shape, lambda i, kk: (0, 0))] + in_specs
        operands = [dep] + operands
    return pl.pallas_call(
        body, name=name,
        out_shape=(jax.ShapeDtypeStruct((m, d), F32), jax.ShapeDtypeStruct((m, d), BF16),
                   jax.ShapeDtypeStruct((1, d), F32)),
        grid=(m // tm, nk),
        in_specs=in_specs,
        out_specs=(row, row, vec),
        scratch_shapes=[pltpu.VMEM((tm, d), F32)] if nk > 1 else [],
        compiler_params=_cparams(("arbitrary", "arbitrary")),
    )(*operands)


def _rms_fwd(x, g, *, name):
    s, d = x.shape
    tr = min(512, s)

    def body(x_ref, g_ref, h_ref):
        xv = x_ref[...]
        rs = lax.rsqrt(jnp.mean(xv * xv, axis=-1, keepdims=True) + NORM_EPS)
        h_ref[...] = (xv * rs * g_ref[...]).astype(BF16)

    return pl.pallas_call(
        body, name=name,
        out_shape=jax.ShapeDtypeStruct((s, d), BF16),
        grid=(s // tr,),
        in_specs=[pl.BlockSpec((tr, d), lambda i: (i, 0)), pl.BlockSpec((1, d), lambda i: (0, 0))],
        out_specs=pl.BlockSpec((tr, d), lambda i: (i, 0)),
        compiler_params=_cparams(("parallel",)),
    )(x, g.reshape(1, d))


def _loss_head(x, g, target):
    s, d = x.shape
    tr = min(512, s)

    def body(x_ref, g_ref, t_ref, loss_ref, dx_ref, dxb_ref, dg_ref):
        i = pl.program_id(0)
        xv = x_ref[...]
        rs = lax.rsqrt(jnp.mean(xv * xv, axis=-1, keepdims=True) + NORM_EPS)
        n = xv * rs
        gv = g_ref[...]
        err = n * gv - t_ref[...]
        row_loss = jnp.mean(err * err, axis=-1, keepdims=True)
        loss_part = 0.5 * jnp.sum(row_loss, axis=0, keepdims=True)
        dy = err * (1.0 / d)
        dn = dy * gv
        dx = rs * (dn - n * jnp.mean(dn * n, axis=-1, keepdims=True))
        dx_ref[...] = dx
        dxb_ref[...] = dx.astype(BF16)
        dg_part = jnp.sum(dy * n, axis=0, keepdims=True)

        @pl.when(i == 0)
        def _():
            loss_ref[...] = loss_part
            dg_ref[...] = dg_part

        @pl.when(i > 0)
        def _():
            loss_ref[...] += loss_part
            dg_ref[...] += dg_part

    row = pl.BlockSpec((tr, d), lambda i: (i, 0))
    vec = pl.BlockSpec((1, d), lambda i: (0, 0))
    one = pl.BlockSpec((1, 1), lambda i: (0, 0))
    return pl.pallas_call(
        body, name="loss_head",
        out_shape=(jax.ShapeDtypeStruct((1, 1), F32), jax.ShapeDtypeStruct((s, d), F32),
                   jax.ShapeDtypeStruct((s, d), BF16), jax.ShapeDtypeStruct((1, d), F32)),
        grid=(s // tr,),
        in_specs=[row, vec, row],
        out_specs=(one, row, row, vec),
        compiler_params=_cparams(("arbitrary",)),
    )(x, g.reshape(1, d), target)


def _conv_taps(u_prev, u, w_ref, rows):
    ext = jnp.concatenate([u_prev, u], axis=0)
    u1 = pltpu.roll(ext, 1, 0)[HALO:HALO + rows]
    u2 = pltpu.roll(ext, 2, 0)[HALO:HALO + rows]
    conv = w_ref[0:1, :] * u2 + w_ref[1:2, :] * u1 + w_ref[2:3, :] * u + w_ref[3:4, :]
    return conv, u1, u2


def _conv_fwd(proj, conv_wb, *, s):
    tr = min(512, s)
    c = CONV_CH
    hb = tr // HALO

    def body(gbi_ref, gci_ref, cx_ref, gci_h_ref, cx_h_ref, w_ref, o_ref):
        i = pl.program_id(0)
        u = gci_ref[...].astype(F32) * cx_ref[...].astype(F32)
        u_prev = gci_h_ref[...].astype(F32) * cx_h_ref[...].astype(F32)
        u_prev = jnp.where(i == 0, 0.0, u_prev)
        conv, _, _ = _conv_taps(u_prev, u, w_ref, tr)
        o_ref[...] = (gbi_ref[...].astype(F32) * conv).astype(BF16)

    def seg(col):
        return pl.BlockSpec((tr, c), lambda i: (i, col // c))

    def halo(col):
        return pl.BlockSpec((HALO, c), lambda i: (jnp.maximum(i * hb - 1, 0), col // c))

    return pl.pallas_call(
        body, name="conv_fwd",
        out_shape=jax.ShapeDtypeStruct((s, c), BF16),
        grid=(s // tr,),
        in_specs=[seg(COL_GBI), seg(COL_GCI), seg(COL_CX), halo(COL_GCI), halo(COL_CX),
                  pl.BlockSpec((8, c), lambda i: (0, 0))],
        out_specs=pl.BlockSpec((tr, c), lambda i: (i, 0)),
        compiler_params=_cparams(("parallel",)),
    )(proj, proj, proj, proj, proj, conv_wb)


def _conv_bwd(proj, dcb, conv_wb, dproj, *, s):
    tr = min(512, s)
    c = CONV_CH
    hb = tr // HALO
    nb = s // tr

    def body(gbi_ref, gci_ref, cx_ref, dcb_ref, gci_h_ref, cx_h_ref, gbi_n_ref, dcb_n_ref, w_ref,
             dproj_in_ref, dp_ref, dw_ref):
        i = pl.program_id(0)
        gbi = gbi_ref[...].astype(F32)
        gci = gci_ref[...].astype(F32)
        cx = cx_ref[...].astype(F32)
        dcb_v = dcb_ref[...].astype(F32)
        u = gci * cx
        u_prev = jnp.where(i == 0, 0.0, gci_h_ref[...].astype(F32) * cx_h_ref[...].astype(F32))
        conv, u1, u2 = _conv_taps(u_prev, u, w_ref, tr)
        dconv = dcb_v * gbi
        dconv_next = jnp.where(i == nb - 1, 0.0, dcb_n_ref[...].astype(F32) * gbi_n_ref[...].astype(F32))
        ext = jnp.concatenate([dconv, dconv_next], axis=0)
        n_ext = tr + HALO
        d1 = pltpu.roll(ext, n_ext - 1, 0)[0:tr]
        d2 = pltpu.roll(ext, n_ext - 2, 0)[0:tr]
        du = w_ref[2:3, :] * dconv + w_ref[1:2, :] * d1 + w_ref[0:1, :] * d2
        dp_ref[:, 0:c] = (dcb_v * conv).astype(BF16)
        dp_ref[:, c:2 * c] = (du * cx).astype(BF16)
        dp_ref[:, 2 * c:3 * c] = (du * gci).astype(BF16)
        part = jnp.concatenate([
            jnp.sum(dconv * u2, axis=0, keepdims=True),
            jnp.sum(dconv * u1, axis=0, keepdims=True),
            jnp.sum(dconv * u, axis=0, keepdims=True),
            jnp.sum(dconv, axis=0, keepdims=True),
            jnp.zeros((4, c), F32)], axis=0)

        @pl.when(i == 0)
        def _():
            dw_ref[...] = part

        @pl.when(i > 0)
        def _():
            dw_ref[...] += part

    def seg(col):
        return pl.BlockSpec((tr, c), lambda i: (i, col // c))

    def halo_prev(col):
        return pl.BlockSpec((HALO, c), lambda i: (jnp.maximum(i * hb - 1, 0), col // c))

    def halo_next(col):
        return pl.BlockSpec((HALO, c), lambda i: (jnp.minimum((i + 1) * hb, nb * hb - 1), col // c))

    return pl.pallas_call(
        body, name="conv_bwd",
        out_shape=(jax.ShapeDtypeStruct((s, PROJ_W), BF16), jax.ShapeDtypeStruct((8, c), F32)),
        grid=(nb,),
        in_specs=[seg(COL_GBI), seg(COL_GCI), seg(COL_CX), pl.BlockSpec((tr, c), lambda i: (i, 0)),
                  halo_prev(COL_GCI), halo_prev(COL_CX), halo_next(COL_GBI),
                  pl.BlockSpec((HALO, c), lambda i: (jnp.minimum((i + 1) * hb, nb * hb - 1), 0)),
                  pl.BlockSpec((8, c), lambda i: (0, 0)), ANY],
        out_specs=(pl.BlockSpec((tr, 3 * c), lambda i: (i, COL_GBI // (3 * c))),
                   pl.BlockSpec((8, c), lambda i: (0, 0))),
        input_output_aliases={9: 0},
        compiler_params=_cparams(("arbitrary",)),
    )(proj, proj, proj, dcb, proj, proj, proj, dcb, conv_wb, dproj)


def _token_spec(dep):
    return ([], []) if dep is None else ([dep], [pl.BlockSpec(dep.shape, lambda i: (0, 0))])


def _mixer_out_fwd(oa, cb_in, proj, x0, w_oa, w_ob, w_o, *, s, dep=None):
    tr = min(512, s)
    d = D_MODEL
    n_dep = dep is not None

    def body(*refs):
        oa_ref, cb_ref, ga_ref, gb_ref, x0_ref, woa_ref, wob_ref, wo_ref, ya_ref, yb_ref, mix_ref, x1_ref = (
            refs[n_dep:])
        ya = jnp.dot(oa_ref[...], woa_ref[...], preferred_element_type=F32).astype(BF16)
        yb = jnp.dot(cb_ref[...], wob_ref[...], preferred_element_type=F32).astype(BF16)
        ya_ref[...] = ya
        yb_ref[...] = yb
        sa = _sigmoid(ga_ref[...].astype(F32))
        sb = _sigmoid(gb_ref[...].astype(F32))
        mix = (sa * ya.astype(F32) + sb * yb.astype(F32)).astype(BF16)
        mix_ref[...] = mix
        x1_ref[...] = x0_ref[...] + jnp.dot(mix, wo_ref[...], preferred_element_type=F32)

    row = pl.BlockSpec((tr, d), lambda i: (i, 0))
    full = pl.BlockSpec((d, d), lambda i: (0, 0))
    tok, tok_specs = _token_spec(dep)
    return pl.pallas_call(
        body, name="mixer_out_fwd",
        out_shape=(jax.ShapeDtypeStruct((s, d), BF16), jax.ShapeDtypeStruct((s, d), BF16),
                   jax.ShapeDtypeStruct((s, d), BF16), jax.ShapeDtypeStruct((s, d), F32)),
        grid=(s // tr,),
        in_specs=tok_specs + [row, row, pl.BlockSpec((tr, d), lambda i: (i, COL_GA // d)),
                              pl.BlockSpec((tr, d), lambda i: (i, COL_GB // d)), row, full, full, full],
        out_specs=(row, row, row, row),
        compiler_params=_cparams(("parallel",)),
    )(*tok, oa, cb_in, proj, proj, x0, w_oa, w_ob, w_o)


def _mixer_out_bwd(dx1b, proj, ya, yb, w_oa, w_ob, w_o, *, s, dep=None):
    tr = min(512, s)
    d = D_MODEL
    n_dep = dep is not None

    def body(*refs):
        (dx_ref, ga_ref, gb_ref, ya_ref, yb_ref, woa_ref, wob_ref, wo_ref,
         dya_ref, dyb_ref, dcb_ref, doa_ref, dg_ref) = refs[n_dep:]
        dm = lax.dot_general(dx_ref[...], wo_ref[...], _DOT_DIMS["nt"], preferred_element_type=F32)
        dm = dm.astype(BF16).astype(F32)
        sa = _sigmoid(ga_ref[...].astype(F32))
        sb = _sigmoid(gb_ref[...].astype(F32))
        dya = (dm * sa).astype(BF16)
        dyb = (dm * sb).astype(BF16)
        dya_ref[...] = dya
        dyb_ref[...] = dyb
        dg_ref[:, 0:d] = (dm * ya_ref[...].astype(F32) * sa * (1.0 - sa)).astype(BF16)
        dg_ref[:, d:2 * d] = (dm * yb_ref[...].astype(F32) * sb * (1.0 - sb)).astype(BF16)
        dcb_ref[...] = lax.dot_general(dyb, wob_ref[...], _DOT_DIMS["nt"],
                                       preferred_element_type=F32).astype(BF16)
        doa_ref[...] = lax.dot_general(dya, woa_ref[...], _DOT_DIMS["nt"],
                                       preferred_element_type=F32).astype(BF16)

    row = pl.BlockSpec((tr, d), lambda i: (i, 0))
    full = pl.BlockSpec((d, d), lambda i: (0, 0))
    tok, tok_specs = _token_spec(dep)
    return pl.pallas_call(
        body, name="mixer_out_bwd",
        out_shape=tuple(jax.ShapeDtypeStruct((s, d), BF16) for _ in range(4))
        + (jax.ShapeDtypeStruct((s, PROJ_W), BF16),),
        grid=(s // tr,),
        in_specs=tok_specs + [row, pl.BlockSpec((tr, d), lambda i: (i, COL_GA // d)),
                              pl.BlockSpec((tr, d), lambda i: (i, COL_GB // d)), row, row, full, full, full],
        out_specs=(row, row, row, row, pl.BlockSpec((tr, 2 * d), lambda i: (i, COL_GA // (2 * d)))),
        compiler_params=_cparams(("parallel",)),
    )(*tok, dx1b, proj, proj, ya, yb, w_oa, w_ob, w_o)


def _ffn_out_bwd(dx2b, w_down, gt, up, *, s):
    tr = min(256, s)
    d, f = D_MODEL, FFN_HIDDEN

    def body(dx_ref, wd_ref, gt_ref, up_ref, dgt_ref, dup_ref):
        dh = lax.dot_general(dx_ref[...], wd_ref[...], _DOT_DIMS["nt"], preferred_element_type=F32)
        gv = gt_ref[...].astype(F32)
        uv = up_ref[...].astype(F32)
        sg = _sigmoid(gv)
        dgt_ref[...] = (dh * uv * sg * (1.0 + gv * (1.0 - sg))).astype(BF16)
        dup_ref[...] = (dh * gv * sg).astype(BF16)

    row = pl.BlockSpec((tr, f), lambda i: (i, 0))
    return pl.pallas_call(
        body, name="ffn_out_bwd",
        out_shape=(jax.ShapeDtypeStruct((s, f), BF16), jax.ShapeDtypeStruct((s, f), BF16)),
        grid=(s // tr,),
        in_specs=[pl.BlockSpec((tr, d), lambda i: (i, 0)), pl.BlockSpec((f, d), lambda i: (0, 0)), row, row],
        out_specs=(row, row),
        compiler_params=_cparams(("parallel",)),
    )(dx2b, w_down, gt, up)


def _tri(strict):
    r = lax.broadcasted_iota(jnp.int32, (CHUNK, CHUNK), 0)
    c = lax.broadcasted_iota(jnp.int32, (CHUNK, CHUNK), 1)
    return jnp.where((c < r) if strict else (c <= r), 1.0, 0.0).astype(F32)


def _gla_gate_terms(la_c, tri):
    cum = jnp.dot(tri, la_c, precision=lax.Precision.HIGHEST, preferred_element_type=F32)
    cend = cum[CHUNK - 1:CHUNK, :]
    return jnp.exp(cend - cum), jnp.exp(cend)


def _hk(h):
    return slice(h * HEAD_K, (h + 1) * HEAD_K)


def _hv(h):
    return slice(h * HEAD_V, (h + 1) * HEAD_V)


def _gla_fwd(proj, wfg2p, bfg, gn, *, s):
    tr = min(512, s)
    nb = s // tr
    nc = tr // CHUNK

    def body(q_ref, k_ref, v_ref, r_ref, fz_ref, w_ref, b_ref, gn_ref, oa_ref, st_ref,
             state, la_scr, o_scr):
        i = pl.program_id(0)

        @pl.when(i == 0)
        def _():
            state[...] = jnp.zeros_like(state)

        fg = jnp.dot(fz_ref[...], w_ref[...], preferred_element_type=F32) + b_ref[...]
        la_scr[...] = _log_sigmoid(fg) * (1.0 / GATE_TAU)
        tri = _tri(False)

        def chunk(ci, carry):
            r0 = pl.multiple_of(ci * CHUNK, CHUNK)
            rows = pl.ds(r0, CHUNK)
            e, gam = _gla_gate_terms(la_scr[rows, :], tri)
            kd = (k_ref[rows, :].astype(F32) * e).astype(BF16)
            qs = (q_ref[rows, :].astype(F32) * Q_SCALE).astype(BF16)
            v_c = v_ref[rows, :]
            for h in range(GLA_HEADS):
                upd = lax.dot_general(v_c[:, _hv(h)], kd[:, _hk(h)], _DOT_DIMS["tn"],
                                      preferred_element_type=F32)
                st_h = state[:, _hk(h)] * gam[:, _hk(h)] + upd
                state[:, _hk(h)] = st_h
                o_scr[rows, _hv(h)] = lax.dot_general(qs[:, _hk(h)], st_h.astype(BF16), _DOT_DIMS["nt"],
                                                       preferred_element_type=F32)
            st_ref[ci] = state[...].astype(BF16)
            return carry

        lax.fori_loop(0, nc, chunk, 0, unroll=True)
        for h in range(GLA_HEADS):
            o = o_scr[:, _hv(h)]
            rs = lax.rsqrt(jnp.mean(o * o, axis=-1, keepdims=True) + NORM_EPS)
            rv = r_ref[:, _hv(h)].astype(F32)
            oa_ref[:, _hv(h)] = ((o * rs * gn_ref[...]).astype(F32) * (rv * _sigmoid(rv))).astype(BF16)

    return pl.pallas_call(
        body, name="gla_fwd",
        out_shape=(jax.ShapeDtypeStruct((s, VAL_WIDTH), BF16),
                   jax.ShapeDtypeStruct((s // CHUNK, HEAD_V, KEY_WIDTH), BF16)),
        grid=(nb,),
        in_specs=[pl.BlockSpec((tr, KEY_WIDTH), lambda i: (i, COL_Q // KEY_WIDTH)),
                  pl.BlockSpec((tr, KEY_WIDTH), lambda i: (i, COL_K // KEY_WIDTH)),
                  pl.BlockSpec((tr, VAL_WIDTH), lambda i: (i, COL_V // VAL_WIDTH)),
                  pl.BlockSpec((tr, VAL_WIDTH), lambda i: (i, COL_R // VAL_WIDTH)),
                  pl.BlockSpec((tr, FZ_PAD), lambda i: (i, COL_FZ // FZ_PAD)),
                  pl.BlockSpec((FZ_PAD, KEY_WIDTH), lambda i: (0, 0)),
                  pl.BlockSpec((1, KEY_WIDTH), lambda i: (0, 0)),
                  pl.BlockSpec((1, HEAD_V), lambda i: (0, 0))],
        out_specs=(pl.BlockSpec((tr, VAL_WIDTH), lambda i: (i, 0)),
                   pl.BlockSpec((nc, HEAD_V, KEY_WIDTH), lambda i: (i, 0, 0))),
        scratch_shapes=[pltpu.VMEM((HEAD_V, KEY_WIDTH), F32), pltpu.VMEM((tr, KEY_WIDTH), F32),
                        pltpu.VMEM((tr, VAL_WIDTH), F32)],
        compiler_params=_cparams(("arbitrary",)),
    )(proj, proj, proj, proj, proj, wfg2p, bfg, gn)


def _gla_bwd(proj, doa, states, wfg2p, bfg, gn, dproj, *, s):
    tr = min(512, s)
    nb = s // tr
    nc = tr // CHUNK

    def body(q_ref, k_ref, v_ref, r_ref, fz_ref, doa_ref, st_ref, stp_ref, w_ref, b_ref, gn_ref, dproj_in_ref,
             dp_ref, dfz_ref, dgn_ref, dw_ref, db_ref,
             carry, fg_scr, la_scr, dla_scr, o_scr, do_scr):
        i = pl.program_id(0)

        @pl.when(i == 0)
        def _():
            carry[...] = jnp.zeros_like(carry)
            dgn_ref[...] = jnp.zeros_like(dgn_ref)
            dw_ref[...] = jnp.zeros_like(dw_ref)
            db_ref[...] = jnp.zeros_like(db_ref)

        fz = fz_ref[...]
        fg = jnp.dot(fz, w_ref[...], preferred_element_type=F32) + b_ref[...]
        fg_scr[...] = fg
        la_scr[...] = _log_sigmoid(fg) * (1.0 / GATE_TAU)

        def fwd_chunk(ci, c0):
            r0 = pl.multiple_of(ci * CHUNK, CHUNK)
            rows = pl.ds(r0, CHUNK)
            qs = (q_ref[rows, :].astype(F32) * Q_SCALE).astype(BF16)
            st = st_ref[ci].astype(BF16)
            for h in range(GLA_HEADS):
                o_scr[rows, _hv(h)] = lax.dot_general(qs[:, _hk(h)], st[:, _hk(h)], _DOT_DIMS["nt"],
                                                       preferred_element_type=F32)
            return c0

        lax.fori_loop(0, nc, fwd_chunk, 0, unroll=True)

        gnv = gn_ref[...]
        dgn_part = jnp.zeros((1, HEAD_V), F32)
        for h in range(GLA_HEADS):
            o = o_scr[:, _hv(h)]
            rs = lax.rsqrt(jnp.mean(o * o, axis=-1, keepdims=True) + NORM_EPS)
            nrm = o * rs
            rv = r_ref[:, _hv(h)].astype(F32)
            sg = _sigmoid(rv)
            doa_h = doa_ref[:, _hv(h)].astype(F32)
            don = doa_h * (rv * sg)
            dp_ref[:, COL_R + h * HEAD_V:COL_R + (h + 1) * HEAD_V] = (
                doa_h * (nrm * gnv) * (sg * (1.0 + rv * (1.0 - sg)))).astype(BF16)
            dgn_part = dgn_part + jnp.sum(don * nrm, axis=0, keepdims=True)
            dn = don * gnv
            do_scr[:, _hv(h)] = (rs * (dn - nrm * jnp.mean(dn * nrm, axis=-1, keepdims=True))).astype(BF16)
        dgn_ref[...] += dgn_part

        tri = _tri(False)
        tri_s = _tri(True)
        first_block = i == nb - 1

        def bwd_chunk(cc, c0):
            ci = nc - 1 - cc
            r0 = pl.multiple_of(ci * CHUNK, CHUNK)
            rows = pl.ds(r0, CHUNK)
            e, gam = _gla_gate_terms(la_scr[rows, :], tri)
            k_c = k_ref[rows, :].astype(F32)
            kd = k_c * e
            kd_b = kd.astype(BF16)
            qs = (q_ref[rows, :].astype(F32) * Q_SCALE).astype(BF16)
            v_c = v_ref[rows, :]
            do_c = do_scr[rows, :]
            st_b = st_ref[ci]
            st_prev_in = st_ref[jnp.maximum(ci - 1, 0)].astype(F32)
            st_prev_edge = jnp.where(first_block, 0.0, stp_ref[0].astype(F32))
            st_prev = jnp.where(ci > 0, st_prev_in, st_prev_edge)
            dkd_parts = []
            dgam_parts = []
            for h in range(GLA_HEADS):
                dst = lax.dot_general(do_c[:, _hv(h)], qs[:, _hk(h)], _DOT_DIMS["tn"],
                                      preferred_element_type=F32) + carry[:, _hk(h)]
                dst_b = dst.astype(BF16)
                dqs = jnp.dot(do_c[:, _hv(h)], st_b[:, _hk(h)], preferred_element_type=F32)
                dp_ref[rows, COL_Q + h * HEAD_K:COL_Q + (h + 1) * HEAD_K] = (dqs * Q_SCALE).astype(BF16)
                dkd_parts.append(jnp.dot(v_c[:, _hv(h)], dst_b, preferred_element_type=F32))
                dp_ref[rows, COL_V + h * HEAD_V:COL_V + (h + 1) * HEAD_V] = lax.dot_general(
                    kd_b[:, _hk(h)], dst_b, _DOT_DIMS["nt"], preferred_element_type=F32).astype(BF16)
                dgam_parts.append(jnp.sum(dst * st_prev[:, _hk(h)], axis=0, keepdims=True))
                carry[:, _hk(h)] = dst * gam[:, _hk(h)]
            dkd = jnp.concatenate(dkd_parts, axis=1)
            dgam = jnp.concatenate(dgam_parts, axis=1)
            dp_ref[rows, COL_K:COL_K + KEY_WIDTH] = (dkd * e).astype(BF16)
            dz = dkd * kd
            dla_scr[rows, :] = gam * dgam + jnp.dot(tri_s, dz, precision=lax.Precision.HIGHEST,
                                                    preferred_element_type=F32)
            return c0

        lax.fori_loop(0, nc, bwd_chunk, 0, unroll=True)

        dfg = dla_scr[...] * (1.0 / GATE_TAU) * _sigmoid(-fg_scr[...])
        dfg_b = dfg.astype(BF16)
        dfz_ref[...] = lax.dot_general(dfg_b, w_ref[...], _DOT_DIMS["nt"],
                                       preferred_element_type=F32).astype(BF16)
        dw_ref[...] += lax.dot_general(fz, dfg_b, _DOT_DIMS["tn"], preferred_element_type=F32)
        db_ref[...] += jnp.sum(dfg, axis=0, keepdims=True)

    def rev(i):
        return nb - 1 - i

    return pl.pallas_call(
        body, name="gla_bwd",
        out_shape=(jax.ShapeDtypeStruct((s, PROJ_W), BF16),
                   jax.ShapeDtypeStruct((s, FZ_PAD), BF16),
                   jax.ShapeDtypeStruct((1, HEAD_V), F32),
                   jax.ShapeDtypeStruct((FZ_PAD, KEY_WIDTH), F32),
                   jax.ShapeDtypeStruct((1, KEY_WIDTH), F32)),
        grid=(nb,),
        in_specs=[pl.BlockSpec((tr, KEY_WIDTH), lambda i: (rev(i), COL_Q // KEY_WIDTH)),
                  pl.BlockSpec((tr, KEY_WIDTH), lambda i: (rev(i), COL_K // KEY_WIDTH)),
                  pl.BlockSpec((tr, VAL_WIDTH), lambda i: (rev(i), COL_V // VAL_WIDTH)),
                  pl.BlockSpec((tr, VAL_WIDTH), lambda i: (rev(i), COL_R // VAL_WIDTH)),
                  pl.BlockSpec((tr, FZ_PAD), lambda i: (rev(i), COL_FZ // FZ_PAD)),
                  pl.BlockSpec((tr, VAL_WIDTH), lambda i: (rev(i), 0)),
                  pl.BlockSpec((nc, HEAD_V, KEY_WIDTH), lambda i: (rev(i), 0, 0)),
                  pl.BlockSpec((1, HEAD_V, KEY_WIDTH), lambda i: (jnp.maximum(rev(i) * nc - 1, 0), 0, 0)),
                  pl.BlockSpec((FZ_PAD, KEY_WIDTH), lambda i: (0, 0)),
                  pl.BlockSpec((1, KEY_WIDTH), lambda i: (0, 0)),
                  pl.BlockSpec((1, HEAD_V), lambda i: (0, 0)), ANY],
        out_specs=(pl.BlockSpec((tr, 3 * VAL_WIDTH), lambda i: (rev(i), 0)),
                   pl.BlockSpec((tr, FZ_PAD), lambda i: (rev(i), 0)),
                   pl.BlockSpec((1, HEAD_V), lambda i: (0, 0)),
                   pl.BlockSpec((FZ_PAD, KEY_WIDTH), lambda i: (0, 0)),
                   pl.BlockSpec((1, KEY_WIDTH), lambda i: (0, 0))),
        scratch_shapes=[pltpu.VMEM((HEAD_V, KEY_WIDTH), F32),
                        pltpu.VMEM((tr, KEY_WIDTH), F32),
                        pltpu.VMEM((tr, KEY_WIDTH), F32),
                        pltpu.VMEM((tr, KEY_WIDTH), F32),
                        pltpu.VMEM((tr, VAL_WIDTH), F32),
                        pltpu.VMEM((tr, VAL_WIDTH), BF16)],
        input_output_aliases={11: 0},
        compiler_params=_cparams(("arbitrary",)),
    )(proj, proj, proj, proj, proj, doa, states, states, wfg2p, bfg, gn, dproj)


def _put_fz(dproj, dfz, *, s):
    tr = min(512, s)

    def body(dfz_ref, dproj_in_ref, o_ref):
        o_ref[...] = dfz_ref[...]

    return pl.pallas_call(
        body, name="put_dfz",
        out_shape=jax.ShapeDtypeStruct((s, PROJ_W), BF16),
        grid=(s // tr,),
        in_specs=[pl.BlockSpec((tr, FZ_PAD), lambda i: (i, 0)), ANY],
        out_specs=pl.BlockSpec((tr, FZ_PAD), lambda i: (i, COL_FZ // FZ_PAD)),
        input_output_aliases={1: 0},
        compiler_params=_cparams(("parallel",)),
    )(dfz, dproj)


_ADAM_C1 = 1.0 / (1.0 - ADAM_B1 ** ADAM_STEP)
_ADAM_C2 = 1.0 / (1.0 - ADAM_B2 ** ADAM_STEP)


def _adamw_math(wv, gv, mv, vv):
    nm = ADAM_B1 * mv + (1.0 - ADAM_B1) * gv
    nv = ADAM_B2 * vv + (1.0 - ADAM_B2) * (gv * gv)
    delta = -ADAM_LR * ((nm * _ADAM_C1) / (jnp.sqrt(nv * _ADAM_C2) + ADAM_EPS) + ADAM_WD * wv)
    return delta, nm, nv


def _adamw(w, g, m, v, *, name):
    shape = w.shape
    if w.ndim == 1:
        w, g, m, v = (t.reshape(1, 1, -1) for t in (w, g, m, v))
    elif w.ndim == 2:
        w, g, m, v = (t.reshape((1,) + t.shape) for t in (w, g, m, v))
    l, a, b = w.shape

    def body(w_ref, g_ref, m_ref, v_ref, d_ref, nm_ref, nv_ref):
        d_ref[...], nm_ref[...], nv_ref[...] = _adamw_math(w_ref[...], g_ref[...], m_ref[...], v_ref[...])

    blk = pl.BlockSpec((1, a, b), lambda li: (li, 0, 0))
    outs = pl.pallas_call(
        body, name=name,
        out_shape=tuple(jax.ShapeDtypeStruct((l, a, b), F32) for _ in range(3)),
        grid=(l,),
        in_specs=[blk, blk, blk, blk],
        out_specs=(blk, blk, blk),
        compiler_params=_cparams(("parallel",)),
    )(w, g, m, v)
    return tuple(o.reshape(shape) for o in outs)


def _adamw_layers(w, reduced, received, on_core0, m, v, *, name):
    l, a, b = w.shape
    assert l == DEPTH == 2
    ta = _row_tile(a, F32_SUBLANES, MAX_ADAMW_ROWS)

    def body(flag_ref, w_ref, r0_ref, o0_ref, r1_ref, o1_ref, m_ref, v_ref, g_ref, d_ref, nm_ref, nv_ref):
        core0 = flag_ref[...] > 0.5
        g0 = jnp.where(core0, r0_ref[...], o0_ref[...])
        g1 = jnp.where(core0, o1_ref[...], r1_ref[...])
        gv = jnp.where(pl.program_id(0) == 0, g0, g1)
        g_ref[0] = gv
        d_ref[0], nm_ref[0], nv_ref[0] = _adamw_math(w_ref[0], gv, m_ref[0], v_ref[0])

    blk = pl.BlockSpec((1, ta, b), lambda li, ai: (li, ai, 0))

    def grad_block(layer):
        return pl.BlockSpec((ta, b), lambda li, ai: (jnp.where(li == layer, ai, 0), 0))

    g0blk, g1blk = grad_block(0), grad_block(1)
    return pl.pallas_call(
        body, name=name,
        out_shape=tuple(jax.ShapeDtypeStruct((l, a, b), F32) for _ in range(4)),
        grid=(l, a // ta),
        in_specs=[pl.BlockSpec((1, 1), lambda li, ai: (0, 0)), blk, g0blk, g0blk, g1blk, g1blk, blk, blk],
        out_specs=(blk, blk, blk, blk),
        compiler_params=_cparams(("parallel", "parallel")),
    )(on_core0, w, reduced[0], received[0], reduced[1], received[1], m, v)


MESH_ID = pl.DeviceIdType.MESH
ANY = pl.BlockSpec(memory_space=pl.ANY)


def _position():
    x, y, c = lax.axis_index("x"), lax.axis_index("y"), lax.axis_index("c")
    chips = [(1 - x, y), (x, 1 - y), (1 - x, 1 - y)]
    return x, y, c, chips


def _chip_index(xy):
    return 2 * xy[0] + xy[1]


HBM = pl.BlockSpec(memory_space=pltpu.HBM)
SEM = pl.BlockSpec(memory_space=pltpu.SEMAPHORE)
DATAFLOW_EFFECT = pltpu.SideEffectType.DATAFLOW_SIDE_EFFECTING
TOKEN_SHAPE = (8, LANE)


def _landing(shape, dtype):
    return pltpu.with_memory_space_constraint(lax.empty(shape, dtype), pltpu.HBM)


def _split_start(bufs, sem_shape, issue, *, name, after=None):
    n = len(bufs)
    n_in = n + (after is not None)

    def body(*refs):
        issue(refs[:n], refs[n_in], refs[n_in + 1])
        token = refs[-1]
        token[...] = jnp.zeros_like(token)

    operands = [pltpu.with_memory_space_constraint(t, pltpu.HBM) for t in bufs]
    outs = pl.pallas_call(
        body, name=name,
        out_shape=(pltpu.SemaphoreType.DMA(sem_shape), pltpu.SemaphoreType.DMA(sem_shape),
                   *[pltpu.HBM(t.shape, t.dtype) for t in bufs], jax.ShapeDtypeStruct(TOKEN_SHAPE, F32)),
        in_specs=[HBM] * n + [ANY] * (after is not None),
        out_specs=(SEM, SEM, *[HBM] * n, pl.BlockSpec(memory_space=pltpu.VMEM)),
        input_output_aliases={i: 2 + i for i in range(n)},
        compiler_params=pltpu.CompilerParams(has_side_effects=DATAFLOW_EFFECT),
    )(*operands, *([after] if after is not None else []))
    return outs[0], outs[1], list(outs[2:2 + n]), outs[-1]


def _split_wait(started, after, settle, *, name):
    send_sems, recv_sems, bufs, _ = started
    n = len(bufs)
    afters = tuple(after) if isinstance(after, (tuple, list)) else (after,)

    def body(*refs):
        settle(refs[:n], refs[n], refs[n + 1])

    outs = pl.pallas_call(
        body, name=name,
        out_shape=tuple(pltpu.HBM(t.shape, t.dtype) for t in bufs),
        in_specs=[HBM] * n + [SEM, SEM] + [ANY] * len(afters),
        out_specs=tuple([HBM] * n),
        input_output_aliases={i: i for i in range(n)},
        compiler_params=pltpu.CompilerParams(has_side_effects=DATAFLOW_EFFECT),
    )(*bufs, send_sems, recv_sems, *afters)
    return list(outs)


def _to_sibling(bufs, n, send_sems, recv_sems):
    x, y, c, _ = _position()
    return [pltpu.make_async_remote_copy(
        src_ref=bufs[a], dst_ref=bufs[n + a], send_sem=send_sems.at[a], recv_sem=recv_sems.at[a],
        device_id=(x, y, 1 - c), device_id_type=MESH_ID) for a in range(n)]


def _sibling_push_start(layer, sender_is_reducer, arrays, *, name):
    n = len(arrays)
    sender = layer if sender_is_reducer else 1 - layer

    def issue(bufs, send_sems, recv_sems):
        @pl.when(lax.axis_index("c") == sender)
        def _():
            for cp in _to_sibling(bufs, n, send_sems, recv_sems):
                cp.start()

    lands = [_landing(t.shape, t.dtype) for t in arrays]
    return _split_start(list(arrays) + lands, (n,), issue, name=name)


def _sibling_push_wait(layer, sender_is_reducer, started, after, *, name):
    n = len(started[2]) // 2
    sender = layer if sender_is_reducer else 1 - layer

    def settle(bufs, send_sems, recv_sems):
        c = lax.axis_index("c")

        @pl.when(c == sender)
        def _():
            for cp in _to_sibling(bufs, n, send_sems, recv_sems):
                cp.wait_send()

        @pl.when(c != sender)
        def _():
            for cp in _to_sibling(bufs, n, send_sems, recv_sems):
                cp.wait_recv()

    outs = _split_wait(started, after, settle, name=name)
    return outs[:n], outs[n:]


def _chip_copies(bufs, n, send_sems, recv_sems):
    x, y, c, chips = _position()
    return [pltpu.make_async_remote_copy(
        src_ref=bufs[a].at[_chip_index(chips[k])], dst_ref=bufs[n + a].at[k],
        send_sem=send_sems.at[3 * a + k], recv_sem=recv_sems.at[3 * a + k],
        device_id=(chips[k][0], chips[k][1], c), device_id_type=MESH_ID)
        for a in range(n) for k in range(3)]


def _chip_exchange_start(layer, parts, *, name):
    n = len(parts)

    def issue(bufs, send_sems, recv_sems):
        @pl.when(lax.axis_index("c") == layer)
        def _():
            for cp in _chip_copies(bufs, n, send_sems, recv_sems):
                cp.start()

    lands = [_landing((3,) + t.shape[1:], t.dtype) for t in parts]
    return _split_start(list(parts) + lands, (3 * n,), issue, name=name)


def _chip_exchange_wait(layer, started, after, *, name):
    n = len(started[2]) // 2

    def settle(bufs, send_sems, recv_sems):
        @pl.when(lax.axis_index("c") == layer)
        def _():
            for cp in _chip_copies(bufs, n, send_sems, recv_sems):
                cp.wait()

    outs = _split_wait(started, after, settle, name=name)
    return outs[:n], outs[n:]


def _gather_copies(layer, bufs, n, send_sems, recv_sems, arriving):
    x, y, c, chips = _position()
    me = _chip_index((x, y))
    return [pltpu.make_async_remote_copy(
        src_ref=bufs[a], dst_ref=bufs[n + a].at[_chip_index(chips[k]) if arriving else me],
        send_sem=send_sems.at[3 * a + k], recv_sem=recv_sems.at[3 * a + k],
        device_id=(chips[k][0], chips[k][1], c), device_id_type=MESH_ID)
        for a in range(n) for k in range(3)]


def _gather_start(layer, shards, *, name, after=None):
    n = len(shards)

    def issue(bufs, send_sems, recv_sems):
        @pl.when(lax.axis_index("c") == layer)
        def _():
            for cp in _gather_copies(layer, bufs, n, send_sems, recv_sems, False):
                cp.start()

    lands = [_landing((N_CHIPS,) + t.shape, t.dtype) for t in shards]
    return _split_start(list(shards) + lands, (3 * n,), issue, name=name, after=after)


def _gather_wait(layer, started, after, *, name):
    n = len(started[2]) // 2

    def settle(bufs, send_sems, recv_sems):
        @pl.when(lax.axis_index("c") == layer)
        def _():
            for cp in _gather_copies(layer, bufs, n, send_sems, recv_sems, False):
                cp.wait_send()
            for cp in _gather_copies(layer, bufs, n, send_sems, recv_sems, True):
                cp.wait_recv()

    return _split_wait(started, after, settle, name=name)[n:]


def _handoff_copies(bufs, n, send_sems, recv_sems):
    x, y, c, chips = _position()
    out = []
    for a in range(n):
        for k in range(3):
            slot = bufs[a].at[_chip_index(chips[k])]
            out.append(pltpu.make_async_remote_copy(
                src_ref=slot, dst_ref=slot, send_sem=send_sems.at[3 * a + k], recv_sem=recv_sems.at[3 * a + k],
                device_id=(x, y, 1 - c), device_id_type=MESH_ID))
    return out


def _handoff_start(layer, gathered, *, name):
    n = len(gathered)

    def issue(bufs, send_sems, recv_sems):
        @pl.when(lax.axis_index("c") == layer)
        def _():
            for cp in _handoff_copies(bufs, n, send_sems, recv_sems):
                cp.start()

    return _split_start(list(gathered), (3 * n,), issue, name=name)


def _handoff_wait(layer, started, after, *, name):
    n = len(started[2])

    def settle(bufs, send_sems, recv_sems):
        c = lax.axis_index("c")

        @pl.when(c == layer)
        def _():
            for cp in _handoff_copies(bufs, n, send_sems, recv_sems):
                cp.wait_send()

        @pl.when(c != layer)
        def _():
            for cp in _handoff_copies(bufs, n, send_sems, recv_sems):
                cp.wait_recv()

    return _split_wait(started, after, settle, name=name)


F32_SUBLANES = 8
BF16_SUBLANES = 16
MAX_STREAM_ROWS = 704
MAX_ADAMW_ROWS = 384


def _row_tile(a, sublanes=BF16_SUBLANES, max_rows=MAX_STREAM_ROWS):
    best = None
    for cand in range(sublanes, min(a, max_rows) + 1, sublanes):
        if a % cand == 0:
            best = cand
    assert best is not None, a
    return best


def _pair_sum(mine, sib, active, *, name):
    nchip, a, b = sib.shape
    ta = _row_tile(a)

    def body(act_ref, m_ref, s_ref, o_ref):
        @pl.when(act_ref[0] == 1)
        def _():
            o_ref[...] = (m_ref[...].astype(F32) + s_ref[...].astype(F32)).astype(BF16)

    blk = pl.BlockSpec((1, ta, b), lambda j, r, act: (j * act[0], r * act[0], 0))
    return pl.pallas_call(
        body, name=name,
        out_shape=jax.ShapeDtypeStruct((nchip, a, b), BF16),
        grid_spec=pltpu.PrefetchScalarGridSpec(
            num_scalar_prefetch=1, grid=(nchip, a // ta), in_specs=[blk, blk], out_specs=blk),
        compiler_params=_cparams(("arbitrary", "arbitrary")),
    )(active, mine, sib)


def _total_sum(own, recv, active, *, name):
    a, b = own.shape
    ta = _row_tile(a)

    def body(act_ref, o_ref, r_ref, t_ref):
        @pl.when(act_ref[0] == 1)
        def _():
            acc = o_ref[...].astype(F32)
            for k in range(3):
                acc = acc + r_ref[k].astype(F32)
            t_ref[...] = acc

    blk = pl.BlockSpec((ta, b), lambda r, act: (r * act[0], 0))
    return pl.pallas_call(
        body, name=name,
        out_shape=jax.ShapeDtypeStruct((a, b), F32),
        grid_spec=pltpu.PrefetchScalarGridSpec(
            num_scalar_prefetch=1, grid=(a // ta,),
            in_specs=[blk, pl.BlockSpec((3, ta, b), lambda r, act: (0, r * act[0], 0))], out_specs=blk),
        compiler_params=_cparams(("arbitrary",)),
    )(active, own, recv)


def _all_reduce_small(packed):
    rows, width = packed.shape

    def body(x_ref, out_ref, gath, send_sems, recv_sems, local_sem):
        x, y, c, chips = _position()
        me, sibling = (x, y, c), (x, y, 1 - c)

        def slot(px, py, pc):
            return gath.at[4 * px + 2 * py + pc]

        def copy(k, block, to, src=None):
            return pltpu.make_async_remote_copy(
                src_ref=slot(*block) if src is None else src, dst_ref=slot(*block),
                send_sem=send_sems.at[k], recv_sem=recv_sems.at[k], device_id=to, device_id_type=MESH_ID)

        mine = pltpu.make_async_copy(x_ref, slot(*me), local_sem)
        mine.start()
        first = [copy(0, me, sibling, src=x_ref)]
        first += [copy(1 + j, me, (*chip, c), src=x_ref) for j, chip in enumerate(chips)]
        for cp in first:
            cp.start()
        passed = [copy(4 + j, (*chip, c), sibling) for j, chip in enumerate(chips)]
        for j, chip in enumerate(chips):
            copy(1 + j, (*chip, c), me).wait_recv()
            passed[j].start()
        copy(0, sibling, me).wait_recv()
        for j, chip in enumerate(chips):
            copy(4 + j, (*chip, 1 - c), me).wait_recv()
        for cp in first + passed:
            cp.wait_send()
        mine.wait()
        acc = gath[0]
        for d in range(1, N_DEV):
            acc = acc + gath[d]
        out_ref[...] = acc

    return pl.pallas_call(
        body, name="all_reduce_small",
        out_shape=jax.ShapeDtypeStruct((rows, width), F32),
        in_specs=[pl.BlockSpec(memory_space=pltpu.VMEM)],
        out_specs=pl.BlockSpec(memory_space=pltpu.VMEM),
        scratch_shapes=[pltpu.VMEM((N_DEV, rows, width), F32), pltpu.SemaphoreType.DMA((7,)),
                        pltpu.SemaphoreType.DMA((7,)), pltpu.SemaphoreType.DMA],
    )(packed)


def _mixer_forward_branches(x0, h, w, s, dep=None, before_conv=None):
    proj = _matmul(h, w["w_in_t"], mode="nt", out_dtype=BF16, tm=1024, tn=1664, tk=1024, name="proj_fwd",
                   dep=dep)
    oa, states = _gla_fwd(proj, w["w_fg2"], w["b_fg"], w["gla_norm_g"], s=s)
    conv_wb = w["conv_wb"]
    if before_conv is not None:
        conv_wb = conv_wb + before_conv(oa)[0, 0]
    cb_in = _conv_fwd(proj, conv_wb, s=s)
    return dict(x0=x0, h=h, proj=proj, oa=oa, states=states, cb_in=cb_in)


def _mixer_forward_out(sv, w, s, dep=None):
    ya, yb, mix, x1 = _mixer_out_fwd(sv["oa"], sv["cb_in"], sv["proj"], sv["x0"], w["w_oa"], w["w_ob"], w["w_o"],
                                     s=s, dep=dep)
    return x1, dict(sv, ya=ya, yb=yb, mix=mix)


def _ffn_forward_hidden(x1, w, s, dep=None):
    tr = min(256, s)
    d, f = D_MODEL, FFN_HIDDEN
    n_dep = dep is not None

    def body(*refs):
        x_ref, g_ref, wg_ref, wu_ref, h2_ref, gt_ref, up_ref, hid_ref = refs[n_dep:]
        xv = x_ref[...]
        rs = lax.rsqrt(jnp.mean(xv * xv, axis=-1, keepdims=True) + NORM_EPS)
        h2 = (xv * rs * g_ref[...]).astype(BF16)
        h2_ref[...] = h2
        gt = lax.dot_general(h2, wg_ref[...], _DOT_DIMS["nt"], preferred_element_type=F32).astype(BF16)
        up = lax.dot_general(h2, wu_ref[...], _DOT_DIMS["nt"], preferred_element_type=F32).astype(BF16)
        gt_ref[...] = gt
        up_ref[...] = up
        gv = gt.astype(F32)
        hid_ref[...] = (gv * _sigmoid(gv) * up.astype(F32)).astype(BF16)

    row_d = pl.BlockSpec((tr, d), lambda i: (i, 0))
    row_f = pl.BlockSpec((tr, f), lambda i: (i, 0))
    weight = pl.BlockSpec((f, d), lambda i: (0, 0))
    tok, tok_specs = _token_spec(dep)
    h2, gt, up, hid = pl.pallas_call(
        body, name="ffn_hidden_fwd",
        out_shape=(jax.ShapeDtypeStruct((s, d), BF16),) + tuple(jax.ShapeDtypeStruct((s, f), BF16) for _ in range(3)),
        grid=(s // tr,),
        in_specs=tok_specs + [row_d, pl.BlockSpec((1, d), lambda i: (0, 0)), weight, weight],
        out_specs=(row_d, row_f, row_f, row_f),
        compiler_params=_cparams(("parallel",)),
    )(*tok, x1, w["norm2_g"].reshape(1, d), w["w_gate_t"], w["w_up_t"])
    return dict(x1=x1, h2=h2, gt=gt, up=up, hid=hid)


def _ffn_forward_out(sv, w, dep=None):
    return _matmul(sv["hid"], w["w_ffn_down"], mode="nn", out_dtype=F32, tm=1024, tn=1024, tk=FFN_HIDDEN,
                   name="ffn_out_fwd", add=sv["x1"], dep=dep)


def _ffn_backward_grads(dx2b, w, sv, s):
    g = {}
    dgt, dup = _ffn_out_bwd(dx2b, w["w_ffn_down"], sv["gt"], sv["up"], s=s)
    g["w_ffn_down"] = _matmul(sv["hid"], dx2b, mode="tn", out_dtype=BF16, tm=1408, tn=1024, tk=1024,
                              name="ffn_out_wgrad")
    g["w_gate_t"] = _matmul(dgt, sv["h2"], mode="tn", out_dtype=BF16, tm=1408, tn=1024, tk=1024,
                            name="ffn_gate_wgrad")
    g["w_up_t"] = _matmul(dup, sv["h2"], mode="tn", out_dtype=BF16, tm=1408, tn=1024, tk=1024,
                          name="ffn_up_wgrad")
    return g, dgt, dup


def _ffn_backward_input(dgt, dup, dx2, w, sv, dep=None):
    dh2 = _matmul(dgt, w["w_gate_t"], mode="nn", out_dtype=BF16, tm=1024, tn=1024, tk=FFN_HIDDEN,
                  name="ffn_gate_bwd", dep=dep)
    return _matmul_norm_bwd(dup, w["w_up_t"], sv["x1"], w["norm2_g"], dx2, tm=512, tk=FFN_HIDDEN,
                            name="ffn_up_bwd_norm2_bwd", add=dh2)


def _mixer_backward_branches(dx1b, w, sv, s, dep=None):
    g = {}
    dya, dyb, dcb, doa, dproj = _mixer_out_bwd(dx1b, sv["proj"], sv["ya"], sv["yb"], w["w_oa"], w["w_ob"],
                                               w["w_o"], s=s, dep=dep)
    g["w_o"] = _matmul(sv["mix"], dx1b, mode="tn", out_dtype=BF16, tm=1024, tn=1024, tk=2048, name="wo_wgrad")
    g["w_ob"] = _matmul(sv["cb_in"], dyb, mode="tn", out_dtype=BF16, tm=1024, tn=1024, tk=2048, name="yb_wgrad")
    g["w_oa"] = _matmul(sv["oa"], dya, mode="tn", out_dtype=BF16, tm=1024, tn=1024, tk=2048, name="ya_wgrad")
    dproj, g["conv_wb"] = _conv_bwd(sv["proj"], dcb, w["conv_wb"], dproj, s=s)
    dproj, dfz, g["gla_norm_g"], g["w_fg2"], g["b_fg"] = _gla_bwd(
        sv["proj"], doa, sv["states"], w["w_fg2"], w["b_fg"], w["gla_norm_g"], dproj, s=s)
    return g, _put_fz(dproj, dfz, s=s)


def _proj_wgrad(dproj, sv, dep=None):
    return _matmul(dproj, sv["h"], mode="tn", out_dtype=BF16, tm=1664, tn=1024, tk=1024, name="proj_wgrad",
                   dep=dep)


def _proj_bwd(dproj, w, x0, dres, dep=None):
    return _matmul_norm_bwd(dproj, w["w_in_t"], x0, w["norm1_g"], dres, tm=512, tk=1664,
                            name="proj_bwd_norm1_bwd", dep=dep)


def _cols_from_chips(t):
    return jnp.transpose(t, (1, 0, 2)).reshape(t.shape[1], -1)


W_IN_ROWS = IN_WIDTH // N_CHIPS
W_IN_ROWS_PAD = -(-W_IN_ROWS // BF16_SUBLANES) * BF16_SUBLANES


def _w_in_t_shard(t):
    return jnp.pad(jnp.transpose(t, (0, 2, 1)), ((0, 0), (0, W_IN_ROWS_PAD - W_IN_ROWS), (0, 0)))


def _w_in_t_unshard(t):
    return jnp.transpose(t[:, :W_IN_ROWS], (0, 2, 1))


def _w_in_row_segments():
    runs = [(0, 0, FZ_ORIG), (FZ_ORIG, FZ_ORIG + GATE_RANK, COL_FZ - FZ_ORIG), (COL_FZ, FZ_ORIG, GATE_RANK)]
    out = []
    for kernel_row, ref_row, length in runs:
        while length:
            chip, local = divmod(ref_row, W_IN_ROWS)
            n = min(length, W_IN_ROWS - local)
            out.append((kernel_row, chip * W_IN_ROWS_PAD + local, n))
            kernel_row, ref_row, length = kernel_row + n, ref_row + n, length - n
    return out


def _permute_rows(src, n_out, segments, *, name):
    n_src, width = src.shape
    block = _row_tile(n_out)
    window = block + BF16_SUBLANES
    assert n_src >= window and src.dtype == BF16

    def body(src_ref, out_ref):
        rows = lax.broadcasted_iota(jnp.int32, (block, window), 0)
        cols = lax.broadcasted_iota(jnp.int32, (block, window), 1)
        for b0 in range(0, n_out, block):
            acc = None
            for o, s, n in segments:
                lo, hi = max(o, b0), min(o + n, b0 + block)
                if lo >= hi:
                    continue
                first = s + (lo - o)
                base = max(0, min(first // BF16_SUBLANES * BF16_SUBLANES, n_src - window))
                shift = (first - base) - (lo - b0)
                pick = (cols == rows + shift) & (rows >= lo - b0) & (rows < hi - b0)
                part = jnp.dot(jnp.where(pick, 1.0, 0.0).astype(BF16), src_ref[base:base + window, :],
                               preferred_element_type=F32)
                acc = part if acc is None else acc + part
            out_ref[b0:b0 + block, :] = (jnp.zeros((block, width), F32) if acc is None else acc).astype(BF16)

    return pl.pallas_call(
        body, name=name,
        out_shape=jax.ShapeDtypeStruct((n_out, width), BF16),
        in_specs=[pl.BlockSpec(memory_space=pltpu.VMEM)],
        out_specs=pl.BlockSpec(memory_space=pltpu.VMEM),
        compiler_params=_cparams(),
    )(src)


def _w_in_t_to_kernel(t):
    return _permute_rows(t.reshape(N_CHIPS * W_IN_ROWS_PAD, D_MODEL), PROJ_W, _w_in_row_segments(),
                         name="w_in_to_kernel_rows")


def _w_in_t_from_kernel(g):
    segments = [(slab, kernel_row, n) for kernel_row, slab, n in _w_in_row_segments()]
    out = _permute_rows(g, N_CHIPS * W_IN_ROWS_PAD, segments, name="w_in_to_chip_rows")
    return out.reshape(N_CHIPS, W_IN_ROWS_PAD, D_MODEL)


def kernel(x, norm1_g, w_in, w_fg2, b_fg, gla_norm_g, w_oa, conv_w, conv_b, w_ob, w_o, norm2_g, w_ffn_gate, w_ffn_up, w_ffn_down, final_g, loss_target, m_norm1_g, m_w_in, m_w_fg2, m_b_fg, m_gla_norm_g, m_w_oa, m_conv_w, m_conv_b, m_w_ob, m_w_o, m_norm2_g, m_w_ffn_gate, m_w_ffn_up, m_w_ffn_down, m_final_g, v_norm1_g, v_w_in, v_w_fg2, v_b_fg, v_gla_norm_g, v_w_oa, v_conv_w, v_conv_b, v_w_ob, v_w_o, v_norm2_g, v_w_ffn_gate, v_w_ffn_up, v_w_ffn_down, v_final_g):
    cx_ = lax.axis_index("x")
    cy_ = lax.axis_index("y")
    cc_ = lax.axis_index("c")
    me = 2 * cx_ + cy_
    on_core0 = jnp.where(cc_ == 0, 1.0, 0.0).astype(F32).reshape(1, 1)

    def swap(t):
        return jnp.swapaxes(t, 1, 2)

    big_names = ["w_in", "w_oa", "w_ob", "w_o", "w_ffn_gate", "w_ffn_up", "w_ffn_down"]
    views = dict(
        w_in=tuple(_w_in_t_shard(t) for t in (w_in, m_w_in, v_w_in)),
        w_oa=(w_oa, m_w_oa, v_w_oa), w_ob=(w_ob, m_w_ob, v_w_ob), w_o=(w_o, m_w_o, v_w_o),
        w_ffn_gate=tuple(swap(t) for t in (w_ffn_gate, m_w_ffn_gate, v_w_ffn_gate)),
        w_ffn_up=tuple(swap(t) for t in (w_ffn_up, m_w_ffn_up, v_w_ffn_up)),
        w_ffn_down=(w_ffn_down, m_w_ffn_down, v_w_ffn_down))
    from_view = dict(w_in=_w_in_t_unshard, w_ffn_gate=swap, w_ffn_up=swap)

    s = x.shape[1]

    proj_names = ["w_in"]
    first_names = ["w_in", "w_fg2", "conv_w"]
    rest_names = ["w_oa", "w_ob", "w_o", "w_ffn_gate", "w_ffn_up", "w_ffn_down"]
    mixer_names = ["w_oa", "w_ob", "w_o"]
    ffn_names = ["w_ffn_gate", "w_ffn_up", "w_ffn_down"]
    weight_key = dict(w_in="w_in_t", w_ffn_gate="w_gate_t", w_ffn_up="w_up_t")
    full_shape = dict(w_oa=(VAL_WIDTH, D_MODEL), w_ob=(CONV_CH, D_MODEL), w_o=(D_MODEL, D_MODEL),
                      w_ffn_gate=(FFN_HIDDEN, D_MODEL), w_ffn_up=(FFN_HIDDEN, D_MODEL),
                      w_ffn_down=(FFN_HIDDEN, D_MODEL))

    conv_w_p = jnp.pad(conv_w, ((0, 0), (0, 8 - conv_w.shape[1]), (0, 0)))
    small_shards = dict(w_fg2=w_fg2, conv_w=conv_w_p)

    def shards_of(l, names):
        return [small_shards[n][l] if n in small_shards else views[n][0][l].astype(BF16) for n in names]

    def small_weights(l):
        return dict(norm1_g=norm1_g[l], norm2_g=norm2_g[l], b_fg=b_fg[l].reshape(1, KEY_WIDTH),
                    gla_norm_g=gla_norm_g[l].reshape(1, HEAD_V))

    def full_weights(l, names, gathered, shards):
        out = {}
        for n, g, t in zip(names, gathered, shards):
            g = lax.dynamic_update_index_in_dim(g, t[None], me, axis=0)
            if n == "w_in":
                out["w_in_t"] = _w_in_t_to_kernel(g)
            elif n == "w_fg2":
                out[n] = jnp.pad(_cols_from_chips(g), ((0, FZ_PAD - GATE_RANK), (0, 0))).astype(BF16)
            elif n == "conv_w":
                out["conv_wb"] = (jnp.pad(_cols_from_chips(g)[:3], ((0, 5), (0, 0)))
                                  + jnp.pad(conv_b[l].reshape(1, CONV_CH), ((3, 4), (0, 0))))
            else:
                out[weight_key.get(n, n)] = g.reshape(full_shape[n])
        return out

    def per_chip(g, names):
        out = []
        for n in names:
            t = g[weight_key.get(n, n)]
            out.append(_w_in_t_from_kernel(t) if n == "w_in"
                       else t.reshape(N_CHIPS, t.shape[0] // N_CHIPS, t.shape[1]))
        return out

    def gather(l, names, tag, after=None):
        shards = shards_of(l, names)
        return shards, _gather_start(l, shards, name=f"gather_{tag}_start", after=after)

    def gathered_to_sibling(l, started, after, tag):
        return _handoff_start(l, _gather_wait(l, started, after, name=f"gather_{tag}_wait"),
                              name=f"handoff_{tag}_start")

    def feed(l, g, names, tag):
        return _sibling_push_start(l, False, per_chip(g, names), name=f"feed_{tag}_start")

    def reduces(l):
        return jnp.where(cc_ == l, 1, 0).astype(jnp.int32).reshape(1)

    def pair_and_exchange(l, fed, after, names, tag):
        mine, sib = _sibling_push_wait(l, False, fed, after, name=f"feed_{tag}_wait")
        parts = [_pair_sum(a, b, reduces(l), name=f"pair_sum_{tag}_{n}") for n, a, b in zip(names, mine, sib)]
        return _chip_exchange_start(l, parts, name=f"exchange_{tag}_start")

    def total_and_share(l, swapped, after, names, tag):
        parts, recvs = _chip_exchange_wait(l, swapped, after, name=f"exchange_{tag}_wait")
        owns = [lax.dynamic_index_in_dim(p, me, 0, keepdims=False) for p in parts]
        totals = [_total_sum(o, rc, reduces(l), name=f"total_{tag}_{n}") for n, o, rc in zip(names, owns, recvs)]
        return _sibling_push_start(l, True, totals, name=f"share_{tag}_start")

    def shared(l, sharing, after, names, tag):
        totals, others = _sibling_push_wait(l, True, sharing, after, name=f"share_{tag}_wait")
        return {n: (t, o) for n, t, o in zip(names, totals, others)}

    def branches_under_gather(l, xin, h, w, tag, shards_r, started_r):
        box = {}

        def before_conv(oa):
            box["handoff"] = gathered_to_sibling(l, started_r, oa, tag)
            return box["handoff"][3]

        sv = _mixer_forward_branches(xin, h, w, s, dep=started_r[3], before_conv=before_conv)
        rest = _handoff_wait(l, box["handoff"], sv["cb_in"], name=f"handoff_{tag}_wait")
        w.update(full_weights(l, rest_names, rest, shards_r))
        return sv

    shards_p, started = gather(0, first_names, "p0")
    h0 = _rms_fwd(x[0], norm1_g[0], name="rms1_fwd")
    later_shards = shards_of(0, rest_names) + shards_of(1, first_names) + shards_of(1, rest_names)
    started = gathered_to_sibling(0, started, (h0, views["w_in"][1], *later_shards), "p0")
    w0 = small_weights(0)
    w0.update(full_weights(0, first_names, _handoff_wait(0, started, views["w_in"][2], name="handoff_p0_wait"),
                           shards_p))
    sv0 = branches_under_gather(0, x[0], h0, w0, "r0", *gather(0, rest_names, "r0", after=w0["w_in_t"]))

    shards_p1, started = gather(1, first_names, "p1", after=w0["w_o"])
    x1, sv0m = _mixer_forward_out(sv0, w0, s, dep=started[3])
    sv0f = _ffn_forward_hidden(x1, w0, s)
    handoff_1 = gathered_to_sibling(1, started, sv0f["hid"], "p1")
    rest_1 = gather(1, rest_names, "r1", after=handoff_1[2][0])
    x2 = _ffn_forward_out(sv0f, w0, dep=handoff_1[3] + rest_1[1][3])
    w1 = small_weights(1)
    w1.update(full_weights(1, first_names, _handoff_wait(1, handoff_1, x2, name="handoff_p1_wait"), shards_p1))

    h1 = _rms_fwd(x2, norm1_g[1], name="rms1_fwd")
    x3, sv1m = _mixer_forward_out(branches_under_gather(1, x2, h1, w1, "r1", *rest_1), w1, s)
    sv1f = _ffn_forward_hidden(x3, w1, s)
    x4 = _ffn_forward_out(sv1f, w1)
    loss_local, dx, dxb, dgf = _loss_head(x4, final_g, loss_target[0])
    loss = lax.psum(loss_local[0, 0], ("x", "y", "c"))

    g1, dgt, dup = _ffn_backward_grads(dxb, w1, sv1f, s)
    dx_mid, dxb_mid, g1["norm2_g"] = _ffn_backward_input(dgt, dup, dx, w1, sv1f)
    gm, dproj = _mixer_backward_branches(dxb_mid, w1, sv1m, s)
    g1.update(gm)
    g1["w_in_t"] = _proj_wgrad(dproj, sv1m)
    fed_1 = feed(1, g1, big_names, "l1")
    dx, dxb, g1["norm1_g"] = _proj_bwd(dproj, w1, sv1m["x0"], dx_mid, dep=fed_1[3])

    g0, dgt, dup = _ffn_backward_grads(dxb, w0, sv0f, s)
    swap_1 = pair_and_exchange(1, fed_1, dgt, big_names, "l1")
    fed_f = feed(0, g0, ffn_names, "f0")
    dx_mid, dxb_mid, g0["norm2_g"] = _ffn_backward_input(dgt, dup, dx, w0, sv0f, dep=swap_1[3] + fed_f[3])
    swap_f = pair_and_exchange(0, fed_f, dx_mid, ffn_names, "f0")
    gm, dproj = _mixer_backward_branches(dxb_mid, w0, sv0m, s, dep=swap_f[3])
    g0.update(gm)
    share_1 = total_and_share(1, swap_1, dproj, big_names, "l1")
    share_f = total_and_share(0, swap_f, dproj, ffn_names, "f0")
    fed_m = feed(0, g0, mixer_names, "m0")
    g0["w_in_t"] = _proj_wgrad(dproj, sv0m, dep=share_1[3] + share_f[3] + fed_m[3])
    fed_p = feed(0, g0, proj_names, "p0")
    swap_m = pair_and_exchange(0, fed_m, g0["w_in_t"], mixer_names, "m0")
    grad_x, _, g0["norm1_g"] = _proj_bwd(dproj, w0, sv0m["x0"], dx_mid, dep=fed_p[3] + swap_m[3])
    reduced1 = shared(1, share_1, grad_x, big_names, "l1")
    reduced0 = shared(0, share_f, grad_x, ffn_names, "f0")

    swap_p = pair_and_exchange(0, fed_p, grad_x, proj_names, "p0")
    share_m = total_and_share(0, swap_m, swap_p[3], mixer_names, "m0")
    reduced0.update(shared(0, share_m, share_m[3], mixer_names, "m0"))
    grads = [g0, g1]

    def small_rows(t):
        return t.reshape(-1, D_MODEL)

    def tile_rows(t):
        return jnp.pad(t, ((0, -t.shape[0] % F32_SUBLANES), (0, 0)))

    g0, g1 = grads
    pieces = [
        jnp.concatenate([g0["norm1_g"], g1["norm1_g"]], axis=0),
        jnp.concatenate([g0["norm2_g"], g1["norm2_g"]], axis=0),
        dgf,
        small_rows(jnp.concatenate([g0["b_fg"], g1["b_fg"]], axis=1)),
        small_rows(jnp.concatenate([g0["gla_norm_g"], g1["gla_norm_g"],
                                    jnp.zeros((1, D_MODEL - 2 * HEAD_V), F32)], axis=1)),
        jnp.concatenate([g0["conv_wb"][3:4], g1["conv_wb"][3:4]], axis=0),
        jnp.concatenate([g0["conv_wb"][:3], g1["conv_wb"][:3]], axis=0),
        small_rows(jnp.stack([g0["w_fg2"][:GATE_RANK], g1["w_fg2"][:GATE_RANK]])),
    ]
    small = _all_reduce_small(jnp.concatenate([tile_rows(p) for p in pieces], axis=0))
    sg = dict(
        norm1_g=small[0:2], norm2_g=small[8:10], final_g=small[16],
        b_fg=small[24].reshape(DEPTH, KEY_WIDTH), gla_norm_g=small[32, :DEPTH * HEAD_V].reshape(DEPTH, HEAD_V),
        conv_b=small[40:42],
        conv_w=lax.dynamic_slice_in_dim(small[48:54].reshape(DEPTH, 3, CONV_CH), me * (CONV_CH // N_CHIPS),
                                        CONV_CH // N_CHIPS, axis=2),
        w_fg2=lax.dynamic_slice_in_dim(small[56:72].reshape(DEPTH, GATE_RANK, KEY_WIDTH),
                                       me * (KEY_WIDTH // N_CHIPS), KEY_WIDTH // N_CHIPS, axis=2),
    )

    small_params = dict(norm1_g=(norm1_g, m_norm1_g, v_norm1_g), w_fg2=(w_fg2, m_w_fg2, v_w_fg2),
                        b_fg=(b_fg, m_b_fg, v_b_fg), gla_norm_g=(gla_norm_g, m_gla_norm_g, v_gla_norm_g),
                        conv_w=(conv_w, m_conv_w, v_conv_w), conv_b=(conv_b, m_conv_b, v_conv_b),
                        norm2_g=(norm2_g, m_norm2_g, v_norm2_g), final_g=(final_g, m_final_g, v_final_g))
    order = ["norm1_g", "w_in", "w_fg2", "b_fg", "gla_norm_g", "w_oa", "conv_w", "conv_b", "w_ob", "w_o",
             "norm2_g", "w_ffn_gate", "w_ffn_up", "w_ffn_down", "final_g"]
    results = {}

    def update_large(n):
        w_, m_, v_ = views[n]
        outs = _adamw_layers(w_, (reduced0[n][0], reduced1[n][0]), (reduced0[n][1], reduced1[n][1]), on_core0,
                             m_, v_, name="adamw_" + n)
        back = from_view.get(n)
        results[n] = tuple(back(r) for r in outs) if back else outs
        return outs[1]

    for n, (w_, m_, v_) in small_params.items():
        results[n] = (sg[n],) + _adamw(w_, sg[n], m_, v_, name="adamw_" + n)
    updated = tuple(update_large(n) for n in rest_names)
    share_p = total_and_share(0, swap_p, updated, proj_names, "p0")
    reduced0.update(shared(0, share_p, share_p[3], proj_names, "p0"))
    update_large("w_in")
    return (loss, grad_x[None], *[results[n][0] for n in order], *[results[n][1] for n in order],
            *[results[n][2] for n in order], *[results[n][3] for n in order])
```

```python
import functools

import jax
import jax.numpy as jnp
from jax import lax
from jax.experimental import pallas as pl
from jax.experimental.pallas import tpu as pltpu

F32 = jnp.float32
BF16 = jnp.bfloat16

D_MODEL = 1024
DEPTH = 2
CHUNK = 64
GLA_HEADS = 4
KEY_WIDTH = 512
VAL_WIDTH = 1024
HEAD_K = 128
HEAD_V = 256
GATE_RANK = 16
GATE_TAU = 16.0
CONV_CH = 1024
FFN_HIDDEN = 2816
IN_WIDTH = 8208
NORM_EPS = 1e-6
Q_SCALE = HEAD_K ** -0.5
ADAM_LR = 0.001
ADAM_B1 = 0.9
ADAM_B2 = 0.999
ADAM_EPS = 1e-08
ADAM_WD = 0.01
ADAM_STEP = 10

N_CHIPS = 4
N_DEV = 8

LANE = 128
FZ_PAD = LANE
PROJ_W = 8192 + FZ_PAD
COL_Q, COL_K, COL_V, COL_R, COL_GBI, COL_GCI, COL_CX, COL_GA, COL_GB, COL_FZ = (
    0, 512, 1024, 2048, 3072, 4096, 5120, 6144, 7168, 8192)
FZ_ORIG = 3072

VMEM_LIMIT = 52 * 1024 * 1024
HALO = 16


def _cparams(sem=None):
    return pltpu.CompilerParams(dimension_semantics=sem, vmem_limit_bytes=VMEM_LIMIT)


def _sigmoid(x):
    return jax.nn.sigmoid(x)


def _log_sigmoid(x):
    return jnp.minimum(x, 0.0) - jnp.log1p(jnp.exp(-jnp.abs(x)))


_DOT_DIMS = {
    "nn": (((1,), (0,)), ((), ())),
    "nt": (((1,), (1,)), ((), ())),
    "tn": (((0,), (0,)), ((), ())),
}


def _matmul(a, b, *, mode, out_dtype, tm, tn, tk, name, add=None, dep=None):
    if mode == "nn":
        (m, k), n = a.shape, b.shape[1]
    elif mode == "nt":
        (m, k), n = a.shape, b.shape[0]
    else:
        (k, m), n = a.shape, b.shape[1]
    tm, tn, tk = min(tm, m), min(tn, n), min(tk, k)
    assert m % tm == 0 and n % tn == 0 and k % tk == 0, (name, m, n, k, tm, tn, tk)
    nk = k // tk
    has_add = add is not None
    has_dep = dep is not None

    def body(*refs):
        if has_dep:
            refs = refs[1:]
        if has_add:
            a_ref, b_ref, add_ref, o_ref = refs[:4]
            scratch = refs[4:]
        else:
            a_ref, b_ref, o_ref = refs[:3]
            add_ref = None
            scratch = refs[3:]
        part = lax.dot_general(a_ref[...], b_ref[...], _DOT_DIMS[mode], preferred_element_type=F32)

        def finish(acc):
            if add_ref is not None:
                acc = acc + add_ref[...].astype(F32)
            o_ref[...] = acc.astype(o_ref.dtype)

        if nk == 1:
            finish(part)
        else:
            acc_ref = scratch[0]
            kk = pl.program_id(2)

            @pl.when(kk == 0)
            def _():
                acc_ref[...] = part

            @pl.when(kk > 0)
            def _():
                acc_ref[...] += part

            @pl.when(kk == nk - 1)
            def _():
                finish(acc_ref[...])

    if mode == "nn":
        a_spec = pl.BlockSpec((tm, tk), lambda i, j, kk: (i, kk))
        b_spec = pl.BlockSpec((tk, tn), lambda i, j, kk: (kk, j))
    elif mode == "nt":
        a_spec = pl.BlockSpec((tm, tk), lambda i, j, kk: (i, kk))
        b_spec = pl.BlockSpec((tn, tk), lambda i, j, kk: (j, kk))
    else:
        a_spec = pl.BlockSpec((tk, tm), lambda i, j, kk: (kk, i))
        b_spec = pl.BlockSpec((tk, tn), lambda i, j, kk: (kk, j))
    o_spec = pl.BlockSpec((tm, tn), lambda i, j, kk: (i, j))
    in_specs = [a_spec, b_spec] + ([o_spec] if has_add else [])
    operands = (a, b) + ((add,) if has_add else ())
    if has_dep:
        in_specs = [pl.BlockSpec(dep.shape, lambda i, j, kk: (0, 0))] + in_specs
        operands = (dep,) + operands
    return pl.pallas_call(
        body,
        name=name,
        out_shape=jax.ShapeDtypeStruct((m, n), out_dtype),
        grid=(m // tm, n // tn, nk),
        in_specs=in_specs,
        out_specs=o_spec,
        scratch_shapes=[pltpu.VMEM((tm, tn), F32)] if nk > 1 else [],
        compiler_params=_cparams(("parallel", "parallel", "arbitrary")),
    )(*operands)


def _matmul_norm_bwd(a, b, x, g, dres, *, tm, tk, name, add=None, dep=None):
    (m, k), d = a.shape, b.shape[1]
    tm, tk = min(tm, m), min(tk, k)
    assert m % tm == 0 and k % tk == 0 and x.shape == (m, d), (name, m, k, d, tm, tk)
    nk = k // tk
    has_add = add is not None
    has_dep = dep is not None

    def body(*refs):
        refs = refs[has_dep:]
        a_ref, b_ref = refs[:2]
        add_ref = refs[2] if has_add else None
        x_ref, g_ref, dres_ref, dx_ref, dxb_ref, dg_ref = refs[2 + has_add:8 + has_add]
        scratch = refs[8 + has_add:]
        i = pl.program_id(0)
        part = jnp.dot(a_ref[...], b_ref[...], preferred_element_type=F32)

        def finish(dh):
            if add_ref is not None:
                dh = dh + add_ref[...].astype(F32)
            xv = x_ref[...]
            rs = lax.rsqrt(jnp.mean(xv * xv, axis=-1, keepdims=True) + NORM_EPS)
            nrm = xv * rs
            dn = dh * g_ref[...]
            dx = dres_ref[...] + rs * (dn - nrm * jnp.mean(dn * nrm, axis=-1, keepdims=True))
            dx_ref[...] = dx
            dxb_ref[...] = dx.astype(BF16)
            dg_part = jnp.sum(dh * nrm, axis=0, keepdims=True)

            @pl.when(i == 0)
            def _():
                dg_ref[...] = dg_part

            @pl.when(i > 0)
            def _():
                dg_ref[...] += dg_part

        if nk == 1:
            finish(part)
        else:
            acc_ref = scratch[0]
            kk = pl.program_id(1)

            @pl.when(kk == 0)
            def _():
                acc_ref[...] = part

            @pl.when(kk > 0)
            def _():
                acc_ref[...] += part

            @pl.when(kk == nk - 1)
            def _():
                finish(acc_ref[...])

    row = pl.BlockSpec((tm, d), lambda i, kk: (i, 0))
    vec = pl.BlockSpec((1, d), lambda i, kk: (0, 0))
    in_specs = [pl.BlockSpec((tm, tk), lambda i, kk: (i, kk)), pl.BlockSpec((tk, d), lambda i, kk: (kk, 0))]
    operands = [a, b]
    if has_add:
        in_specs.append(row)
        operands.append(add)
    in_specs += [row, vec, row]
    operands += [x, g.reshape(1, d), dres]
    if has_dep:
        in_specs = [pl.BlockSpec(dep.shape, lambda i, kk: (0, 0))] + in_specs
        operands = [dep] + operands
    return pl.pallas_call(
        body, name=name,
        out_shape=(jax.ShapeDtypeStruct((m, d), F32), jax.ShapeDtypeStruct((m, d), BF16),
                   jax.ShapeDtypeStruct((1, d), F32)),
        grid=(m // tm, nk),
        in_specs=in_specs,
        out_specs=(row, row, vec),
        scratch_shapes=[pltpu.VMEM((tm, d), F32)] if nk > 1 else [],
        compiler_params=_cparams(("arbitrary", "arbitrary")),
    )(*operands)


def _rms_fwd(x, g, *, name):
    s, d = x.shape
    tr = min(512, s)

    def body(x_ref, g_ref, h_ref):
        xv = x_ref[...]
        rs = lax.rsqrt(jnp.mean(xv * xv, axis=-1, keepdims=True) + NORM_EPS)
        h_ref[...] = (xv * rs * g_ref[...]).astype(BF16)

    return pl.pallas_call(
        body, name=name,
        out_shape=jax.ShapeDtypeStruct((s, d), BF16),
        grid=(s // tr,),
        in_specs=[pl.BlockSpec((tr, d), lambda i: (i, 0)), pl.BlockSpec((1, d), lambda i: (0, 0))],
        out_specs=pl.BlockSpec((tr, d), lambda i: (i, 0)),
        compiler_params=_cparams(("parallel",)),
    )(x, g.reshape(1, d))


def _loss_head(x, g, target):
    s, d = x.shape
    tr = min(512, s)

    def body(x_ref, g_ref, t_ref, loss_ref, dx_ref, dxb_ref, dg_ref):
        i = pl.program_id(0)
        xv = x_ref[...]
        rs = lax.rsqrt(jnp.mean(xv * xv, axis=-1, keepdims=True) + NORM_EPS)
        n = xv * rs
        gv = g_ref[...]
        err = n * gv - t_ref[...]
        row_loss = jnp.mean(err * err, axis=-1, keepdims=True)
        loss_part = 0.5 * jnp.sum(row_loss, axis=0, keepdims=True)
        dy = err * (1.0 / d)
        dn = dy * gv
        dx = rs * (dn - n * jnp.mean(dn * n, axis=-1, keepdims=True))
        dx_ref[...] = dx
        dxb_ref[...] = dx.astype(BF16)
        dg_part = jnp.sum(dy * n, axis=0, keepdims=True)

        @pl.when(i == 0)
        def _():
            loss_ref[...] = loss_part
            dg_ref[...] = dg_part

        @pl.when(i > 0)
        def _():
            loss_ref[...] += loss_part
            dg_ref[...] += dg_part

    row = pl.BlockSpec((tr, d), lambda i: (i, 0))
    vec = pl.BlockSpec((1, d), lambda i: (0, 0))
    one = pl.BlockSpec((1, 1), lambda i: (0, 0))
    return pl.pallas_call(
        body, name="loss_head",
        out_shape=(jax.ShapeDtypeStruct((1, 1), F32), jax.ShapeDtypeStruct((s, d), F32),
                   jax.ShapeDtypeStruct((s, d), BF16), jax.ShapeDtypeStruct((1, d), F32)),
        grid=(s // tr,),
        in_specs=[row, vec, row],
        out_specs=(one, row, row, vec),
        compiler_params=_cparams(("arbitrary",)),
    )(x, g.reshape(1, d), target)


def _conv_taps(u_prev, u, w_ref, rows):
    ext = jnp.concatenate([u_prev, u], axis=0)
    u1 = pltpu.roll(ext, 1, 0)[HALO:HALO + rows]
    u2 = pltpu.roll(ext, 2, 0)[HALO:HALO + rows]
    conv = w_ref[0:1, :] * u2 + w_ref[1:2, :] * u1 + w_ref[2:3, :] * u + w_ref[3:4, :]
    return conv, u1, u2


def _conv_fwd(proj, conv_wb, *, s):
    tr = min(512, s)
    c = CONV_CH
    hb = tr // HALO

    def body(gbi_ref, gci_ref, cx_ref, gci_h_ref, cx_h_ref, w_ref, o_ref):
        i = pl.program_id(0)
        u = gci_ref[...].astype(F32) * cx_ref[...].astype(F32)
        u_prev = gci_h_ref[...].astype(F32) * cx_h_ref[...].astype(F32)
        u_prev = jnp.where(i == 0, 0.0, u_prev)
        conv, _, _ = _conv_taps(u_prev, u, w_ref, tr)
        o_ref[...] = (gbi_ref[...].astype(F32) * conv).astype(BF16)

    def seg(col):
        return pl.BlockSpec((tr, c), lambda i: (i, col // c))

    def halo(col):
        return pl.BlockSpec((HALO, c), lambda i: (jnp.maximum(i * hb - 1, 0), col // c))

    return pl.pallas_call(
        body, name="conv_fwd",
        out_shape=jax.ShapeDtypeStruct((s, c), BF16),
        grid=(s // tr,),
        in_specs=[seg(COL_GBI), seg(COL_GCI), seg(COL_CX), halo(COL_GCI), halo(COL_CX),
                  pl.BlockSpec((8, c), lambda i: (0, 0))],
        out_specs=pl.BlockSpec((tr, c), lambda i: (i, 0)),
        compiler_params=_cparams(("parallel",)),
    )(proj, proj, proj, proj, proj, conv_wb)


def _conv_bwd(proj, dcb, conv_wb, dproj, *, s):
    tr = min(512, s)
    c = CONV_CH
    hb = tr // HALO
    nb = s // tr

    def body(gbi_ref, gci_ref, cx_ref, dcb_ref, gci_h_ref, cx_h_ref, gbi_n_ref, dcb_n_ref, w_ref,
             dproj_in_ref, dp_ref, dw_ref):
        i = pl.program_id(0)
        gbi = gbi_ref[...].astype(F32)
        gci = gci_ref[...].astype(F32)
        cx = cx_ref[...].astype(F32)
        dcb_v = dcb_ref[...].astype(F32)
        u = gci * cx
        u_prev = jnp.where(i == 0, 0.0, gci_h_ref[...].astype(F32) * cx_h_ref[...].astype(F32))
        conv, u1, u2 = _conv_taps(u_prev, u, w_ref, tr)
        dconv = dcb_v * gbi
        dconv_next = jnp.where(i == nb - 1, 0.0, dcb_n_ref[...].astype(F32) * gbi_n_ref[...].astype(F32))
        ext = jnp.concatenate([dconv, dconv_next], axis=0)
        n_ext = tr + HALO
        d1 = pltpu.roll(ext, n_ext - 1, 0)[0:tr]
        d2 = pltpu.roll(ext, n_ext - 2, 0)[0:tr]
        du = w_ref[2:3, :] * dconv + w_ref[1:2, :] * d1 + w_ref[0:1, :] * d2
        dp_ref[:, 0:c] = (dcb_v * conv).astype(BF16)
        dp_ref[:, c:2 * c] = (du * cx).astype(BF16)
        dp_ref[:, 2 * c:3 * c] = (du * gci).astype(BF16)
        part = jnp.concatenate([
            jnp.sum(dconv * u2, axis=0, keepdims=True),
            jnp.sum(dconv * u1, axis=0, keepdims=True),
            jnp.sum(dconv * u, axis=0, keepdims=True),
            jnp.sum(dconv, axis=0, keepdims=True),
            jnp.zeros((4, c), F32)], axis=0)

        @pl.when(i == 0)
        def _():
            dw_ref[...] = part

        @pl.when(i > 0)
        def _():
            dw_ref[...] += part

    def seg(col):
        return pl.BlockSpec((tr, c), lambda i: (i, col // c))

    def halo_prev(col):
        return pl.BlockSpec((HALO, c), lambda i: (jnp.maximum(i * hb - 1, 0), col // c))

    def halo_next(col):
        return pl.BlockSpec((HALO, c), lambda i: (jnp.minimum((i + 1) * hb, nb * hb - 1), col // c))

    return pl.pallas_call(
        body, name="conv_bwd",
        out_shape=(jax.ShapeDtypeStruct((s, PROJ_W), BF16), jax.ShapeDtypeStruct((8, c), F32)),
        grid=(nb,),
        in_specs=[seg(COL_GBI), seg(COL_GCI), seg(COL_CX), pl.BlockSpec((tr, c), lambda i: (i, 0)),
                  halo_prev(COL_GCI), halo_prev(COL_CX), halo_next(COL_GBI),
                  pl.BlockSpec((HALO, c), lambda i: (jnp.minimum((i + 1) * hb, nb * hb - 1), 0)),
                  pl.BlockSpec((8, c), lambda i: (0, 0)), ANY],
        out_specs=(pl.BlockSpec((tr, 3 * c), lambda i: (i, COL_GBI // (3 * c))),
                   pl.BlockSpec((8, c), lambda i: (0, 0))),
        input_output_aliases={9: 0},
        compiler_params=_cparams(("arbitrary",)),
    )(proj, proj, proj, dcb, proj, proj, proj, dcb, conv_wb, dproj)


def _token_spec(dep):
    return ([], []) if dep is None else ([dep], [pl.BlockSpec(dep.shape, lambda i: (0, 0))])


def _mixer_out_fwd(oa, cb_in, proj, x0, w_oa, w_ob, w_o, *, s, dep=None):
    tr = min(512, s)
    d = D_MODEL
    n_dep = dep is not None

    def body(*refs):
        oa_ref, cb_ref, ga_ref, gb_ref, x0_ref, woa_ref, wob_ref, wo_ref, ya_ref, yb_ref, mix_ref, x1_ref = (
            refs[n_dep:])
        ya = jnp.dot(oa_ref[...], woa_ref[...], preferred_element_type=F32).astype(BF16)
        yb = jnp.dot(cb_ref[...], wob_ref[...], preferred_element_type=F32).astype(BF16)
        ya_ref[...] = ya
        yb_ref[...] = yb
        sa = _sigmoid(ga_ref[...].astype(F32))
        sb = _sigmoid(gb_ref[...].astype(F32))
        mix = (sa * ya.astype(F32) + sb * yb.astype(F32)).astype(BF16)
        mix_ref[...] = mix
        x1_ref[...] = x0_ref[...] + jnp.dot(mix, wo_ref[...], preferred_element_type=F32)

    row = pl.BlockSpec((tr, d), lambda i: (i, 0))
    full = pl.BlockSpec((d, d), lambda i: (0, 0))
    tok, tok_specs = _token_spec(dep)
    return pl.pallas_call(
        body, name="mixer_out_fwd",
        out_shape=(jax.ShapeDtypeStruct((s, d), BF16), jax.ShapeDtypeStruct((s, d), BF16),
                   jax.ShapeDtypeStruct((s, d), BF16), jax.ShapeDtypeStruct((s, d), F32)),
        grid=(s // tr,),
        in_specs=tok_specs + [row, row, pl.BlockSpec((tr, d), lambda i: (i, COL_GA // d)),
                              pl.BlockSpec((tr, d), lambda i: (i, COL_GB // d)), row, full, full, full],
        out_specs=(row, row, row, row),
        compiler_params=_cparams(("parallel",)),
    )(*tok, oa, cb_in, proj, proj, x0, w_oa, w_ob, w_o)


def _mixer_out_bwd(dx1b, proj, ya, yb, w_oa, w_ob, w_o, *, s, dep=None):
    tr = min(512, s)
    d = D_MODEL
    n_dep = dep is not None

    def body(*refs):
        (dx_ref, ga_ref, gb_ref, ya_ref, yb_ref, woa_ref, wob_ref, wo_ref,
         dya_ref, dyb_ref, dcb_ref, doa_ref, dg_ref) = refs[n_dep:]
        dm = lax.dot_general(dx_ref[...], wo_ref[...], _DOT_DIMS["nt"], preferred_element_type=F32)
        dm = dm.astype(BF16).astype(F32)
        sa = _sigmoid(ga_ref[...].astype(F32))
        sb = _sigmoid(gb_ref[...].astype(F32))
        dya = (dm * sa).astype(BF16)
        dyb = (dm * sb).astype(BF16)
        dya_ref[...] = dya
        dyb_ref[...] = dyb
        dg_ref[:, 0:d] = (dm * ya_ref[...].astype(F32) * sa * (1.0 - sa)).astype(BF16)
        dg_ref[:, d:2 * d] = (dm * yb_ref[...].astype(F32) * sb * (1.0 - sb)).astype(BF16)
        dcb_ref[...] = lax.dot_general(dyb, wob_ref[...], _DOT_DIMS["nt"],
                                       preferred_element_type=F32).astype(BF16)
        doa_ref[...] = lax.dot_general(dya, woa_ref[...], _DOT_DIMS["nt"],
                                       preferred_element_type=F32).astype(BF16)

    row = pl.BlockSpec((tr, d), lambda i: (i, 0))
    full = pl.BlockSpec((d, d), lambda i: (0, 0))
    tok, tok_specs = _token_spec(dep)
    return pl.pallas_call(
        body, name="mixer_out_bwd",
        out_shape=tuple(jax.ShapeDtypeStruct((s, d), BF16) for _ in range(4))
        + (jax.ShapeDtypeStruct((s, PROJ_W), BF16),),
        grid=(s // tr,),
        in_specs=tok_specs + [row, pl.BlockSpec((tr, d), lambda i: (i, COL_GA // d)),
                              pl.BlockSpec((tr, d), lambda i: (i, COL_GB // d)), row, row, full, full, full],
        out_specs=(row, row, row, row, pl.BlockSpec((tr, 2 * d), lambda i: (i, COL_GA // (2 * d)))),
        compiler_params=_cparams(("parallel",)),
    )(*tok, dx1b, proj, proj, ya, yb, w_oa, w_ob, w_o)


def _ffn_out_bwd(dx2b, w_down, gt, up, *, s):
    tr = min(256, s)
    d, f = D_MODEL, FFN_HIDDEN

    def body(dx_ref, wd_ref, gt_ref, up_ref, dgt_ref, dup_ref):
        dh = lax.dot_general(dx_ref[...], wd_ref[...], _DOT_DIMS["nt"], preferred_element_type=F32)
        gv = gt_ref[...].astype(F32)
        uv = up_ref[...].astype(F32)
        sg = _sigmoid(gv)
        dgt_ref[...] = (dh * uv * sg * (1.0 + gv * (1.0 - sg))).astype(BF16)
        dup_ref[...] = (dh * gv * sg).astype(BF16)

    row = pl.BlockSpec((tr, f), lambda i: (i, 0))
    return pl.pallas_call(
        body, name="ffn_out_bwd",
        out_shape=(jax.ShapeDtypeStruct((s, f), BF16), jax.ShapeDtypeStruct((s, f), BF16)),
        grid=(s // tr,),
        in_specs=[pl.BlockSpec((tr, d), lambda i: (i, 0)), pl.BlockSpec((f, d), lambda i: (0, 0)), row, row],
        out_specs=(row, row),
        compiler_params=_cparams(("parallel",)),
    )(dx2b, w_down, gt, up)


def _tri(strict):
    r = lax.broadcasted_iota(jnp.int32, (CHUNK, CHUNK), 0)
    c = lax.broadcasted_iota(jnp.int32, (CHUNK, CHUNK), 1)
    return jnp.where((c < r) if strict else (c <= r), 1.0, 0.0).astype(F32)


def _gla_gate_terms(la_c, tri):
    cum = jnp.dot(tri, la_c, precision=lax.Precision.HIGHEST, preferred_element_type=F32)
    cend = cum[CHUNK - 1:CHUNK, :]
    return jnp.exp(cend - cum), jnp.exp(cend)


def _hk(h):
    return slice(h * HEAD_K, (h + 1) * HEAD_K)


def _hv(h):
    return slice(h * HEAD_V, (h + 1) * HEAD_V)


def _gla_fwd(proj, wfg2p, bfg, gn, *, s):
    tr = min(512, s)
    nb = s // tr
    nc = tr // CHUNK

    def body(q_ref, k_ref, v_ref, r_ref, fz_ref, w_ref, b_ref, gn_ref, oa_ref, st_ref,
             state, la_scr, o_scr):
        i = pl.program_id(0)

        @pl.when(i == 0)
        def _():
            state[...] = jnp.zeros_like(state)

        fg = jnp.dot(fz_ref[...], w_ref[...], preferred_element_type=F32) + b_ref[...]
        la_scr[...] = _log_sigmoid(fg) * (1.0 / GATE_TAU)
        tri = _tri(False)

        def chunk(ci, carry):
            r0 = pl.multiple_of(ci * CHUNK, CHUNK)
            rows = pl.ds(r0, CHUNK)
            e, gam = _gla_gate_terms(la_scr[rows, :], tri)
            kd = (k_ref[rows, :].astype(F32) * e).astype(BF16)
            qs = (q_ref[rows, :].astype(F32) * Q_SCALE).astype(BF16)
            v_c = v_ref[rows, :]
            for h in range(GLA_HEADS):
                upd = lax.dot_general(v_c[:, _hv(h)], kd[:, _hk(h)], _DOT_DIMS["tn"],
                                      preferred_element_type=F32)
                st_h = state[:, _hk(h)] * gam[:, _hk(h)] + upd
                state[:, _hk(h)] = st_h
                o_scr[rows, _hv(h)] = lax.dot_general(qs[:, _hk(h)], st_h.astype(BF16), _DOT_DIMS["nt"],
                                                       preferred_element_type=F32)
            st_ref[ci] = state[...].astype(BF16)
            return carry

        lax.fori_loop(0, nc, chunk, 0, unroll=True)
        for h in range(GLA_HEADS):
            o = o_scr[:, _hv(h)]
            rs = lax.rsqrt(jnp.mean(o * o, axis=-1, keepdims=True) + NORM_EPS)
            rv = r_ref[:, _hv(h)].astype(F32)
            oa_ref[:, _hv(h)] = ((o * rs * gn_ref[...]).astype(F32) * (rv * _sigmoid(rv))).astype(BF16)

    return pl.pallas_call(
        body, name="gla_fwd",
        out_shape=(jax.ShapeDtypeStruct((s, VAL_WIDTH), BF16),
                   jax.ShapeDtypeStruct((s // CHUNK, HEAD_V, KEY_WIDTH), BF16)),
        grid=(nb,),
        in_specs=[pl.BlockSpec((tr, KEY_WIDTH), lambda i: (i, COL_Q // KEY_WIDTH)),
                  pl.BlockSpec((tr, KEY_WIDTH), lambda i: (i, COL_K // KEY_WIDTH)),
                  pl.BlockSpec((tr, VAL_WIDTH), lambda i: (i, COL_V // VAL_WIDTH)),
                  pl.BlockSpec((tr, VAL_WIDTH), lambda i: (i, COL_R // VAL_WIDTH)),
                  pl.BlockSpec((tr, FZ_PAD), lambda i: (i, COL_FZ // FZ_PAD)),
                  pl.BlockSpec((FZ_PAD, KEY_WIDTH), lambda i: (0, 0)),
                  pl.BlockSpec((1, KEY_WIDTH), lambda i: (0, 0)),
                  pl.BlockSpec((1, HEAD_V), lambda i: (0, 0))],
        out_specs=(pl.BlockSpec((tr, VAL_WIDTH), lambda i: (i, 0)),
                   pl.BlockSpec((nc, HEAD_V, KEY_WIDTH), lambda i: (i, 0, 0))),
        scratch_shapes=[pltpu.VMEM((HEAD_V, KEY_WIDTH), F32), pltpu.VMEM((tr, KEY_WIDTH), F32),
                        pltpu.VMEM((tr, VAL_WIDTH), F32)],
        compiler_params=_cparams(("arbitrary",)),
    )(proj, proj, proj, proj, proj, wfg2p, bfg, gn)


def _gla_bwd(proj, doa, states, wfg2p, bfg, gn, dproj, *, s):
    tr = min(512, s)
    nb = s // tr
    nc = tr // CHUNK

    def body(q_ref, k_ref, v_ref, r_ref, fz_ref, doa_ref, st_ref, stp_ref, w_ref, b_ref, gn_ref, dproj_in_ref,
             dp_ref, dfz_ref, dgn_ref, dw_ref, db_ref,
             carry, fg_scr, la_scr, dla_scr, o_scr, do_scr):
        i = pl.program_id(0)

        @pl.when(i == 0)
        def _():
            carry[...] = jnp.zeros_like(carry)
            dgn_ref[...] = jnp.zeros_like(dgn_ref)
            dw_ref[...] = jnp.zeros_like(dw_ref)
            db_ref[...] = jnp.zeros_like(db_ref)

        fz = fz_ref[...]
        fg = jnp.dot(fz, w_ref[...], preferred_element_type=F32) + b_ref[...]
        fg_scr[...] = fg
        la_scr[...] = _log_sigmoid(fg) * (1.0 / GATE_TAU)

        def fwd_chunk(ci, c0):
            r0 = pl.multiple_of(ci * CHUNK, CHUNK)
            rows = pl.ds(r0, CHUNK)
            qs = (q_ref[rows, :].astype(F32) * Q_SCALE).astype(BF16)
            st = st_ref[ci].astype(BF16)
            for h in range(GLA_HEADS):
                o_scr[rows, _hv(h)] = lax.dot_general(qs[:, _hk(h)], st[:, _hk(h)], _DOT_DIMS["nt"],
                                                       preferred_element_type=F32)
            return c0

        lax.fori_loop(0, nc, fwd_chunk, 0, unroll=True)

        gnv = gn_ref[...]
        dgn_part = jnp.zeros((1, HEAD_V), F32)
        for h in range(GLA_HEADS):
            o = o_scr[:, _hv(h)]
            rs = lax.rsqrt(jnp.mean(o * o, axis=-1, keepdims=True) + NORM_EPS)
            nrm = o * rs
            rv = r_ref[:, _hv(h)].astype(F32)
            sg = _sigmoid(rv)
            doa_h = doa_ref[:, _hv(h)].astype(F32)
            don = doa_h * (rv * sg)
            dp_ref[:, COL_R + h * HEAD_V:COL_R + (h + 1) * HEAD_V] = (
                doa_h * (nrm * gnv) * (sg * (1.0 + rv * (1.0 - sg)))).astype(BF16)
            dgn_part = dgn_part + jnp.sum(don * nrm, axis=0, keepdims=True)
            dn = don * gnv
            do_scr[:, _hv(h)] = (rs * (dn - nrm * jnp.mean(dn * nrm, axis=-1, keepdims=True))).astype(BF16)
        dgn_ref[...] += dgn_part

        tri = _tri(False)
        tri_s = _tri(True)
        first_block = i == nb - 1

        def bwd_chunk(cc, c0):
            ci = nc - 1 - cc
            r0 = pl.multiple_of(ci * CHUNK, CHUNK)
            rows = pl.ds(r0, CHUNK)
            e, gam = _gla_gate_terms(la_scr[rows, :], tri)
            k_c = k_ref[rows, :].astype(F32)
            kd = k_c * e
            kd_b = kd.astype(BF16)
            qs = (q_ref[rows, :].astype(F32) * Q_SCALE).astype(BF16)
            v_c = v_ref[rows, :]
            do_c = do_scr[rows, :]
            st_b = st_ref[ci]
            st_prev_in = st_ref[jnp.maximum(ci - 1, 0)].astype(F32)
            st_prev_edge = jnp.where(first_block, 0.0, stp_ref[0].astype(F32))
            st_prev = jnp.where(ci > 0, st_prev_in, st_prev_edge)
            dkd_parts = []
            dgam_parts = []
            for h in range(GLA_HEADS):
                dst = lax.dot_general(do_c[:, _hv(h)], qs[:, _hk(h)], _DOT_DIMS["tn"],
                                      preferred_element_type=F32) + carry[:, _hk(h)]
                dst_b = dst.astype(BF16)
                dqs = jnp.dot(do_c[:, _hv(h)], st_b[:, _hk(h)], preferred_element_type=F32)
                dp_ref[rows, COL_Q + h * HEAD_K:COL_Q + (h + 1) * HEAD_K] = (dqs * Q_SCALE).astype(BF16)
                dkd_parts.append(jnp.dot(v_c[:, _hv(h)], dst_b, preferred_element_type=F32))
                dp_ref[rows, COL_V + h * HEAD_V:COL_V + (h + 1) * HEAD_V] = lax.dot_general(
                    kd_b[:, _hk(h)], dst_b, _DOT_DIMS["nt"], preferred_element_type=F32).astype(BF16)
                dgam_parts.append(jnp.sum(dst * st_prev[:, _hk(h)], axis=0, keepdims=True))
                carry[:, _hk(h)] = dst * gam[:, _hk(h)]
            dkd = jnp.concatenate(dkd_parts, axis=1)
            dgam = jnp.concatenate(dgam_parts, axis=1)
            dp_ref[rows, COL_K:COL_K + KEY_WIDTH] = (dkd * e).astype(BF16)
            dz = dkd * kd
            dla_scr[rows, :] = gam * dgam + jnp.dot(tri_s, dz, precision=lax.Precision.HIGHEST,
                                                    preferred_element_type=F32)
            return c0

        lax.fori_loop(0, nc, bwd_chunk, 0, unroll=True)

        dfg = dla_scr[...] * (1.0 / GATE_TAU) * _sigmoid(-fg_scr[...])
        dfg_b = dfg.astype(BF16)
        dfz_ref[...] = lax.dot_general(dfg_b, w_ref[...], _DOT_DIMS["nt"],
                                       preferred_element_type=F32).astype(BF16)
        dw_ref[...] += lax.dot_general(fz, dfg_b, _DOT_DIMS["tn"], preferred_element_type=F32)
        db_ref[...] += jnp.sum(dfg, axis=0, keepdims=True)

    def rev(i):
        return nb - 1 - i

    return pl.pallas_call(
        body, name="gla_bwd",
        out_shape=(jax.ShapeDtypeStruct((s, PROJ_W), BF16),
                   jax.ShapeDtypeStruct((s, FZ_PAD), BF16),
                   jax.ShapeDtypeStruct((1, HEAD_V), F32),
                   jax.ShapeDtypeStruct((FZ_PAD, KEY_WIDTH), F32),
                   jax.ShapeDtypeStruct((1, KEY_WIDTH), F32)),
        grid=(nb,),
        in_specs=[pl.BlockSpec((tr, KEY_WIDTH), lambda i: (rev(i), COL_Q // KEY_WIDTH)),
                  pl.BlockSpec((tr, KEY_WIDTH), lambda i: (rev(i), COL_K // KEY_WIDTH)),
                  pl.BlockSpec((tr, VAL_WIDTH), lambda i: (rev(i), COL_V // VAL_WIDTH)),
                  pl.BlockSpec((tr, VAL_WIDTH), lambda i: (rev(i), COL_R // VAL_WIDTH)),
                  pl.BlockSpec((tr, FZ_PAD), lambda i: (rev(i), COL_FZ // FZ_PAD)),
                  pl.BlockSpec((tr, VAL_WIDTH), lambda i: (rev(i), 0)),
                  pl.BlockSpec((nc, HEAD_V, KEY_WIDTH), lambda i: (rev(i), 0, 0)),
                  pl.BlockSpec((1, HEAD_V, KEY_WIDTH), lambda i: (jnp.maximum(rev(i) * nc - 1, 0), 0, 0)),
                  pl.BlockSpec((FZ_PAD, KEY_WIDTH), lambda i: (0, 0)),
                  pl.BlockSpec((1, KEY_WIDTH), lambda i: (0, 0)),
                  pl.BlockSpec((1, HEAD_V), lambda i: (0, 0)), ANY],
        out_specs=(pl.BlockSpec((tr, 3 * VAL_WIDTH), lambda i: (rev(i), 0)),
                   pl.BlockSpec((tr, FZ_PAD), lambda i: (rev(i), 0)),
                   pl.BlockSpec((1, HEAD_V), lambda i: (0, 0)),
                   pl.BlockSpec((FZ_PAD, KEY_WIDTH), lambda i: (0, 0)),
                   pl.BlockSpec((1, KEY_WIDTH), lambda i: (0, 0))),
        scratch_shapes=[pltpu.VMEM((HEAD_V, KEY_WIDTH), F32),
                        pltpu.VMEM((tr, KEY_WIDTH), F32),
                        pltpu.VMEM((tr, KEY_WIDTH), F32),
                        pltpu.VMEM((tr, KEY_WIDTH), F32),
                        pltpu.VMEM((tr, VAL_WIDTH), F32),
                        pltpu.VMEM((tr, VAL_WIDTH), BF16)],
        input_output_aliases={11: 0},
        compiler_params=_cparams(("arbitrary",)),
    )(proj, proj, proj, proj, proj, doa, states, states, wfg2p, bfg, gn, dproj)


def _put_fz(dproj, dfz, *, s):
    tr = min(512, s)

    def body(dfz_ref, dproj_in_ref, o_ref):
        o_ref[...] = dfz_ref[...]

    return pl.pallas_call(
        body, name="put_dfz",
        out_shape=jax.ShapeDtypeStruct((s, PROJ_W), BF16),
        grid=(s // tr,),
        in_specs=[pl.BlockSpec((tr, FZ_PAD), lambda i: (i, 0)), ANY],
        out_specs=pl.BlockSpec((tr, FZ_PAD), lambda i: (i, COL_FZ // FZ_PAD)),
        input_output_aliases={1: 0},
        compiler_params=_cparams(("parallel",)),
    )(dfz, dproj)


_ADAM_C1 = 1.0 / (1.0 - ADAM_B1 ** ADAM_STEP)
_ADAM_C2 = 1.0 / (1.0 - ADAM_B2 ** ADAM_STEP)


def _adamw_math(wv, gv, mv, vv):
    nm = ADAM_B1 * mv + (1.0 - ADAM_B1) * gv
    nv = ADAM_B2 * vv + (1.0 - ADAM_B2) * (gv * gv)
    delta = -ADAM_LR * ((nm * _ADAM_C1) / (jnp.sqrt(nv * _ADAM_C2) + ADAM_EPS) + ADAM_WD * wv)
    return delta, nm, nv


def _adamw(w, g, m, v, *, name):
    shape = w.shape
    if w.ndim == 1:
        w, g, m, v = (t.reshape(1, 1, -1) for t in (w, g, m, v))
    elif w.ndim == 2:
        w, g, m, v = (t.reshape((1,) + t.shape) for t in (w, g, m, v))
    l, a, b = w.shape

    def body(w_ref, g_ref, m_ref, v_ref, d_ref, nm_ref, nv_ref):
        d_ref[...], nm_ref[...], nv_ref[...] = _adamw_math(w_ref[...], g_ref[...], m_ref[...], v_ref[...])

    blk = pl.BlockSpec((1, a, b), lambda li: (li, 0, 0))
    outs = pl.pallas_call(
        body, name=name,
        out_shape=tuple(jax.ShapeDtypeStruct((l, a, b), F32) for _ in range(3)),
        grid=(l,),
        in_specs=[blk, blk, blk, blk],
        out_specs=(blk, blk, blk),
        compiler_params=_cparams(("parallel",)),
    )(w, g, m, v)
    return tuple(o.reshape(shape) for o in outs)


def _adamw_layers(w, reduced, received, on_core0, m, v, *, name):
    l, a, b = w.shape
    assert l == DEPTH == 2
    ta = _row_tile(a, F32_SUBLANES, MAX_ADAMW_ROWS)

    def body(flag_ref, w_ref, r0_ref, o0_ref, r1_ref, o1_ref, m_ref, v_ref, g_ref, d_ref, nm_ref, nv_ref):
        core0 = flag_ref[...] > 0.5
        g0 = jnp.where(core0, r0_ref[...], o0_ref[...])
        g1 = jnp.where(core0, o1_ref[...], r1_ref[...])
        gv = jnp.where(pl.program_id(0) == 0, g0, g1)
        g_ref[0] = gv
        d_ref[0], nm_ref[0], nv_ref[0] = _adamw_math(w_ref[0], gv, m_ref[0], v_ref[0])

    blk = pl.BlockSpec((1, ta, b), lambda li, ai: (li, ai, 0))

    def grad_block(layer):
        return pl.BlockSpec((ta, b), lambda li, ai: (jnp.where(li == layer, ai, 0), 0))

    g0blk, g1blk = grad_block(0), grad_block(1)
    return pl.pallas_call(
        body, name=name,
        out_shape=tuple(jax.ShapeDtypeStruct((l, a, b), F32) for _ in range(4)),
        grid=(l, a // ta),
        in_specs=[pl.BlockSpec((1, 1), lambda li, ai: (0, 0)), blk, g0blk, g0blk, g1blk, g1blk, blk, blk],
        out_specs=(blk, blk, blk, blk),
        compiler_params=_cparams(("parallel", "parallel")),
    )(on_core0, w, reduced[0], received[0], reduced[1], received[1], m, v)


MESH_ID = pl.DeviceIdType.MESH
ANY = pl.BlockSpec(memory_space=pl.ANY)


def _position():
    x, y, c = lax.axis_index("x"), lax.axis_index("y"), lax.axis_index("c")
    chips = [(1 - x, y), (x, 1 - y), (1 - x, 1 - y)]
    return x, y, c, chips


def _chip_index(xy):
    return 2 * xy[0] + xy[1]


HBM = pl.BlockSpec(memory_space=pltpu.HBM)
SEM = pl.BlockSpec(memory_space=pltpu.SEMAPHORE)
DATAFLOW_EFFECT = pltpu.SideEffectType.DATAFLOW_SIDE_EFFECTING
TOKEN_SHAPE = (8, LANE)


def _landing(shape, dtype):
    return pltpu.with_memory_space_constraint(lax.empty(shape, dtype), pltpu.HBM)


def _split_start(bufs, sem_shape, issue, *, name, after=None):
    n = len(bufs)
    n_in = n + (after is not None)

    def body(*refs):
        issue(refs[:n], refs[n_in], refs[n_in + 1])
        token = refs[-1]
        token[...] = jnp.zeros_like(token)

    operands = [pltpu.with_memory_space_constraint(t, pltpu.HBM) for t in bufs]
    outs = pl.pallas_call(
        body, name=name,
        out_shape=(pltpu.SemaphoreType.DMA(sem_shape), pltpu.SemaphoreType.DMA(sem_shape),
                   *[pltpu.HBM(t.shape, t.dtype) for t in bufs], jax.ShapeDtypeStruct(TOKEN_SHAPE, F32)),
        in_specs=[HBM] * n + [ANY] * (after is not None),
        out_specs=(SEM, SEM, *[HBM] * n, pl.BlockSpec(memory_space=pltpu.VMEM)),
        input_output_aliases={i: 2 + i for i in range(n)},
        compiler_params=pltpu.CompilerParams(has_side_effects=DATAFLOW_EFFECT),
    )(*operands, *([after] if after is not None else []))
    return outs[0], outs[1], list(outs[2:2 + n]), outs[-1]


def _split_wait(started, after, settle, *, name):
    send_sems, recv_sems, bufs, _ = started
    n = len(bufs)
    afters = tuple(after) if isinstance(after, (tuple, list)) else (after,)

    def body(*refs):
        settle(refs[:n], refs[n], refs[n + 1])

    outs = pl.pallas_call(
        body, name=name,
        out_shape=tuple(pltpu.HBM(t.shape, t.dtype) for t in bufs),
        in_specs=[HBM] * n + [SEM, SEM] + [ANY] * len(afters),
        out_specs=tuple([HBM] * n),
        input_output_aliases={i: i for i in range(n)},
        compiler_params=pltpu.CompilerParams(has_side_effects=DATAFLOW_EFFECT),
    )(*bufs, send_sems, recv_sems, *afters)
    return list(outs)


def _to_sibling(bufs, n, send_sems, recv_sems):
    x, y, c, _ = _position()
    return [pltpu.make_async_remote_copy(
        src_ref=bufs[a], dst_ref=bufs[n + a], send_sem=send_sems.at[a], recv_sem=recv_sems.at[a],
        device_id=(x, y, 1 - c), device_id_type=MESH_ID) for a in range(n)]


def _sibling_push_start(layer, sender_is_reducer, arrays, *, name):
    n = len(arrays)
    sender = layer if sender_is_reducer else 1 - layer

    def issue(bufs, send_sems, recv_sems):
        @pl.when(lax.axis_index("c") == sender)
        def _():
            for cp in _to_sibling(bufs, n, send_sems, recv_sems):
                cp.start()

    lands = [_landing(t.shape, t.dtype) for t in arrays]
    return _split_start(list(arrays) + lands, (n,), issue, name=name)


def _sibling_push_wait(layer, sender_is_reducer, started, after, *, name):
    n = len(started[2]) // 2
    sender = layer if sender_is_reducer else 1 - layer

    def settle(bufs, send_sems, recv_sems):
        c = lax.axis_index("c")

        @pl.when(c == sender)
        def _():
            for cp in _to_sibling(bufs, n, send_sems, recv_sems):
                cp.wait_send()

        @pl.when(c != sender)
        def _():
            for cp in _to_sibling(bufs, n, send_sems, recv_sems):
                cp.wait_recv()

    outs = _split_wait(started, after, settle, name=name)
    return outs[:n], outs[n:]


def _chip_copies(bufs, n, send_sems, recv_sems):
    x, y, c, chips = _position()
    return [pltpu.make_async_remote_copy(
        src_ref=bufs[a].at[_chip_index(chips[k])], dst_ref=bufs[n + a].at[k],
        send_sem=send_sems.at[3 * a + k], recv_sem=recv_sems.at[3 * a + k],
        device_id=(chips[k][0], chips[k][1], c), device_id_type=MESH_ID)
        for a in range(n) for k in range(3)]


def _chip_exchange_start(layer, parts, *, name):
    n = len(parts)

    def issue(bufs, send_sems, recv_sems):
        @pl.when(lax.axis_index("c") == layer)
        def _():
            for cp in _chip_copies(bufs, n, send_sems, recv_sems):
                cp.start()

    lands = [_landing((3,) + t.shape[1:], t.dtype) for t in parts]
    return _split_start(list(parts) + lands, (3 * n,), issue, name=name)


def _chip_exchange_wait(layer, started, after, *, name):
    n = len(started[2]) // 2

    def settle(bufs, send_sems, recv_sems):
        @pl.when(lax.axis_index("c") == layer)
        def _():
            for cp in _chip_copies(bufs, n, send_sems, recv_sems):
                cp.wait()

    outs = _split_wait(started, after, settle, name=name)
    return outs[:n], outs[n:]


def _gather_copies(layer, bufs, n, send_sems, recv_sems, arriving):
    x, y, c, chips = _position()
    me = _chip_index((x, y))
    return [pltpu.make_async_remote_copy(
        src_ref=bufs[a], dst_ref=bufs[n + a].at[_chip_index(chips[k]) if arriving else me],
        send_sem=send_sems.at[3 * a + k], recv_sem=recv_sems.at[3 * a + k],
        device_id=(chips[k][0], chips[k][1], c), device_id_type=MESH_ID)
        for a in range(n) for k in range(3)]


def _gather_start(layer, shards, *, name, after=None):
    n = len(shards)

    def issue(bufs, send_sems, recv_sems):
        @pl.when(lax.axis_index("c") == layer)
        def _():
            for cp in _gather_copies(layer, bufs, n, send_sems, recv_sems, False):
                cp.start()

    lands = [_landing((N_CHIPS,) + t.shape, t.dtype) for t in shards]
    return _split_start(list(shards) + lands, (3 * n,), issue, name=name, after=after)


def _gather_wait(layer, started, after, *, name):
    n = len(started[2]) // 2

    def settle(bufs, send_sems, recv_sems):
        @pl.when(lax.axis_index("c") == layer)
        def _():
            for cp in _gather_copies(layer, bufs, n, send_sems, recv_sems, False):
                cp.wait_send()
            for cp in _gather_copies(layer, bufs, n, send_sems, recv_sems, True):
                cp.wait_recv()

    return _split_wait(started, after, settle, name=name)[n:]


def _handoff_copies(bufs, n, send_sems, recv_sems):
    x, y, c, chips = _position()
    out = []
    for a in range(n):
        for k in range(3):
            slot = bufs[a].at[_chip_index(chips[k])]
            out.append(pltpu.make_async_remote_copy(
                src_ref=slot, dst_ref=slot, send_sem=send_sems.at[3 * a + k], recv_sem=recv_sems.at[3 * a + k],
                device_id=(x, y, 1 - c), device_id_type=MESH_ID))
    return out


def _handoff_start(layer, gathered, *, name):
    n = len(gathered)

    def issue(bufs, send_sems, recv_sems):
        @pl.when(lax.axis_index("c") == layer)
        def _():
            for cp in _handoff_copies(bufs, n, send_sems, recv_sems):
                cp.start()

    return _split_start(list(gathered), (3 * n,), issue, name=name)


def _handoff_wait(layer, started, after, *, name):
    n = len(started[2])

    def settle(bufs, send_sems, recv_sems):
        c = lax.axis_index("c")

        @pl.when(c == layer)
        def _():
            for cp in _handoff_copies(bufs, n, send_sems, recv_sems):
                cp.wait_send()

        @pl.when(c != layer)
        def _():
            for cp in _handoff_copies(bufs, n, send_sems, recv_sems):
                cp.wait_recv()

    return _split_wait(started, after, settle, name=name)


F32_SUBLANES = 8
BF16_SUBLANES = 16
MAX_STREAM_ROWS = 704
MAX_ADAMW_ROWS = 384


def _row_tile(a, sublanes=BF16_SUBLANES, max_rows=MAX_STREAM_ROWS):
    best = None
    for cand in range(sublanes, min(a, max_rows) + 1, sublanes):
        if a % cand == 0:
            best = cand
    assert best is not None, a
    return best


def _pair_sum(mine, sib, active, *, name):
    nchip, a, b = sib.shape
    ta = _row_tile(a)

    def body(act_ref, m_ref, s_ref, o_ref):
        @pl.when(act_ref[0] == 1)
        def _():
            o_ref[...] = (m_ref[...].astype(F32) + s_ref[...].astype(F32)).astype(BF16)

    blk = pl.BlockSpec((1, ta, b), lambda j, r, act: (j * act[0], r * act[0], 0))
    return pl.pallas_call(
        body, name=name,
        out_shape=jax.ShapeDtypeStruct((nchip, a, b), BF16),
        grid_spec=pltpu.PrefetchScalarGridSpec(
            num_scalar_prefetch=1, grid=(nchip, a // ta), in_specs=[blk, blk], out_specs=blk),
        compiler_params=_cparams(("arbitrary", "arbitrary")),
    )(active, mine, sib)


def _total_sum(own, recv, active, *, name):
    a, b = own.shape
    ta = _row_tile(a)

    def body(act_ref, o_ref, r_ref, t_ref):
        @pl.when(act_ref[0] == 1)
        def _():
            acc = o_ref[...].astype(F32)
            for k in range(3):
                acc = acc + r_ref[k].astype(F32)
            t_ref[...] = acc

    blk = pl.BlockSpec((ta, b), lambda r, act: (r * act[0], 0))
    return pl.pallas_call(
        body, name=name,
        out_shape=jax.ShapeDtypeStruct((a, b), F32),
        grid_spec=pltpu.PrefetchScalarGridSpec(
            num_scalar_prefetch=1, grid=(a // ta,),
            in_specs=[blk, pl.BlockSpec((3, ta, b), lambda r, act: (0, r * act[0], 0))], out_specs=blk),
        compiler_params=_cparams(("arbitrary",)),
    )(active, own, recv)


def _all_reduce_small(packed):
    rows, width = packed.shape

    def body(x_ref, out_ref, gath, send_sems, recv_sems, local_sem):
        x, y, c, chips = _position()
        me, sibling = (x, y, c), (x, y, 1 - c)

        def slot(px, py, pc):
            return gath.at[4 * px + 2 * py + pc]

        def copy(k, block, to, src=None):
            return pltpu.make_async_remote_copy(
                src_ref=slot(*block) if src is None else src, dst_ref=slot(*block),
                send_sem=send_sems.at[k], recv_sem=recv_sems.at[k], device_id=to, device_id_type=MESH_ID)

        mine = pltpu.make_async_copy(x_ref, slot(*me), local_sem)
        mine.start()
        first = [copy(0, me, sibling, src=x_ref)]
        first += [copy(1 + j, me, (*chip, c), src=x_ref) for j, chip in enumerate(chips)]
        for cp in first:
            cp.start()
        passed = [copy(4 + j, (*chip, c), sibling) for j, chip in enumerate(chips)]
        for j, chip in enumerate(chips):
            copy(1 + j, (*chip, c), me).wait_recv()
            passed[j].start()
        copy(0, sibling, me).wait_recv()
        for j, chip in enumerate(chips):
            copy(4 + j, (*chip, 1 - c), me).wait_recv()
        for cp in first + passed:
            cp.wait_send()
        mine.wait()
        acc = gath[0]
        for d in range(1, N_DEV):
            acc = acc + gath[d]
        out_ref[...] = acc

    return pl.pallas_call(
        body, name="all_reduce_small",
        out_shape=jax.ShapeDtypeStruct((rows, width), F32),
        in_specs=[pl.BlockSpec(memory_space=pltpu.VMEM)],
        out_specs=pl.BlockSpec(memory_space=pltpu.VMEM),
        scratch_shapes=[pltpu.VMEM((N_DEV, rows, width), F32), pltpu.SemaphoreType.DMA((7,)),
                        pltpu.SemaphoreType.DMA((7,)), pltpu.SemaphoreType.DMA],
    )(packed)


def _mixer_forward_branches(x0, h, w, s, dep=None, before_conv=None):
    proj = _matmul(h, w["w_in_t"], mode="nt", out_dtype=BF16, tm=1024, tn=1664, tk=1024, name="proj_fwd",
                   dep=dep)
    oa, states = _gla_fwd(proj, w["w_fg2"], w["b_fg"], w["gla_norm_g"], s=s)
    conv_wb = w["conv_wb"]
    if before_conv is not None:
        conv_wb = conv_wb + before_conv(oa)[0, 0]
    cb_in = _conv_fwd(proj, conv_wb, s=s)
    return dict(x0=x0, h=h, proj=proj, oa=oa, states=states, cb_in=cb_in)


def _mixer_forward_out(sv, w, s, dep=None):
    ya, yb, mix, x1 = _mixer_out_fwd(sv["oa"], sv["cb_in"], sv["proj"], sv["x0"], w["w_oa"], w["w_ob"], w["w_o"],
                                     s=s, dep=dep)
    return x1, dict(sv, ya=ya, yb=yb, mix=mix)


def _ffn_forward_hidden(x1, w, s, dep=None):
    tr = min(256, s)
    d, f = D_MODEL, FFN_HIDDEN
    n_dep = dep is not None

    def body(*refs):
        x_ref, g_ref, wg_ref, wu_ref, h2_ref, gt_ref, up_ref, hid_ref = refs[n_dep:]
        xv = x_ref[...]
        rs = lax.rsqrt(jnp.mean(xv * xv, axis=-1, keepdims=True) + NORM_EPS)
        h2 = (xv * rs * g_ref[...]).astype(BF16)
        h2_ref[...] = h2
        gt = lax.dot_general(h2, wg_ref[...], _DOT_DIMS["nt"], preferred_element_type=F32).astype(BF16)
        up = lax.dot_general(h2, wu_ref[...], _DOT_DIMS["nt"], preferred_element_type=F32).astype(BF16)
        gt_ref[...] = gt
        up_ref[...] = up
        gv = gt.astype(F32)
        hid_ref[...] = (gv * _sigmoid(gv) * up.astype(F32)).astype(BF16)

    row_d = pl.BlockSpec((tr, d), lambda i: (i, 0))
    row_f = pl.BlockSpec((tr, f), lambda i: (i, 0))
    weight = pl.BlockSpec((f, d), lambda i: (0, 0))
    tok, tok_specs = _token_spec(dep)
    h2, gt, up, hid = pl.pallas_call(
        body, name="ffn_hidden_fwd",
        out_shape=(jax.ShapeDtypeStruct((s, d), BF16),) + tuple(jax.ShapeDtypeStruct((s, f), BF16) for _ in range(3)),
        grid=(s // tr,),
        in_specs=tok_specs + [row_d, pl.BlockSpec((1, d), lambda i: (0, 0)), weight, weight],
        out_specs=(row_d, row_f, row_f, row_f),
        compiler_params=_cparams(("parallel",)),
    )(*tok, x1, w["norm2_g"].reshape(1, d), w["w_gate_t"], w["w_up_t"])
    return dict(x1=x1, h2=h2, gt=gt, up=up, hid=hid)


def _ffn_forward_out(sv, w, dep=None):
    return _matmul(sv["hid"], w["w_ffn_down"], mode="nn", out_dtype=F32, tm=1024, tn=1024, tk=FFN_HIDDEN,
                   name="ffn_out_fwd", add=sv["x1"], dep=dep)


def _ffn_backward_grads(dx2b, w, sv, s):
    g = {}
    dgt, dup = _ffn_out_bwd(dx2b, w["w_ffn_down"], sv["gt"], sv["up"], s=s)
    g["w_ffn_down"] = _matmul(sv["hid"], dx2b, mode="tn", out_dtype=BF16, tm=1408, tn=1024, tk=1024,
                              name="ffn_out_wgrad")
    g["w_gate_t"] = _matmul(dgt, sv["h2"], mode="tn", out_dtype=BF16, tm=1408, tn=1024, tk=1024,
                            name="ffn_gate_wgrad")
    g["w_up_t"] = _matmul(dup, sv["h2"], mode="tn", out_dtype=BF16, tm=1408, tn=1024, tk=1024,
                          name="ffn_up_wgrad")
    return g, dgt, dup


def _ffn_backward_input(dgt, dup, dx2, w, sv, dep=None):
    dh2 = _matmul(dgt, w["w_gate_t"], mode="nn", out_dtype=BF16, tm=1024, tn=1024, tk=FFN_HIDDEN,
                  name="ffn_gate_bwd", dep=dep)
    return _matmul_norm_bwd(dup, w["w_up_t"], sv["x1"], w["norm2_g"], dx2, tm=512, tk=FFN_HIDDEN,
                            name="ffn_up_bwd_norm2_bwd", add=dh2)


def _mixer_backward_branches(dx1b, w, sv, s, dep=None):
    g = {}
    dya, dyb, dcb, doa, dproj = _mixer_out_bwd(dx1b, sv["proj"], sv["ya"], sv["yb"], w["w_oa"], w["w_ob"],
                                               w["w_o"], s=s, dep=dep)
    g["w_o"] = _matmul(sv["mix"], dx1b, mode="tn", out_dtype=BF16, tm=1024, tn=1024, tk=2048, name="wo_wgrad")
    g["w_ob"] = _matmul(sv["cb_in"], dyb, mode="tn", out_dtype=BF16, tm=1024, tn=1024, tk=2048, name="yb_wgrad")
    g["w_oa"] = _matmul(sv["oa"], dya, mode="tn", out_dtype=BF16, tm=1024, tn=1024, tk=2048, name="ya_wgrad")
    dproj, g["conv_wb"] = _conv_bwd(sv["proj"], dcb, w["conv_wb"], dproj, s=s)
    dproj, dfz, g["gla_norm_g"], g["w_fg2"], g["b_fg"] = _gla_bwd(
        sv["proj"], doa, sv["states"], w["w_fg2"], w["b_fg"], w["gla_norm_g"], dproj, s=s)
    return g, _put_fz(dproj, dfz, s=s)


def _proj_wgrad(dproj, sv, dep=None):
    return _matmul(dproj, sv["h"], mode="tn", out_dtype=BF16, tm=1664, tn=1024, tk=1024, name="proj_wgrad",
                   dep=dep)


def _proj_bwd(dproj, w, x0, dres, dep=None):
    return _matmul_norm_bwd(dproj, w["w_in_t"], x0, w["norm1_g"], dres, tm=512, tk=1664,
                            name="proj_bwd_norm1_bwd", dep=dep)


def _cols_from_chips(t):
    return jnp.transpose(t, (1, 0, 2)).reshape(t.shape[1], -1)


W_IN_ROWS = IN_WIDTH // N_CHIPS
W_IN_ROWS_PAD = -(-W_IN_ROWS // BF16_SUBLANES) * BF16_SUBLANES


def _w_in_t_shard(t):
    return jnp.pad(jnp.transpose(t, (0, 2, 1)), ((0, 0), (0, W_IN_ROWS_PAD - W_IN_ROWS), (0, 0)))


def _w_in_t_unshard(t):
    return jnp.transpose(t[:, :W_IN_ROWS], (0, 2, 1))


def _w_in_row_segments():
    runs = [(0, 0, FZ_ORIG), (FZ_ORIG, FZ_ORIG + GATE_RANK, COL_FZ - FZ_ORIG), (COL_FZ, FZ_ORIG, GATE_RANK)]
    out = []
    for kernel_row, ref_row, length in runs:
        while length:
            chip, local = divmod(ref_row, W_IN_ROWS)
            n = min(length, W_IN_ROWS - local)
            out.append((kernel_row, chip * W_IN_ROWS_PAD + local, n))
            kernel_row, ref_row, length = kernel_row + n, ref_row + n, length - n
    return out


def _permute_rows(src, n_out, segments, *, name):
    n_src, width = src.shape
    block = _row_tile(n_out)
    window = block + BF16_SUBLANES
    assert n_src >= window and src.dtype == BF16

    def body(src_ref, out_ref):
        rows = lax.broadcasted_iota(jnp.int32, (block, window), 0)
        cols = lax.broadcasted_iota(jnp.int32, (block, window), 1)
        for b0 in range(0, n_out, block):
            acc = None
            for o, s, n in segments:
                lo, hi = max(o, b0), min(o + n, b0 + block)
                if lo >= hi:
                    continue
                first = s + (lo - o)
                base = max(0, min(first // BF16_SUBLANES * BF16_SUBLANES, n_src - window))
                shift = (first - base) - (lo - b0)
                pick = (cols == rows + shift) & (rows >= lo - b0) & (rows < hi - b0)
                part = jnp.dot(jnp.where(pick, 1.0, 0.0).astype(BF16), src_ref[base:base + window, :],
                               preferred_element_type=F32)
                acc = part if acc is None else acc + part
            out_ref[b0:b0 + block, :] = (jnp.zeros((block, width), F32) if acc is None else acc).astype(BF16)

    return pl.pallas_call(
        body, name=name,
        out_shape=jax.ShapeDtypeStruct((n_out, width), BF16),
        in_specs=[pl.BlockSpec(memory_space=pltpu.VMEM)],
        out_specs=pl.BlockSpec(memory_space=pltpu.VMEM),
        compiler_params=_cparams(),
    )(src)


def _w_in_t_to_kernel(t):
    return _permute_rows(t.reshape(N_CHIPS * W_IN_ROWS_PAD, D_MODEL), PROJ_W, _w_in_row_segments(),
                         name="w_in_to_kernel_rows")


def _w_in_t_from_kernel(g):
    segments = [(slab, kernel_row, n) for kernel_row, slab, n in _w_in_row_segments()]
    out = _permute_rows(g, N_CHIPS * W_IN_ROWS_PAD, segments, name="w_in_to_chip_rows")
    return out.reshape(N_CHIPS, W_IN_ROWS_PAD, D_MODEL)


def kernel(x, norm1_g, w_in, w_fg2, b_fg, gla_norm_g, w_oa, conv_w, conv_b, w_ob, w_o, norm2_g, w_ffn_gate, w_ffn_up, w_ffn_down, final_g, loss_target, m_norm1_g, m_w_in, m_w_fg2, m_b_fg, m_gla_norm_g, m_w_oa, m_conv_w, m_conv_b, m_w_ob, m_w_o, m_norm2_g, m_w_ffn_gate, m_w_ffn_up, m_w_ffn_down, m_final_g, v_norm1_g, v_w_in, v_w_fg2, v_b_fg, v_gla_norm_g, v_w_oa, v_conv_w, v_conv_b, v_w_ob, v_w_o, v_norm2_g, v_w_ffn_gate, v_w_ffn_up, v_w_ffn_down, v_final_g):
    cx_ = lax.axis_index("x")
    cy_ = lax.axis_index("y")
    cc_ = lax.axis_index("c")
    me = 2 * cx_ + cy_
    on_core0 = jnp.where(cc_ == 0, 1.0, 0.0).astype(F32).reshape(1, 1)

    def swap(t):
        return jnp.swapaxes(t, 1, 2)

    big_names = ["w_in", "w_oa", "w_ob", "w_o", "w_ffn_gate", "w_ffn_up", "w_ffn_down"]
    views = dict(
        w_in=tuple(_w_in_t_shard(t) for t in (w_in, m_w_in, v_w_in)),
        w_oa=(w_oa, m_w_oa, v_w_oa), w_ob=(w_ob, m_w_ob, v_w_ob), w_o=(w_o, m_w_o, v_w_o),
        w_ffn_gate=tuple(swap(t) for t in (w_ffn_gate, m_w_ffn_gate, v_w_ffn_gate)),
        w_ffn_up=tuple(swap(t) for t in (w_ffn_up, m_w_ffn_up, v_w_ffn_up)),
        w_ffn_down=(w_ffn_down, m_w_ffn_down, v_w_ffn_down))
    from_view = dict(w_in=_w_in_t_unshard, w_ffn_gate=swap, w_ffn_up=swap)

    s = x.shape[1]

    proj_names = ["w_in"]
    first_names = ["w_in", "w_fg2", "conv_w"]
    rest_names = ["w_oa", "w_ob", "w_o", "w_ffn_gate", "w_ffn_up", "w_ffn_down"]
    mixer_names = ["w_oa", "w_ob", "w_o"]
    ffn_names = ["w_ffn_gate", "w_ffn_up", "w_ffn_down"]
    weight_key = dict(w_in="w_in_t", w_ffn_gate="w_gate_t", w_ffn_up="w_up_t")
    full_shape = dict(w_oa=(VAL_WIDTH, D_MODEL), w_ob=(CONV_CH, D_MODEL), w_o=(D_MODEL, D_MODEL),
                      w_ffn_gate=(FFN_HIDDEN, D_MODEL), w_ffn_up=(FFN_HIDDEN, D_MODEL),
                      w_ffn_down=(FFN_HIDDEN, D_MODEL))

    conv_w_p = jnp.pad(conv_w, ((0, 0), (0, 8 - conv_w.shape[1]), (0, 0)))
    small_shards = dict(w_fg2=w_fg2, conv_w=conv_w_p)

    def shards_of(l, names):
        return [small_shards[n][l] if n in small_shards else views[n][0][l].astype(BF16) for n in names]

    def small_weights(l):
        return dict(norm1_g=norm1_g[l], norm2_g=norm2_g[l], b_fg=b_fg[l].reshape(1, KEY_WIDTH),
                    gla_norm_g=gla_norm_g[l].reshape(1, HEAD_V))

    def full_weights(l, names, gathered, shards):
        out = {}
        for n, g, t in zip(names, gathered, shards):
            g = lax.dynamic_update_index_in_dim(g, t[None], me, axis=0)
            if n == "w_in":
                out["w_in_t"] = _w_in_t_to_kernel(g)
            elif n == "w_fg2":
                out[n] = jnp.pad(_cols_from_chips(g), ((0, FZ_PAD - GATE_RANK), (0, 0))).astype(BF16)
            elif n == "conv_w":
                out["conv_wb"] = (jnp.pad(_cols_from_chips(g)[:3], ((0, 5), (0, 0)))
                                  + jnp.pad(conv_b[l].reshape(1, CONV_CH), ((3, 4), (0, 0))))
            else:
                out[weight_key.get(n, n)] = g.reshape(full_shape[n])
        return out

    def per_chip(g, names):
        out = []
        for n in names:
            t = g[weight_key.get(n, n)]
            out.append(_w_in_t_from_kernel(t) if n == "w_in"
                       else t.reshape(N_CHIPS, t.shape[0] // N_CHIPS, t.shape[1]))
        return out

    def gather(l, names, tag, after=None):
        shards = shards_of(l, names)
        return shards, _gather_start(l, shards, name=f"gather_{tag}_start", after=after)

    def gathered_to_sibling(l, started, after, tag):
        return _handoff_start(l, _gather_wait(l, started, after, name=f"gather_{tag}_wait"),
                              name=f"handoff_{tag}_start")

    def feed(l, g, names, tag):
        return _sibling_push_start(l, False, per_chip(g, names), name=f"feed_{tag}_start")

    def reduces(l):
        return jnp.where(cc_ == l, 1, 0).astype(jnp.int32).reshape(1)

    def pair_sums(l, fed, after, names, tag):
        mine, sib = _sibling_push_wait(l, False, fed, after, name=f"feed_{tag}_wait")
        return [_pair_sum(a, b, reduces(l), name=f"pair_sum_{tag}_{n}") for n, a, b in zip(names, mine, sib)]

    def exchange(l, parts, tag):
        return _chip_exchange_start(l, parts, name=f"exchange_{tag}_start")

    def totals_of(l, swapped, after, names, tag):
        parts, recvs = _chip_exchange_wait(l, swapped, after, name=f"exchange_{tag}_wait")
        owns = [lax.dynamic_index_in_dim(p, me, 0, keepdims=False) for p in parts]
        return [_total_sum(o, rc, reduces(l), name=f"total_{tag}_{n}") for n, o, rc in zip(names, owns, recvs)]

    def share(l, totals, tag):
        return _sibling_push_start(l, True, totals, name=f"share_{tag}_start")

    def shared(l, sharing, after, names, tag):
        totals, others = _sibling_push_wait(l, True, sharing, after, name=f"share_{tag}_wait")
        return {n: (t, o) for n, t, o in zip(names, totals, others)}

    def branches_under_gather(l, xin, h, w, tag, shards_r, started_r):
        box = {}

        def before_conv(oa):
            box["handoff"] = gathered_to_sibling(l, started_r, oa, tag)
            return box["handoff"][3]

        sv = _mixer_forward_branches(xin, h, w, s, dep=started_r[3], before_conv=before_conv)
        rest = _handoff_wait(l, box["handoff"], sv["cb_in"], name=f"handoff_{tag}_wait")
        w.update(full_weights(l, rest_names, rest, shards_r))
        return sv

    shards_p, started = gather(0, first_names, "p0")
    h0 = _rms_fwd(x[0], norm1_g[0], name="rms1_fwd")
    later_shards = shards_of(0, rest_names) + shards_of(1, first_names) + shards_of(1, rest_names)
    started = gathered_to_sibling(0, started, (h0, views["w_in"][1], *later_shards), "p0")
    w0 = small_weights(0)
    w0.update(full_weights(0, first_names, _handoff_wait(0, started, views["w_in"][2], name="handoff_p0_wait"),
                           shards_p))
    sv0 = branches_under_gather(0, x[0], h0, w0, "r0", *gather(0, rest_names, "r0", after=w0["w_in_t"]))

    shards_p1, started = gather(1, first_names, "p1", after=w0["w_o"])
    x1, sv0m = _mixer_forward_out(sv0, w0, s, dep=started[3])
    sv0f = _ffn_forward_hidden(x1, w0, s)
    handoff_1 = gathered_to_sibling(1, started, sv0f["hid"], "p1")
    rest_1 = gather(1, rest_names, "r1", after=handoff_1[2][0])
    x2 = _ffn_forward_out(sv0f, w0, dep=handoff_1[3] + rest_1[1][3])
    w1 = small_weights(1)
    w1.update(full_weights(1, first_names, _handoff_wait(1, handoff_1, x2, name="handoff_p1_wait"), shards_p1))

    h1 = _rms_fwd(x2, norm1_g[1], name="rms1_fwd")
    x3, sv1m = _mixer_forward_out(branches_under_gather(1, x2, h1, w1, "r1", *rest_1), w1, s)
    sv1f = _ffn_forward_hidden(x3, w1, s)
    x4 = _ffn_forward_out(sv1f, w1)
    loss_local, dx, dxb, dgf = _loss_head(x4, final_g, loss_target[0])
    loss = lax.psum(loss_local[0, 0], ("x", "y", "c"))

    g1, dgt, dup = _ffn_backward_grads(dxb, w1, sv1f, s)
    dx_mid, dxb_mid, g1["norm2_g"] = _ffn_backward_input(dgt, dup, dx, w1, sv1f)
    gm, dproj = _mixer_backward_branches(dxb_mid, w1, sv1m, s)
    g1.update(gm)
    g1["w_in_t"] = _proj_wgrad(dproj, sv1m)
    fed_1 = feed(1, g1, big_names, "l1")
    dx, dxb, g1["norm1_g"] = _proj_bwd(dproj, w1, sv1m["x0"], dx_mid, dep=fed_1[3])

    g0, dgt, dup = _ffn_backward_grads(dxb, w0, sv0f, s)
    fed_f = feed(0, g0, ffn_names, "f0")
    dx_mid, dxb_mid, g0["norm2_g"] = _ffn_backward_input(dgt, dup, dx, w0, sv0f, dep=fed_f[3])
    parts_1 = pair_sums(1, fed_1, dx_mid, big_names, "l1")
    parts_f = pair_sums(0, fed_f, dx_mid, ffn_names, "f0")
    swap_1, swap_f = exchange(1, parts_1, "l1"), exchange(0, parts_f, "f0")
    gm, dproj = _mixer_backward_branches(dxb_mid, w0, sv0m, s, dep=swap_1[3] + swap_f[3])
    g0.update(gm)
    totals_1 = totals_of(1, swap_1, dproj, big_names, "l1")
    totals_f = totals_of(0, swap_f, dproj, ffn_names, "f0")
    share_1, share_f = share(1, totals_1, "l1"), share(0, totals_f, "f0")
    fed_m = feed(0, g0, mixer_names, "m0")
    g0["w_in_t"] = _proj_wgrad(dproj, sv0m, dep=share_1[3] + share_f[3] + fed_m[3])
    fed_p = feed(0, g0, proj_names, "p0")
    swap_m = exchange(0, pair_sums(0, fed_m, g0["w_in_t"], mixer_names, "m0"), "m0")
    grad_x, _, g0["norm1_g"] = _proj_bwd(dproj, w0, sv0m["x0"], dx_mid, dep=fed_p[3] + swap_m[3])
    reduced1 = shared(1, share_1, grad_x, big_names, "l1")
    reduced0 = shared(0, share_f, grad_x, ffn_names, "f0")

    parts_p = pair_sums(0, fed_p, grad_x, proj_names, "p0")
    totals_m = totals_of(0, swap_m, parts_p[0], mixer_names, "m0")
    swap_p = exchange(0, parts_p, "p0")
    share_m = share(0, totals_m, "m0")
    reduced0.update(shared(0, share_m, share_m[3], mixer_names, "m0"))
    grads = [g0, g1]

    def small_rows(t):
        return t.reshape(-1, D_MODEL)

    def tile_rows(t):
        return jnp.pad(t, ((0, -t.shape[0] % F32_SUBLANES), (0, 0)))

    g0, g1 = grads
    pieces = [
        jnp.concatenate([g0["norm1_g"], g1["norm1_g"]], axis=0),
        jnp.concatenate([g0["norm2_g"], g1["norm2_g"]], axis=0),
        dgf,
        small_rows(jnp.concatenate([g0["b_fg"], g1["b_fg"]], axis=1)),
        small_rows(jnp.concatenate([g0["gla_norm_g"], g1["gla_norm_g"],
                                    jnp.zeros((1, D_MODEL - 2 * HEAD_V), F32)], axis=1)),
        jnp.concatenate([g0["conv_wb"][3:4], g1["conv_wb"][3:4]], axis=0),
        jnp.concatenate([g0["conv_wb"][:3], g1["conv_wb"][:3]], axis=0),
        small_rows(jnp.stack([g0["w_fg2"][:GATE_RANK], g1["w_fg2"][:GATE_RANK]])),
    ]
    small = _all_reduce_small(jnp.concatenate([tile_rows(p) for p in pieces], axis=0))
    sg = dict(
        norm1_g=small[0:2], norm2_g=small[8:10], final_g=small[16],
        b_fg=small[24].reshape(DEPTH, KEY_WIDTH), gla_norm_g=small[32, :DEPTH * HEAD_V].reshape(DEPTH, HEAD_V),
        conv_b=small[40:42],
        conv_w=lax.dynamic_slice_in_dim(small[48:54].reshape(DEPTH, 3, CONV_CH), me * (CONV_CH // N_CHIPS),
                                        CONV_CH // N_CHIPS, axis=2),
        w_fg2=lax.dynamic_slice_in_dim(small[56:72].reshape(DEPTH, GATE_RANK, KEY_WIDTH),
                                       me * (KEY_WIDTH // N_CHIPS), KEY_WIDTH // N_CHIPS, axis=2),
    )

    small_params = dict(norm1_g=(norm1_g, m_norm1_g, v_norm1_g), w_fg2=(w_fg2, m_w_fg2, v_w_fg2),
                        b_fg=(b_fg, m_b_fg, v_b_fg), gla_norm_g=(gla_norm_g, m_gla_norm_g, v_gla_norm_g),
                        conv_w=(conv_w, m_conv_w, v_conv_w), conv_b=(conv_b, m_conv_b, v_conv_b),
                        norm2_g=(norm2_g, m_norm2_g, v_norm2_g), final_g=(final_g, m_final_g, v_final_g))
    order = ["norm1_g", "w_in", "w_fg2", "b_fg", "gla_norm_g", "w_oa", "conv_w", "conv_b", "w_ob", "w_o",
             "norm2_g", "w_ffn_gate", "w_ffn_up", "w_ffn_down", "final_g"]
    results = {}

    def update_large(n):
        w_, m_, v_ = views[n]
        outs = _adamw_layers(w_, (reduced0[n][0], reduced1[n][0]), (reduced0[n][1], reduced1[n][1]), on_core0,
                             m_, v_, name="adamw_" + n)
        back = from_view.get(n)
        results[n] = tuple(back(r) for r in outs) if back else outs
        return outs[1]

    for n, (w_, m_, v_) in small_params.items():
        results[n] = (sg[n],) + _adamw(w_, sg[n], m_, v_, name="adamw_" + n)
    updated = tuple(update_large(n) for n in rest_names)
    share_p = share(0, totals_of(0, swap_p, updated, proj_names, "p0"), "p0")
    reduced0.update(shared(0, share_p, share_p[3], proj_names, "p0"))
    update_large("w_in")
    return (loss, grad_x[None], *[results[n][0] for n in order], *[results[n][1] for n in order],
            *[results[n][2] for n in order], *[results[n][3] for n in order])
```

```python
import functools

import jax
import jax.numpy as jnp
from jax import lax
from jax.experimental import pallas as pl
from jax.experimental.pallas import tpu as pltpu

F32 = jnp.float32
BF16 = jnp.bfloat16

D_MODEL = 1024
DEPTH = 2
CHUNK = 64
GLA_HEADS = 4
KEY_WIDTH = 512
VAL_WIDTH = 1024
HEAD_K = 128
HEAD_V = 256
GATE_RANK = 16
GATE_TAU = 16.0
CONV_CH = 1024
FFN_HIDDEN = 2816
IN_WIDTH = 8208
NORM_EPS = 1e-6
Q_SCALE = HEAD_K ** -0.5
ADAM_LR = 0.001
ADAM_B1 = 0.9
ADAM_B2 = 0.999
ADAM_EPS = 1e-08
ADAM_WD = 0.01
ADAM_STEP = 10

N_CHIPS = 4
N_DEV = 8

LANE = 128
FZ_PAD = LANE
PROJ_W = 8192 + FZ_PAD
COL_Q, COL_K, COL_V, COL_R, COL_GBI, COL_GCI, COL_CX, COL_GA, COL_GB, COL_FZ = (
    0, 512, 1024, 2048, 3072, 4096, 5120, 6144, 7168, 8192)
FZ_ORIG = 3072

VMEM_LIMIT = 52 * 1024 * 1024
HALO = 16


def _cparams(sem=None):
    return pltpu.CompilerParams(dimension_semantics=sem, vmem_limit_bytes=VMEM_LIMIT)


def _sigmoid(x):
    return jax.nn.sigmoid(x)


def _log_sigmoid(x):
    return jnp.minimum(x, 0.0) - jnp.log1p(jnp.exp(-jnp.abs(x)))


_DOT_DIMS = {
    "nn": (((1,), (0,)), ((), ())),
    "nt": (((1,), (1,)), ((), ())),
    "tn": (((0,), (0,)), ((), ())),
}


def _matmul(a, b, *, mode, out_dtype, tm, tn, tk, name, add=None, dep=None):
    if mode == "nn":
        (m, k), n = a.shape, b.shape[1]
    elif mode == "nt":
        (m, k), n = a.shape, b.shape[0]
    else:
        (k, m), n = a.shape, b.shape[1]
    tm, tn, tk = min(tm, m), min(tn, n), min(tk, k)
    assert m % tm == 0 and n % tn == 0 and k % tk == 0, (name, m, n, k, tm, tn, tk)
    nk = k // tk
    has_add = add is not None
    has_dep = dep is not None

    def body(*refs):
        if has_dep:
            refs = refs[1:]
        if has_add:
            a_ref, b_ref, add_ref, o_ref = refs[:4]
            scratch = refs[4:]
        else:
            a_ref, b_ref, o_ref = refs[:3]
            add_ref = None
            scratch = refs[3:]
        part = lax.dot_general(a_ref[...], b_ref[...], _DOT_DIMS[mode], preferred_element_type=F32)

        def finish(acc):
            if add_ref is not None:
                acc = acc + add_ref[...].astype(F32)
            o_ref[...] = acc.astype(o_ref.dtype)

        if nk == 1:
            finish(part)
        else:
            acc_ref = scratch[0]
            kk = pl.program_id(2)

            @pl.when(kk == 0)
            def _():
                acc_ref[...] = part

            @pl.when(kk > 0)
            def _():
                acc_ref[...] += part

            @pl.when(kk == nk - 1)
            def _():
                finish(acc_ref[...])

    if mode == "nn":
        a_spec = pl.BlockSpec((tm, tk), lambda i, j, kk: (i, kk))
        b_spec = pl.BlockSpec((tk, tn), lambda i, j, kk: (kk, j))
    elif mode == "nt":
        a_spec = pl.BlockSpec((tm, tk), lambda i, j, kk: (i, kk))
        b_spec = pl.BlockSpec((tn, tk), lambda i, j, kk: (j, kk))
    else:
        a_spec = pl.BlockSpec((tk, tm), lambda i, j, kk: (kk, i))
        b_spec = pl.BlockSpec((tk, tn), lambda i, j, kk: (kk, j))
    o_spec = pl.BlockSpec((tm, tn), lambda i, j, kk: (i, j))
    in_specs = [a_spec, b_spec] + ([o_spec] if has_add else [])
    operands = (a, b) + ((add,) if has_add else ())
    if has_dep:
        in_specs = [pl.BlockSpec(dep.shape, lambda i, j, kk: (0, 0))] + in_specs
        operands = (dep,) + operands
    return pl.pallas_call(
        body,
        name=name,
        out_shape=jax.ShapeDtypeStruct((m, n), out_dtype),
        grid=(m // tm, n // tn, nk),
        in_specs=in_specs,
        out_specs=o_spec,
        scratch_shapes=[pltpu.VMEM((tm, tn), F32)] if nk > 1 else [],
        compiler_params=_cparams(("parallel", "parallel", "arbitrary")),
    )(*operands)


def _matmul_norm_bwd(a, b, x, g, dres, *, tm, tk, name, add=None, dep=None):
    (m, k), d = a.shape, b.shape[1]
    tm, tk = min(tm, m), min(tk, k)
    assert m % tm == 0 and k % tk == 0 and x.shape == (m, d), (name, m, k, d, tm, tk)
    nk = k // tk
    has_add = add is not None
    has_dep = dep is not None

    def body(*refs):
        refs = refs[has_dep:]
        a_ref, b_ref = refs[:2]
        add_ref = refs[2] if has_add else None
        x_ref, g_ref, dres_ref, dx_ref, dxb_ref, dg_ref = refs[2 + has_add:8 + has_add]
        scratch = refs[8 + has_add:]
        i = pl.program_id(0)
        part = jnp.dot(a_ref[...], b_ref[...], preferred_element_type=F32)

        def finish(dh):
            if add_ref is not None:
                dh = dh + add_ref[...].astype(F32)
            xv = x_ref[...]
            rs = lax.rsqrt(jnp.mean(xv * xv, axis=-1, keepdims=True) + NORM_EPS)
            nrm = xv * rs
            dn = dh * g_ref[...]
            dx = dres_ref[...] + rs * (dn - nrm * jnp.mean(dn * nrm, axis=-1, keepdims=True))
            dx_ref[...] = dx
            dxb_ref[...] = dx.astype(BF16)
            dg_part = jnp.sum(dh * nrm, axis=0, keepdims=True)

            @pl.when(i == 0)
            def _():
                dg_ref[...] = dg_part

            @pl.when(i > 0)
            def _():
                dg_ref[...] += dg_part

        if nk == 1:
            finish(part)
        else:
            acc_ref = scratch[0]
            kk = pl.program_id(1)

            @pl.when(kk == 0)
            def _():
                acc_ref[...] = part

            @pl.when(kk > 0)
            def _():
                acc_ref[...] += part

            @pl.when(kk == nk - 1)
            def _():
                finish(acc_ref[...])

    row = pl.BlockSpec((tm, d), lambda i, kk: (i, 0))
    vec = pl.BlockSpec((1, d), lambda i, kk: (0, 0))
    in_specs = [pl.BlockSpec((tm, tk), lambda i, kk: (i, kk)), pl.BlockSpec((tk, d), lambda i, kk: (kk, 0))]
    operands = [a, b]
    if has_add:
        in_specs.append(row)
        operands.append(add)
    in_specs += [row, vec, row]
    operands += [x, g.reshape(1, d), dres]
    if has_dep:
        in_specs = [pl.BlockSpec(dep.shape, lambda i, kk: (0, 0))] + in_specs
        operands = [dep] + operands
    return pl.pallas_call(
        body, name=name,
        out_shape=(jax.ShapeDtypeStruct((m, d), F32), jax.ShapeDtypeStruct((m, d), BF16),
                   jax.ShapeDtypeStruct((1, d), F32)),
        grid=(m // tm, nk),
        in_specs=in_specs,
        out_specs=(row, row, vec),
        scratch_shapes=[pltpu.VMEM((tm, d), F32)] if nk > 1 else [],
        compiler_params=_cparams(("arbitrary", "arbitrary")),
    )(*operands)


def _rms_fwd(x, g, *, name):
    s, d = x.shape
    tr = min(512, s)

    def body(x_ref, g_ref, h_ref):
        xv = x_ref[...]
        rs = lax.rsqrt(jnp.mean(xv * xv, axis=-1, keepdims=True) + NORM_EPS)
        h_ref[...] = (xv * rs * g_ref[...]).astype(BF16)

    return pl.pallas_call(
        body, name=name,
        out_shape=jax.ShapeDtypeStruct((s, d), BF16),
        grid=(s // tr,),
        in_specs=[pl.BlockSpec((tr, d), lambda i: (i, 0)), pl.BlockSpec((1, d), lambda i: (0, 0))],
        out_specs=pl.BlockSpec((tr, d), lambda i: (i, 0)),
        compiler_params=_cparams(("parallel",)),
    )(x, g.reshape(1, d))


def _loss_head(x, g, target):
    s, d = x.shape
    tr = min(512, s)

    def body(x_ref, g_ref, t_ref, loss_ref, dx_ref, dxb_ref, dg_ref):
        i = pl.program_id(0)
        xv = x_ref[...]
        rs = lax.rsqrt(jnp.mean(xv * xv, axis=-1, keepdims=True) + NORM_EPS)
        n = xv * rs
        gv = g_ref[...]
        err = n * gv - t_ref[...]
        row_loss = jnp.mean(err * err, axis=-1, keepdims=True)
        loss_part = 0.5 * jnp.sum(row_loss, axis=0, keepdims=True)
        dy = err * (1.0 / d)
        dn = dy * gv
        dx = rs * (dn - n * jnp.mean(dn * n, axis=-1, keepdims=True))
        dx_ref[...] = dx
        dxb_ref[...] = dx.astype(BF16)
        dg_part = jnp.sum(dy * n, axis=0, keepdims=True)

        @pl.when(i == 0)
        def _():
            loss_ref[...] = loss_part
            dg_ref[...] = dg_part

        @pl.when(i > 0)
        def _():
            loss_ref[...] += loss_part
            dg_ref[...] += dg_part

    row = pl.BlockSpec((tr, d), lambda i: (i, 0))
    vec = pl.BlockSpec((1, d), lambda i: (0, 0))
    one = pl.BlockSpec((1, 1), lambda i: (0, 0))
    return pl.pallas_call(
        body, name="loss_head",
        out_shape=(jax.ShapeDtypeStruct((1, 1), F32), jax.ShapeDtypeStruct((s, d), F32),
                   jax.ShapeDtypeStruct((s, d), BF16), jax.ShapeDtypeStruct((1, d), F32)),
        grid=(s // tr,),
        in_specs=[row, vec, row],
        out_specs=(one, row, row, vec),
        compiler_params=_cparams(("arbitrary",)),
    )(x, g.reshape(1, d), target)


def _conv_taps(u_prev, u, w_ref, rows):
    ext = jnp.concatenate([u_prev, u], axis=0)
    u1 = pltpu.roll(ext, 1, 0)[HALO:HALO + rows]
    u2 = pltpu.roll(ext, 2, 0)[HALO:HALO + rows]
    conv = w_ref[0:1, :] * u2 + w_ref[1:2, :] * u1 + w_ref[2:3, :] * u + w_ref[3:4, :]
    return conv, u1, u2


def _conv_fwd(proj, conv_wb, *, s):
    tr = min(512, s)
    c = CONV_CH
    hb = tr // HALO

    def body(gbi_ref, gci_ref, cx_ref, gci_h_ref, cx_h_ref, w_ref, o_ref):
        i = pl.program_id(0)
        u = gci_ref[...].astype(F32) * cx_ref[...].astype(F32)
        u_prev = gci_h_ref[...].astype(F32) * cx_h_ref[...].astype(F32)
        u_prev = jnp.where(i == 0, 0.0, u_prev)
        conv, _, _ = _conv_taps(u_prev, u, w_ref, tr)
        o_ref[...] = (gbi_ref[...].astype(F32) * conv).astype(BF16)

    def seg(col):
        return pl.BlockSpec((tr, c), lambda i: (i, col // c))

    def halo(col):
        return pl.BlockSpec((HALO, c), lambda i: (jnp.maximum(i * hb - 1, 0), col // c))

    return pl.pallas_call(
        body, name="conv_fwd",
        out_shape=jax.ShapeDtypeStruct((s, c), BF16),
        grid=(s // tr,),
        in_specs=[seg(COL_GBI), seg(COL_GCI), seg(COL_CX), halo(COL_GCI), halo(COL_CX),
                  pl.BlockSpec((8, c), lambda i: (0, 0))],
        out_specs=pl.BlockSpec((tr, c), lambda i: (i, 0)),
        compiler_params=_cparams(("parallel",)),
    )(proj, proj, proj, proj, proj, conv_wb)


def _conv_bwd(proj, dcb, conv_wb, dproj, *, s):
    tr = min(512, s)
    c = CONV_CH
    hb = tr // HALO
    nb = s // tr

    def body(gbi_ref, gci_ref, cx_ref, dcb_ref, gci_h_ref, cx_h_ref, gbi_n_ref, dcb_n_ref, w_ref,
             dproj_in_ref, dp_ref, dw_ref):
        i = pl.program_id(0)
        gbi = gbi_ref[...].astype(F32)
        gci = gci_ref[...].astype(F32)
        cx = cx_ref[...].astype(F32)
        dcb_v = dcb_ref[...].astype(F32)
        u = gci * cx
        u_prev = jnp.where(i == 0, 0.0, gci_h_ref[...].astype(F32) * cx_h_ref[...].astype(F32))
        conv, u1, u2 = _conv_taps(u_prev, u, w_ref, tr)
        dconv = dcb_v * gbi
        dconv_next = jnp.where(i == nb - 1, 0.0, dcb_n_ref[...].astype(F32) * gbi_n_ref[...].astype(F32))
        ext = jnp.concatenate([dconv, dconv_next], axis=0)
        n_ext = tr + HALO
        d1 = pltpu.roll(ext, n_ext - 1, 0)[0:tr]
        d2 = pltpu.roll(ext, n_ext - 2, 0)[0:tr]
        du = w_ref[2:3, :] * dconv + w_ref[1:2, :] * d1 + w_ref[0:1, :] * d2
        dp_ref[:, 0:c] = (dcb_v * conv).astype(BF16)
        dp_ref[:, c:2 * c] = (du * cx).astype(BF16)
        dp_ref[:, 2 * c:3 * c] = (du * gci).astype(BF16)
        part = jnp.concatenate([
            jnp.sum(dconv * u2, axis=0, keepdims=True),
            jnp.sum(dconv * u1, axis=0, keepdims=True),
            jnp.sum(dconv * u, axis=0, keepdims=True),
            jnp.sum(dconv, axis=0, keepdims=True),
            jnp.zeros((4, c), F32)], axis=0)

        @pl.when(i == 0)
        def _():
            dw_ref[...] = part

        @pl.when(i > 0)
        def _():
            dw_ref[...] += part

    def seg(col):
        return pl.BlockSpec((tr, c), lambda i: (i, col // c))

    def halo_prev(col):
        return pl.BlockSpec((HALO, c), lambda i: (jnp.maximum(i * hb - 1, 0), col // c))

    def halo_next(col):
        return pl.BlockSpec((HALO, c), lambda i: (jnp.minimum((i + 1) * hb, nb * hb - 1), col // c))

    return pl.pallas_call(
        body, name="conv_bwd",
        out_shape=(jax.ShapeDtypeStruct((s, PROJ_W), BF16), jax.ShapeDtypeStruct((8, c), F32)),
        grid=(nb,),
        in_specs=[seg(COL_GBI), seg(COL_GCI), seg(COL_CX), pl.BlockSpec((tr, c), lambda i: (i, 0)),
                  halo_prev(COL_GCI), halo_prev(COL_CX), halo_next(COL_GBI),
                  pl.BlockSpec((HALO, c), lambda i: (jnp.minimum((i + 1) * hb, nb * hb - 1), 0)),
                  pl.BlockSpec((8, c), lambda i: (0, 0)), ANY],
        out_specs=(pl.BlockSpec((tr, 3 * c), lambda i: (i, COL_GBI // (3 * c))),
                   pl.BlockSpec((8, c), lambda i: (0, 0))),
        input_output_aliases={9: 0},
        compiler_params=_cparams(("arbitrary",)),
    )(proj, proj, proj, dcb, proj, proj, proj, dcb, conv_wb, dproj)


def _token_spec(dep):
    return ([], []) if dep is None else ([dep], [pl.BlockSpec(dep.shape, lambda i: (0, 0))])


def _mixer_out_fwd(oa, cb_in, proj, x0, w_oa, w_ob, w_o, *, s, dep=None):
    tr = min(512, s)
    d = D_MODEL
    n_dep = dep is not None

    def body(*refs):
        oa_ref, cb_ref, ga_ref, gb_ref, x0_ref, woa_ref, wob_ref, wo_ref, ya_ref, yb_ref, mix_ref, x1_ref = (
            refs[n_dep:])
        ya = jnp.dot(oa_ref[...], woa_ref[...], preferred_element_type=F32).astype(BF16)
        yb = jnp.dot(cb_ref[...], wob_ref[...], preferred_element_type=F32).astype(BF16)
        ya_ref[...] = ya
        yb_ref[...] = yb
        sa = _sigmoid(ga_ref[...].astype(F32))
        sb = _sigmoid(gb_ref[...].astype(F32))
        mix = (sa * ya.astype(F32) + sb * yb.astype(F32)).astype(BF16)
        mix_ref[...] = mix
        x1_ref[...] = x0_ref[...] + jnp.dot(mix, wo_ref[...], preferred_element_type=F32)

    row = pl.BlockSpec((tr, d), lambda i: (i, 0))
    full = pl.BlockSpec((d, d), lambda i: (0, 0))
    tok, tok_specs = _token_spec(dep)
    return pl.pallas_call(
        body, name="mixer_out_fwd",
        out_shape=(jax.ShapeDtypeStruct((s, d), BF16), jax.ShapeDtypeStruct((s, d), BF16),
                   jax.ShapeDtypeStruct((s, d), BF16), jax.ShapeDtypeStruct((s, d), F32)),
        grid=(s // tr,),
        in_specs=tok_specs + [row, row, pl.BlockSpec((tr, d), lambda i: (i, COL_GA // d)),
                              pl.BlockSpec((tr, d), lambda i: (i, COL_GB // d)), row, full, full, full],
        out_specs=(row, row, row, row),
        compiler_params=_cparams(("parallel",)),
    )(*tok, oa, cb_in, proj, proj, x0, w_oa, w_ob, w_o)


def _mixer_out_bwd(dx1b, proj, ya, yb, w_oa, w_ob, w_o, *, s, dep=None):
    tr = min(512, s)
    d = D_MODEL
    n_dep = dep is not None

    def body(*refs):
        (dx_ref, ga_ref, gb_ref, ya_ref, yb_ref, woa_ref, wob_ref, wo_ref,
         dya_ref, dyb_ref, dcb_ref, doa_ref, dg_ref) = refs[n_dep:]
        dm = lax.dot_general(dx_ref[...], wo_ref[...], _DOT_DIMS["nt"], preferred_element_type=F32)
        dm = dm.astype(BF16).astype(F32)
        sa = _sigmoid(ga_ref[...].astype(F32))
        sb = _sigmoid(gb_ref[...].astype(F32))
        dya = (dm * sa).astype(BF16)
        dyb = (dm * sb).astype(BF16)
        dya_ref[...] = dya
        dyb_ref[...] = dyb
        dg_ref[:, 0:d] = (dm * ya_ref[...].astype(F32) * sa * (1.0 - sa)).astype(BF16)
        dg_ref[:, d:2 * d] = (dm * yb_ref[...].astype(F32) * sb * (1.0 - sb)).astype(BF16)
        dcb_ref[...] = lax.dot_general(dyb, wob_ref[...], _DOT_DIMS["nt"],
                                       preferred_element_type=F32).astype(BF16)
        doa_ref[...] = lax.dot_general(dya, woa_ref[...], _DOT_DIMS["nt"],
                                       preferred_element_type=F32).astype(BF16)

    row = pl.BlockSpec((tr, d), lambda i: (i, 0))
    full = pl.BlockSpec((d, d), lambda i: (0, 0))
    tok, tok_specs = _token_spec(dep)
    return pl.pallas_call(
        body, name="mixer_out_bwd",
        out_shape=tuple(jax.ShapeDtypeStruct((s, d), BF16) for _ in range(4))
        + (jax.ShapeDtypeStruct((s, PROJ_W), BF16),),
        grid=(s // tr,),
        in_specs=tok_specs + [row, pl.BlockSpec((tr, d), lambda i: (i, COL_GA // d)),
                              pl.BlockSpec((tr, d), lambda i: (i, COL_GB // d)), row, row, full, full, full],
        out_specs=(row, row, row, row, pl.BlockSpec((tr, 2 * d), lambda i: (i, COL_GA // (2 * d)))),
        compiler_params=_cparams(("parallel",)),
    )(*tok, dx1b, proj, proj, ya, yb, w_oa, w_ob, w_o)


def _ffn_out_bwd(dx2b, w_down, gt, up, *, s):
    tr = min(256, s)
    d, f = D_MODEL, FFN_HIDDEN

    def body(dx_ref, wd_ref, gt_ref, up_ref, dgt_ref, dup_ref):
        dh = lax.dot_general(dx_ref[...], wd_ref[...], _DOT_DIMS["nt"], preferred_element_type=F32)
        gv = gt_ref[...].astype(F32)
        uv = up_ref[...].astype(F32)
        sg = _sigmoid(gv)
        dgt_ref[...] = (dh * uv * sg * (1.0 + gv * (1.0 - sg))).astype(BF16)
        dup_ref[...] = (dh * gv * sg).astype(BF16)

    row = pl.BlockSpec((tr, f), lambda i: (i, 0))
    return pl.pallas_call(
        body, name="ffn_out_bwd",
        out_shape=(jax.ShapeDtypeStruct((s, f), BF16), jax.ShapeDtypeStruct((s, f), BF16)),
        grid=(s // tr,),
        in_specs=[pl.BlockSpec((tr, d), lambda i: (i, 0)), pl.BlockSpec((f, d), lambda i: (0, 0)), row, row],
        out_specs=(row, row),
        compiler_params=_cparams(("parallel",)),
    )(dx2b, w_down, gt, up)


def _tri(strict):
    r = lax.broadcasted_iota(jnp.int32, (CHUNK, CHUNK), 0)
    c = lax.broadcasted_iota(jnp.int32, (CHUNK, CHUNK), 1)
    return jnp.where((c < r) if strict else (c <= r), 1.0, 0.0).astype(F32)


def _gla_gate_terms(la_c, tri):
    cum = jnp.dot(tri, la_c, precision=lax.Precision.HIGHEST, preferred_element_type=F32)
    cend = cum[CHUNK - 1:CHUNK, :]
    return jnp.exp(cend - cum), jnp.exp(cend)


def _hk(h):
    return slice(h * HEAD_K, (h + 1) * HEAD_K)


def _hv(h):
    return slice(h * HEAD_V, (h + 1) * HEAD_V)


def _gla_fwd(proj, wfg2p, bfg, gn, *, s):
    tr = min(512, s)
    nb = s // tr
    nc = tr // CHUNK

    def body(q_ref, k_ref, v_ref, r_ref, fz_ref, w_ref, b_ref, gn_ref, oa_ref, st_ref,
             state, la_scr, o_scr):
        i = pl.program_id(0)

        @pl.when(i == 0)
        def _():
            state[...] = jnp.zeros_like(state)

        fg = jnp.dot(fz_ref[...], w_ref[...], preferred_element_type=F32) + b_ref[...]
        la_scr[...] = _log_sigmoid(fg) * (1.0 / GATE_TAU)
        tri = _tri(False)

        def chunk(ci, carry):
            r0 = pl.multiple_of(ci * CHUNK, CHUNK)
            rows = pl.ds(r0, CHUNK)
            e, gam = _gla_gate_terms(la_scr[rows, :], tri)
            kd = (k_ref[rows, :].astype(F32) * e).astype(BF16)
            qs = (q_ref[rows, :].astype(F32) * Q_SCALE).astype(BF16)
            v_c = v_ref[rows, :]
            for h in range(GLA_HEADS):
                upd = lax.dot_general(v_c[:, _hv(h)], kd[:, _hk(h)], _DOT_DIMS["tn"],
                                      preferred_element_type=F32)
                st_h = state[:, _hk(h)] * gam[:, _hk(h)] + upd
                state[:, _hk(h)] = st_h
                o_scr[rows, _hv(h)] = lax.dot_general(qs[:, _hk(h)], st_h.astype(BF16), _DOT_DIMS["nt"],
                                                       preferred_element_type=F32)
            st_ref[ci] = state[...].astype(BF16)
            return carry

        lax.fori_loop(0, nc, chunk, 0, unroll=True)
        for h in range(GLA_HEADS):
            o = o_scr[:, _hv(h)]
            rs = lax.rsqrt(jnp.mean(o * o, axis=-1, keepdims=True) + NORM_EPS)
            rv = r_ref[:, _hv(h)].astype(F32)
            oa_ref[:, _hv(h)] = ((o * rs * gn_ref[...]).astype(F32) * (rv * _sigmoid(rv))).astype(BF16)

    return pl.pallas_call(
        body, name="gla_fwd",
        out_shape=(jax.ShapeDtypeStruct((s, VAL_WIDTH), BF16),
                   jax.ShapeDtypeStruct((s // CHUNK, HEAD_V, KEY_WIDTH), BF16)),
        grid=(nb,),
        in_specs=[pl.BlockSpec((tr, KEY_WIDTH), lambda i: (i, COL_Q // KEY_WIDTH)),
                  pl.BlockSpec((tr, KEY_WIDTH), lambda i: (i, COL_K // KEY_WIDTH)),
                  pl.BlockSpec((tr, VAL_WIDTH), lambda i: (i, COL_V // VAL_WIDTH)),
                  pl.BlockSpec((tr, VAL_WIDTH), lambda i: (i, COL_R // VAL_WIDTH)),
                  pl.BlockSpec((tr, FZ_PAD), lambda i: (i, COL_FZ // FZ_PAD)),
                  pl.BlockSpec((FZ_PAD, KEY_WIDTH), lambda i: (0, 0)),
                  pl.BlockSpec((1, KEY_WIDTH), lambda i: (0, 0)),
                  pl.BlockSpec((1, HEAD_V), lambda i: (0, 0))],
        out_specs=(pl.BlockSpec((tr, VAL_WIDTH), lambda i: (i, 0)),
                   pl.BlockSpec((nc, HEAD_V, KEY_WIDTH), lambda i: (i, 0, 0))),
        scratch_shapes=[pltpu.VMEM((HEAD_V, KEY_WIDTH), F32), pltpu.VMEM((tr, KEY_WIDTH), F32),
                        pltpu.VMEM((tr, VAL_WIDTH), F32)],
        compiler_params=_cparams(("arbitrary",)),
    )(proj, proj, proj, proj, proj, wfg2p, bfg, gn)


def _gla_bwd(proj, doa, states, wfg2p, bfg, gn, dproj, *, s):
    tr = min(512, s)
    nb = s // tr
    nc = tr // CHUNK

    def body(q_ref, k_ref, v_ref, r_ref, fz_ref, doa_ref, st_ref, stp_ref, w_ref, b_ref, gn_ref, dproj_in_ref,
             dp_ref, dfz_ref, dgn_ref, dw_ref, db_ref,
             carry, fg_scr, la_scr, dla_scr, o_scr, do_scr):
        i = pl.program_id(0)

        @pl.when(i == 0)
        def _():
            carry[...] = jnp.zeros_like(carry)
            dgn_ref[...] = jnp.zeros_like(dgn_ref)
            dw_ref[...] = jnp.zeros_like(dw_ref)
            db_ref[...] = jnp.zeros_like(db_ref)

        fz = fz_ref[...]
        fg = jnp.dot(fz, w_ref[...], preferred_element_type=F32) + b_ref[...]
        fg_scr[...] = fg
        la_scr[...] = _log_sigmoid(fg) * (1.0 / GATE_TAU)

        def fwd_chunk(ci, c0):
            r0 = pl.multiple_of(ci * CHUNK, CHUNK)
            rows = pl.ds(r0, CHUNK)
            qs = (q_ref[rows, :].astype(F32) * Q_SCALE).astype(BF16)
            st = st_ref[ci].astype(BF16)
            for h in range(GLA_HEADS):
                o_scr[rows, _hv(h)] = lax.dot_general(qs[:, _hk(h)], st[:, _hk(h)], _DOT_DIMS["nt"],
                                                       preferred_element_type=F32)
            return c0

        lax.fori_loop(0, nc, fwd_chunk, 0, unroll=True)

        gnv = gn_ref[...]
        dgn_part = jnp.zeros((1, HEAD_V), F32)
        for h in range(GLA_HEADS):
            o = o_scr[:, _hv(h)]
            rs = lax.rsqrt(jnp.mean(o * o, axis=-1, keepdims=True) + NORM_EPS)
            nrm = o * rs
            rv = r_ref[:, _hv(h)].astype(F32)
            sg = _sigmoid(rv)
            doa_h = doa_ref[:, _hv(h)].astype(F32)
            don = doa_h * (rv * sg)
            dp_ref[:, COL_R + h * HEAD_V:COL_R + (h + 1) * HEAD_V] = (
                doa_h * (nrm * gnv) * (sg * (1.0 + rv * (1.0 - sg)))).astype(BF16)
            dgn_part = dgn_part + jnp.sum(don * nrm, axis=0, keepdims=True)
            dn = don * gnv
            do_scr[:, _hv(h)] = (rs * (dn - nrm * jnp.mean(dn * nrm, axis=-1, keepdims=True))).astype(BF16)
        dgn_ref[...] += dgn_part

        tri = _tri(False)
        tri_s = _tri(True)
        first_block = i == nb - 1

        def bwd_chunk(cc, c0):
            ci = nc - 1 - cc
            r0 = pl.multiple_of(ci * CHUNK, CHUNK)
            rows = pl.ds(r0, CHUNK)
            e, gam = _gla_gate_terms(la_scr[rows, :], tri)
            k_c = k_ref[rows, :].astype(F32)
            kd = k_c * e
            kd_b = kd.astype(BF16)
            qs = (q_ref[rows, :].astype(F32) * Q_SCALE).astype(BF16)
            v_c = v_ref[rows, :]
            do_c = do_scr[rows, :]
            st_b = st_ref[ci]
            st_prev_in = st_ref[jnp.maximum(ci - 1, 0)].astype(F32)
            st_prev_edge = jnp.where(first_block, 0.0, stp_ref[0].astype(F32))
            st_prev = jnp.where(ci > 0, st_prev_in, st_prev_edge)
            dkd_parts = []
            dgam_parts = []
            for h in range(GLA_HEADS):
                dst = lax.dot_general(do_c[:, _hv(h)], qs[:, _hk(h)], _DOT_DIMS["tn"],
                                      preferred_element_type=F32) + carry[:, _hk(h)]
                dst_b = dst.astype(BF16)
                dqs = jnp.dot(do_c[:, _hv(h)], st_b[:, _hk(h)], preferred_element_type=F32)
                dp_ref[rows, COL_Q + h * HEAD_K:COL_Q + (h + 1) * HEAD_K] = (dqs * Q_SCALE).astype(BF16)
                dkd_parts.append(jnp.dot(v_c[:, _hv(h)], dst_b, preferred_element_type=F32))
                dp_ref[rows, COL_V + h * HEAD_V:COL_V + (h + 1) * HEAD_V] = lax.dot_general(
                    kd_b[:, _hk(h)], dst_b, _DOT_DIMS["nt"], preferred_element_type=F32).astype(BF16)
                dgam_parts.append(jnp.sum(dst * st_prev[:, _hk(h)], axis=0, keepdims=True))
                carry[:, _hk(h)] = dst * gam[:, _hk(h)]
            dkd = jnp.concatenate(dkd_parts, axis=1)
            dgam = jnp.concatenate(dgam_parts, axis=1)
            dp_ref[rows, COL_K:COL_K + KEY_WIDTH] = (dkd * e).astype(BF16)
            dz = dkd * kd
            dla_scr[rows, :] = gam * dgam + jnp.dot(tri_s, dz, precision=lax.Precision.HIGHEST,
                                                    preferred_element_type=F32)
            return c0

        lax.fori_loop(0, nc, bwd_chunk, 0, unroll=True)

        dfg = dla_scr[...] * (1.0 / GATE_TAU) * _sigmoid(-fg_scr[...])
        dfg_b = dfg.astype(BF16)
        dfz_ref[...] = lax.dot_general(dfg_b, w_ref[...], _DOT_DIMS["nt"],
                                       preferred_element_type=F32).astype(BF16)
        dw_ref[...] += lax.dot_general(fz, dfg_b, _DOT_DIMS["tn"], preferred_element_type=F32)
        db_ref[...] += jnp.sum(dfg, axis=0, keepdims=True)

    def rev(i):
        return nb - 1 - i

    return pl.pallas_call(
        body, name="gla_bwd",
        out_shape=(jax.ShapeDtypeStruct((s, PROJ_W), BF16),
                   jax.ShapeDtypeStruct((s, FZ_PAD), BF16),
                   jax.ShapeDtypeStruct((1, HEAD_V), F32),
                   jax.ShapeDtypeStruct((FZ_PAD, KEY_WIDTH), F32),
                   jax.ShapeDtypeStruct((1, KEY_WIDTH), F32)),
        grid=(nb,),
        in_specs=[pl.BlockSpec((tr, KEY_WIDTH), lambda i: (rev(i), COL_Q // KEY_WIDTH)),
                  pl.BlockSpec((tr, KEY_WIDTH), lambda i: (rev(i), COL_K // KEY_WIDTH)),
                  pl.BlockSpec((tr, VAL_WIDTH), lambda i: (rev(i), COL_V // VAL_WIDTH)),
                  pl.BlockSpec((tr, VAL_WIDTH), lambda i: (rev(i), COL_R // VAL_WIDTH)),
                  pl.BlockSpec((tr, FZ_PAD), lambda i: (rev(i), COL_FZ // FZ_PAD)),
                  pl.BlockSpec((tr, VAL_WIDTH), lambda i: (rev(i), 0)),
                  pl.BlockSpec((nc, HEAD_V, KEY_WIDTH), lambda i: (rev(i), 0, 0)),
                  pl.BlockSpec((1, HEAD_V, KEY_WIDTH), lambda i: (jnp.maximum(rev(i) * nc - 1, 0), 0, 0)),
                  pl.BlockSpec((FZ_PAD, KEY_WIDTH), lambda i: (0, 0)),
                  pl.BlockSpec((1, KEY_WIDTH), lambda i: (0, 0)),
                  pl.BlockSpec((1, HEAD_V), lambda i: (0, 0)), ANY],
        out_specs=(pl.BlockSpec((tr, 3 * VAL_WIDTH), lambda i: (rev(i), 0)),
                   pl.BlockSpec((tr, FZ_PAD), lambda i: (rev(i), 0)),
                   pl.BlockSpec((1, HEAD_V), lambda i: (0, 0)),
                   pl.BlockSpec((FZ_PAD, KEY_WIDTH), lambda i: (0, 0)),
                   pl.BlockSpec((1, KEY_WIDTH), lambda i: (0, 0))),
        scratch_shapes=[pltpu.VMEM((HEAD_V, KEY_WIDTH), F32),
                        pltpu.VMEM((tr, KEY_WIDTH), F32),
                        pltpu.VMEM((tr, KEY_WIDTH), F32),
                        pltpu.VMEM((tr, KEY_WIDTH), F32),
                        pltpu.VMEM((tr, VAL_WIDTH), F32),
                        pltpu.VMEM((tr, VAL_WIDTH), BF16)],
        input_output_aliases={11: 0},
        compiler_params=_cparams(("arbitrary",)),
    )(proj, proj, proj, proj, proj, doa, states, states, wfg2p, bfg, gn, dproj)


def _put_fz(dproj, dfz, *, s):
    tr = min(512, s)

    def body(dfz_ref, dproj_in_ref, o_ref):
        o_ref[...] = dfz_ref[...]

    return pl.pallas_call(
        body, name="put_dfz",
        out_shape=jax.ShapeDtypeStruct((s, PROJ_W), BF16),
        grid=(s // tr,),
        in_specs=[pl.BlockSpec((tr, FZ_PAD), lambda i: (i, 0)), ANY],
        out_specs=pl.BlockSpec((tr, FZ_PAD), lambda i: (i, COL_FZ // FZ_PAD)),
        input_output_aliases={1: 0},
        compiler_params=_cparams(("parallel",)),
    )(dfz, dproj)


_ADAM_C1 = 1.0 / (1.0 - ADAM_B1 ** ADAM_STEP)
_ADAM_C2 = 1.0 / (1.0 - ADAM_B2 ** ADAM_STEP)


def _adamw_math(wv, gv, mv, vv):
    nm = ADAM_B1 * mv + (1.0 - ADAM_B1) * gv
    nv = ADAM_B2 * vv + (1.0 - ADAM_B2) * (gv * gv)
    delta = -ADAM_LR * ((nm * _ADAM_C1) / (jnp.sqrt(nv * _ADAM_C2) + ADAM_EPS) + ADAM_WD * wv)
    return delta, nm, nv


def _adamw(w, g, m, v, *, name):
    shape = w.shape
    if w.ndim == 1:
        w, g, m, v = (t.reshape(1, 1, -1) for t in (w, g, m, v))
    elif w.ndim == 2:
        w, g, m, v = (t.reshape((1,) + t.shape) for t in (w, g, m, v))
    l, a, b = w.shape

    def body(w_ref, g_ref, m_ref, v_ref, d_ref, nm_ref, nv_ref):
        d_ref[...], nm_ref[...], nv_ref[...] = _adamw_math(w_ref[...], g_ref[...], m_ref[...], v_ref[...])

    blk = pl.BlockSpec((1, a, b), lambda li: (li, 0, 0))
    outs = pl.pallas_call(
        body, name=name,
        out_shape=tuple(jax.ShapeDtypeStruct((l, a, b), F32) for _ in range(3)),
        grid=(l,),
        in_specs=[blk, blk, blk, blk],
        out_specs=(blk, blk, blk),
        compiler_params=_cparams(("parallel",)),
    )(w, g, m, v)
    return tuple(o.reshape(shape) for o in outs)


def _adamw_layers(w, reduced, received, on_core0, m, v, *, name):
    l, a, b = w.shape
    assert l == DEPTH == 2
    ta = _row_tile(a, F32_SUBLANES, MAX_ADAMW_ROWS)

    def body(flag_ref, w_ref, r0_ref, o0_ref, r1_ref, o1_ref, m_ref, v_ref, g_ref, d_ref, nm_ref, nv_ref):
        core0 = flag_ref[...] > 0.5
        g0 = jnp.where(core0, r0_ref[...], o0_ref[...])
        g1 = jnp.where(core0, o1_ref[...], r1_ref[...])
        gv = jnp.where(pl.program_id(0) == 0, g0, g1)
        g_ref[0] = gv
        d_ref[0], nm_ref[0], nv_ref[0] = _adamw_math(w_ref[0], gv, m_ref[0], v_ref[0])

    blk = pl.BlockSpec((1, ta, b), lambda li, ai: (li, ai, 0))

    def grad_block(layer):
        return pl.BlockSpec((ta, b), lambda li, ai: (jnp.where(li == layer, ai, 0), 0))

    g0blk, g1blk = grad_block(0), grad_block(1)
    return pl.pallas_call(
        body, name=name,
        out_shape=tuple(jax.ShapeDtypeStruct((l, a, b), F32) for _ in range(4)),
        grid=(l, a // ta),
        in_specs=[pl.BlockSpec((1, 1), lambda li, ai: (0, 0)), blk, g0blk, g0blk, g1blk, g1blk, blk, blk],
        out_specs=(blk, blk, blk, blk),
        compiler_params=_cparams(("parallel", "parallel")),
    )(on_core0, w, reduced[0], received[0], reduced[1], received[1], m, v)


MESH_ID = pl.DeviceIdType.MESH
ANY = pl.BlockSpec(memory_space=pl.ANY)


def _position():
    x, y, c = lax.axis_index("x"), lax.axis_index("y"), lax.axis_index("c")
    chips = [(1 - x, y), (x, 1 - y), (1 - x, 1 - y)]
    return x, y, c, chips


def _chip_index(xy):
    return 2 * xy[0] + xy[1]


HBM = pl.BlockSpec(memory_space=pltpu.HBM)
SEM = pl.BlockSpec(memory_space=pltpu.SEMAPHORE)
DATAFLOW_EFFECT = pltpu.SideEffectType.DATAFLOW_SIDE_EFFECTING
TOKEN_SHAPE = (8, LANE)


def _landing(shape, dtype):
    return pltpu.with_memory_space_constraint(lax.empty(shape, dtype), pltpu.HBM)


def _split_start(bufs, sem_shape, issue, *, name, after=None):
    n = len(bufs)
    n_in = n + (after is not None)

    def body(*refs):
        issue(refs[:n], refs[n_in], refs[n_in + 1])
        token = refs[-1]
        token[...] = jnp.zeros_like(token)

    operands = [pltpu.with_memory_space_constraint(t, pltpu.HBM) for t in bufs]
    outs = pl.pallas_call(
        body, name=name,
        out_shape=(pltpu.SemaphoreType.DMA(sem_shape), pltpu.SemaphoreType.DMA(sem_shape),
                   *[pltpu.HBM(t.shape, t.dtype) for t in bufs], jax.ShapeDtypeStruct(TOKEN_SHAPE, F32)),
        in_specs=[HBM] * n + [ANY] * (after is not None),
        out_specs=(SEM, SEM, *[HBM] * n, pl.BlockSpec(memory_space=pltpu.VMEM)),
        input_output_aliases={i: 2 + i for i in range(n)},
        compiler_params=pltpu.CompilerParams(has_side_effects=DATAFLOW_EFFECT),
    )(*operands, *([after] if after is not None else []))
    return outs[0], outs[1], list(outs[2:2 + n]), outs[-1]


def _split_wait(started, after, settle, *, name):
    send_sems, recv_sems, bufs, _ = started
    n = len(bufs)
    afters = tuple(after) if isinstance(after, (tuple, list)) else (after,)

    def body(*refs):
        settle(refs[:n], refs[n], refs[n + 1])

    outs = pl.pallas_call(
        body, name=name,
        out_shape=tuple(pltpu.HBM(t.shape, t.dtype) for t in bufs),
        in_specs=[HBM] * n + [SEM, SEM] + [ANY] * len(afters),
        out_specs=tuple([HBM] * n),
        input_output_aliases={i: i for i in range(n)},
        compiler_params=pltpu.CompilerParams(has_side_effects=DATAFLOW_EFFECT),
    )(*bufs, send_sems, recv_sems, *afters)
    return list(outs)


def _to_sibling(bufs, n, send_sems, recv_sems):
    x, y, c, _ = _position()
    return [pltpu.make_async_remote_copy(
        src_ref=bufs[a], dst_ref=bufs[n + a], send_sem=send_sems.at[a], recv_sem=recv_sems.at[a],
        device_id=(x, y, 1 - c), device_id_type=MESH_ID) for a in range(n)]


def _sibling_push_start(layer, sender_is_reducer, arrays, *, name):
    n = len(arrays)
    sender = layer if sender_is_reducer else 1 - layer

    def issue(bufs, send_sems, recv_sems):
        @pl.when(lax.axis_index("c") == sender)
        def _():
            for cp in _to_sibling(bufs, n, send_sems, recv_sems):
                cp.start()

    lands = [_landing(t.shape, t.dtype) for t in arrays]
    return _split_start(list(arrays) + lands, (n,), issue, name=name)


def _sibling_push_wait(layer, sender_is_reducer, started, after, *, name):
    n = len(started[2]) // 2
    sender = layer if sender_is_reducer else 1 - layer

    def settle(bufs, send_sems, recv_sems):
        c = lax.axis_index("c")

        @pl.when(c == sender)
        def _():
            for cp in _to_sibling(bufs, n, send_sems, recv_sems):
                cp.wait_send()

        @pl.when(c != sender)
        def _():
            for cp in _to_sibling(bufs, n, send_sems, recv_sems):
                cp.wait_recv()

    outs = _split_wait(started, after, settle, name=name)
    return outs[:n], outs[n:]


def _chip_copies(bufs, n, send_sems, recv_sems):
    x, y, c, chips = _position()
    return [pltpu.make_async_remote_copy(
        src_ref=bufs[a].at[_chip_index(chips[k])], dst_ref=bufs[n + a].at[k],
        send_sem=send_sems.at[3 * a + k], recv_sem=recv_sems.at[3 * a + k],
        device_id=(chips[k][0], chips[k][1], c), device_id_type=MESH_ID)
        for a in range(n) for k in range(3)]


def _chip_exchange_start(layer, parts, *, name):
    n = len(parts)

    def issue(bufs, send_sems, recv_sems):
        @pl.when(lax.axis_index("c") == layer)
        def _():
            for cp in _chip_copies(bufs, n, send_sems, recv_sems):
                cp.start()

    lands = [_landing((3,) + t.shape[1:], t.dtype) for t in parts]
    return _split_start(list(parts) + lands, (3 * n,), issue, name=name)


def _chip_exchange_wait(layer, started, after, *, name):
    n = len(started[2]) // 2

    def settle(bufs, send_sems, recv_sems):
        @pl.when(lax.axis_index("c") == layer)
        def _():
            for cp in _chip_copies(bufs, n, send_sems, recv_sems):
                cp.wait()

    outs = _split_wait(started, after, settle, name=name)
    return outs[:n], outs[n:]


def _gather_copies(layer, bufs, n, send_sems, recv_sems, arriving):
    x, y, c, chips = _position()
    me = _chip_index((x, y))
    return [pltpu.make_async_remote_copy(
        src_ref=bufs[a], dst_ref=bufs[n + a].at[_chip_index(chips[k]) if arriving else me],
        send_sem=send_sems.at[3 * a + k], recv_sem=recv_sems.at[3 * a + k],
        device_id=(chips[k][0], chips[k][1], c), device_id_type=MESH_ID)
        for a in range(n) for k in range(3)]


def _gather_start(layer, shards, *, name, after=None):
    n = len(shards)

    def issue(bufs, send_sems, recv_sems):
        @pl.when(lax.axis_index("c") == layer)
        def _():
            for cp in _gather_copies(layer, bufs, n, send_sems, recv_sems, False):
                cp.start()

    lands = [_landing((N_CHIPS,) + t.shape, t.dtype) for t in shards]
    return _split_start(list(shards) + lands, (3 * n,), issue, name=name, after=after)


def _gather_wait(layer, started, after, *, name):
    n = len(started[2]) // 2

    def settle(bufs, send_sems, recv_sems):
        @pl.when(lax.axis_index("c") == layer)
        def _():
            for cp in _gather_copies(layer, bufs, n, send_sems, recv_sems, False):
                cp.wait_send()
            for cp in _gather_copies(layer, bufs, n, send_sems, recv_sems, True):
                cp.wait_recv()

    return _split_wait(started, after, settle, name=name)[n:]


def _handoff_copies(bufs, n, send_sems, recv_sems):
    x, y, c, chips = _position()
    out = []
    for a in range(n):
        for k in range(3):
            slot = bufs[a].at[_chip_index(chips[k])]
            out.append(pltpu.make_async_remote_copy(
                src_ref=slot, dst_ref=slot, send_sem=send_sems.at[3 * a + k], recv_sem=recv_sems.at[3 * a + k],
                device_id=(x, y, 1 - c), device_id_type=MESH_ID))
    return out


def _handoff_start(layer, gathered, *, name):
    n = len(gathered)

    def issue(bufs, send_sems, recv_sems):
        @pl.when(lax.axis_index("c") == layer)
        def _():
            for cp in _handoff_copies(bufs, n, send_sems, recv_sems):
                cp.start()

    return _split_start(list(gathered), (3 * n,), issue, name=name)


def _handoff_wait(layer, started, after, *, name):
    n = len(started[2])

    def settle(bufs, send_sems, recv_sems):
        c = lax.axis_index("c")

        @pl.when(c == layer)
        def _():
            for cp in _handoff_copies(bufs, n, send_sems, recv_sems):
                cp.wait_send()

        @pl.when(c != layer)
        def _():
            for cp in _handoff_copies(bufs, n, send_sems, recv_sems):
                cp.wait_recv()

    return _split_wait(started, after, settle, name=name)


F32_SUBLANES = 8
BF16_SUBLANES = 16
MAX_STREAM_ROWS = 704
MAX_ADAMW_ROWS = 384


def _row_tile(a, sublanes=BF16_SUBLANES, max_rows=MAX_STREAM_ROWS):
    best = None
    for cand in range(sublanes, min(a, max_rows) + 1, sublanes):
        if a % cand == 0:
            best = cand
    assert best is not None, a
    return best


def _pair_sum(mine, sib, active, *, name):
    nchip, a, b = sib.shape
    ta = _row_tile(a)

    def body(act_ref, m_ref, s_ref, o_ref):
        @pl.when(act_ref[0] == 1)
        def _():
            o_ref[...] = (m_ref[...].astype(F32) + s_ref[...].astype(F32)).astype(BF16)

    blk = pl.BlockSpec((1, ta, b), lambda j, r, act: (j * act[0], r * act[0], 0))
    return pl.pallas_call(
        body, name=name,
        out_shape=jax.ShapeDtypeStruct((nchip, a, b), BF16),
        grid_spec=pltpu.PrefetchScalarGridSpec(
            num_scalar_prefetch=1, grid=(nchip, a // ta), in_specs=[blk, blk], out_specs=blk),
        compiler_params=_cparams(("arbitrary", "arbitrary")),
    )(active, mine, sib)


def _total_sum(own, recv, active, *, name):
    a, b = own.shape
    ta = _row_tile(a)

    def body(act_ref, o_ref, r_ref, t_ref):
        @pl.when(act_ref[0] == 1)
        def _():
            acc = o_ref[...].astype(F32)
            for k in range(3):
                acc = acc + r_ref[k].astype(F32)
            t_ref[...] = acc

    blk = pl.BlockSpec((ta, b), lambda r, act: (r * act[0], 0))
    return pl.pallas_call(
        body, name=name,
        out_shape=jax.ShapeDtypeStruct((a, b), F32),
        grid_spec=pltpu.PrefetchScalarGridSpec(
            num_scalar_prefetch=1, grid=(a // ta,),
            in_specs=[blk, pl.BlockSpec((3, ta, b), lambda r, act: (0, r * act[0], 0))], out_specs=blk),
        compiler_params=_cparams(("arbitrary",)),
    )(active, own, recv)


def _all_reduce_small(packed):
    rows, width = packed.shape

    def body(x_ref, out_ref, gath, send_sems, recv_sems, local_sem):
        x, y, c, chips = _position()
        me, sibling = (x, y, c), (x, y, 1 - c)

        def slot(px, py, pc):
            return gath.at[4 * px + 2 * py + pc]

        def copy(k, block, to, src=None):
            return pltpu.make_async_remote_copy(
                src_ref=slot(*block) if src is None else src, dst_ref=slot(*block),
                send_sem=send_sems.at[k], recv_sem=recv_sems.at[k], device_id=to, device_id_type=MESH_ID)

        mine = pltpu.make_async_copy(x_ref, slot(*me), local_sem)
        mine.start()
        first = [copy(0, me, sibling, src=x_ref)]
        first += [copy(1 + j, me, (*chip, c), src=x_ref) for j, chip in enumerate(chips)]
        for cp in first:
            cp.start()
        passed = [copy(4 + j, (*chip, c), sibling) for j, chip in enumerate(chips)]
        for j, chip in enumerate(chips):
            copy(1 + j, (*chip, c), me).wait_recv()
            passed[j].start()
        copy(0, sibling, me).wait_recv()
        for j, chip in enumerate(chips):
            copy(4 + j, (*chip, 1 - c), me).wait_recv()
        for cp in first + passed:
            cp.wait_send()
        mine.wait()
        acc = gath[0]
        for d in range(1, N_DEV):
            acc = acc + gath[d]
        out_ref[...] = acc

    return pl.pallas_call(
        body, name="all_reduce_small",
        out_shape=jax.ShapeDtypeStruct((rows, width), F32),
        in_specs=[pl.BlockSpec(memory_space=pltpu.VMEM)],
        out_specs=pl.BlockSpec(memory_space=pltpu.VMEM),
        scratch_shapes=[pltpu.VMEM((N_DEV, rows, width), F32), pltpu.SemaphoreType.DMA((7,)),
                        pltpu.SemaphoreType.DMA((7,)), pltpu.SemaphoreType.DMA],
    )(packed)


def _mixer_forward_branches(x0, h, w, s, dep=None, before_conv=None):
    proj = _matmul(h, w["w_in_t"], mode="nt", out_dtype=BF16, tm=1024, tn=1664, tk=1024, name="proj_fwd",
                   dep=dep)
    oa, states = _gla_fwd(proj, w["w_fg2"], w["b_fg"], w["gla_norm_g"], s=s)
    conv_wb = w["conv_wb"]
    if before_conv is not None:
        conv_wb = conv_wb + before_conv(oa)[0, 0]
    cb_in = _conv_fwd(proj, conv_wb, s=s)
    return dict(x0=x0, h=h, proj=proj, oa=oa, states=states, cb_in=cb_in)


def _mixer_forward_out(sv, w, s, dep=None):
    ya, yb, mix, x1 = _mixer_out_fwd(sv["oa"], sv["cb_in"], sv["proj"], sv["x0"], w["w_oa"], w["w_ob"], w["w_o"],
                                     s=s, dep=dep)
    return x1, dict(sv, ya=ya, yb=yb, mix=mix)


def _ffn_forward_hidden(x1, w, s, dep=None):
    tr = min(256, s)
    d, f = D_MODEL, FFN_HIDDEN
    n_dep = dep is not None

    def body(*refs):
        x_ref, g_ref, wg_ref, wu_ref, h2_ref, gt_ref, up_ref, hid_ref = refs[n_dep:]
        xv = x_ref[...]
        rs = lax.rsqrt(jnp.mean(xv * xv, axis=-1, keepdims=True) + NORM_EPS)
        h2 = (xv * rs * g_ref[...]).astype(BF16)
        h2_ref[...] = h2
        gt = lax.dot_general(h2, wg_ref[...], _DOT_DIMS["nt"], preferred_element_type=F32).astype(BF16)
        up = lax.dot_general(h2, wu_ref[...], _DOT_DIMS["nt"], preferred_element_type=F32).astype(BF16)
        gt_ref[...] = gt
        up_ref[...] = up
        gv = gt.astype(F32)
        hid_ref[...] = (gv * _sigmoid(gv) * up.astype(F32)).astype(BF16)

    row_d = pl.BlockSpec((tr, d), lambda i: (i, 0))
    row_f = pl.BlockSpec((tr, f), lambda i: (i, 0))
    weight = pl.BlockSpec((f, d), lambda i: (0, 0))
    tok, tok_specs = _token_spec(dep)
    h2, gt, up, hid = pl.pallas_call(
        body, name="ffn_hidden_fwd",
        out_shape=(jax.ShapeDtypeStruct((s, d), BF16),) + tuple(jax.ShapeDtypeStruct((s, f), BF16) for _ in range(3)),
        grid=(s // tr,),
        in_specs=tok_specs + [row_d, pl.BlockSpec((1, d), lambda i: (0, 0)), weight, weight],
        out_specs=(row_d, row_f, row_f, row_f),
        compiler_params=_cparams(("parallel",)),
    )(*tok, x1, w["norm2_g"].reshape(1, d), w["w_gate_t"], w["w_up_t"])
    return dict(x1=x1, h2=h2, gt=gt, up=up, hid=hid)


def _ffn_forward_out(sv, w, dep=None):
    return _matmul(sv["hid"], w["w_ffn_down"], mode="nn", out_dtype=F32, tm=1024, tn=1024, tk=FFN_HIDDEN,
                   name="ffn_out_fwd", add=sv["x1"], dep=dep)


def _ffn_backward_grads(dx2b, w, sv, s):
    g = {}
    dgt, dup = _ffn_out_bwd(dx2b, w["w_ffn_down"], sv["gt"], sv["up"], s=s)
    g["w_ffn_down"] = _matmul(sv["hid"], dx2b, mode="tn", out_dtype=BF16, tm=1408, tn=1024, tk=1024,
                              name="ffn_out_wgrad")
    g["w_gate_t"] = _matmul(dgt, sv["h2"], mode="tn", out_dtype=BF16, tm=1408, tn=1024, tk=1024,
                            name="ffn_gate_wgrad")
    g["w_up_t"] = _matmul(dup, sv["h2"], mode="tn", out_dtype=BF16, tm=1408, tn=1024, tk=1024,
                          name="ffn_up_wgrad")
    return g, dgt, dup


def _ffn_backward_input(dgt, dup, dx2, w, sv, dep=None):
    dh2 = _matmul(dgt, w["w_gate_t"], mode="nn", out_dtype=BF16, tm=1024, tn=1024, tk=FFN_HIDDEN,
                  name="ffn_gate_bwd", dep=dep)
    return _matmul_norm_bwd(dup, w["w_up_t"], sv["x1"], w["norm2_g"], dx2, tm=512, tk=FFN_HIDDEN,
                            name="ffn_up_bwd_norm2_bwd", add=dh2)


def _mixer_backward_branches(dx1b, w, sv, s, dep=None):
    g = {}
    dya, dyb, dcb, doa, dproj = _mixer_out_bwd(dx1b, sv["proj"], sv["ya"], sv["yb"], w["w_oa"], w["w_ob"],
                                               w["w_o"], s=s, dep=dep)
    g["w_o"] = _matmul(sv["mix"], dx1b, mode="tn", out_dtype=BF16, tm=1024, tn=1024, tk=2048, name="wo_wgrad")
    g["w_ob"] = _matmul(sv["cb_in"], dyb, mode="tn", out_dtype=BF16, tm=1024, tn=1024, tk=2048, name="yb_wgrad")
    g["w_oa"] = _matmul(sv["oa"], dya, mode="tn", out_dtype=BF16, tm=1024, tn=1024, tk=2048, name="ya_wgrad")
    dproj, g["conv_wb"] = _conv_bwd(sv["proj"], dcb, w["conv_wb"], dproj, s=s)
    dproj, dfz, g["gla_norm_g"], g["w_fg2"], g["b_fg"] = _gla_bwd(
        sv["proj"], doa, sv["states"], w["w_fg2"], w["b_fg"], w["gla_norm_g"], dproj, s=s)
    return g, _put_fz(dproj, dfz, s=s)


def _proj_wgrad(dproj, sv, dep=None):
    return _matmul(dproj, sv["h"], mode="tn", out_dtype=BF16, tm=1664, tn=1024, tk=1024, name="proj_wgrad",
                   dep=dep)


def _proj_bwd(dproj, w, x0, dres, dep=None):
    return _matmul_norm_bwd(dproj, w["w_in_t"], x0, w["norm1_g"], dres, tm=512, tk=1664,
                            name="proj_bwd_norm1_bwd", dep=dep)


def _cols_from_chips(t):
    return jnp.transpose(t, (1, 0, 2)).reshape(t.shape[1], -1)


W_IN_ROWS = IN_WIDTH // N_CHIPS
W_IN_ROWS_PAD = -(-W_IN_ROWS // BF16_SUBLANES) * BF16_SUBLANES


def _w_in_t_shard(t):
    return jnp.pad(jnp.transpose(t, (0, 2, 1)), ((0, 0), (0, W_IN_ROWS_PAD - W_IN_ROWS), (0, 0)))


def _w_in_t_unshard(t):
    return jnp.transpose(t[:, :W_IN_ROWS], (0, 2, 1))


def _w_in_row_segments():
    runs = [(0, 0, FZ_ORIG), (FZ_ORIG, FZ_ORIG + GATE_RANK, COL_FZ - FZ_ORIG), (COL_FZ, FZ_ORIG, GATE_RANK)]
    out = []
    for kernel_row, ref_row, length in runs:
        while length:
            chip, local = divmod(ref_row, W_IN_ROWS)
            n = min(length, W_IN_ROWS - local)
            out.append((kernel_row, chip * W_IN_ROWS_PAD + local, n))
            kernel_row, ref_row, length = kernel_row + n, ref_row + n, length - n
    return out


def _permute_rows(src, n_out, segments, *, name):
    n_src, width = src.shape
    block = _row_tile(n_out)
    window = block + BF16_SUBLANES
    assert n_src >= window and src.dtype == BF16

    def body(src_ref, out_ref):
        rows = lax.broadcasted_iota(jnp.int32, (block, window), 0)
        cols = lax.broadcasted_iota(jnp.int32, (block, window), 1)
        for b0 in range(0, n_out, block):
            acc = None
            for o, s, n in segments:
                lo, hi = max(o, b0), min(o + n, b0 + block)
                if lo >= hi:
                    continue
                first = s + (lo - o)
                base = max(0, min(first // BF16_SUBLANES * BF16_SUBLANES, n_src - window))
                shift = (first - base) - (lo - b0)
                pick = (cols == rows + shift) & (rows >= lo - b0) & (rows < hi - b0)
                part = jnp.dot(jnp.where(pick, 1.0, 0.0).astype(BF16), src_ref[base:base + window, :],
                               preferred_element_type=F32)
                acc = part if acc is None else acc + part
            out_ref[b0:b0 + block, :] = (jnp.zeros((block, width), F32) if acc is None else acc).astype(BF16)

    return pl.pallas_call(
        body, name=name,
        out_shape=jax.ShapeDtypeStruct((n_out, width), BF16),
        in_specs=[pl.BlockSpec(memory_space=pltpu.VMEM)],
        out_specs=pl.BlockSpec(memory_space=pltpu.VMEM),
        compiler_params=_cparams(),
    )(src)


def _w_in_t_to_kernel(t):
    return _permute_rows(t.reshape(N_CHIPS * W_IN_ROWS_PAD, D_MODEL), PROJ_W, _w_in_row_segments(),
                         name="w_in_to_kernel_rows")


def _w_in_t_from_kernel(g):
    segments = [(slab, kernel_row, n) for kernel_row, slab, n in _w_in_row_segments()]
    out = _permute_rows(g, N_CHIPS * W_IN_ROWS_PAD, segments, name="w_in_to_chip_rows")
    return out.reshape(N_CHIPS, W_IN_ROWS_PAD, D_MODEL)


def kernel(x, norm1_g, w_in, w_fg2, b_fg, gla_norm_g, w_oa, conv_w, conv_b, w_ob, w_o, norm2_g, w_ffn_gate, w_ffn_up, w_ffn_down, final_g, loss_target, m_norm1_g, m_w_in, m_w_fg2, m_b_fg, m_gla_norm_g, m_w_oa, m_conv_w, m_conv_b, m_w_ob, m_w_o, m_norm2_g, m_w_ffn_gate, m_w_ffn_up, m_w_ffn_down, m_final_g, v_norm1_g, v_w_in, v_w_fg2, v_b_fg, v_gla_norm_g, v_w_oa, v_conv_w, v_conv_b, v_w_ob, v_w_o, v_norm2_g, v_w_ffn_gate, v_w_ffn_up, v_w_ffn_down, v_final_g):
    cx_ = lax.axis_index("x")
    cy_ = lax.axis_index("y")
    cc_ = lax.axis_index("c")
    me = 2 * cx_ + cy_
    on_core0 = jnp.where(cc_ == 0, 1.0, 0.0).astype(F32).reshape(1, 1)

    def swap(t):
        return jnp.swapaxes(t, 1, 2)

    big_names = ["w_in", "w_oa", "w_ob", "w_o", "w_ffn_gate", "w_ffn_up", "w_ffn_down"]
    views = dict(
        w_in=tuple(_w_in_t_shard(t) for t in (w_in, m_w_in, v_w_in)),
        w_oa=(w_oa, m_w_oa, v_w_oa), w_ob=(w_ob, m_w_ob, v_w_ob), w_o=(w_o, m_w_o, v_w_o),
        w_ffn_gate=tuple(swap(t) for t in (w_ffn_gate, m_w_ffn_gate, v_w_ffn_gate)),
        w_ffn_up=tuple(swap(t) for t in (w_ffn_up, m_w_ffn_up, v_w_ffn_up)),
        w_ffn_down=(w_ffn_down, m_w_ffn_down, v_w_ffn_down))
    from_view = dict(w_in=_w_in_t_unshard, w_ffn_gate=swap, w_ffn_up=swap)

    s = x.shape[1]

    proj_names = ["w_in"]
    first_names = ["w_in", "w_fg2", "conv_w"]
    rest_names = ["w_oa", "w_ob", "w_o", "w_ffn_gate", "w_ffn_up", "w_ffn_down"]
    mixer_names = ["w_oa", "w_ob", "w_o"]
    ffn_names = ["w_ffn_gate", "w_ffn_up", "w_ffn_down"]
    weight_key = dict(w_in="w_in_t", w_ffn_gate="w_gate_t", w_ffn_up="w_up_t")
    full_shape = dict(w_oa=(VAL_WIDTH, D_MODEL), w_ob=(CONV_CH, D_MODEL), w_o=(D_MODEL, D_MODEL),
                      w_ffn_gate=(FFN_HIDDEN, D_MODEL), w_ffn_up=(FFN_HIDDEN, D_MODEL),
                      w_ffn_down=(FFN_HIDDEN, D_MODEL))

    conv_w_p = jnp.pad(conv_w, ((0, 0), (0, 8 - conv_w.shape[1]), (0, 0)))
    small_shards = dict(w_fg2=w_fg2, conv_w=conv_w_p)

    def shards_of(l, names):
        return [small_shards[n][l] if n in small_shards else views[n][0][l].astype(BF16) for n in names]

    def small_weights(l):
        return dict(norm1_g=norm1_g[l], norm2_g=norm2_g[l], b_fg=b_fg[l].reshape(1, KEY_WIDTH),
                    gla_norm_g=gla_norm_g[l].reshape(1, HEAD_V))

    def full_weights(l, names, gathered, shards):
        out = {}
        for n, g, t in zip(names, gathered, shards):
            g = lax.dynamic_update_index_in_dim(g, t[None], me, axis=0)
            if n == "w_in":
                out["w_in_t"] = _w_in_t_to_kernel(g)
            elif n == "w_fg2":
                out[n] = jnp.pad(_cols_from_chips(g), ((0, FZ_PAD - GATE_RANK), (0, 0))).astype(BF16)
            elif n == "conv_w":
                out["conv_wb"] = (jnp.pad(_cols_from_chips(g)[:3], ((0, 5), (0, 0)))
                                  + jnp.pad(conv_b[l].reshape(1, CONV_CH), ((3, 4), (0, 0))))
            else:
                out[weight_key.get(n, n)] = g.reshape(full_shape[n])
        return out

    def per_chip(g, names):
        out = []
        for n in names:
            t = g[weight_key.get(n, n)]
            out.append(_w_in_t_from_kernel(t) if n == "w_in"
                       else t.reshape(N_CHIPS, t.shape[0] // N_CHIPS, t.shape[1]))
        return out

    def gather(l, names, tag, after=None):
        shards = shards_of(l, names)
        return shards, _gather_start(l, shards, name=f"gather_{tag}_start", after=after)

    def gathered_to_sibling(l, started, after, tag):
        return _handoff_start(l, _gather_wait(l, started, after, name=f"gather_{tag}_wait"),
                              name=f"handoff_{tag}_start")

    def feed(l, g, names, tag):
        return _sibling_push_start(l, False, per_chip(g, names), name=f"feed_{tag}_start")

    def reduces(l):
        return jnp.where(cc_ == l, 1, 0).astype(jnp.int32).reshape(1)

    def pair_sums(l, fed, after, names, tag):
        mine, sib = _sibling_push_wait(l, False, fed, after, name=f"feed_{tag}_wait")
        return [_pair_sum(a, b, reduces(l), name=f"pair_sum_{tag}_{n}") for n, a, b in zip(names, mine, sib)]

    def exchange(l, parts, tag):
        return _chip_exchange_start(l, parts, name=f"exchange_{tag}_start")

    def totals_of(l, swapped, after, names, tag):
        parts, recvs = _chip_exchange_wait(l, swapped, after, name=f"exchange_{tag}_wait")
        owns = [lax.dynamic_index_in_dim(p, me, 0, keepdims=False) for p in parts]
        return [_total_sum(o, rc, reduces(l), name=f"total_{tag}_{n}") for n, o, rc in zip(names, owns, recvs)]

    def share(l, totals, tag):
        return _sibling_push_start(l, True, totals, name=f"share_{tag}_start")

    def shared(l, sharing, after, names, tag):
        totals, others = _sibling_push_wait(l, True, sharing, after, name=f"share_{tag}_wait")
        return {n: (t, o) for n, t, o in zip(names, totals, others)}

    def branches_under_gather(l, xin, h, w, tag, shards_r, started_r):
        box = {}

        def before_conv(oa):
            box["handoff"] = gathered_to_sibling(l, started_r, oa, tag)
            return box["handoff"][3]

        sv = _mixer_forward_branches(xin, h, w, s, dep=started_r[3], before_conv=before_conv)
        rest = _handoff_wait(l, box["handoff"], sv["cb_in"], name=f"handoff_{tag}_wait")
        w.update(full_weights(l, rest_names, rest, shards_r))
        return sv

    shards_p, started = gather(0, first_names, "p0")
    h0 = _rms_fwd(x[0], norm1_g[0], name="rms1_fwd")
    later_shards = shards_of(0, rest_names) + shards_of(1, first_names) + shards_of(1, rest_names)
    started = gathered_to_sibling(0, started, (h0, views["w_in"][1], *later_shards), "p0")
    w0 = small_weights(0)
    w0.update(full_weights(0, first_names, _handoff_wait(0, started, views["w_in"][2], name="handoff_p0_wait"),
                           shards_p))
    sv0 = branches_under_gather(0, x[0], h0, w0, "r0", *gather(0, rest_names, "r0", after=w0["w_in_t"]))

    shards_p1, started = gather(1, first_names, "p1", after=w0["w_o"])
    x1, sv0m = _mixer_forward_out(sv0, w0, s, dep=started[3])
    sv0f = _ffn_forward_hidden(x1, w0, s)
    handoff_1 = gathered_to_sibling(1, started, sv0f["hid"], "p1")
    rest_1 = gather(1, rest_names, "r1", after=handoff_1[2][0])
    x2 = _ffn_forward_out(sv0f, w0, dep=handoff_1[3] + rest_1[1][3])
    w1 = small_weights(1)
    w1.update(full_weights(1, first_names, _handoff_wait(1, handoff_1, x2, name="handoff_p1_wait"), shards_p1))

    h1 = _rms_fwd(x2, norm1_g[1], name="rms1_fwd")
    x3, sv1m = _mixer_forward_out(branches_under_gather(1, x2, h1, w1, "r1", *rest_1), w1, s)
    sv1f = _ffn_forward_hidden(x3, w1, s)
    x4 = _ffn_forward_out(sv1f, w1)
    loss_local, dx, dxb, dgf = _loss_head(x4, final_g, loss_target[0])
    loss = lax.psum(loss_local[0, 0], ("x", "y", "c"))

    g1, dgt, dup = _ffn_backward_grads(dxb, w1, sv1f, s)
    dx_mid, dxb_mid, g1["norm2_g"] = _ffn_backward_input(dgt, dup, dx, w1, sv1f)
    gm, dproj = _mixer_backward_branches(dxb_mid, w1, sv1m, s)
    g1.update(gm)
    g1["w_in_t"] = _proj_wgrad(dproj, sv1m)
    fed_1 = feed(1, g1, big_names, "l1")
    dx, dxb, g1["norm1_g"] = _proj_bwd(dproj, w1, sv1m["x0"], dx_mid, dep=fed_1[3])

    g0, dgt, dup = _ffn_backward_grads(dxb, w0, sv0f, s)
    swap_1 = exchange(1, pair_sums(1, fed_1, dgt, big_names, "l1"), "l1")
    fed_f = feed(0, g0, ffn_names, "f0")
    dx_mid, dxb_mid, g0["norm2_g"] = _ffn_backward_input(dgt, dup, dx, w0, sv0f, dep=swap_1[3] + fed_f[3])
    swap_f = exchange(0, pair_sums(0, fed_f, dx_mid, ffn_names, "f0"), "f0")
    gm, dproj = _mixer_backward_branches(dxb_mid, w0, sv0m, s, dep=swap_f[3])
    g0.update(gm)
    totals_1 = totals_of(1, swap_1, dproj, big_names, "l1")
    totals_f = totals_of(0, swap_f, dproj, ffn_names, "f0")
    share_1, share_f = share(1, totals_1, "l1"), share(0, totals_f, "f0")
    fed_m = feed(0, g0, mixer_names, "m0")
    g0["w_in_t"] = _proj_wgrad(dproj, sv0m, dep=share_1[3] + share_f[3] + fed_m[3])
    fed_p = feed(0, g0, proj_names, "p0")
    swap_m = exchange(0, pair_sums(0, fed_m, g0["w_in_t"], mixer_names, "m0"), "m0")
    grad_x, _, g0["norm1_g"] = _proj_bwd(dproj, w0, sv0m["x0"], dx_mid, dep=fed_p[3] + swap_m[3])
    reduced1 = shared(1, share_1, grad_x, big_names, "l1")
    reduced0 = shared(0, share_f, grad_x, ffn_names, "f0")

    parts_p = pair_sums(0, fed_p, grad_x, proj_names, "p0")
    totals_m = totals_of(0, swap_m, parts_p[0], mixer_names, "m0")
    swap_p = exchange(0, parts_p, "p0")
    share_m = share(0, totals_m, "m0")
    reduced0.update(shared(0, share_m, share_m[3], mixer_names, "m0"))
    grads = [g0, g1]

    def small_rows(t):
        return t.reshape(-1, D_MODEL)

    def tile_rows(t):
        return jnp.pad(t, ((0, -t.shape[0] % F32_SUBLANES), (0, 0)))

    g0, g1 = grads
    pieces = [
        jnp.concatenate([g0["norm1_g"], g1["norm1_g"]], axis=0),
        jnp.concatenate([g0["norm2_g"], g1["norm2_g"]], axis=0),
        dgf,
        small_rows(jnp.concatenate([g0["b_fg"], g1["b_fg"]], axis=1)),
        small_rows(jnp.concatenate([g0["gla_norm_g"], g1["gla_norm_g"],
                                    jnp.zeros((1, D_MODEL - 2 * HEAD_V), F32)], axis=1)),
        jnp.concatenate([g0["conv_wb"][3:4], g1["conv_wb"][3:4]], axis=0),
        jnp.concatenate([g0["conv_wb"][:3], g1["conv_wb"][:3]], axis=0),
        small_rows(jnp.stack([g0["w_fg2"][:GATE_RANK], g1["w_fg2"][:GATE_RANK]])),
    ]
    small = _all_reduce_small(jnp.concatenate([tile_rows(p) for p in pieces], axis=0))
    sg = dict(
        norm1_g=small[0:2], norm2_g=small[8:10], final_g=small[16],
        b_fg=small[24].reshape(DEPTH, KEY_WIDTH), gla_norm_g=small[32, :DEPTH * HEAD_V].reshape(DEPTH, HEAD_V),
        conv_b=small[40:42],
        conv_w=lax.dynamic_slice_in_dim(small[48:54].reshape(DEPTH, 3, CONV_CH), me * (CONV_CH // N_CHIPS),
                                        CONV_CH // N_CHIPS, axis=2),
        w_fg2=lax.dynamic_slice_in_dim(small[56:72].reshape(DEPTH, GATE_RANK, KEY_WIDTH),
                                       me * (KEY_WIDTH // N_CHIPS), KEY_WIDTH // N_CHIPS, axis=2),
    )

    small_params = dict(norm1_g=(norm1_g, m_norm1_g, v_norm1_g), w_fg2=(w_fg2, m_w_fg2, v_w_fg2),
                        b_fg=(b_fg, m_b_fg, v_b_fg), gla_norm_g=(gla_norm_g, m_gla_norm_g, v_gla_norm_g),
                        conv_w=(conv_w, m_conv_w, v_conv_w), conv_b=(conv_b, m_conv_b, v_conv_b),
                        norm2_g=(norm2_g, m_norm2_g, v_norm2_g), final_g=(final_g, m_final_g, v_final_g))
    order = ["norm1_g", "w_in", "w_fg2", "b_fg", "gla_norm_g", "w_oa", "conv_w", "conv_b", "w_ob", "w_o",
             "norm2_g", "w_ffn_gate", "w_ffn_up", "w_ffn_down", "final_g"]
    results = {}

    def update_large(n):
        w_, m_, v_ = views[n]
        outs = _adamw_layers(w_, (reduced0[n][0], reduced1[n][0]), (reduced0[n][1], reduced1[n][1]), on_core0,
                             m_, v_, name="adamw_" + n)
        back = from_view.get(n)
        results[n] = tuple(back(r) for r in outs) if back else outs
        return outs[1]

    for n, (w_, m_, v_) in small_params.items():
        results[n] = (sg[n],) + _adamw(w_, sg[n], m_, v_, name="adamw_" + n)
    updated = tuple(update_large(n) for n in rest_names)
    share_p = share(0, totals_of(0, swap_p, updated, proj_names, "p0"), "p0")
    reduced0.update(shared(0, share_p, share_p[3], proj_names, "p0"))
    update_large("w_in")
    return (loss, grad_x[None], *[results[n][0] for n in order], *[results[n][1] for n in order],
            *[results[n][2] for n in order], *[results[n][3] for n in order])
```

```python
import functools

import jax
import jax.numpy as jnp
from jax import lax
from jax.experimental import pallas as pl
from jax.experimental.pallas import tpu as pltpu

F32 = jnp.float32
BF16 = jnp.bfloat16

D_MODEL = 1024
DEPTH = 2
CHUNK = 64
GLA_HEADS = 4
KEY_WIDTH = 512
VAL_WIDTH = 1024
HEAD_K = 128
HEAD_V = 256
GATE_RANK = 16
GATE_TAU = 16.0
CONV_CH = 1024
FFN_HIDDEN = 2816
IN_WIDTH = 8208
NORM_EPS = 1e-6
Q_SCALE = HEAD_K ** -0.5
ADAM_LR = 0.001
ADAM_B1 = 0.9
ADAM_B2 = 0.999
ADAM_EPS = 1e-08
ADAM_WD = 0.01
ADAM_STEP = 10

N_CHIPS = 4
N_DEV = 8

LANE = 128
FZ_PAD = LANE
PROJ_W = 8192 + FZ_PAD
COL_Q, COL_K, COL_V, COL_R, COL_GBI, COL_GCI, COL_CX, COL_GA, COL_GB, COL_FZ = (
    0, 512, 1024, 2048, 3072, 4096, 5120, 6144, 7168, 8192)
FZ_ORIG = 3072

VMEM_LIMIT = 52 * 1024 * 1024
HALO = 16


def _cparams(sem=None):
    return pltpu.CompilerParams(dimension_semantics=sem, vmem_limit_bytes=VMEM_LIMIT)


def _sigmoid(x):
    return jax.nn.sigmoid(x)


def _log_sigmoid(x):
    return jnp.minimum(x, 0.0) - jnp.log1p(jnp.exp(-jnp.abs(x)))


_DOT_DIMS = {
    "nn": (((1,), (0,)), ((), ())),
    "nt": (((1,), (1,)), ((), ())),
    "tn": (((0,), (0,)), ((), ())),
}


def _matmul(a, b, *, mode, out_dtype, tm, tn, tk, name, add=None, dep=None):
    if mode == "nn":
        (m, k), n = a.shape, b.shape[1]
    elif mode == "nt":
        (m, k), n = a.shape, b.shape[0]
    else:
        (k, m), n = a.shape, b.shape[1]
    tm, tn, tk = min(tm, m), min(tn, n), min(tk, k)
    assert m % tm == 0 and n % tn == 0 and k % tk == 0, (name, m, n, k, tm, tn, tk)
    nk = k // tk
    has_add = add is not None
    has_dep = dep is not None

    def body(*refs):
        if has_dep:
            refs = refs[1:]
        if has_add:
            a_ref, b_ref, add_ref, o_ref = refs[:4]
            scratch = refs[4:]
        else:
            a_ref, b_ref, o_ref = refs[:3]
            add_ref = None
            scratch = refs[3:]
        part = lax.dot_general(a_ref[...], b_ref[...], _DOT_DIMS[mode], preferred_element_type=F32)

        def finish(acc):
            if add_ref is not None:
                acc = acc + add_ref[...].astype(F32)
            o_ref[...] = acc.astype(o_ref.dtype)

        if nk == 1:
            finish(part)
        else:
            acc_ref = scratch[0]
            kk = pl.program_id(2)

            @pl.when(kk == 0)
            def _():
                acc_ref[...] = part

            @pl.when(kk > 0)
            def _():
                acc_ref[...] += part

            @pl.when(kk == nk - 1)
            def _():
                finish(acc_ref[...])

    if mode == "nn":
        a_spec = pl.BlockSpec((tm, tk), lambda i, j, kk: (i, kk))
        b_spec = pl.BlockSpec((tk, tn), lambda i, j, kk: (kk, j))
    elif mode == "nt":
        a_spec = pl.BlockSpec((tm, tk), lambda i, j, kk: (i, kk))
        b_spec = pl.BlockSpec((tn, tk), lambda i, j, kk: (j, kk))
    else:
        a_spec = pl.BlockSpec((tk, tm), lambda i, j, kk: (kk, i))
        b_spec = pl.BlockSpec((tk, tn), lambda i, j, kk: (kk, j))
    o_spec = pl.BlockSpec((tm, tn), lambda i, j, kk: (i, j))
    in_specs = [a_spec, b_spec] + ([o_spec] if has_add else [])
    operands = (a, b) + ((add,) if has_add else ())
    if has_dep:
        in_specs = [pl.BlockSpec(dep.shape, lambda i, j, kk: (0, 0))] + in_specs
        operands = (dep,) + operands
    return pl.pallas_call(
        body,
        name=name,
        out_shape=jax.ShapeDtypeStruct((m, n), out_dtype),
        grid=(m // tm, n // tn, nk),
        in_specs=in_specs,
        out_specs=o_spec,
        scratch_shapes=[pltpu.VMEM((tm, tn), F32)] if nk > 1 else [],
        compiler_params=_cparams(("parallel", "parallel", "arbitrary")),
    )(*operands)


def _matmul_norm_bwd(a, b, x, g, dres, *, tm, tk, name, second=None, dep=None):
    (m, k), d = a.shape, b.shape[1]
    tm, tk = min(tm, m), min(tk, k)
    assert m % tm == 0 and k % tk == 0 and x.shape == (m, d), (name, m, k, d, tm, tk)
    nk = k // tk
    n_mm = 2 if second is None else 4
    has_dep = dep is not None

    def body(*refs):
        refs = refs[has_dep:]
        x_ref, g_ref, dres_ref, dx_ref, dxb_ref, dg_ref = refs[n_mm:n_mm + 6]
        scratch = refs[n_mm + 6:]
        add_ref = None
        i = pl.program_id(0)
        part = jnp.dot(refs[0][...], refs[1][...], preferred_element_type=F32)
        if second is not None:
            part = part + jnp.dot(refs[2][...], refs[3][...], preferred_element_type=F32)

        def finish(dh):
            if add_ref is not None:
                dh = dh + add_ref[...].astype(F32)
            xv = x_ref[...]
            rs = lax.rsqrt(jnp.mean(xv * xv, axis=-1, keepdims=True) + NORM_EPS)
            nrm = xv * rs
            dn = dh * g_ref[...]
            dx = dres_ref[...] + rs * (dn - nrm * jnp.mean(dn * nrm, axis=-1, keepdims=True))
            dx_ref[...] = dx
            dxb_ref[...] = dx.astype(BF16)
            dg_part = jnp.sum(dh * nrm, axis=0, keepdims=True)

            @pl.when(i == 0)
            def _():
                dg_ref[...] = dg_part

            @pl.when(i > 0)
            def _():
                dg_ref[...] += dg_part

        if nk == 1:
            finish(part)
        else:
            acc_ref = scratch[0]
            kk = pl.program_id(1)

            @pl.when(kk == 0)
            def _():
                acc_ref[...] = part

            @pl.when(kk > 0)
            def _():
                acc_ref[...] += part

            @pl.when(kk == nk - 1)
            def _():
                finish(acc_ref[...])

    row = pl.BlockSpec((tm, d), lambda i, kk: (i, 0))
    vec = pl.BlockSpec((1, d), lambda i, kk: (0, 0))
    in_specs = [pl.BlockSpec((tm, tk), lambda i, kk: (i, kk)), pl.BlockSpec((tk, d), lambda i, kk: (kk, 0))]
    operands = [a, b]
    if second is not None:
        in_specs = in_specs + in_specs
        operands += list(second)
    in_specs += [row, vec, row]
    operands += [x, g.reshape(1, d), dres]
    if has_dep:
        in_specs = [pl.BlockSpec(dep.shape, lambda i, kk: (0, 0))] + in_specs
        operands = [dep] + operands
    return pl.pallas_call(
        body, name=name,
        out_shape=(jax.ShapeDtypeStruct((m, d), F32), jax.ShapeDtypeStruct((m, d), BF16),
                   jax.ShapeDtypeStruct((1, d), F32)),
        grid=(m // tm, nk),
        in_specs=in_specs,
        out_specs=(row, row, vec),
        scratch_shapes=[pltpu.VMEM((tm, d), F32)] if nk > 1 else [],
        compiler_params=_cparams(("arbitrary", "arbitrary")),
    )(*operands)


def _rms_fwd(x, g, *, name):
    s, d = x.shape
    tr = min(512, s)

    def body(x_ref, g_ref, h_ref):
        xv = x_ref[...]
        rs = lax.rsqrt(jnp.mean(xv * xv, axis=-1, keepdims=True) + NORM_EPS)
        h_ref[...] = (xv * rs * g_ref[...]).astype(BF16)

    return pl.pallas_call(
        body, name=name,
        out_shape=jax.ShapeDtypeStruct((s, d), BF16),
        grid=(s // tr,),
        in_specs=[pl.BlockSpec((tr, d), lambda i: (i, 0)), pl.BlockSpec((1, d), lambda i: (0, 0))],
        out_specs=pl.BlockSpec((tr, d), lambda i: (i, 0)),
        compiler_params=_cparams(("parallel",)),
    )(x, g.reshape(1, d))


def _loss_head(x, g, target):
    s, d = x.shape
    tr = min(512, s)

    def body(x_ref, g_ref, t_ref, loss_ref, dx_ref, dxb_ref, dg_ref):
        i = pl.program_id(0)
        xv = x_ref[...]
        rs = lax.rsqrt(jnp.mean(xv * xv, axis=-1, keepdims=True) + NORM_EPS)
        n = xv * rs
        gv = g_ref[...]
        err = n * gv - t_ref[...]
        row_loss = jnp.mean(err * err, axis=-1, keepdims=True)
        loss_part = 0.5 * jnp.sum(row_loss, axis=0, keepdims=True)
        dy = err * (1.0 / d)
        dn = dy * gv
        dx = rs * (dn - n * jnp.mean(dn * n, axis=-1, keepdims=True))
        dx_ref[...] = dx
        dxb_ref[...] = dx.astype(BF16)
        dg_part = jnp.sum(dy * n, axis=0, keepdims=True)

        @pl.when(i == 0)
        def _():
            loss_ref[...] = loss_part
            dg_ref[...] = dg_part

        @pl.when(i > 0)
        def _():
            loss_ref[...] += loss_part
            dg_ref[...] += dg_part

    row = pl.BlockSpec((tr, d), lambda i: (i, 0))
    vec = pl.BlockSpec((1, d), lambda i: (0, 0))
    one = pl.BlockSpec((1, 1), lambda i: (0, 0))
    return pl.pallas_call(
        body, name="loss_head",
        out_shape=(jax.ShapeDtypeStruct((1, 1), F32), jax.ShapeDtypeStruct((s, d), F32),
                   jax.ShapeDtypeStruct((s, d), BF16), jax.ShapeDtypeStruct((1, d), F32)),
        grid=(s // tr,),
        in_specs=[row, vec, row],
        out_specs=(one, row, row, vec),
        compiler_params=_cparams(("arbitrary",)),
    )(x, g.reshape(1, d), target)


def _conv_taps(u_prev, u, w_ref, rows):
    ext = jnp.concatenate([u_prev, u], axis=0)
    u1 = pltpu.roll(ext, 1, 0)[HALO:HALO + rows]
    u2 = pltpu.roll(ext, 2, 0)[HALO:HALO + rows]
    conv = w_ref[0:1, :] * u2 + w_ref[1:2, :] * u1 + w_ref[2:3, :] * u + w_ref[3:4, :]
    return conv, u1, u2


def _conv_fwd(proj, conv_wb, *, s):
    tr = min(512, s)
    c = CONV_CH
    hb = tr // HALO

    def body(gbi_ref, gci_ref, cx_ref, gci_h_ref, cx_h_ref, w_ref, o_ref):
        i = pl.program_id(0)
        u = gci_ref[...].astype(F32) * cx_ref[...].astype(F32)
        u_prev = gci_h_ref[...].astype(F32) * cx_h_ref[...].astype(F32)
        u_prev = jnp.where(i == 0, 0.0, u_prev)
        conv, _, _ = _conv_taps(u_prev, u, w_ref, tr)
        o_ref[...] = (gbi_ref[...].astype(F32) * conv).astype(BF16)

    def seg(col):
        return pl.BlockSpec((tr, c), lambda i: (i, col // c))

    def halo(col):
        return pl.BlockSpec((HALO, c), lambda i: (jnp.maximum(i * hb - 1, 0), col // c))

    return pl.pallas_call(
        body, name="conv_fwd",
        out_shape=jax.ShapeDtypeStruct((s, c), BF16),
        grid=(s // tr,),
        in_specs=[seg(COL_GBI), seg(COL_GCI), seg(COL_CX), halo(COL_GCI), halo(COL_CX),
                  pl.BlockSpec((8, c), lambda i: (0, 0))],
        out_specs=pl.BlockSpec((tr, c), lambda i: (i, 0)),
        compiler_params=_cparams(("parallel",)),
    )(proj, proj, proj, proj, proj, conv_wb)


def _conv_bwd(proj, dcb, conv_wb, dproj, *, s):
    tr = min(512, s)
    c = CONV_CH
    hb = tr // HALO
    nb = s // tr

    def body(gbi_ref, gci_ref, cx_ref, dcb_ref, gci_h_ref, cx_h_ref, gbi_n_ref, dcb_n_ref, w_ref,
             dproj_in_ref, dp_ref, dw_ref):
        i = pl.program_id(0)
        gbi = gbi_ref[...].astype(F32)
        gci = gci_ref[...].astype(F32)
        cx = cx_ref[...].astype(F32)
        dcb_v = dcb_ref[...].astype(F32)
        u = gci * cx
        u_prev = jnp.where(i == 0, 0.0, gci_h_ref[...].astype(F32) * cx_h_ref[...].astype(F32))
        conv, u1, u2 = _conv_taps(u_prev, u, w_ref, tr)
        dconv = dcb_v * gbi
        dconv_next = jnp.where(i == nb - 1, 0.0, dcb_n_ref[...].astype(F32) * gbi_n_ref[...].astype(F32))
        ext = jnp.concatenate([dconv, dconv_next], axis=0)
        n_ext = tr + HALO
        d1 = pltpu.roll(ext, n_ext - 1, 0)[0:tr]
        d2 = pltpu.roll(ext, n_ext - 2, 0)[0:tr]
        du = w_ref[2:3, :] * dconv + w_ref[1:2, :] * d1 + w_ref[0:1, :] * d2
        dp_ref[:, 0:c] = (dcb_v * conv).astype(BF16)
        dp_ref[:, c:2 * c] = (du * cx).astype(BF16)
        dp_ref[:, 2 * c:3 * c] = (du * gci).astype(BF16)
        part = jnp.concatenate([
            jnp.sum(dconv * u2, axis=0, keepdims=True),
            jnp.sum(dconv * u1, axis=0, keepdims=True),
            jnp.sum(dconv * u, axis=0, keepdims=True),
            jnp.sum(dconv, axis=0, keepdims=True),
            jnp.zeros((4, c), F32)], axis=0)

        @pl.when(i == 0)
        def _():
            dw_ref[...] = part

        @pl.when(i > 0)
        def _():
            dw_ref[...] += part

    def seg(col):
        return pl.BlockSpec((tr, c), lambda i: (i, col // c))

    def halo_prev(col):
        return pl.BlockSpec((HALO, c), lambda i: (jnp.maximum(i * hb - 1, 0), col // c))

    def halo_next(col):
        return pl.BlockSpec((HALO, c), lambda i: (jnp.minimum((i + 1) * hb, nb * hb - 1), col // c))

    return pl.pallas_call(
        body, name="conv_bwd",
        out_shape=(jax.ShapeDtypeStruct((s, PROJ_W), BF16), jax.ShapeDtypeStruct((8, c), F32)),
        grid=(nb,),
        in_specs=[seg(COL_GBI), seg(COL_GCI), seg(COL_CX), pl.BlockSpec((tr, c), lambda i: (i, 0)),
                  halo_prev(COL_GCI), halo_prev(COL_CX), halo_next(COL_GBI),
                  pl.BlockSpec((HALO, c), lambda i: (jnp.minimum((i + 1) * hb, nb * hb - 1), 0)),
                  pl.BlockSpec((8, c), lambda i: (0, 0)), ANY],
        out_specs=(pl.BlockSpec((tr, 3 * c), lambda i: (i, COL_GBI // (3 * c))),
                   pl.BlockSpec((8, c), lambda i: (0, 0))),
        input_output_aliases={9: 0},
        compiler_params=_cparams(("arbitrary",)),
    )(proj, proj, proj, dcb, proj, proj, proj, dcb, conv_wb, dproj)


def _token_spec(dep):
    return ([], []) if dep is None else ([dep], [pl.BlockSpec(dep.shape, lambda i: (0, 0))])


def _mixer_out_fwd(oa, cb_in, proj, x0, w_oa, w_ob, w_o, *, s, dep=None):
    tr = min(512, s)
    d = D_MODEL
    n_dep = dep is not None

    def body(*refs):
        oa_ref, cb_ref, ga_ref, gb_ref, x0_ref, woa_ref, wob_ref, wo_ref, ya_ref, yb_ref, mix_ref, x1_ref = (
            refs[n_dep:])
        ya = jnp.dot(oa_ref[...], woa_ref[...], preferred_element_type=F32).astype(BF16)
        yb = jnp.dot(cb_ref[...], wob_ref[...], preferred_element_type=F32).astype(BF16)
        ya_ref[...] = ya
        yb_ref[...] = yb
        sa = _sigmoid(ga_ref[...].astype(F32))
        sb = _sigmoid(gb_ref[...].astype(F32))
        mix = (sa * ya.astype(F32) + sb * yb.astype(F32)).astype(BF16)
        mix_ref[...] = mix
        x1_ref[...] = x0_ref[...] + jnp.dot(mix, wo_ref[...], preferred_element_type=F32)

    row = pl.BlockSpec((tr, d), lambda i: (i, 0))
    full = pl.BlockSpec((d, d), lambda i: (0, 0))
    tok, tok_specs = _token_spec(dep)
    return pl.pallas_call(
        body, name="mixer_out_fwd",
        out_shape=(jax.ShapeDtypeStruct((s, d), BF16), jax.ShapeDtypeStruct((s, d), BF16),
                   jax.ShapeDtypeStruct((s, d), BF16), jax.ShapeDtypeStruct((s, d), F32)),
        grid=(s // tr,),
        in_specs=tok_specs + [row, row, pl.BlockSpec((tr, d), lambda i: (i, COL_GA // d)),
                              pl.BlockSpec((tr, d), lambda i: (i, COL_GB // d)), row, full, full, full],
        out_specs=(row, row, row, row),
        compiler_params=_cparams(("parallel",)),
    )(*tok, oa, cb_in, proj, proj, x0, w_oa, w_ob, w_o)


def _mixer_out_bwd(dx1b, proj, ya, yb, w_oa, w_ob, w_o, *, s, dep=None):
    tr = min(512, s)
    d = D_MODEL
    n_dep = dep is not None

    def body(*refs):
        (dx_ref, ga_ref, gb_ref, ya_ref, yb_ref, woa_ref, wob_ref, wo_ref,
         dya_ref, dyb_ref, dcb_ref, doa_ref, dg_ref) = refs[n_dep:]
        dm = lax.dot_general(dx_ref[...], wo_ref[...], _DOT_DIMS["nt"], preferred_element_type=F32)
        dm = dm.astype(BF16).astype(F32)
        sa = _sigmoid(ga_ref[...].astype(F32))
        sb = _sigmoid(gb_ref[...].astype(F32))
        dya = (dm * sa).astype(BF16)
        dyb = (dm * sb).astype(BF16)
        dya_ref[...] = dya
        dyb_ref[...] = dyb
        dg_ref[:, 0:d] = (dm * ya_ref[...].astype(F32) * sa * (1.0 - sa)).astype(BF16)
        dg_ref[:, d:2 * d] = (dm * yb_ref[...].astype(F32) * sb * (1.0 - sb)).astype(BF16)
        dcb_ref[...] = lax.dot_general(dyb, wob_ref[...], _DOT_DIMS["nt"],
                                       preferred_element_type=F32).astype(BF16)
        doa_ref[...] = lax.dot_general(dya, woa_ref[...], _DOT_DIMS["nt"],
                                       preferred_element_type=F32).astype(BF16)

    row = pl.BlockSpec((tr, d), lambda i: (i, 0))
    full = pl.BlockSpec((d, d), lambda i: (0, 0))
    tok, tok_specs = _token_spec(dep)
    return pl.pallas_call(
        body, name="mixer_out_bwd",
        out_shape=tuple(jax.ShapeDtypeStruct((s, d), BF16) for _ in range(4))
        + (jax.ShapeDtypeStruct((s, PROJ_W), BF16),),
        grid=(s // tr,),
        in_specs=tok_specs + [row, pl.BlockSpec((tr, d), lambda i: (i, COL_GA // d)),
                              pl.BlockSpec((tr, d), lambda i: (i, COL_GB // d)), row, row, full, full, full],
        out_specs=(row, row, row, row, pl.BlockSpec((tr, 2 * d), lambda i: (i, COL_GA // (2 * d)))),
        compiler_params=_cparams(("parallel",)),
    )(*tok, dx1b, proj, proj, ya, yb, w_oa, w_ob, w_o)


def _ffn_out_bwd(dx2b, w_down, gt, up, *, s):
    tr = min(256, s)
    d, f = D_MODEL, FFN_HIDDEN

    def body(dx_ref, wd_ref, gt_ref, up_ref, dgt_ref, dup_ref):
        dh = lax.dot_general(dx_ref[...], wd_ref[...], _DOT_DIMS["nt"], preferred_element_type=F32)
        gv = gt_ref[...].astype(F32)
        uv = up_ref[...].astype(F32)
        sg = _sigmoid(gv)
        dgt_ref[...] = (dh * uv * sg * (1.0 + gv * (1.0 - sg))).astype(BF16)
        dup_ref[...] = (dh * gv * sg).astype(BF16)

    row = pl.BlockSpec((tr, f), lambda i: (i, 0))
    return pl.pallas_call(
        body, name="ffn_out_bwd",
        out_shape=(jax.ShapeDtypeStruct((s, f), BF16), jax.ShapeDtypeStruct((s, f), BF16)),
        grid=(s // tr,),
        in_specs=[pl.BlockSpec((tr, d), lambda i: (i, 0)), pl.BlockSpec((f, d), lambda i: (0, 0)), row, row],
        out_specs=(row, row),
        compiler_params=_cparams(("parallel",)),
    )(dx2b, w_down, gt, up)


def _tri(strict):
    r = lax.broadcasted_iota(jnp.int32, (CHUNK, CHUNK), 0)
    c = lax.broadcasted_iota(jnp.int32, (CHUNK, CHUNK), 1)
    return jnp.where((c < r) if strict else (c <= r), 1.0, 0.0).astype(F32)


def _gla_gate_terms(la_c, tri):
    cum = jnp.dot(tri, la_c, precision=lax.Precision.HIGHEST, preferred_element_type=F32)
    cend = cum[CHUNK - 1:CHUNK, :]
    return jnp.exp(cend - cum), jnp.exp(cend)


def _hk(h):
    return slice(h * HEAD_K, (h + 1) * HEAD_K)


def _hv(h):
    return slice(h * HEAD_V, (h + 1) * HEAD_V)


def _gla_fwd(proj, wfg2p, bfg, gn, *, s):
    tr = min(512, s)
    nb = s // tr
    nc = tr // CHUNK

    def body(q_ref, k_ref, v_ref, r_ref, fz_ref, w_ref, b_ref, gn_ref, oa_ref, st_ref,
             state, la_scr, o_scr):
        i = pl.program_id(0)

        @pl.when(i == 0)
        def _():
            state[...] = jnp.zeros_like(state)

        fg = jnp.dot(fz_ref[...], w_ref[...], preferred_element_type=F32) + b_ref[...]
        la_scr[...] = _log_sigmoid(fg) * (1.0 / GATE_TAU)
        tri = _tri(False)

        def chunk(ci, carry):
            r0 = pl.multiple_of(ci * CHUNK, CHUNK)
            rows = pl.ds(r0, CHUNK)
            e, gam = _gla_gate_terms(la_scr[rows, :], tri)
            kd = (k_ref[rows, :].astype(F32) * e).astype(BF16)
            qs = (q_ref[rows, :].astype(F32) * Q_SCALE).astype(BF16)
            v_c = v_ref[rows, :]
            for h in range(GLA_HEADS):
                upd = lax.dot_general(v_c[:, _hv(h)], kd[:, _hk(h)], _DOT_DIMS["tn"],
                                      preferred_element_type=F32)
                st_h = state[:, _hk(h)] * gam[:, _hk(h)] + upd
                state[:, _hk(h)] = st_h
                o_scr[rows, _hv(h)] = lax.dot_general(qs[:, _hk(h)], st_h.astype(BF16), _DOT_DIMS["nt"],
                                                       preferred_element_type=F32)
            st_ref[ci] = state[...].astype(BF16)
            return carry

        lax.fori_loop(0, nc, chunk, 0, unroll=True)
        for h in range(GLA_HEADS):
            o = o_scr[:, _hv(h)]
            rs = lax.rsqrt(jnp.mean(o * o, axis=-1, keepdims=True) + NORM_EPS)
            rv = r_ref[:, _hv(h)].astype(F32)
            oa_ref[:, _hv(h)] = ((o * rs * gn_ref[...]).astype(F32) * (rv * _sigmoid(rv))).astype(BF16)

    return pl.pallas_call(
        body, name="gla_fwd",
        out_shape=(jax.ShapeDtypeStruct((s, VAL_WIDTH), BF16),
                   jax.ShapeDtypeStruct((s // CHUNK, HEAD_V, KEY_WIDTH), BF16)),
        grid=(nb,),
        in_specs=[pl.BlockSpec((tr, KEY_WIDTH), lambda i: (i, COL_Q // KEY_WIDTH)),
                  pl.BlockSpec((tr, KEY_WIDTH), lambda i: (i, COL_K // KEY_WIDTH)),
                  pl.BlockSpec((tr, VAL_WIDTH), lambda i: (i, COL_V // VAL_WIDTH)),
                  pl.BlockSpec((tr, VAL_WIDTH), lambda i: (i, COL_R // VAL_WIDTH)),
                  pl.BlockSpec((tr, FZ_PAD), lambda i: (i, COL_FZ // FZ_PAD)),
                  pl.BlockSpec((FZ_PAD, KEY_WIDTH), lambda i: (0, 0)),
                  pl.BlockSpec((1, KEY_WIDTH), lambda i: (0, 0)),
                  pl.BlockSpec((1, HEAD_V), lambda i: (0, 0))],
        out_specs=(pl.BlockSpec((tr, VAL_WIDTH), lambda i: (i, 0)),
                   pl.BlockSpec((nc, HEAD_V, KEY_WIDTH), lambda i: (i, 0, 0))),
        scratch_shapes=[pltpu.VMEM((HEAD_V, KEY_WIDTH), F32), pltpu.VMEM((tr, KEY_WIDTH), F32),
                        pltpu.VMEM((tr, VAL_WIDTH), F32)],
        compiler_params=_cparams(("arbitrary",)),
    )(proj, proj, proj, proj, proj, wfg2p, bfg, gn)


def _gla_bwd(proj, doa, states, wfg2p, bfg, gn, dproj, *, s):
    tr = min(512, s)
    nb = s // tr
    nc = tr // CHUNK

    def body(q_ref, k_ref, v_ref, r_ref, fz_ref, doa_ref, st_ref, stp_ref, w_ref, b_ref, gn_ref, dproj_in_ref,
             dp_ref, dfz_ref, dgn_ref, dw_ref, db_ref,
             carry, fg_scr, la_scr, dla_scr, o_scr, do_scr):
        i = pl.program_id(0)

        @pl.when(i == 0)
        def _():
            carry[...] = jnp.zeros_like(carry)
            dgn_ref[...] = jnp.zeros_like(dgn_ref)
            dw_ref[...] = jnp.zeros_like(dw_ref)
            db_ref[...] = jnp.zeros_like(db_ref)

        fz = fz_ref[...]
        fg = jnp.dot(fz, w_ref[...], preferred_element_type=F32) + b_ref[...]
        fg_scr[...] = fg
        la_scr[...] = _log_sigmoid(fg) * (1.0 / GATE_TAU)

        def fwd_chunk(ci, c0):
            r0 = pl.multiple_of(ci * CHUNK, CHUNK)
            rows = pl.ds(r0, CHUNK)
            qs = (q_ref[rows, :].astype(F32) * Q_SCALE).astype(BF16)
            st = st_ref[ci].astype(BF16)
            for h in range(GLA_HEADS):
                o_scr[rows, _hv(h)] = lax.dot_general(qs[:, _hk(h)], st[:, _hk(h)], _DOT_DIMS["nt"],
                                                       preferred_element_type=F32)
            return c0

        lax.fori_loop(0, nc, fwd_chunk, 0, unroll=True)

        gnv = gn_ref[...]
        dgn_part = jnp.zeros((1, HEAD_V), F32)
        for h in range(GLA_HEADS):
            o = o_scr[:, _hv(h)]
            rs = lax.rsqrt(jnp.mean(o * o, axis=-1, keepdims=True) + NORM_EPS)
            nrm = o * rs
            rv = r_ref[:, _hv(h)].astype(F32)
            sg = _sigmoid(rv)
            doa_h = doa_ref[:, _hv(h)].astype(F32)
            don = doa_h * (rv * sg)
            dp_ref[:, COL_R + h * HEAD_V:COL_R + (h + 1) * HEAD_V] = (
                doa_h * (nrm * gnv) * (sg * (1.0 + rv * (1.0 - sg)))).astype(BF16)
            dgn_part = dgn_part + jnp.sum(don * nrm, axis=0, keepdims=True)
            dn = don * gnv
            do_scr[:, _hv(h)] = (rs * (dn - nrm * jnp.mean(dn * nrm, axis=-1, keepdims=True))).astype(BF16)
        dgn_ref[...] += dgn_part

        tri = _tri(False)
        tri_s = _tri(True)
        first_block = i == nb - 1

        def bwd_chunk(cc, c0):
            ci = nc - 1 - cc
            r0 = pl.multiple_of(ci * CHUNK, CHUNK)
            rows = pl.ds(r0, CHUNK)
            e, gam = _gla_gate_terms(la_scr[rows, :], tri)
            k_c = k_ref[rows, :].astype(F32)
            kd = k_c * e
            kd_b = kd.astype(BF16)
            qs = (q_ref[rows, :].astype(F32) * Q_SCALE).astype(BF16)
            v_c = v_ref[rows, :]
            do_c = do_scr[rows, :]
            st_b = st_ref[ci]
            st_prev_in = st_ref[jnp.maximum(ci - 1, 0)].astype(F32)
            st_prev_edge = jnp.where(first_block, 0.0, stp_ref[0].astype(F32))
            st_prev = jnp.where(ci > 0, st_prev_in, st_prev_edge)
            dkd_parts = []
            dgam_parts = []
            for h in range(GLA_HEADS):
                dst = lax.dot_general(do_c[:, _hv(h)], qs[:, _hk(h)], _DOT_DIMS["tn"],
                                      preferred_element_type=F32) + carry[:, _hk(h)]
                dst_b = dst.astype(BF16)
                dqs = jnp.dot(do_c[:, _hv(h)], st_b[:, _hk(h)], preferred_element_type=F32)
                dp_ref[rows, COL_Q + h * HEAD_K:COL_Q + (h + 1) * HEAD_K] = (dqs * Q_SCALE).astype(BF16)
                dkd_parts.append(jnp.dot(v_c[:, _hv(h)], dst_b, preferred_element_type=F32))
                dp_ref[rows, COL_V + h * HEAD_V:COL_V + (h + 1) * HEAD_V] = lax.dot_general(
                    kd_b[:, _hk(h)], dst_b, _DOT_DIMS["nt"], preferred_element_type=F32).astype(BF16)
                dgam_parts.append(jnp.sum(dst * st_prev[:, _hk(h)], axis=0, keepdims=True))
                carry[:, _hk(h)] = dst * gam[:, _hk(h)]
            dkd = jnp.concatenate(dkd_parts, axis=1)
            dgam = jnp.concatenate(dgam_parts, axis=1)
            dp_ref[rows, COL_K:COL_K + KEY_WIDTH] = (dkd * e).astype(BF16)
            dz = dkd * kd
            dla_scr[rows, :] = gam * dgam + jnp.dot(tri_s, dz, precision=lax.Precision.HIGHEST,
                                                    preferred_element_type=F32)
            return c0

        lax.fori_loop(0, nc, bwd_chunk, 0, unroll=True)

        dfg = dla_scr[...] * (1.0 / GATE_TAU) * _sigmoid(-fg_scr[...])
        dfg_b = dfg.astype(BF16)
        dfz_ref[...] = lax.dot_general(dfg_b, w_ref[...], _DOT_DIMS["nt"],
                                       preferred_element_type=F32).astype(BF16)
        dw_ref[...] += lax.dot_general(fz, dfg_b, _DOT_DIMS["tn"], preferred_element_type=F32)
        db_ref[...] += jnp.sum(dfg, axis=0, keepdims=True)

    def rev(i):
        return nb - 1 - i

    return pl.pallas_call(
        body, name="gla_bwd",
        out_shape=(jax.ShapeDtypeStruct((s, PROJ_W), BF16),
                   jax.ShapeDtypeStruct((s, FZ_PAD), BF16),
                   jax.ShapeDtypeStruct((1, HEAD_V), F32),
                   jax.ShapeDtypeStruct((FZ_PAD, KEY_WIDTH), F32),
                   jax.ShapeDtypeStruct((1, KEY_WIDTH), F32)),
        grid=(nb,),
        in_specs=[pl.BlockSpec((tr, KEY_WIDTH), lambda i: (rev(i), COL_Q // KEY_WIDTH)),
                  pl.BlockSpec((tr, KEY_WIDTH), lambda i: (rev(i), COL_K // KEY_WIDTH)),
                  pl.BlockSpec((tr, VAL_WIDTH), lambda i: (rev(i), COL_V // VAL_WIDTH)),
                  pl.BlockSpec((tr, VAL_WIDTH), lambda i: (rev(i), COL_R // VAL_WIDTH)),
                  pl.BlockSpec((tr, FZ_PAD), lambda i: (rev(i), COL_FZ // FZ_PAD)),
                  pl.BlockSpec((tr, VAL_WIDTH), lambda i: (rev(i), 0)),
                  pl.BlockSpec((nc, HEAD_V, KEY_WIDTH), lambda i: (rev(i), 0, 0)),
                  pl.BlockSpec((1, HEAD_V, KEY_WIDTH), lambda i: (jnp.maximum(rev(i) * nc - 1, 0), 0, 0)),
                  pl.BlockSpec((FZ_PAD, KEY_WIDTH), lambda i: (0, 0)),
                  pl.BlockSpec((1, KEY_WIDTH), lambda i: (0, 0)),
                  pl.BlockSpec((1, HEAD_V), lambda i: (0, 0)), ANY],
        out_specs=(pl.BlockSpec((tr, 3 * VAL_WIDTH), lambda i: (rev(i), 0)),
                   pl.BlockSpec((tr, FZ_PAD), lambda i: (rev(i), 0)),
                   pl.BlockSpec((1, HEAD_V), lambda i: (0, 0)),
                   pl.BlockSpec((FZ_PAD, KEY_WIDTH), lambda i: (0, 0)),
                   pl.BlockSpec((1, KEY_WIDTH), lambda i: (0, 0))),
        scratch_shapes=[pltpu.VMEM((HEAD_V, KEY_WIDTH), F32),
                        pltpu.VMEM((tr, KEY_WIDTH), F32),
                        pltpu.VMEM((tr, KEY_WIDTH), F32),
                        pltpu.VMEM((tr, KEY_WIDTH), F32),
                        pltpu.VMEM((tr, VAL_WIDTH), F32),
                        pltpu.VMEM((tr, VAL_WIDTH), BF16)],
        input_output_aliases={11: 0},
        compiler_params=_cparams(("arbitrary",)),
    )(proj, proj, proj, proj, proj, doa, states, states, wfg2p, bfg, gn, dproj)


def _put_fz(dproj, dfz, *, s):
    tr = min(512, s)

    def body(dfz_ref, dproj_in_ref, o_ref):
        o_ref[...] = dfz_ref[...]

    return pl.pallas_call(
        body, name="put_dfz",
        out_shape=jax.ShapeDtypeStruct((s, PROJ_W), BF16),
        grid=(s // tr,),
        in_specs=[pl.BlockSpec((tr, FZ_PAD), lambda i: (i, 0)), ANY],
        out_specs=pl.BlockSpec((tr, FZ_PAD), lambda i: (i, COL_FZ // FZ_PAD)),
        input_output_aliases={1: 0},
        compiler_params=_cparams(("parallel",)),
    )(dfz, dproj)


_ADAM_C1 = 1.0 / (1.0 - ADAM_B1 ** ADAM_STEP)
_ADAM_C2 = 1.0 / (1.0 - ADAM_B2 ** ADAM_STEP)


def _adamw_math(wv, gv, mv, vv):
    nm = ADAM_B1 * mv + (1.0 - ADAM_B1) * gv
    nv = ADAM_B2 * vv + (1.0 - ADAM_B2) * (gv * gv)
    delta = -ADAM_LR * ((nm * _ADAM_C1) / (jnp.sqrt(nv * _ADAM_C2) + ADAM_EPS) + ADAM_WD * wv)
    return delta, nm, nv


def _adamw(w, g, m, v, *, name):
    shape = w.shape
    if w.ndim == 1:
        w, g, m, v = (t.reshape(1, 1, -1) for t in (w, g, m, v))
    elif w.ndim == 2:
        w, g, m, v = (t.reshape((1,) + t.shape) for t in (w, g, m, v))
    l, a, b = w.shape

    def body(w_ref, g_ref, m_ref, v_ref, d_ref, nm_ref, nv_ref):
        d_ref[...], nm_ref[...], nv_ref[...] = _adamw_math(w_ref[...], g_ref[...], m_ref[...], v_ref[...])

    blk = pl.BlockSpec((1, a, b), lambda li: (li, 0, 0))
    outs = pl.pallas_call(
        body, name=name,
        out_shape=tuple(jax.ShapeDtypeStruct((l, a, b), F32) for _ in range(3)),
        grid=(l,),
        in_specs=[blk, blk, blk, blk],
        out_specs=(blk, blk, blk),
        compiler_params=_cparams(("parallel",)),
    )(w, g, m, v)
    return tuple(o.reshape(shape) for o in outs)


def _adamw_layers(w, reduced, received, on_core0, m, v, *, name):
    l, a, b = w.shape
    assert l == DEPTH == 2
    ta = _row_tile(a, F32_SUBLANES, MAX_ADAMW_ROWS)

    def body(flag_ref, w_ref, r0_ref, o0_ref, r1_ref, o1_ref, m_ref, v_ref, g_ref, d_ref, nm_ref, nv_ref):
        core0 = flag_ref[...] > 0.5
        g0 = jnp.where(core0, r0_ref[...], o0_ref[...])
        g1 = jnp.where(core0, o1_ref[...], r1_ref[...])
        gv = jnp.where(pl.program_id(0) == 0, g0, g1)
        g_ref[0] = gv
        d_ref[0], nm_ref[0], nv_ref[0] = _adamw_math(w_ref[0], gv, m_ref[0], v_ref[0])

    blk = pl.BlockSpec((1, ta, b), lambda li, ai: (li, ai, 0))

    def grad_block(layer):
        return pl.BlockSpec((ta, b), lambda li, ai: (jnp.where(li == layer, ai, 0), 0))

    g0blk, g1blk = grad_block(0), grad_block(1)
    return pl.pallas_call(
        body, name=name,
        out_shape=tuple(jax.ShapeDtypeStruct((l, a, b), F32) for _ in range(4)),
        grid=(l, a // ta),
        in_specs=[pl.BlockSpec((1, 1), lambda li, ai: (0, 0)), blk, g0blk, g0blk, g1blk, g1blk, blk, blk],
        out_specs=(blk, blk, blk, blk),
        compiler_params=_cparams(("parallel", "parallel")),
    )(on_core0, w, reduced[0], received[0], reduced[1], received[1], m, v)


MESH_ID = pl.DeviceIdType.MESH
ANY = pl.BlockSpec(memory_space=pl.ANY)


def _position():
    x, y, c = lax.axis_index("x"), lax.axis_index("y"), lax.axis_index("c")
    chips = [(1 - x, y), (x, 1 - y), (1 - x, 1 - y)]
    return x, y, c, chips


def _chip_index(xy):
    return 2 * xy[0] + xy[1]


HBM = pl.BlockSpec(memory_space=pltpu.HBM)
SEM = pl.BlockSpec(memory_space=pltpu.SEMAPHORE)
DATAFLOW_EFFECT = pltpu.SideEffectType.DATAFLOW_SIDE_EFFECTING
TOKEN_SHAPE = (8, LANE)


def _landing(shape, dtype):
    return pltpu.with_memory_space_constraint(lax.empty(shape, dtype), pltpu.HBM)


def _split_start(bufs, sem_shape, issue, *, name, after=None):
    n = len(bufs)
    n_in = n + (after is not None)

    def body(*refs):
        issue(refs[:n], refs[n_in], refs[n_in + 1])
        token = refs[-1]
        token[...] = jnp.zeros_like(token)

    operands = [pltpu.with_memory_space_constraint(t, pltpu.HBM) for t in bufs]
    outs = pl.pallas_call(
        body, name=name,
        out_shape=(pltpu.SemaphoreType.DMA(sem_shape), pltpu.SemaphoreType.DMA(sem_shape),
                   *[pltpu.HBM(t.shape, t.dtype) for t in bufs], jax.ShapeDtypeStruct(TOKEN_SHAPE, F32)),
        in_specs=[HBM] * n + [ANY] * (after is not None),
        out_specs=(SEM, SEM, *[HBM] * n, pl.BlockSpec(memory_space=pltpu.VMEM)),
        input_output_aliases={i: 2 + i for i in range(n)},
        compiler_params=pltpu.CompilerParams(has_side_effects=DATAFLOW_EFFECT),
    )(*operands, *([after] if after is not None else []))
    return outs[0], outs[1], list(outs[2:2 + n]), outs[-1]


def _split_wait(started, after, settle, *, name):
    send_sems, recv_sems, bufs, _ = started
    n = len(bufs)
    afters = tuple(after) if isinstance(after, (tuple, list)) else (after,)

    def body(*refs):
        settle(refs[:n], refs[n], refs[n + 1])

    outs = pl.pallas_call(
        body, name=name,
        out_shape=tuple(pltpu.HBM(t.shape, t.dtype) for t in bufs),
        in_specs=[HBM] * n + [SEM, SEM] + [ANY] * len(afters),
        out_specs=tuple([HBM] * n),
        input_output_aliases={i: i for i in range(n)},
        compiler_params=pltpu.CompilerParams(has_side_effects=DATAFLOW_EFFECT),
    )(*bufs, send_sems, recv_sems, *afters)
    return list(outs)


def _to_sibling(bufs, n, send_sems, recv_sems):
    x, y, c, _ = _position()
    return [pltpu.make_async_remote_copy(
        src_ref=bufs[a], dst_ref=bufs[n + a], send_sem=send_sems.at[a], recv_sem=recv_sems.at[a],
        device_id=(x, y, 1 - c), device_id_type=MESH_ID) for a in range(n)]


def _sibling_push_start(layer, sender_is_reducer, arrays, *, name):
    n = len(arrays)
    sender = layer if sender_is_reducer else 1 - layer

    def issue(bufs, send_sems, recv_sems):
        @pl.when(lax.axis_index("c") == sender)
        def _():
            for cp in _to_sibling(bufs, n, send_sems, recv_sems):
                cp.start()

    lands = [_landing(t.shape, t.dtype) for t in arrays]
    return _split_start(list(arrays) + lands, (n,), issue, name=name)


def _sibling_push_wait(layer, sender_is_reducer, started, after, *, name):
    n = len(started[2]) // 2
    sender = layer if sender_is_reducer else 1 - layer

    def settle(bufs, send_sems, recv_sems):
        c = lax.axis_index("c")

        @pl.when(c == sender)
        def _():
            for cp in _to_sibling(bufs, n, send_sems, recv_sems):
                cp.wait_send()

        @pl.when(c != sender)
        def _():
            for cp in _to_sibling(bufs, n, send_sems, recv_sems):
                cp.wait_recv()

    outs = _split_wait(started, after, settle, name=name)
    return outs[:n], outs[n:]


def _chip_copies(bufs, n, send_sems, recv_sems):
    x, y, c, chips = _position()
    return [pltpu.make_async_remote_copy(
        src_ref=bufs[a].at[_chip_index(chips[k])], dst_ref=bufs[n + a].at[k],
        send_sem=send_sems.at[3 * a + k], recv_sem=recv_sems.at[3 * a + k],
        device_id=(chips[k][0], chips[k][1], c), device_id_type=MESH_ID)
        for a in range(n) for k in range(3)]


def _chip_exchange_start(layer, parts, *, name):
    n = len(parts)

    def issue(bufs, send_sems, recv_sems):
        @pl.when(lax.axis_index("c") == layer)
        def _():
            for cp in _chip_copies(bufs, n, send_sems, recv_sems):
                cp.start()

    lands = [_landing((3,) + t.shape[1:], t.dtype) for t in parts]
    return _split_start(list(parts) + lands, (3 * n,), issue, name=name)


def _chip_exchange_wait(layer, started, after, *, name):
    n = len(started[2]) // 2

    def settle(bufs, send_sems, recv_sems):
        @pl.when(lax.axis_index("c") == layer)
        def _():
            for cp in _chip_copies(bufs, n, send_sems, recv_sems):
                cp.wait()

    outs = _split_wait(started, after, settle, name=name)
    return outs[:n], outs[n:]


def _gather_copies(layer, bufs, n, send_sems, recv_sems, arriving):
    x, y, c, chips = _position()
    me = _chip_index((x, y))
    return [pltpu.make_async_remote_copy(
        src_ref=bufs[a], dst_ref=bufs[n + a].at[_chip_index(chips[k]) if arriving else me],
        send_sem=send_sems.at[3 * a + k], recv_sem=recv_sems.at[3 * a + k],
        device_id=(chips[k][0], chips[k][1], c), device_id_type=MESH_ID)
        for a in range(n) for k in range(3)]


def _gather_start(layer, shards, *, name, after=None):
    n = len(shards)

    def issue(bufs, send_sems, recv_sems):
        @pl.when(lax.axis_index("c") == layer)
        def _():
            for cp in _gather_copies(layer, bufs, n, send_sems, recv_sems, False):
                cp.start()

    lands = [_landing((N_CHIPS,) + t.shape, t.dtype) for t in shards]
    return _split_start(list(shards) + lands, (3 * n,), issue, name=name, after=after)


def _gather_wait(layer, started, after, *, name):
    n = len(started[2]) // 2

    def settle(bufs, send_sems, recv_sems):
        @pl.when(lax.axis_index("c") == layer)
        def _():
            for cp in _gather_copies(layer, bufs, n, send_sems, recv_sems, False):
                cp.wait_send()
            for cp in _gather_copies(layer, bufs, n, send_sems, recv_sems, True):
                cp.wait_recv()

    return _split_wait(started, after, settle, name=name)[n:]


def _handoff_copies(bufs, n, send_sems, recv_sems):
    x, y, c, chips = _position()
    out = []
    for a in range(n):
        for k in range(3):
            slot = bufs[a].at[_chip_index(chips[k])]
            out.append(pltpu.make_async_remote_copy(
                src_ref=slot, dst_ref=slot, send_sem=send_sems.at[3 * a + k], recv_sem=recv_sems.at[3 * a + k],
                device_id=(x, y, 1 - c), device_id_type=MESH_ID))
    return out


def _handoff_start(layer, gathered, *, name):
    n = len(gathered)

    def issue(bufs, send_sems, recv_sems):
        @pl.when(lax.axis_index("c") == layer)
        def _():
            for cp in _handoff_copies(bufs, n, send_sems, recv_sems):
                cp.start()

    return _split_start(list(gathered), (3 * n,), issue, name=name)


def _handoff_wait(layer, started, after, *, name):
    n = len(started[2])

    def settle(bufs, send_sems, recv_sems):
        c = lax.axis_index("c")

        @pl.when(c == layer)
        def _():
            for cp in _handoff_copies(bufs, n, send_sems, recv_sems):
                cp.wait_send()

        @pl.when(c != layer)
        def _():
            for cp in _handoff_copies(bufs, n, send_sems, recv_sems):
                cp.wait_recv()

    return _split_wait(started, after, settle, name=name)


F32_SUBLANES = 8
BF16_SUBLANES = 16
MAX_STREAM_ROWS = 704
MAX_ADAMW_ROWS = 384


def _row_tile(a, sublanes=BF16_SUBLANES, max_rows=MAX_STREAM_ROWS):
    best = None
    for cand in range(sublanes, min(a, max_rows) + 1, sublanes):
        if a % cand == 0:
            best = cand
    assert best is not None, a
    return best


def _pair_sum(mine, sib, active, *, name):
    nchip, a, b = sib.shape
    ta = _row_tile(a)

    def body(act_ref, m_ref, s_ref, o_ref):
        @pl.when(act_ref[0] == 1)
        def _():
            o_ref[...] = (m_ref[...].astype(F32) + s_ref[...].astype(F32)).astype(BF16)

    blk = pl.BlockSpec((1, ta, b), lambda j, r, act: (j * act[0], r * act[0], 0))
    return pl.pallas_call(
        body, name=name,
        out_shape=jax.ShapeDtypeStruct((nchip, a, b), BF16),
        grid_spec=pltpu.PrefetchScalarGridSpec(
            num_scalar_prefetch=1, grid=(nchip, a // ta), in_specs=[blk, blk], out_specs=blk),
        compiler_params=_cparams(("arbitrary", "arbitrary")),
    )(active, mine, sib)


def _total_sum(own, recv, active, *, name):
    a, b = own.shape
    ta = _row_tile(a)

    def body(act_ref, o_ref, r_ref, t_ref):
        @pl.when(act_ref[0] == 1)
        def _():
            acc = o_ref[...].astype(F32)
            for k in range(3):
                acc = acc + r_ref[k].astype(F32)
            t_ref[...] = acc

    blk = pl.BlockSpec((ta, b), lambda r, act: (r * act[0], 0))
    return pl.pallas_call(
        body, name=name,
        out_shape=jax.ShapeDtypeStruct((a, b), F32),
        grid_spec=pltpu.PrefetchScalarGridSpec(
            num_scalar_prefetch=1, grid=(a // ta,),
            in_specs=[blk, pl.BlockSpec((3, ta, b), lambda r, act: (0, r * act[0], 0))], out_specs=blk),
        compiler_params=_cparams(("arbitrary",)),
    )(active, own, recv)


def _all_reduce_small(packed):
    rows, width = packed.shape

    def body(x_ref, out_ref, gath, send_sems, recv_sems, local_sem):
        x, y, c, chips = _position()
        me, sibling = (x, y, c), (x, y, 1 - c)

        def slot(px, py, pc):
            return gath.at[4 * px + 2 * py + pc]

        def copy(k, block, to, src=None):
            return pltpu.make_async_remote_copy(
                src_ref=slot(*block) if src is None else src, dst_ref=slot(*block),
                send_sem=send_sems.at[k], recv_sem=recv_sems.at[k], device_id=to, device_id_type=MESH_ID)

        mine = pltpu.make_async_copy(x_ref, slot(*me), local_sem)
        mine.start()
        first = [copy(0, me, sibling, src=x_ref)]
        first += [copy(1 + j, me, (*chip, c), src=x_ref) for j, chip in enumerate(chips)]
        for cp in first:
            cp.start()
        passed = [copy(4 + j, (*chip, c), sibling) for j, chip in enumerate(chips)]
        for j, chip in enumerate(chips):
            copy(1 + j, (*chip, c), me).wait_recv()
            passed[j].start()
        copy(0, sibling, me).wait_recv()
        for j, chip in enumerate(chips):
            copy(4 + j, (*chip, 1 - c), me).wait_recv()
        for cp in first + passed:
            cp.wait_send()
        mine.wait()
        acc = gath[0]
        for d in range(1, N_DEV):
            acc = acc + gath[d]
        out_ref[...] = acc

    return pl.pallas_call(
        body, name="all_reduce_small",
        out_shape=jax.ShapeDtypeStruct((rows, width), F32),
        in_specs=[pl.BlockSpec(memory_space=pltpu.VMEM)],
        out_specs=pl.BlockSpec(memory_space=pltpu.VMEM),
        scratch_shapes=[pltpu.VMEM((N_DEV, rows, width), F32), pltpu.SemaphoreType.DMA((7,)),
                        pltpu.SemaphoreType.DMA((7,)), pltpu.SemaphoreType.DMA],
    )(packed)


def _mixer_forward_branches(x0, h, w, s, dep=None, before_conv=None):
    proj = _matmul(h, w["w_in_t"], mode="nt", out_dtype=BF16, tm=1024, tn=1664, tk=1024, name="proj_fwd",
                   dep=dep)
    oa, states = _gla_fwd(proj, w["w_fg2"], w["b_fg"], w["gla_norm_g"], s=s)
    conv_wb = w["conv_wb"]
    if before_conv is not None:
        conv_wb = conv_wb + before_conv(oa)[0, 0]
    cb_in = _conv_fwd(proj, conv_wb, s=s)
    return dict(x0=x0, h=h, proj=proj, oa=oa, states=states, cb_in=cb_in)


def _mixer_forward_out(sv, w, s, dep=None):
    ya, yb, mix, x1 = _mixer_out_fwd(sv["oa"], sv["cb_in"], sv["proj"], sv["x0"], w["w_oa"], w["w_ob"], w["w_o"],
                                     s=s, dep=dep)
    return x1, dict(sv, ya=ya, yb=yb, mix=mix)


def _ffn_forward_hidden(x1, w, s, dep=None):
    tr = min(256, s)
    d, f = D_MODEL, FFN_HIDDEN
    n_dep = dep is not None

    def body(*refs):
        x_ref, g_ref, wg_ref, wu_ref, h2_ref, gt_ref, up_ref, hid_ref = refs[n_dep:]
        xv = x_ref[...]
        rs = lax.rsqrt(jnp.mean(xv * xv, axis=-1, keepdims=True) + NORM_EPS)
        h2 = (xv * rs * g_ref[...]).astype(BF16)
        h2_ref[...] = h2
        gt = lax.dot_general(h2, wg_ref[...], _DOT_DIMS["nt"], preferred_element_type=F32).astype(BF16)
        up = lax.dot_general(h2, wu_ref[...], _DOT_DIMS["nt"], preferred_element_type=F32).astype(BF16)
        gt_ref[...] = gt
        up_ref[...] = up
        gv = gt.astype(F32)
        hid_ref[...] = (gv * _sigmoid(gv) * up.astype(F32)).astype(BF16)

    row_d = pl.BlockSpec((tr, d), lambda i: (i, 0))
    row_f = pl.BlockSpec((tr, f), lambda i: (i, 0))
    weight = pl.BlockSpec((f, d), lambda i: (0, 0))
    tok, tok_specs = _token_spec(dep)
    h2, gt, up, hid = pl.pallas_call(
        body, name="ffn_hidden_fwd",
        out_shape=(jax.ShapeDtypeStruct((s, d), BF16),) + tuple(jax.ShapeDtypeStruct((s, f), BF16) for _ in range(3)),
        grid=(s // tr,),
        in_specs=tok_specs + [row_d, pl.BlockSpec((1, d), lambda i: (0, 0)), weight, weight],
        out_specs=(row_d, row_f, row_f, row_f),
        compiler_params=_cparams(("parallel",)),
    )(*tok, x1, w["norm2_g"].reshape(1, d), w["w_gate_t"], w["w_up_t"])
    return dict(x1=x1, h2=h2, gt=gt, up=up, hid=hid)


def _ffn_forward_out(sv, w, dep=None):
    return _matmul(sv["hid"], w["w_ffn_down"], mode="nn", out_dtype=F32, tm=1024, tn=1024, tk=FFN_HIDDEN,
                   name="ffn_out_fwd", add=sv["x1"], dep=dep)


def _ffn_backward_grads(dx2b, w, sv, s):
    g = {}
    dgt, dup = _ffn_out_bwd(dx2b, w["w_ffn_down"], sv["gt"], sv["up"], s=s)
    g["w_ffn_down"] = _matmul(sv["hid"], dx2b, mode="tn", out_dtype=BF16, tm=1408, tn=1024, tk=1024,
                              name="ffn_out_wgrad")
    g["w_gate_t"] = _matmul(dgt, sv["h2"], mode="tn", out_dtype=BF16, tm=1408, tn=1024, tk=1024,
                            name="ffn_gate_wgrad")
    g["w_up_t"] = _matmul(dup, sv["h2"], mode="tn", out_dtype=BF16, tm=1408, tn=1024, tk=1024,
                          name="ffn_up_wgrad")
    return g, dgt, dup


def _ffn_backward_input(dgt, dup, dx2, w, sv, dep=None):
    return _matmul_norm_bwd(dgt, w["w_gate_t"], sv["x1"], w["norm2_g"], dx2, tm=256, tk=FFN_HIDDEN,
                            name="ffn_in_bwd_norm2_bwd", second=(dup, w["w_up_t"]), dep=dep)


def _mixer_backward_branches(dx1b, w, sv, s, dep=None):
    g = {}
    dya, dyb, dcb, doa, dproj = _mixer_out_bwd(dx1b, sv["proj"], sv["ya"], sv["yb"], w["w_oa"], w["w_ob"],
                                               w["w_o"], s=s, dep=dep)
    g["w_o"] = _matmul(sv["mix"], dx1b, mode="tn", out_dtype=BF16, tm=1024, tn=1024, tk=2048, name="wo_wgrad")
    g["w_ob"] = _matmul(sv["cb_in"], dyb, mode="tn", out_dtype=BF16, tm=1024, tn=1024, tk=2048, name="yb_wgrad")
    g["w_oa"] = _matmul(sv["oa"], dya, mode="tn", out_dtype=BF16, tm=1024, tn=1024, tk=2048, name="ya_wgrad")
    dproj, g["conv_wb"] = _conv_bwd(sv["proj"], dcb, w["conv_wb"], dproj, s=s)
    dproj, dfz, g["gla_norm_g"], g["w_fg2"], g["b_fg"] = _gla_bwd(
        sv["proj"], doa, sv["states"], w["w_fg2"], w["b_fg"], w["gla_norm_g"], dproj, s=s)
    return g, _put_fz(dproj, dfz, s=s)


def _proj_wgrad(dproj, sv, dep=None):
    return _matmul(dproj, sv["h"], mode="tn", out_dtype=BF16, tm=1664, tn=1024, tk=1024, name="proj_wgrad",
                   dep=dep)


def _proj_bwd(dproj, w, x0, dres, dep=None):
    return _matmul_norm_bwd(dproj, w["w_in_t"], x0, w["norm1_g"], dres, tm=512, tk=1664,
                            name="proj_bwd_norm1_bwd", dep=dep)


def _cols_from_chips(t):
    return jnp.transpose(t, (1, 0, 2)).reshape(t.shape[1], -1)


W_IN_ROWS = IN_WIDTH // N_CHIPS
W_IN_ROWS_PAD = -(-W_IN_ROWS // BF16_SUBLANES) * BF16_SUBLANES


def _w_in_t_shard(t):
    return jnp.pad(jnp.transpose(t, (0, 2, 1)), ((0, 0), (0, W_IN_ROWS_PAD - W_IN_ROWS), (0, 0)))


def _w_in_t_unshard(t):
    return jnp.transpose(t[:, :W_IN_ROWS], (0, 2, 1))


def _w_in_row_segments():
    runs = [(0, 0, FZ_ORIG), (FZ_ORIG, FZ_ORIG + GATE_RANK, COL_FZ - FZ_ORIG), (COL_FZ, FZ_ORIG, GATE_RANK)]
    out = []
    for kernel_row, ref_row, length in runs:
        while length:
            chip, local = divmod(ref_row, W_IN_ROWS)
            n = min(length, W_IN_ROWS - local)
            out.append((kernel_row, chip * W_IN_ROWS_PAD + local, n))
            kernel_row, ref_row, length = kernel_row + n, ref_row + n, length - n
    return out


def _permute_rows(src, n_out, segments, *, name):
    n_src, width = src.shape
    block = _row_tile(n_out)
    window = block + BF16_SUBLANES
    assert n_src >= window and src.dtype == BF16

    def body(src_ref, out_ref):
        rows = lax.broadcasted_iota(jnp.int32, (block, window), 0)
        cols = lax.broadcasted_iota(jnp.int32, (block, window), 1)
        for b0 in range(0, n_out, block):
            acc = None
            for o, s, n in segments:
                lo, hi = max(o, b0), min(o + n, b0 + block)
                if lo >= hi:
                    continue
                first = s + (lo - o)
                base = max(0, min(first // BF16_SUBLANES * BF16_SUBLANES, n_src - window))
                shift = (first - base) - (lo - b0)
                pick = (cols == rows + shift) & (rows >= lo - b0) & (rows < hi - b0)
                part = jnp.dot(jnp.where(pick, 1.0, 0.0).astype(BF16), src_ref[base:base + window, :],
                               preferred_element_type=F32)
                acc = part if acc is None else acc + part
            out_ref[b0:b0 + block, :] = (jnp.zeros((block, width), F32) if acc is None else acc).astype(BF16)

    return pl.pallas_call(
        body, name=name,
        out_shape=jax.ShapeDtypeStruct((n_out, width), BF16),
        in_specs=[pl.BlockSpec(memory_space=pltpu.VMEM)],
        out_specs=pl.BlockSpec(memory_space=pltpu.VMEM),
        compiler_params=_cparams(),
    )(src)


def _w_in_t_to_kernel(t):
    return _permute_rows(t.reshape(N_CHIPS * W_IN_ROWS_PAD, D_MODEL), PROJ_W, _w_in_row_segments(),
                         name="w_in_to_kernel_rows")


def _w_in_t_from_kernel(g):
    segments = [(slab, kernel_row, n) for kernel_row, slab, n in _w_in_row_segments()]
    out = _permute_rows(g, N_CHIPS * W_IN_ROWS_PAD, segments, name="w_in_to_chip_rows")
    return out.reshape(N_CHIPS, W_IN_ROWS_PAD, D_MODEL)


def kernel(x, norm1_g, w_in, w_fg2, b_fg, gla_norm_g, w_oa, conv_w, conv_b, w_ob, w_o, norm2_g, w_ffn_gate, w_ffn_up, w_ffn_down, final_g, loss_target, m_norm1_g, m_w_in, m_w_fg2, m_b_fg, m_gla_norm_g, m_w_oa, m_conv_w, m_conv_b, m_w_ob, m_w_o, m_norm2_g, m_w_ffn_gate, m_w_ffn_up, m_w_ffn_down, m_final_g, v_norm1_g, v_w_in, v_w_fg2, v_b_fg, v_gla_norm_g, v_w_oa, v_conv_w, v_conv_b, v_w_ob, v_w_o, v_norm2_g, v_w_ffn_gate, v_w_ffn_up, v_w_ffn_down, v_final_g):
    cx_ = lax.axis_index("x")
    cy_ = lax.axis_index("y")
    cc_ = lax.axis_index("c")
    me = 2 * cx_ + cy_
    on_core0 = jnp.where(cc_ == 0, 1.0, 0.0).astype(F32).reshape(1, 1)

    def swap(t):
        return jnp.swapaxes(t, 1, 2)

    big_names = ["w_in", "w_oa", "w_ob", "w_o", "w_ffn_gate", "w_ffn_up", "w_ffn_down"]
    views = dict(
        w_in=tuple(_w_in_t_shard(t) for t in (w_in, m_w_in, v_w_in)),
        w_oa=(w_oa, m_w_oa, v_w_oa), w_ob=(w_ob, m_w_ob, v_w_ob), w_o=(w_o, m_w_o, v_w_o),
        w_ffn_gate=tuple(swap(t) for t in (w_ffn_gate, m_w_ffn_gate, v_w_ffn_gate)),
        w_ffn_up=tuple(swap(t) for t in (w_ffn_up, m_w_ffn_up, v_w_ffn_up)),
        w_ffn_down=(w_ffn_down, m_w_ffn_down, v_w_ffn_down))
    from_view = dict(w_in=_w_in_t_unshard, w_ffn_gate=swap, w_ffn_up=swap)

    s = x.shape[1]

    proj_names = ["w_in"]
    first_names = ["w_in", "w_fg2", "conv_w"]
    rest_names = ["w_oa", "w_ob", "w_o", "w_ffn_gate", "w_ffn_up", "w_ffn_down"]
    mixer_names = ["w_oa", "w_ob", "w_o"]
    ffn_names = ["w_ffn_gate", "w_ffn_up", "w_ffn_down"]
    weight_key = dict(w_in="w_in_t", w_ffn_gate="w_gate_t", w_ffn_up="w_up_t")
    full_shape = dict(w_oa=(VAL_WIDTH, D_MODEL), w_ob=(CONV_CH, D_MODEL), w_o=(D_MODEL, D_MODEL),
                      w_ffn_gate=(FFN_HIDDEN, D_MODEL), w_ffn_up=(FFN_HIDDEN, D_MODEL),
                      w_ffn_down=(FFN_HIDDEN, D_MODEL))

    conv_w_p = jnp.pad(conv_w, ((0, 0), (0, 8 - conv_w.shape[1]), (0, 0)))
    small_shards = dict(w_fg2=w_fg2, conv_w=conv_w_p)

    def shards_of(l, names):
        return [small_shards[n][l] if n in small_shards else views[n][0][l].astype(BF16) for n in names]

    def small_weights(l):
        return dict(norm1_g=norm1_g[l], norm2_g=norm2_g[l], b_fg=b_fg[l].reshape(1, KEY_WIDTH),
                    gla_norm_g=gla_norm_g[l].reshape(1, HEAD_V))

    def full_weights(l, names, gathered, shards):
        out = {}
        for n, g, t in zip(names, gathered, shards):
            g = lax.dynamic_update_index_in_dim(g, t[None], me, axis=0)
            if n == "w_in":
                out["w_in_t"] = _w_in_t_to_kernel(g)
            elif n == "w_fg2":
                out[n] = jnp.pad(_cols_from_chips(g), ((0, FZ_PAD - GATE_RANK), (0, 0))).astype(BF16)
            elif n == "conv_w":
                out["conv_wb"] = (jnp.pad(_cols_from_chips(g)[:3], ((0, 5), (0, 0)))
                                  + jnp.pad(conv_b[l].reshape(1, CONV_CH), ((3, 4), (0, 0))))
            else:
                out[weight_key.get(n, n)] = g.reshape(full_shape[n])
        return out

    def per_chip(g, names):
        out = []
        for n in names:
            t = g[weight_key.get(n, n)]
            out.append(_w_in_t_from_kernel(t) if n == "w_in"
                       else t.reshape(N_CHIPS, t.shape[0] // N_CHIPS, t.shape[1]))
        return out

    def gather(l, names, tag, after=None):
        shards = shards_of(l, names)
        return shards, _gather_start(l, shards, name=f"gather_{tag}_start", after=after)

    def gathered_to_sibling(l, started, after, tag):
        return _handoff_start(l, _gather_wait(l, started, after, name=f"gather_{tag}_wait"),
                              name=f"handoff_{tag}_start")

    def feed(l, g, names, tag):
        return _sibling_push_start(l, False, per_chip(g, names), name=f"feed_{tag}_start")

    def reduces(l):
        return jnp.where(cc_ == l, 1, 0).astype(jnp.int32).reshape(1)

    def pair_and_exchange(l, fed, after, names, tag):
        mine, sib = _sibling_push_wait(l, False, fed, after, name=f"feed_{tag}_wait")
        parts = [_pair_sum(a, b, reduces(l), name=f"pair_sum_{tag}_{n}") for n, a, b in zip(names, mine, sib)]
        return _chip_exchange_start(l, parts, name=f"exchange_{tag}_start")

    def total_and_share(l, swapped, after, names, tag):
        parts, recvs = _chip_exchange_wait(l, swapped, after, name=f"exchange_{tag}_wait")
        owns = [lax.dynamic_index_in_dim(p, me, 0, keepdims=False) for p in parts]
        totals = [_total_sum(o, rc, reduces(l), name=f"total_{tag}_{n}") for n, o, rc in zip(names, owns, recvs)]
        return _sibling_push_start(l, True, totals, name=f"share_{tag}_start")

    def shared(l, sharing, after, names, tag):
        totals, others = _sibling_push_wait(l, True, sharing, after, name=f"share_{tag}_wait")
        return {n: (t, o) for n, t, o in zip(names, totals, others)}

    def branches_under_gather(l, xin, h, w, tag, shards_r, started_r):
        box = {}

        def before_conv(oa):
            box["handoff"] = gathered_to_sibling(l, started_r, oa, tag)
            return box["handoff"][3]

        sv = _mixer_forward_branches(xin, h, w, s, dep=started_r[3], before_conv=before_conv)
        rest = _handoff_wait(l, box["handoff"], sv["cb_in"], name=f"handoff_{tag}_wait")
        w.update(full_weights(l, rest_names, rest, shards_r))
        return sv

    shards_p, started = gather(0, first_names, "p0")
    h0 = _rms_fwd(x[0], norm1_g[0], name="rms1_fwd")
    later_shards = shards_of(0, rest_names) + shards_of(1, first_names) + shards_of(1, rest_names)
    started = gathered_to_sibling(0, started, (h0, views["w_in"][1], *later_shards), "p0")
    w0 = small_weights(0)
    w0.update(full_weights(0, first_names, _handoff_wait(0, started, views["w_in"][2], name="handoff_p0_wait"),
                           shards_p))
    sv0 = branches_under_gather(0, x[0], h0, w0, "r0", *gather(0, rest_names, "r0", after=w0["w_in_t"]))

    shards_p1, started = gather(1, first_names, "p1", after=w0["w_o"])
    x1, sv0m = _mixer_forward_out(sv0, w0, s, dep=started[3])
    sv0f = _ffn_forward_hidden(x1, w0, s)
    handoff_1 = gathered_to_sibling(1, started, sv0f["hid"], "p1")
    rest_1 = gather(1, rest_names, "r1", after=handoff_1[2][0])
    x2 = _ffn_forward_out(sv0f, w0, dep=handoff_1[3] + rest_1[1][3])
    w1 = small_weights(1)
    w1.update(full_weights(1, first_names, _handoff_wait(1, handoff_1, x2, name="handoff_p1_wait"), shards_p1))

    h1 = _rms_fwd(x2, norm1_g[1], name="rms1_fwd")
    x3, sv1m = _mixer_forward_out(branches_under_gather(1, x2, h1, w1, "r1", *rest_1), w1, s)
    sv1f = _ffn_forward_hidden(x3, w1, s)
    x4 = _ffn_forward_out(sv1f, w1)
    loss_local, dx, dxb, dgf = _loss_head(x4, final_g, loss_target[0])
    loss = lax.psum(loss_local[0, 0], ("x", "y", "c"))

    g1, dgt, dup = _ffn_backward_grads(dxb, w1, sv1f, s)
    dx_mid, dxb_mid, g1["norm2_g"] = _ffn_backward_input(dgt, dup, dx, w1, sv1f)
    gm, dproj = _mixer_backward_branches(dxb_mid, w1, sv1m, s)
    g1.update(gm)
    g1["w_in_t"] = _proj_wgrad(dproj, sv1m)
    fed_1 = feed(1, g1, big_names, "l1")
    dx, dxb, g1["norm1_g"] = _proj_bwd(dproj, w1, sv1m["x0"], dx_mid, dep=fed_1[3])

    g0, dgt, dup = _ffn_backward_grads(dxb, w0, sv0f, s)
    swap_1 = pair_and_exchange(1, fed_1, dgt, big_names, "l1")
    fed_f = feed(0, g0, ffn_names, "f0")
    dx_mid, dxb_mid, g0["norm2_g"] = _ffn_backward_input(dgt, dup, dx, w0, sv0f, dep=swap_1[3] + fed_f[3])
    swap_f = pair_and_exchange(0, fed_f, dx_mid, ffn_names, "f0")
    gm, dproj = _mixer_backward_branches(dxb_mid, w0, sv0m, s, dep=swap_f[3])
    g0.update(gm)
    share_1 = total_and_share(1, swap_1, dproj, big_names, "l1")
    share_f = total_and_share(0, swap_f, dproj, ffn_names, "f0")
    fed_m = feed(0, g0, mixer_names, "m0")
    g0["w_in_t"] = _proj_wgrad(dproj, sv0m, dep=share_1[3] + share_f[3] + fed_m[3])
    fed_p = feed(0, g0, proj_names, "p0")
    swap_m = pair_and_exchange(0, fed_m, g0["w_in_t"], mixer_names, "m0")
    grad_x, _, g0["norm1_g"] = _proj_bwd(dproj, w0, sv0m["x0"], dx_mid, dep=fed_p[3] + swap_m[3])
    reduced1 = shared(1, share_1, grad_x, big_names, "l1")
    reduced0 = shared(0, share_f, grad_x, ffn_names, "f0")

    swap_p = pair_and_exchange(0, fed_p, grad_x, proj_names, "p0")
    share_m = total_and_share(0, swap_m, swap_p[3], mixer_names, "m0")
    reduced0.update(shared(0, share_m, share_m[3], mixer_names, "m0"))
    grads = [g0, g1]

    def small_rows(t):
        return t.reshape(-1, D_MODEL)

    def tile_rows(t):
        return jnp.pad(t, ((0, -t.shape[0] % F32_SUBLANES), (0, 0)))

    g0, g1 = grads
    pieces = [
        jnp.concatenate([g0["norm1_g"], g1["norm1_g"]], axis=0),
        jnp.concatenate([g0["norm2_g"], g1["norm2_g"]], axis=0),
        dgf,
        small_rows(jnp.concatenate([g0["b_fg"], g1["b_fg"]], axis=1)),
        small_rows(jnp.concatenate([g0["gla_norm_g"], g1["gla_norm_g"],
                                    jnp.zeros((1, D_MODEL - 2 * HEAD_V), F32)], axis=1)),
        jnp.concatenate([g0["conv_wb"][3:4], g1["conv_wb"][3:4]], axis=0),
        jnp.concatenate([g0["conv_wb"][:3], g1["conv_wb"][:3]], axis=0),
        small_rows(jnp.stack([g0["w_fg2"][:GATE_RANK], g1["w_fg2"][:GATE_RANK]])),
    ]
    small = _all_reduce_small(jnp.concatenate([tile_rows(p) for p in pieces], axis=0))
    sg = dict(
        norm1_g=small[0:2], norm2_g=small[8:10], final_g=small[16],
        b_fg=small[24].reshape(DEPTH, KEY_WIDTH), gla_norm_g=small[32, :DEPTH * HEAD_V].reshape(DEPTH, HEAD_V),
        conv_b=small[40:42],
        conv_w=lax.dynamic_slice_in_dim(small[48:54].reshape(DEPTH, 3, CONV_CH), me * (CONV_CH // N_CHIPS),
                                        CONV_CH // N_CHIPS, axis=2),
        w_fg2=lax.dynamic_slice_in_dim(small[56:72].reshape(DEPTH, GATE_RANK, KEY_WIDTH),
                                       me * (KEY_WIDTH // N_CHIPS), KEY_WIDTH // N_CHIPS, axis=2),
    )

    small_params = dict(norm1_g=(norm1_g, m_norm1_g, v_norm1_g), w_fg2=(w_fg2, m_w_fg2, v_w_fg2),
                        b_fg=(b_fg, m_b_fg, v_b_fg), gla_norm_g=(gla_norm_g, m_gla_norm_g, v_gla_norm_g),
                        conv_w=(conv_w, m_conv_w, v_conv_w), conv_b=(conv_b, m_conv_b, v_conv_b),
                        norm2_g=(norm2_g, m_norm2_g, v_norm2_g), final_g=(final_g, m_final_g, v_final_g))
    order = ["norm1_g", "w_in", "w_fg2", "b_fg", "gla_norm_g", "w_oa", "conv_w", "conv_b", "w_ob", "w_o",
             "norm2_g", "w_ffn_gate", "w_ffn_up", "w_ffn_down", "final_g"]
    results = {}

    def update_large(n):
        w_, m_, v_ = views[n]
        outs = _adamw_layers(w_, (reduced0[n][0], reduced1[n][0]), (reduced0[n][1], reduced1[n][1]), on_core0,
                             m_, v_, name="adamw_" + n)
        back = from_view.get(n)
        results[n] = tuple(back(r) for r in outs) if back else outs
        return outs[1]

    for n, (w_, m_, v_) in small_params.items():
        results[n] = (sg[n],) + _adamw(w_, sg[n], m_, v_, name="adamw_" + n)
    updated = tuple(update_large(n) for n in rest_names)
    share_p = total_and_share(0, swap_p, updated, proj_names, "p0")
    reduced0.update(shared(0, share_p, share_p[3], proj_names, "p0"))
    update_large("w_in")
    return (loss, grad_x[None], *[results[n][0] for n in order], *[results[n][1] for n in order],
            *[results[n][2] for n in order], *[results[n][3] for n in order])
```
